```python
import math
import jax
import jax.numpy as jnp
from jax import lax
import numpy as np

D_MODEL = 1024
BATCH = 2
SEQ = 16384
DEPTH = 4

GRID_W = 64
CTX_LEN = 256
MIX_W = D_MODEL
GROUP_W = MIX_W // 4
D_FF = 4 * D_MODEL
N_ADA = 6
EPS = 1e-6
GN_EPS = 64e-5
CHUNK = 64

S5_P = 16
S5_G = GROUP_W // S5_P
S5_N = 64

RW_HEAD = 64
RW_HEADS = GROUP_W // RW_HEAD
RW_W_RANK = 32
RW_A_RANK = 32
RW_G_RANK = 64
RW_COLS = 3 * GROUP_W + RW_W_RANK + RW_A_RANK + RW_G_RANK
RW_SPLITS = (GROUP_W, 2 * GROUP_W, 3 * GROUP_W, 3 * GROUP_W + RW_W_RANK, 3 * GROUP_W + RW_W_RANK + RW_A_RANK)

RET_HEADS = 4
RET_DK = GROUP_W // RET_HEADS
RET_DV = GROUP_W // RET_HEADS
ROPE_BASE = 10000.0

GLA_HEADS = 4
GLA_DK = 32
GLA_DV = GROUP_W // GLA_HEADS
GLA_QK = GLA_HEADS * GLA_DK
GLA_RANK = 16
GLA_TAU = 16.0
GLA_COLS = 2 * GLA_QK + 2 * GROUP_W + GLA_RANK
GLA_SPLITS = (GLA_QK, 2 * GLA_QK, 2 * GLA_QK + GROUP_W, 2 * GLA_QK + 2 * GROUP_W)

IN_TOTAL = GROUP_W + RW_COLS + 4 * GROUP_W + GLA_COLS
IN_SPLITS = (GROUP_W, GROUP_W + RW_COLS, GROUP_W + RW_COLS + 4 * GROUP_W)

kernel_name = 'hybrid_s5_rwkv7_retnet_gla_dit'


def _rmsnorm(x, g):
    xf = x.astype(jnp.float32)
    y = xf * lax.rsqrt(jnp.mean(xf * xf, axis=-1, keepdims=True) + EPS)
    return (y * g.astype(jnp.float32)).astype(x.dtype)


def _head_rmsnorm(y, g):
    yf = y.astype(jnp.float32)
    yn = yf * lax.rsqrt(jnp.mean(yf * yf, axis=-1, keepdims=True) + EPS)
    b_, l_, h_, dh = y.shape
    return yn.reshape(b_, l_, h_ * dh) * g.astype(jnp.float32)


def _head_groupnorm(y, g, b):
    yf = y.astype(jnp.float32)
    mu = jnp.mean(yf, axis=-1, keepdims=True)
    var = jnp.mean(jnp.square(yf - mu), axis=-1, keepdims=True)
    yn = (yf - mu) * lax.rsqrt(var + GN_EPS)
    b_, l_, h_, dh = y.shape
    return yn.reshape(b_, l_, h_ * dh) * g.astype(jnp.float32) + b.astype(jnp.float32)


def _gated_head_norm(o, g, ln_g):
    return (_head_rmsnorm(o, ln_g) * jax.nn.silu(g.astype(jnp.float32))).astype(g.dtype)


def _shift_seq(z):
    a, b = jnp.split(z, 2, axis=-1)
    a = jnp.pad(a, ((0, 0), (1, 0), (0, 0)))[:, :-1]
    b = jnp.pad(b, ((0, 0), (0, 1), (0, 0)))[:, 1:]
    return jnp.concatenate([a, b], axis=-1)


def _shift_grid(z, rows):
    b_, l_, c_ = z.shape
    g4 = z.reshape(b_, rows, GRID_W, c_)
    a, b, c, d = jnp.split(g4, 4, axis=-1)
    a = jnp.pad(a, ((0, 0), (0, 0), (1, 0), (0, 0)))[:, :, :-1]
    b = jnp.pad(b, ((0, 0), (0, 0), (0, 1), (0, 0)))[:, :, 1:]
    c = jnp.pad(c, ((0, 0), (1, 0), (0, 0), (0, 0)))[:, :-1]
    d = jnp.pad(d, ((0, 0), (0, 1), (0, 0), (0, 0)))[:, 1:]
    return jnp.concatenate([a, b, c, d], axis=-1).reshape(b_, l_, c_)


def _axial_rope(x, rows_idx, cols_idx):
    half = x.shape[-1] // 2
    nf = half // 2
    inv = ROPE_BASE ** (-jnp.arange(nf, dtype=jnp.float32) / nf)

    def rot(xh, pos):
        ang = pos.astype(jnp.float32)[:, None] * inv[None, :]
        cos = jnp.cos(ang)[None, :, None, :]
        sin = jnp.sin(ang)[None, :, None, :]
        x1 = xh[..., :nf].astype(jnp.float32)
        x2 = xh[..., nf:].astype(jnp.float32)
        return jnp.concatenate([x1 * cos - x2 * sin, x1 * sin + x2 * cos], axis=-1)

    out = jnp.concatenate([rot(x[..., :half], rows_idx), rot(x[..., half:], cols_idx)], axis=-1)
    return out.astype(x.dtype)


def _ssm_combine(e1, e2):
    a1, b1 = e1
    a2, b2 = e2
    return a2 * a1, a2 * b1 + b2


def _s5(u_ctx, u_lat, lam_re, lam_im, log_dt, b_re, b_im, c_re, c_im, d_skip, glu_w, glu_b):
    f32 = jnp.float32
    lam = lax.complex(jnp.minimum(lam_re.astype(f32), -1e-4), lam_im.astype(f32))
    dt = jnp.exp(log_dt.astype(f32))[..., None]
    a_bar = jnp.exp(lam * dt)
    b_bar = ((a_bar - 1.0) / lam)[..., None] * lax.complex(b_re.astype(f32), b_im.astype(f32))
    c_mat = lax.complex(c_re.astype(f32), c_im.astype(f32))

    def run(u, d, h0, reverse):
        b_, l_, _ = u.shape
        ug = u.astype(f32).reshape(b_, l_, S5_G, S5_P)
        if reverse:
            ug = jnp.flip(ug, axis=1)
        bu = jnp.einsum('gnp,blgp->blgn', b_bar[d], ug)
        if h0 is not None:
            bu = bu.at[:, 0].add(a_bar[d] * h0)
        a = jnp.broadcast_to(a_bar[d], (1, l_, S5_G, S5_N))
        _, h = lax.associative_scan(_ssm_combine, (a, bu), axis=1)
        if reverse:
            h = jnp.flip(h, axis=1)
        return h

    def readout(h, d):
        return jnp.einsum('gpn,blgn->blgp', c_mat[d], h).real

    def out(u, hf, hb):
        b_, l_, _ = u.shape
        y = (readout(hf, 0) + readout(hb, 1)).reshape(b_, l_, GROUP_W) + d_skip.astype(f32) * u.astype(f32)
        y = jax.nn.gelu(y)
        y = y * jax.nn.sigmoid(y @ glu_w.astype(f32) + glu_b.astype(f32))
        return y.astype(u.dtype)

    hcf = run(u_ctx, 0, None, False)
    hcb = run(u_ctx, 1, None, True)
    hlf = run(u_lat, 0, hcf[:, -1], False)
    hlb = run(u_lat, 1, hcb[:, 0], True)
    return out(u_ctx, hcf, hcb), out(u_lat, hlf, hlb)


def _rwkv_scan(r, w, k, v, a, b, s0, reverse):
    def tm(t):
        return jnp.moveaxis(t.astype(jnp.float32), 1, 0)

    def step(s, inp):
        r_t, w_t, k_t, v_t, a_t, b_t = inp
        sa = jnp.einsum('bhvk,bhk->bhv', s, a_t)
        s = s * w_t[:, :, None, :] + sa[..., None] * b_t[:, :, None, :] + v_t[..., None] * k_t[:, :, None, :]
        return s, jnp.einsum('bhvk,bhk->bhv', s, r_t)

    xs = (tm(r), tm(w), tm(k), tm(v), tm(a), tm(b))
    s_fin, y = lax.scan(step, s0, xs, reverse=reverse)
    return jnp.moveaxis(y, 0, 1), s_fin


def _rwkv7(z_ctx, z_lat, rows, mu, w0, w_up, a0, a_up, g_up, k_k, k_a, r_k, ln_g, ln_b):
    f32 = jnp.float32

    def prep(z, shifted):
        b_, l_, _ = z.shape

        def hd(t):
            return t.astype(f32).reshape(b_, l_, RW_HEADS, RW_HEAD)

        zm = z + mu * (shifted - z)
        r, k, v, wlo, alo, glo = jnp.split(zm, RW_SPLITS, axis=-1)
        g = jax.nn.sigmoid(glo) @ g_up
        kk = hd(k * k_k)
        kk = kk * lax.rsqrt(jnp.sum(kk * kk, axis=-1, keepdims=True) + 1e-12)
        dirs = []
        for d in range(2):
            w_raw = -jax.nn.softplus(-(w0[d] + jnp.tanh(wlo) @ w_up[d])) - 0.5
            decay = jnp.exp(-jnp.exp(w_raw.astype(f32)))
            a = jax.nn.sigmoid((a0[d] + alo @ a_up[d]).astype(f32))
            k_d = k.astype(f32) * (1.0 + (a - 1.0) * k_a.astype(f32))
            dirs.append((hd(decay), hd(k_d), -kk, kk * hd(a)))
        return hd(r), hd(k), hd(v), g, dirs

    def finish(r, k, v, g, y):
        b_, l_ = r.shape[:2]
        bonus = jnp.sum(r * k * r_k.astype(f32), axis=-1, keepdims=True) * v
        o = _head_groupnorm(y, ln_g, ln_b) + bonus.reshape(b_, l_, GROUP_W)
        return (o * g.astype(f32)).astype(g.dtype)

    rc, kc, vc, gc, dc = prep(z_ctx, _shift_seq(z_ctx))
    rl, kl, vl, gl, dl = prep(z_lat, _shift_grid(z_lat, rows))
    s0 = jnp.zeros((z_lat.shape[0], RW_HEADS, RW_HEAD, RW_HEAD), f32)
    yc_f, sc_f = _rwkv_scan(rc, dc[0][0], dc[0][1], vc, dc[0][2], dc[0][3], s0, False)
    yc_b, sc_b = _rwkv_scan(rc, dc[1][0], dc[1][1], vc, dc[1][2], dc[1][3], s0, True)
    yl_f, _ = _rwkv_scan(rl, dl[0][0], dl[0][1], vl, dl[0][2], dl[0][3], sc_f, False)
    yl_b, _ = _rwkv_scan(rl, dl[1][0], dl[1][1], vl, dl[1][2], dl[1][3], sc_b, True)
    return finish(rc, kc, vc, gc, yc_f + yc_b), finish(rl, kl, vl, gl, yl_f + yl_b)


def _chunked_gla(q, k, v, log_a, s0, reverse):
    if reverse:
        q, k, v, log_a = jnp.flip(q, 1), jnp.flip(k, 1), jnp.flip(v, 1), jnp.flip(log_a, 1)
    b_, l_, h_, _ = q.shape
    dv = v.shape[-1]
    nc = l_ // CHUNK

    def chunks(t):
        return t.astype(jnp.float32).reshape(b_, nc, CHUNK, h_, t.shape[-1]).transpose(1, 0, 3, 2, 4)

    tril = jnp.tril(jnp.ones((CHUNK, CHUNK), dtype=bool))[:, :, None]

    def body(s, inp):
        qc, kc, vc, gc = inp
        bcum = jnp.cumsum(gc, axis=2)
        o_inter = jnp.einsum('bhck,bhkv->bhcv', qc * jnp.exp(bcum), s)
        diff = bcum[:, :, :, None, :] - bcum[:, :, None, :, :]
        dec = jnp.where(tril, jnp.exp(jnp.where(tril, diff, 0.0)), 0.0)
        scores = jnp.einsum('bhtk,bhsk,bhtsk->bhts', qc, kc, dec)
        o = o_inter + jnp.einsum('bhts,bhsv->bhtv', scores, vc)
        b_last = bcum[:, :, -1:, :]
        s = jnp.exp(b_last[:, :, 0, :])[..., None] * s + jnp.einsum('bhsk,bhsv->bhkv', kc * jnp.exp(b_last - bcum), vc)
        return s, o

    s_fin, o = lax.scan(body, s0, (chunks(q), chunks(k), chunks(v), chunks(log_a)))
    o = o.transpose(1, 0, 3, 2, 4).reshape(b_, l_, h_, dv)
    if reverse:
        o = jnp.flip(o, 1)
    return o, s_fin


def _retention(z_ctx, z_lat, rows_idx, cols_idx, decay_logit, ln_g):
    f32 = jnp.float32
    log_gamma = jax.nn.log_sigmoid(decay_logit.astype(f32))

    def prep(z):
        b_, l_, _ = z.shape
        q, k, v, g = jnp.split(z, 4, axis=-1)
        sh = (b_, l_, RET_HEADS, RET_DK)
        return q.reshape(sh), k.reshape(sh) * RET_DK ** -0.5, v.reshape(b_, l_, RET_HEADS, RET_DV), g

    def la(t, d):
        return jnp.broadcast_to(log_gamma[d][None, None, :, None], t.shape[:3] + (RET_DK,))

    qc, kc, vc, gc = prep(z_ctx)
    ql, kl, vl, gl = prep(z_lat)
    ql = _axial_rope(ql, rows_idx, cols_idx)
    kl = _axial_rope(kl, rows_idx, cols_idx)
    s0 = jnp.zeros((z_lat.shape[0], RET_HEADS, RET_DK, RET_DV), f32)
    oc_f, sf = _chunked_gla(qc, kc, vc, la(qc, 0), s0, False)
    oc_b, sb = _chunked_gla(qc, kc, vc, la(qc, 1), s0, True)
    ol_f, _ = _chunked_gla(ql, kl, vl, la(ql, 0), sf, False)
    ol_b, _ = _chunked_gla(ql, kl, vl, la(ql, 1), sb, True)
    return _gated_head_norm(oc_f + oc_b, gc, ln_g), _gated_head_norm(ol_f + ol_b, gl, ln_g)


def _gla(z_ctx, z_lat, a_up, a_b, ln_g):
    f32 = jnp.float32

    def prep(z):
        b_, l_, _ = z.shape
        q, k, v, g, alo = jnp.split(z, GLA_SPLITS, axis=-1)
        sh = (b_, l_, GLA_HEADS, GLA_DK)
        log_a = [(jax.nn.log_sigmoid((alo @ a_up[d] + a_b[d]).astype(f32)) / GLA_TAU).reshape(sh) for d in range(2)]
        return q.reshape(sh), k.reshape(sh) * GLA_DK ** -0.5, v.reshape(b_, l_, GLA_HEADS, GLA_DV), g, log_a

    qc, kc, vc, gc, lac = prep(z_ctx)
    ql, kl, vl, gl, lal = prep(z_lat)
    s0 = jnp.zeros((z_lat.shape[0], GLA_HEADS, GLA_DK, GLA_DV), f32)
    oc_f, sf = _chunked_gla(qc, kc, vc, lac[0], s0, False)
    oc_b, sb = _chunked_gla(qc, kc, vc, lac[1], s0, True)
    ol_f, _ = _chunked_gla(ql, kl, vl, lal[0], sf, False)
    ol_b, _ = _chunked_gla(ql, kl, vl, lal[1], sb, True)
    return _gated_head_norm(oc_f + oc_b, gc, ln_g), _gated_head_norm(ol_f + ol_b, gl, ln_g)


def _sq_relu_mlp(h, w1, w2):
    return jnp.square(jax.nn.relu(h @ w1)) @ w2


def setup_inputs(seed: int = 0) -> dict:
    key = jax.random.key(seed)
    keys = iter(jax.random.split(key, 64))
    f32 = jnp.float32

    def nrm(shape, scale):
        return jax.random.normal(next(keys), shape, f32) * scale

    n_idx = jnp.arange(S5_N, dtype=f32)
    gam = 1.0 - 2.0 ** (-5.0 - jnp.arange(RET_HEADS, dtype=f32))
    inp = {}
    inp['x'] = nrm((BATCH, SEQ, D_MODEL), 1.0)
    inp['c'] = nrm((BATCH, D_MODEL), 1.0)
    inp['ctx'] = nrm((BATCH, CTX_LEN, D_MODEL), 1.0)
    inp['c_ctx'] = nrm((D_MODEL,), 1.0)
    inp['ada_w'] = nrm((DEPTH, D_MODEL, N_ADA * D_MODEL), 0.5 * D_MODEL ** -0.5)
    inp['ada_b'] = nrm((DEPTH, N_ADA * D_MODEL), 0.02)
    inp['norm1_g'] = 1.0 + nrm((DEPTH, D_MODEL), 0.02)
    inp['norm2_g'] = 1.0 + nrm((DEPTH, D_MODEL), 0.02)
    inp['w_in'] = nrm((DEPTH, D_MODEL, IN_TOTAL), D_MODEL ** -0.5)
    inp['w_out'] = nrm((DEPTH, MIX_W, D_MODEL), MIX_W ** -0.5)
    inp['s5_lam_re'] = -0.5 + nrm((DEPTH, 2, S5_G, S5_N), 0.01)
    inp['s5_lam_im'] = math.pi * n_idx + nrm((DEPTH, 2, S5_G, S5_N), 0.01)
    inp['s5_log_dt'] = jax.random.uniform(next(keys), (DEPTH, 2, S5_G), f32, math.log(1e-3), math.log(1e-1))
    inp['s5_b_re'] = nrm((DEPTH, 2, S5_G, S5_N, S5_P), (2 * S5_P) ** -0.5)
    inp['s5_b_im'] = nrm((DEPTH, 2, S5_G, S5_N, S5_P), (2 * S5_P) ** -0.5)
    inp['s5_c_re'] = nrm((DEPTH, 2, S5_G, S5_P, S5_N), 0.5)
    inp['s5_c_im'] = nrm((DEPTH, 2, S5_G, S5_P, S5_N), 0.5)
    inp['s5_d'] = nrm((DEPTH, GROUP_W), 0.5)
    inp['s5_glu_w'] = nrm((DEPTH, GROUP_W, GROUP_W), GROUP_W ** -0.5)
    inp['s5_glu_b'] = nrm((DEPTH, GROUP_W), 0.02)
    inp['rw_mu'] = jax.random.uniform(next(keys), (DEPTH, RW_COLS), f32)
    inp['rw_w0'] = jnp.linspace(-6.5, -1.5, GROUP_W, dtype=f32) + nrm((DEPTH, 2, GROUP_W), 0.1)
    inp['rw_w_up'] = nrm((DEPTH, 2, RW_W_RANK, GROUP_W), 0.5 * RW_W_RANK ** -0.5)
    inp['rw_a0'] = nrm((DEPTH, 2, GROUP_W), 0.1)
    inp['rw_a_up'] = nrm((DEPTH, 2, RW_A_RANK, GROUP_W), 0.5 * RW_A_RANK ** -0.5)
    inp['rw_g_up'] = nrm((DEPTH, RW_G_RANK, GROUP_W), RW_G_RANK ** -0.5)
    inp['rw_k_k'] = 0.85 + nrm((DEPTH, GROUP_W), 0.02)
    inp['rw_k_a'] = 1.0 + nrm((DEPTH, GROUP_W), 0.02)
    inp['rw_r_k'] = nrm((DEPTH, RW_HEADS, RW_HEAD), 0.1)
    inp['rw_ln_g'] = 1.0 + nrm((DEPTH, GROUP_W), 0.02)
    inp['rw_ln_b'] = nrm((DEPTH, GROUP_W), 0.02)
    inp['ret_decay_logit'] = jnp.log(gam / (1.0 - gam)) + nrm((DEPTH, 2, RET_HEADS), 0.05)
    inp['ret_ln_g'] = 1.0 + nrm((DEPTH, GROUP_W), 0.02)
    inp['gla_a_up'] = nrm((DEPTH, 2, GLA_RANK, GLA_QK), GLA_RANK ** -0.5)
    inp['gla_a_b'] = 1.0 + nrm((DEPTH, 2, GLA_QK), 0.5)
    inp['gla_ln_g'] = 1.0 + nrm((DEPTH, GROUP_W), 0.02)
    inp['mlp_w1'] = nrm((DEPTH, D_MODEL, D_FF), D_MODEL ** -0.5)
    inp['mlp_w2'] = nrm((DEPTH, D_FF, D_MODEL), D_FF ** -0.5)
    inp['final_g'] = 1.0 + nrm((D_MODEL,), 0.02)
    return inp


def reference(x, c, ctx, c_ctx, ada_w, ada_b, norm1_g, norm2_g, w_in, w_out,
              s5_lam_re, s5_lam_im, s5_log_dt, s5_b_re, s5_b_im, s5_c_re, s5_c_im, s5_d, s5_glu_w, s5_glu_b,
              rw_mu, rw_w0, rw_w_up, rw_a0, rw_a_up, rw_g_up, rw_k_k, rw_k_a, rw_r_k, rw_ln_g, rw_ln_b,
              ret_decay_logit, ret_ln_g, gla_a_up, gla_a_b, gla_ln_g, mlp_w1, mlp_w2, final_g):
    l_ = x.shape[1]
    rows = l_ // GRID_W
    rows_idx = jnp.repeat(jnp.arange(rows, dtype=jnp.int32), GRID_W)
    cols_idx = jnp.tile(jnp.arange(GRID_W, dtype=jnp.int32), rows)
    cond_lat = jax.nn.silu(c)
    cond_ctx = jax.nn.silu(c_ctx)[None]
    xc = ctx
    for i in range(DEPTH):
        last = i == DEPTH - 1
        mod_l = (cond_lat @ ada_w[i] + ada_b[i])[:, None, :]
        mod_c = (cond_ctx @ ada_w[i] + ada_b[i])[:, None, :]
        sh1, sc1, g1, sh2, sc2, g2 = jnp.split(mod_l, N_ADA, axis=-1)
        sh1c, sc1c, g1c, sh2c, sc2c, g2c = jnp.split(mod_c, N_ADA, axis=-1)

        h_l = _rmsnorm(x, norm1_g[i]) * (1.0 + sc1) + sh1
        h_c = _rmsnorm(xc, norm1_g[i]) * (1.0 + sc1c) + sh1c
        z_l = h_l @ w_in[i]
        z_c = h_c @ w_in[i]
        s5_l, rw_l, ret_l, gla_l = jnp.split(z_l, IN_SPLITS, axis=-1)
        s5_c, rw_c, ret_c, gla_c = jnp.split(z_c, IN_SPLITS, axis=-1)

        ya_c, ya_l = _s5(s5_c, s5_l, s5_lam_re[i], s5_lam_im[i], s5_log_dt[i], s5_b_re[i], s5_b_im[i],
                         s5_c_re[i], s5_c_im[i], s5_d[i], s5_glu_w[i], s5_glu_b[i])
        yb_c, yb_l = _rwkv7(rw_c, rw_l, rows, rw_mu[i], rw_w0[i], rw_w_up[i], rw_a0[i], rw_a_up[i], rw_g_up[i],
                            rw_k_k[i], rw_k_a[i], rw_r_k[i], rw_ln_g[i], rw_ln_b[i])
        yc_c, yc_l = _retention(ret_c, ret_l, rows_idx, cols_idx, ret_decay_logit[i], ret_ln_g[i])
        yd_c, yd_l = _gla(gla_c, gla_l, gla_a_up[i], gla_a_b[i], gla_ln_g[i])

        mix_l = jnp.concatenate([ya_l, yb_l, yc_l, yd_l], axis=-1) @ w_out[i]
        x = x + g1 * mix_l
        h_l = _rmsnorm(x, norm2_g[i]) * (1.0 + sc2) + sh2
        x = x + g2 * _sq_relu_mlp(h_l, mlp_w1[i], mlp_w2[i])

        if not last:
            mix_c = jnp.concatenate([ya_c, yb_c, yc_c, yd_c], axis=-1) @ w_out[i]
            xc = xc + g1c * mix_c
            h_c = _rmsnorm(xc, norm2_g[i]) * (1.0 + sc2c) + sh2c
            xc = xc + g2c * _sq_relu_mlp(h_c, mlp_w1[i], mlp_w2[i])
    return _rmsnorm(x, final_g)
```

```python
import functools
import math

import jax
import jax.numpy as jnp
from jax import lax
from jax.experimental import pallas as pl
from jax.experimental.pallas import tpu as pltpu

F32 = jnp.float32
BF16 = jnp.bfloat16

GRID_W = 64
GROUP_W = 256
N_ADA = 6
EPS = 1e-6
GN_EPS = 64e-5
CHUNK = 64
HEADS = 4
HEAD_V = 64
S5_P = 16
S5_G = 16
S5_N = 64
S5_T = 32
S5_TP = S5_T * S5_P
RW_COLS = 896
RW_W_RANK = 32
RW_A_RANK = 32
RW_G_RANK = 64
RET_DK = 64
GLA_DK = 32
GLA_QK = 128
GLA_RANK = 16
GLA_TAU = 16.0
GLA_COLS_PAD = 896
ROPE_BASE = 10000.0
VMEM_LIMIT = 56 * 1024 * 1024


def _cparams(*sem):
    return pltpu.CompilerParams(dimension_semantics=sem, vmem_limit_bytes=VMEM_LIMIT)


def _mm(a, b):
    return jnp.dot(a.astype(BF16), b.astype(BF16), preferred_element_type=F32)


def _mm_nt(a, b):
    return lax.dot_general(a.astype(BF16), b.astype(BF16), (((1,), (1,)), ((), ())),
                           preferred_element_type=F32)


def _mm_tn(a, b):
    return lax.dot_general(a.astype(BF16), b.astype(BF16), (((0,), (0,)), ((), ())),
                           preferred_element_type=F32)


def _mm_x3(a, b):
    ah = a.astype(BF16)
    al = (a - ah.astype(F32)).astype(BF16)
    bh = b.astype(BF16)
    bl = (b - bh.astype(F32)).astype(BF16)
    return (jnp.dot(ah, bh, preferred_element_type=F32) + jnp.dot(ah, bl, preferred_element_type=F32)
            + jnp.dot(al, bh, preferred_element_type=F32))


def _split3(x):
    hi = x.astype(BF16)
    r1 = x - hi.astype(F32)
    mid = r1.astype(BF16)
    lo = (r1 - mid.astype(F32)).astype(BF16)
    return hi, mid, lo


def _mm_left01(m01, x):
    hi, mid, lo = _split3(x)
    return (jnp.dot(m01, hi, preferred_element_type=F32)
            + jnp.dot(m01, mid, preferred_element_type=F32)
            + jnp.dot(m01, lo, preferred_element_type=F32))


def _mm_right01(x, m01):
    hi, mid, lo = _split3(x)
    return (jnp.dot(hi, m01, preferred_element_type=F32)
            + jnp.dot(mid, m01, preferred_element_type=F32)
            + jnp.dot(lo, m01, preferred_element_type=F32))


def _sigmoid(x):
    return 1.0 / (1.0 + jnp.exp(-x))


def _softplus(x):
    return jnp.maximum(x, 0.0) + jnp.log(1.0 + jnp.exp(-jnp.abs(x)))


def _silu(x):
    return x * _sigmoid(x)


def _gelu_tanh(x):
    return 0.5 * x * (1.0 + jnp.tanh(math.sqrt(2.0 / math.pi) * (x + 0.044715 * x * x * x)))


def _tri_masks(t, reverse):
    ri = lax.broadcasted_iota(jnp.int32, (t, t), 0)
    ci = lax.broadcasted_iota(jnp.int32, (t, t), 1)
    if reverse:
        return ci >= ri, ci > ri, ci == ri
    return ci <= ri, ci < ri, ci == ri


def _row_to_col(row, eye):
    n = row.shape[1]
    return jnp.sum(jnp.where(eye, jnp.broadcast_to(row, (n, n)), 0.0), axis=1, keepdims=True)


def _mod_kernel(cond_ref, w_ref, b_ref, o_ref):
    c = cond_ref[...]
    o_ref[0] = _mm(_silu(c), w_ref[0]) + b_ref[0]


def _modulation(cond, ada_w, ada_b):
    depth, d, n = ada_w.shape
    tn = 1536
    return pl.pallas_call(
        _mod_kernel,
        grid=(depth, n // tn),
        in_specs=[pl.BlockSpec((8, d), lambda i, j: (0, 0)),
                  pl.BlockSpec((1, d, tn), lambda i, j: (i, 0, j)),
                  pl.BlockSpec((1, 1, tn), lambda i, j: (i, 0, j))],
        out_specs=pl.BlockSpec((1, 8, tn), lambda i, j: (i, 0, j)),
        out_shape=jax.ShapeDtypeStruct((depth, 8, n), F32),
        compiler_params=_cparams("parallel", "parallel"),
        name="adaln_mod",
    )(cond, ada_w, ada_b.reshape(depth, 1, n))


def _modnorm(x, g, sc, sh):
    ms = jnp.mean(x * x, axis=-1, keepdims=True)
    return x * lax.rsqrt(ms + EPS) * g * (1.0 + sc) + sh


def _inproj_kernel(x_ref, g_ref, sc_ref, sh_ref, w5_ref, wr_ref, wt_ref, wg_ref,
                   o5_ref, or_ref, ot_ref, og_ref):
    hb = _modnorm(x_ref[0], g_ref[...], sc_ref[0], sh_ref[0]).astype(BF16)
    o5_ref[0] = jnp.dot(hb, w5_ref[...], preferred_element_type=F32)
    or_ref[0] = jnp.dot(hb, wr_ref[...], preferred_element_type=F32)
    ot_ref[0] = jnp.dot(hb, wt_ref[...], preferred_element_type=F32)
    og_ref[0] = jnp.dot(hb, wg_ref[...], preferred_element_type=F32)


def _inproj(x, g, sc, sh, w5, wr, wt, wg, tm):
    b, l, d = x.shape
    widths = (w5.shape[1], wr.shape[1], wt.shape[1], wg.shape[1])
    tok = lambda n: pl.BlockSpec((1, tm, n), lambda i, j: (i, j, 0))
    vec = pl.BlockSpec((1, 1, d), lambda i, j: (i, 0, 0))
    full = lambda w: pl.BlockSpec(w.shape, lambda i, j: (0, 0))
    return pl.pallas_call(
        _inproj_kernel,
        grid=(b, l // tm),
        in_specs=[tok(d), pl.BlockSpec((1, d), lambda i, j: (0, 0)), vec, vec,
                  full(w5), full(wr), full(wt), full(wg)],
        out_specs=[tok(n) for n in widths],
        out_shape=[jax.ShapeDtypeStruct((b, l, n), F32) for n in widths],
        compiler_params=_cparams("parallel", "parallel"),
        name="norm_inproj",
    )(x, g, sc, sh, w5, wr, wt, wg)


def _s5_tables(lam_re, lam_im, log_dt, b_re, b_im, c_re, c_im, nlev):
    hp = lax.Precision.HIGHEST
    t = S5_T
    lam = lax.complex(jnp.minimum(lam_re.astype(F32), -1e-4), lam_im.astype(F32))
    ldt = lam * jnp.exp(log_dt.astype(F32))[..., None]
    a_bar = jnp.exp(ldt)
    bb = ((a_bar - 1.0) / lam)[..., None] * lax.complex(b_re.astype(F32), b_im.astype(F32))
    cm = lax.complex(c_re.astype(F32), c_im.astype(F32))
    tau = jnp.arange(t + 1, dtype=F32)
    apow = jnp.exp(ldt[:, :, None, :] * tau[None, None, :, None])
    taps = jnp.einsum('dgpn,dgtn,dgnq->dgtpq', cm, apow[:, :, :t], bb, precision=hp).real
    s_i = jnp.arange(t)[:, None]
    t_i = jnp.arange(t)[None, :]

    def toeplitz(k, lag, ok):
        m = k[:, jnp.clip(lag, 0, t - 1)]
        m = jnp.where(ok[None, :, :, None, None], m, 0.0)
        return m.transpose(0, 1, 4, 2, 3).reshape(S5_G, S5_TP, S5_TP)

    conv = jnp.stack([toeplitz(taps[0], t_i - s_i, t_i >= s_i),
                      toeplitz(taps[1], s_i - t_i, s_i >= t_i)])

    def pack(zc):
        return jnp.concatenate([zc.real, zc.imag], axis=-1)

    win_f = apow[0][:, t - 1 - jnp.arange(t), None, :] * bb[0].transpose(0, 2, 1)[:, None]
    win_b = apow[1][:, jnp.arange(t), None, :] * bb[1].transpose(0, 2, 1)[:, None]
    win = jnp.stack([pack(win_f), pack(win_b)]).reshape(2, S5_G, S5_TP, 2 * S5_N)
    ca_f = cm[0][:, None] * apow[0][:, 1 + jnp.arange(t), None, :]
    ca_b = cm[1][:, None] * apow[1][:, t - jnp.arange(t), None, :]

    def outpack(ca):
        w = jnp.concatenate([ca.real, -ca.imag], axis=-1)
        return w.reshape(S5_G, S5_TP, 2 * S5_N).transpose(0, 2, 1)

    wout = jnp.stack([outpack(ca_f), outpack(ca_b)])
    lev = (2.0 ** jnp.arange(nlev, dtype=F32)) * t
    pw = jnp.exp(ldt[:, :, None, :] * lev[None, None, :, None])
    p1 = jnp.concatenate([pw.real, pw.real], axis=-1)
    p2 = jnp.concatenate([-pw.imag, pw.imag], axis=-1)
    pw = jnp.stack([p1, p2], axis=3)
    return conv.astype(BF16), win.astype(BF16), wout.astype(BF16), pw


def _s5_kernel(nc, nlev, u_ref, conv_ref, win_ref, wout_ref, pw_ref, h0_ref, y_ref, hfin_ref):
    u = u_ref[0, 0].astype(BF16)
    row = lax.broadcasted_iota(jnp.int32, (nc, 2 * S5_N), 0)

    def cmul(x, d, j):
        return pw_ref[d, 0, j, 0:1] * x + pw_ref[d, 0, j, 1:2] * pltpu.roll(x, S5_N, 1)

    y = None
    for d in range(2):
        v = jnp.dot(u, win_ref[d, 0], preferred_element_type=F32)
        h0 = h0_ref[0, d, 0]
        if d == 0:
            x = jnp.where(row == 0, h0, pltpu.roll(v, 1, 0))
        else:
            x = jnp.where(row == nc - 1, h0, pltpu.roll(v, nc - 1, 0))
        for j in range(nlev):
            sh = 2 ** j
            if d == 0:
                xs = jnp.where(row >= sh, pltpu.roll(x, sh, 0), 0.0)
            else:
                xs = jnp.where(row < nc - sh, pltpu.roll(x, nc - sh, 0), 0.0)
            x = x + cmul(xs, d, j)
        last = nc - 1 if d == 0 else 0
        hfin_ref[0, d, 0] = cmul(x[last:last + 1], d, 0) + v[last:last + 1]
        yd = (jnp.dot(u, conv_ref[d, 0], preferred_element_type=F32)
              + _mm(x, wout_ref[d, 0]))
        y = yd if y is None else y + yd
    y_ref[0, 0] = y


def _s5_scan(u, tables, h0):
    conv, win, wout, pw = tables
    b, l, _ = u.shape
    nc = l // S5_T
    nlev = max(1, (nc - 1).bit_length())
    pw = pw[:, :, :nlev]
    uf = u.reshape(b, nc, S5_T, S5_G, S5_P).transpose(0, 3, 1, 2, 4).reshape(b, S5_G, nc, S5_TP)
    n2 = 2 * S5_N
    y, hfin = pl.pallas_call(
        functools.partial(_s5_kernel, nc, nlev),
        grid=(b, S5_G),
        in_specs=[pl.BlockSpec((1, 1, nc, S5_TP), lambda i, g: (i, g, 0, 0)),
                  pl.BlockSpec((2, 1, S5_TP, S5_TP), lambda i, g: (0, g, 0, 0)),
                  pl.BlockSpec((2, 1, S5_TP, n2), lambda i, g: (0, g, 0, 0)),
                  pl.BlockSpec((2, 1, n2, S5_TP), lambda i, g: (0, g, 0, 0)),
                  pl.BlockSpec((2, 1, nlev, 2, n2), lambda i, g: (0, g, 0, 0, 0)),
                  pl.BlockSpec((1, 2, 1, 1, n2), lambda i, g: (i, 0, g, 0, 0))],
        out_specs=[pl.BlockSpec((1, 1, nc, S5_TP), lambda i, g: (i, g, 0, 0)),
                   pl.BlockSpec((1, 2, 1, 1, n2), lambda i, g: (i, 0, g, 0, 0))],
        out_shape=[jax.ShapeDtypeStruct((b, S5_G, nc, S5_TP), F32),
                   jax.ShapeDtypeStruct((b, 2, S5_G, 1, n2), F32)],
        compiler_params=_cparams("parallel", "parallel"),
        name="s5_scan",
    )(uf, conv, win, wout, pw, h0)
    y = y.reshape(b, S5_G, nc, S5_T, S5_P).transpose(0, 2, 3, 1, 4).reshape(b, l, GROUP_W)
    return y, hfin


def _rw_prep_kernel(grid_shift, nt, z_ref, zp_ref, zn_ref, mu_ref, w0_ref, wup_ref, a0_ref, aup_ref,
                    gup_ref, kk_ref, ka_ref, rk_ref, bones_ref,
                    r_ref, v_ref, kkn_ref, g_ref, bonus_ref,
                    lw0_ref, kd0_ref, as0_ref, lw1_ref, kd1_ref, as1_ref):
    i = pl.program_id(1)
    z = z_ref[0]
    tm = z.shape[0]
    row = lax.broadcasted_iota(jnp.int32, z.shape, 0)
    lane = lax.broadcasted_iota(jnp.int32, z.shape, 1)
    prev1 = pltpu.roll(z, 1, 0)
    next1 = pltpu.roll(z, tm - 1, 0)
    if grid_shift:
        col = row & (GRID_W - 1)
        left = jnp.where(col == 0, 0.0, prev1)
        right = jnp.where(col == GRID_W - 1, 0.0, next1)
        zp = jnp.where(i > 0, zp_ref[0], 0.0)
        zn = jnp.where(i < nt - 1, zn_ref[0], 0.0)
        up = jnp.concatenate([zp, z[:tm - GRID_W]], axis=0)
        down = jnp.concatenate([z[GRID_W:], zn], axis=0)
        q = RW_COLS // 4
        shifted = jnp.where(lane < q, left,
                            jnp.where(lane < 2 * q, right, jnp.where(lane < 3 * q, up, down)))
    else:
        prev = jnp.where(row == 0, 0.0, prev1)
        nxt = jnp.where(row == tm - 1, 0.0, next1)
        shifted = jnp.where(lane < RW_COLS // 2, prev, nxt)
    zm = z + mu_ref[...] * (shifted - z)
    r = zm[:, 0:256]
    k = zm[:, 256:512]
    v = zm[:, 512:768]
    lo = zm[:, 768:896]
    bones = bones_ref[...]
    g_ref[0] = _mm(_sigmoid(lo), gup_ref[...])
    kk = k * kk_ref[...]
    kk = kk * lax.rsqrt(_mm_right01(kk * kk, bones) + 1e-12)
    r_ref[0] = r
    v_ref[0] = v
    kkn_ref[0] = kk
    bonus_ref[0] = _mm_right01(r * k * rk_ref[...], bones) * v
    tlo = jnp.tanh(lo)
    for d, (lw_ref, kd_ref, as_ref) in enumerate(((lw0_ref, kd0_ref, as0_ref), (lw1_ref, kd1_ref, as1_ref))):
        w_raw = -_softplus(-(w0_ref[d] + _mm(tlo, wup_ref[d]))) - 0.5
        lw_ref[0] = -jnp.exp(w_raw)
        a = _sigmoid(a0_ref[d] + _mm(lo, aup_ref[d]))
        kd_ref[0] = k * (1.0 + (a - 1.0) * ka_ref[...])
        as_ref[0] = a


def _rw_prep(z, grid_shift, p):
    b, l, _ = z.shape
    tm = 256 if grid_shift else l
    nt = l // tm
    hb = tm // GRID_W
    nhb = l // GRID_W
    tok = pl.BlockSpec((1, tm, GROUP_W), lambda i, j: (i, j, 0))
    full = lambda a: pl.BlockSpec(a.shape, lambda i, j: (0,) * a.ndim)
    params = (p['mu'], p['w0'], p['wup'], p['a0'], p['aup'], p['gup'], p['kk'], p['ka'], p['rk'], p['bones'])
    return pl.pallas_call(
        functools.partial(_rw_prep_kernel, grid_shift, nt),
        grid=(b, nt),
        in_specs=[pl.BlockSpec((1, tm, RW_COLS), lambda i, j: (i, j, 0)),
                  pl.BlockSpec((1, GRID_W, RW_COLS), lambda i, j: (i, jnp.maximum(j * hb - 1, 0), 0)),
                  pl.BlockSpec((1, GRID_W, RW_COLS), lambda i, j: (i, jnp.minimum((j + 1) * hb, nhb - 1), 0))]
                 + [full(a) for a in params],
        out_specs=[tok] * 11,
        out_shape=[jax.ShapeDtypeStruct((b, l, GROUP_W), F32)] * 11,
        compiler_params=_cparams("parallel", "parallel"),
        name="rwkv_prep",
    )(z, z, z, *params)


def _rw_scan_kernel(nc, rf_ref, vf_ref, kkf_ref, lwf_ref, kdf_ref, asf_ref,
                    rb_ref, vb_ref, kkb_ref, lwb_ref, kdb_ref, asb_ref, s0_ref,
                    yf_ref, yb_ref, sfin_ref, st_ref):
    c = pl.program_id(1)

    @pl.when(c == 0)
    def _():
        st_ref[...] = s0_ref[0]

    t = CHUNK
    dirs = ((rf_ref, vf_ref, kkf_ref, lwf_ref, kdf_ref, asf_ref, yf_ref),
            (rb_ref, vb_ref, kkb_ref, lwb_ref, kdb_ref, asb_ref, yb_ref))
    for d, (r_ref, v_ref, kk_ref, lw_ref, kd_ref, as_ref, y_ref) in enumerate(dirs):
        incl, strict, eye = _tri_masks(t, d == 1)
        incl01 = jnp.where(incl, 1.0, 0.0).astype(BF16)
        r, v, kk, lw, kd = r_ref[0], v_ref[0], kk_ref[0], lw_ref[0], kd_ref[0]
        cin = _mm_left01(incl01, lw)
        clast = cin[t - 1:t] if d == 0 else cin[0:1]
        e_in = jnp.exp(cin)
        e_neg = jnp.exp(-cin)
        rt = r * e_in
        at = -kk * jnp.exp(cin - lw)
        bvec = kk * as_ref[0]
        bt = bvec * e_neg
        kt = kd * e_neg
        dl = jnp.exp(clast - cin)
        bh = bvec * dl
        kh = kd * dl
        dec = jnp.exp(clast)
        ys = []
        for h in range(HEADS):
            sl = slice(h * HEAD_V, (h + 1) * HEAD_V)
            a_all = _mm_nt(jnp.concatenate([at[:, sl], rt[:, sl]], axis=0),
                           jnp.concatenate([bt[:, sl], kt[:, sl]], axis=0))
            a_ab = jnp.where(strict, a_all[0:t, 0:t], 0.0)
            a_ak = jnp.where(strict, a_all[0:t, t:2 * t], 0.0)
            a_rb = jnp.where(incl, a_all[t:2 * t, 0:t], 0.0)
            a_rk = jnp.where(incl, a_all[t:2 * t, t:2 * t], 0.0)
            vh = v[:, sl]
            zz = jnp.concatenate([at[:, sl], _mm(a_ak, vh)], axis=1)
            npow = a_ab
            for it in range(6):
                zz = zz + _mm_x3(npow, zz)
                if it < 5:
                    npow = _mm_x3(npow, npow)
            gh = _mm_tn(bh[:, sl], zz)
            qy = _mm(a_rb, zz)
            st = st_ref[d, h]
            hm = gh[:, HEAD_V:] + _mm_tn(kh[:, sl], vh)
            q = rt[:, sl] + qy[:, :HEAD_V]
            y0 = qy[:, HEAD_V:] + _mm(a_rk, vh)
            ys.append(_mm(q, st) + y0)
            dcol = _row_to_col(dec[:, sl], eye)
            st_ref[d, h] = dcol * st + _mm(gh[:, :HEAD_V], st) + hm
        y_ref[0] = jnp.concatenate(ys, axis=1)

    @pl.when(c == nc - 1)
    def _():
        sfin_ref[0] = st_ref[...]


def _rw_scan(pre, s0):
    r, v, kk, _, _, lw0, kd0, as0, lw1, kd1, as1 = pre
    b, l, _ = r.shape
    nc = l // CHUNK
    fw = pl.BlockSpec((1, CHUNK, GROUP_W), lambda i, c: (i, c, 0))
    bw = pl.BlockSpec((1, CHUNK, GROUP_W), lambda i, c: (i, nc - 1 - c, 0))
    st = pl.BlockSpec((1, 2, HEADS, HEAD_V, HEAD_V), lambda i, c: (i, 0, 0, 0, 0))
    return pl.pallas_call(
        functools.partial(_rw_scan_kernel, nc),
        grid=(b, nc),
        in_specs=[fw] * 6 + [bw] * 6 + [st],
        out_specs=[fw, bw, st],
        out_shape=[jax.ShapeDtypeStruct((b, l, GROUP_W), F32)] * 2
                  + [jax.ShapeDtypeStruct((b, 2, HEADS, HEAD_V, HEAD_V), F32)],
        scratch_shapes=[pltpu.VMEM((2, HEADS, HEAD_V, HEAD_V), F32)],
        compiler_params=_cparams("parallel", "arbitrary"),
        name="rwkv_scan",
    )(r, v, kk, lw0, kd0, as0, r, v, kk, lw1, kd1, as1, s0)


def _rope(x, cos, sin_signed):
    lane = lax.broadcasted_iota(jnp.int32, x.shape, 1)
    n = x.shape[1]
    swapped = jnp.where((lane & 16) == 0, pltpu.roll(x, n - 16, 1), pltpu.roll(x, 16, 1))
    return x * cos + swapped * sin_signed


def _la_scan_kernel(nc, dk, mode, *refs):
    if mode == 'ret':
        (qf_ref, kf_ref, vf_ref, qb_ref, kb_ref, vb_ref, lg_ref,
         crf_ref, srf_ref, crb_ref, srb_ref, cc_ref, sc_ref, s0_ref,
         of_ref, ob_ref, sfin_ref, st_ref) = refs
    else:
        (qf_ref, kf_ref, vf_ref, af_ref, qb_ref, kb_ref, vb_ref, ab_ref, aup_ref, abias_ref, s0_ref,
         of_ref, ob_ref, sfin_ref, st_ref) = refs
    c = pl.program_id(1)

    @pl.when(c == 0)
    def _():
        st_ref[...] = s0_ref[0]

    t = CHUNK
    hk = HEADS * dk
    scale = dk ** -0.5
    dirs = ((qf_ref, kf_ref, vf_ref, of_ref), (qb_ref, kb_ref, vb_ref, ob_ref))
    for d, (q_ref, k_ref, v_ref, o_ref) in enumerate(dirs):
        incl, _, _ = _tri_masks(t, d == 1)
        incl01 = jnp.where(incl, 1.0, 0.0).astype(BF16)
        eye = _tri_masks(dk, False)[2]
        q, k, v = q_ref[0], k_ref[0] * scale, v_ref[0]
        if mode == 'ret':
            lw = jnp.broadcast_to(lg_ref[d], (t, hk))
            cr_ref, sr_ref = (crf_ref, srf_ref) if d == 0 else (crb_ref, srb_ref)
            lane = lax.broadcasted_iota(jnp.int32, (t, hk), 1)
            by_row = (lane & 32) == 0
            cos = jnp.where(by_row, cr_ref[0], cc_ref[...])
            sin = jnp.where(by_row, sr_ref[0], sc_ref[...])
            q = _rope(q, cos, sin)
            k = _rope(k, cos, sin)
        else:
            a_ref = af_ref if d == 0 else ab_ref
            lw = -_softplus(-(_mm(a_ref[0], aup_ref[d]) + abias_ref[d])) * (1.0 / GLA_TAU)
        cin = _mm_left01(incl01, lw)
        clast = cin[t - 1:t] if d == 0 else cin[0:1]
        qt = q * jnp.exp(cin)
        kt = k * jnp.exp(-cin)
        kh = k * jnp.exp(clast - cin)
        dec = jnp.exp(clast)
        outs = []
        for h in range(HEADS):
            sk = slice(h * dk, (h + 1) * dk)
            sv = slice(h * HEAD_V, (h + 1) * HEAD_V)
            st = st_ref[d, h]
            a = jnp.where(incl, _mm_nt(qt[:, sk], kt[:, sk]), 0.0)
            outs.append(_mm(a, v[:, sv]) + _mm(qt[:, sk], st))
            st_ref[d, h] = _row_to_col(dec[:, sk], eye) * st + _mm_tn(kh[:, sk], v[:, sv])
        o_ref[0] = jnp.concatenate(outs, axis=1)

    @pl.when(c == nc - 1)
    def _():
        sfin_ref[0] = st_ref[...]


def _rope_tables(rows):
    nf = RET_DK // 4
    inv = ROPE_BASE ** (-jnp.arange(nf, dtype=F32) / nf)
    lane = jnp.arange(GROUP_W)
    freq = inv[lane % nf]
    sign = jnp.where((lane & 16) == 0, -1.0, 1.0)
    ar = jnp.arange(rows, dtype=F32)[:, None] * freq[None, :]
    ac = jnp.arange(GRID_W, dtype=F32)[:, None] * freq[None, :]
    return (jnp.cos(ar).reshape(rows, 1, GROUP_W), (jnp.sin(ar) * sign).reshape(rows, 1, GROUP_W),
            jnp.cos(ac), jnp.sin(ac) * sign)


def _ret_scan(z, lg, rope, s0):
    b, l, _ = z.shape
    nc = l // CHUNK
    cr, sr, cc, sc = rope
    fw = lambda j: pl.BlockSpec((1, CHUNK, GROUP_W), lambda i, c: (i, c, j))
    bw = lambda j: pl.BlockSpec((1, CHUNK, GROUP_W), lambda i, c: (i, nc - 1 - c, j))
    rowf = pl.BlockSpec((1, 1, GROUP_W), lambda i, c: (c, 0, 0))
    rowb = pl.BlockSpec((1, 1, GROUP_W), lambda i, c: (nc - 1 - c, 0, 0))
    colt = pl.BlockSpec((GRID_W, GROUP_W), lambda i, c: (0, 0))
    st = pl.BlockSpec((1, 2, HEADS, RET_DK, HEAD_V), lambda i, c: (i, 0, 0, 0, 0))
    return pl.pallas_call(
        functools.partial(_la_scan_kernel, nc, RET_DK, 'ret'),
        grid=(b, nc),
        in_specs=[fw(0), fw(1), fw(2), bw(0), bw(1), bw(2),
                  pl.BlockSpec((2, 1, GROUP_W), lambda i, c: (0, 0, 0)),
                  rowf, rowf, rowb, rowb, colt, colt, st],
        out_specs=[fw(0), bw(0), st],
        out_shape=[jax.ShapeDtypeStruct((b, l, GROUP_W), F32)] * 2
                  + [jax.ShapeDtypeStruct((b, 2, HEADS, RET_DK, HEAD_V), F32)],
        scratch_shapes=[pltpu.VMEM((2, HEADS, RET_DK, HEAD_V), F32)],
        compiler_params=_cparams("parallel", "arbitrary"),
        name="retention_scan",
    )(z, z, z, z, z, z, lg, cr, sr, cr, sr, cc, sc, s0)


def _gla_scan(z, aup, abias, s0):
    b, l, _ = z.shape
    nc = l // CHUNK
    blk = lambda w, j, rev: pl.BlockSpec(
        (1, CHUNK, w), (lambda i, c: (i, nc - 1 - c, j)) if rev else (lambda i, c: (i, c, j)))
    st = pl.BlockSpec((1, 2, HEADS, GLA_DK, HEAD_V), lambda i, c: (i, 0, 0, 0, 0))
    ofw = pl.BlockSpec((1, CHUNK, GROUP_W), lambda i, c: (i, c, 0))
    obw = pl.BlockSpec((1, CHUNK, GROUP_W), lambda i, c: (i, nc - 1 - c, 0))
    return pl.pallas_call(
        functools.partial(_la_scan_kernel, nc, GLA_DK, 'gla'),
        grid=(b, nc),
        in_specs=[blk(GLA_QK, 0, False), blk(GLA_QK, 1, False), blk(GROUP_W, 1, False), blk(128, 6, False),
                  blk(GLA_QK, 0, True), blk(GLA_QK, 1, True), blk(GROUP_W, 1, True), blk(128, 6, True),
                  pl.BlockSpec(aup.shape, lambda i, c: (0, 0, 0)),
                  pl.BlockSpec(abias.shape, lambda i, c: (0, 0, 0)), st],
        out_specs=[ofw, obw, st],
        out_shape=[jax.ShapeDtypeStruct((b, l, GROUP_W), F32)] * 2
                  + [jax.ShapeDtypeStruct((b, 2, HEADS, GLA_DK, HEAD_V), F32)],
        scratch_shapes=[pltpu.VMEM((2, HEADS, GLA_DK, HEAD_V), F32)],
        compiler_params=_cparams("parallel", "arbitrary"),
        name="gla_scan",
    )(z, z, z, z, z, z, z, z, aup, abias, s0)


def _mix_kernel(x_ref, g1_ref, y5_ref, u5_ref, d5_ref, gw_ref, gb_ref,
                ryf_ref, ryb_ref, rbonus_ref, rg_ref, rlng_ref, rlnb_ref,
                tof_ref, tob_ref, tg_ref, tln_ref,
                gof_ref, gob_ref, gg_ref, gln_ref,
                bones_ref, wo_ref, o_ref):
    bones = bones_ref[...]
    inv = 1.0 / HEAD_V

    def hmean(a):
        return _mm_right01(a, bones) * inv

    y = y5_ref[0] + d5_ref[...] * u5_ref[0]
    y = _gelu_tanh(y)
    ya = y * _sigmoid(_mm(y, gw_ref[...]) + gb_ref[...])
    yr = ryf_ref[0] + ryb_ref[0]
    dlt = yr - hmean(yr)
    yn = dlt * lax.rsqrt(hmean(dlt * dlt) + GN_EPS)
    yb = (yn * rlng_ref[...] + rlnb_ref[...] + rbonus_ref[0]) * rg_ref[0]
    ot = tof_ref[0] + tob_ref[0]
    yc = ot * lax.rsqrt(hmean(ot * ot) + EPS) * tln_ref[...] * _silu(tg_ref[0])
    og = gof_ref[0] + gob_ref[0]
    yd = og * lax.rsqrt(hmean(og * og) + EPS) * gln_ref[...] * _silu(gg_ref[0])
    mix = (_mm(ya, wo_ref[0]) + _mm(yb, wo_ref[1]) + _mm(yc, wo_ref[2]) + _mm(yd, wo_ref[3]))
    o_ref[0] = x_ref[0] + g1_ref[0] * mix


def _mix(x, g1, y5, u5, rw_pre, rw_y, zret, ret_o, zgla, gla_o, p, tm):
    b, l, d = x.shape
    tok = lambda w, j: pl.BlockSpec((1, tm, w), lambda i, t: (i, t, j))
    vec = lambda a: pl.BlockSpec(a.shape, lambda i, t: (0,) * a.ndim)
    t256 = tok(GROUP_W, 0)
    args = [x, g1, y5, u5, p['s5_d'], p['glu_w'], p['glu_b'],
            rw_y[0], rw_y[1], rw_pre[4], rw_pre[3], p['rw_ln_g'], p['rw_ln_b'],
            ret_o[0], ret_o[1], zret, p['ret_ln_g'],
            gla_o[0], gla_o[1], zgla, p['gla_ln_g'],
            p['bones'], p['w_out']]
    specs = [tok(d, 0), pl.BlockSpec((1, 1, d), lambda i, t: (i, 0, 0)), t256, t256,
             vec(p['s5_d']), vec(p['glu_w']), vec(p['glu_b']),
             t256, t256, t256, t256, vec(p['rw_ln_g']), vec(p['rw_ln_b']),
             t256, t256, tok(GROUP_W, 3), vec(p['ret_ln_g']),
             t256, t256, tok(GROUP_W, 2), vec(p['gla_ln_g']),
             vec(p['bones']), vec(p['w_out'])]
    return pl.pallas_call(
        _mix_kernel,
        grid=(b, l // tm),
        in_specs=specs,
        out_specs=tok(d, 0),
        out_shape=jax.ShapeDtypeStruct((b, l, d), F32),
        compiler_params=_cparams("parallel", "parallel"),
        name="mix_outproj",
    )(*args)


def _mlp_kernel(final, nff, x_ref, g_ref, sc_ref, sh_ref, gate_ref, w1_ref, w2_ref, fg_ref, o_ref):
    x = x_ref[0]
    hb = _modnorm(x, g_ref[...], sc_ref[0], sh_ref[0]).astype(BF16)
    ff = w1_ref.shape[1] // nff
    acc = None
    for j in range(nff):
        a = jnp.maximum(jnp.dot(hb, w1_ref[:, j * ff:(j + 1) * ff], preferred_element_type=F32), 0.0)
        part = jnp.dot((a * a).astype(BF16), w2_ref[j * ff:(j + 1) * ff, :], preferred_element_type=F32)
        acc = part if acc is None else acc + part
    y = x + gate_ref[0] * acc
    if final:
        ms = jnp.mean(y * y, axis=-1, keepdims=True)
        y = y * lax.rsqrt(ms + EPS) * fg_ref[...]
    o_ref[0] = y


def _mlp(x, g, sc, sh, gate, w1, w2, final_g, final, tm):
    b, l, d = x.shape
    tok = pl.BlockSpec((1, tm, d), lambda i, t: (i, t, 0))
    vec = pl.BlockSpec((1, 1, d), lambda i, t: (i, 0, 0))
    row = pl.BlockSpec((1, d), lambda i, t: (0, 0))
    once = lambda a: pl.BlockSpec(a.shape, lambda i, t: (0, 0), pipeline_mode=pl.Buffered(1))
    return pl.pallas_call(
        functools.partial(_mlp_kernel, final, 4),
        grid=(b, l // tm),
        in_specs=[tok, row, vec, vec, vec, once(w1), once(w2), row],
        out_specs=tok,
        out_shape=jax.ShapeDtypeStruct((b, l, d), F32),
        compiler_params=_cparams("parallel", "parallel"),
        name="mlp",
    )(x, g, sc, sh, gate, w1, w2, final_g)


def kernel(x, c, ctx, c_ctx, ada_w, ada_b, norm1_g, norm2_g, w_in, w_out, s5_lam_re, s5_lam_im, s5_log_dt, s5_b_re, s5_b_im, s5_c_re, s5_c_im, s5_d, s5_glu_w, s5_glu_b, rw_mu, rw_w0, rw_w_up, rw_a0, rw_a_up, rw_g_up, rw_k_k, rw_k_a, rw_r_k, rw_ln_g, rw_ln_b, ret_decay_logit, ret_ln_g, gla_a_up, gla_a_b, gla_ln_g, mlp_w1, mlp_w2, final_g):
    b, l, d = x.shape
    lc = ctx.shape[1]
    depth = ada_w.shape[0]
    rows = l // GRID_W
    assert l % 512 == 0 and lc % CHUNK == 0 and lc % S5_T == 0 and d % 128 == 0

    cond = jnp.zeros((8, d), F32).at[:b].set(c).at[b].set(c_ctx)
    mod = _modulation(cond, ada_w, ada_b)

    lane = jnp.arange(GROUP_W)
    bones = (lane[:, None] // HEAD_V == lane[None, :] // HEAD_V).astype(BF16)
    rope_lat = _rope_tables(rows)
    ident = (jnp.ones((lc // CHUNK, 1, GROUP_W), F32), jnp.zeros((lc // CHUNK, 1, GROUP_W), F32),
             jnp.ones((GRID_W, GROUP_W), F32), jnp.zeros((GRID_W, GROUP_W), F32))
    nlev = max(1, (l // S5_T - 1).bit_length())
    c0, c1, c2 = GROUP_W, GROUP_W + RW_COLS, GROUP_W + RW_COLS + 4 * GROUP_W
    row2 = lambda a: a.reshape(1, -1).astype(F32)

    xc = ctx
    for i in range(depth):
        last = i == depth - 1
        m = mod[i].reshape(8, N_ADA, d)
        ml = m[:b, :, None, :]
        mc = jnp.broadcast_to(m[b][None, :, None, :], (b, N_ADA, 1, d))
        wi = w_in[i].astype(BF16)
        w5, wr, wt = wi[:, :c0], wi[:, c0:c1], wi[:, c1:c2]
        wg = jnp.pad(wi[:, c2:], ((0, 0), (0, GLA_COLS_PAD - (wi.shape[1] - c2))))
        n1 = row2(norm1_g[i])
        n2 = row2(norm2_g[i])

        s5_tab = _s5_tables(s5_lam_re[i], s5_lam_im[i], s5_log_dt[i], s5_b_re[i], s5_b_im[i],
                            s5_c_re[i], s5_c_im[i], nlev)
        zpad = lambda a, lo, n: jnp.zeros((2, 128, n), F32).at[:, lo:lo + a.shape[1]].set(a).astype(BF16)
        rwp = dict(
            mu=row2(rw_mu[i]),
            w0=rw_w0[i].reshape(2, 1, GROUP_W), a0=rw_a0[i].reshape(2, 1, GROUP_W),
            wup=zpad(rw_w_up[i], 0, GROUP_W), aup=zpad(rw_a_up[i], RW_W_RANK, GROUP_W),
            gup=jnp.zeros((128, GROUP_W), F32).at[RW_W_RANK + RW_A_RANK:].set(rw_g_up[i]).astype(BF16),
            kk=row2(rw_k_k[i]), ka=row2(rw_k_a[i]), rk=row2(rw_r_k[i]), bones=bones)
        lg = jnp.repeat(jax.nn.log_sigmoid(ret_decay_logit[i].astype(F32)), RET_DK, axis=-1).reshape(2, 1, GROUP_W)
        gla_aup = zpad(gla_a_up[i], 0, GLA_QK)
        gla_ab = gla_a_b[i].reshape(2, 1, GLA_QK).astype(F32)
        mixp = dict(s5_d=row2(s5_d[i]), glu_w=s5_glu_w[i].astype(BF16), glu_b=row2(s5_glu_b[i]),
                    rw_ln_g=row2(rw_ln_g[i]), rw_ln_b=row2(rw_ln_b[i]), ret_ln_g=row2(ret_ln_g[i]),
                    gla_ln_g=row2(gla_ln_g[i]), bones=bones,
                    w_out=w_out[i].astype(BF16).reshape(4, GROUP_W, d))
        w1 = mlp_w1[i].astype(BF16)
        w2 = mlp_w2[i].astype(BF16)

        def mixers(xx, mm, is_lat, states):
            tm = 512 if is_lat else lc
            z5, zr, zt, zg = _inproj(xx, n1, mm[:, 1], mm[:, 0], w5, wr, wt, wg, tm)
            y5, h5 = _s5_scan(z5, s5_tab, states[0])
            pre = _rw_prep(zr, is_lat, rwp)
            yrf, yrb, srw = _rw_scan(pre, states[1])
            otf, otb, sret = _ret_scan(zt, lg, rope_lat if is_lat else ident, states[2])
            ogf, ogb, sgla = _gla_scan(zg, gla_aup, gla_ab, states[3])
            outs = (z5, y5, pre, (yrf, yrb), zt, (otf, otb), zg, (ogf, ogb))
            return outs, (h5, srw, sret, sgla)

        def block(xx, mm, outs, is_lat, fin):
            tm = 512 if is_lat else lc
            z5, y5, pre, yr, zt, ot, zg, og = outs
            x1 = _mix(xx, mm[:, 2], y5, z5, pre, yr, zt, ot, zg, og, mixp, tm)
            return _mlp(x1, n2, mm[:, 4], mm[:, 3], mm[:, 5], w1, w2, row2(final_g), fin, tm)

        zeros = (jnp.zeros((b, 2, S5_G, 1, 2 * S5_N), F32),
                 jnp.zeros((b, 2, HEADS, HEAD_V, HEAD_V), F32),
                 jnp.zeros((b, 2, HEADS, RET_DK, HEAD_V), F32),
                 jnp.zeros((b, 2, HEADS, GLA_DK, HEAD_V), F32))
        outs_c, st_c = mixers(xc, mc, False, zeros)
        outs_l, _ = mixers(x, ml, True, st_c)
        x = block(x, ml, outs_l, True, last)
        if not last:
            xc = block(xc, mc, outs_c, False, False)
    return x
```

```python
import functools
import math

import jax
import jax.numpy as jnp
from jax import lax
from jax.experimental import pallas as pl
from jax.experimental.pallas import tpu as pltpu

F32 = jnp.float32
BF16 = jnp.bfloat16

GRID_W = 64
GROUP_W = 256
N_ADA = 6
EPS = 1e-6
GN_EPS = 64e-5
CHUNK = 64
RW_CPB = 2
LA_CPB = 4
HEADS = 4
HEAD_V = 64
S5_P = 16
S5_G = 16
S5_N = 64
S5_T = 32
S5_TP = S5_T * S5_P
RW_COLS = 896
RW_W_RANK = 32
RW_A_RANK = 32
RW_G_RANK = 64
RET_DK = 64
GLA_DK = 32
GLA_QK = 128
GLA_RANK = 16
GLA_TAU = 16.0
GLA_COLS_PAD = 896
ROPE_BASE = 10000.0
VMEM_LIMIT = 56 * 1024 * 1024


def _cparams(*sem):
    return pltpu.CompilerParams(dimension_semantics=sem, vmem_limit_bytes=VMEM_LIMIT)


def _mm(a, b):
    return jnp.dot(a.astype(BF16), b.astype(BF16), preferred_element_type=F32)


def _mm_nt(a, b):
    return lax.dot_general(a.astype(BF16), b.astype(BF16), (((1,), (1,)), ((), ())),
                           preferred_element_type=F32)


def _mm_tn(a, b):
    return lax.dot_general(a.astype(BF16), b.astype(BF16), (((0,), (0,)), ((), ())),
                           preferred_element_type=F32)


def _split2(x):
    hi = x.astype(BF16)
    return hi, (x - hi.astype(F32)).astype(BF16)


def _join2(p):
    return p[0].astype(F32) + p[1].astype(F32)


def _mm_x3p(a, b):
    return (jnp.dot(a[0], b[0], preferred_element_type=F32) + jnp.dot(a[0], b[1], preferred_element_type=F32)
            + jnp.dot(a[1], b[0], preferred_element_type=F32))


def _chunk_tri01(n, reverse):
    ri = lax.broadcasted_iota(jnp.int32, (n, n), 0)
    ci = lax.broadcasted_iota(jnp.int32, (n, n), 1)
    shift = CHUNK.bit_length() - 1
    tri = (ci >= ri) if reverse else (ci <= ri)
    return jnp.where((ri >> shift) == (ci >> shift), jnp.where(tri, 1.0, 0.0), 0.0).astype(BF16)


def _split3(x):
    hi = x.astype(BF16)
    r1 = x - hi.astype(F32)
    mid = r1.astype(BF16)
    lo = (r1 - mid.astype(F32)).astype(BF16)
    return hi, mid, lo


def _mm_left01(m01, x):
    hi, mid, lo = _split3(x)
    return (jnp.dot(m01, hi, preferred_element_type=F32)
            + jnp.dot(m01, mid, preferred_element_type=F32)
            + jnp.dot(m01, lo, preferred_element_type=F32))


def _mm_right01(x, m01):
    hi, mid, lo = _split3(x)
    return (jnp.dot(hi, m01, preferred_element_type=F32)
            + jnp.dot(mid, m01, preferred_element_type=F32)
            + jnp.dot(lo, m01, preferred_element_type=F32))


def _sigmoid(x):
    return 1.0 / (1.0 + jnp.exp(-x))


def _softplus(x):
    return jnp.maximum(x, 0.0) + jnp.log(1.0 + jnp.exp(-jnp.abs(x)))


def _silu(x):
    return x * _sigmoid(x)


def _gelu_tanh(x):
    return 0.5 * x * (1.0 + jnp.tanh(math.sqrt(2.0 / math.pi) * (x + 0.044715 * x * x * x)))


def _tri_masks(t, reverse):
    ri = lax.broadcasted_iota(jnp.int32, (t, t), 0)
    ci = lax.broadcasted_iota(jnp.int32, (t, t), 1)
    if reverse:
        return ci >= ri, ci > ri, ci == ri
    return ci <= ri, ci < ri, ci == ri


def _row_to_col(row, eye):
    n = row.shape[1]
    return jnp.sum(jnp.where(eye, jnp.broadcast_to(row, (n, n)), 0.0), axis=1, keepdims=True)


def _mod_kernel(cond_ref, w_ref, b_ref, o_ref):
    c = cond_ref[...]
    o_ref[0] = _mm(_silu(c), w_ref[0]) + b_ref[0]


def _modulation(cond, ada_w, ada_b):
    depth, d, n = ada_w.shape
    tn = 1536
    return pl.pallas_call(
        _mod_kernel,
        grid=(depth, n // tn),
        in_specs=[pl.BlockSpec((8, d), lambda i, j: (0, 0)),
                  pl.BlockSpec((1, d, tn), lambda i, j: (i, 0, j)),
                  pl.BlockSpec((1, 1, tn), lambda i, j: (i, 0, j))],
        out_specs=pl.BlockSpec((1, 8, tn), lambda i, j: (i, 0, j)),
        out_shape=jax.ShapeDtypeStruct((depth, 8, n), F32),
        compiler_params=_cparams("parallel", "parallel"),
        name="adaln_mod",
    )(cond, ada_w, ada_b.reshape(depth, 1, n))


def _modnorm(x, g, sc, sh):
    ms = jnp.mean(x * x, axis=-1, keepdims=True)
    return x * lax.rsqrt(ms + EPS) * g * (1.0 + sc) + sh


def _inproj_kernel(x_ref, g_ref, sc_ref, sh_ref, w5_ref, wr_ref, wt_ref, wg_ref,
                   o5_ref, or_ref, ot_ref, og_ref):
    hb = _modnorm(x_ref[0], g_ref[...], sc_ref[0], sh_ref[0]).astype(BF16)
    o5_ref[0] = jnp.dot(hb, w5_ref[...], preferred_element_type=F32)
    or_ref[0] = jnp.dot(hb, wr_ref[...], preferred_element_type=F32)
    ot_ref[0] = jnp.dot(hb, wt_ref[...], preferred_element_type=F32)
    og_ref[0] = jnp.dot(hb, wg_ref[...], preferred_element_type=F32)


def _inproj(x, g, sc, sh, w5, wr, wt, wg, tm):
    b, l, d = x.shape
    widths = (w5.shape[1], wr.shape[1], wt.shape[1], wg.shape[1])
    tok = lambda n: pl.BlockSpec((1, tm, n), lambda i, j: (i, j, 0))
    vec = pl.BlockSpec((1, 1, d), lambda i, j: (i, 0, 0))
    full = lambda w: pl.BlockSpec(w.shape, lambda i, j: (0, 0))
    return pl.pallas_call(
        _inproj_kernel,
        grid=(b, l // tm),
        in_specs=[tok(d), pl.BlockSpec((1, d), lambda i, j: (0, 0)), vec, vec,
                  full(w5), full(wr), full(wt), full(wg)],
        out_specs=[tok(n) for n in widths],
        out_shape=[jax.ShapeDtypeStruct((b, l, n), F32) for n in widths],
        compiler_params=_cparams("parallel", "parallel"),
        name="norm_inproj",
    )(x, g, sc, sh, w5, wr, wt, wg)


def _s5_tables(lam_re, lam_im, log_dt, b_re, b_im, c_re, c_im, nlev):
    hp = lax.Precision.HIGHEST
    t = S5_T
    lam = lax.complex(jnp.minimum(lam_re.astype(F32), -1e-4), lam_im.astype(F32))
    ldt = lam * jnp.exp(log_dt.astype(F32))[..., None]
    a_bar = jnp.exp(ldt)
    bb = ((a_bar - 1.0) / lam)[..., None] * lax.complex(b_re.astype(F32), b_im.astype(F32))
    cm = lax.complex(c_re.astype(F32), c_im.astype(F32))
    tau = jnp.arange(t + 1, dtype=F32)
    apow = jnp.exp(ldt[:, :, None, :] * tau[None, None, :, None])
    taps = jnp.einsum('dgpn,dgtn,dgnq->dgtpq', cm, apow[:, :, :t], bb, precision=hp).real
    s_i = jnp.arange(t)[:, None]
    t_i = jnp.arange(t)[None, :]

    def toeplitz(k, lag, ok):
        m = k[:, jnp.clip(lag, 0, t - 1)]
        m = jnp.where(ok[None, :, :, None, None], m, 0.0)
        return m.transpose(0, 1, 4, 2, 3).reshape(S5_G, S5_TP, S5_TP)

    conv = jnp.stack([toeplitz(taps[0], t_i - s_i, t_i >= s_i),
                      toeplitz(taps[1], s_i - t_i, s_i >= t_i)])

    def pack(zc):
        return jnp.concatenate([zc.real, zc.imag], axis=-1)

    win_f = apow[0][:, t - 1 - jnp.arange(t), None, :] * bb[0].transpose(0, 2, 1)[:, None]
    win_b = apow[1][:, jnp.arange(t), None, :] * bb[1].transpose(0, 2, 1)[:, None]
    win = jnp.stack([pack(win_f), pack(win_b)]).reshape(2, S5_G, S5_TP, 2 * S5_N)
    ca_f = cm[0][:, None] * apow[0][:, 1 + jnp.arange(t), None, :]
    ca_b = cm[1][:, None] * apow[1][:, t - jnp.arange(t), None, :]

    def outpack(ca):
        w = jnp.concatenate([ca.real, -ca.imag], axis=-1)
        return w.reshape(S5_G, S5_TP, 2 * S5_N).transpose(0, 2, 1)

    wout = jnp.stack([outpack(ca_f), outpack(ca_b)])
    lev = (2.0 ** jnp.arange(nlev, dtype=F32)) * t
    pw = jnp.exp(ldt[:, :, None, :] * lev[None, None, :, None])
    p1 = jnp.concatenate([pw.real, pw.real], axis=-1)
    p2 = jnp.concatenate([-pw.imag, pw.imag], axis=-1)
    pw = jnp.stack([p1, p2], axis=3)
    return conv.astype(BF16), win.astype(BF16), wout.astype(BF16), pw


def _s5_kernel(nc, nlev, u_ref, conv_ref, win_ref, wout_ref, pw_ref, h0_ref, y_ref, hfin_ref):
    u = u_ref[0, 0].astype(BF16)
    row = lax.broadcasted_iota(jnp.int32, (nc, 2 * S5_N), 0)

    def cmul(x, d, j):
        return pw_ref[d, 0, j, 0:1] * x + pw_ref[d, 0, j, 1:2] * pltpu.roll(x, S5_N, 1)

    y = None
    for d in range(2):
        v = jnp.dot(u, win_ref[d, 0], preferred_element_type=F32)
        h0 = h0_ref[0, d, 0]
        if d == 0:
            x = jnp.where(row == 0, h0, pltpu.roll(v, 1, 0))
        else:
            x = jnp.where(row == nc - 1, h0, pltpu.roll(v, nc - 1, 0))
        for j in range(nlev):
            sh = 2 ** j
            if d == 0:
                xs = jnp.where(row >= sh, pltpu.roll(x, sh, 0), 0.0)
            else:
                xs = jnp.where(row < nc - sh, pltpu.roll(x, nc - sh, 0), 0.0)
            x = x + cmul(xs, d, j)
        last = nc - 1 if d == 0 else 0
        hfin_ref[0, d, 0] = cmul(x[last:last + 1], d, 0) + v[last:last + 1]
        yd = (jnp.dot(u, conv_ref[d, 0], preferred_element_type=F32)
              + _mm(x, wout_ref[d, 0]))
        y = yd if y is None else y + yd
    y_ref[0, 0] = y


def _s5_scan(u, tables, h0):
    conv, win, wout, pw = tables
    b, l, _ = u.shape
    nc = l // S5_T
    nlev = max(1, (nc - 1).bit_length())
    pw = pw[:, :, :nlev]
    uf = u.reshape(b, nc, S5_T, S5_G, S5_P).transpose(0, 3, 1, 2, 4).reshape(b, S5_G, nc, S5_TP)
    n2 = 2 * S5_N
    y, hfin = pl.pallas_call(
        functools.partial(_s5_kernel, nc, nlev),
        grid=(b, S5_G),
        in_specs=[pl.BlockSpec((1, 1, nc, S5_TP), lambda i, g: (i, g, 0, 0)),
                  pl.BlockSpec((2, 1, S5_TP, S5_TP), lambda i, g: (0, g, 0, 0)),
                  pl.BlockSpec((2, 1, S5_TP, n2), lambda i, g: (0, g, 0, 0)),
                  pl.BlockSpec((2, 1, n2, S5_TP), lambda i, g: (0, g, 0, 0)),
                  pl.BlockSpec((2, 1, nlev, 2, n2), lambda i, g: (0, g, 0, 0, 0)),
                  pl.BlockSpec((1, 2, 1, 1, n2), lambda i, g: (i, 0, g, 0, 0))],
        out_specs=[pl.BlockSpec((1, 1, nc, S5_TP), lambda i, g: (i, g, 0, 0)),
                   pl.BlockSpec((1, 2, 1, 1, n2), lambda i, g: (i, 0, g, 0, 0))],
        out_shape=[jax.ShapeDtypeStruct((b, S5_G, nc, S5_TP), F32),
                   jax.ShapeDtypeStruct((b, 2, S5_G, 1, n2), F32)],
        compiler_params=_cparams("parallel", "parallel"),
        name="s5_scan",
    )(uf, conv, win, wout, pw, h0)
    y = y.reshape(b, S5_G, nc, S5_T, S5_P).transpose(0, 2, 3, 1, 4).reshape(b, l, GROUP_W)
    return y, hfin


def _rw_prep_kernel(grid_shift, nt, z_ref, zp_ref, zn_ref, mu_ref, w0_ref, wup_ref, a0_ref, aup_ref,
                    gup_ref, kk_ref, ka_ref, rk_ref, bones_ref,
                    r_ref, v_ref, kkn_ref, g_ref, bonus_ref,
                    lw0_ref, kd0_ref, as0_ref, lw1_ref, kd1_ref, as1_ref):
    i = pl.program_id(1)
    z = z_ref[0]
    tm = z.shape[0]
    row = lax.broadcasted_iota(jnp.int32, z.shape, 0)
    lane = lax.broadcasted_iota(jnp.int32, z.shape, 1)
    prev1 = pltpu.roll(z, 1, 0)
    next1 = pltpu.roll(z, tm - 1, 0)
    if grid_shift:
        col = row & (GRID_W - 1)
        left = jnp.where(col == 0, 0.0, prev1)
        right = jnp.where(col == GRID_W - 1, 0.0, next1)
        zp = jnp.where(i > 0, zp_ref[0], 0.0)
        zn = jnp.where(i < nt - 1, zn_ref[0], 0.0)
        up = jnp.concatenate([zp, z[:tm - GRID_W]], axis=0)
        down = jnp.concatenate([z[GRID_W:], zn], axis=0)
        q = RW_COLS // 4
        shifted = jnp.where(lane < q, left,
                            jnp.where(lane < 2 * q, right, jnp.where(lane < 3 * q, up, down)))
    else:
        prev = jnp.where(row == 0, 0.0, prev1)
        nxt = jnp.where(row == tm - 1, 0.0, next1)
        shifted = jnp.where(lane < RW_COLS // 2, prev, nxt)
    zm = z + mu_ref[...] * (shifted - z)
    r = zm[:, 0:256]
    k = zm[:, 256:512]
    v = zm[:, 512:768]
    lo = zm[:, 768:896]
    bones = bones_ref[...]
    g_ref[0] = _mm(_sigmoid(lo), gup_ref[...])
    kk = k * kk_ref[...]
    kk = kk * lax.rsqrt(_mm_right01(kk * kk, bones) + 1e-12)
    r_ref[0] = r
    v_ref[0] = v
    kkn_ref[0] = kk
    bonus_ref[0] = _mm_right01(r * k * rk_ref[...], bones) * v
    tlo = jnp.tanh(lo)
    for d, (lw_ref, kd_ref, as_ref) in enumerate(((lw0_ref, kd0_ref, as0_ref), (lw1_ref, kd1_ref, as1_ref))):
        w_raw = -_softplus(-(w0_ref[d] + _mm(tlo, wup_ref[d]))) - 0.5
        lw_ref[0] = -jnp.exp(w_raw)
        a = _sigmoid(a0_ref[d] + _mm(lo, aup_ref[d]))
        kd_ref[0] = k * (1.0 + (a - 1.0) * ka_ref[...])
        as_ref[0] = a


def _rw_prep(z, grid_shift, p):
    b, l, _ = z.shape
    tm = 256 if grid_shift else l
    nt = l // tm
    hb = tm // GRID_W
    nhb = l // GRID_W
    tok = pl.BlockSpec((1, tm, GROUP_W), lambda i, j: (i, j, 0))
    full = lambda a: pl.BlockSpec(a.shape, lambda i, j: (0,) * a.ndim)
    params = (p['mu'], p['w0'], p['wup'], p['a0'], p['aup'], p['gup'], p['kk'], p['ka'], p['rk'], p['bones'])
    return pl.pallas_call(
        functools.partial(_rw_prep_kernel, grid_shift, nt),
        grid=(b, nt),
        in_specs=[pl.BlockSpec((1, tm, RW_COLS), lambda i, j: (i, j, 0)),
                  pl.BlockSpec((1, GRID_W, RW_COLS), lambda i, j: (i, jnp.maximum(j * hb - 1, 0), 0)),
                  pl.BlockSpec((1, GRID_W, RW_COLS), lambda i, j: (i, jnp.minimum((j + 1) * hb, nhb - 1), 0))]
                 + [full(a) for a in params],
        out_specs=[tok] * 11,
        out_shape=[jax.ShapeDtypeStruct((b, l, GROUP_W), F32)] * 11,
        compiler_params=_cparams("parallel", "parallel"),
        name="rwkv_prep",
    )(z, z, z, *params)


def _rw_scan_kernel(nb, cpb, rf_ref, vf_ref, kkf_ref, lwf_ref, kdf_ref, asf_ref,
                    rb_ref, vb_ref, kkb_ref, lwb_ref, kdb_ref, asb_ref, s0_ref,
                    yf_ref, yb_ref, sfin_ref, st_ref):
    c = pl.program_id(1)

    @pl.when(c == 0)
    def _():
        st_ref[...] = s0_ref[0]

    t = CHUNK
    ri = lax.broadcasted_iota(jnp.int32, (t, t), 0)
    ci = lax.broadcasted_iota(jnp.int32, (t, t), 1)
    eye = ri == ci
    eyef = jnp.where(eye, 1.0, 0.0)
    same = lambda s: jnp.where((ri >> s) == (ci >> s), 1.0, 0.0)
    m8, m16, m32 = same(3), same(4), same(5)
    merge_masks = (m16 - m8, m32 - m16, 1.0 - m32)
    dir_refs = ((rf_ref, vf_ref, kkf_ref, lwf_ref, kdf_ref, asf_ref),
                (rb_ref, vb_ref, kkb_ref, lwb_ref, kdb_ref, asb_ref))
    units = []
    for d, (r_ref, v_ref, kk_ref, lw_ref, kd_ref, as_ref) in enumerate(dir_refs):
        incl = (ci >= ri) if d == 1 else (ci <= ri)
        strict = (ci > ri) if d == 1 else (ci < ri)
        r, v, kk, lw, kd = r_ref[0], v_ref[0], kk_ref[0], lw_ref[0], kd_ref[0]
        cin = _mm_left01(_chunk_tri01(cpb * t, d == 1), lw)
        e_in = jnp.exp(cin)
        e_neg = jnp.exp(-cin)
        rt = r * e_in
        at = -kk * jnp.exp(cin - lw)
        bvec = kk * as_ref[0]
        bt = bvec * e_neg
        kt = kd * e_neg
        for j in range(cpb):
            rows = slice(j * t, (j + 1) * t)
            last = j * t + (0 if d == 1 else t - 1)
            clast = cin[last:last + 1]
            dl = jnp.exp(clast - cin[rows])
            bh = bvec[rows] * dl
            kh = kd[rows] * dl
            dec = jnp.exp(clast)
            for h in range(HEADS):
                sl = slice(h * HEAD_V, (h + 1) * HEAD_V)
                units.append(dict(d=d, j=j, h=h, incl=incl, strict=strict,
                                  at=at[rows, sl], rt=rt[rows, sl], bt=bt[rows, sl], kt=kt[rows, sl],
                                  bh=bh[:, sl], kh=kh[:, sl], v=v[rows, sl], dec=dec[:, sl]))

    a_all = [_mm_nt(jnp.concatenate([u['at'], u['rt']], axis=0),
                    jnp.concatenate([u['bt'], u['kt']], axis=0)) for u in units]
    nmat = [jnp.where(u['strict'], a[0:t, 0:t], 0.0) for u, a in zip(units, a_all)]
    a_kk = [jnp.concatenate([jnp.where(u['strict'], a[0:t, t:2 * t], 0.0),
                             jnp.where(u['incl'], a[t:2 * t, t:2 * t], 0.0)], axis=0)
            for u, a in zip(units, a_all)]
    a_rb = [jnp.where(u['incl'], a[t:2 * t, 0:t], 0.0) for u, a in zip(units, a_all)]
    akv = [_mm(a, u['v']) for u, a in zip(units, a_kk)]
    kv = [_mm_tn(u['kh'], u['v']) for u in units]
    nd = [_split2(x * m8) for x in nmat]
    ind = [_split2(eyef + x * m8) for x in nmat]
    n2 = [_split2(_mm_x3p(x, x)) for x in nd]
    n4 = [_split2(_mm_x3p(x, x)) for x in n2]
    p1 = [_split2(_join2(a) + _mm_x3p(a, b)) for a, b in zip(ind, n2)]
    tinv = [_join2(a) + _mm_x3p(a, b) for a, b in zip(p1, n4)]
    for mk in merge_masks:
        w = [_mm(ti, x * mk) for ti, x in zip(tinv, nmat)]
        tinv = [ti + _mm(wi, ti) for ti, wi in zip(tinv, w)]
    zz = [_mm(ti, jnp.concatenate([u['at'], kvv[0:t]], axis=1)) for ti, u, kvv in zip(tinv, units, akv)]
    gh = [_mm_tn(u['bh'], z) for u, z in zip(units, zz)]
    qy = [_mm(a, z) for a, z in zip(a_rb, zz)]
    qmat = [u['rt'] + x[:, :HEAD_V] for u, x in zip(units, qy)]
    y0 = [x[:, HEAD_V:] + kvv[t:2 * t] for x, kvv in zip(qy, akv)]
    gmat = [x[:, :HEAD_V] for x in gh]
    hmat = [x[:, HEAD_V:] + k2 for x, k2 in zip(gh, kv)]
    dcol = [_row_to_col(u['dec'], eye) for u in units]
    idx = {(u['d'], u['j'], u['h']): n for n, u in enumerate(units)}
    state = {(d, h): st_ref[d, h] for d in range(2) for h in range(HEADS)}
    ys = {}
    for step in range(cpb):
        for d in range(2):
            j = step if d == 0 else cpb - 1 - step
            for h in range(HEADS):
                n = idx[(d, j, h)]
                st = state[(d, h)]
                ys[(d, j, h)] = _mm(qmat[n], st) + y0[n]
                state[(d, h)] = dcol[n] * st + _mm(gmat[n], st) + hmat[n]
    for d, y_ref in enumerate((yf_ref, yb_ref)):
        y_ref[0] = jnp.concatenate(
            [jnp.concatenate([ys[(d, j, h)] for h in range(HEADS)], axis=1) for j in range(cpb)], axis=0)
        for h in range(HEADS):
            st_ref[d, h] = state[(d, h)]

    @pl.when(c == nb - 1)
    def _():
        sfin_ref[0] = st_ref[...]


def _rw_scan(pre, s0):
    r, v, kk, _, _, lw0, kd0, as0, lw1, kd1, as1 = pre
    b, l, _ = r.shape
    cpb = RW_CPB
    nb = l // (cpb * CHUNK)
    fw = pl.BlockSpec((1, cpb * CHUNK, GROUP_W), lambda i, c: (i, c, 0))
    bw = pl.BlockSpec((1, cpb * CHUNK, GROUP_W), lambda i, c: (i, nb - 1 - c, 0))
    st = pl.BlockSpec((1, 2, HEADS, HEAD_V, HEAD_V), lambda i, c: (i, 0, 0, 0, 0))
    return pl.pallas_call(
        functools.partial(_rw_scan_kernel, nb, cpb),
        grid=(b, nb),
        in_specs=[fw] * 6 + [bw] * 6 + [st],
        out_specs=[fw, bw, st],
        out_shape=[jax.ShapeDtypeStruct((b, l, GROUP_W), F32)] * 2
                  + [jax.ShapeDtypeStruct((b, 2, HEADS, HEAD_V, HEAD_V), F32)],
        scratch_shapes=[pltpu.VMEM((2, HEADS, HEAD_V, HEAD_V), F32)],
        compiler_params=_cparams("parallel", "arbitrary"),
        name="rwkv_scan",
    )(r, v, kk, lw0, kd0, as0, r, v, kk, lw1, kd1, as1, s0)


def _rope(x, cos, sin_signed):
    lane = lax.broadcasted_iota(jnp.int32, x.shape, 1)
    n = x.shape[1]
    swapped = jnp.where((lane & 16) == 0, pltpu.roll(x, n - 16, 1), pltpu.roll(x, 16, 1))
    return x * cos + swapped * sin_signed


def _la_scan_kernel(nb, cpb, dk, mode, *refs):
    if mode == 'ret':
        (qf_ref, kf_ref, vf_ref, qb_ref, kb_ref, vb_ref, lg_ref,
         crf_ref, srf_ref, crb_ref, srb_ref, cc_ref, sc_ref, s0_ref,
         of_ref, ob_ref, sfin_ref, st_ref) = refs
    else:
        (qf_ref, kf_ref, vf_ref, af_ref, qb_ref, kb_ref, vb_ref, ab_ref, aup_ref, abias_ref, s0_ref,
         of_ref, ob_ref, sfin_ref, st_ref) = refs
    c = pl.program_id(1)

    @pl.when(c == 0)
    def _():
        st_ref[...] = s0_ref[0]

    t = CHUNK
    n = cpb * t
    hk = HEADS * dk
    scale = dk ** -0.5
    eye = _tri_masks(dk, False)[2]
    dirs = ((qf_ref, kf_ref, vf_ref), (qb_ref, kb_ref, vb_ref))
    units = []
    for d, (q_ref, k_ref, v_ref) in enumerate(dirs):
        incl = _tri_masks(t, d == 1)[0]
        q, k, v = q_ref[0], k_ref[0] * scale, v_ref[0]
        if mode == 'ret':
            lw = jnp.broadcast_to(lg_ref[d], (n, hk))
            cr_ref, sr_ref = (crf_ref, srf_ref) if d == 0 else (crb_ref, srb_ref)
            lane = lax.broadcasted_iota(jnp.int32, (t, hk), 1)
            by_row = (lane & 32) == 0
            cos = jnp.concatenate([jnp.where(by_row, cr_ref[j], cc_ref[...]) for j in range(cpb)], axis=0)
            sin = jnp.concatenate([jnp.where(by_row, sr_ref[j], sc_ref[...]) for j in range(cpb)], axis=0)
            q = _rope(q, cos, sin)
            k = _rope(k, cos, sin)
        else:
            a_ref = af_ref if d == 0 else ab_ref
            lw = -_softplus(-(_mm(a_ref[0], aup_ref[d]) + abias_ref[d])) * (1.0 / GLA_TAU)
        cin = _mm_left01(_chunk_tri01(n, d == 1), lw)
        qt = q * jnp.exp(cin)
        kt = k * jnp.exp(-cin)
        for j in range(cpb):
            rows = slice(j * t, (j + 1) * t)
            last = j * t + (0 if d == 1 else t - 1)
            clast = cin[last:last + 1]
            kh = k[rows] * jnp.exp(clast - cin[rows])
            dec = jnp.exp(clast)
            for h in range(HEADS):
                sk = slice(h * dk, (h + 1) * dk)
                sv = slice(h * HEAD_V, (h + 1) * HEAD_V)
                units.append(dict(d=d, j=j, h=h, incl=incl, qt=qt[rows, sk], kt=kt[rows, sk],
                                  kh=kh[:, sk], v=v[rows, sv], dec=dec[:, sk]))

    amat = [jnp.where(u['incl'], _mm_nt(u['qt'], u['kt']), 0.0) for u in units]
    intra = [_mm(a, u['v']) for a, u in zip(amat, units)]
    kv = [_mm_tn(u['kh'], u['v']) for u in units]
    dcol = [_row_to_col(u['dec'], eye) for u in units]
    idx = {(u['d'], u['j'], u['h']): i for i, u in enumerate(units)}
    entering = {}
    for d in range(2):
        for h in range(HEADS):
            st = st_ref[d, h]
            for step in range(cpb):
                j = step if d == 0 else cpb - 1 - step
                i = idx[(d, j, h)]
                entering[i] = st
                st = dcol[i] * st + kv[i]
            st_ref[d, h] = st
    outs = [x + _mm(u['qt'], entering[i]) for i, (x, u) in enumerate(zip(intra, units))]
    for d, o_ref in enumerate((of_ref, ob_ref)):
        o_ref[0] = jnp.concatenate(
            [jnp.concatenate([outs[idx[(d, j, h)]] for h in range(HEADS)], axis=1) for j in range(cpb)], axis=0)

    @pl.when(c == nb - 1)
    def _():
        sfin_ref[0] = st_ref[...]


def _rope_tables(rows):
    nf = RET_DK // 4
    inv = ROPE_BASE ** (-jnp.arange(nf, dtype=F32) / nf)
    lane = jnp.arange(GROUP_W)
    freq = inv[lane % nf]
    sign = jnp.where((lane & 16) == 0, -1.0, 1.0)
    ar = jnp.arange(rows, dtype=F32)[:, None] * freq[None, :]
    ac = jnp.arange(GRID_W, dtype=F32)[:, None] * freq[None, :]
    return (jnp.cos(ar).reshape(rows, 1, GROUP_W), (jnp.sin(ar) * sign).reshape(rows, 1, GROUP_W),
            jnp.cos(ac), jnp.sin(ac) * sign)


def _ret_scan(z, lg, rope, s0):
    b, l, _ = z.shape
    cpb = LA_CPB
    nb = l // (cpb * CHUNK)
    cr, sr, cc, sc = rope
    fw = lambda j: pl.BlockSpec((1, cpb * CHUNK, GROUP_W), lambda i, c: (i, c, j))
    bw = lambda j: pl.BlockSpec((1, cpb * CHUNK, GROUP_W), lambda i, c: (i, nb - 1 - c, j))
    rowf = pl.BlockSpec((cpb, 1, GROUP_W), lambda i, c: (c, 0, 0))
    rowb = pl.BlockSpec((cpb, 1, GROUP_W), lambda i, c: (nb - 1 - c, 0, 0))
    colt = pl.BlockSpec((GRID_W, GROUP_W), lambda i, c: (0, 0))
    st = pl.BlockSpec((1, 2, HEADS, RET_DK, HEAD_V), lambda i, c: (i, 0, 0, 0, 0))
    return pl.pallas_call(
        functools.partial(_la_scan_kernel, nb, cpb, RET_DK, 'ret'),
        grid=(b, nb),
        in_specs=[fw(0), fw(1), fw(2), bw(0), bw(1), bw(2),
                  pl.BlockSpec((2, 1, GROUP_W), lambda i, c: (0, 0, 0)),
                  rowf, rowf, rowb, rowb, colt, colt, st],
        out_specs=[fw(0), bw(0), st],
        out_shape=[jax.ShapeDtypeStruct((b, l, GROUP_W), F32)] * 2
                  + [jax.ShapeDtypeStruct((b, 2, HEADS, RET_DK, HEAD_V), F32)],
        scratch_shapes=[pltpu.VMEM((2, HEADS, RET_DK, HEAD_V), F32)],
        compiler_params=_cparams("parallel", "arbitrary"),
        name="retention_scan",
    )(z, z, z, z, z, z, lg, cr, sr, cr, sr, cc, sc, s0)


def _gla_scan(z, aup, abias, s0):
    b, l, _ = z.shape
    cpb = LA_CPB
    nb = l // (cpb * CHUNK)
    blk = lambda w, j, rev: pl.BlockSpec(
        (1, cpb * CHUNK, w), (lambda i, c: (i, nb - 1 - c, j)) if rev else (lambda i, c: (i, c, j)))
    st = pl.BlockSpec((1, 2, HEADS, GLA_DK, HEAD_V), lambda i, c: (i, 0, 0, 0, 0))
    ofw = pl.BlockSpec((1, cpb * CHUNK, GROUP_W), lambda i, c: (i, c, 0))
    obw = pl.BlockSpec((1, cpb * CHUNK, GROUP_W), lambda i, c: (i, nb - 1 - c, 0))
    return pl.pallas_call(
        functools.partial(_la_scan_kernel, nb, cpb, GLA_DK, 'gla'),
        grid=(b, nb),
        in_specs=[blk(GLA_QK, 0, False), blk(GLA_QK, 1, False), blk(GROUP_W, 1, False), blk(128, 6, False),
                  blk(GLA_QK, 0, True), blk(GLA_QK, 1, True), blk(GROUP_W, 1, True), blk(128, 6, True),
                  pl.BlockSpec(aup.shape, lambda i, c: (0, 0, 0)),
                  pl.BlockSpec(abias.shape, lambda i, c: (0, 0, 0)), st],
        out_specs=[ofw, obw, st],
        out_shape=[jax.ShapeDtypeStruct((b, l, GROUP_W), F32)] * 2
                  + [jax.ShapeDtypeStruct((b, 2, HEADS, GLA_DK, HEAD_V), F32)],
        scratch_shapes=[pltpu.VMEM((2, HEADS, GLA_DK, HEAD_V), F32)],
        compiler_params=_cparams("parallel", "arbitrary"),
        name="gla_scan",
    )(z, z, z, z, z, z, z, z, aup, abias, s0)


def _mix_kernel(x_ref, g1_ref, y5_ref, u5_ref, d5_ref, gw_ref, gb_ref,
                ryf_ref, ryb_ref, rbonus_ref, rg_ref, rlng_ref, rlnb_ref,
                tof_ref, tob_ref, tg_ref, tln_ref,
                gof_ref, gob_ref, gg_ref, gln_ref,
                bones_ref, wo_ref, o_ref):
    bones = bones_ref[...]
    inv = 1.0 / HEAD_V

    def hmean(a):
        return _mm_right01(a, bones) * inv

    y = y5_ref[0] + d5_ref[...] * u5_ref[0]
    y = _gelu_tanh(y)
    ya = y * _sigmoid(_mm(y, gw_ref[...]) + gb_ref[...])
    yr = ryf_ref[0] + ryb_ref[0]
    dlt = yr - hmean(yr)
    yn = dlt * lax.rsqrt(hmean(dlt * dlt) + GN_EPS)
    yb = (yn * rlng_ref[...] + rlnb_ref[...] + rbonus_ref[0]) * rg_ref[0]
    ot = tof_ref[0] + tob_ref[0]
    yc = ot * lax.rsqrt(hmean(ot * ot) + EPS) * tln_ref[...] * _silu(tg_ref[0])
    og = gof_ref[0] + gob_ref[0]
    yd = og * lax.rsqrt(hmean(og * og) + EPS) * gln_ref[...] * _silu(gg_ref[0])
    mix = (_mm(ya, wo_ref[0]) + _mm(yb, wo_ref[1]) + _mm(yc, wo_ref[2]) + _mm(yd, wo_ref[3]))
    o_ref[0] = x_ref[0] + g1_ref[0] * mix


def _mix(x, g1, y5, u5, rw_pre, rw_y, zret, ret_o, zgla, gla_o, p, tm):
    b, l, d = x.shape
    tok = lambda w, j: pl.BlockSpec((1, tm, w), lambda i, t: (i, t, j))
    vec = lambda a: pl.BlockSpec(a.shape, lambda i, t: (0,) * a.ndim)
    t256 = tok(GROUP_W, 0)
    args = [x, g1, y5, u5, p['s5_d'], p['glu_w'], p['glu_b'],
            rw_y[0], rw_y[1], rw_pre[4], rw_pre[3], p['rw_ln_g'], p['rw_ln_b'],
            ret_o[0], ret_o[1], zret, p['ret_ln_g'],
            gla_o[0], gla_o[1], zgla, p['gla_ln_g'],
            p['bones'], p['w_out']]
    specs = [tok(d, 0), pl.BlockSpec((1, 1, d), lambda i, t: (i, 0, 0)), t256, t256,
             vec(p['s5_d']), vec(p['glu_w']), vec(p['glu_b']),
             t256, t256, t256, t256, vec(p['rw_ln_g']), vec(p['rw_ln_b']),
             t256, t256, tok(GROUP_W, 3), vec(p['ret_ln_g']),
             t256, t256, tok(GROUP_W, 2), vec(p['gla_ln_g']),
             vec(p['bones']), vec(p['w_out'])]
    return pl.pallas_call(
        _mix_kernel,
        grid=(b, l // tm),
        in_specs=specs,
        out_specs=tok(d, 0),
        out_shape=jax.ShapeDtypeStruct((b, l, d), F32),
        compiler_params=_cparams("parallel", "parallel"),
        name="mix_outproj",
    )(*args)


def _mlp_kernel(final, nff, x_ref, g_ref, sc_ref, sh_ref, gate_ref, w1_ref, w2_ref, fg_ref, o_ref):
    x = x_ref[0]
    hb = _modnorm(x, g_ref[...], sc_ref[0], sh_ref[0]).astype(BF16)
    ff = w1_ref.shape[1] // nff
    acc = None
    for j in range(nff):
        a = jnp.maximum(jnp.dot(hb, w1_ref[:, j * ff:(j + 1) * ff], preferred_element_type=F32), 0.0)
        part = jnp.dot((a * a).astype(BF16), w2_ref[j * ff:(j + 1) * ff, :], preferred_element_type=F32)
        acc = part if acc is None else acc + part
    y = x + gate_ref[0] * acc
    if final:
        ms = jnp.mean(y * y, axis=-1, keepdims=True)
        y = y * lax.rsqrt(ms + EPS) * fg_ref[...]
    o_ref[0] = y


def _mlp(x, g, sc, sh, gate, w1, w2, final_g, final, tm):
    b, l, d = x.shape
    tok = pl.BlockSpec((1, tm, d), lambda i, t: (i, t, 0))
    vec = pl.BlockSpec((1, 1, d), lambda i, t: (i, 0, 0))
    row = pl.BlockSpec((1, d), lambda i, t: (0, 0))
    once = lambda a: pl.BlockSpec(a.shape, lambda i, t: (0, 0), pipeline_mode=pl.Buffered(1))
    return pl.pallas_call(
        functools.partial(_mlp_kernel, final, 4),
        grid=(b, l // tm),
        in_specs=[tok, row, vec, vec, vec, once(w1), once(w2), row],
        out_specs=tok,
        out_shape=jax.ShapeDtypeStruct((b, l, d), F32),
        compiler_params=_cparams("parallel", "parallel"),
        name="mlp",
    )(x, g, sc, sh, gate, w1, w2, final_g)


def kernel(x, c, ctx, c_ctx, ada_w, ada_b, norm1_g, norm2_g, w_in, w_out, s5_lam_re, s5_lam_im, s5_log_dt, s5_b_re, s5_b_im, s5_c_re, s5_c_im, s5_d, s5_glu_w, s5_glu_b, rw_mu, rw_w0, rw_w_up, rw_a0, rw_a_up, rw_g_up, rw_k_k, rw_k_a, rw_r_k, rw_ln_g, rw_ln_b, ret_decay_logit, ret_ln_g, gla_a_up, gla_a_b, gla_ln_g, mlp_w1, mlp_w2, final_g):
    b, l, d = x.shape
    lc = ctx.shape[1]
    depth = ada_w.shape[0]
    rows = l // GRID_W
    assert l % 512 == 0 and lc % CHUNK == 0 and lc % S5_T == 0 and d % 128 == 0

    cond = jnp.zeros((8, d), F32).at[:b].set(c).at[b].set(c_ctx)
    mod = _modulation(cond, ada_w, ada_b)

    lane = jnp.arange(GROUP_W)
    bones = (lane[:, None] // HEAD_V == lane[None, :] // HEAD_V).astype(BF16)
    rope_lat = _rope_tables(rows)
    ident = (jnp.ones((lc // CHUNK, 1, GROUP_W), F32), jnp.zeros((lc // CHUNK, 1, GROUP_W), F32),
             jnp.ones((GRID_W, GROUP_W), F32), jnp.zeros((GRID_W, GROUP_W), F32))
    nlev = max(1, (l // S5_T - 1).bit_length())
    c0, c1, c2 = GROUP_W, GROUP_W + RW_COLS, GROUP_W + RW_COLS + 4 * GROUP_W
    row2 = lambda a: a.reshape(1, -1).astype(F32)

    xc = ctx
    for i in range(depth):
        last = i == depth - 1
        m = mod[i].reshape(8, N_ADA, d)
        ml = m[:b, :, None, :]
        mc = jnp.broadcast_to(m[b][None, :, None, :], (b, N_ADA, 1, d))
        wi = w_in[i].astype(BF16)
        w5, wr, wt = wi[:, :c0], wi[:, c0:c1], wi[:, c1:c2]
        wg = jnp.pad(wi[:, c2:], ((0, 0), (0, GLA_COLS_PAD - (wi.shape[1] - c2))))
        n1 = row2(norm1_g[i])
        n2 = row2(norm2_g[i])

        s5_tab = _s5_tables(s5_lam_re[i], s5_lam_im[i], s5_log_dt[i], s5_b_re[i], s5_b_im[i],
                            s5_c_re[i], s5_c_im[i], nlev)
        zpad = lambda a, lo, n: jnp.zeros((2, 128, n), F32).at[:, lo:lo + a.shape[1]].set(a).astype(BF16)
        rwp = dict(
            mu=row2(rw_mu[i]),
            w0=rw_w0[i].reshape(2, 1, GROUP_W), a0=rw_a0[i].reshape(2, 1, GROUP_W),
            wup=zpad(rw_w_up[i], 0, GROUP_W), aup=zpad(rw_a_up[i], RW_W_RANK, GROUP_W),
            gup=jnp.zeros((128, GROUP_W), F32).at[RW_W_RANK + RW_A_RANK:].set(rw_g_up[i]).astype(BF16),
            kk=row2(rw_k_k[i]), ka=row2(rw_k_a[i]), rk=row2(rw_r_k[i]), bones=bones)
        lg = jnp.repeat(jax.nn.log_sigmoid(ret_decay_logit[i].astype(F32)), RET_DK, axis=-1).reshape(2, 1, GROUP_W)
        gla_aup = zpad(gla_a_up[i], 0, GLA_QK)
        gla_ab = gla_a_b[i].reshape(2, 1, GLA_QK).astype(F32)
        mixp = dict(s5_d=row2(s5_d[i]), glu_w=s5_glu_w[i].astype(BF16), glu_b=row2(s5_glu_b[i]),
                    rw_ln_g=row2(rw_ln_g[i]), rw_ln_b=row2(rw_ln_b[i]), ret_ln_g=row2(ret_ln_g[i]),
                    gla_ln_g=row2(gla_ln_g[i]), bones=bones,
                    w_out=w_out[i].astype(BF16).reshape(4, GROUP_W, d))
        w1 = mlp_w1[i].astype(BF16)
        w2 = mlp_w2[i].astype(BF16)

        def mixers(xx, mm, is_lat, states):
            tm = 512 if is_lat else lc
            z5, zr, zt, zg = _inproj(xx, n1, mm[:, 1], mm[:, 0], w5, wr, wt, wg, tm)
            y5, h5 = _s5_scan(z5, s5_tab, states[0])
            pre = _rw_prep(zr, is_lat, rwp)
            yrf, yrb, srw = _rw_scan(pre, states[1])
            otf, otb, sret = _ret_scan(zt, lg, rope_lat if is_lat else ident, states[2])
            ogf, ogb, sgla = _gla_scan(zg, gla_aup, gla_ab, states[3])
            outs = (z5, y5, pre, (yrf, yrb), zt, (otf, otb), zg, (ogf, ogb))
            return outs, (h5, srw, sret, sgla)

        def block(xx, mm, outs, is_lat, fin):
            tm = 512 if is_lat else lc
            z5, y5, pre, yr, zt, ot, zg, og = outs
            x1 = _mix(xx, mm[:, 2], y5, z5, pre, yr, zt, ot, zg, og, mixp, tm)
            return _mlp(x1, n2, mm[:, 4], mm[:, 3], mm[:, 5], w1, w2, row2(final_g), fin, tm)

        zeros = (jnp.zeros((b, 2, S5_G, 1, 2 * S5_N), F32),
                 jnp.zeros((b, 2, HEADS, HEAD_V, HEAD_V), F32),
                 jnp.zeros((b, 2, HEADS, RET_DK, HEAD_V), F32),
                 jnp.zeros((b, 2, HEADS, GLA_DK, HEAD_V), F32))
        outs_c, st_c = mixers(xc, mc, False, zeros)
        outs_l, _ = mixers(x, ml, True, st_c)
        x = block(x, ml, outs_l, True, last)
        if not last:
            xc = block(xc, mc, outs_c, False, False)
    return x
```

```python
import functools
import math

import jax
import jax.numpy as jnp
from jax import lax
from jax.experimental import pallas as pl
from jax.experimental.pallas import tpu as pltpu

F32 = jnp.float32
BF16 = jnp.bfloat16

GRID_W = 64
GROUP_W = 256
N_ADA = 6
EPS = 1e-6
GN_EPS = 64e-5
CHUNK = 64
RW_CPB = 2
LA_CPB = 4
HEADS = 4
HEAD_V = 64
S5_P = 16
S5_G = 16
S5_N = 64
S5_T = 32
S5_TP = S5_T * S5_P
RW_COLS = 896
RW_W_RANK = 32
RW_A_RANK = 32
RW_G_RANK = 64
RET_DK = 64
GLA_DK = 32
GLA_QK = 128
GLA_RANK = 16
GLA_TAU = 16.0
GLA_COLS_PAD = 896
ROPE_BASE = 10000.0
VMEM_LIMIT = 56 * 1024 * 1024


def _cparams(*sem):
    return pltpu.CompilerParams(dimension_semantics=sem, vmem_limit_bytes=VMEM_LIMIT)


def _mm(a, b):
    return jnp.dot(a.astype(BF16), b.astype(BF16), preferred_element_type=F32)


def _mm_nt(a, b):
    return lax.dot_general(a.astype(BF16), b.astype(BF16), (((1,), (1,)), ((), ())),
                           preferred_element_type=F32)


def _mm_tn(a, b):
    return lax.dot_general(a.astype(BF16), b.astype(BF16), (((0,), (0,)), ((), ())),
                           preferred_element_type=F32)


def _split2(x):
    hi = x.astype(BF16)
    return hi, (x - hi.astype(F32)).astype(BF16)


def _join2(p):
    return p[0].astype(F32) + p[1].astype(F32)


def _mm_x3p(a, b):
    return (jnp.dot(a[0], b[0], preferred_element_type=F32) + jnp.dot(a[0], b[1], preferred_element_type=F32)
            + jnp.dot(a[1], b[0], preferred_element_type=F32))


def _chunk_tri01(n, reverse):
    ri = lax.broadcasted_iota(jnp.int32, (n, n), 0)
    ci = lax.broadcasted_iota(jnp.int32, (n, n), 1)
    shift = CHUNK.bit_length() - 1
    tri = (ci >= ri) if reverse else (ci <= ri)
    return jnp.where((ri >> shift) == (ci >> shift), jnp.where(tri, 1.0, 0.0), 0.0).astype(BF16)


def _split3(x):
    hi = x.astype(BF16)
    r1 = x - hi.astype(F32)
    mid = r1.astype(BF16)
    lo = (r1 - mid.astype(F32)).astype(BF16)
    return hi, mid, lo


def _mm_left01(m01, x):
    hi, mid, lo = _split3(x)
    return (jnp.dot(m01, hi, preferred_element_type=F32)
            + jnp.dot(m01, mid, preferred_element_type=F32)
            + jnp.dot(m01, lo, preferred_element_type=F32))


def _mm_right01(x, m01):
    hi, mid, lo = _split3(x)
    return (jnp.dot(hi, m01, preferred_element_type=F32)
            + jnp.dot(mid, m01, preferred_element_type=F32)
            + jnp.dot(lo, m01, preferred_element_type=F32))


def _sigmoid(x):
    return 1.0 / (1.0 + jnp.exp(-x))


def _softplus(x):
    return jnp.maximum(x, 0.0) + jnp.log(1.0 + jnp.exp(-jnp.abs(x)))


def _silu(x):
    return x * _sigmoid(x)


def _gelu_tanh(x):
    return 0.5 * x * (1.0 + jnp.tanh(math.sqrt(2.0 / math.pi) * (x + 0.044715 * x * x * x)))


def _tri_masks(t, reverse):
    ri = lax.broadcasted_iota(jnp.int32, (t, t), 0)
    ci = lax.broadcasted_iota(jnp.int32, (t, t), 1)
    if reverse:
        return ci >= ri, ci > ri, ci == ri
    return ci <= ri, ci < ri, ci == ri


def _row_to_col(row, eye):
    n = row.shape[1]
    return jnp.sum(jnp.where(eye, jnp.broadcast_to(row, (n, n)), 0.0), axis=1, keepdims=True)


def _mod_kernel(cond_ref, w_ref, b_ref, o_ref):
    c = cond_ref[...]
    o_ref[0] = _mm(_silu(c), w_ref[0]) + b_ref[0]


def _modulation(cond, ada_w, ada_b):
    depth, d, n = ada_w.shape
    tn = 1536
    return pl.pallas_call(
        _mod_kernel,
        grid=(depth, n // tn),
        in_specs=[pl.BlockSpec((8, d), lambda i, j: (0, 0)),
                  pl.BlockSpec((1, d, tn), lambda i, j: (i, 0, j)),
                  pl.BlockSpec((1, 1, tn), lambda i, j: (i, 0, j))],
        out_specs=pl.BlockSpec((1, 8, tn), lambda i, j: (i, 0, j)),
        out_shape=jax.ShapeDtypeStruct((depth, 8, n), F32),
        compiler_params=_cparams("parallel", "parallel"),
        name="adaln_mod",
    )(cond, ada_w, ada_b.reshape(depth, 1, n))


def _modnorm(x, g, sc, sh):
    ms = jnp.mean(x * x, axis=-1, keepdims=True)
    return x * lax.rsqrt(ms + EPS) * g * (1.0 + sc) + sh


def _inproj_kernel(x_ref, g_ref, sc_ref, sh_ref, w5_ref, wr_ref, wt_ref, wg_ref,
                   o5_ref, or_ref, ot_ref, og_ref):
    hb = _modnorm(x_ref[0], g_ref[...], sc_ref[0], sh_ref[0]).astype(BF16)
    o5_ref[0] = jnp.dot(hb, w5_ref[...], preferred_element_type=F32)
    or_ref[0] = jnp.dot(hb, wr_ref[...], preferred_element_type=F32)
    ot_ref[0] = jnp.dot(hb, wt_ref[...], preferred_element_type=F32)
    og_ref[0] = jnp.dot(hb, wg_ref[...], preferred_element_type=F32)


def _inproj(x, g, sc, sh, w5, wr, wt, wg, tm):
    b, l, d = x.shape
    widths = (w5.shape[1], wr.shape[1], wt.shape[1], wg.shape[1])
    tok = lambda n: pl.BlockSpec((1, tm, n), lambda i, j: (i, j, 0))
    vec = pl.BlockSpec((1, 1, d), lambda i, j: (i, 0, 0))
    full = lambda w: pl.BlockSpec(w.shape, lambda i, j: (0, 0))
    return pl.pallas_call(
        _inproj_kernel,
        grid=(b, l // tm),
        in_specs=[tok(d), pl.BlockSpec((1, d), lambda i, j: (0, 0)), vec, vec,
                  full(w5), full(wr), full(wt), full(wg)],
        out_specs=[tok(n) for n in widths],
        out_shape=[jax.ShapeDtypeStruct((b, l, n), F32) for n in widths],
        compiler_params=_cparams("parallel", "parallel"),
        name="norm_inproj",
    )(x, g, sc, sh, w5, wr, wt, wg)


def _s5_tables(lam_re, lam_im, log_dt, b_re, b_im, c_re, c_im, nlev):
    hp = lax.Precision.HIGHEST
    t = S5_T
    lam = lax.complex(jnp.minimum(lam_re.astype(F32), -1e-4), lam_im.astype(F32))
    ldt = lam * jnp.exp(log_dt.astype(F32))[..., None]
    a_bar = jnp.exp(ldt)
    bb = ((a_bar - 1.0) / lam)[..., None] * lax.complex(b_re.astype(F32), b_im.astype(F32))
    cm = lax.complex(c_re.astype(F32), c_im.astype(F32))
    tau = jnp.arange(t + 1, dtype=F32)
    apow = jnp.exp(ldt[:, :, None, :] * tau[None, None, :, None])
    taps = jnp.einsum('dgpn,dgtn,dgnq->dgtpq', cm, apow[:, :, :t], bb, precision=hp).real
    s_i = jnp.arange(t)[:, None]
    t_i = jnp.arange(t)[None, :]

    def toeplitz(k, lag, ok):
        m = k[:, jnp.clip(lag, 0, t - 1)]
        m = jnp.where(ok[None, :, :, None, None], m, 0.0)
        return m.transpose(0, 1, 4, 2, 3).reshape(S5_G, S5_TP, S5_TP)

    conv = jnp.stack([toeplitz(taps[0], t_i - s_i, t_i >= s_i),
                      toeplitz(taps[1], s_i - t_i, s_i >= t_i)])

    def pack(zc):
        return jnp.concatenate([zc.real, zc.imag], axis=-1)

    win_f = apow[0][:, t - 1 - jnp.arange(t), None, :] * bb[0].transpose(0, 2, 1)[:, None]
    win_b = apow[1][:, jnp.arange(t), None, :] * bb[1].transpose(0, 2, 1)[:, None]
    win = jnp.stack([pack(win_f), pack(win_b)]).reshape(2, S5_G, S5_TP, 2 * S5_N)
    ca_f = cm[0][:, None] * apow[0][:, 1 + jnp.arange(t), None, :]
    ca_b = cm[1][:, None] * apow[1][:, t - jnp.arange(t), None, :]

    def outpack(ca):
        w = jnp.concatenate([ca.real, -ca.imag], axis=-1)
        return w.reshape(S5_G, S5_TP, 2 * S5_N).transpose(0, 2, 1)

    wout = jnp.stack([outpack(ca_f), outpack(ca_b)])
    lev = (2.0 ** jnp.arange(nlev, dtype=F32)) * t
    pw = jnp.exp(ldt[:, :, None, :] * lev[None, None, :, None])
    p1 = jnp.concatenate([pw.real, pw.real], axis=-1)
    p2 = jnp.concatenate([-pw.imag, pw.imag], axis=-1)
    pw = jnp.stack([p1, p2], axis=3)
    return conv.astype(BF16), win.astype(BF16), wout.astype(BF16), pw


def _s5_kernel(nc, nlev, u_ref, conv_ref, win_ref, wout_ref, pw_ref, h0_ref, y_ref, hfin_ref):
    u = u_ref[0, 0].astype(BF16)
    row = lax.broadcasted_iota(jnp.int32, (nc, 2 * S5_N), 0)

    def cmul(x, d, j):
        return pw_ref[d, 0, j, 0:1] * x + pw_ref[d, 0, j, 1:2] * pltpu.roll(x, S5_N, 1)

    y = None
    for d in range(2):
        v = jnp.dot(u, win_ref[d, 0], preferred_element_type=F32)
        h0 = h0_ref[0, d, 0]
        if d == 0:
            x = jnp.where(row == 0, h0, pltpu.roll(v, 1, 0))
        else:
            x = jnp.where(row == nc - 1, h0, pltpu.roll(v, nc - 1, 0))
        for j in range(nlev):
            sh = 2 ** j
            if d == 0:
                xs = jnp.where(row >= sh, pltpu.roll(x, sh, 0), 0.0)
            else:
                xs = jnp.where(row < nc - sh, pltpu.roll(x, nc - sh, 0), 0.0)
            x = x + cmul(xs, d, j)
        last = nc - 1 if d == 0 else 0
        hfin_ref[0, d, 0] = cmul(x[last:last + 1], d, 0) + v[last:last + 1]
        yd = (jnp.dot(u, conv_ref[d, 0], preferred_element_type=F32)
              + _mm(x, wout_ref[d, 0]))
        y = yd if y is None else y + yd
    y_ref[0, 0] = y


def _s5_scan(u, tables, h0):
    conv, win, wout, pw = tables
    b, l, _ = u.shape
    nc = l // S5_T
    nlev = max(1, (nc - 1).bit_length())
    pw = pw[:, :, :nlev]
    uf = u.reshape(b, nc, S5_T, S5_G, S5_P).transpose(0, 3, 1, 2, 4).reshape(b, S5_G, nc, S5_TP)
    n2 = 2 * S5_N
    y, hfin = pl.pallas_call(
        functools.partial(_s5_kernel, nc, nlev),
        grid=(b, S5_G),
        in_specs=[pl.BlockSpec((1, 1, nc, S5_TP), lambda i, g: (i, g, 0, 0)),
                  pl.BlockSpec((2, 1, S5_TP, S5_TP), lambda i, g: (0, g, 0, 0)),
                  pl.BlockSpec((2, 1, S5_TP, n2), lambda i, g: (0, g, 0, 0)),
                  pl.BlockSpec((2, 1, n2, S5_TP), lambda i, g: (0, g, 0, 0)),
                  pl.BlockSpec((2, 1, nlev, 2, n2), lambda i, g: (0, g, 0, 0, 0)),
                  pl.BlockSpec((1, 2, 1, 1, n2), lambda i, g: (i, 0, g, 0, 0))],
        out_specs=[pl.BlockSpec((1, 1, nc, S5_TP), lambda i, g: (i, g, 0, 0)),
                   pl.BlockSpec((1, 2, 1, 1, n2), lambda i, g: (i, 0, g, 0, 0))],
        out_shape=[jax.ShapeDtypeStruct((b, S5_G, nc, S5_TP), F32),
                   jax.ShapeDtypeStruct((b, 2, S5_G, 1, n2), F32)],
        compiler_params=_cparams("parallel", "parallel"),
        name="s5_scan",
    )(uf, conv, win, wout, pw, h0)
    y = y.reshape(b, S5_G, nc, S5_T, S5_P).transpose(0, 2, 3, 1, 4).reshape(b, l, GROUP_W)
    return y, hfin


def _rw_prep_kernel(grid_shift, nt, z_ref, zp_ref, zn_ref, mu_ref, w0_ref, wup_ref, a0_ref, aup_ref,
                    gup_ref, kk_ref, ka_ref, rk_ref, bones_ref,
                    r_ref, v_ref, kkn_ref, g_ref, bonus_ref,
                    lw0_ref, kd0_ref, as0_ref, lw1_ref, kd1_ref, as1_ref):
    i = pl.program_id(1)
    z = z_ref[0]
    tm = z.shape[0]
    row = lax.broadcasted_iota(jnp.int32, z.shape, 0)
    lane = lax.broadcasted_iota(jnp.int32, z.shape, 1)
    prev1 = pltpu.roll(z, 1, 0)
    next1 = pltpu.roll(z, tm - 1, 0)
    if grid_shift:
        col = row & (GRID_W - 1)
        left = jnp.where(col == 0, 0.0, prev1)
        right = jnp.where(col == GRID_W - 1, 0.0, next1)
        zp = jnp.where(i > 0, zp_ref[0], 0.0)
        zn = jnp.where(i < nt - 1, zn_ref[0], 0.0)
        up = jnp.concatenate([zp, z[:tm - GRID_W]], axis=0)
        down = jnp.concatenate([z[GRID_W:], zn], axis=0)
        q = RW_COLS // 4
        shifted = jnp.where(lane < q, left,
                            jnp.where(lane < 2 * q, right, jnp.where(lane < 3 * q, up, down)))
    else:
        prev = jnp.where(row == 0, 0.0, prev1)
        nxt = jnp.where(row == tm - 1, 0.0, next1)
        shifted = jnp.where(lane < RW_COLS // 2, prev, nxt)
    zm = z + mu_ref[...] * (shifted - z)
    r = zm[:, 0:256]
    k = zm[:, 256:512]
    v = zm[:, 512:768]
    lo = zm[:, 768:896]
    bones = bones_ref[...]
    g_ref[0] = _mm(_sigmoid(lo), gup_ref[...])
    kk = k * kk_ref[...]
    kk = kk * lax.rsqrt(_mm_right01(kk * kk, bones) + 1e-12)
    r_ref[0] = r
    v_ref[0] = v
    kkn_ref[0] = kk
    bonus_ref[0] = _mm_right01(r * k * rk_ref[...], bones) * v
    tlo = jnp.tanh(lo)
    for d, (lw_ref, kd_ref, as_ref) in enumerate(((lw0_ref, kd0_ref, as0_ref), (lw1_ref, kd1_ref, as1_ref))):
        w_raw = -_softplus(-(w0_ref[d] + _mm(tlo, wup_ref[d]))) - 0.5
        lw_ref[0] = -jnp.exp(w_raw)
        a = _sigmoid(a0_ref[d] + _mm(lo, aup_ref[d]))
        kd_ref[0] = k * (1.0 + (a - 1.0) * ka_ref[...])
        as_ref[0] = a


def _rw_prep(z, grid_shift, p):
    b, l, _ = z.shape
    tm = 256 if grid_shift else l
    nt = l // tm
    hb = tm // GRID_W
    nhb = l // GRID_W
    tok = pl.BlockSpec((1, tm, GROUP_W), lambda i, j: (i, j, 0))
    full = lambda a: pl.BlockSpec(a.shape, lambda i, j: (0,) * a.ndim)
    params = (p['mu'], p['w0'], p['wup'], p['a0'], p['aup'], p['gup'], p['kk'], p['ka'], p['rk'], p['bones'])
    return pl.pallas_call(
        functools.partial(_rw_prep_kernel, grid_shift, nt),
        grid=(b, nt),
        in_specs=[pl.BlockSpec((1, tm, RW_COLS), lambda i, j: (i, j, 0)),
                  pl.BlockSpec((1, GRID_W, RW_COLS), lambda i, j: (i, jnp.maximum(j * hb - 1, 0), 0)),
                  pl.BlockSpec((1, GRID_W, RW_COLS), lambda i, j: (i, jnp.minimum((j + 1) * hb, nhb - 1), 0))]
                 + [full(a) for a in params],
        out_specs=[tok] * 11,
        out_shape=[jax.ShapeDtypeStruct((b, l, GROUP_W), F32)] * 11,
        compiler_params=_cparams("parallel", "parallel"),
        name="rwkv_prep",
    )(z, z, z, *params)


def _rw_scan_kernel(nb, cpb, rf_ref, vf_ref, kkf_ref, lwf_ref, kdf_ref, asf_ref,
                    rb_ref, vb_ref, kkb_ref, lwb_ref, kdb_ref, asb_ref, s0_ref,
                    yf_ref, yb_ref, sfin_ref, st_ref):
    c = pl.program_id(1)

    @pl.when(c == 0)
    def _():
        st_ref[...] = s0_ref[0]

    t = CHUNK
    ri = lax.broadcasted_iota(jnp.int32, (t, t), 0)
    ci = lax.broadcasted_iota(jnp.int32, (t, t), 1)
    eye = ri == ci
    eyef = jnp.where(eye, 1.0, 0.0)
    same = lambda s: jnp.where((ri >> s) == (ci >> s), 1.0, 0.0)
    m4, m8, m16, m32 = same(2), same(3), same(4), same(5)
    merge_masks = (m8 - m4, m16 - m8, m32 - m16, 1.0 - m32)
    dir_refs = ((rf_ref, vf_ref, kkf_ref, lwf_ref, kdf_ref, asf_ref),
                (rb_ref, vb_ref, kkb_ref, lwb_ref, kdb_ref, asb_ref))
    units = []
    for d, (r_ref, v_ref, kk_ref, lw_ref, kd_ref, as_ref) in enumerate(dir_refs):
        incl = (ci >= ri) if d == 1 else (ci <= ri)
        strict = (ci > ri) if d == 1 else (ci < ri)
        r, v, kk, lw, kd = r_ref[0], v_ref[0], kk_ref[0], lw_ref[0], kd_ref[0]
        cin = _mm_left01(_chunk_tri01(cpb * t, d == 1), lw)
        e_in = jnp.exp(cin)
        e_neg = jnp.exp(-cin)
        rt = r * e_in
        at = -kk * jnp.exp(cin - lw)
        bvec = kk * as_ref[0]
        bt = bvec * e_neg
        kt = kd * e_neg
        for j in range(cpb):
            rows = slice(j * t, (j + 1) * t)
            last = j * t + (0 if d == 1 else t - 1)
            clast = cin[last:last + 1]
            dl = jnp.exp(clast - cin[rows])
            bh = bvec[rows] * dl
            kh = kd[rows] * dl
            dec = jnp.exp(clast)
            for h in range(HEADS):
                sl = slice(h * HEAD_V, (h + 1) * HEAD_V)
                units.append(dict(d=d, j=j, h=h, incl=incl, strict=strict,
                                  at=at[rows, sl], rt=rt[rows, sl], bt=bt[rows, sl], kt=kt[rows, sl],
                                  bh=bh[:, sl], kh=kh[:, sl], v=v[rows, sl], dec=dec[:, sl]))

    a_all = [_mm_nt(jnp.concatenate([u['at'], u['rt']], axis=0),
                    jnp.concatenate([u['bt'], u['kt']], axis=0)) for u in units]
    nmat = [jnp.where(u['strict'], a[0:t, 0:t], 0.0) for u, a in zip(units, a_all)]
    a_kk = [jnp.concatenate([jnp.where(u['strict'], a[0:t, t:2 * t], 0.0),
                             jnp.where(u['incl'], a[t:2 * t, t:2 * t], 0.0)], axis=0)
            for u, a in zip(units, a_all)]
    a_rb = [jnp.where(u['incl'], a[t:2 * t, 0:t], 0.0) for u, a in zip(units, a_all)]
    akv = [_mm(a, u['v']) for u, a in zip(units, a_kk)]
    kv = [_mm_tn(u['v'], u['kh']) for u in units]
    nd = [x * m4 for x in nmat]
    n2 = [_mm(x, x) for x in nd]
    tinv = [eyef + x + _mm(eyef + x, y) for x, y in zip(nd, n2)]
    for mk in merge_masks:
        w = [_mm(ti, x * mk) for ti, x in zip(tinv, nmat)]
        tinv = [ti + _mm(wi, ti) for ti, wi in zip(tinv, w)]
    zz = [_mm(ti, jnp.concatenate([u['at'], kvv[0:t]], axis=1)) for ti, u, kvv in zip(tinv, units, akv)]
    ght = [_mm_tn(z, u['bh']) for u, z in zip(units, zz)]
    qy = [_mm(a, z) for a, z in zip(a_rb, zz)]
    qmat = [u['rt'] + x[:, :HEAD_V] for u, x in zip(units, qy)]
    y0 = [x[:, HEAD_V:] + kvv[t:2 * t] for x, kvv in zip(qy, akv)]
    gmat = [x[:HEAD_V] for x in ght]
    hmat = [x[HEAD_V:] + k2 for x, k2 in zip(ght, kv)]
    idx = {(u['d'], u['j'], u['h']): n for n, u in enumerate(units)}
    state = {(d, h): st_ref[d, h] for d in range(2) for h in range(HEADS)}
    ys = {}
    for step in range(cpb):
        for d in range(2):
            j = step if d == 0 else cpb - 1 - step
            for h in range(HEADS):
                n = idx[(d, j, h)]
                st = state[(d, h)]
                ys[(d, j, h)] = _mm_nt(qmat[n], st) + y0[n]
                state[(d, h)] = units[n]['dec'] * st + _mm(st, gmat[n]) + hmat[n]
    for d, y_ref in enumerate((yf_ref, yb_ref)):
        y_ref[0] = jnp.concatenate(
            [jnp.concatenate([ys[(d, j, h)] for h in range(HEADS)], axis=1) for j in range(cpb)], axis=0)
        for h in range(HEADS):
            st_ref[d, h] = state[(d, h)]

    @pl.when(c == nb - 1)
    def _():
        sfin_ref[0] = st_ref[...]


def _rw_scan(pre, s0):
    r, v, kk, _, _, lw0, kd0, as0, lw1, kd1, as1 = pre
    b, l, _ = r.shape
    cpb = RW_CPB
    nb = l // (cpb * CHUNK)
    fw = pl.BlockSpec((1, cpb * CHUNK, GROUP_W), lambda i, c: (i, c, 0))
    bw = pl.BlockSpec((1, cpb * CHUNK, GROUP_W), lambda i, c: (i, nb - 1 - c, 0))
    st = pl.BlockSpec((1, 2, HEADS, HEAD_V, HEAD_V), lambda i, c: (i, 0, 0, 0, 0))
    return pl.pallas_call(
        functools.partial(_rw_scan_kernel, nb, cpb),
        grid=(b, nb),
        in_specs=[fw] * 6 + [bw] * 6 + [st],
        out_specs=[fw, bw, st],
        out_shape=[jax.ShapeDtypeStruct((b, l, GROUP_W), F32)] * 2
                  + [jax.ShapeDtypeStruct((b, 2, HEADS, HEAD_V, HEAD_V), F32)],
        scratch_shapes=[pltpu.VMEM((2, HEADS, HEAD_V, HEAD_V), F32)],
        compiler_params=_cparams("parallel", "arbitrary"),
        name="rwkv_scan",
    )(r, v, kk, lw0, kd0, as0, r, v, kk, lw1, kd1, as1, s0)


def _rope(x, cos, sin_signed):
    lane = lax.broadcasted_iota(jnp.int32, x.shape, 1)
    n = x.shape[1]
    swapped = jnp.where((lane & 16) == 0, pltpu.roll(x, n - 16, 1), pltpu.roll(x, 16, 1))
    return x * cos + swapped * sin_signed


def _ret_scan_kernel(nb, cpb, qf_ref, kf_ref, vf_ref, qb_ref, kb_ref, vb_ref,
                     dmat_ref, qdec_ref, kdec_ref, sdec_ref,
                     crf_ref, srf_ref, crb_ref, srb_ref, cc_ref, sc_ref, s0_ref,
                     of_ref, ob_ref, sfin_ref, st_ref):
    c = pl.program_id(1)

    @pl.when(c == 0)
    def _():
        st_ref[...] = s0_ref[0]

    t = CHUNK
    scale = RET_DK ** -0.5
    lane = lax.broadcasted_iota(jnp.int32, (t, GROUP_W), 1)
    by_row = (lane & 32) == 0
    qs, ks, qds, khs, vs = [], [], [], [], []
    for d, (q_ref, k_ref, v_ref) in enumerate(((qf_ref, kf_ref, vf_ref), (qb_ref, kb_ref, vb_ref))):
        cr_ref, sr_ref = (crf_ref, srf_ref) if d == 0 else (crb_ref, srb_ref)
        cos = jnp.concatenate([jnp.where(by_row, cr_ref[j], cc_ref[...]) for j in range(cpb)], axis=0)
        sin = jnp.concatenate([jnp.where(by_row, sr_ref[j], sc_ref[...]) for j in range(cpb)], axis=0)
        q = _rope(q_ref[0], cos, sin)
        k = _rope(k_ref[0] * scale, cos, sin)
        v = v_ref[0]
        qd = q * qdec_ref[d]
        kh = k * kdec_ref[d]
        for h in range(HEADS):
            sl = slice(h * HEAD_V, (h + 1) * HEAD_V)
            qs.append(q[:, sl])
            ks.append(k[:, sl])
            qds.append(qd[:, sl])
            khs.append(kh[:, sl])
            vs.append(v[:, sl])
    scores = [_mm_nt(q, k) * dmat_ref[i // HEADS, i % HEADS] for i, (q, k) in enumerate(zip(qs, ks))]
    states = [st_ref[i // HEADS, i % HEADS] for i in range(2 * HEADS)]
    outs = [_mm(p, v) + _mm_nt(qd, st) for p, v, qd, st in zip(scores, vs, qds, states)]
    for i, (st, kh, v) in enumerate(zip(states, khs, vs)):
        d, h = i // HEADS, i % HEADS
        st_ref[d, h] = st * sdec_ref[d][:, h * HEAD_V:(h + 1) * HEAD_V] + _mm_tn(v, kh)
    of_ref[0] = jnp.concatenate(outs[:HEADS], axis=1)
    ob_ref[0] = jnp.concatenate(outs[HEADS:], axis=1)

    @pl.when(c == nb - 1)
    def _():
        sfin_ref[0] = st_ref[...]


def _ret_tables(decay_logit, n):
    lg = jax.nn.log_sigmoid(decay_logit.astype(F32))
    pos = jnp.arange(n, dtype=F32)
    lag = pos[:, None] - pos[None, :]
    lag = jnp.stack([lag, -lag])
    dmat = jnp.where(lag[:, None] >= 0, jnp.exp(lg[:, :, None, None] * lag[:, None]), 0.0)
    lanes = jnp.repeat(lg, RET_DK, axis=-1)[:, None, :]
    qpow = jnp.stack([pos + 1.0, n - pos])[:, :, None]
    kpow = jnp.stack([n - 1.0 - pos, pos])[:, :, None]
    return dmat, jnp.exp(lanes * qpow), jnp.exp(lanes * kpow), jnp.exp(lanes * n)


def _gla_scan_kernel(nb, cpb, qf_ref, kf_ref, vf_ref, af_ref, qb_ref, kb_ref, vb_ref, ab_ref,
                     aup_ref, abias_ref, s0_ref, of_ref, ob_ref, sfin_ref, st_ref):
    c = pl.program_id(1)

    @pl.when(c == 0)
    def _():
        st_ref[...] = s0_ref[0]

    t = CHUNK
    n = cpb * t
    dk = GLA_DK
    scale = dk ** -0.5
    dirs = ((qf_ref, kf_ref, vf_ref, af_ref), (qb_ref, kb_ref, vb_ref, ab_ref))
    units = []
    for d, (q_ref, k_ref, v_ref, a_ref) in enumerate(dirs):
        incl = _tri_masks(t, d == 1)[0]
        q, k, v = q_ref[0], k_ref[0] * scale, v_ref[0]
        lw = -_softplus(-(_mm(a_ref[0], aup_ref[d]) + abias_ref[d])) * (1.0 / GLA_TAU)
        cin = _mm_left01(_chunk_tri01(n, d == 1), lw)
        qt = q * jnp.exp(cin)
        kt = k * jnp.exp(-cin)
        for j in range(cpb):
            rows = slice(j * t, (j + 1) * t)
            last = j * t + (0 if d == 1 else t - 1)
            clast = cin[last:last + 1]
            kh = k[rows] * jnp.exp(clast - cin[rows])
            dec = jnp.exp(clast)
            for h in range(HEADS):
                sk = slice(h * dk, (h + 1) * dk)
                sv = slice(h * HEAD_V, (h + 1) * HEAD_V)
                units.append(dict(d=d, j=j, h=h, incl=incl, qt=qt[rows, sk], kt=kt[rows, sk],
                                  kh=kh[:, sk], v=v[rows, sv], dec=dec[:, sk]))

    amat = [jnp.where(u['incl'], _mm_nt(u['qt'], u['kt']), 0.0) for u in units]
    intra = [_mm(a, u['v']) for a, u in zip(amat, units)]
    kv = [_mm_tn(u['v'], u['kh']) for u in units]
    idx = {(u['d'], u['j'], u['h']): i for i, u in enumerate(units)}
    entering = {}
    for d in range(2):
        for h in range(HEADS):
            st = st_ref[d, h]
            for step in range(cpb):
                j = step if d == 0 else cpb - 1 - step
                i = idx[(d, j, h)]
                entering[i] = st
                st = units[i]['dec'] * st + kv[i]
            st_ref[d, h] = st
    outs = [x + _mm_nt(u['qt'], entering[i]) for i, (x, u) in enumerate(zip(intra, units))]
    for d, o_ref in enumerate((of_ref, ob_ref)):
        o_ref[0] = jnp.concatenate(
            [jnp.concatenate([outs[idx[(d, j, h)]] for h in range(HEADS)], axis=1) for j in range(cpb)], axis=0)

    @pl.when(c == nb - 1)
    def _():
        sfin_ref[0] = st_ref[...]


def _rope_tables(rows):
    nf = RET_DK // 4
    inv = ROPE_BASE ** (-jnp.arange(nf, dtype=F32) / nf)
    lane = jnp.arange(GROUP_W)
    freq = inv[lane % nf]
    sign = jnp.where((lane & 16) == 0, -1.0, 1.0)
    ar = jnp.arange(rows, dtype=F32)[:, None] * freq[None, :]
    ac = jnp.arange(GRID_W, dtype=F32)[:, None] * freq[None, :]
    return (jnp.cos(ar).reshape(rows, 1, GROUP_W), (jnp.sin(ar) * sign).reshape(rows, 1, GROUP_W),
            jnp.cos(ac), jnp.sin(ac) * sign)


def _ret_scan(z, tables, rope, s0):
    b, l, _ = z.shape
    cpb = LA_CPB
    n = cpb * CHUNK
    nb = l // n
    cr, sr, cc, sc = rope
    fw = lambda j: pl.BlockSpec((1, n, GROUP_W), lambda i, c: (i, c, j))
    bw = lambda j: pl.BlockSpec((1, n, GROUP_W), lambda i, c: (i, nb - 1 - c, j))
    rowf = pl.BlockSpec((cpb, 1, GROUP_W), lambda i, c: (c, 0, 0))
    rowb = pl.BlockSpec((cpb, 1, GROUP_W), lambda i, c: (nb - 1 - c, 0, 0))
    colt = pl.BlockSpec((GRID_W, GROUP_W), lambda i, c: (0, 0))
    full = lambda a: pl.BlockSpec(a.shape, lambda i, c: (0,) * a.ndim)
    st = pl.BlockSpec((1, 2, HEADS, HEAD_V, RET_DK), lambda i, c: (i, 0, 0, 0, 0))
    return pl.pallas_call(
        functools.partial(_ret_scan_kernel, nb, cpb),
        grid=(b, nb),
        in_specs=[fw(0), fw(1), fw(2), bw(0), bw(1), bw(2)] + [full(a) for a in tables]
                 + [rowf, rowf, rowb, rowb, colt, colt, st],
        out_specs=[fw(0), bw(0), st],
        out_shape=[jax.ShapeDtypeStruct((b, l, GROUP_W), F32)] * 2
                  + [jax.ShapeDtypeStruct((b, 2, HEADS, HEAD_V, RET_DK), F32)],
        scratch_shapes=[pltpu.VMEM((2, HEADS, HEAD_V, RET_DK), F32)],
        compiler_params=_cparams("parallel", "arbitrary"),
        name="retention_scan",
    )(z, z, z, z, z, z, *tables, cr, sr, cr, sr, cc, sc, s0)


def _gla_scan(z, aup, abias, s0):
    b, l, _ = z.shape
    cpb = LA_CPB
    nb = l // (cpb * CHUNK)
    blk = lambda w, j, rev: pl.BlockSpec(
        (1, cpb * CHUNK, w), (lambda i, c: (i, nb - 1 - c, j)) if rev else (lambda i, c: (i, c, j)))
    st = pl.BlockSpec((1, 2, HEADS, HEAD_V, GLA_DK), lambda i, c: (i, 0, 0, 0, 0))
    ofw = pl.BlockSpec((1, cpb * CHUNK, GROUP_W), lambda i, c: (i, c, 0))
    obw = pl.BlockSpec((1, cpb * CHUNK, GROUP_W), lambda i, c: (i, nb - 1 - c, 0))
    return pl.pallas_call(
        functools.partial(_gla_scan_kernel, nb, cpb),
        grid=(b, nb),
        in_specs=[blk(GLA_QK, 0, False), blk(GLA_QK, 1, False), blk(GROUP_W, 1, False), blk(128, 6, False),
                  blk(GLA_QK, 0, True), blk(GLA_QK, 1, True), blk(GROUP_W, 1, True), blk(128, 6, True),
                  pl.BlockSpec(aup.shape, lambda i, c: (0, 0, 0)),
                  pl.BlockSpec(abias.shape, lambda i, c: (0, 0, 0)), st],
        out_specs=[ofw, obw, st],
        out_shape=[jax.ShapeDtypeStruct((b, l, GROUP_W), F32)] * 2
                  + [jax.ShapeDtypeStruct((b, 2, HEADS, HEAD_V, GLA_DK), F32)],
        scratch_shapes=[pltpu.VMEM((2, HEADS, HEAD_V, GLA_DK), F32)],
        compiler_params=_cparams("parallel", "arbitrary"),
        name="gla_scan",
    )(z, z, z, z, z, z, z, z, aup, abias, s0)


def _mix_kernel(x_ref, g1_ref, y5_ref, u5_ref, d5_ref, gw_ref, gb_ref,
                ryf_ref, ryb_ref, rbonus_ref, rg_ref, rlng_ref, rlnb_ref,
                tof_ref, tob_ref, tg_ref, tln_ref,
                gof_ref, gob_ref, gg_ref, gln_ref,
                bones_ref, wo_ref, o_ref):
    bones = bones_ref[...]
    inv = 1.0 / HEAD_V

    def hmean(a):
        return _mm_right01(a, bones) * inv

    y = y5_ref[0] + d5_ref[...] * u5_ref[0]
    y = _gelu_tanh(y)
    ya = y * _sigmoid(_mm(y, gw_ref[...]) + gb_ref[...])
    yr = ryf_ref[0] + ryb_ref[0]
    dlt = yr - hmean(yr)
    yn = dlt * lax.rsqrt(hmean(dlt * dlt) + GN_EPS)
    yb = (yn * rlng_ref[...] + rlnb_ref[...] + rbonus_ref[0]) * rg_ref[0]
    ot = tof_ref[0] + tob_ref[0]
    yc = ot * lax.rsqrt(hmean(ot * ot) + EPS) * tln_ref[...] * _silu(tg_ref[0])
    og = gof_ref[0] + gob_ref[0]
    yd = og * lax.rsqrt(hmean(og * og) + EPS) * gln_ref[...] * _silu(gg_ref[0])
    mix = (_mm(ya, wo_ref[0]) + _mm(yb, wo_ref[1]) + _mm(yc, wo_ref[2]) + _mm(yd, wo_ref[3]))
    o_ref[0] = x_ref[0] + g1_ref[0] * mix


def _mix(x, g1, y5, u5, rw_pre, rw_y, zret, ret_o, zgla, gla_o, p, tm):
    b, l, d = x.shape
    tok = lambda w, j: pl.BlockSpec((1, tm, w), lambda i, t: (i, t, j))
    vec = lambda a: pl.BlockSpec(a.shape, lambda i, t: (0,) * a.ndim)
    t256 = tok(GROUP_W, 0)
    args = [x, g1, y5, u5, p['s5_d'], p['glu_w'], p['glu_b'],
            rw_y[0], rw_y[1], rw_pre[4], rw_pre[3], p['rw_ln_g'], p['rw_ln_b'],
            ret_o[0], ret_o[1], zret, p['ret_ln_g'],
            gla_o[0], gla_o[1], zgla, p['gla_ln_g'],
            p['bones'], p['w_out']]
    specs = [tok(d, 0), pl.BlockSpec((1, 1, d), lambda i, t: (i, 0, 0)), t256, t256,
             vec(p['s5_d']), vec(p['glu_w']), vec(p['glu_b']),
             t256, t256, t256, t256, vec(p['rw_ln_g']), vec(p['rw_ln_b']),
             t256, t256, tok(GROUP_W, 3), vec(p['ret_ln_g']),
             t256, t256, tok(GROUP_W, 2), vec(p['gla_ln_g']),
             vec(p['bones']), vec(p['w_out'])]
    return pl.pallas_call(
        _mix_kernel,
        grid=(b, l // tm),
        in_specs=specs,
        out_specs=tok(d, 0),
        out_shape=jax.ShapeDtypeStruct((b, l, d), F32),
        compiler_params=_cparams("parallel", "parallel"),
        name="mix_outproj",
    )(*args)


def _mlp_kernel(final, nff, x_ref, g_ref, sc_ref, sh_ref, gate_ref, w1_ref, w2_ref, fg_ref, o_ref):
    x = x_ref[0]
    hb = _modnorm(x, g_ref[...], sc_ref[0], sh_ref[0]).astype(BF16)
    ff = w1_ref.shape[1] // nff
    acc = None
    for j in range(nff):
        a = jnp.maximum(jnp.dot(hb, w1_ref[:, j * ff:(j + 1) * ff], preferred_element_type=F32), 0.0)
        part = jnp.dot((a * a).astype(BF16), w2_ref[j * ff:(j + 1) * ff, :], preferred_element_type=F32)
        acc = part if acc is None else acc + part
    y = x + gate_ref[0] * acc
    if final:
        ms = jnp.mean(y * y, axis=-1, keepdims=True)
        y = y * lax.rsqrt(ms + EPS) * fg_ref[...]
    o_ref[0] = y


def _mlp(x, g, sc, sh, gate, w1, w2, final_g, final, tm):
    b, l, d = x.shape
    tok = pl.BlockSpec((1, tm, d), lambda i, t: (i, t, 0))
    vec = pl.BlockSpec((1, 1, d), lambda i, t: (i, 0, 0))
    row = pl.BlockSpec((1, d), lambda i, t: (0, 0))
    once = lambda a: pl.BlockSpec(a.shape, lambda i, t: (0, 0), pipeline_mode=pl.Buffered(1))
    return pl.pallas_call(
        functools.partial(_mlp_kernel, final, 4),
        grid=(b, l // tm),
        in_specs=[tok, row, vec, vec, vec, once(w1), once(w2), row],
        out_specs=tok,
        out_shape=jax.ShapeDtypeStruct((b, l, d), F32),
        compiler_params=_cparams("parallel", "parallel"),
        name="mlp",
    )(x, g, sc, sh, gate, w1, w2, final_g)


def kernel(x, c, ctx, c_ctx, ada_w, ada_b, norm1_g, norm2_g, w_in, w_out, s5_lam_re, s5_lam_im, s5_log_dt, s5_b_re, s5_b_im, s5_c_re, s5_c_im, s5_d, s5_glu_w, s5_glu_b, rw_mu, rw_w0, rw_w_up, rw_a0, rw_a_up, rw_g_up, rw_k_k, rw_k_a, rw_r_k, rw_ln_g, rw_ln_b, ret_decay_logit, ret_ln_g, gla_a_up, gla_a_b, gla_ln_g, mlp_w1, mlp_w2, final_g):
    b, l, d = x.shape
    lc = ctx.shape[1]
    depth = ada_w.shape[0]
    rows = l // GRID_W
    assert l % 512 == 0 and lc % CHUNK == 0 and lc % S5_T == 0 and d % 128 == 0

    cond = jnp.zeros((8, d), F32).at[:b].set(c).at[b].set(c_ctx)
    mod = _modulation(cond, ada_w, ada_b)

    lane = jnp.arange(GROUP_W)
    bones = (lane[:, None] // HEAD_V == lane[None, :] // HEAD_V).astype(BF16)
    rope_lat = _rope_tables(rows)
    ident = (jnp.ones((lc // CHUNK, 1, GROUP_W), F32), jnp.zeros((lc // CHUNK, 1, GROUP_W), F32),
             jnp.ones((GRID_W, GROUP_W), F32), jnp.zeros((GRID_W, GROUP_W), F32))
    nlev = max(1, (l // S5_T - 1).bit_length())
    c0, c1, c2 = GROUP_W, GROUP_W + RW_COLS, GROUP_W + RW_COLS + 4 * GROUP_W
    row2 = lambda a: a.reshape(1, -1).astype(F32)

    xc = ctx
    for i in range(depth):
        last = i == depth - 1
        m = mod[i].reshape(8, N_ADA, d)
        ml = m[:b, :, None, :]
        mc = jnp.broadcast_to(m[b][None, :, None, :], (b, N_ADA, 1, d))
        wi = w_in[i].astype(BF16)
        w5, wr, wt = wi[:, :c0], wi[:, c0:c1], wi[:, c1:c2]
        wg = jnp.pad(wi[:, c2:], ((0, 0), (0, GLA_COLS_PAD - (wi.shape[1] - c2))))
        n1 = row2(norm1_g[i])
        n2 = row2(norm2_g[i])

        s5_tab = _s5_tables(s5_lam_re[i], s5_lam_im[i], s5_log_dt[i], s5_b_re[i], s5_b_im[i],
                            s5_c_re[i], s5_c_im[i], nlev)
        zpad = lambda a, lo, n: jnp.zeros((2, 128, n), F32).at[:, lo:lo + a.shape[1]].set(a).astype(BF16)
        rwp = dict(
            mu=row2(rw_mu[i]),
            w0=rw_w0[i].reshape(2, 1, GROUP_W), a0=rw_a0[i].reshape(2, 1, GROUP_W),
            wup=zpad(rw_w_up[i], 0, GROUP_W), aup=zpad(rw_a_up[i], RW_W_RANK, GROUP_W),
            gup=jnp.zeros((128, GROUP_W), F32).at[RW_W_RANK + RW_A_RANK:].set(rw_g_up[i]).astype(BF16),
            kk=row2(rw_k_k[i]), ka=row2(rw_k_a[i]), rk=row2(rw_r_k[i]), bones=bones)
        ret_tab = _ret_tables(ret_decay_logit[i], LA_CPB * CHUNK)
        gla_aup = zpad(gla_a_up[i], 0, GLA_QK)
        gla_ab = gla_a_b[i].reshape(2, 1, GLA_QK).astype(F32)
        mixp = dict(s5_d=row2(s5_d[i]), glu_w=s5_glu_w[i].astype(BF16), glu_b=row2(s5_glu_b[i]),
                    rw_ln_g=row2(rw_ln_g[i]), rw_ln_b=row2(rw_ln_b[i]), ret_ln_g=row2(ret_ln_g[i]),
                    gla_ln_g=row2(gla_ln_g[i]), bones=bones,
                    w_out=w_out[i].astype(BF16).reshape(4, GROUP_W, d))
        w1 = mlp_w1[i].astype(BF16)
        w2 = mlp_w2[i].astype(BF16)

        def mixers(xx, mm, is_lat, states):
            tm = 512 if is_lat else lc
            z5, zr, zt, zg = _inproj(xx, n1, mm[:, 1], mm[:, 0], w5, wr, wt, wg, tm)
            y5, h5 = _s5_scan(z5, s5_tab, states[0])
            pre = _rw_prep(zr, is_lat, rwp)
            yrf, yrb, srw = _rw_scan(pre, states[1])
            otf, otb, sret = _ret_scan(zt, ret_tab, rope_lat if is_lat else ident, states[2])
            ogf, ogb, sgla = _gla_scan(zg, gla_aup, gla_ab, states[3])
            outs = (z5, y5, pre, (yrf, yrb), zt, (otf, otb), zg, (ogf, ogb))
            return outs, (h5, srw, sret, sgla)

        def block(xx, mm, outs, is_lat, fin):
            tm = 512 if is_lat else lc
            z5, y5, pre, yr, zt, ot, zg, og = outs
            x1 = _mix(xx, mm[:, 2], y5, z5, pre, yr, zt, ot, zg, og, mixp, tm)
            return _mlp(x1, n2, mm[:, 4], mm[:, 3], mm[:, 5], w1, w2, row2(final_g), fin, tm)

        zeros = (jnp.zeros((b, 2, S5_G, 1, 2 * S5_N), F32),
                 jnp.zeros((b, 2, HEADS, HEAD_V, HEAD_V), F32),
                 jnp.zeros((b, 2, HEADS, HEAD_V, RET_DK), F32),
                 jnp.zeros((b, 2, HEADS, HEAD_V, GLA_DK), F32))
        outs_c, st_c = mixers(xc, mc, False, zeros)
        outs_l, _ = mixers(x, ml, True, st_c)
        x = block(x, ml, outs_l, True, last)
        if not last:
            xc = block(xc, mc, outs_c, False, False)
    return x
```

```python
import functools
import math

import jax
import jax.numpy as jnp
from jax import lax
from jax.experimental import pallas as pl
from jax.experimental.pallas import tpu as pltpu

F32 = jnp.float32
BF16 = jnp.bfloat16

LANES = 128
GRID_W = 64
GROUP_W = 256
N_ADA = 6
EPS = 1e-6
GN_EPS = 64e-5
CHUNK = 64
RW_CPB = 2
LA_CPB = 4
HEADS = 4
HEAD_V = 64
S5_P = 16
S5_G = 16
S5_N = 64
S5_T = 32
S5_TP = S5_T * S5_P
RW_COLS = 896
RW_W_RANK = 32
RW_A_RANK = 32
RW_G_RANK = 64
RET_DK = 64
GLA_DK = 32
GLA_QK = 128
GLA_RANK = 16
GLA_TAU = 16.0
GLA_COLS_PAD = 896
ROPE_BASE = 10000.0
VMEM_LIMIT = 56 * 1024 * 1024


def _cparams(*sem):
    return pltpu.CompilerParams(dimension_semantics=sem, vmem_limit_bytes=VMEM_LIMIT)


def _mm(a, b):
    return jnp.dot(a.astype(BF16), b.astype(BF16), preferred_element_type=F32)


def _mm_nt(a, b):
    return lax.dot_general(a.astype(BF16), b.astype(BF16), (((1,), (1,)), ((), ())),
                           preferred_element_type=F32)


def _mm_tn(a, b):
    return lax.dot_general(a.astype(BF16), b.astype(BF16), (((0,), (0,)), ((), ())),
                           preferred_element_type=F32)


def _split2(x):
    hi = x.astype(BF16)
    return hi, (x - hi.astype(F32)).astype(BF16)


def _join2(p):
    return p[0].astype(F32) + p[1].astype(F32)


def _mm_x3p(a, b):
    return (jnp.dot(a[0], b[0], preferred_element_type=F32) + jnp.dot(a[0], b[1], preferred_element_type=F32)
            + jnp.dot(a[1], b[0], preferred_element_type=F32))


def _chunk_tri01(n, reverse):
    ri = lax.broadcasted_iota(jnp.int32, (n, n), 0)
    ci = lax.broadcasted_iota(jnp.int32, (n, n), 1)
    shift = CHUNK.bit_length() - 1
    tri = (ci >= ri) if reverse else (ci <= ri)
    return jnp.where((ri >> shift) == (ci >> shift), jnp.where(tri, 1.0, 0.0), 0.0).astype(BF16)


def _split3(x):
    hi = x.astype(BF16)
    r1 = x - hi.astype(F32)
    mid = r1.astype(BF16)
    lo = (r1 - mid.astype(F32)).astype(BF16)
    return hi, mid, lo


def _mm_left01(m01, x):
    hi, mid, lo = _split3(x)
    return (jnp.dot(m01, hi, preferred_element_type=F32)
            + jnp.dot(m01, mid, preferred_element_type=F32)
            + jnp.dot(m01, lo, preferred_element_type=F32))


def _mm_right01(x, m01):
    hi, mid, lo = _split3(x)
    return (jnp.dot(hi, m01, preferred_element_type=F32)
            + jnp.dot(mid, m01, preferred_element_type=F32)
            + jnp.dot(lo, m01, preferred_element_type=F32))


def _sigmoid(x):
    return 1.0 / (1.0 + jnp.exp(-x))


def _softplus(x):
    return jnp.maximum(x, 0.0) + jnp.log(1.0 + jnp.exp(-jnp.abs(x)))


def _silu(x):
    return x * _sigmoid(x)


def _gelu_tanh(x):
    return 0.5 * x * (1.0 + jnp.tanh(math.sqrt(2.0 / math.pi) * (x + 0.044715 * x * x * x)))


def _tri_masks(t, reverse):
    ri = lax.broadcasted_iota(jnp.int32, (t, t), 0)
    ci = lax.broadcasted_iota(jnp.int32, (t, t), 1)
    if reverse:
        return ci >= ri, ci > ri, ci == ri
    return ci <= ri, ci < ri, ci == ri


def _row_to_col(row, eye):
    n = row.shape[1]
    return jnp.sum(jnp.where(eye, jnp.broadcast_to(row, (n, n)), 0.0), axis=1, keepdims=True)


def _mod_kernel(cond_ref, w_ref, b_ref, o_ref):
    c = cond_ref[...]
    o_ref[0] = _mm(_silu(c), w_ref[0]) + b_ref[0]


def _modulation(cond, ada_w, ada_b):
    depth, d, n = ada_w.shape
    tn = 1536
    return pl.pallas_call(
        _mod_kernel,
        grid=(depth, n // tn),
        in_specs=[pl.BlockSpec((8, d), lambda i, j: (0, 0)),
                  pl.BlockSpec((1, d, tn), lambda i, j: (i, 0, j)),
                  pl.BlockSpec((1, 1, tn), lambda i, j: (i, 0, j))],
        out_specs=pl.BlockSpec((1, 8, tn), lambda i, j: (i, 0, j)),
        out_shape=jax.ShapeDtypeStruct((depth, 8, n), F32),
        compiler_params=_cparams("parallel", "parallel"),
        name="adaln_mod",
    )(cond, ada_w, ada_b.reshape(depth, 1, n))


def _modnorm(x, g, sc, sh):
    ms = jnp.mean(x * x, axis=-1, keepdims=True)
    return x * lax.rsqrt(ms + EPS) * g * (1.0 + sc) + sh


def _inproj_kernel(x_ref, g_ref, sc_ref, sh_ref, w5_ref, wr_ref, wt_ref, wg_ref,
                   o5_ref, o5g_ref, or_ref, ot_ref, og_ref, z5h_ref):
    hb = _modnorm(x_ref[0], g_ref[...], sc_ref[0], sh_ref[0]).astype(BF16)
    z5 = jnp.dot(hb, w5_ref[...], preferred_element_type=F32)
    o5_ref[0] = z5
    or_ref[0] = jnp.dot(hb, wr_ref[...], preferred_element_type=F32)
    ot_ref[0] = jnp.dot(hb, wt_ref[...], preferred_element_type=F32)
    og_ref[0] = jnp.dot(hb, wg_ref[...], preferred_element_type=F32)
    nch = o5g_ref.shape[2]
    gph = LANES // S5_P
    for hf in range(GROUP_W // LANES):
        z5h_ref[hf] = z5[:, hf * LANES:(hf + 1) * LANES]
    for s in range(S5_T):
        for hf in range(GROUP_W // LANES):
            rows = z5h_ref[hf, pl.ds(s, nch, stride=S5_T), :]
            for g in range(gph):
                o5g_ref[0, hf * gph + g, :, s * S5_P:(s + 1) * S5_P] = rows[:, g * S5_P:(g + 1) * S5_P]


def _inproj(x, g, sc, sh, w5, wr, wt, wg, tm):
    b, l, d = x.shape
    widths = (w5.shape[1], wr.shape[1], wt.shape[1], wg.shape[1])
    tok = lambda n: pl.BlockSpec((1, tm, n), lambda i, j: (i, j, 0))
    vec = pl.BlockSpec((1, 1, d), lambda i, j: (i, 0, 0))
    full = lambda w: pl.BlockSpec(w.shape, lambda i, j: (0, 0))
    grouped = pl.BlockSpec((1, S5_G, tm // S5_T, S5_TP), lambda i, j: (i, 0, j, 0))
    return pl.pallas_call(
        _inproj_kernel,
        grid=(b, l // tm),
        in_specs=[tok(d), pl.BlockSpec((1, d), lambda i, j: (0, 0)), vec, vec,
                  full(w5), full(wr), full(wt), full(wg)],
        out_specs=[tok(widths[0]), grouped] + [tok(n) for n in widths[1:]],
        out_shape=[jax.ShapeDtypeStruct((b, l, widths[0]), F32),
                   jax.ShapeDtypeStruct((b, S5_G, l // S5_T, S5_TP), F32)]
                  + [jax.ShapeDtypeStruct((b, l, n), F32) for n in widths[1:]],
        scratch_shapes=[pltpu.VMEM((GROUP_W // LANES, tm, LANES), F32)],
        compiler_params=_cparams("parallel", "parallel"),
        name="norm_inproj",
    )(x, g, sc, sh, w5, wr, wt, wg)


def _s5_tables(lam_re, lam_im, log_dt, b_re, b_im, c_re, c_im, nlev):
    hp = lax.Precision.HIGHEST
    t = S5_T
    lam = lax.complex(jnp.minimum(lam_re.astype(F32), -1e-4), lam_im.astype(F32))
    ldt = lam * jnp.exp(log_dt.astype(F32))[..., None]
    a_bar = jnp.exp(ldt)
    bb = ((a_bar - 1.0) / lam)[..., None] * lax.complex(b_re.astype(F32), b_im.astype(F32))
    cm = lax.complex(c_re.astype(F32), c_im.astype(F32))
    tau = jnp.arange(t + 1, dtype=F32)
    apow = jnp.exp(ldt[:, :, None, :] * tau[None, None, :, None])
    taps = jnp.einsum('dgpn,dgtn,dgnq->dgtpq', cm, apow[:, :, :t], bb, precision=hp).real
    s_i = jnp.arange(t)[:, None]
    t_i = jnp.arange(t)[None, :]

    def toeplitz(k, lag, ok):
        m = k[:, jnp.clip(lag, 0, t - 1)]
        m = jnp.where(ok[None, :, :, None, None], m, 0.0)
        return m.transpose(0, 1, 4, 2, 3).reshape(S5_G, S5_TP, S5_TP)

    conv = jnp.stack([toeplitz(taps[0], t_i - s_i, t_i >= s_i),
                      toeplitz(taps[1], s_i - t_i, s_i >= t_i)])

    def pack(zc):
        return jnp.concatenate([zc.real, zc.imag], axis=-1)

    win_f = apow[0][:, t - 1 - jnp.arange(t), None, :] * bb[0].transpose(0, 2, 1)[:, None]
    win_b = apow[1][:, jnp.arange(t), None, :] * bb[1].transpose(0, 2, 1)[:, None]
    win = jnp.stack([pack(win_f), pack(win_b)]).reshape(2, S5_G, S5_TP, 2 * S5_N)
    ca_f = cm[0][:, None] * apow[0][:, 1 + jnp.arange(t), None, :]
    ca_b = cm[1][:, None] * apow[1][:, t - jnp.arange(t), None, :]

    def outpack(ca):
        w = jnp.concatenate([ca.real, -ca.imag], axis=-1)
        return w.reshape(S5_G, S5_TP, 2 * S5_N).transpose(0, 2, 1)

    wout = jnp.stack([outpack(ca_f), outpack(ca_b)])
    lev = (2.0 ** jnp.arange(nlev, dtype=F32)) * t
    pw = jnp.exp(ldt[:, :, None, :] * lev[None, None, :, None])
    p1 = jnp.concatenate([pw.real, pw.real], axis=-1)
    p2 = jnp.concatenate([-pw.imag, pw.imag], axis=-1)
    pw = jnp.stack([p1, p2], axis=3)
    return conv.astype(BF16), win.astype(BF16), wout.astype(BF16), pw


def _s5_kernel(nc, nlev, u_ref, conv_ref, win_ref, wout_ref, pw_ref, h0_ref, y_ref, hfin_ref):
    u = u_ref[0, 0].astype(BF16)
    row = lax.broadcasted_iota(jnp.int32, (nc, 2 * S5_N), 0)

    def cmul(x, d, j):
        return pw_ref[d, 0, j, 0:1] * x + pw_ref[d, 0, j, 1:2] * pltpu.roll(x, S5_N, 1)

    y = None
    for d in range(2):
        v = jnp.dot(u, win_ref[d, 0], preferred_element_type=F32)
        h0 = h0_ref[0, d, 0]
        if d == 0:
            x = jnp.where(row == 0, h0, pltpu.roll(v, 1, 0))
        else:
            x = jnp.where(row == nc - 1, h0, pltpu.roll(v, nc - 1, 0))
        for j in range(nlev):
            sh = 2 ** j
            if d == 0:
                xs = jnp.where(row >= sh, pltpu.roll(x, sh, 0), 0.0)
            else:
                xs = jnp.where(row < nc - sh, pltpu.roll(x, nc - sh, 0), 0.0)
            x = x + cmul(xs, d, j)
        last = nc - 1 if d == 0 else 0
        hfin_ref[0, d, 0] = cmul(x[last:last + 1], d, 0) + v[last:last + 1]
        yd = (jnp.dot(u, conv_ref[d, 0], preferred_element_type=F32)
              + _mm(x, wout_ref[d, 0]))
        y = yd if y is None else y + yd
    y_ref[0, 0] = y


def _s5_scan(uf, tables, h0):
    conv, win, wout, pw = tables
    b, _, nc, _ = uf.shape
    nlev = max(1, (nc - 1).bit_length())
    pw = pw[:, :, :nlev]
    n2 = 2 * S5_N
    y, hfin = pl.pallas_call(
        functools.partial(_s5_kernel, nc, nlev),
        grid=(b, S5_G),
        in_specs=[pl.BlockSpec((1, 1, nc, S5_TP), lambda i, g: (i, g, 0, 0)),
                  pl.BlockSpec((2, 1, S5_TP, S5_TP), lambda i, g: (0, g, 0, 0)),
                  pl.BlockSpec((2, 1, S5_TP, n2), lambda i, g: (0, g, 0, 0)),
                  pl.BlockSpec((2, 1, n2, S5_TP), lambda i, g: (0, g, 0, 0)),
                  pl.BlockSpec((2, 1, nlev, 2, n2), lambda i, g: (0, g, 0, 0, 0)),
                  pl.BlockSpec((1, 2, 1, 1, n2), lambda i, g: (i, 0, g, 0, 0))],
        out_specs=[pl.BlockSpec((1, 1, nc, S5_TP), lambda i, g: (i, g, 0, 0)),
                   pl.BlockSpec((1, 2, 1, 1, n2), lambda i, g: (i, 0, g, 0, 0))],
        out_shape=[jax.ShapeDtypeStruct((b, S5_G, nc, S5_TP), F32),
                   jax.ShapeDtypeStruct((b, 2, S5_G, 1, n2), F32)],
        compiler_params=_cparams("parallel", "parallel"),
        name="s5_scan",
    )(uf, conv, win, wout, pw, h0)
    return y, hfin


def _rw_prep_kernel(grid_shift, nt, z_ref, zp_ref, zn_ref, mu_ref, w0_ref, wup_ref, a0_ref, aup_ref,
                    gup_ref, kk_ref, ka_ref, rk_ref, bones_ref,
                    r_ref, v_ref, kkn_ref, g_ref, bonus_ref,
                    lw0_ref, kd0_ref, as0_ref, lw1_ref, kd1_ref, as1_ref):
    i = pl.program_id(1)
    z = z_ref[0]
    tm = z.shape[0]
    row = lax.broadcasted_iota(jnp.int32, z.shape, 0)
    lane = lax.broadcasted_iota(jnp.int32, z.shape, 1)
    prev1 = pltpu.roll(z, 1, 0)
    next1 = pltpu.roll(z, tm - 1, 0)
    if grid_shift:
        col = row & (GRID_W - 1)
        left = jnp.where(col == 0, 0.0, prev1)
        right = jnp.where(col == GRID_W - 1, 0.0, next1)
        zp = jnp.where(i > 0, zp_ref[0], 0.0)
        zn = jnp.where(i < nt - 1, zn_ref[0], 0.0)
        up = jnp.concatenate([zp, z[:tm - GRID_W]], axis=0)
        down = jnp.concatenate([z[GRID_W:], zn], axis=0)
        q = RW_COLS // 4
        shifted = jnp.where(lane < q, left,
                            jnp.where(lane < 2 * q, right, jnp.where(lane < 3 * q, up, down)))
    else:
        prev = jnp.where(row == 0, 0.0, prev1)
        nxt = jnp.where(row == tm - 1, 0.0, next1)
        shifted = jnp.where(lane < RW_COLS // 2, prev, nxt)
    zm = z + mu_ref[...] * (shifted - z)
    r = zm[:, 0:256]
    k = zm[:, 256:512]
    v = zm[:, 512:768]
    lo = zm[:, 768:896]
    bones = bones_ref[...]
    g_ref[0] = _mm(_sigmoid(lo), gup_ref[...])
    kk = k * kk_ref[...]
    kk = kk * lax.rsqrt(_mm_right01(kk * kk, bones) + 1e-12)
    r_ref[0] = r
    v_ref[0] = v
    kkn_ref[0] = kk
    bonus_ref[0] = _mm_right01(r * k * rk_ref[...], bones) * v
    tlo = jnp.tanh(lo)
    for d, (lw_ref, kd_ref, as_ref) in enumerate(((lw0_ref, kd0_ref, as0_ref), (lw1_ref, kd1_ref, as1_ref))):
        w_raw = -_softplus(-(w0_ref[d] + _mm(tlo, wup_ref[d]))) - 0.5
        lw_ref[0] = -jnp.exp(w_raw)
        a = _sigmoid(a0_ref[d] + _mm(lo, aup_ref[d]))
        kd_ref[0] = k * (1.0 + (a - 1.0) * ka_ref[...])
        as_ref[0] = a


def _rw_prep(z, grid_shift, p):
    b, l, _ = z.shape
    tm = 256 if grid_shift else l
    nt = l // tm
    hb = tm // GRID_W
    nhb = l // GRID_W
    tok = pl.BlockSpec((1, tm, GROUP_W), lambda i, j: (i, j, 0))
    full = lambda a: pl.BlockSpec(a.shape, lambda i, j: (0,) * a.ndim)
    params = (p['mu'], p['w0'], p['wup'], p['a0'], p['aup'], p['gup'], p['kk'], p['ka'], p['rk'], p['bones'])
    return pl.pallas_call(
        functools.partial(_rw_prep_kernel, grid_shift, nt),
        grid=(b, nt),
        in_specs=[pl.BlockSpec((1, tm, RW_COLS), lambda i, j: (i, j, 0)),
                  pl.BlockSpec((1, GRID_W, RW_COLS), lambda i, j: (i, jnp.maximum(j * hb - 1, 0), 0)),
                  pl.BlockSpec((1, GRID_W, RW_COLS), lambda i, j: (i, jnp.minimum((j + 1) * hb, nhb - 1), 0))]
                 + [full(a) for a in params],
        out_specs=[tok] * 11,
        out_shape=[jax.ShapeDtypeStruct((b, l, GROUP_W), F32)] * 11,
        compiler_params=_cparams("parallel", "parallel"),
        name="rwkv_prep",
    )(z, z, z, *params)


def _rw_scan_kernel(nb, cpb, rf_ref, vf_ref, kkf_ref, lwf_ref, kdf_ref, asf_ref,
                    rb_ref, vb_ref, kkb_ref, lwb_ref, kdb_ref, asb_ref, s0_ref,
                    yf_ref, yb_ref, sfin_ref, st_ref):
    c = pl.program_id(1)

    @pl.when(c == 0)
    def _():
        st_ref[...] = s0_ref[0]

    t = CHUNK
    ri = lax.broadcasted_iota(jnp.int32, (t, t), 0)
    ci = lax.broadcasted_iota(jnp.int32, (t, t), 1)
    eye = ri == ci
    eyef = jnp.where(eye, 1.0, 0.0)
    same = lambda s: jnp.where((ri >> s) == (ci >> s), 1.0, 0.0)
    m4, m8, m16, m32 = same(2), same(3), same(4), same(5)
    merge_masks = (m8 - m4, m16 - m8, m32 - m16, 1.0 - m32)
    dir_refs = ((rf_ref, vf_ref, kkf_ref, lwf_ref, kdf_ref, asf_ref),
                (rb_ref, vb_ref, kkb_ref, lwb_ref, kdb_ref, asb_ref))
    units = []
    for d, (r_ref, v_ref, kk_ref, lw_ref, kd_ref, as_ref) in enumerate(dir_refs):
        incl = (ci >= ri) if d == 1 else (ci <= ri)
        strict = (ci > ri) if d == 1 else (ci < ri)
        r, v, kk, lw, kd = r_ref[0], v_ref[0], kk_ref[0], lw_ref[0], kd_ref[0]
        cin = _mm_left01(_chunk_tri01(cpb * t, d == 1), lw)
        e_in = jnp.exp(cin)
        e_neg = jnp.exp(-cin)
        rt = r * e_in
        at = -kk * jnp.exp(cin - lw)
        bvec = kk * as_ref[0]
        bt = bvec * e_neg
        kt = kd * e_neg
        for j in range(cpb):
            rows = slice(j * t, (j + 1) * t)
            last = j * t + (0 if d == 1 else t - 1)
            clast = cin[last:last + 1]
            dl = jnp.exp(clast - cin[rows])
            bh = bvec[rows] * dl
            kh = kd[rows] * dl
            dec = jnp.exp(clast)
            for h in range(HEADS):
                sl = slice(h * HEAD_V, (h + 1) * HEAD_V)
                units.append(dict(d=d, j=j, h=h, incl=incl, strict=strict,
                                  at=at[rows, sl], rt=rt[rows, sl], bt=bt[rows, sl], kt=kt[rows, sl],
                                  bh=bh[:, sl], kh=kh[:, sl], v=v[rows, sl], dec=dec[:, sl]))

    a_all = [_mm_nt(jnp.concatenate([u['at'], u['rt']], axis=0),
                    jnp.concatenate([u['bt'], u['kt']], axis=0)) for u in units]
    nmat = [jnp.where(u['strict'], a[0:t, 0:t], 0.0) for u, a in zip(units, a_all)]
    a_kk = [jnp.concatenate([jnp.where(u['strict'], a[0:t, t:2 * t], 0.0),
                             jnp.where(u['incl'], a[t:2 * t, t:2 * t], 0.0)], axis=0)
            for u, a in zip(units, a_all)]
    a_rb = [jnp.where(u['incl'], a[t:2 * t, 0:t], 0.0) for u, a in zip(units, a_all)]
    akv = [_mm(a, u['v']) for u, a in zip(units, a_kk)]
    kv = [_mm_tn(u['v'], u['kh']) for u in units]
    nd = [x * m4 for x in nmat]
    n2 = [_mm(x, x) for x in nd]
    tinv = [eyef + x + _mm(eyef + x, y) for x, y in zip(nd, n2)]
    for mk in merge_masks:
        w = [_mm(ti, x * mk) for ti, x in zip(tinv, nmat)]
        tinv = [ti + _mm(wi, ti) for ti, wi in zip(tinv, w)]
    zz = [_mm(ti, jnp.concatenate([u['at'], kvv[0:t]], axis=1)) for ti, u, kvv in zip(tinv, units, akv)]
    ght = [_mm_tn(z, u['bh']) for u, z in zip(units, zz)]
    qy = [_mm(a, z) for a, z in zip(a_rb, zz)]
    qmat = [u['rt'] + x[:, :HEAD_V] for u, x in zip(units, qy)]
    y0 = [x[:, HEAD_V:] + kvv[t:2 * t] for x, kvv in zip(qy, akv)]
    gmat = [x[:HEAD_V] for x in ght]
    hmat = [x[HEAD_V:] + k2 for x, k2 in zip(ght, kv)]
    idx = {(u['d'], u['j'], u['h']): n for n, u in enumerate(units)}
    state = {(d, h): st_ref[d, h] for d in range(2) for h in range(HEADS)}
    ys = {}
    for step in range(cpb):
        for d in range(2):
            j = step if d == 0 else cpb - 1 - step
            for h in range(HEADS):
                n = idx[(d, j, h)]
                st = state[(d, h)]
                ys[(d, j, h)] = _mm_nt(qmat[n], st) + y0[n]
                state[(d, h)] = units[n]['dec'] * st + _mm(st, gmat[n]) + hmat[n]
    for d, y_ref in enumerate((yf_ref, yb_ref)):
        y_ref[0] = jnp.concatenate(
            [jnp.concatenate([ys[(d, j, h)] for h in range(HEADS)], axis=1) for j in range(cpb)], axis=0)
        for h in range(HEADS):
            st_ref[d, h] = state[(d, h)]

    @pl.when(c == nb - 1)
    def _():
        sfin_ref[0] = st_ref[...]


def _rw_scan(pre, s0):
    r, v, kk, _, _, lw0, kd0, as0, lw1, kd1, as1 = pre
    b, l, _ = r.shape
    cpb = RW_CPB
    nb = l // (cpb * CHUNK)
    fw = pl.BlockSpec((1, cpb * CHUNK, GROUP_W), lambda i, c: (i, c, 0))
    bw = pl.BlockSpec((1, cpb * CHUNK, GROUP_W), lambda i, c: (i, nb - 1 - c, 0))
    st = pl.BlockSpec((1, 2, HEADS, HEAD_V, HEAD_V), lambda i, c: (i, 0, 0, 0, 0))
    return pl.pallas_call(
        functools.partial(_rw_scan_kernel, nb, cpb),
        grid=(b, nb),
        in_specs=[fw] * 6 + [bw] * 6 + [st],
        out_specs=[fw, bw, st],
        out_shape=[jax.ShapeDtypeStruct((b, l, GROUP_W), F32)] * 2
                  + [jax.ShapeDtypeStruct((b, 2, HEADS, HEAD_V, HEAD_V), F32)],
        scratch_shapes=[pltpu.VMEM((2, HEADS, HEAD_V, HEAD_V), F32)],
        compiler_params=_cparams("parallel", "arbitrary"),
        name="rwkv_scan",
    )(r, v, kk, lw0, kd0, as0, r, v, kk, lw1, kd1, as1, s0)


def _rope(x, cos, sin_signed):
    lane = lax.broadcasted_iota(jnp.int32, x.shape, 1)
    n = x.shape[1]
    swapped = jnp.where((lane & 16) == 0, pltpu.roll(x, n - 16, 1), pltpu.roll(x, 16, 1))
    return x * cos + swapped * sin_signed


def _ret_scan_kernel(nb, cpb, qf_ref, kf_ref, vf_ref, qb_ref, kb_ref, vb_ref,
                     dmat_ref, qdec_ref, kdec_ref, sdec_ref,
                     crf_ref, srf_ref, crb_ref, srb_ref, cc_ref, sc_ref, s0_ref,
                     of_ref, ob_ref, sfin_ref, st_ref):
    c = pl.program_id(1)

    @pl.when(c == 0)
    def _():
        st_ref[...] = s0_ref[0]

    t = CHUNK
    scale = RET_DK ** -0.5
    lane = lax.broadcasted_iota(jnp.int32, (t, GROUP_W), 1)
    by_row = (lane & 32) == 0
    qs, ks, qds, khs, vs = [], [], [], [], []
    for d, (q_ref, k_ref, v_ref) in enumerate(((qf_ref, kf_ref, vf_ref), (qb_ref, kb_ref, vb_ref))):
        cr_ref, sr_ref = (crf_ref, srf_ref) if d == 0 else (crb_ref, srb_ref)
        cos = jnp.concatenate([jnp.where(by_row, cr_ref[j], cc_ref[...]) for j in range(cpb)], axis=0)
        sin = jnp.concatenate([jnp.where(by_row, sr_ref[j], sc_ref[...]) for j in range(cpb)], axis=0)
        q = _rope(q_ref[0], cos, sin)
        k = _rope(k_ref[0] * scale, cos, sin)
        v = v_ref[0]
        qd = q * qdec_ref[d]
        kh = k * kdec_ref[d]
        for h in range(HEADS):
            sl = slice(h * HEAD_V, (h + 1) * HEAD_V)
            qs.append(q[:, sl])
            ks.append(k[:, sl])
            qds.append(qd[:, sl])
            khs.append(kh[:, sl])
            vs.append(v[:, sl])
    scores = [_mm_nt(q, k) * dmat_ref[i // HEADS, i % HEADS] for i, (q, k) in enumerate(zip(qs, ks))]
    states = [st_ref[i // HEADS, i % HEADS] for i in range(2 * HEADS)]
    outs = [_mm(p, v) + _mm_nt(qd, st) for p, v, qd, st in zip(scores, vs, qds, states)]
    for i, (st, kh, v) in enumerate(zip(states, khs, vs)):
        d, h = i // HEADS, i % HEADS
        st_ref[d, h] = st * sdec_ref[d][:, h * HEAD_V:(h + 1) * HEAD_V] + _mm_tn(v, kh)
    of_ref[0] = jnp.concatenate(outs[:HEADS], axis=1)
    ob_ref[0] = jnp.concatenate(outs[HEADS:], axis=1)

    @pl.when(c == nb - 1)
    def _():
        sfin_ref[0] = st_ref[...]


def _ret_tables(decay_logit, n):
    lg = jax.nn.log_sigmoid(decay_logit.astype(F32))
    pos = jnp.arange(n, dtype=F32)
    lag = pos[:, None] - pos[None, :]
    lag = jnp.stack([lag, -lag])
    dmat = jnp.where(lag[:, None] >= 0, jnp.exp(lg[:, :, None, None] * lag[:, None]), 0.0)
    lanes = jnp.repeat(lg, RET_DK, axis=-1)[:, None, :]
    qpow = jnp.stack([pos + 1.0, n - pos])[:, :, None]
    kpow = jnp.stack([n - 1.0 - pos, pos])[:, :, None]
    return dmat, jnp.exp(lanes * qpow), jnp.exp(lanes * kpow), jnp.exp(lanes * n)


def _gla_scan_kernel(nb, cpb, qf_ref, kf_ref, vf_ref, af_ref, qb_ref, kb_ref, vb_ref, ab_ref,
                     aup_ref, abias_ref, s0_ref, of_ref, ob_ref, sfin_ref, st_ref):
    c = pl.program_id(1)

    @pl.when(c == 0)
    def _():
        st_ref[...] = s0_ref[0]

    t = CHUNK
    n = cpb * t
    dk = GLA_DK
    scale = dk ** -0.5
    dirs = ((qf_ref, kf_ref, vf_ref, af_ref), (qb_ref, kb_ref, vb_ref, ab_ref))
    units = []
    for d, (q_ref, k_ref, v_ref, a_ref) in enumerate(dirs):
        incl = _tri_masks(t, d == 1)[0]
        q, k, v = q_ref[0], k_ref[0] * scale, v_ref[0]
        lw = -_softplus(-(_mm(a_ref[0], aup_ref[d]) + abias_ref[d])) * (1.0 / GLA_TAU)
        cin = _mm_left01(_chunk_tri01(n, d == 1), lw)
        qt = q * jnp.exp(cin)
        kt = k * jnp.exp(-cin)
        for j in range(cpb):
            rows = slice(j * t, (j + 1) * t)
            last = j * t + (0 if d == 1 else t - 1)
            clast = cin[last:last + 1]
            kh = k[rows] * jnp.exp(clast - cin[rows])
            dec = jnp.exp(clast)
            for h in range(HEADS):
                sk = slice(h * dk, (h + 1) * dk)
                sv = slice(h * HEAD_V, (h + 1) * HEAD_V)
                units.append(dict(d=d, j=j, h=h, incl=incl, qt=qt[rows, sk], kt=kt[rows, sk],
                                  kh=kh[:, sk], v=v[rows, sv], dec=dec[:, sk]))

    amat = [jnp.where(u['incl'], _mm_nt(u['qt'], u['kt']), 0.0) for u in units]
    intra = [_mm(a, u['v']) for a, u in zip(amat, units)]
    kv = [_mm_tn(u['v'], u['kh']) for u in units]
    idx = {(u['d'], u['j'], u['h']): i for i, u in enumerate(units)}
    entering = {}
    for d in range(2):
        for h in range(HEADS):
            st = st_ref[d, h]
            for step in range(cpb):
                j = step if d == 0 else cpb - 1 - step
                i = idx[(d, j, h)]
                entering[i] = st
                st = units[i]['dec'] * st + kv[i]
            st_ref[d, h] = st
    outs = [x + _mm_nt(u['qt'], entering[i]) for i, (x, u) in enumerate(zip(intra, units))]
    for d, o_ref in enumerate((of_ref, ob_ref)):
        o_ref[0] = jnp.concatenate(
            [jnp.concatenate([outs[idx[(d, j, h)]] for h in range(HEADS)], axis=1) for j in range(cpb)], axis=0)

    @pl.when(c == nb - 1)
    def _():
        sfin_ref[0] = st_ref[...]


def _rope_tables(rows):
    nf = RET_DK // 4
    inv = ROPE_BASE ** (-jnp.arange(nf, dtype=F32) / nf)
    lane = jnp.arange(GROUP_W)
    freq = inv[lane % nf]
    sign = jnp.where((lane & 16) == 0, -1.0, 1.0)
    ar = jnp.arange(rows, dtype=F32)[:, None] * freq[None, :]
    ac = jnp.arange(GRID_W, dtype=F32)[:, None] * freq[None, :]
    return (jnp.cos(ar).reshape(rows, 1, GROUP_W), (jnp.sin(ar) * sign).reshape(rows, 1, GROUP_W),
            jnp.cos(ac), jnp.sin(ac) * sign)


def _ret_scan(z, tables, rope, s0):
    b, l, _ = z.shape
    cpb = LA_CPB
    n = cpb * CHUNK
    nb = l // n
    cr, sr, cc, sc = rope
    fw = lambda j: pl.BlockSpec((1, n, GROUP_W), lambda i, c: (i, c, j))
    bw = lambda j: pl.BlockSpec((1, n, GROUP_W), lambda i, c: (i, nb - 1 - c, j))
    rowf = pl.BlockSpec((cpb, 1, GROUP_W), lambda i, c: (c, 0, 0))
    rowb = pl.BlockSpec((cpb, 1, GROUP_W), lambda i, c: (nb - 1 - c, 0, 0))
    colt = pl.BlockSpec((GRID_W, GROUP_W), lambda i, c: (0, 0))
    full = lambda a: pl.BlockSpec(a.shape, lambda i, c: (0,) * a.ndim)
    st = pl.BlockSpec((1, 2, HEADS, HEAD_V, RET_DK), lambda i, c: (i, 0, 0, 0, 0))
    return pl.pallas_call(
        functools.partial(_ret_scan_kernel, nb, cpb),
        grid=(b, nb),
        in_specs=[fw(0), fw(1), fw(2), bw(0), bw(1), bw(2)] + [full(a) for a in tables]
                 + [rowf, rowf, rowb, rowb, colt, colt, st],
        out_specs=[fw(0), bw(0), st],
        out_shape=[jax.ShapeDtypeStruct((b, l, GROUP_W), F32)] * 2
                  + [jax.ShapeDtypeStruct((b, 2, HEADS, HEAD_V, RET_DK), F32)],
        scratch_shapes=[pltpu.VMEM((2, HEADS, HEAD_V, RET_DK), F32)],
        compiler_params=_cparams("parallel", "arbitrary"),
        name="retention_scan",
    )(z, z, z, z, z, z, *tables, cr, sr, cr, sr, cc, sc, s0)


def _gla_scan(z, aup, abias, s0):
    b, l, _ = z.shape
    cpb = LA_CPB
    nb = l // (cpb * CHUNK)
    blk = lambda w, j, rev: pl.BlockSpec(
        (1, cpb * CHUNK, w), (lambda i, c: (i, nb - 1 - c, j)) if rev else (lambda i, c: (i, c, j)))
    st = pl.BlockSpec((1, 2, HEADS, HEAD_V, GLA_DK), lambda i, c: (i, 0, 0, 0, 0))
    ofw = pl.BlockSpec((1, cpb * CHUNK, GROUP_W), lambda i, c: (i, c, 0))
    obw = pl.BlockSpec((1, cpb * CHUNK, GROUP_W), lambda i, c: (i, nb - 1 - c, 0))
    return pl.pallas_call(
        functools.partial(_gla_scan_kernel, nb, cpb),
        grid=(b, nb),
        in_specs=[blk(GLA_QK, 0, False), blk(GLA_QK, 1, False), blk(GROUP_W, 1, False), blk(128, 6, False),
                  blk(GLA_QK, 0, True), blk(GLA_QK, 1, True), blk(GROUP_W, 1, True), blk(128, 6, True),
                  pl.BlockSpec(aup.shape, lambda i, c: (0, 0, 0)),
                  pl.BlockSpec(abias.shape, lambda i, c: (0, 0, 0)), st],
        out_specs=[ofw, obw, st],
        out_shape=[jax.ShapeDtypeStruct((b, l, GROUP_W), F32)] * 2
                  + [jax.ShapeDtypeStruct((b, 2, HEADS, HEAD_V, GLA_DK), F32)],
        scratch_shapes=[pltpu.VMEM((2, HEADS, HEAD_V, GLA_DK), F32)],
        compiler_params=_cparams("parallel", "arbitrary"),
        name="gla_scan",
    )(z, z, z, z, z, z, z, z, aup, abias, s0)


def _mix_kernel(x_ref, g1_ref, y5_ref, u5_ref, d5_ref, gw_ref, gb_ref,
                ryf_ref, ryb_ref, rbonus_ref, rg_ref, rlng_ref, rlnb_ref,
                tof_ref, tob_ref, tg_ref, tln_ref,
                gof_ref, gob_ref, gg_ref, gln_ref,
                bones_ref, wo_ref, o_ref, y5t_ref):
    bones = bones_ref[...]
    inv = 1.0 / HEAD_V

    def hmean(a):
        return _mm_right01(a, bones) * inv

    nch = y5_ref.shape[2]
    gph = LANES // S5_P
    y5g = [y5_ref[0, g] for g in range(S5_G)]
    for s in range(S5_T):
        for hf in range(GROUP_W // LANES):
            y5t_ref[hf, pl.ds(s, nch, stride=S5_T), :] = jnp.concatenate(
                [yg[:, s * S5_P:(s + 1) * S5_P] for yg in y5g[hf * gph:(hf + 1) * gph]], axis=1)
    y = jnp.concatenate([y5t_ref[0], y5t_ref[1]], axis=1) + d5_ref[...] * u5_ref[0]
    y = _gelu_tanh(y)
    ya = y * _sigmoid(_mm(y, gw_ref[...]) + gb_ref[...])
    yr = ryf_ref[0] + ryb_ref[0]
    dlt = yr - hmean(yr)
    yn = dlt * lax.rsqrt(hmean(dlt * dlt) + GN_EPS)
    yb = (yn * rlng_ref[...] + rlnb_ref[...] + rbonus_ref[0]) * rg_ref[0]
    ot = tof_ref[0] + tob_ref[0]
    yc = ot * lax.rsqrt(hmean(ot * ot) + EPS) * tln_ref[...] * _silu(tg_ref[0])
    og = gof_ref[0] + gob_ref[0]
    yd = og * lax.rsqrt(hmean(og * og) + EPS) * gln_ref[...] * _silu(gg_ref[0])
    mix = (_mm(ya, wo_ref[0]) + _mm(yb, wo_ref[1]) + _mm(yc, wo_ref[2]) + _mm(yd, wo_ref[3]))
    o_ref[0] = x_ref[0] + g1_ref[0] * mix


def _mix(x, g1, y5, u5, rw_pre, rw_y, zret, ret_o, zgla, gla_o, p, tm):
    b, l, d = x.shape
    tok = lambda w, j: pl.BlockSpec((1, tm, w), lambda i, t: (i, t, j))
    vec = lambda a: pl.BlockSpec(a.shape, lambda i, t: (0,) * a.ndim)
    t256 = tok(GROUP_W, 0)
    args = [x, g1, y5, u5, p['s5_d'], p['glu_w'], p['glu_b'],
            rw_y[0], rw_y[1], rw_pre[4], rw_pre[3], p['rw_ln_g'], p['rw_ln_b'],
            ret_o[0], ret_o[1], zret, p['ret_ln_g'],
            gla_o[0], gla_o[1], zgla, p['gla_ln_g'],
            p['bones'], p['w_out']]
    grouped = pl.BlockSpec((1, S5_G, tm // S5_T, S5_TP), lambda i, t: (i, 0, t, 0))
    specs = [tok(d, 0), pl.BlockSpec((1, 1, d), lambda i, t: (i, 0, 0)), grouped, t256,
             vec(p['s5_d']), vec(p['glu_w']), vec(p['glu_b']),
             t256, t256, t256, t256, vec(p['rw_ln_g']), vec(p['rw_ln_b']),
             t256, t256, tok(GROUP_W, 3), vec(p['ret_ln_g']),
             t256, t256, tok(GROUP_W, 2), vec(p['gla_ln_g']),
             vec(p['bones']), vec(p['w_out'])]
    return pl.pallas_call(
        _mix_kernel,
        grid=(b, l // tm),
        in_specs=specs,
        out_specs=tok(d, 0),
        out_shape=jax.ShapeDtypeStruct((b, l, d), F32),
        scratch_shapes=[pltpu.VMEM((GROUP_W // LANES, tm, LANES), F32)],
        compiler_params=_cparams("parallel", "parallel"),
        name="mix_outproj",
    )(*args)


def _mlp_kernel(final, nff, x_ref, g_ref, sc_ref, sh_ref, gate_ref, w1_ref, w2_ref, fg_ref, o_ref):
    x = x_ref[0]
    hb = _modnorm(x, g_ref[...], sc_ref[0], sh_ref[0]).astype(BF16)
    ff = w1_ref.shape[1] // nff
    acc = None
    for j in range(nff):
        a = jnp.maximum(jnp.dot(hb, w1_ref[:, j * ff:(j + 1) * ff], preferred_element_type=F32), 0.0)
        part = jnp.dot((a * a).astype(BF16), w2_ref[j * ff:(j + 1) * ff, :], preferred_element_type=F32)
        acc = part if acc is None else acc + part
    y = x + gate_ref[0] * acc
    if final:
        ms = jnp.mean(y * y, axis=-1, keepdims=True)
        y = y * lax.rsqrt(ms + EPS) * fg_ref[...]
    o_ref[0] = y


def _mlp(x, g, sc, sh, gate, w1, w2, final_g, final, tm):
    b, l, d = x.shape
    tok = pl.BlockSpec((1, tm, d), lambda i, t: (i, t, 0))
    vec = pl.BlockSpec((1, 1, d), lambda i, t: (i, 0, 0))
    row = pl.BlockSpec((1, d), lambda i, t: (0, 0))
    once = lambda a: pl.BlockSpec(a.shape, lambda i, t: (0, 0), pipeline_mode=pl.Buffered(1))
    return pl.pallas_call(
        functools.partial(_mlp_kernel, final, 4),
        grid=(b, l // tm),
        in_specs=[tok, row, vec, vec, vec, once(w1), once(w2), row],
        out_specs=tok,
        out_shape=jax.ShapeDtypeStruct((b, l, d), F32),
        compiler_params=_cparams("parallel", "parallel"),
        name="mlp",
    )(x, g, sc, sh, gate, w1, w2, final_g)


def kernel(x, c, ctx, c_ctx, ada_w, ada_b, norm1_g, norm2_g, w_in, w_out, s5_lam_re, s5_lam_im, s5_log_dt, s5_b_re, s5_b_im, s5_c_re, s5_c_im, s5_d, s5_glu_w, s5_glu_b, rw_mu, rw_w0, rw_w_up, rw_a0, rw_a_up, rw_g_up, rw_k_k, rw_k_a, rw_r_k, rw_ln_g, rw_ln_b, ret_decay_logit, ret_ln_g, gla_a_up, gla_a_b, gla_ln_g, mlp_w1, mlp_w2, final_g):
    b, l, d = x.shape
    lc = ctx.shape[1]
    depth = ada_w.shape[0]
    rows = l // GRID_W
    assert l % 512 == 0 and lc % CHUNK == 0 and lc % S5_T == 0 and d % 128 == 0

    cond = jnp.zeros((8, d), F32).at[:b].set(c).at[b].set(c_ctx)
    mod = _modulation(cond, ada_w, ada_b)

    lane = jnp.arange(GROUP_W)
    bones = (lane[:, None] // HEAD_V == lane[None, :] // HEAD_V).astype(BF16)
    rope_lat = _rope_tables(rows)
    ident = (jnp.ones((lc // CHUNK, 1, GROUP_W), F32), jnp.zeros((lc // CHUNK, 1, GROUP_W), F32),
             jnp.ones((GRID_W, GROUP_W), F32), jnp.zeros((GRID_W, GROUP_W), F32))
    nlev = max(1, (l // S5_T - 1).bit_length())
    c0, c1, c2 = GROUP_W, GROUP_W + RW_COLS, GROUP_W + RW_COLS + 4 * GROUP_W
    row2 = lambda a: a.reshape(1, -1).astype(F32)

    xc = ctx
    for i in range(depth):
        last = i == depth - 1
        m = mod[i].reshape(8, N_ADA, d)
        ml = m[:b, :, None, :]
        mc = jnp.broadcast_to(m[b][None, :, None, :], (b, N_ADA, 1, d))
        wi = w_in[i].astype(BF16)
        w5, wr, wt = wi[:, :c0], wi[:, c0:c1], wi[:, c1:c2]
        wg = jnp.pad(wi[:, c2:], ((0, 0), (0, GLA_COLS_PAD - (wi.shape[1] - c2))))
        n1 = row2(norm1_g[i])
        n2 = row2(norm2_g[i])

        s5_tab = _s5_tables(s5_lam_re[i], s5_lam_im[i], s5_log_dt[i], s5_b_re[i], s5_b_im[i],
                            s5_c_re[i], s5_c_im[i], nlev)
        zpad = lambda a, lo, n: jnp.zeros((2, 128, n), F32).at[:, lo:lo + a.shape[1]].set(a).astype(BF16)
        rwp = dict(
            mu=row2(rw_mu[i]),
            w0=rw_w0[i].reshape(2, 1, GROUP_W), a0=rw_a0[i].reshape(2, 1, GROUP_W),
            wup=zpad(rw_w_up[i], 0, GROUP_W), aup=zpad(rw_a_up[i], RW_W_RANK, GROUP_W),
            gup=jnp.zeros((128, GROUP_W), F32).at[RW_W_RANK + RW_A_RANK:].set(rw_g_up[i]).astype(BF16),
            kk=row2(rw_k_k[i]), ka=row2(rw_k_a[i]), rk=row2(rw_r_k[i]), bones=bones)
        ret_tab = _ret_tables(ret_decay_logit[i], LA_CPB * CHUNK)
        gla_aup = zpad(gla_a_up[i], 0, GLA_QK)
        gla_ab = gla_a_b[i].reshape(2, 1, GLA_QK).astype(F32)
        mixp = dict(s5_d=row2(s5_d[i]), glu_w=s5_glu_w[i].astype(BF16), glu_b=row2(s5_glu_b[i]),
                    rw_ln_g=row2(rw_ln_g[i]), rw_ln_b=row2(rw_ln_b[i]), ret_ln_g=row2(ret_ln_g[i]),
                    gla_ln_g=row2(gla_ln_g[i]), bones=bones,
                    w_out=w_out[i].astype(BF16).reshape(4, GROUP_W, d))
        w1 = mlp_w1[i].astype(BF16)
        w2 = mlp_w2[i].astype(BF16)

        def mixers(xx, mm, is_lat, states):
            tm = 512 if is_lat else lc
            z5, z5g, zr, zt, zg = _inproj(xx, n1, mm[:, 1], mm[:, 0], w5, wr, wt, wg, tm)
            y5, h5 = _s5_scan(z5g, s5_tab, states[0])
            pre = _rw_prep(zr, is_lat, rwp)
            yrf, yrb, srw = _rw_scan(pre, states[1])
            otf, otb, sret = _ret_scan(zt, ret_tab, rope_lat if is_lat else ident, states[2])
            ogf, ogb, sgla = _gla_scan(zg, gla_aup, gla_ab, states[3])
            outs = (z5, y5, pre, (yrf, yrb), zt, (otf, otb), zg, (ogf, ogb))
            return outs, (h5, srw, sret, sgla)

        def block(xx, mm, outs, is_lat, fin):
            tm = 512 if is_lat else lc
            z5, y5, pre, yr, zt, ot, zg, og = outs
            x1 = _mix(xx, mm[:, 2], y5, z5, pre, yr, zt, ot, zg, og, mixp, tm)
            return _mlp(x1, n2, mm[:, 4], mm[:, 3], mm[:, 5], w1, w2, row2(final_g), fin, tm)

        zeros = (jnp.zeros((b, 2, S5_G, 1, 2 * S5_N), F32),
                 jnp.zeros((b, 2, HEADS, HEAD_V, HEAD_V), F32),
                 jnp.zeros((b, 2, HEADS, HEAD_V, RET_DK), F32),
                 jnp.zeros((b, 2, HEADS, HEAD_V, GLA_DK), F32))
        outs_c, st_c = mixers(xc, mc, False, zeros)
        outs_l, _ = mixers(x, ml, True, st_c)
        x = block(x, ml, outs_l, True, last)
        if not last:
            xc = block(xc, mc, outs_c, False, False)
    return x
```

```python
import functools
import math

import jax
import jax.numpy as jnp
from jax import lax
from jax.experimental import pallas as pl
from jax.experimental.pallas import tpu as pltpu

F32 = jnp.float32
BF16 = jnp.bfloat16

LANES = 128
GRID_W = 64
GROUP_W = 256
N_ADA = 6
EPS = 1e-6
GN_EPS = 64e-5
CHUNK = 64
RW_CPB = 2
LA_CPB = 4
HEADS = 4
HEAD_V = 64
S5_P = 16
S5_G = 16
S5_N = 64
S5_T = 32
S5_TP = S5_T * S5_P
RW_COLS = 896
RW_W_RANK = 32
RW_A_RANK = 32
RW_G_RANK = 64
RET_DK = 64
GLA_DK = 32
GLA_QK = 128
GLA_RANK = 16
GLA_TAU = 16.0
GLA_COLS_PAD = 896
ROPE_BASE = 10000.0
VMEM_LIMIT = 56 * 1024 * 1024


def _cparams(*sem):
    return pltpu.CompilerParams(dimension_semantics=sem, vmem_limit_bytes=VMEM_LIMIT)


def _mm(a, b):
    return jnp.dot(a.astype(BF16), b.astype(BF16), preferred_element_type=F32)


def _mm_nt(a, b):
    return lax.dot_general(a.astype(BF16), b.astype(BF16), (((1,), (1,)), ((), ())),
                           preferred_element_type=F32)


def _mm_tn(a, b):
    return lax.dot_general(a.astype(BF16), b.astype(BF16), (((0,), (0,)), ((), ())),
                           preferred_element_type=F32)


def _split2(x):
    hi = x.astype(BF16)
    return hi, (x - hi.astype(F32)).astype(BF16)


def _join2(p):
    return p[0].astype(F32) + p[1].astype(F32)


def _mm_x3p(a, b):
    return (jnp.dot(a[0], b[0], preferred_element_type=F32) + jnp.dot(a[0], b[1], preferred_element_type=F32)
            + jnp.dot(a[1], b[0], preferred_element_type=F32))


def _chunk_tri01(n, reverse):
    ri = lax.broadcasted_iota(jnp.int32, (n, n), 0)
    ci = lax.broadcasted_iota(jnp.int32, (n, n), 1)
    shift = CHUNK.bit_length() - 1
    tri = (ci >= ri) if reverse else (ci <= ri)
    return jnp.where((ri >> shift) == (ci >> shift), jnp.where(tri, 1.0, 0.0), 0.0).astype(BF16)


def _split3(x):
    hi = x.astype(BF16)
    r1 = x - hi.astype(F32)
    mid = r1.astype(BF16)
    lo = (r1 - mid.astype(F32)).astype(BF16)
    return hi, mid, lo


def _mm_left01(m01, x):
    hi, mid, lo = _split3(x)
    return (jnp.dot(m01, hi, preferred_element_type=F32)
            + jnp.dot(m01, mid, preferred_element_type=F32)
            + jnp.dot(m01, lo, preferred_element_type=F32))


def _mm_right01(x, m01):
    hi, mid, lo = _split3(x)
    return (jnp.dot(hi, m01, preferred_element_type=F32)
            + jnp.dot(mid, m01, preferred_element_type=F32)
            + jnp.dot(lo, m01, preferred_element_type=F32))


def _sigmoid(x):
    return 1.0 / (1.0 + jnp.exp(-x))


def _softplus(x):
    return jnp.maximum(x, 0.0) + jnp.log(1.0 + jnp.exp(-jnp.abs(x)))


def _silu(x):
    return x * _sigmoid(x)


def _gelu_tanh(x):
    return 0.5 * x * (1.0 + jnp.tanh(math.sqrt(2.0 / math.pi) * (x + 0.044715 * x * x * x)))


def _tri_masks(t, reverse):
    ri = lax.broadcasted_iota(jnp.int32, (t, t), 0)
    ci = lax.broadcasted_iota(jnp.int32, (t, t), 1)
    if reverse:
        return ci >= ri, ci > ri, ci == ri
    return ci <= ri, ci < ri, ci == ri


def _row_to_col(row, eye):
    n = row.shape[1]
    return jnp.sum(jnp.where(eye, jnp.broadcast_to(row, (n, n)), 0.0), axis=1, keepdims=True)


def _cast_kernel(x_ref, o_ref):
    o_ref[...] = x_ref[...].astype(o_ref.dtype)


def _to_bf16(w):
    depth, r, c = w.shape
    tr = 256
    spec = pl.BlockSpec((1, tr, c), lambda i, j: (i, j, 0))
    return pl.pallas_call(
        _cast_kernel,
        grid=(depth, r // tr),
        in_specs=[spec],
        out_specs=spec,
        out_shape=jax.ShapeDtypeStruct(w.shape, BF16),
        compiler_params=_cparams("parallel", "parallel"),
        name="cast_bf16",
    )(w)


def _mod_kernel(cond_ref, w_ref, b_ref, o_ref):
    c = cond_ref[...]
    o_ref[0] = _mm(_silu(c), w_ref[0]) + b_ref[0]


def _modulation(cond, ada_w, ada_b):
    depth, d, n = ada_w.shape
    tn = 1536
    return pl.pallas_call(
        _mod_kernel,
        grid=(depth, n // tn),
        in_specs=[pl.BlockSpec((8, d), lambda i, j: (0, 0)),
                  pl.BlockSpec((1, d, tn), lambda i, j: (i, 0, j)),
                  pl.BlockSpec((1, 1, tn), lambda i, j: (i, 0, j))],
        out_specs=pl.BlockSpec((1, 8, tn), lambda i, j: (i, 0, j)),
        out_shape=jax.ShapeDtypeStruct((depth, 8, n), F32),
        compiler_params=_cparams("parallel", "parallel"),
        name="adaln_mod",
    )(cond, ada_w, ada_b.reshape(depth, 1, n))


def _modnorm(x, g, sc, sh):
    ms = jnp.mean(x * x, axis=-1, keepdims=True)
    return x * lax.rsqrt(ms + EPS) * g * (1.0 + sc) + sh


def _inproj_kernel(x_ref, g_ref, sc_ref, sh_ref, w5_ref, wr_ref, wt_ref, wg_ref,
                   o5_ref, o5g_ref, or_ref, ot_ref, og_ref, z5h_ref):
    hb = _modnorm(x_ref[0], g_ref[...], sc_ref[0], sh_ref[0]).astype(BF16)
    z5 = jnp.dot(hb, w5_ref[...], preferred_element_type=F32)
    o5_ref[0] = z5
    or_ref[0] = jnp.dot(hb, wr_ref[...], preferred_element_type=F32)
    ot_ref[0] = jnp.dot(hb, wt_ref[...], preferred_element_type=F32)
    og_ref[0] = jnp.dot(hb, wg_ref[...], preferred_element_type=F32)
    nch = o5g_ref.shape[2]
    gph = LANES // S5_P
    for hf in range(GROUP_W // LANES):
        z5h_ref[hf] = z5[:, hf * LANES:(hf + 1) * LANES]
    for s in range(S5_T):
        for hf in range(GROUP_W // LANES):
            rows = z5h_ref[hf, pl.ds(s, nch, stride=S5_T), :]
            for g in range(gph):
                o5g_ref[0, hf * gph + g, :, s * S5_P:(s + 1) * S5_P] = rows[:, g * S5_P:(g + 1) * S5_P]


def _inproj(x, g, sc, sh, w5, wr, wt, wg, tm):
    b, l, d = x.shape
    widths = (w5.shape[1], wr.shape[1], wt.shape[1], wg.shape[1])
    tok = lambda n: pl.BlockSpec((1, tm, n), lambda i, j: (i, j, 0))
    vec = pl.BlockSpec((1, 1, d), lambda i, j: (i, 0, 0))
    full = lambda w: pl.BlockSpec(w.shape, lambda i, j: (0, 0))
    grouped = pl.BlockSpec((1, S5_G, tm // S5_T, S5_TP), lambda i, j: (i, 0, j, 0))
    return pl.pallas_call(
        _inproj_kernel,
        grid=(b, l // tm),
        in_specs=[tok(d), pl.BlockSpec((1, d), lambda i, j: (0, 0)), vec, vec,
                  full(w5), full(wr), full(wt), full(wg)],
        out_specs=[tok(widths[0]), grouped] + [tok(n) for n in widths[1:]],
        out_shape=[jax.ShapeDtypeStruct((b, l, widths[0]), F32),
                   jax.ShapeDtypeStruct((b, S5_G, l // S5_T, S5_TP), F32)]
                  + [jax.ShapeDtypeStruct((b, l, n), F32) for n in widths[1:]],
        scratch_shapes=[pltpu.VMEM((GROUP_W // LANES, tm, LANES), F32)],
        compiler_params=_cparams("parallel", "parallel"),
        name="norm_inproj",
    )(x, g, sc, sh, w5, wr, wt, wg)


def _s5_tables(lam_re, lam_im, log_dt, b_re, b_im, c_re, c_im, nlev):
    hp = lax.Precision.HIGHEST
    t = S5_T
    lam = lax.complex(jnp.minimum(lam_re.astype(F32), -1e-4), lam_im.astype(F32))
    ldt = lam * jnp.exp(log_dt.astype(F32))[..., None]
    a_bar = jnp.exp(ldt)
    bb = ((a_bar - 1.0) / lam)[..., None] * lax.complex(b_re.astype(F32), b_im.astype(F32))
    cm = lax.complex(c_re.astype(F32), c_im.astype(F32))
    tau = jnp.arange(t + 1, dtype=F32)
    apow = jnp.exp(ldt[:, :, None, :] * tau[None, None, :, None])
    taps = jnp.einsum('dgpn,dgtn,dgnq->dgtpq', cm, apow[:, :, :t], bb, precision=hp).real
    taprow = jnp.stack([taps[0], taps[1][:, ::-1]]).transpose(0, 1, 4, 2, 3).reshape(2, S5_G, S5_P, S5_TP)

    def pack(zc):
        return jnp.concatenate([zc.real, zc.imag], axis=-1)

    win_f = apow[0][:, t - 1 - jnp.arange(t), None, :] * bb[0].transpose(0, 2, 1)[:, None]
    win_b = apow[1][:, jnp.arange(t), None, :] * bb[1].transpose(0, 2, 1)[:, None]
    win = jnp.stack([pack(win_f), pack(win_b)]).reshape(2, S5_G, S5_TP, 2 * S5_N)
    ca_f = cm[0][:, None] * apow[0][:, 1 + jnp.arange(t), None, :]
    ca_b = cm[1][:, None] * apow[1][:, t - jnp.arange(t), None, :]

    def outpack(ca):
        w = jnp.concatenate([ca.real, -ca.imag], axis=-1)
        return w.reshape(S5_G, S5_TP, 2 * S5_N).transpose(0, 2, 1)

    wout = jnp.stack([outpack(ca_f), outpack(ca_b)])
    lev = (2.0 ** jnp.arange(nlev, dtype=F32)) * t
    pw = jnp.exp(ldt[:, :, None, :] * lev[None, None, :, None])
    p1 = jnp.concatenate([pw.real, pw.real], axis=-1)
    p2 = jnp.concatenate([-pw.imag, pw.imag], axis=-1)
    pw = jnp.stack([p1, p2], axis=3)
    return taprow, win.astype(BF16), wout.astype(BF16), pw


def _s5_kernel(nc, nlev, u_ref, tap_ref, win_ref, wout_ref, pw_ref, h0_ref, y_ref, hfin_ref, conv_ref):
    u = u_ref[0, 0].astype(BF16)
    row = lax.broadcasted_iota(jnp.int32, (nc, 2 * S5_N), 0)
    lane = lax.broadcasted_iota(jnp.int32, (S5_P, S5_TP), 1)
    for s in range(S5_T):
        lo = s * S5_P
        fwd = tap_ref[0, 0] if s == 0 else jnp.where(lane >= lo, pltpu.roll(tap_ref[0, 0], lo, 1), 0.0)
        hi = lo + S5_P
        bwd = tap_ref[1, 0] if hi == S5_TP else jnp.where(lane < hi, pltpu.roll(tap_ref[1, 0], hi, 1), 0.0)
        conv_ref[0, lo:hi, :] = fwd.astype(BF16)
        conv_ref[1, lo:hi, :] = bwd.astype(BF16)

    def cmul(x, d, j):
        return pw_ref[d, 0, j, 0:1] * x + pw_ref[d, 0, j, 1:2] * pltpu.roll(x, S5_N, 1)

    y = None
    for d in range(2):
        v = jnp.dot(u, win_ref[d, 0], preferred_element_type=F32)
        h0 = h0_ref[0, d, 0]
        if d == 0:
            x = jnp.where(row == 0, h0, pltpu.roll(v, 1, 0))
        else:
            x = jnp.where(row == nc - 1, h0, pltpu.roll(v, nc - 1, 0))
        for j in range(nlev):
            sh = 2 ** j
            if d == 0:
                xs = jnp.where(row >= sh, pltpu.roll(x, sh, 0), 0.0)
            else:
                xs = jnp.where(row < nc - sh, pltpu.roll(x, nc - sh, 0), 0.0)
            x = x + cmul(xs, d, j)
        last = nc - 1 if d == 0 else 0
        hfin_ref[0, d, 0] = cmul(x[last:last + 1], d, 0) + v[last:last + 1]
        yd = (jnp.dot(u, conv_ref[d], preferred_element_type=F32)
              + _mm(x, wout_ref[d, 0]))
        y = yd if y is None else y + yd
    y_ref[0, 0] = y


def _s5_scan(uf, tables, h0):
    conv, win, wout, pw = tables
    b, _, nc, _ = uf.shape
    nlev = max(1, (nc - 1).bit_length())
    pw = pw[:, :, :nlev]
    n2 = 2 * S5_N
    y, hfin = pl.pallas_call(
        functools.partial(_s5_kernel, nc, nlev),
        grid=(b, S5_G),
        in_specs=[pl.BlockSpec((1, 1, nc, S5_TP), lambda i, g: (i, g, 0, 0)),
                  pl.BlockSpec((2, 1, S5_P, S5_TP), lambda i, g: (0, g, 0, 0)),
                  pl.BlockSpec((2, 1, S5_TP, n2), lambda i, g: (0, g, 0, 0)),
                  pl.BlockSpec((2, 1, n2, S5_TP), lambda i, g: (0, g, 0, 0)),
                  pl.BlockSpec((2, 1, nlev, 2, n2), lambda i, g: (0, g, 0, 0, 0)),
                  pl.BlockSpec((1, 2, 1, 1, n2), lambda i, g: (i, 0, g, 0, 0))],
        out_specs=[pl.BlockSpec((1, 1, nc, S5_TP), lambda i, g: (i, g, 0, 0)),
                   pl.BlockSpec((1, 2, 1, 1, n2), lambda i, g: (i, 0, g, 0, 0))],
        out_shape=[jax.ShapeDtypeStruct((b, S5_G, nc, S5_TP), F32),
                   jax.ShapeDtypeStruct((b, 2, S5_G, 1, n2), F32)],
        scratch_shapes=[pltpu.VMEM((2, S5_TP, S5_TP), BF16)],
        compiler_params=_cparams("parallel", "parallel"),
        name="s5_scan",
    )(uf, conv, win, wout, pw, h0)
    return y, hfin


def _rw_prep_kernel(grid_shift, nt, z_ref, zp_ref, zn_ref, mu_ref, w0_ref, wup_ref, a0_ref, aup_ref,
                    gup_ref, kk_ref, ka_ref, rk_ref, bones_ref,
                    r_ref, v_ref, kkn_ref, g_ref, bonus_ref,
                    lw0_ref, kd0_ref, as0_ref, lw1_ref, kd1_ref, as1_ref):
    i = pl.program_id(1)
    z = z_ref[0]
    tm = z.shape[0]
    row = lax.broadcasted_iota(jnp.int32, z.shape, 0)
    lane = lax.broadcasted_iota(jnp.int32, z.shape, 1)
    prev1 = pltpu.roll(z, 1, 0)
    next1 = pltpu.roll(z, tm - 1, 0)
    if grid_shift:
        col = row & (GRID_W - 1)
        left = jnp.where(col == 0, 0.0, prev1)
        right = jnp.where(col == GRID_W - 1, 0.0, next1)
        zp = jnp.where(i > 0, zp_ref[0], 0.0)
        zn = jnp.where(i < nt - 1, zn_ref[0], 0.0)
        up = jnp.concatenate([zp, z[:tm - GRID_W]], axis=0)
        down = jnp.concatenate([z[GRID_W:], zn], axis=0)
        q = RW_COLS // 4
        shifted = jnp.where(lane < q, left,
                            jnp.where(lane < 2 * q, right, jnp.where(lane < 3 * q, up, down)))
    else:
        prev = jnp.where(row == 0, 0.0, prev1)
        nxt = jnp.where(row == tm - 1, 0.0, next1)
        shifted = jnp.where(lane < RW_COLS // 2, prev, nxt)
    zm = z + mu_ref[...] * (shifted - z)
    r = zm[:, 0:256]
    k = zm[:, 256:512]
    v = zm[:, 512:768]
    lo = zm[:, 768:896]
    bones = bones_ref[...]
    g_ref[0] = _mm(_sigmoid(lo), gup_ref[...]).astype(BF16)
    kk = k * kk_ref[...]
    kk = kk * lax.rsqrt(_mm_right01(kk * kk, bones) + 1e-12)
    r_ref[0] = r.astype(BF16)
    v_ref[0] = v.astype(BF16)
    kkn_ref[0] = kk.astype(BF16)
    bonus_ref[0] = (_mm_right01(r * k * rk_ref[...], bones) * v).astype(BF16)
    tlo = jnp.tanh(lo)
    for d, (lw_ref, kd_ref, as_ref) in enumerate(((lw0_ref, kd0_ref, as0_ref), (lw1_ref, kd1_ref, as1_ref))):
        w_raw = -_softplus(-(w0_ref[d] + _mm(tlo, wup_ref[d]))) - 0.5
        lw_ref[0] = -jnp.exp(w_raw)
        a = _sigmoid(a0_ref[d] + _mm(lo, aup_ref[d]))
        kd_ref[0] = (k * (1.0 + (a - 1.0) * ka_ref[...])).astype(BF16)
        as_ref[0] = a.astype(BF16)


def _rw_prep(z, grid_shift, p):
    b, l, _ = z.shape
    tm = 256 if grid_shift else l
    nt = l // tm
    hb = tm // GRID_W
    nhb = l // GRID_W
    tok = pl.BlockSpec((1, tm, GROUP_W), lambda i, j: (i, j, 0))
    full = lambda a: pl.BlockSpec(a.shape, lambda i, j: (0,) * a.ndim)
    params = (p['mu'], p['w0'], p['wup'], p['a0'], p['aup'], p['gup'], p['kk'], p['ka'], p['rk'], p['bones'])
    return pl.pallas_call(
        functools.partial(_rw_prep_kernel, grid_shift, nt),
        grid=(b, nt),
        in_specs=[pl.BlockSpec((1, tm, RW_COLS), lambda i, j: (i, j, 0)),
                  pl.BlockSpec((1, GRID_W, RW_COLS), lambda i, j: (i, jnp.maximum(j * hb - 1, 0), 0)),
                  pl.BlockSpec((1, GRID_W, RW_COLS), lambda i, j: (i, jnp.minimum((j + 1) * hb, nhb - 1), 0))]
                 + [full(a) for a in params],
        out_specs=[tok] * 11,
        out_shape=[jax.ShapeDtypeStruct((b, l, GROUP_W), dt)
                   for dt in (BF16,) * 5 + (F32, BF16, BF16) * 2],
        compiler_params=_cparams("parallel", "parallel"),
        name="rwkv_prep",
    )(z, z, z, *params)


def _rw_scan_kernel(nb, cpb, rf_ref, vf_ref, kkf_ref, lwf_ref, kdf_ref, asf_ref,
                    rb_ref, vb_ref, kkb_ref, lwb_ref, kdb_ref, asb_ref, s0_ref,
                    yf_ref, yb_ref, sfin_ref, st_ref):
    c = pl.program_id(1)

    @pl.when(c == 0)
    def _():
        st_ref[...] = s0_ref[0]

    t = CHUNK
    ri = lax.broadcasted_iota(jnp.int32, (t, t), 0)
    ci = lax.broadcasted_iota(jnp.int32, (t, t), 1)
    eye = ri == ci
    eyef = jnp.where(eye, 1.0, 0.0)
    same = lambda s: jnp.where((ri >> s) == (ci >> s), 1.0, 0.0)
    m4, m8, m16, m32 = same(2), same(3), same(4), same(5)
    merge_masks = (m8 - m4, m16 - m8, m32 - m16, 1.0 - m32)
    dir_refs = ((rf_ref, vf_ref, kkf_ref, lwf_ref, kdf_ref, asf_ref),
                (rb_ref, vb_ref, kkb_ref, lwb_ref, kdb_ref, asb_ref))
    units = []
    for d, (r_ref, v_ref, kk_ref, lw_ref, kd_ref, as_ref) in enumerate(dir_refs):
        incl = (ci >= ri) if d == 1 else (ci <= ri)
        strict = (ci > ri) if d == 1 else (ci < ri)
        r, v, kk, kd = (x[0].astype(F32) for x in (r_ref, v_ref, kk_ref, kd_ref))
        lw = lw_ref[0]
        cin = _mm_left01(_chunk_tri01(cpb * t, d == 1), lw)
        e_in = jnp.exp(cin)
        e_neg = jnp.exp(-cin)
        rt = r * e_in
        at = -kk * jnp.exp(cin - lw)
        bvec = kk * as_ref[0].astype(F32)
        bt = bvec * e_neg
        kt = kd * e_neg
        for j in range(cpb):
            rows = slice(j * t, (j + 1) * t)
            last = j * t + (0 if d == 1 else t - 1)
            clast = cin[last:last + 1]
            dl = jnp.exp(clast - cin[rows])
            bh = bvec[rows] * dl
            kh = kd[rows] * dl
            dec = jnp.exp(clast)
            for h in range(HEADS):
                sl = slice(h * HEAD_V, (h + 1) * HEAD_V)
                units.append(dict(d=d, j=j, h=h, incl=incl, strict=strict,
                                  at=at[rows, sl], rt=rt[rows, sl], bt=bt[rows, sl], kt=kt[rows, sl],
                                  bh=bh[:, sl], kh=kh[:, sl], v=v[rows, sl], dec=dec[:, sl]))

    a_all = [_mm_nt(jnp.concatenate([u['at'], u['rt']], axis=0),
                    jnp.concatenate([u['bt'], u['kt']], axis=0)) for u in units]
    nmat = [jnp.where(u['strict'], a[0:t, 0:t], 0.0) for u, a in zip(units, a_all)]
    a_kk = [jnp.concatenate([jnp.where(u['strict'], a[0:t, t:2 * t], 0.0),
                             jnp.where(u['incl'], a[t:2 * t, t:2 * t], 0.0)], axis=0)
            for u, a in zip(units, a_all)]
    a_rb = [jnp.where(u['incl'], a[t:2 * t, 0:t], 0.0) for u, a in zip(units, a_all)]
    akv = [_mm(a, u['v']) for u, a in zip(units, a_kk)]
    kv = [_mm_tn(u['v'], u['kh']) for u in units]
    nd = [x * m4 for x in nmat]
    n2 = [_mm(x, x) for x in nd]
    tinv = [eyef + x + _mm(eyef + x, y) for x, y in zip(nd, n2)]
    for mk in merge_masks:
        w = [_mm(ti, x * mk) for ti, x in zip(tinv, nmat)]
        tinv = [ti + _mm(wi, ti) for ti, wi in zip(tinv, w)]
    zz = [_mm(ti, jnp.concatenate([u['at'], kvv[0:t]], axis=1)) for ti, u, kvv in zip(tinv, units, akv)]
    ght = [_mm_tn(z, u['bh']) for u, z in zip(units, zz)]
    qy = [_mm(a, z) for a, z in zip(a_rb, zz)]
    qmat = [u['rt'] + x[:, :HEAD_V] for u, x in zip(units, qy)]
    y0 = [x[:, HEAD_V:] + kvv[t:2 * t] for x, kvv in zip(qy, akv)]
    gmat = [x[:HEAD_V] for x in ght]
    hmat = [x[HEAD_V:] + k2 for x, k2 in zip(ght, kv)]
    idx = {(u['d'], u['j'], u['h']): n for n, u in enumerate(units)}
    state = {(d, h): st_ref[d, h] for d in range(2) for h in range(HEADS)}
    ys = {}
    for step in range(cpb):
        for d in range(2):
            j = step if d == 0 else cpb - 1 - step
            for h in range(HEADS):
                n = idx[(d, j, h)]
                st = state[(d, h)]
                ys[(d, j, h)] = _mm_nt(qmat[n], st) + y0[n]
                state[(d, h)] = units[n]['dec'] * st + _mm(st, gmat[n]) + hmat[n]
    for d, y_ref in enumerate((yf_ref, yb_ref)):
        y_ref[0] = jnp.concatenate(
            [jnp.concatenate([ys[(d, j, h)] for h in range(HEADS)], axis=1) for j in range(cpb)], axis=0)
        for h in range(HEADS):
            st_ref[d, h] = state[(d, h)]

    @pl.when(c == nb - 1)
    def _():
        sfin_ref[0] = st_ref[...]


def _rw_scan(pre, s0):
    r, v, kk, _, _, lw0, kd0, as0, lw1, kd1, as1 = pre
    b, l, _ = r.shape
    cpb = RW_CPB
    nb = l // (cpb * CHUNK)
    fw = pl.BlockSpec((1, cpb * CHUNK, GROUP_W), lambda i, c: (i, c, 0))
    bw = pl.BlockSpec((1, cpb * CHUNK, GROUP_W), lambda i, c: (i, nb - 1 - c, 0))
    st = pl.BlockSpec((1, 2, HEADS, HEAD_V, HEAD_V), lambda i, c: (i, 0, 0, 0, 0))
    return pl.pallas_call(
        functools.partial(_rw_scan_kernel, nb, cpb),
        grid=(b, nb),
        in_specs=[fw] * 6 + [bw] * 6 + [st],
        out_specs=[fw, bw, st],
        out_shape=[jax.ShapeDtypeStruct((b, l, GROUP_W), F32)] * 2
                  + [jax.ShapeDtypeStruct((b, 2, HEADS, HEAD_V, HEAD_V), F32)],
        scratch_shapes=[pltpu.VMEM((2, HEADS, HEAD_V, HEAD_V), F32)],
        compiler_params=_cparams("parallel", "arbitrary"),
        name="rwkv_scan",
    )(r, v, kk, lw0, kd0, as0, r, v, kk, lw1, kd1, as1, s0)


def _rope(x, cos, sin_signed):
    lane = lax.broadcasted_iota(jnp.int32, x.shape, 1)
    n = x.shape[1]
    swapped = jnp.where((lane & 16) == 0, pltpu.roll(x, n - 16, 1), pltpu.roll(x, 16, 1))
    return x * cos + swapped * sin_signed


def _ret_scan_kernel(nb, cpb, qf_ref, kf_ref, vf_ref, qb_ref, kb_ref, vb_ref,
                     dmat_ref, qdec_ref, kdec_ref, sdec_ref,
                     crf_ref, srf_ref, crb_ref, srb_ref, cc_ref, sc_ref, s0_ref,
                     of_ref, ob_ref, sfin_ref, st_ref):
    c = pl.program_id(1)

    @pl.when(c == 0)
    def _():
        st_ref[...] = s0_ref[0]

    t = CHUNK
    scale = RET_DK ** -0.5
    lane = lax.broadcasted_iota(jnp.int32, (t, GROUP_W), 1)
    by_row = (lane & 32) == 0
    qs, ks, qds, khs, vs = [], [], [], [], []
    for d, (q_ref, k_ref, v_ref) in enumerate(((qf_ref, kf_ref, vf_ref), (qb_ref, kb_ref, vb_ref))):
        cr_ref, sr_ref = (crf_ref, srf_ref) if d == 0 else (crb_ref, srb_ref)
        cos = jnp.concatenate([jnp.where(by_row, cr_ref[j], cc_ref[...]) for j in range(cpb)], axis=0)
        sin = jnp.concatenate([jnp.where(by_row, sr_ref[j], sc_ref[...]) for j in range(cpb)], axis=0)
        q = _rope(q_ref[0], cos, sin)
        k = _rope(k_ref[0] * scale, cos, sin)
        v = v_ref[0]
        qd = q * qdec_ref[d]
        kh = k * kdec_ref[d]
        for h in range(HEADS):
            sl = slice(h * HEAD_V, (h + 1) * HEAD_V)
            qs.append(q[:, sl])
            ks.append(k[:, sl])
            qds.append(qd[:, sl])
            khs.append(kh[:, sl])
            vs.append(v[:, sl])
    scores = [_mm_nt(q, k) * dmat_ref[i // HEADS, i % HEADS] for i, (q, k) in enumerate(zip(qs, ks))]
    states = [st_ref[i // HEADS, i % HEADS] for i in range(2 * HEADS)]
    outs = [_mm(p, v) + _mm_nt(qd, st) for p, v, qd, st in zip(scores, vs, qds, states)]
    for i, (st, kh, v) in enumerate(zip(states, khs, vs)):
        d, h = i // HEADS, i % HEADS
        st_ref[d, h] = st * sdec_ref[d][:, h * HEAD_V:(h + 1) * HEAD_V] + _mm_tn(v, kh)
    of_ref[0] = jnp.concatenate(outs[:HEADS], axis=1)
    ob_ref[0] = jnp.concatenate(outs[HEADS:], axis=1)

    @pl.when(c == nb - 1)
    def _():
        sfin_ref[0] = st_ref[...]


def _ret_tables(decay_logit, n):
    lg = jax.nn.log_sigmoid(decay_logit.astype(F32))
    pos = jnp.arange(n, dtype=F32)
    lag = pos[:, None] - pos[None, :]
    lag = jnp.stack([lag, -lag])
    dmat = jnp.where(lag[:, None] >= 0, jnp.exp(lg[:, :, None, None] * lag[:, None]), 0.0)
    lanes = jnp.repeat(lg, RET_DK, axis=-1)[:, None, :]
    qpow = jnp.stack([pos + 1.0, n - pos])[:, :, None]
    kpow = jnp.stack([n - 1.0 - pos, pos])[:, :, None]
    return dmat, jnp.exp(lanes * qpow), jnp.exp(lanes * kpow), jnp.exp(lanes * n)


def _gla_scan_kernel(nb, cpb, qf_ref, kf_ref, vf_ref, af_ref, qb_ref, kb_ref, vb_ref, ab_ref,
                     aup_ref, abias_ref, s0_ref, of_ref, ob_ref, sfin_ref, st_ref):
    c = pl.program_id(1)

    @pl.when(c == 0)
    def _():
        st_ref[...] = s0_ref[0]

    t = CHUNK
    n = cpb * t
    dk = GLA_DK
    scale = dk ** -0.5
    dirs = ((qf_ref, kf_ref, vf_ref, af_ref), (qb_ref, kb_ref, vb_ref, ab_ref))
    units = []
    for d, (q_ref, k_ref, v_ref, a_ref) in enumerate(dirs):
        incl = _tri_masks(t, d == 1)[0]
        q, k, v = q_ref[0], k_ref[0] * scale, v_ref[0]
        lw = -_softplus(-(_mm(a_ref[0], aup_ref[d]) + abias_ref[d])) * (1.0 / GLA_TAU)
        cin = _mm_left01(_chunk_tri01(n, d == 1), lw)
        qt = q * jnp.exp(cin)
        kt = k * jnp.exp(-cin)
        for j in range(cpb):
            rows = slice(j * t, (j + 1) * t)
            last = j * t + (0 if d == 1 else t - 1)
            clast = cin[last:last + 1]
            kh = k[rows] * jnp.exp(clast - cin[rows])
            dec = jnp.exp(clast)
            for h in range(HEADS):
                sk = slice(h * dk, (h + 1) * dk)
                sv = slice(h * HEAD_V, (h + 1) * HEAD_V)
                units.append(dict(d=d, j=j, h=h, incl=incl, qt=qt[rows, sk], kt=kt[rows, sk],
                                  kh=kh[:, sk], v=v[rows, sv], dec=dec[:, sk]))

    amat = [jnp.where(u['incl'], _mm_nt(u['qt'], u['kt']), 0.0) for u in units]
    intra = [_mm(a, u['v']) for a, u in zip(amat, units)]
    kv = [_mm_tn(u['v'], u['kh']) for u in units]
    idx = {(u['d'], u['j'], u['h']): i for i, u in enumerate(units)}
    entering = {}
    for d in range(2):
        for h in range(HEADS):
            st = st_ref[d, h]
            for step in range(cpb):
                j = step if d == 0 else cpb - 1 - step
                i = idx[(d, j, h)]
                entering[i] = st
                st = units[i]['dec'] * st + kv[i]
            st_ref[d, h] = st
    outs = [x + _mm_nt(u['qt'], entering[i]) for i, (x, u) in enumerate(zip(intra, units))]
    for d, o_ref in enumerate((of_ref, ob_ref)):
        o_ref[0] = jnp.concatenate(
            [jnp.concatenate([outs[idx[(d, j, h)]] for h in range(HEADS)], axis=1) for j in range(cpb)], axis=0)

    @pl.when(c == nb - 1)
    def _():
        sfin_ref[0] = st_ref[...]


def _rope_tables(rows):
    nf = RET_DK // 4
    inv = ROPE_BASE ** (-jnp.arange(nf, dtype=F32) / nf)
    lane = jnp.arange(GROUP_W)
    freq = inv[lane % nf]
    sign = jnp.where((lane & 16) == 0, -1.0, 1.0)
    ar = jnp.arange(rows, dtype=F32)[:, None] * freq[None, :]
    ac = jnp.arange(GRID_W, dtype=F32)[:, None] * freq[None, :]
    return (jnp.cos(ar).reshape(rows, 1, GROUP_W), (jnp.sin(ar) * sign).reshape(rows, 1, GROUP_W),
            jnp.cos(ac), jnp.sin(ac) * sign)


def _ret_scan(z, tables, rope, s0):
    b, l, _ = z.shape
    cpb = LA_CPB
    n = cpb * CHUNK
    nb = l // n
    cr, sr, cc, sc = rope
    fw = lambda j: pl.BlockSpec((1, n, GROUP_W), lambda i, c: (i, c, j))
    bw = lambda j: pl.BlockSpec((1, n, GROUP_W), lambda i, c: (i, nb - 1 - c, j))
    rowf = pl.BlockSpec((cpb, 1, GROUP_W), lambda i, c: (c, 0, 0))
    rowb = pl.BlockSpec((cpb, 1, GROUP_W), lambda i, c: (nb - 1 - c, 0, 0))
    colt = pl.BlockSpec((GRID_W, GROUP_W), lambda i, c: (0, 0))
    full = lambda a: pl.BlockSpec(a.shape, lambda i, c: (0,) * a.ndim)
    st = pl.BlockSpec((1, 2, HEADS, HEAD_V, RET_DK), lambda i, c: (i, 0, 0, 0, 0))
    return pl.pallas_call(
        functools.partial(_ret_scan_kernel, nb, cpb),
        grid=(b, nb),
        in_specs=[fw(0), fw(1), fw(2), bw(0), bw(1), bw(2)] + [full(a) for a in tables]
                 + [rowf, rowf, rowb, rowb, colt, colt, st],
        out_specs=[fw(0), bw(0), st],
        out_shape=[jax.ShapeDtypeStruct((b, l, GROUP_W), F32)] * 2
                  + [jax.ShapeDtypeStruct((b, 2, HEADS, HEAD_V, RET_DK), F32)],
        scratch_shapes=[pltpu.VMEM((2, HEADS, HEAD_V, RET_DK), F32)],
        compiler_params=_cparams("parallel", "arbitrary"),
        name="retention_scan",
    )(z, z, z, z, z, z, *tables, cr, sr, cr, sr, cc, sc, s0)


def _gla_scan(z, aup, abias, s0):
    b, l, _ = z.shape
    cpb = LA_CPB
    nb = l // (cpb * CHUNK)
    blk = lambda w, j, rev: pl.BlockSpec(
        (1, cpb * CHUNK, w), (lambda i, c: (i, nb - 1 - c, j)) if rev else (lambda i, c: (i, c, j)))
    st = pl.BlockSpec((1, 2, HEADS, HEAD_V, GLA_DK), lambda i, c: (i, 0, 0, 0, 0))
    ofw = pl.BlockSpec((1, cpb * CHUNK, GROUP_W), lambda i, c: (i, c, 0))
    obw = pl.BlockSpec((1, cpb * CHUNK, GROUP_W), lambda i, c: (i, nb - 1 - c, 0))
    return pl.pallas_call(
        functools.partial(_gla_scan_kernel, nb, cpb),
        grid=(b, nb),
        in_specs=[blk(GLA_QK, 0, False), blk(GLA_QK, 1, False), blk(GROUP_W, 1, False), blk(128, 6, False),
                  blk(GLA_QK, 0, True), blk(GLA_QK, 1, True), blk(GROUP_W, 1, True), blk(128, 6, True),
                  pl.BlockSpec(aup.shape, lambda i, c: (0, 0, 0)),
                  pl.BlockSpec(abias.shape, lambda i, c: (0, 0, 0)), st],
        out_specs=[ofw, obw, st],
        out_shape=[jax.ShapeDtypeStruct((b, l, GROUP_W), F32)] * 2
                  + [jax.ShapeDtypeStruct((b, 2, HEADS, HEAD_V, GLA_DK), F32)],
        scratch_shapes=[pltpu.VMEM((2, HEADS, HEAD_V, GLA_DK), F32)],
        compiler_params=_cparams("parallel", "arbitrary"),
        name="gla_scan",
    )(z, z, z, z, z, z, z, z, aup, abias, s0)


def _mix_kernel(x_ref, g1_ref, y5_ref, u5_ref, d5_ref, gw_ref, gb_ref,
                ryf_ref, ryb_ref, rbonus_ref, rg_ref, rlng_ref, rlnb_ref,
                tof_ref, tob_ref, tg_ref, tln_ref,
                gof_ref, gob_ref, gg_ref, gln_ref,
                bones_ref, wo_ref, o_ref, y5t_ref):
    bones = bones_ref[...]
    inv = 1.0 / HEAD_V

    def hmean(a):
        return _mm_right01(a, bones) * inv

    nch = y5_ref.shape[2]
    gph = LANES // S5_P
    y5g = [y5_ref[0, g] for g in range(S5_G)]
    for s in range(S5_T):
        for hf in range(GROUP_W // LANES):
            y5t_ref[hf, pl.ds(s, nch, stride=S5_T), :] = jnp.concatenate(
                [yg[:, s * S5_P:(s + 1) * S5_P] for yg in y5g[hf * gph:(hf + 1) * gph]], axis=1)
    y = jnp.concatenate([y5t_ref[0], y5t_ref[1]], axis=1) + d5_ref[...] * u5_ref[0]
    y = _gelu_tanh(y)
    ya = y * _sigmoid(_mm(y, gw_ref[...]) + gb_ref[...])
    yr = ryf_ref[0] + ryb_ref[0]
    dlt = yr - hmean(yr)
    yn = dlt * lax.rsqrt(hmean(dlt * dlt) + GN_EPS)
    yb = (yn * rlng_ref[...] + rlnb_ref[...] + rbonus_ref[0].astype(F32)) * rg_ref[0].astype(F32)
    ot = tof_ref[0] + tob_ref[0]
    yc = ot * lax.rsqrt(hmean(ot * ot) + EPS) * tln_ref[...] * _silu(tg_ref[0])
    og = gof_ref[0] + gob_ref[0]
    yd = og * lax.rsqrt(hmean(og * og) + EPS) * gln_ref[...] * _silu(gg_ref[0])
    mix = (_mm(ya, wo_ref[0]) + _mm(yb, wo_ref[1]) + _mm(yc, wo_ref[2]) + _mm(yd, wo_ref[3]))
    o_ref[0] = x_ref[0] + g1_ref[0] * mix


def _mix(x, g1, y5, u5, rw_pre, rw_y, zret, ret_o, zgla, gla_o, p, tm):
    b, l, d = x.shape
    tok = lambda w, j: pl.BlockSpec((1, tm, w), lambda i, t: (i, t, j))
    vec = lambda a: pl.BlockSpec(a.shape, lambda i, t: (0,) * a.ndim)
    t256 = tok(GROUP_W, 0)
    args = [x, g1, y5, u5, p['s5_d'], p['glu_w'], p['glu_b'],
            rw_y[0], rw_y[1], rw_pre[4], rw_pre[3], p['rw_ln_g'], p['rw_ln_b'],
            ret_o[0], ret_o[1], zret, p['ret_ln_g'],
            gla_o[0], gla_o[1], zgla, p['gla_ln_g'],
            p['bones'], p['w_out']]
    grouped = pl.BlockSpec((1, S5_G, tm // S5_T, S5_TP), lambda i, t: (i, 0, t, 0))
    specs = [tok(d, 0), pl.BlockSpec((1, 1, d), lambda i, t: (i, 0, 0)), grouped, t256,
             vec(p['s5_d']), vec(p['glu_w']), vec(p['glu_b']),
             t256, t256, t256, t256, vec(p['rw_ln_g']), vec(p['rw_ln_b']),
             t256, t256, tok(GROUP_W, 3), vec(p['ret_ln_g']),
             t256, t256, tok(GROUP_W, 2), vec(p['gla_ln_g']),
             vec(p['bones']), vec(p['w_out'])]
    return pl.pallas_call(
        _mix_kernel,
        grid=(b, l // tm),
        in_specs=specs,
        out_specs=tok(d, 0),
        out_shape=jax.ShapeDtypeStruct((b, l, d), F32),
        scratch_shapes=[pltpu.VMEM((GROUP_W // LANES, tm, LANES), F32)],
        compiler_params=_cparams("parallel", "parallel"),
        name="mix_outproj",
    )(*args)


def _mlp_kernel(final, nff, x_ref, g_ref, sc_ref, sh_ref, gate_ref, w1_ref, w2_ref, fg_ref, o_ref):
    x = x_ref[0]
    hb = _modnorm(x, g_ref[...], sc_ref[0], sh_ref[0]).astype(BF16)
    ff = w1_ref.shape[2] // nff
    acc = None
    for j in range(nff):
        a = jnp.maximum(jnp.dot(hb, w1_ref[0, :, j * ff:(j + 1) * ff], preferred_element_type=F32), 0.0)
        part = jnp.dot((a * a).astype(BF16), w2_ref[0, j * ff:(j + 1) * ff, :], preferred_element_type=F32)
        acc = part if acc is None else acc + part
    y = x + gate_ref[0] * acc
    if final:
        ms = jnp.mean(y * y, axis=-1, keepdims=True)
        y = y * lax.rsqrt(ms + EPS) * fg_ref[...]
    o_ref[0] = y


def _mlp(x, g, sc, sh, gate, w1, w2, layer, final_g, final, tm):
    b, l, d = x.shape
    tok = pl.BlockSpec((1, tm, d), lambda i, t: (i, t, 0))
    vec = pl.BlockSpec((1, 1, d), lambda i, t: (i, 0, 0))
    row = pl.BlockSpec((1, d), lambda i, t: (0, 0))
    once = lambda a: pl.BlockSpec((1,) + a.shape[1:], lambda i, t: (layer, 0, 0), pipeline_mode=pl.Buffered(1))
    return pl.pallas_call(
        functools.partial(_mlp_kernel, final, 4),
        grid=(b, l // tm),
        in_specs=[tok, row, vec, vec, vec, once(w1), once(w2), row],
        out_specs=tok,
        out_shape=jax.ShapeDtypeStruct((b, l, d), F32),
        compiler_params=_cparams("parallel", "parallel"),
        name="mlp",
    )(x, g, sc, sh, gate, w1, w2, final_g)


def kernel(x, c, ctx, c_ctx, ada_w, ada_b, norm1_g, norm2_g, w_in, w_out, s5_lam_re, s5_lam_im, s5_log_dt, s5_b_re, s5_b_im, s5_c_re, s5_c_im, s5_d, s5_glu_w, s5_glu_b, rw_mu, rw_w0, rw_w_up, rw_a0, rw_a_up, rw_g_up, rw_k_k, rw_k_a, rw_r_k, rw_ln_g, rw_ln_b, ret_decay_logit, ret_ln_g, gla_a_up, gla_a_b, gla_ln_g, mlp_w1, mlp_w2, final_g):
    b, l, d = x.shape
    lc = ctx.shape[1]
    depth = ada_w.shape[0]
    rows = l // GRID_W
    assert l % 512 == 0 and lc % CHUNK == 0 and lc % S5_T == 0 and d % 128 == 0

    cond = jnp.zeros((8, d), F32).at[:b].set(c).at[b].set(c_ctx)
    mod = _modulation(cond, ada_w, ada_b)

    lane = jnp.arange(GROUP_W)
    bones = (lane[:, None] // HEAD_V == lane[None, :] // HEAD_V).astype(BF16)
    rope_lat = _rope_tables(rows)
    ident = (jnp.ones((lc // CHUNK, 1, GROUP_W), F32), jnp.zeros((lc // CHUNK, 1, GROUP_W), F32),
             jnp.ones((GRID_W, GROUP_W), F32), jnp.zeros((GRID_W, GROUP_W), F32))
    nlev = max(1, (l // S5_T - 1).bit_length())
    c0, c1, c2 = GROUP_W, GROUP_W + RW_COLS, GROUP_W + RW_COLS + 4 * GROUP_W
    row2 = lambda a: a.reshape(1, -1).astype(F32)
    w_in_b, w_out_b, w1_b, w2_b = _to_bf16(w_in), _to_bf16(w_out), _to_bf16(mlp_w1), _to_bf16(mlp_w2)

    xc = ctx
    for i in range(depth):
        last = i == depth - 1
        m = mod[i].reshape(8, N_ADA, d)
        ml = m[:b, :, None, :]
        mc = jnp.broadcast_to(m[b][None, :, None, :], (b, N_ADA, 1, d))
        wi = w_in_b[i]
        w5, wr, wt = wi[:, :c0], wi[:, c0:c1], wi[:, c1:c2]
        wg = jnp.pad(wi[:, c2:], ((0, 0), (0, GLA_COLS_PAD - (wi.shape[1] - c2))))
        n1 = row2(norm1_g[i])
        n2 = row2(norm2_g[i])

        s5_tab = _s5_tables(s5_lam_re[i], s5_lam_im[i], s5_log_dt[i], s5_b_re[i], s5_b_im[i],
                            s5_c_re[i], s5_c_im[i], nlev)
        zpad = lambda a, lo, n: jnp.zeros((2, 128, n), F32).at[:, lo:lo + a.shape[1]].set(a).astype(BF16)
        rwp = dict(
            mu=row2(rw_mu[i]),
            w0=rw_w0[i].reshape(2, 1, GROUP_W), a0=rw_a0[i].reshape(2, 1, GROUP_W),
            wup=zpad(rw_w_up[i], 0, GROUP_W), aup=zpad(rw_a_up[i], RW_W_RANK, GROUP_W),
            gup=jnp.zeros((128, GROUP_W), F32).at[RW_W_RANK + RW_A_RANK:].set(rw_g_up[i]).astype(BF16),
            kk=row2(rw_k_k[i]), ka=row2(rw_k_a[i]), rk=row2(rw_r_k[i]), bones=bones)
        ret_tab = _ret_tables(ret_decay_logit[i], LA_CPB * CHUNK)
        gla_aup = zpad(gla_a_up[i], 0, GLA_QK)
        gla_ab = gla_a_b[i].reshape(2, 1, GLA_QK).astype(F32)
        mixp = dict(s5_d=row2(s5_d[i]), glu_w=s5_glu_w[i].astype(BF16), glu_b=row2(s5_glu_b[i]),
                    rw_ln_g=row2(rw_ln_g[i]), rw_ln_b=row2(rw_ln_b[i]), ret_ln_g=row2(ret_ln_g[i]),
                    gla_ln_g=row2(gla_ln_g[i]), bones=bones,
                    w_out=w_out_b[i].reshape(4, GROUP_W, d))

        def mixers(xx, mm, is_lat, states):
            tm = 512 if is_lat else lc
            z5, z5g, zr, zt, zg = _inproj(xx, n1, mm[:, 1], mm[:, 0], w5, wr, wt, wg, tm)
            y5, h5 = _s5_scan(z5g, s5_tab, states[0])
            pre = _rw_prep(zr, is_lat, rwp)
            yrf, yrb, srw = _rw_scan(pre, states[1])
            otf, otb, sret = _ret_scan(zt, ret_tab, rope_lat if is_lat else ident, states[2])
            ogf, ogb, sgla = _gla_scan(zg, gla_aup, gla_ab, states[3])
            outs = (z5, y5, pre, (yrf, yrb), zt, (otf, otb), zg, (ogf, ogb))
            return outs, (h5, srw, sret, sgla)

        def block(xx, mm, outs, is_lat, fin):
            tm = 512 if is_lat else lc
            z5, y5, pre, yr, zt, ot, zg, og = outs
            x1 = _mix(xx, mm[:, 2], y5, z5, pre, yr, zt, ot, zg, og, mixp, tm)
            return _mlp(x1, n2, mm[:, 4], mm[:, 3], mm[:, 5], w1_b, w2_b, i, row2(final_g), fin, tm)

        zeros = (jnp.zeros((b, 2, S5_G, 1, 2 * S5_N), F32),
                 jnp.zeros((b, 2, HEADS, HEAD_V, HEAD_V), F32),
                 jnp.zeros((b, 2, HEADS, HEAD_V, RET_DK), F32),
                 jnp.zeros((b, 2, HEADS, HEAD_V, GLA_DK), F32))
        outs_c, st_c = mixers(xc, mc, False, zeros)
        outs_l, _ = mixers(x, ml, True, st_c)
        x = block(x, ml, outs_l, True, last)
        if not last:
            xc = block(xc, mc, outs_c, False, False)
    return x
```

```python
import functools
import math

import jax
import jax.numpy as jnp
from jax import lax
from jax.experimental import pallas as pl
from jax.experimental.pallas import tpu as pltpu

F32 = jnp.float32
BF16 = jnp.bfloat16

LANES = 128
GRID_W = 64
GROUP_W = 256
N_ADA = 6
EPS = 1e-6
GN_EPS = 64e-5
CHUNK = 64
RW_CPB = 4
LA_CPB = 4
HEADS = 4
HEAD_V = 64
S5_P = 16
S5_G = 16
S5_N = 64
S5_T = 32
S5_TP = S5_T * S5_P
RW_COLS = 896
RW_W_RANK = 32
RW_A_RANK = 32
RW_G_RANK = 64
RET_DK = 64
GLA_DK = 32
GLA_QK = 128
GLA_RANK = 16
GLA_TAU = 16.0
GLA_COLS_PAD = 896
ROPE_BASE = 10000.0
VMEM_LIMIT = 56 * 1024 * 1024


def _cparams(*sem):
    return pltpu.CompilerParams(dimension_semantics=sem, vmem_limit_bytes=VMEM_LIMIT)


def _mm(a, b):
    return jnp.dot(a.astype(BF16), b.astype(BF16), preferred_element_type=F32)


def _mm_nt(a, b):
    return lax.dot_general(a.astype(BF16), b.astype(BF16), (((1,), (1,)), ((), ())),
                           preferred_element_type=F32)


def _mm_tn(a, b):
    return lax.dot_general(a.astype(BF16), b.astype(BF16), (((0,), (0,)), ((), ())),
                           preferred_element_type=F32)


def _split2(x):
    hi = x.astype(BF16)
    return hi, (x - hi.astype(F32)).astype(BF16)


def _join2(p):
    return p[0].astype(F32) + p[1].astype(F32)


def _mm_x3p(a, b):
    return (jnp.dot(a[0], b[0], preferred_element_type=F32) + jnp.dot(a[0], b[1], preferred_element_type=F32)
            + jnp.dot(a[1], b[0], preferred_element_type=F32))


def _chunk_tri01(n, reverse):
    ri = lax.broadcasted_iota(jnp.int32, (n, n), 0)
    ci = lax.broadcasted_iota(jnp.int32, (n, n), 1)
    shift = CHUNK.bit_length() - 1
    tri = (ci >= ri) if reverse else (ci <= ri)
    return jnp.where((ri >> shift) == (ci >> shift), jnp.where(tri, 1.0, 0.0), 0.0).astype(BF16)


def _split3(x):
    hi = x.astype(BF16)
    r1 = x - hi.astype(F32)
    mid = r1.astype(BF16)
    lo = (r1 - mid.astype(F32)).astype(BF16)
    return hi, mid, lo


def _mm_left01(m01, x):
    hi, mid, lo = _split3(x)
    return (jnp.dot(m01, hi, preferred_element_type=F32)
            + jnp.dot(m01, mid, preferred_element_type=F32)
            + jnp.dot(m01, lo, preferred_element_type=F32))


def _mm_right01(x, m01):
    hi, mid, lo = _split3(x)
    return (jnp.dot(hi, m01, preferred_element_type=F32)
            + jnp.dot(mid, m01, preferred_element_type=F32)
            + jnp.dot(lo, m01, preferred_element_type=F32))


def _sigmoid(x):
    return 1.0 / (1.0 + jnp.exp(-x))


def _softplus(x):
    return jnp.maximum(x, 0.0) + jnp.log(1.0 + jnp.exp(-jnp.abs(x)))


def _silu(x):
    return x * _sigmoid(x)


def _gelu_tanh(x):
    return 0.5 * x * (1.0 + jnp.tanh(math.sqrt(2.0 / math.pi) * (x + 0.044715 * x * x * x)))


def _tri_masks(t, reverse):
    ri = lax.broadcasted_iota(jnp.int32, (t, t), 0)
    ci = lax.broadcasted_iota(jnp.int32, (t, t), 1)
    if reverse:
        return ci >= ri, ci > ri, ci == ri
    return ci <= ri, ci < ri, ci == ri


def _row_to_col(row, eye):
    n = row.shape[1]
    return jnp.sum(jnp.where(eye, jnp.broadcast_to(row, (n, n)), 0.0), axis=1, keepdims=True)


def _cast_kernel(x_ref, o_ref):
    o_ref[...] = x_ref[...].astype(o_ref.dtype)


def _to_bf16(w):
    depth, r, c = w.shape
    tr = 256
    spec = pl.BlockSpec((1, tr, c), lambda i, j: (i, j, 0))
    return pl.pallas_call(
        _cast_kernel,
        grid=(depth, r // tr),
        in_specs=[spec],
        out_specs=spec,
        out_shape=jax.ShapeDtypeStruct(w.shape, BF16),
        compiler_params=_cparams("parallel", "parallel"),
        name="cast_bf16",
    )(w)


def _mod_kernel(cond_ref, w_ref, b_ref, o_ref):
    c = cond_ref[...]
    o_ref[0] = _mm(_silu(c), w_ref[0]) + b_ref[0]


def _modulation(cond, ada_w, ada_b):
    depth, d, n = ada_w.shape
    tn = 1536
    return pl.pallas_call(
        _mod_kernel,
        grid=(depth, n // tn),
        in_specs=[pl.BlockSpec((8, d), lambda i, j: (0, 0)),
                  pl.BlockSpec((1, d, tn), lambda i, j: (i, 0, j)),
                  pl.BlockSpec((1, 1, tn), lambda i, j: (i, 0, j))],
        out_specs=pl.BlockSpec((1, 8, tn), lambda i, j: (i, 0, j)),
        out_shape=jax.ShapeDtypeStruct((depth, 8, n), F32),
        compiler_params=_cparams("parallel", "parallel"),
        name="adaln_mod",
    )(cond, ada_w, ada_b.reshape(depth, 1, n))


def _modnorm(x, g, sc, sh):
    ms = jnp.mean(x * x, axis=-1, keepdims=True)
    return x * lax.rsqrt(ms + EPS) * g * (1.0 + sc) + sh


def _inproj_kernel(x_ref, g_ref, sc_ref, sh_ref, w5_ref, wr_ref, wt_ref, wg_ref,
                   o5_ref, o5g_ref, or_ref, ot_ref, og_ref, z5h_ref):
    hb = _modnorm(x_ref[0], g_ref[...], sc_ref[0], sh_ref[0]).astype(BF16)
    z5 = jnp.dot(hb, w5_ref[...], preferred_element_type=F32)
    o5_ref[0] = z5
    or_ref[0] = jnp.dot(hb, wr_ref[...], preferred_element_type=F32)
    ot_ref[0] = jnp.dot(hb, wt_ref[...], preferred_element_type=F32)
    og_ref[0] = jnp.dot(hb, wg_ref[...], preferred_element_type=F32)
    nch = o5g_ref.shape[2]
    gph = LANES // S5_P
    for hf in range(GROUP_W // LANES):
        z5h_ref[hf] = z5[:, hf * LANES:(hf + 1) * LANES]
    for s in range(S5_T):
        for hf in range(GROUP_W // LANES):
            rows = z5h_ref[hf, pl.ds(s, nch, stride=S5_T), :]
            for g in range(gph):
                o5g_ref[0, hf * gph + g, :, s * S5_P:(s + 1) * S5_P] = rows[:, g * S5_P:(g + 1) * S5_P]


def _inproj(x, g, sc, sh, w5, wr, wt, wg, tm):
    b, l, d = x.shape
    widths = (w5.shape[1], wr.shape[1], wt.shape[1], wg.shape[1])
    tok = lambda n: pl.BlockSpec((1, tm, n), lambda i, j: (i, j, 0))
    vec = pl.BlockSpec((1, 1, d), lambda i, j: (i, 0, 0))
    full = lambda w: pl.BlockSpec(w.shape, lambda i, j: (0, 0))
    grouped = pl.BlockSpec((1, S5_G, tm // S5_T, S5_TP), lambda i, j: (i, 0, j, 0))
    return pl.pallas_call(
        _inproj_kernel,
        grid=(b, l // tm),
        in_specs=[tok(d), pl.BlockSpec((1, d), lambda i, j: (0, 0)), vec, vec,
                  full(w5), full(wr), full(wt), full(wg)],
        out_specs=[tok(widths[0]), grouped] + [tok(n) for n in widths[1:]],
        out_shape=[jax.ShapeDtypeStruct((b, l, widths[0]), F32),
                   jax.ShapeDtypeStruct((b, S5_G, l // S5_T, S5_TP), F32)]
                  + [jax.ShapeDtypeStruct((b, l, n), F32) for n in widths[1:]],
        scratch_shapes=[pltpu.VMEM((GROUP_W // LANES, tm, LANES), F32)],
        compiler_params=_cparams("parallel", "parallel"),
        name="norm_inproj",
    )(x, g, sc, sh, w5, wr, wt, wg)


def _s5_tables(lam_re, lam_im, log_dt, b_re, b_im, c_re, c_im, nlev):
    hp = lax.Precision.HIGHEST
    t = S5_T
    lam = lax.complex(jnp.minimum(lam_re.astype(F32), -1e-4), lam_im.astype(F32))
    ldt = lam * jnp.exp(log_dt.astype(F32))[..., None]
    a_bar = jnp.exp(ldt)
    bb = ((a_bar - 1.0) / lam)[..., None] * lax.complex(b_re.astype(F32), b_im.astype(F32))
    cm = lax.complex(c_re.astype(F32), c_im.astype(F32))
    tau = jnp.arange(t + 1, dtype=F32)
    apow = jnp.exp(ldt[:, :, None, :] * tau[None, None, :, None])
    taps = jnp.einsum('dgpn,dgtn,dgnq->dgtpq', cm, apow[:, :, :t], bb, precision=hp).real
    taprow = jnp.stack([taps[0], taps[1][:, ::-1]]).transpose(0, 1, 4, 2, 3).reshape(2, S5_G, S5_P, S5_TP)

    def pack(zc):
        return jnp.concatenate([zc.real, zc.imag], axis=-1)

    win_f = apow[0][:, t - 1 - jnp.arange(t), None, :] * bb[0].transpose(0, 2, 1)[:, None]
    win_b = apow[1][:, jnp.arange(t), None, :] * bb[1].transpose(0, 2, 1)[:, None]
    win = jnp.stack([pack(win_f), pack(win_b)]).reshape(2, S5_G, S5_TP, 2 * S5_N)
    ca_f = cm[0][:, None] * apow[0][:, 1 + jnp.arange(t), None, :]
    ca_b = cm[1][:, None] * apow[1][:, t - jnp.arange(t), None, :]

    def outpack(ca):
        w = jnp.concatenate([ca.real, -ca.imag], axis=-1)
        return w.reshape(S5_G, S5_TP, 2 * S5_N).transpose(0, 2, 1)

    wout = jnp.stack([outpack(ca_f), outpack(ca_b)])
    lev = (2.0 ** jnp.arange(nlev, dtype=F32)) * t
    pw = jnp.exp(ldt[:, :, None, :] * lev[None, None, :, None])
    p1 = jnp.concatenate([pw.real, pw.real], axis=-1)
    p2 = jnp.concatenate([-pw.imag, pw.imag], axis=-1)
    pw = jnp.stack([p1, p2], axis=3)
    return taprow, win.astype(BF16), wout.astype(BF16), pw


def _s5_kernel(nc, nlev, u_ref, tap_ref, win_ref, wout_ref, pw_ref, h0_ref, y_ref, hfin_ref, conv_ref):
    u = u_ref[0, 0].astype(BF16)
    row = lax.broadcasted_iota(jnp.int32, (nc, 2 * S5_N), 0)
    lane = lax.broadcasted_iota(jnp.int32, (S5_P, S5_TP), 1)
    for s in range(S5_T):
        lo = s * S5_P
        fwd = tap_ref[0, 0] if s == 0 else jnp.where(lane >= lo, pltpu.roll(tap_ref[0, 0], lo, 1), 0.0)
        hi = lo + S5_P
        bwd = tap_ref[1, 0] if hi == S5_TP else jnp.where(lane < hi, pltpu.roll(tap_ref[1, 0], hi, 1), 0.0)
        conv_ref[0, lo:hi, :] = fwd.astype(BF16)
        conv_ref[1, lo:hi, :] = bwd.astype(BF16)

    def cmul(x, d, j):
        return pw_ref[d, 0, j, 0:1] * x + pw_ref[d, 0, j, 1:2] * pltpu.roll(x, S5_N, 1)

    y = None
    for d in range(2):
        v = jnp.dot(u, win_ref[d, 0], preferred_element_type=F32)
        h0 = h0_ref[0, d, 0]
        if d == 0:
            x = jnp.where(row == 0, h0, pltpu.roll(v, 1, 0))
        else:
            x = jnp.where(row == nc - 1, h0, pltpu.roll(v, nc - 1, 0))
        for j in range(nlev):
            sh = 2 ** j
            if d == 0:
                xs = jnp.where(row >= sh, pltpu.roll(x, sh, 0), 0.0)
            else:
                xs = jnp.where(row < nc - sh, pltpu.roll(x, nc - sh, 0), 0.0)
            x = x + cmul(xs, d, j)
        last = nc - 1 if d == 0 else 0
        hfin_ref[0, d, 0] = cmul(x[last:last + 1], d, 0) + v[last:last + 1]
        yd = (jnp.dot(u, conv_ref[d], preferred_element_type=F32)
              + _mm(x, wout_ref[d, 0]))
        y = yd if y is None else y + yd
    y_ref[0, 0] = y


def _s5_scan(uf, tables, h0):
    conv, win, wout, pw = tables
    b, _, nc, _ = uf.shape
    nlev = max(1, (nc - 1).bit_length())
    pw = pw[:, :, :nlev]
    n2 = 2 * S5_N
    y, hfin = pl.pallas_call(
        functools.partial(_s5_kernel, nc, nlev),
        grid=(b, S5_G),
        in_specs=[pl.BlockSpec((1, 1, nc, S5_TP), lambda i, g: (i, g, 0, 0)),
                  pl.BlockSpec((2, 1, S5_P, S5_TP), lambda i, g: (0, g, 0, 0)),
                  pl.BlockSpec((2, 1, S5_TP, n2), lambda i, g: (0, g, 0, 0)),
                  pl.BlockSpec((2, 1, n2, S5_TP), lambda i, g: (0, g, 0, 0)),
                  pl.BlockSpec((2, 1, nlev, 2, n2), lambda i, g: (0, g, 0, 0, 0)),
                  pl.BlockSpec((1, 2, 1, 1, n2), lambda i, g: (i, 0, g, 0, 0))],
        out_specs=[pl.BlockSpec((1, 1, nc, S5_TP), lambda i, g: (i, g, 0, 0)),
                   pl.BlockSpec((1, 2, 1, 1, n2), lambda i, g: (i, 0, g, 0, 0))],
        out_shape=[jax.ShapeDtypeStruct((b, S5_G, nc, S5_TP), F32),
                   jax.ShapeDtypeStruct((b, 2, S5_G, 1, n2), F32)],
        scratch_shapes=[pltpu.VMEM((2, S5_TP, S5_TP), BF16)],
        compiler_params=_cparams("parallel", "parallel"),
        name="s5_scan",
    )(uf, conv, win, wout, pw, h0)
    return y, hfin


def _rw_prep_kernel(grid_shift, nt, z_ref, zp_ref, zn_ref, mu_ref, w0_ref, wup_ref, a0_ref, aup_ref,
                    gup_ref, kk_ref, ka_ref, rk_ref, bones_ref,
                    r_ref, v_ref, kkn_ref, g_ref, bonus_ref,
                    lw0_ref, kd0_ref, as0_ref, lw1_ref, kd1_ref, as1_ref):
    i = pl.program_id(1)
    z = z_ref[0]
    tm = z.shape[0]
    row = lax.broadcasted_iota(jnp.int32, z.shape, 0)
    lane = lax.broadcasted_iota(jnp.int32, z.shape, 1)
    prev1 = pltpu.roll(z, 1, 0)
    next1 = pltpu.roll(z, tm - 1, 0)
    if grid_shift:
        col = row & (GRID_W - 1)
        left = jnp.where(col == 0, 0.0, prev1)
        right = jnp.where(col == GRID_W - 1, 0.0, next1)
        zp = jnp.where(i > 0, zp_ref[0], 0.0)
        zn = jnp.where(i < nt - 1, zn_ref[0], 0.0)
        up = jnp.concatenate([zp, z[:tm - GRID_W]], axis=0)
        down = jnp.concatenate([z[GRID_W:], zn], axis=0)
        q = RW_COLS // 4
        shifted = jnp.where(lane < q, left,
                            jnp.where(lane < 2 * q, right, jnp.where(lane < 3 * q, up, down)))
    else:
        prev = jnp.where(row == 0, 0.0, prev1)
        nxt = jnp.where(row == tm - 1, 0.0, next1)
        shifted = jnp.where(lane < RW_COLS // 2, prev, nxt)
    zm = z + mu_ref[...] * (shifted - z)
    r = zm[:, 0:256]
    k = zm[:, 256:512]
    v = zm[:, 512:768]
    lo = zm[:, 768:896]
    bones = bones_ref[...]
    g_ref[0] = _mm(_sigmoid(lo), gup_ref[...]).astype(BF16)
    kk = k * kk_ref[...]
    kk = kk * lax.rsqrt(_mm_right01(kk * kk, bones) + 1e-12)
    r_ref[0] = r.astype(BF16)
    v_ref[0] = v.astype(BF16)
    kkn_ref[0] = kk.astype(BF16)
    bonus_ref[0] = (_mm_right01(r * k * rk_ref[...], bones) * v).astype(BF16)
    tlo = jnp.tanh(lo)
    for d, (lw_ref, kd_ref, as_ref) in enumerate(((lw0_ref, kd0_ref, as0_ref), (lw1_ref, kd1_ref, as1_ref))):
        w_raw = -_softplus(-(w0_ref[d] + _mm(tlo, wup_ref[d]))) - 0.5
        lw_ref[0] = -jnp.exp(w_raw)
        a = _sigmoid(a0_ref[d] + _mm(lo, aup_ref[d]))
        kd_ref[0] = (k * (1.0 + (a - 1.0) * ka_ref[...])).astype(BF16)
        as_ref[0] = a.astype(BF16)


def _rw_prep(z, grid_shift, p):
    b, l, _ = z.shape
    tm = 256 if grid_shift else l
    nt = l // tm
    hb = tm // GRID_W
    nhb = l // GRID_W
    tok = pl.BlockSpec((1, tm, GROUP_W), lambda i, j: (i, j, 0))
    full = lambda a: pl.BlockSpec(a.shape, lambda i, j: (0,) * a.ndim)
    params = (p['mu'], p['w0'], p['wup'], p['a0'], p['aup'], p['gup'], p['kk'], p['ka'], p['rk'], p['bones'])
    return pl.pallas_call(
        functools.partial(_rw_prep_kernel, grid_shift, nt),
        grid=(b, nt),
        in_specs=[pl.BlockSpec((1, tm, RW_COLS), lambda i, j: (i, j, 0)),
                  pl.BlockSpec((1, GRID_W, RW_COLS), lambda i, j: (i, jnp.maximum(j * hb - 1, 0), 0)),
                  pl.BlockSpec((1, GRID_W, RW_COLS), lambda i, j: (i, jnp.minimum((j + 1) * hb, nhb - 1), 0))]
                 + [full(a) for a in params],
        out_specs=[tok] * 11,
        out_shape=[jax.ShapeDtypeStruct((b, l, GROUP_W), dt)
                   for dt in (BF16,) * 5 + (F32, BF16, BF16) * 2],
        compiler_params=_cparams("parallel", "parallel"),
        name="rwkv_prep",
    )(z, z, z, *params)


def _rw_scan_kernel(nb, cpb, rf_ref, vf_ref, kkf_ref, lwf_ref, kdf_ref, asf_ref,
                    rb_ref, vb_ref, kkb_ref, lwb_ref, kdb_ref, asb_ref, s0_ref,
                    yf_ref, yb_ref, sfin_ref, st_ref):
    c = pl.program_id(1)

    @pl.when(c == 0)
    def _():
        st_ref[...] = s0_ref[0]

    t = CHUNK
    ri = lax.broadcasted_iota(jnp.int32, (t, t), 0)
    ci = lax.broadcasted_iota(jnp.int32, (t, t), 1)
    eye = ri == ci
    eyef = jnp.where(eye, 1.0, 0.0)
    same = lambda s: jnp.where((ri >> s) == (ci >> s), 1.0, 0.0)
    m4, m8, m16, m32 = same(2), same(3), same(4), same(5)
    merge_masks = (m8 - m4, m16 - m8, m32 - m16, 1.0 - m32)
    dir_refs = ((rf_ref, vf_ref, kkf_ref, lwf_ref, kdf_ref, asf_ref),
                (rb_ref, vb_ref, kkb_ref, lwb_ref, kdb_ref, asb_ref))
    units = []
    for d, (r_ref, v_ref, kk_ref, lw_ref, kd_ref, as_ref) in enumerate(dir_refs):
        incl = (ci >= ri) if d == 1 else (ci <= ri)
        strict = (ci > ri) if d == 1 else (ci < ri)
        r, v, kk, kd = (x[0].astype(F32) for x in (r_ref, v_ref, kk_ref, kd_ref))
        lw = lw_ref[0]
        cin = _mm_left01(_chunk_tri01(cpb * t, d == 1), lw)
        e_in = jnp.exp(cin)
        e_neg = jnp.exp(-cin)
        rt = r * e_in
        at = -kk * jnp.exp(cin - lw)
        bvec = kk * as_ref[0].astype(F32)
        bt = bvec * e_neg
        kt = kd * e_neg
        for j in range(cpb):
            rows = slice(j * t, (j + 1) * t)
            last = j * t + (0 if d == 1 else t - 1)
            clast = cin[last:last + 1]
            dl = jnp.exp(clast - cin[rows])
            bh = bvec[rows] * dl
            kh = kd[rows] * dl
            dec = jnp.exp(clast)
            for h in range(HEADS):
                sl = slice(h * HEAD_V, (h + 1) * HEAD_V)
                units.append(dict(d=d, j=j, h=h, incl=incl, strict=strict,
                                  at=at[rows, sl], rt=rt[rows, sl], bt=bt[rows, sl], kt=kt[rows, sl],
                                  bh=bh[:, sl], kh=kh[:, sl], v=v[rows, sl], dec=dec[:, sl]))

    a_all = [_mm_nt(jnp.concatenate([u['at'], u['rt']], axis=0),
                    jnp.concatenate([u['bt'], u['kt']], axis=0)) for u in units]
    nmat = [jnp.where(u['strict'], a[0:t, 0:t], 0.0) for u, a in zip(units, a_all)]
    a_kk = [jnp.concatenate([jnp.where(u['strict'], a[0:t, t:2 * t], 0.0),
                             jnp.where(u['incl'], a[t:2 * t, t:2 * t], 0.0)], axis=0)
            for u, a in zip(units, a_all)]
    a_rb = [jnp.where(u['incl'], a[t:2 * t, 0:t], 0.0) for u, a in zip(units, a_all)]
    akv = [_mm(a, u['v']) for u, a in zip(units, a_kk)]
    kv = [_mm_tn(u['v'], u['kh']) for u in units]
    nd = [x * m4 for x in nmat]
    n2 = [_mm(x, x) for x in nd]
    tinv = [eyef + x + _mm(eyef + x, y) for x, y in zip(nd, n2)]
    for mk in merge_masks:
        w = [_mm(ti, x * mk) for ti, x in zip(tinv, nmat)]
        tinv = [ti + _mm(wi, ti) for ti, wi in zip(tinv, w)]
    zz = [_mm(ti, jnp.concatenate([u['at'], kvv[0:t]], axis=1)) for ti, u, kvv in zip(tinv, units, akv)]
    ght = [_mm_tn(z, u['bh']) for u, z in zip(units, zz)]
    qy = [_mm(a, z) for a, z in zip(a_rb, zz)]
    qmat = [u['rt'] + x[:, :HEAD_V] for u, x in zip(units, qy)]
    y0 = [x[:, HEAD_V:] + kvv[t:2 * t] for x, kvv in zip(qy, akv)]
    gmat = [x[:HEAD_V] for x in ght]
    hmat = [x[HEAD_V:] + k2 for x, k2 in zip(ght, kv)]
    idx = {(u['d'], u['j'], u['h']): n for n, u in enumerate(units)}
    state = {(d, h): st_ref[d, h] for d in range(2) for h in range(HEADS)}
    ys = {}
    for step in range(cpb):
        for d in range(2):
            j = step if d == 0 else cpb - 1 - step
            for h in range(HEADS):
                n = idx[(d, j, h)]
                st = state[(d, h)]
                ys[(d, j, h)] = _mm_nt(qmat[n], st) + y0[n]
                state[(d, h)] = units[n]['dec'] * st + _mm(st, gmat[n]) + hmat[n]
    for d, y_ref in enumerate((yf_ref, yb_ref)):
        y_ref[0] = jnp.concatenate(
            [jnp.concatenate([ys[(d, j, h)] for h in range(HEADS)], axis=1) for j in range(cpb)], axis=0)
        for h in range(HEADS):
            st_ref[d, h] = state[(d, h)]

    @pl.when(c == nb - 1)
    def _():
        sfin_ref[0] = st_ref[...]


def _rw_scan(pre, s0):
    r, v, kk, _, _, lw0, kd0, as0, lw1, kd1, as1 = pre
    b, l, _ = r.shape
    cpb = min(RW_CPB, l // CHUNK)
    nb = l // (cpb * CHUNK)
    fw = pl.BlockSpec((1, cpb * CHUNK, GROUP_W), lambda i, c: (i, c, 0))
    bw = pl.BlockSpec((1, cpb * CHUNK, GROUP_W), lambda i, c: (i, nb - 1 - c, 0))
    st = pl.BlockSpec((1, 2, HEADS, HEAD_V, HEAD_V), lambda i, c: (i, 0, 0, 0, 0))
    return pl.pallas_call(
        functools.partial(_rw_scan_kernel, nb, cpb),
        grid=(b, nb),
        in_specs=[fw] * 6 + [bw] * 6 + [st],
        out_specs=[fw, bw, st],
        out_shape=[jax.ShapeDtypeStruct((b, l, GROUP_W), F32)] * 2
                  + [jax.ShapeDtypeStruct((b, 2, HEADS, HEAD_V, HEAD_V), F32)],
        scratch_shapes=[pltpu.VMEM((2, HEADS, HEAD_V, HEAD_V), F32)],
        compiler_params=_cparams("parallel", "arbitrary"),
        name="rwkv_scan",
    )(r, v, kk, lw0, kd0, as0, r, v, kk, lw1, kd1, as1, s0)


def _rope(x, cos, sin_signed):
    lane = lax.broadcasted_iota(jnp.int32, x.shape, 1)
    n = x.shape[1]
    swapped = jnp.where((lane & 16) == 0, pltpu.roll(x, n - 16, 1), pltpu.roll(x, 16, 1))
    return x * cos + swapped * sin_signed


def _ret_scan_kernel(nb, cpb, qf_ref, kf_ref, vf_ref, qb_ref, kb_ref, vb_ref,
                     dmat_ref, qdec_ref, kdec_ref, sdec_ref,
                     crf_ref, srf_ref, crb_ref, srb_ref, cc_ref, sc_ref, s0_ref,
                     of_ref, ob_ref, sfin_ref, st_ref):
    c = pl.program_id(1)

    @pl.when(c == 0)
    def _():
        st_ref[...] = s0_ref[0]

    t = CHUNK
    scale = RET_DK ** -0.5
    lane = lax.broadcasted_iota(jnp.int32, (t, GROUP_W), 1)
    by_row = (lane & 32) == 0
    qs, ks, qds, khs, vs = [], [], [], [], []
    for d, (q_ref, k_ref, v_ref) in enumerate(((qf_ref, kf_ref, vf_ref), (qb_ref, kb_ref, vb_ref))):
        cr_ref, sr_ref = (crf_ref, srf_ref) if d == 0 else (crb_ref, srb_ref)
        cos = jnp.concatenate([jnp.where(by_row, cr_ref[j], cc_ref[...]) for j in range(cpb)], axis=0)
        sin = jnp.concatenate([jnp.where(by_row, sr_ref[j], sc_ref[...]) for j in range(cpb)], axis=0)
        q = _rope(q_ref[0], cos, sin)
        k = _rope(k_ref[0] * scale, cos, sin)
        v = v_ref[0]
        qd = q * qdec_ref[d]
        kh = k * kdec_ref[d]
        for h in range(HEADS):
            sl = slice(h * HEAD_V, (h + 1) * HEAD_V)
            qs.append(q[:, sl])
            ks.append(k[:, sl])
            qds.append(qd[:, sl])
            khs.append(kh[:, sl])
            vs.append(v[:, sl])
    scores = [_mm_nt(q, k) * dmat_ref[i // HEADS, i % HEADS] for i, (q, k) in enumerate(zip(qs, ks))]
    states = [st_ref[i // HEADS, i % HEADS] for i in range(2 * HEADS)]
    outs = [_mm(p, v) + _mm_nt(qd, st) for p, v, qd, st in zip(scores, vs, qds, states)]
    for i, (st, kh, v) in enumerate(zip(states, khs, vs)):
        d, h = i // HEADS, i % HEADS
        st_ref[d, h] = st * sdec_ref[d][:, h * HEAD_V:(h + 1) * HEAD_V] + _mm_tn(v, kh)
    of_ref[0] = jnp.concatenate(outs[:HEADS], axis=1)
    ob_ref[0] = jnp.concatenate(outs[HEADS:], axis=1)

    @pl.when(c == nb - 1)
    def _():
        sfin_ref[0] = st_ref[...]


def _ret_tables(decay_logit, n):
    lg = jax.nn.log_sigmoid(decay_logit.astype(F32))
    pos = jnp.arange(n, dtype=F32)
    lag = pos[:, None] - pos[None, :]
    lag = jnp.stack([lag, -lag])
    dmat = jnp.where(lag[:, None] >= 0, jnp.exp(lg[:, :, None, None] * lag[:, None]), 0.0)
    lanes = jnp.repeat(lg, RET_DK, axis=-1)[:, None, :]
    qpow = jnp.stack([pos + 1.0, n - pos])[:, :, None]
    kpow = jnp.stack([n - 1.0 - pos, pos])[:, :, None]
    return dmat, jnp.exp(lanes * qpow), jnp.exp(lanes * kpow), jnp.exp(lanes * n)


def _gla_scan_kernel(nb, cpb, qf_ref, kf_ref, vf_ref, af_ref, qb_ref, kb_ref, vb_ref, ab_ref,
                     aup_ref, abias_ref, s0_ref, of_ref, ob_ref, sfin_ref, st_ref):
    c = pl.program_id(1)

    @pl.when(c == 0)
    def _():
        st_ref[...] = s0_ref[0]

    t = CHUNK
    n = cpb * t
    dk = GLA_DK
    scale = dk ** -0.5
    dirs = ((qf_ref, kf_ref, vf_ref, af_ref), (qb_ref, kb_ref, vb_ref, ab_ref))
    units = []
    for d, (q_ref, k_ref, v_ref, a_ref) in enumerate(dirs):
        incl = _tri_masks(t, d == 1)[0]
        q, k, v = q_ref[0], k_ref[0] * scale, v_ref[0]
        lw = -_softplus(-(_mm(a_ref[0], aup_ref[d]) + abias_ref[d])) * (1.0 / GLA_TAU)
        cin = _mm_left01(_chunk_tri01(n, d == 1), lw)
        qt = q * jnp.exp(cin)
        kt = k * jnp.exp(-cin)
        for j in range(cpb):
            rows = slice(j * t, (j + 1) * t)
            last = j * t + (0 if d == 1 else t - 1)
            clast = cin[last:last + 1]
            kh = k[rows] * jnp.exp(clast - cin[rows])
            dec = jnp.exp(clast)
            for h in range(HEADS):
                sk = slice(h * dk, (h + 1) * dk)
                sv = slice(h * HEAD_V, (h + 1) * HEAD_V)
                units.append(dict(d=d, j=j, h=h, incl=incl, qt=qt[rows, sk], kt=kt[rows, sk],
                                  kh=kh[:, sk], v=v[rows, sv], dec=dec[:, sk]))

    amat = [jnp.where(u['incl'], _mm_nt(u['qt'], u['kt']), 0.0) for u in units]
    intra = [_mm(a, u['v']) for a, u in zip(amat, units)]
    kv = [_mm_tn(u['v'], u['kh']) for u in units]
    idx = {(u['d'], u['j'], u['h']): i for i, u in enumerate(units)}
    entering = {}
    for d in range(2):
        for h in range(HEADS):
            st = st_ref[d, h]
            for step in range(cpb):
                j = step if d == 0 else cpb - 1 - step
                i = idx[(d, j, h)]
                entering[i] = st
                st = units[i]['dec'] * st + kv[i]
            st_ref[d, h] = st
    outs = [x + _mm_nt(u['qt'], entering[i]) for i, (x, u) in enumerate(zip(intra, units))]
    for d, o_ref in enumerate((of_ref, ob_ref)):
        o_ref[0] = jnp.concatenate(
            [jnp.concatenate([outs[idx[(d, j, h)]] for h in range(HEADS)], axis=1) for j in range(cpb)], axis=0)

    @pl.when(c == nb - 1)
    def _():
        sfin_ref[0] = st_ref[...]


def _rope_tables(rows):
    nf = RET_DK // 4
    inv = ROPE_BASE ** (-jnp.arange(nf, dtype=F32) / nf)
    lane = jnp.arange(GROUP_W)
    freq = inv[lane % nf]
    sign = jnp.where((lane & 16) == 0, -1.0, 1.0)
    ar = jnp.arange(rows, dtype=F32)[:, None] * freq[None, :]
    ac = jnp.arange(GRID_W, dtype=F32)[:, None] * freq[None, :]
    return (jnp.cos(ar).reshape(rows, 1, GROUP_W), (jnp.sin(ar) * sign).reshape(rows, 1, GROUP_W),
            jnp.cos(ac), jnp.sin(ac) * sign)


def _ret_scan(z, tables, rope, s0):
    b, l, _ = z.shape
    cpb = LA_CPB
    n = cpb * CHUNK
    nb = l // n
    cr, sr, cc, sc = rope
    fw = lambda j: pl.BlockSpec((1, n, GROUP_W), lambda i, c: (i, c, j))
    bw = lambda j: pl.BlockSpec((1, n, GROUP_W), lambda i, c: (i, nb - 1 - c, j))
    rowf = pl.BlockSpec((cpb, 1, GROUP_W), lambda i, c: (c, 0, 0))
    rowb = pl.BlockSpec((cpb, 1, GROUP_W), lambda i, c: (nb - 1 - c, 0, 0))
    colt = pl.BlockSpec((GRID_W, GROUP_W), lambda i, c: (0, 0))
    full = lambda a: pl.BlockSpec(a.shape, lambda i, c: (0,) * a.ndim)
    st = pl.BlockSpec((1, 2, HEADS, HEAD_V, RET_DK), lambda i, c: (i, 0, 0, 0, 0))
    return pl.pallas_call(
        functools.partial(_ret_scan_kernel, nb, cpb),
        grid=(b, nb),
        in_specs=[fw(0), fw(1), fw(2), bw(0), bw(1), bw(2)] + [full(a) for a in tables]
                 + [rowf, rowf, rowb, rowb, colt, colt, st],
        out_specs=[fw(0), bw(0), st],
        out_shape=[jax.ShapeDtypeStruct((b, l, GROUP_W), F32)] * 2
                  + [jax.ShapeDtypeStruct((b, 2, HEADS, HEAD_V, RET_DK), F32)],
        scratch_shapes=[pltpu.VMEM((2, HEADS, HEAD_V, RET_DK), F32)],
        compiler_params=_cparams("parallel", "arbitrary"),
        name="retention_scan",
    )(z, z, z, z, z, z, *tables, cr, sr, cr, sr, cc, sc, s0)


def _gla_scan(z, aup, abias, s0):
    b, l, _ = z.shape
    cpb = LA_CPB
    nb = l // (cpb * CHUNK)
    blk = lambda w, j, rev: pl.BlockSpec(
        (1, cpb * CHUNK, w), (lambda i, c: (i, nb - 1 - c, j)) if rev else (lambda i, c: (i, c, j)))
    st = pl.BlockSpec((1, 2, HEADS, HEAD_V, GLA_DK), lambda i, c: (i, 0, 0, 0, 0))
    ofw = pl.BlockSpec((1, cpb * CHUNK, GROUP_W), lambda i, c: (i, c, 0))
    obw = pl.BlockSpec((1, cpb * CHUNK, GROUP_W), lambda i, c: (i, nb - 1 - c, 0))
    return pl.pallas_call(
        functools.partial(_gla_scan_kernel, nb, cpb),
        grid=(b, nb),
        in_specs=[blk(GLA_QK, 0, False), blk(GLA_QK, 1, False), blk(GROUP_W, 1, False), blk(128, 6, False),
                  blk(GLA_QK, 0, True), blk(GLA_QK, 1, True), blk(GROUP_W, 1, True), blk(128, 6, True),
                  pl.BlockSpec(aup.shape, lambda i, c: (0, 0, 0)),
                  pl.BlockSpec(abias.shape, lambda i, c: (0, 0, 0)), st],
        out_specs=[ofw, obw, st],
        out_shape=[jax.ShapeDtypeStruct((b, l, GROUP_W), F32)] * 2
                  + [jax.ShapeDtypeStruct((b, 2, HEADS, HEAD_V, GLA_DK), F32)],
        scratch_shapes=[pltpu.VMEM((2, HEADS, HEAD_V, GLA_DK), F32)],
        compiler_params=_cparams("parallel", "arbitrary"),
        name="gla_scan",
    )(z, z, z, z, z, z, z, z, aup, abias, s0)


def _mix_kernel(x_ref, g1_ref, y5_ref, u5_ref, d5_ref, gw_ref, gb_ref,
                ryf_ref, ryb_ref, rbonus_ref, rg_ref, rlng_ref, rlnb_ref,
                tof_ref, tob_ref, tg_ref, tln_ref,
                gof_ref, gob_ref, gg_ref, gln_ref,
                bones_ref, wo_ref, o_ref, y5t_ref):
    bones = bones_ref[...]
    inv = 1.0 / HEAD_V

    def hmean(a):
        return _mm_right01(a, bones) * inv

    nch = y5_ref.shape[2]
    gph = LANES // S5_P
    y5g = [y5_ref[0, g] for g in range(S5_G)]
    for s in range(S5_T):
        for hf in range(GROUP_W // LANES):
            y5t_ref[hf, pl.ds(s, nch, stride=S5_T), :] = jnp.concatenate(
                [yg[:, s * S5_P:(s + 1) * S5_P] for yg in y5g[hf * gph:(hf + 1) * gph]], axis=1)
    y = jnp.concatenate([y5t_ref[0], y5t_ref[1]], axis=1) + d5_ref[...] * u5_ref[0]
    y = _gelu_tanh(y)
    ya = y * _sigmoid(_mm(y, gw_ref[...]) + gb_ref[...])
    yr = ryf_ref[0] + ryb_ref[0]
    dlt = yr - hmean(yr)
    yn = dlt * lax.rsqrt(hmean(dlt * dlt) + GN_EPS)
    yb = (yn * rlng_ref[...] + rlnb_ref[...] + rbonus_ref[0].astype(F32)) * rg_ref[0].astype(F32)
    ot = tof_ref[0] + tob_ref[0]
    yc = ot * lax.rsqrt(hmean(ot * ot) + EPS) * tln_ref[...] * _silu(tg_ref[0])
    og = gof_ref[0] + gob_ref[0]
    yd = og * lax.rsqrt(hmean(og * og) + EPS) * gln_ref[...] * _silu(gg_ref[0])
    mix = (_mm(ya, wo_ref[0]) + _mm(yb, wo_ref[1]) + _mm(yc, wo_ref[2]) + _mm(yd, wo_ref[3]))
    o_ref[0] = x_ref[0] + g1_ref[0] * mix


def _mix(x, g1, y5, u5, rw_pre, rw_y, zret, ret_o, zgla, gla_o, p, tm):
    b, l, d = x.shape
    tok = lambda w, j: pl.BlockSpec((1, tm, w), lambda i, t: (i, t, j))
    vec = lambda a: pl.BlockSpec(a.shape, lambda i, t: (0,) * a.ndim)
    t256 = tok(GROUP_W, 0)
    args = [x, g1, y5, u5, p['s5_d'], p['glu_w'], p['glu_b'],
            rw_y[0], rw_y[1], rw_pre[4], rw_pre[3], p['rw_ln_g'], p['rw_ln_b'],
            ret_o[0], ret_o[1], zret, p['ret_ln_g'],
            gla_o[0], gla_o[1], zgla, p['gla_ln_g'],
            p['bones'], p['w_out']]
    grouped = pl.BlockSpec((1, S5_G, tm // S5_T, S5_TP), lambda i, t: (i, 0, t, 0))
    specs = [tok(d, 0), pl.BlockSpec((1, 1, d), lambda i, t: (i, 0, 0)), grouped, t256,
             vec(p['s5_d']), vec(p['glu_w']), vec(p['glu_b']),
             t256, t256, t256, t256, vec(p['rw_ln_g']), vec(p['rw_ln_b']),
             t256, t256, tok(GROUP_W, 3), vec(p['ret_ln_g']),
             t256, t256, tok(GROUP_W, 2), vec(p['gla_ln_g']),
             vec(p['bones']), vec(p['w_out'])]
    return pl.pallas_call(
        _mix_kernel,
        grid=(b, l // tm),
        in_specs=specs,
        out_specs=tok(d, 0),
        out_shape=jax.ShapeDtypeStruct((b, l, d), F32),
        scratch_shapes=[pltpu.VMEM((GROUP_W // LANES, tm, LANES), F32)],
        compiler_params=_cparams("parallel", "parallel"),
        name="mix_outproj",
    )(*args)


def _mlp_kernel(final, nff, x_ref, g_ref, sc_ref, sh_ref, gate_ref, w1_ref, w2_ref, fg_ref, o_ref):
    x = x_ref[0]
    hb = _modnorm(x, g_ref[...], sc_ref[0], sh_ref[0]).astype(BF16)
    ff = w1_ref.shape[2] // nff
    acc = None
    for j in range(nff):
        a = jnp.maximum(jnp.dot(hb, w1_ref[0, :, j * ff:(j + 1) * ff], preferred_element_type=F32), 0.0)
        part = jnp.dot((a * a).astype(BF16), w2_ref[0, j * ff:(j + 1) * ff, :], preferred_element_type=F32)
        acc = part if acc is None else acc + part
    y = x + gate_ref[0] * acc
    if final:
        ms = jnp.mean(y * y, axis=-1, keepdims=True)
        y = y * lax.rsqrt(ms + EPS) * fg_ref[...]
    o_ref[0] = y


def _mlp(x, g, sc, sh, gate, w1, w2, layer, final_g, final, tm):
    b, l, d = x.shape
    tok = pl.BlockSpec((1, tm, d), lambda i, t: (i, t, 0))
    vec = pl.BlockSpec((1, 1, d), lambda i, t: (i, 0, 0))
    row = pl.BlockSpec((1, d), lambda i, t: (0, 0))
    once = lambda a: pl.BlockSpec((1,) + a.shape[1:], lambda i, t: (layer, 0, 0), pipeline_mode=pl.Buffered(1))
    return pl.pallas_call(
        functools.partial(_mlp_kernel, final, 4),
        grid=(b, l // tm),
        in_specs=[tok, row, vec, vec, vec, once(w1), once(w2), row],
        out_specs=tok,
        out_shape=jax.ShapeDtypeStruct((b, l, d), F32),
        compiler_params=_cparams("parallel", "parallel"),
        name="mlp",
    )(x, g, sc, sh, gate, w1, w2, final_g)


def kernel(x, c, ctx, c_ctx, ada_w, ada_b, norm1_g, norm2_g, w_in, w_out, s5_lam_re, s5_lam_im, s5_log_dt, s5_b_re, s5_b_im, s5_c_re, s5_c_im, s5_d, s5_glu_w, s5_glu_b, rw_mu, rw_w0, rw_w_up, rw_a0, rw_a_up, rw_g_up, rw_k_k, rw_k_a, rw_r_k, rw_ln_g, rw_ln_b, ret_decay_logit, ret_ln_g, gla_a_up, gla_a_b, gla_ln_g, mlp_w1, mlp_w2, final_g):
    b, l, d = x.shape
    lc = ctx.shape[1]
    depth = ada_w.shape[0]
    rows = l // GRID_W
    assert l % 512 == 0 and lc % CHUNK == 0 and lc % S5_T == 0 and d % 128 == 0

    cond = jnp.zeros((8, d), F32).at[:b].set(c).at[b].set(c_ctx)
    mod = _modulation(cond, ada_w, ada_b)

    lane = jnp.arange(GROUP_W)
    bones = (lane[:, None] // HEAD_V == lane[None, :] // HEAD_V).astype(BF16)
    rope_lat = _rope_tables(rows)
    ident = (jnp.ones((lc // CHUNK, 1, GROUP_W), F32), jnp.zeros((lc // CHUNK, 1, GROUP_W), F32),
             jnp.ones((GRID_W, GROUP_W), F32), jnp.zeros((GRID_W, GROUP_W), F32))
    nlev = max(1, (l // S5_T - 1).bit_length())
    c0, c1, c2 = GROUP_W, GROUP_W + RW_COLS, GROUP_W + RW_COLS + 4 * GROUP_W
    row2 = lambda a: a.reshape(1, -1).astype(F32)
    w_in_b, w_out_b, w1_b, w2_b = _to_bf16(w_in), _to_bf16(w_out), _to_bf16(mlp_w1), _to_bf16(mlp_w2)

    xc = ctx
    for i in range(depth):
        last = i == depth - 1
        m = mod[i].reshape(8, N_ADA, d)
        ml = m[:b, :, None, :]
        mc = jnp.broadcast_to(m[b][None, :, None, :], (b, N_ADA, 1, d))
        wi = w_in_b[i]
        w5, wr, wt = wi[:, :c0], wi[:, c0:c1], wi[:, c1:c2]
        wg = jnp.pad(wi[:, c2:], ((0, 0), (0, GLA_COLS_PAD - (wi.shape[1] - c2))))
        n1 = row2(norm1_g[i])
        n2 = row2(norm2_g[i])

        s5_tab = _s5_tables(s5_lam_re[i], s5_lam_im[i], s5_log_dt[i], s5_b_re[i], s5_b_im[i],
                            s5_c_re[i], s5_c_im[i], nlev)
        zpad = lambda a, lo, n: jnp.zeros((2, 128, n), F32).at[:, lo:lo + a.shape[1]].set(a).astype(BF16)
        rwp = dict(
            mu=row2(rw_mu[i]),
            w0=rw_w0[i].reshape(2, 1, GROUP_W), a0=rw_a0[i].reshape(2, 1, GROUP_W),
            wup=zpad(rw_w_up[i], 0, GROUP_W), aup=zpad(rw_a_up[i], RW_W_RANK, GROUP_W),
            gup=jnp.zeros((128, GROUP_W), F32).at[RW_W_RANK + RW_A_RANK:].set(rw_g_up[i]).astype(BF16),
            kk=row2(rw_k_k[i]), ka=row2(rw_k_a[i]), rk=row2(rw_r_k[i]), bones=bones)
        ret_tab = _ret_tables(ret_decay_logit[i], LA_CPB * CHUNK)
        gla_aup = zpad(gla_a_up[i], 0, GLA_QK)
        gla_ab = gla_a_b[i].reshape(2, 1, GLA_QK).astype(F32)
        mixp = dict(s5_d=row2(s5_d[i]), glu_w=s5_glu_w[i].astype(BF16), glu_b=row2(s5_glu_b[i]),
                    rw_ln_g=row2(rw_ln_g[i]), rw_ln_b=row2(rw_ln_b[i]), ret_ln_g=row2(ret_ln_g[i]),
                    gla_ln_g=row2(gla_ln_g[i]), bones=bones,
                    w_out=w_out_b[i].reshape(4, GROUP_W, d))

        def mixers(xx, mm, is_lat, states):
            tm = 512 if is_lat else lc
            z5, z5g, zr, zt, zg = _inproj(xx, n1, mm[:, 1], mm[:, 0], w5, wr, wt, wg, tm)
            y5, h5 = _s5_scan(z5g, s5_tab, states[0])
            pre = _rw_prep(zr, is_lat, rwp)
            yrf, yrb, srw = _rw_scan(pre, states[1])
            otf, otb, sret = _ret_scan(zt, ret_tab, rope_lat if is_lat else ident, states[2])
            ogf, ogb, sgla = _gla_scan(zg, gla_aup, gla_ab, states[3])
            outs = (z5, y5, pre, (yrf, yrb), zt, (otf, otb), zg, (ogf, ogb))
            return outs, (h5, srw, sret, sgla)

        def block(xx, mm, outs, is_lat, fin):
            tm = 512 if is_lat else lc
            z5, y5, pre, yr, zt, ot, zg, og = outs
            x1 = _mix(xx, mm[:, 2], y5, z5, pre, yr, zt, ot, zg, og, mixp, tm)
            return _mlp(x1, n2, mm[:, 4], mm[:, 3], mm[:, 5], w1_b, w2_b, i, row2(final_g), fin, tm)

        zeros = (jnp.zeros((b, 2, S5_G, 1, 2 * S5_N), F32),
                 jnp.zeros((b, 2, HEADS, HEAD_V, HEAD_V), F32),
                 jnp.zeros((b, 2, HEADS, HEAD_V, RET_DK), F32),
                 jnp.zeros((b, 2, HEADS, HEAD_V, GLA_DK), F32))
        outs_c, st_c = mixers(xc, mc, False, zeros)
        outs_l, _ = mixers(x, ml, True, st_c)
        x = block(x, ml, outs_l, True, last)
        if not last:
            xc = block(xc, mc, outs_c, False, False)
    return x
```

```python
import functools
import math

import jax
import jax.numpy as jnp
from jax import lax
from jax.experimental import pallas as pl
from jax.experimental.pallas import tpu as pltpu

F32 = jnp.float32
BF16 = jnp.bfloat16

LANES = 128
GRID_W = 64
GROUP_W = 256
N_ADA = 6
EPS = 1e-6
GN_EPS = 64e-5
CHUNK = 64
RW_CPB = 4
LA_CPB = 4
GLA_T = 128
GLA_BLOCK = 512
HEADS = 4
HEAD_V = 64
S5_P = 16
S5_G = 16
S5_N = 64
S5_T = 32
S5_TP = S5_T * S5_P
RW_COLS = 896
RW_W_RANK = 32
RW_A_RANK = 32
RW_G_RANK = 64
RET_DK = 64
GLA_DK = 32
GLA_QK = 128
GLA_RANK = 16
GLA_TAU = 16.0
GLA_COLS_PAD = 896
ROPE_BASE = 10000.0
VMEM_LIMIT = 56 * 1024 * 1024


def _cparams(*sem):
    return pltpu.CompilerParams(dimension_semantics=sem, vmem_limit_bytes=VMEM_LIMIT)


def _mm(a, b):
    return jnp.dot(a.astype(BF16), b.astype(BF16), preferred_element_type=F32)


def _mm_nt(a, b):
    return lax.dot_general(a.astype(BF16), b.astype(BF16), (((1,), (1,)), ((), ())),
                           preferred_element_type=F32)


def _mm_tn(a, b):
    return lax.dot_general(a.astype(BF16), b.astype(BF16), (((0,), (0,)), ((), ())),
                           preferred_element_type=F32)


def _split2(x):
    hi = x.astype(BF16)
    return hi, (x - hi.astype(F32)).astype(BF16)


def _join2(p):
    return p[0].astype(F32) + p[1].astype(F32)


def _mm_x3p(a, b):
    return (jnp.dot(a[0], b[0], preferred_element_type=F32) + jnp.dot(a[0], b[1], preferred_element_type=F32)
            + jnp.dot(a[1], b[0], preferred_element_type=F32))


def _chunk_tri01(n, t, reverse):
    ri = lax.broadcasted_iota(jnp.int32, (n, n), 0)
    ci = lax.broadcasted_iota(jnp.int32, (n, n), 1)
    shift = t.bit_length() - 1
    tri = (ci >= ri) if reverse else (ci <= ri)
    return jnp.where((ri >> shift) == (ci >> shift), jnp.where(tri, 1.0, 0.0), 0.0).astype(BF16)


def _split3(x):
    hi = x.astype(BF16)
    r1 = x - hi.astype(F32)
    mid = r1.astype(BF16)
    lo = (r1 - mid.astype(F32)).astype(BF16)
    return hi, mid, lo


def _mm_left01(m01, x):
    hi, mid, lo = _split3(x)
    return (jnp.dot(m01, hi, preferred_element_type=F32)
            + jnp.dot(m01, mid, preferred_element_type=F32)
            + jnp.dot(m01, lo, preferred_element_type=F32))


def _mm_right01(x, m01):
    hi, lo = _split2(x)
    return jnp.dot(hi, m01, preferred_element_type=F32) + jnp.dot(lo, m01, preferred_element_type=F32)


def _sigmoid(x):
    return 1.0 / (1.0 + jnp.exp(-x))


def _softplus(x):
    return jnp.maximum(x, 0.0) + jnp.log(1.0 + jnp.exp(-jnp.abs(x)))


def _silu(x):
    return x * _sigmoid(x)


def _gelu_tanh(x):
    return 0.5 * x * (1.0 + jnp.tanh(math.sqrt(2.0 / math.pi) * (x + 0.044715 * x * x * x)))


def _tri_masks(t, reverse):
    ri = lax.broadcasted_iota(jnp.int32, (t, t), 0)
    ci = lax.broadcasted_iota(jnp.int32, (t, t), 1)
    if reverse:
        return ci >= ri, ci > ri, ci == ri
    return ci <= ri, ci < ri, ci == ri


def _row_to_col(row, eye):
    n = row.shape[1]
    return jnp.sum(jnp.where(eye, jnp.broadcast_to(row, (n, n)), 0.0), axis=1, keepdims=True)


def _cast_kernel(x_ref, o_ref):
    o_ref[...] = x_ref[...].astype(o_ref.dtype)


def _to_bf16(w):
    depth, r, c = w.shape
    tr = 256
    spec = pl.BlockSpec((1, tr, c), lambda i, j: (i, j, 0))
    return pl.pallas_call(
        _cast_kernel,
        grid=(depth, r // tr),
        in_specs=[spec],
        out_specs=spec,
        out_shape=jax.ShapeDtypeStruct(w.shape, BF16),
        compiler_params=_cparams("parallel", "parallel"),
        name="cast_bf16",
    )(w)


def _mod_kernel(cond_ref, w_ref, b_ref, o_ref):
    c = cond_ref[...]
    o_ref[0] = _mm(_silu(c), w_ref[0]) + b_ref[0]


def _modulation(cond, ada_w, ada_b):
    depth, d, n = ada_w.shape
    tn = 1536
    return pl.pallas_call(
        _mod_kernel,
        grid=(depth, n // tn),
        in_specs=[pl.BlockSpec((8, d), lambda i, j: (0, 0)),
                  pl.BlockSpec((1, d, tn), lambda i, j: (i, 0, j)),
                  pl.BlockSpec((1, 1, tn), lambda i, j: (i, 0, j))],
        out_specs=pl.BlockSpec((1, 8, tn), lambda i, j: (i, 0, j)),
        out_shape=jax.ShapeDtypeStruct((depth, 8, n), F32),
        compiler_params=_cparams("parallel", "parallel"),
        name="adaln_mod",
    )(cond, ada_w, ada_b.reshape(depth, 1, n))


def _modnorm(x, g, sc, sh):
    ms = jnp.mean(x * x, axis=-1, keepdims=True)
    return x * lax.rsqrt(ms + EPS) * g * (1.0 + sc) + sh


def _inproj_kernel(x_ref, g_ref, sc_ref, sh_ref, w5_ref, wr_ref, wt_ref, wg_ref,
                   cr_ref, sr_ref, cc_ref, sn_ref,
                   o5_ref, o5g_ref, or_ref, ot_ref, og_ref, z5h_ref):
    hb = _modnorm(x_ref[0], g_ref[...], sc_ref[0], sh_ref[0]).astype(BF16)
    z5 = jnp.dot(hb, w5_ref[...], preferred_element_type=F32)
    o5_ref[0] = z5
    or_ref[0] = jnp.dot(hb, wr_ref[...], preferred_element_type=F32)
    og_ref[0] = jnp.dot(hb, wg_ref[...], preferred_element_type=F32)
    zt = jnp.dot(hb, wt_ref[...], preferred_element_type=F32)
    lane = lax.broadcasted_iota(jnp.int32, (GRID_W, GROUP_W), 1)
    by_row = (lane & 32) == 0
    nrow = cr_ref.shape[0]
    cos = jnp.concatenate([jnp.where(by_row, cr_ref[j], cc_ref[...]) for j in range(nrow)], axis=0)
    sin = jnp.concatenate([jnp.where(by_row, sr_ref[j], sn_ref[...]) for j in range(nrow)], axis=0)
    ot_ref[0, :, 0:GROUP_W] = _rope(zt[:, 0:GROUP_W], cos, sin)
    ot_ref[0, :, GROUP_W:2 * GROUP_W] = _rope(zt[:, GROUP_W:2 * GROUP_W] * RET_DK ** -0.5, cos, sin)
    ot_ref[0, :, 2 * GROUP_W:] = zt[:, 2 * GROUP_W:]
    nch = o5g_ref.shape[2]
    gph = LANES // S5_P
    for hf in range(GROUP_W // LANES):
        z5h_ref[hf] = z5[:, hf * LANES:(hf + 1) * LANES]
    for s in range(S5_T):
        for hf in range(GROUP_W // LANES):
            rows = z5h_ref[hf, pl.ds(s, nch, stride=S5_T), :]
            for g in range(gph):
                o5g_ref[0, hf * gph + g, :, s * S5_P:(s + 1) * S5_P] = rows[:, g * S5_P:(g + 1) * S5_P]


def _inproj(x, g, sc, sh, w5, wr, wt, wg, rope, tm):
    b, l, d = x.shape
    widths = (w5.shape[1], wr.shape[1], wt.shape[1], wg.shape[1])
    tok = lambda n: pl.BlockSpec((1, tm, n), lambda i, j: (i, j, 0))
    vec = pl.BlockSpec((1, 1, d), lambda i, j: (i, 0, 0))
    full = lambda w: pl.BlockSpec(w.shape, lambda i, j: (0, 0))
    grouped = pl.BlockSpec((1, S5_G, tm // S5_T, S5_TP), lambda i, j: (i, 0, j, 0))
    rowt = pl.BlockSpec((tm // GRID_W, 1, GROUP_W), lambda i, j: (j, 0, 0))
    cr, sr, cc, sn = rope
    return pl.pallas_call(
        _inproj_kernel,
        grid=(b, l // tm),
        in_specs=[tok(d), pl.BlockSpec((1, d), lambda i, j: (0, 0)), vec, vec,
                  full(w5), full(wr), full(wt), full(wg), rowt, rowt, full(cc), full(sn)],
        out_specs=[tok(widths[0]), grouped] + [tok(n) for n in widths[1:]],
        out_shape=[jax.ShapeDtypeStruct((b, l, widths[0]), F32),
                   jax.ShapeDtypeStruct((b, S5_G, l // S5_T, S5_TP), F32)]
                  + [jax.ShapeDtypeStruct((b, l, n), F32) for n in widths[1:]],
        scratch_shapes=[pltpu.VMEM((GROUP_W // LANES, tm, LANES), F32)],
        compiler_params=_cparams("parallel", "parallel"),
        name="norm_inproj",
    )(x, g, sc, sh, w5, wr, wt, wg, cr, sr, cc, sn)


def _s5_tables(lam_re, lam_im, log_dt, b_re, b_im, c_re, c_im, nlev):
    hp = lax.Precision.HIGHEST
    t = S5_T
    lam = lax.complex(jnp.minimum(lam_re.astype(F32), -1e-4), lam_im.astype(F32))
    ldt = lam * jnp.exp(log_dt.astype(F32))[..., None]
    a_bar = jnp.exp(ldt)
    bb = ((a_bar - 1.0) / lam)[..., None] * lax.complex(b_re.astype(F32), b_im.astype(F32))
    cm = lax.complex(c_re.astype(F32), c_im.astype(F32))
    tau = jnp.arange(t + 1, dtype=F32)
    apow = jnp.exp(ldt[:, :, None, :] * tau[None, None, :, None])
    taps = jnp.einsum('dgpn,dgtn,dgnq->dgtpq', cm, apow[:, :, :t], bb, precision=hp).real
    taprow = jnp.stack([taps[0], taps[1][:, ::-1]]).transpose(0, 1, 4, 2, 3).reshape(2, S5_G, S5_P, S5_TP)

    def pack(zc):
        return jnp.concatenate([zc.real, zc.imag], axis=-1)

    win_f = apow[0][:, t - 1 - jnp.arange(t), None, :] * bb[0].transpose(0, 2, 1)[:, None]
    win_b = apow[1][:, jnp.arange(t), None, :] * bb[1].transpose(0, 2, 1)[:, None]
    win = jnp.stack([pack(win_f), pack(win_b)]).reshape(2, S5_G, S5_TP, 2 * S5_N)
    ca_f = cm[0][:, None] * apow[0][:, 1 + jnp.arange(t), None, :]
    ca_b = cm[1][:, None] * apow[1][:, t - jnp.arange(t), None, :]

    def outpack(ca):
        w = jnp.concatenate([ca.real, -ca.imag], axis=-1)
        return w.reshape(S5_G, S5_TP, 2 * S5_N).transpose(0, 2, 1)

    wout = jnp.stack([outpack(ca_f), outpack(ca_b)])
    lev = (2.0 ** jnp.arange(nlev, dtype=F32)) * t
    pw = jnp.exp(ldt[:, :, None, :] * lev[None, None, :, None])
    p1 = jnp.concatenate([pw.real, pw.real], axis=-1)
    p2 = jnp.concatenate([-pw.imag, pw.imag], axis=-1)
    pw = jnp.stack([p1, p2], axis=3)
    return taprow, win.astype(BF16), wout.astype(BF16), pw


def _s5_kernel(nc, nlev, u_ref, tap_ref, win_ref, wout_ref, pw_ref, h0_ref, y_ref, hfin_ref, conv_ref):
    u = u_ref[0, 0].astype(BF16)
    row = lax.broadcasted_iota(jnp.int32, (nc, 2 * S5_N), 0)
    lane = lax.broadcasted_iota(jnp.int32, (S5_P, S5_TP), 1)
    for s in range(S5_T):
        lo = s * S5_P
        fwd = tap_ref[0, 0] if s == 0 else jnp.where(lane >= lo, pltpu.roll(tap_ref[0, 0], lo, 1), 0.0)
        hi = lo + S5_P
        bwd = tap_ref[1, 0] if hi == S5_TP else jnp.where(lane < hi, pltpu.roll(tap_ref[1, 0], hi, 1), 0.0)
        conv_ref[0, lo:hi, :] = fwd.astype(BF16)
        conv_ref[1, lo:hi, :] = bwd.astype(BF16)

    def cmul(x, d, j):
        return pw_ref[d, 0, j, 0:1] * x + pw_ref[d, 0, j, 1:2] * pltpu.roll(x, S5_N, 1)

    y = None
    for d in range(2):
        v = jnp.dot(u, win_ref[d, 0], preferred_element_type=F32)
        h0 = h0_ref[0, d, 0]
        if d == 0:
            x = jnp.where(row == 0, h0, pltpu.roll(v, 1, 0))
        else:
            x = jnp.where(row == nc - 1, h0, pltpu.roll(v, nc - 1, 0))
        for j in range(nlev):
            sh = 2 ** j
            if d == 0:
                xs = jnp.where(row >= sh, pltpu.roll(x, sh, 0), 0.0)
            else:
                xs = jnp.where(row < nc - sh, pltpu.roll(x, nc - sh, 0), 0.0)
            x = x + cmul(xs, d, j)
        last = nc - 1 if d == 0 else 0
        hfin_ref[0, d, 0] = cmul(x[last:last + 1], d, 0) + v[last:last + 1]
        yd = (jnp.dot(u, conv_ref[d], preferred_element_type=F32)
              + _mm(x, wout_ref[d, 0]))
        y = yd if y is None else y + yd
    y_ref[0, 0] = y


def _s5_scan(uf, tables, h0):
    conv, win, wout, pw = tables
    b, _, nc, _ = uf.shape
    nlev = max(1, (nc - 1).bit_length())
    pw = pw[:, :, :nlev]
    n2 = 2 * S5_N
    y, hfin = pl.pallas_call(
        functools.partial(_s5_kernel, nc, nlev),
        grid=(b, S5_G),
        in_specs=[pl.BlockSpec((1, 1, nc, S5_TP), lambda i, g: (i, g, 0, 0)),
                  pl.BlockSpec((2, 1, S5_P, S5_TP), lambda i, g: (0, g, 0, 0)),
                  pl.BlockSpec((2, 1, S5_TP, n2), lambda i, g: (0, g, 0, 0)),
                  pl.BlockSpec((2, 1, n2, S5_TP), lambda i, g: (0, g, 0, 0)),
                  pl.BlockSpec((2, 1, nlev, 2, n2), lambda i, g: (0, g, 0, 0, 0)),
                  pl.BlockSpec((1, 2, 1, 1, n2), lambda i, g: (i, 0, g, 0, 0))],
        out_specs=[pl.BlockSpec((1, 1, nc, S5_TP), lambda i, g: (i, g, 0, 0)),
                   pl.BlockSpec((1, 2, 1, 1, n2), lambda i, g: (i, 0, g, 0, 0))],
        out_shape=[jax.ShapeDtypeStruct((b, S5_G, nc, S5_TP), F32),
                   jax.ShapeDtypeStruct((b, 2, S5_G, 1, n2), F32)],
        scratch_shapes=[pltpu.VMEM((2, S5_TP, S5_TP), BF16)],
        compiler_params=_cparams("parallel", "parallel"),
        name="s5_scan",
    )(uf, conv, win, wout, pw, h0)
    return y, hfin


def _rw_prep_kernel(grid_shift, nt, z_ref, zp_ref, zn_ref, mu_ref, w0_ref, wup_ref, a0_ref, aup_ref,
                    gup_ref, kk_ref, ka_ref, rk_ref, bones_ref,
                    r_ref, v_ref, kkn_ref, g_ref, bonus_ref,
                    lw0_ref, kd0_ref, as0_ref, lw1_ref, kd1_ref, as1_ref):
    i = pl.program_id(1)
    z = z_ref[0]
    tm = z.shape[0]
    row = lax.broadcasted_iota(jnp.int32, z.shape, 0)
    lane = lax.broadcasted_iota(jnp.int32, z.shape, 1)
    prev1 = pltpu.roll(z, 1, 0)
    next1 = pltpu.roll(z, tm - 1, 0)
    if grid_shift:
        col = row & (GRID_W - 1)
        left = jnp.where(col == 0, 0.0, prev1)
        right = jnp.where(col == GRID_W - 1, 0.0, next1)
        zp = jnp.where(i > 0, zp_ref[0], 0.0)
        zn = jnp.where(i < nt - 1, zn_ref[0], 0.0)
        up = jnp.concatenate([zp, z[:tm - GRID_W]], axis=0)
        down = jnp.concatenate([z[GRID_W:], zn], axis=0)
        q = RW_COLS // 4
        shifted = jnp.where(lane < q, left,
                            jnp.where(lane < 2 * q, right, jnp.where(lane < 3 * q, up, down)))
    else:
        prev = jnp.where(row == 0, 0.0, prev1)
        nxt = jnp.where(row == tm - 1, 0.0, next1)
        shifted = jnp.where(lane < RW_COLS // 2, prev, nxt)
    zm = z + mu_ref[...] * (shifted - z)
    r = zm[:, 0:256]
    k = zm[:, 256:512]
    v = zm[:, 512:768]
    lo = zm[:, 768:896]
    bones = bones_ref[...]
    g_ref[0] = _mm(_sigmoid(lo), gup_ref[...]).astype(BF16)
    kk = k * kk_ref[...]
    kk = kk * lax.rsqrt(_mm_right01(kk * kk, bones) + 1e-12)
    r_ref[0] = r.astype(BF16)
    v_ref[0] = v.astype(BF16)
    kkn_ref[0] = kk.astype(BF16)
    bonus_ref[0] = (_mm_right01(r * k * rk_ref[...], bones) * v).astype(BF16)
    tlo = jnp.tanh(lo)
    for d, (lw_ref, kd_ref, as_ref) in enumerate(((lw0_ref, kd0_ref, as0_ref), (lw1_ref, kd1_ref, as1_ref))):
        w_raw = -_softplus(-(w0_ref[d] + _mm(tlo, wup_ref[d]))) - 0.5
        lw_ref[0] = -jnp.exp(w_raw)
        a = _sigmoid(a0_ref[d] + _mm(lo, aup_ref[d]))
        kd_ref[0] = (k * (1.0 + (a - 1.0) * ka_ref[...])).astype(BF16)
        as_ref[0] = a.astype(BF16)


def _rw_prep(z, grid_shift, p):
    b, l, _ = z.shape
    tm = 256 if grid_shift else l
    nt = l // tm
    hb = tm // GRID_W
    nhb = l // GRID_W
    tok = pl.BlockSpec((1, tm, GROUP_W), lambda i, j: (i, j, 0))
    full = lambda a: pl.BlockSpec(a.shape, lambda i, j: (0,) * a.ndim)
    params = (p['mu'], p['w0'], p['wup'], p['a0'], p['aup'], p['gup'], p['kk'], p['ka'], p['rk'], p['bones'])
    return pl.pallas_call(
        functools.partial(_rw_prep_kernel, grid_shift, nt),
        grid=(b, nt),
        in_specs=[pl.BlockSpec((1, tm, RW_COLS), lambda i, j: (i, j, 0)),
                  pl.BlockSpec((1, GRID_W, RW_COLS), lambda i, j: (i, jnp.maximum(j * hb - 1, 0), 0)),
                  pl.BlockSpec((1, GRID_W, RW_COLS), lambda i, j: (i, jnp.minimum((j + 1) * hb, nhb - 1), 0))]
                 + [full(a) for a in params],
        out_specs=[tok] * 11,
        out_shape=[jax.ShapeDtypeStruct((b, l, GROUP_W), dt)
                   for dt in (BF16,) * 5 + (F32, BF16, BF16) * 2],
        compiler_params=_cparams("parallel", "parallel"),
        name="rwkv_prep",
    )(z, z, z, *params)


def _rw_scan_kernel(nb, cpb, rf_ref, vf_ref, kkf_ref, lwf_ref, kdf_ref, asf_ref,
                    rb_ref, vb_ref, kkb_ref, lwb_ref, kdb_ref, asb_ref, s0_ref,
                    yf_ref, yb_ref, sfin_ref, st_ref):
    c = pl.program_id(1)

    @pl.when(c == 0)
    def _():
        st_ref[...] = s0_ref[0]

    t = CHUNK
    ri = lax.broadcasted_iota(jnp.int32, (t, t), 0)
    ci = lax.broadcasted_iota(jnp.int32, (t, t), 1)
    eye = ri == ci
    eyef = jnp.where(eye, 1.0, 0.0)
    same = lambda s: jnp.where((ri >> s) == (ci >> s), 1.0, 0.0)
    m4, m8, m16, m32 = same(2), same(3), same(4), same(5)
    merge_masks = (m8 - m4, m16 - m8, m32 - m16, 1.0 - m32)
    dir_refs = ((rf_ref, vf_ref, kkf_ref, lwf_ref, kdf_ref, asf_ref),
                (rb_ref, vb_ref, kkb_ref, lwb_ref, kdb_ref, asb_ref))
    units = []
    for d, (r_ref, v_ref, kk_ref, lw_ref, kd_ref, as_ref) in enumerate(dir_refs):
        incl = (ci >= ri) if d == 1 else (ci <= ri)
        strict = (ci > ri) if d == 1 else (ci < ri)
        r, v, kk, kd = (x[0].astype(F32) for x in (r_ref, v_ref, kk_ref, kd_ref))
        lw = lw_ref[0]
        cin = _mm_left01(_chunk_tri01(cpb * t, t, d == 1), lw)
        e_in = jnp.exp(cin)
        e_neg = jnp.exp(-cin)
        rt = r * e_in
        at = -kk * jnp.exp(cin - lw)
        bvec = kk * as_ref[0].astype(F32)
        bt = bvec * e_neg
        kt = kd * e_neg
        for j in range(cpb):
            rows = slice(j * t, (j + 1) * t)
            last = j * t + (0 if d == 1 else t - 1)
            clast = cin[last:last + 1]
            dl = jnp.exp(clast - cin[rows])
            bh = bvec[rows] * dl
            kh = kd[rows] * dl
            dec = jnp.exp(clast)
            for h in range(HEADS):
                sl = slice(h * HEAD_V, (h + 1) * HEAD_V)
                units.append(dict(d=d, j=j, h=h, incl=incl, strict=strict,
                                  at=at[rows, sl], rt=rt[rows, sl], bt=bt[rows, sl], kt=kt[rows, sl],
                                  bh=bh[:, sl], kh=kh[:, sl], v=v[rows, sl], dec=dec[:, sl]))

    a_all = [_mm_nt(jnp.concatenate([u['at'], u['rt']], axis=0),
                    jnp.concatenate([u['bt'], u['kt']], axis=0)) for u in units]
    nmat = [jnp.where(u['strict'], a[0:t, 0:t], 0.0) for u, a in zip(units, a_all)]
    a_kk = [jnp.concatenate([jnp.where(u['strict'], a[0:t, t:2 * t], 0.0),
                             jnp.where(u['incl'], a[t:2 * t, t:2 * t], 0.0)], axis=0)
            for u, a in zip(units, a_all)]
    a_rb = [jnp.where(u['incl'], a[t:2 * t, 0:t], 0.0) for u, a in zip(units, a_all)]
    akv = [_mm(a, u['v']) for u, a in zip(units, a_kk)]
    kv = [_mm_tn(u['v'], u['kh']) for u in units]
    nd = [x * m4 for x in nmat]
    n2 = [_mm(x, x) for x in nd]
    tinv = [eyef + x + _mm(eyef + x, y) for x, y in zip(nd, n2)]
    for mk in merge_masks:
        w = [_mm(ti, x * mk) for ti, x in zip(tinv, nmat)]
        tinv = [ti + _mm(wi, ti) for ti, wi in zip(tinv, w)]
    zz = [_mm(ti, jnp.concatenate([u['at'], kvv[0:t]], axis=1)) for ti, u, kvv in zip(tinv, units, akv)]
    ght = [_mm_tn(z, u['bh']) for u, z in zip(units, zz)]
    qy = [_mm(a, z) for a, z in zip(a_rb, zz)]
    qmat = [u['rt'] + x[:, :HEAD_V] for u, x in zip(units, qy)]
    y0 = [x[:, HEAD_V:] + kvv[t:2 * t] for x, kvv in zip(qy, akv)]
    gmat = [x[:HEAD_V] for x in ght]
    hmat = [x[HEAD_V:] + k2 for x, k2 in zip(ght, kv)]
    idx = {(u['d'], u['j'], u['h']): n for n, u in enumerate(units)}
    state = {(d, h): st_ref[d, h] for d in range(2) for h in range(HEADS)}
    ys = {}
    for step in range(cpb):
        for d in range(2):
            j = step if d == 0 else cpb - 1 - step
            for h in range(HEADS):
                n = idx[(d, j, h)]
                st = state[(d, h)]
                ys[(d, j, h)] = _mm_nt(qmat[n], st) + y0[n]
                state[(d, h)] = units[n]['dec'] * st + _mm(st, gmat[n]) + hmat[n]
    for d, y_ref in enumerate((yf_ref, yb_ref)):
        y_ref[0] = jnp.concatenate(
            [jnp.concatenate([ys[(d, j, h)] for h in range(HEADS)], axis=1) for j in range(cpb)], axis=0)
        for h in range(HEADS):
            st_ref[d, h] = state[(d, h)]

    @pl.when(c == nb - 1)
    def _():
        sfin_ref[0] = st_ref[...]


def _rw_scan(pre, s0):
    r, v, kk, _, _, lw0, kd0, as0, lw1, kd1, as1 = pre
    b, l, _ = r.shape
    cpb = min(RW_CPB, l // CHUNK)
    nb = l // (cpb * CHUNK)
    fw = pl.BlockSpec((1, cpb * CHUNK, GROUP_W), lambda i, c: (i, c, 0))
    bw = pl.BlockSpec((1, cpb * CHUNK, GROUP_W), lambda i, c: (i, nb - 1 - c, 0))
    st = pl.BlockSpec((1, 2, HEADS, HEAD_V, HEAD_V), lambda i, c: (i, 0, 0, 0, 0))
    return pl.pallas_call(
        functools.partial(_rw_scan_kernel, nb, cpb),
        grid=(b, nb),
        in_specs=[fw] * 6 + [bw] * 6 + [st],
        out_specs=[fw, bw, st],
        out_shape=[jax.ShapeDtypeStruct((b, l, GROUP_W), F32)] * 2
                  + [jax.ShapeDtypeStruct((b, 2, HEADS, HEAD_V, HEAD_V), F32)],
        scratch_shapes=[pltpu.VMEM((2, HEADS, HEAD_V, HEAD_V), F32)],
        compiler_params=_cparams("parallel", "arbitrary"),
        name="rwkv_scan",
    )(r, v, kk, lw0, kd0, as0, r, v, kk, lw1, kd1, as1, s0)


def _rope(x, cos, sin_signed):
    lane = lax.broadcasted_iota(jnp.int32, x.shape, 1)
    n = x.shape[1]
    swapped = jnp.where((lane & 16) == 0, pltpu.roll(x, n - 16, 1), pltpu.roll(x, 16, 1))
    return x * cos + swapped * sin_signed


def _ret_scan_kernel(nb, qf_ref, kf_ref, vf_ref, qb_ref, kb_ref, vb_ref,
                     dmat_ref, qdec_ref, kdec_ref, sdec_ref, s0_ref,
                     of_ref, ob_ref, sfin_ref, st_ref):
    c = pl.program_id(1)

    @pl.when(c == 0)
    def _():
        st_ref[...] = s0_ref[0]

    qs, ks, qds, khs, vs = [], [], [], [], []
    for d, (q_ref, k_ref, v_ref) in enumerate(((qf_ref, kf_ref, vf_ref), (qb_ref, kb_ref, vb_ref))):
        q, k, v = q_ref[0], k_ref[0], v_ref[0]
        qd = q * qdec_ref[d]
        kh = k * kdec_ref[d]
        for h in range(HEADS):
            sl = slice(h * HEAD_V, (h + 1) * HEAD_V)
            qs.append(q[:, sl])
            ks.append(k[:, sl])
            qds.append(qd[:, sl])
            khs.append(kh[:, sl])
            vs.append(v[:, sl])
    scores = [_mm_nt(q, k).astype(BF16) * dmat_ref[i // HEADS, i % HEADS]
              for i, (q, k) in enumerate(zip(qs, ks))]
    states = [st_ref[i // HEADS, i % HEADS] for i in range(2 * HEADS)]
    outs = [_mm(p, v) + _mm_nt(qd, st) for p, v, qd, st in zip(scores, vs, qds, states)]
    for i, (st, kh, v) in enumerate(zip(states, khs, vs)):
        d, h = i // HEADS, i % HEADS
        st_ref[d, h] = st * sdec_ref[d][:, h * HEAD_V:(h + 1) * HEAD_V] + _mm_tn(v, kh)
    of_ref[0] = jnp.concatenate(outs[:HEADS], axis=1)
    ob_ref[0] = jnp.concatenate(outs[HEADS:], axis=1)

    @pl.when(c == nb - 1)
    def _():
        sfin_ref[0] = st_ref[...]


def _ret_tables(decay_logit, n):
    lg = jax.nn.log_sigmoid(decay_logit.astype(F32))
    pos = jnp.arange(n, dtype=F32)
    lag = pos[:, None] - pos[None, :]
    lag = jnp.stack([lag, -lag])
    dmat = jnp.where(lag[:, None] >= 0, jnp.exp(lg[:, :, None, None] * lag[:, None]), 0.0)
    lanes = jnp.repeat(lg, RET_DK, axis=-1)[:, None, :]
    qpow = jnp.stack([pos + 1.0, n - pos])[:, :, None]
    kpow = jnp.stack([n - 1.0 - pos, pos])[:, :, None]
    return dmat.astype(BF16), jnp.exp(lanes * qpow), jnp.exp(lanes * kpow), jnp.exp(lanes * n)


def _gla_scan_kernel(nb, cpb, qf_ref, kf_ref, vf_ref, af_ref, qb_ref, kb_ref, vb_ref, ab_ref,
                     aup_ref, abias_ref, s0_ref, of_ref, ob_ref, sfin_ref, st_ref):
    c = pl.program_id(1)

    @pl.when(c == 0)
    def _():
        st_ref[...] = s0_ref[0]

    t = GLA_T
    n = cpb * t
    dk = GLA_DK
    scale = dk ** -0.5
    dirs = ((qf_ref, kf_ref, vf_ref, af_ref), (qb_ref, kb_ref, vb_ref, ab_ref))
    units = []
    for d, (q_ref, k_ref, v_ref, a_ref) in enumerate(dirs):
        incl = _tri_masks(t, d == 1)[0]
        q, k, v = q_ref[0], k_ref[0] * scale, v_ref[0]
        lw = -_softplus(-(_mm(a_ref[0], aup_ref[d]) + abias_ref[d])) * (1.0 / GLA_TAU)
        cin = _mm_left01(_chunk_tri01(n, t, d == 1), lw)
        qe = q * jnp.exp(cin)
        for j in range(cpb):
            rows = slice(j * t, (j + 1) * t)
            last = j * t + (0 if d == 1 else t - 1)
            mid = j * t + (t // 2 if d == 1 else t // 2 - 1)
            cj = cin[rows]
            clast = cin[last:last + 1]
            cmid = cin[mid:mid + 1]
            qt = q[rows] * jnp.exp(cj - cmid)
            kt = k[rows] * jnp.exp(cmid - cj)
            kh = k[rows] * jnp.exp(clast - cj)
            dec = jnp.exp(clast)
            for h in range(HEADS):
                sk = slice(h * dk, (h + 1) * dk)
                sv = slice(h * HEAD_V, (h + 1) * HEAD_V)
                units.append(dict(d=d, j=j, h=h, incl=incl, qt=qt[:, sk], kt=kt[:, sk], qe=qe[rows, sk],
                                  kh=kh[:, sk], v=v[rows, sv], dec=dec[:, sk]))

    amat = [jnp.where(u['incl'], _mm_nt(u['qt'], u['kt']), 0.0) for u in units]
    intra = [_mm(a, u['v']) for a, u in zip(amat, units)]
    kv = [_mm_tn(u['v'], u['kh']) for u in units]
    idx = {(u['d'], u['j'], u['h']): i for i, u in enumerate(units)}
    entering = {}
    for d in range(2):
        for h in range(HEADS):
            st = st_ref[d, h]
            for step in range(cpb):
                j = step if d == 0 else cpb - 1 - step
                i = idx[(d, j, h)]
                entering[i] = st
                st = units[i]['dec'] * st + kv[i]
            st_ref[d, h] = st
    outs = [x + _mm_nt(u['qe'], entering[i]) for i, (x, u) in enumerate(zip(intra, units))]
    for d, o_ref in enumerate((of_ref, ob_ref)):
        o_ref[0] = jnp.concatenate(
            [jnp.concatenate([outs[idx[(d, j, h)]] for h in range(HEADS)], axis=1) for j in range(cpb)], axis=0)

    @pl.when(c == nb - 1)
    def _():
        sfin_ref[0] = st_ref[...]


def _rope_tables(rows):
    nf = RET_DK // 4
    inv = ROPE_BASE ** (-jnp.arange(nf, dtype=F32) / nf)
    lane = jnp.arange(GROUP_W)
    freq = inv[lane % nf]
    sign = jnp.where((lane & 16) == 0, -1.0, 1.0)
    ar = jnp.arange(rows, dtype=F32)[:, None] * freq[None, :]
    ac = jnp.arange(GRID_W, dtype=F32)[:, None] * freq[None, :]
    return (jnp.cos(ar).reshape(rows, 1, GROUP_W), (jnp.sin(ar) * sign).reshape(rows, 1, GROUP_W),
            jnp.cos(ac), jnp.sin(ac) * sign)


def _ret_scan(z, tables, s0):
    b, l, _ = z.shape
    n = LA_CPB * CHUNK
    nb = l // n
    fw = lambda j: pl.BlockSpec((1, n, GROUP_W), lambda i, c: (i, c, j))
    bw = lambda j: pl.BlockSpec((1, n, GROUP_W), lambda i, c: (i, nb - 1 - c, j))
    full = lambda a: pl.BlockSpec(a.shape, lambda i, c: (0,) * a.ndim)
    st = pl.BlockSpec((1, 2, HEADS, HEAD_V, RET_DK), lambda i, c: (i, 0, 0, 0, 0))
    return pl.pallas_call(
        functools.partial(_ret_scan_kernel, nb),
        grid=(b, nb),
        in_specs=[fw(0), fw(1), fw(2), bw(0), bw(1), bw(2)] + [full(a) for a in tables] + [st],
        out_specs=[fw(0), bw(0), st],
        out_shape=[jax.ShapeDtypeStruct((b, l, GROUP_W), F32)] * 2
                  + [jax.ShapeDtypeStruct((b, 2, HEADS, HEAD_V, RET_DK), F32)],
        scratch_shapes=[pltpu.VMEM((2, HEADS, HEAD_V, RET_DK), F32)],
        compiler_params=_cparams("parallel", "arbitrary"),
        name="retention_scan",
    )(z, z, z, z, z, z, *tables, s0)


def _gla_scan(z, aup, abias, s0):
    b, l, _ = z.shape
    cpb = min(GLA_BLOCK, l) // CHUNK
    nb = l // (cpb * CHUNK)
    blk = lambda w, j, rev: pl.BlockSpec(
        (1, cpb * CHUNK, w), (lambda i, c: (i, nb - 1 - c, j)) if rev else (lambda i, c: (i, c, j)))
    st = pl.BlockSpec((1, 2, HEADS, HEAD_V, GLA_DK), lambda i, c: (i, 0, 0, 0, 0))
    ofw = pl.BlockSpec((1, cpb * CHUNK, GROUP_W), lambda i, c: (i, c, 0))
    obw = pl.BlockSpec((1, cpb * CHUNK, GROUP_W), lambda i, c: (i, nb - 1 - c, 0))
    return pl.pallas_call(
        functools.partial(_gla_scan_kernel, nb, cpb * CHUNK // GLA_T),
        grid=(b, nb),
        in_specs=[blk(GLA_QK, 0, False), blk(GLA_QK, 1, False), blk(GROUP_W, 1, False), blk(128, 6, False),
                  blk(GLA_QK, 0, True), blk(GLA_QK, 1, True), blk(GROUP_W, 1, True), blk(128, 6, True),
                  pl.BlockSpec(aup.shape, lambda i, c: (0, 0, 0)),
                  pl.BlockSpec(abias.shape, lambda i, c: (0, 0, 0)), st],
        out_specs=[ofw, obw, st],
        out_shape=[jax.ShapeDtypeStruct((b, l, GROUP_W), F32)] * 2
                  + [jax.ShapeDtypeStruct((b, 2, HEADS, HEAD_V, GLA_DK), F32)],
        scratch_shapes=[pltpu.VMEM((2, HEADS, HEAD_V, GLA_DK), F32)],
        compiler_params=_cparams("parallel", "arbitrary"),
        name="gla_scan",
    )(z, z, z, z, z, z, z, z, aup, abias, s0)


def _mix_kernel(x_ref, g1_ref, y5_ref, u5_ref, d5_ref, gw_ref, gb_ref,
                ryf_ref, ryb_ref, rbonus_ref, rg_ref, rlng_ref, rlnb_ref,
                tof_ref, tob_ref, tg_ref, tln_ref,
                gof_ref, gob_ref, gg_ref, gln_ref,
                bones_ref, wo_ref, o_ref, y5t_ref):
    bones = bones_ref[...]
    inv = 1.0 / HEAD_V

    def hmean(a):
        return _mm_right01(a, bones) * inv

    nch = y5_ref.shape[2]
    gph = LANES // S5_P
    y5g = [y5_ref[0, g] for g in range(S5_G)]
    for s in range(S5_T):
        for hf in range(GROUP_W // LANES):
            y5t_ref[hf, pl.ds(s, nch, stride=S5_T), :] = jnp.concatenate(
                [yg[:, s * S5_P:(s + 1) * S5_P] for yg in y5g[hf * gph:(hf + 1) * gph]], axis=1)
    y = jnp.concatenate([y5t_ref[0], y5t_ref[1]], axis=1) + d5_ref[...] * u5_ref[0]
    y = _gelu_tanh(y)
    ya = y * _sigmoid(_mm(y, gw_ref[...]) + gb_ref[...])
    yr = ryf_ref[0] + ryb_ref[0]
    dlt = yr - hmean(yr)
    yn = dlt * lax.rsqrt(hmean(dlt * dlt) + GN_EPS)
    yb = (yn * rlng_ref[...] + rlnb_ref[...] + rbonus_ref[0].astype(F32)) * rg_ref[0].astype(F32)
    ot = tof_ref[0] + tob_ref[0]
    yc = ot * lax.rsqrt(hmean(ot * ot) + EPS) * tln_ref[...] * _silu(tg_ref[0])
    og = gof_ref[0] + gob_ref[0]
    yd = og * lax.rsqrt(hmean(og * og) + EPS) * gln_ref[...] * _silu(gg_ref[0])
    mix = (_mm(ya, wo_ref[0]) + _mm(yb, wo_ref[1]) + _mm(yc, wo_ref[2]) + _mm(yd, wo_ref[3]))
    o_ref[0] = x_ref[0] + g1_ref[0] * mix


def _mix(x, g1, y5, u5, rw_pre, rw_y, zret, ret_o, zgla, gla_o, p, tm):
    b, l, d = x.shape
    tok = lambda w, j: pl.BlockSpec((1, tm, w), lambda i, t: (i, t, j))
    vec = lambda a: pl.BlockSpec(a.shape, lambda i, t: (0,) * a.ndim)
    t256 = tok(GROUP_W, 0)
    args = [x, g1, y5, u5, p['s5_d'], p['glu_w'], p['glu_b'],
            rw_y[0], rw_y[1], rw_pre[4], rw_pre[3], p['rw_ln_g'], p['rw_ln_b'],
            ret_o[0], ret_o[1], zret, p['ret_ln_g'],
            gla_o[0], gla_o[1], zgla, p['gla_ln_g'],
            p['bones'], p['w_out']]
    grouped = pl.BlockSpec((1, S5_G, tm // S5_T, S5_TP), lambda i, t: (i, 0, t, 0))
    specs = [tok(d, 0), pl.BlockSpec((1, 1, d), lambda i, t: (i, 0, 0)), grouped, t256,
             vec(p['s5_d']), vec(p['glu_w']), vec(p['glu_b']),
             t256, t256, t256, t256, vec(p['rw_ln_g']), vec(p['rw_ln_b']),
             t256, t256, tok(GROUP_W, 3), vec(p['ret_ln_g']),
             t256, t256, tok(GROUP_W, 2), vec(p['gla_ln_g']),
             vec(p['bones']), vec(p['w_out'])]
    return pl.pallas_call(
        _mix_kernel,
        grid=(b, l // tm),
        in_specs=specs,
        out_specs=tok(d, 0),
        out_shape=jax.ShapeDtypeStruct((b, l, d), F32),
        scratch_shapes=[pltpu.VMEM((GROUP_W // LANES, tm, LANES), F32)],
        compiler_params=_cparams("parallel", "parallel"),
        name="mix_outproj",
    )(*args)


def _mlp_kernel(final, nff, x_ref, g_ref, sc_ref, sh_ref, gate_ref, w1_ref, w2_ref, fg_ref, o_ref):
    x = x_ref[0]
    hb = _modnorm(x, g_ref[...], sc_ref[0], sh_ref[0]).astype(BF16)
    ff = w1_ref.shape[2] // nff
    acc = None
    for j in range(nff):
        a = jnp.maximum(jnp.dot(hb, w1_ref[0, :, j * ff:(j + 1) * ff], preferred_element_type=F32), 0.0)
        part = jnp.dot((a * a).astype(BF16), w2_ref[0, j * ff:(j + 1) * ff, :], preferred_element_type=F32)
        acc = part if acc is None else acc + part
    y = x + gate_ref[0] * acc
    if final:
        ms = jnp.mean(y * y, axis=-1, keepdims=True)
        y = y * lax.rsqrt(ms + EPS) * fg_ref[...]
    o_ref[0] = y


def _mlp(x, g, sc, sh, gate, w1, w2, layer, final_g, final, tm):
    b, l, d = x.shape
    tok = pl.BlockSpec((1, tm, d), lambda i, t: (i, t, 0))
    vec = pl.BlockSpec((1, 1, d), lambda i, t: (i, 0, 0))
    row = pl.BlockSpec((1, d), lambda i, t: (0, 0))
    once = lambda a: pl.BlockSpec((1,) + a.shape[1:], lambda i, t: (layer, 0, 0), pipeline_mode=pl.Buffered(1))
    return pl.pallas_call(
        functools.partial(_mlp_kernel, final, 4),
        grid=(b, l // tm),
        in_specs=[tok, row, vec, vec, vec, once(w1), once(w2), row],
        out_specs=tok,
        out_shape=jax.ShapeDtypeStruct((b, l, d), F32),
        compiler_params=_cparams("parallel", "parallel"),
        name="mlp",
    )(x, g, sc, sh, gate, w1, w2, final_g)


def kernel(x, c, ctx, c_ctx, ada_w, ada_b, norm1_g, norm2_g, w_in, w_out, s5_lam_re, s5_lam_im, s5_log_dt, s5_b_re, s5_b_im, s5_c_re, s5_c_im, s5_d, s5_glu_w, s5_glu_b, rw_mu, rw_w0, rw_w_up, rw_a0, rw_a_up, rw_g_up, rw_k_k, rw_k_a, rw_r_k, rw_ln_g, rw_ln_b, ret_decay_logit, ret_ln_g, gla_a_up, gla_a_b, gla_ln_g, mlp_w1, mlp_w2, final_g):
    b, l, d = x.shape
    lc = ctx.shape[1]
    depth = ada_w.shape[0]
    rows = l // GRID_W
    assert l % 512 == 0 and lc % CHUNK == 0 and lc % S5_T == 0 and d % 128 == 0

    cond = jnp.zeros((8, d), F32).at[:b].set(c).at[b].set(c_ctx)
    mod = _modulation(cond, ada_w, ada_b)

    lane = jnp.arange(GROUP_W)
    bones = (lane[:, None] // HEAD_V == lane[None, :] // HEAD_V).astype(BF16)
    rope_lat = _rope_tables(rows)
    ident = (jnp.ones((lc // CHUNK, 1, GROUP_W), F32), jnp.zeros((lc // CHUNK, 1, GROUP_W), F32),
             jnp.ones((GRID_W, GROUP_W), F32), jnp.zeros((GRID_W, GROUP_W), F32))
    nlev = max(1, (l // S5_T - 1).bit_length())
    c0, c1, c2 = GROUP_W, GROUP_W + RW_COLS, GROUP_W + RW_COLS + 4 * GROUP_W
    row2 = lambda a: a.reshape(1, -1).astype(F32)
    w_in_b, w_out_b, w1_b, w2_b = _to_bf16(w_in), _to_bf16(w_out), _to_bf16(mlp_w1), _to_bf16(mlp_w2)

    xc = ctx
    for i in range(depth):
        last = i == depth - 1
        m = mod[i].reshape(8, N_ADA, d)
        ml = m[:b, :, None, :]
        mc = jnp.broadcast_to(m[b][None, :, None, :], (b, N_ADA, 1, d))
        wi = w_in_b[i]
        w5, wr, wt = wi[:, :c0], wi[:, c0:c1], wi[:, c1:c2]
        wg = jnp.pad(wi[:, c2:], ((0, 0), (0, GLA_COLS_PAD - (wi.shape[1] - c2))))
        n1 = row2(norm1_g[i])
        n2 = row2(norm2_g[i])

        s5_tab = _s5_tables(s5_lam_re[i], s5_lam_im[i], s5_log_dt[i], s5_b_re[i], s5_b_im[i],
                            s5_c_re[i], s5_c_im[i], nlev)
        zpad = lambda a, lo, n: jnp.zeros((2, 128, n), F32).at[:, lo:lo + a.shape[1]].set(a).astype(BF16)
        rwp = dict(
            mu=row2(rw_mu[i]),
            w0=rw_w0[i].reshape(2, 1, GROUP_W), a0=rw_a0[i].reshape(2, 1, GROUP_W),
            wup=zpad(rw_w_up[i], 0, GROUP_W), aup=zpad(rw_a_up[i], RW_W_RANK, GROUP_W),
            gup=jnp.zeros((128, GROUP_W), F32).at[RW_W_RANK + RW_A_RANK:].set(rw_g_up[i]).astype(BF16),
            kk=row2(rw_k_k[i]), ka=row2(rw_k_a[i]), rk=row2(rw_r_k[i]), bones=bones)
        ret_tab = _ret_tables(ret_decay_logit[i], LA_CPB * CHUNK)
        gla_aup = zpad(gla_a_up[i], 0, GLA_QK)
        gla_ab = gla_a_b[i].reshape(2, 1, GLA_QK).astype(F32)
        mixp = dict(s5_d=row2(s5_d[i]), glu_w=s5_glu_w[i].astype(BF16), glu_b=row2(s5_glu_b[i]),
                    rw_ln_g=row2(rw_ln_g[i]), rw_ln_b=row2(rw_ln_b[i]), ret_ln_g=row2(ret_ln_g[i]),
                    gla_ln_g=row2(gla_ln_g[i]), bones=bones,
                    w_out=w_out_b[i].reshape(4, GROUP_W, d))

        def mixers(xx, mm, is_lat, states):
            tm = 512 if is_lat else lc
            z5, z5g, zr, zt, zg = _inproj(xx, n1, mm[:, 1], mm[:, 0], w5, wr, wt, wg,
                                          rope_lat if is_lat else ident, tm)
            y5, h5 = _s5_scan(z5g, s5_tab, states[0])
            pre = _rw_prep(zr, is_lat, rwp)
            yrf, yrb, srw = _rw_scan(pre, states[1])
            otf, otb, sret = _ret_scan(zt, ret_tab, states[2])
            ogf, ogb, sgla = _gla_scan(zg, gla_aup, gla_ab, states[3])
            outs = (z5, y5, pre, (yrf, yrb), zt, (otf, otb), zg, (ogf, ogb))
            return outs, (h5, srw, sret, sgla)

        def block(xx, mm, outs, is_lat, fin):
            tm = 512 if is_lat else lc
            z5, y5, pre, yr, zt, ot, zg, og = outs
            x1 = _mix(xx, mm[:, 2], y5, z5, pre, yr, zt, ot, zg, og, mixp, tm)
            return _mlp(x1, n2, mm[:, 4], mm[:, 3], mm[:, 5], w1_b, w2_b, i, row2(final_g), fin, tm)

        zeros = (jnp.zeros((b, 2, S5_G, 1, 2 * S5_N), F32),
                 jnp.zeros((b, 2, HEADS, HEAD_V, HEAD_V), F32),
                 jnp.zeros((b, 2, HEADS, HEAD_V, RET_DK), F32),
                 jnp.zeros((b, 2, HEADS, HEAD_V, GLA_DK), F32))
        outs_c, st_c = mixers(xc, mc, False, zeros)
        outs_l, _ = mixers(x, ml, True, st_c)
        x = block(x, ml, outs_l, True, last)
        if not last:
            xc = block(xc, mc, outs_c, False, False)
    return x
```

```python
import functools
import math

import jax
import jax.numpy as jnp
from jax import lax
from jax.experimental import pallas as pl
from jax.experimental.pallas import tpu as pltpu

F32 = jnp.float32
BF16 = jnp.bfloat16

LANES = 128
GRID_W = 64
GROUP_W = 256
N_ADA = 6
EPS = 1e-6
GN_EPS = 64e-5
CHUNK = 64
RW_CPB = 4
LA_CPB = 4
GLA_T = 128
GLA_BLOCK = 512
HEADS = 4
HEAD_V = 64
S5_P = 16
S5_G = 16
S5_N = 64
S5_T = 32
S5_TP = S5_T * S5_P
RW_COLS = 896
RW_W_RANK = 32
RW_A_RANK = 32
RW_G_RANK = 64
RET_DK = 64
GLA_DK = 32
GLA_QK = 128
GLA_RANK = 16
GLA_TAU = 16.0
GLA_COLS_PAD = 896
ROPE_BASE = 10000.0
VMEM_LIMIT = 56 * 1024 * 1024


def _cparams(*sem):
    return pltpu.CompilerParams(dimension_semantics=sem, vmem_limit_bytes=VMEM_LIMIT)


def _mm(a, b):
    return jnp.dot(a.astype(BF16), b.astype(BF16), preferred_element_type=F32)


def _mm_nt(a, b):
    return lax.dot_general(a.astype(BF16), b.astype(BF16), (((1,), (1,)), ((), ())),
                           preferred_element_type=F32)


def _mm_tn(a, b):
    return lax.dot_general(a.astype(BF16), b.astype(BF16), (((0,), (0,)), ((), ())),
                           preferred_element_type=F32)


def _split2(x):
    hi = x.astype(BF16)
    return hi, (x - hi.astype(F32)).astype(BF16)


def _join2(p):
    return p[0].astype(F32) + p[1].astype(F32)


def _mm_x3p(a, b):
    return (jnp.dot(a[0], b[0], preferred_element_type=F32) + jnp.dot(a[0], b[1], preferred_element_type=F32)
            + jnp.dot(a[1], b[0], preferred_element_type=F32))


def _chunk_tri01(n, t, reverse):
    ri = lax.broadcasted_iota(jnp.int32, (n, n), 0)
    ci = lax.broadcasted_iota(jnp.int32, (n, n), 1)
    shift = t.bit_length() - 1
    tri = (ci >= ri) if reverse else (ci <= ri)
    return jnp.where((ri >> shift) == (ci >> shift), jnp.where(tri, 1.0, 0.0), 0.0).astype(BF16)


def _split3(x):
    hi = x.astype(BF16)
    r1 = x - hi.astype(F32)
    mid = r1.astype(BF16)
    lo = (r1 - mid.astype(F32)).astype(BF16)
    return hi, mid, lo


def _mm_left01(m01, x):
    hi, mid, lo = _split3(x)
    return (jnp.dot(m01, hi, preferred_element_type=F32)
            + jnp.dot(m01, mid, preferred_element_type=F32)
            + jnp.dot(m01, lo, preferred_element_type=F32))


def _mm_right01(x, m01):
    hi, lo = _split2(x)
    return jnp.dot(hi, m01, preferred_element_type=F32) + jnp.dot(lo, m01, preferred_element_type=F32)


def _sigmoid(x):
    return 1.0 / (1.0 + jnp.exp(-x))


def _softplus(x):
    return jnp.maximum(x, 0.0) + jnp.log(1.0 + jnp.exp(-jnp.abs(x)))


def _silu(x):
    return x * _sigmoid(x)


def _gelu_tanh(x):
    return 0.5 * x * (1.0 + jnp.tanh(math.sqrt(2.0 / math.pi) * (x + 0.044715 * x * x * x)))


def _tri_masks(t, reverse):
    ri = lax.broadcasted_iota(jnp.int32, (t, t), 0)
    ci = lax.broadcasted_iota(jnp.int32, (t, t), 1)
    if reverse:
        return ci >= ri, ci > ri, ci == ri
    return ci <= ri, ci < ri, ci == ri


def _row_to_col(row, eye):
    n = row.shape[1]
    return jnp.sum(jnp.where(eye, jnp.broadcast_to(row, (n, n)), 0.0), axis=1, keepdims=True)


def _cast_kernel(x_ref, o_ref):
    o_ref[...] = x_ref[...].astype(o_ref.dtype)


def _to_bf16(w):
    depth, r, c = w.shape
    tr = 256
    spec = pl.BlockSpec((1, tr, c), lambda i, j: (i, j, 0))
    return pl.pallas_call(
        _cast_kernel,
        grid=(depth, r // tr),
        in_specs=[spec],
        out_specs=spec,
        out_shape=jax.ShapeDtypeStruct(w.shape, BF16),
        compiler_params=_cparams("parallel", "parallel"),
        name="cast_bf16",
    )(w)


def _mod_kernel(cond_ref, w_ref, b_ref, o_ref):
    c = cond_ref[...]
    o_ref[0] = _mm(_silu(c), w_ref[0]) + b_ref[0]


def _modulation(cond, ada_w, ada_b):
    depth, d, n = ada_w.shape
    tn = 1536
    return pl.pallas_call(
        _mod_kernel,
        grid=(depth, n // tn),
        in_specs=[pl.BlockSpec((8, d), lambda i, j: (0, 0)),
                  pl.BlockSpec((1, d, tn), lambda i, j: (i, 0, j)),
                  pl.BlockSpec((1, 1, tn), lambda i, j: (i, 0, j))],
        out_specs=pl.BlockSpec((1, 8, tn), lambda i, j: (i, 0, j)),
        out_shape=jax.ShapeDtypeStruct((depth, 8, n), F32),
        compiler_params=_cparams("parallel", "parallel"),
        name="adaln_mod",
    )(cond, ada_w, ada_b.reshape(depth, 1, n))


def _modnorm(x, g, sc, sh):
    ms = jnp.mean(x * x, axis=-1, keepdims=True)
    return x * lax.rsqrt(ms + EPS) * g * (1.0 + sc) + sh


def _shift_columns(z, segments):
    tm, width = z.shape
    lane = lax.broadcasted_iota(jnp.int32, (tm, LANES), 1)
    starts = [s for s, _ in segments] + [width]
    cols = []
    for c in range(width // LANES):
        lo, hi = c * LANES, (c + 1) * LANES
        inside = [(max(starts[i], lo), segments[i][1]) for i in range(len(segments))
                  if starts[i] < hi and starts[i + 1] > lo]
        col = inside[-1][1](c)
        for first, fn in reversed(inside[:-1]):
            nxt = [f for f, _ in inside if f > first][0]
            col = jnp.where(lane < nxt - lo, fn(c), col)
        cols.append(col)
    return jnp.concatenate(cols, axis=1)


def _rw_prep_math(grid_shift, z, zp, zn, mu, w0_ref, wup_ref, a0_ref, aup_ref, gup, kkp, ka, rk, bones):
    tm = z.shape[0]
    row = lax.broadcasted_iota(jnp.int32, (tm, LANES), 0)
    col = lambda x, c: x[:, c * LANES:(c + 1) * LANES]
    if grid_shift:
        pos = row & (GRID_W - 1)
        left = lambda c: jnp.where(pos == 0, 0.0, pltpu.roll(col(z, c), 1, 0))
        right = lambda c: jnp.where(pos == GRID_W - 1, 0.0, pltpu.roll(col(z, c), tm - 1, 0))
        up = lambda c: jnp.concatenate([col(zp, c), col(z, c)[:tm - GRID_W]], axis=0)
        down = lambda c: jnp.concatenate([col(z, c)[GRID_W:], col(zn, c)], axis=0)
        q = RW_COLS // 4
        shifted = _shift_columns(z, [(0, left), (q, right), (2 * q, up), (3 * q, down)])
    else:
        prev = lambda c: jnp.where(row == 0, 0.0, pltpu.roll(col(z, c), 1, 0))
        nxt = lambda c: jnp.where(row == tm - 1, 0.0, pltpu.roll(col(z, c), tm - 1, 0))
        shifted = _shift_columns(z, [(0, prev), (RW_COLS // 2, nxt)])
    zm = z + mu * (shifted - z)
    r = zm[:, 0:256]
    k = zm[:, 256:512]
    v = zm[:, 512:768]
    lo = zm[:, 768:896]
    g = _mm(_sigmoid(lo), gup)
    kk = k * kkp
    kk = kk * lax.rsqrt(_mm_right01(kk * kk, bones) + 1e-12)
    bonus = _mm_right01(r * k * rk, bones) * v
    outs = [r.astype(BF16), v.astype(BF16), kk.astype(BF16), g.astype(BF16), bonus.astype(BF16)]
    tlo = jnp.tanh(lo)
    for d in range(2):
        w_raw = -_softplus(-(w0_ref[d] + _mm(tlo, wup_ref[d]))) - 0.5
        a = _sigmoid(a0_ref[d] + _mm(lo, aup_ref[d]))
        outs += [-jnp.exp(w_raw),
                 (k * (1.0 + (a - 1.0) * ka)).astype(BF16), a.astype(BF16)]
    return outs


def _inproj_kernel(grid_shift, nt, x_ref, xp_ref, xn_ref, g_ref, sc_ref, sh_ref,
                   w5_ref, wr_ref, wt_ref, wg_ref, cr_ref, sr_ref, cc_ref, sn_ref,
                   mu_ref, w0_ref, wup_ref, a0_ref, aup_ref, gup_ref, kk_ref, ka_ref, rk_ref, bones_ref,
                   o5_ref, o5g_ref, ot_ref, og_ref, *rest):
    rw_refs, z5h_ref = rest[:-1], rest[-1]
    j = pl.program_id(1)
    norm = lambda x: _modnorm(x, g_ref[...], sc_ref[0], sh_ref[0]).astype(BF16)
    hb = norm(x_ref[0])
    z5 = jnp.dot(hb, w5_ref[...], preferred_element_type=F32)
    o5_ref[0] = z5
    og_ref[0] = jnp.dot(hb, wg_ref[...], preferred_element_type=F32)
    zr = jnp.dot(hb, wr_ref[...], preferred_element_type=F32)
    zp = zn = None
    if grid_shift:
        zp = jnp.where(j > 0, jnp.dot(norm(xp_ref[0]), wr_ref[...], preferred_element_type=F32), 0.0)
        zn = jnp.where(j < nt - 1, jnp.dot(norm(xn_ref[0]), wr_ref[...], preferred_element_type=F32), 0.0)
    rw = _rw_prep_math(grid_shift, zr, zp, zn, mu_ref[...], w0_ref, wup_ref, a0_ref, aup_ref, gup_ref[...],
                       kk_ref[...], ka_ref[...], rk_ref[...], bones_ref[...])
    for ref, val in zip(rw_refs, rw):
        ref[0] = val
    zt = jnp.dot(hb, wt_ref[...], preferred_element_type=F32)
    lane = lax.broadcasted_iota(jnp.int32, (GRID_W, GROUP_W), 1)
    by_row = (lane & 32) == 0
    nrow = cr_ref.shape[0]
    cos = jnp.concatenate([jnp.where(by_row, cr_ref[j], cc_ref[...]) for j in range(nrow)], axis=0)
    sin = jnp.concatenate([jnp.where(by_row, sr_ref[j], sn_ref[...]) for j in range(nrow)], axis=0)
    ot_ref[0, :, 0:GROUP_W] = _rope(zt[:, 0:GROUP_W], cos, sin)
    ot_ref[0, :, GROUP_W:2 * GROUP_W] = _rope(zt[:, GROUP_W:2 * GROUP_W] * RET_DK ** -0.5, cos, sin)
    ot_ref[0, :, 2 * GROUP_W:] = zt[:, 2 * GROUP_W:]
    nch = o5g_ref.shape[2]
    gph = LANES // S5_P
    for hf in range(GROUP_W // LANES):
        z5h_ref[hf] = z5[:, hf * LANES:(hf + 1) * LANES]
    for s in range(S5_T):
        for hf in range(GROUP_W // LANES):
            rows = z5h_ref[hf, pl.ds(s, nch, stride=S5_T), :]
            for g in range(gph):
                o5g_ref[0, hf * gph + g, :, s * S5_P:(s + 1) * S5_P] = rows[:, g * S5_P:(g + 1) * S5_P]


def _inproj(x, g, sc, sh, w5, wr, wt, wg, rope, rwp, grid_shift, tm):
    b, l, d = x.shape
    nt = l // tm
    hb = tm // GRID_W
    nhb = l // GRID_W
    tok = lambda n: pl.BlockSpec((1, tm, n), lambda i, j: (i, j, 0))
    vec = pl.BlockSpec((1, 1, d), lambda i, j: (i, 0, 0))
    full = lambda a: pl.BlockSpec(a.shape, lambda i, j: (0,) * a.ndim)
    grouped = pl.BlockSpec((1, S5_G, tm // S5_T, S5_TP), lambda i, j: (i, 0, j, 0))
    rowt = pl.BlockSpec((hb, 1, GROUP_W), lambda i, j: (j, 0, 0))
    above = pl.BlockSpec((1, GRID_W, d), lambda i, j: (i, jnp.maximum(j * hb - 1, 0), 0))
    below = pl.BlockSpec((1, GRID_W, d), lambda i, j: (i, jnp.minimum((j + 1) * hb, nhb - 1), 0))
    cr, sr, cc, sn = rope
    params = (rwp['mu'], rwp['w0'], rwp['wup'], rwp['a0'], rwp['aup'], rwp['gup'], rwp['kk'], rwp['ka'],
              rwp['rk'], rwp['bones'])
    rw_dtypes = (BF16,) * 5 + (F32, BF16, BF16) * 2
    outs = pl.pallas_call(
        functools.partial(_inproj_kernel, grid_shift, nt),
        grid=(b, nt),
        in_specs=[tok(d), above, below, pl.BlockSpec((1, d), lambda i, j: (0, 0)), vec, vec,
                  full(w5), full(wr), full(wt), full(wg), rowt, rowt, full(cc), full(sn)]
                 + [full(a) for a in params],
        out_specs=[tok(GROUP_W), grouped, tok(wt.shape[1]), tok(wg.shape[1])] + [tok(GROUP_W)] * 11,
        out_shape=[jax.ShapeDtypeStruct((b, l, GROUP_W), F32),
                   jax.ShapeDtypeStruct((b, S5_G, l // S5_T, S5_TP), F32),
                   jax.ShapeDtypeStruct((b, l, wt.shape[1]), F32),
                   jax.ShapeDtypeStruct((b, l, wg.shape[1]), F32)]
                  + [jax.ShapeDtypeStruct((b, l, GROUP_W), dt) for dt in rw_dtypes],
        scratch_shapes=[pltpu.VMEM((GROUP_W // LANES, tm, LANES), F32)],
        compiler_params=_cparams("parallel", "parallel"),
        name="norm_inproj",
    )(x, x, x, g, sc, sh, w5, wr, wt, wg, cr, sr, cc, sn, *params)
    return outs[0], outs[1], outs[2], outs[3], tuple(outs[4:])


def _s5_tables(lam_re, lam_im, log_dt, b_re, b_im, c_re, c_im, nlev):
    hp = lax.Precision.HIGHEST
    t = S5_T
    lam = lax.complex(jnp.minimum(lam_re.astype(F32), -1e-4), lam_im.astype(F32))
    ldt = lam * jnp.exp(log_dt.astype(F32))[..., None]
    a_bar = jnp.exp(ldt)
    bb = ((a_bar - 1.0) / lam)[..., None] * lax.complex(b_re.astype(F32), b_im.astype(F32))
    cm = lax.complex(c_re.astype(F32), c_im.astype(F32))
    tau = jnp.arange(t + 1, dtype=F32)
    apow = jnp.exp(ldt[:, :, None, :] * tau[None, None, :, None])
    taps = jnp.einsum('dgpn,dgtn,dgnq->dgtpq', cm, apow[:, :, :t], bb, precision=hp).real
    taprow = jnp.stack([taps[0], taps[1][:, ::-1]]).transpose(0, 1, 4, 2, 3).reshape(2, S5_G, S5_P, S5_TP)

    def pack(zc):
        return jnp.concatenate([zc.real, zc.imag], axis=-1)

    win_f = apow[0][:, t - 1 - jnp.arange(t), None, :] * bb[0].transpose(0, 2, 1)[:, None]
    win_b = apow[1][:, jnp.arange(t), None, :] * bb[1].transpose(0, 2, 1)[:, None]
    win = jnp.stack([pack(win_f), pack(win_b)]).reshape(2, S5_G, S5_TP, 2 * S5_N)
    ca_f = cm[0][:, None] * apow[0][:, 1 + jnp.arange(t), None, :]
    ca_b = cm[1][:, None] * apow[1][:, t - jnp.arange(t), None, :]

    def outpack(ca):
        w = jnp.concatenate([ca.real, -ca.imag], axis=-1)
        return w.reshape(S5_G, S5_TP, 2 * S5_N).transpose(0, 2, 1)

    wout = jnp.stack([outpack(ca_f), outpack(ca_b)])
    lev = (2.0 ** jnp.arange(nlev, dtype=F32)) * t
    pw = jnp.exp(ldt[:, :, None, :] * lev[None, None, :, None])
    p1 = jnp.concatenate([pw.real, pw.real], axis=-1)
    p2 = jnp.concatenate([-pw.imag, pw.imag], axis=-1)
    pw = jnp.stack([p1, p2], axis=3)
    return taprow, win.astype(BF16), wout.astype(BF16), pw


def _s5_kernel(nc, nlev, u_ref, tap_ref, win_ref, wout_ref, pw_ref, h0_ref, y_ref, hfin_ref, conv_ref):
    u = u_ref[0, 0].astype(BF16)
    row = lax.broadcasted_iota(jnp.int32, (nc, 2 * S5_N), 0)
    lane = lax.broadcasted_iota(jnp.int32, (S5_P, S5_TP), 1)
    for s in range(S5_T):
        lo = s * S5_P
        fwd = tap_ref[0, 0] if s == 0 else jnp.where(lane >= lo, pltpu.roll(tap_ref[0, 0], lo, 1), 0.0)
        hi = lo + S5_P
        bwd = tap_ref[1, 0] if hi == S5_TP else jnp.where(lane < hi, pltpu.roll(tap_ref[1, 0], hi, 1), 0.0)
        conv_ref[0, lo:hi, :] = fwd.astype(BF16)
        conv_ref[1, lo:hi, :] = bwd.astype(BF16)

    def cmul(x, d, j):
        return pw_ref[d, 0, j, 0:1] * x + pw_ref[d, 0, j, 1:2] * pltpu.roll(x, S5_N, 1)

    y = None
    for d in range(2):
        v = jnp.dot(u, win_ref[d, 0], preferred_element_type=F32)
        h0 = h0_ref[0, d, 0]
        if d == 0:
            x = jnp.where(row == 0, h0, pltpu.roll(v, 1, 0))
        else:
            x = jnp.where(row == nc - 1, h0, pltpu.roll(v, nc - 1, 0))
        for j in range(nlev):
            sh = 2 ** j
            if d == 0:
                xs = jnp.where(row >= sh, pltpu.roll(x, sh, 0), 0.0)
            else:
                xs = jnp.where(row < nc - sh, pltpu.roll(x, nc - sh, 0), 0.0)
            x = x + cmul(xs, d, j)
        last = nc - 1 if d == 0 else 0
        hfin_ref[0, d, 0] = cmul(x[last:last + 1], d, 0) + v[last:last + 1]
        yd = (jnp.dot(u, conv_ref[d], preferred_element_type=F32)
              + _mm(x, wout_ref[d, 0]))
        y = yd if y is None else y + yd
    y_ref[0, 0] = y


def _s5_scan(uf, tables, h0):
    conv, win, wout, pw = tables
    b, _, nc, _ = uf.shape
    nlev = max(1, (nc - 1).bit_length())
    pw = pw[:, :, :nlev]
    n2 = 2 * S5_N
    y, hfin = pl.pallas_call(
        functools.partial(_s5_kernel, nc, nlev),
        grid=(b, S5_G),
        in_specs=[pl.BlockSpec((1, 1, nc, S5_TP), lambda i, g: (i, g, 0, 0)),
                  pl.BlockSpec((2, 1, S5_P, S5_TP), lambda i, g: (0, g, 0, 0)),
                  pl.BlockSpec((2, 1, S5_TP, n2), lambda i, g: (0, g, 0, 0)),
                  pl.BlockSpec((2, 1, n2, S5_TP), lambda i, g: (0, g, 0, 0)),
                  pl.BlockSpec((2, 1, nlev, 2, n2), lambda i, g: (0, g, 0, 0, 0)),
                  pl.BlockSpec((1, 2, 1, 1, n2), lambda i, g: (i, 0, g, 0, 0))],
        out_specs=[pl.BlockSpec((1, 1, nc, S5_TP), lambda i, g: (i, g, 0, 0)),
                   pl.BlockSpec((1, 2, 1, 1, n2), lambda i, g: (i, 0, g, 0, 0))],
        out_shape=[jax.ShapeDtypeStruct((b, S5_G, nc, S5_TP), F32),
                   jax.ShapeDtypeStruct((b, 2, S5_G, 1, n2), F32)],
        scratch_shapes=[pltpu.VMEM((2, S5_TP, S5_TP), BF16)],
        compiler_params=_cparams("parallel", "parallel"),
        name="s5_scan",
    )(uf, conv, win, wout, pw, h0)
    return y, hfin


def _rw_scan_kernel(nb, cpb, rf_ref, vf_ref, kkf_ref, lwf_ref, kdf_ref, asf_ref,
                    rb_ref, vb_ref, kkb_ref, lwb_ref, kdb_ref, asb_ref, s0_ref,
                    yf_ref, yb_ref, sfin_ref, st_ref):
    c = pl.program_id(1)

    @pl.when(c == 0)
    def _():
        st_ref[...] = s0_ref[0]

    t = CHUNK
    ri = lax.broadcasted_iota(jnp.int32, (t, t), 0)
    ci = lax.broadcasted_iota(jnp.int32, (t, t), 1)
    eye = ri == ci
    eyef = jnp.where(eye, 1.0, 0.0)
    same = lambda s: jnp.where((ri >> s) == (ci >> s), 1.0, 0.0)
    m4, m8, m16, m32 = same(2), same(3), same(4), same(5)
    merge_masks = (m8 - m4, m16 - m8, m32 - m16, 1.0 - m32)
    dir_refs = ((rf_ref, vf_ref, kkf_ref, lwf_ref, kdf_ref, asf_ref),
                (rb_ref, vb_ref, kkb_ref, lwb_ref, kdb_ref, asb_ref))
    units = []
    for d, (r_ref, v_ref, kk_ref, lw_ref, kd_ref, as_ref) in enumerate(dir_refs):
        incl = (ci >= ri) if d == 1 else (ci <= ri)
        strict = (ci > ri) if d == 1 else (ci < ri)
        r, v, kk, kd = (x[0].astype(F32) for x in (r_ref, v_ref, kk_ref, kd_ref))
        lw = lw_ref[0]
        cin = _mm_left01(_chunk_tri01(cpb * t, t, d == 1), lw)
        e_in = jnp.exp(cin)
        e_neg = jnp.exp(-cin)
        rt = r * e_in
        at = -kk * jnp.exp(cin - lw)
        bvec = kk * as_ref[0].astype(F32)
        bt = bvec * e_neg
        kt = kd * e_neg
        for j in range(cpb):
            rows = slice(j * t, (j + 1) * t)
            last = j * t + (0 if d == 1 else t - 1)
            clast = cin[last:last + 1]
            dl = jnp.exp(clast - cin[rows])
            bh = bvec[rows] * dl
            kh = kd[rows] * dl
            dec = jnp.exp(clast)
            for h in range(HEADS):
                sl = slice(h * HEAD_V, (h + 1) * HEAD_V)
                units.append(dict(d=d, j=j, h=h, incl=incl, strict=strict,
                                  at=at[rows, sl], rt=rt[rows, sl], bt=bt[rows, sl], kt=kt[rows, sl],
                                  bh=bh[:, sl], kh=kh[:, sl], v=v[rows, sl], dec=dec[:, sl]))

    a_all = [_mm_nt(jnp.concatenate([u['at'], u['rt']], axis=0),
                    jnp.concatenate([u['bt'], u['kt']], axis=0)) for u in units]
    nmat = [jnp.where(u['strict'], a[0:t, 0:t], 0.0) for u, a in zip(units, a_all)]
    a_kk = [jnp.concatenate([jnp.where(u['strict'], a[0:t, t:2 * t], 0.0),
                             jnp.where(u['incl'], a[t:2 * t, t:2 * t], 0.0)], axis=0)
            for u, a in zip(units, a_all)]
    a_rb = [jnp.where(u['incl'], a[t:2 * t, 0:t], 0.0) for u, a in zip(units, a_all)]
    akv = [_mm(a, u['v']) for u, a in zip(units, a_kk)]
    kv = [_mm_tn(u['v'], u['kh']) for u in units]
    nd = [x * m4 for x in nmat]
    n2 = [_mm(x, x) for x in nd]
    tinv = [eyef + x + _mm(eyef + x, y) for x, y in zip(nd, n2)]
    for mk in merge_masks:
        w = [_mm(ti, x * mk) for ti, x in zip(tinv, nmat)]
        tinv = [ti + _mm(wi, ti) for ti, wi in zip(tinv, w)]
    zz = [_mm(ti, jnp.concatenate([u['at'], kvv[0:t]], axis=1)) for ti, u, kvv in zip(tinv, units, akv)]
    ght = [_mm_tn(z, u['bh']) for u, z in zip(units, zz)]
    qy = [_mm(a, z) for a, z in zip(a_rb, zz)]
    qmat = [u['rt'] + x[:, :HEAD_V] for u, x in zip(units, qy)]
    y0 = [x[:, HEAD_V:] + kvv[t:2 * t] for x, kvv in zip(qy, akv)]
    gmat = [x[:HEAD_V] for x in ght]
    hmat = [x[HEAD_V:] + k2 for x, k2 in zip(ght, kv)]
    idx = {(u['d'], u['j'], u['h']): n for n, u in enumerate(units)}
    state = {(d, h): st_ref[d, h] for d in range(2) for h in range(HEADS)}
    ys = {}
    for step in range(cpb):
        for d in range(2):
            j = step if d == 0 else cpb - 1 - step
            for h in range(HEADS):
                n = idx[(d, j, h)]
                st = state[(d, h)]
                ys[(d, j, h)] = _mm_nt(qmat[n], st) + y0[n]
                state[(d, h)] = units[n]['dec'] * st + _mm(st, gmat[n]) + hmat[n]
    for d, y_ref in enumerate((yf_ref, yb_ref)):
        y_ref[0] = jnp.concatenate(
            [jnp.concatenate([ys[(d, j, h)] for h in range(HEADS)], axis=1) for j in range(cpb)], axis=0)
        for h in range(HEADS):
            st_ref[d, h] = state[(d, h)]

    @pl.when(c == nb - 1)
    def _():
        sfin_ref[0] = st_ref[...]


def _rw_scan(pre, s0):
    r, v, kk, _, _, lw0, kd0, as0, lw1, kd1, as1 = pre
    b, l, _ = r.shape
    cpb = min(RW_CPB, l // CHUNK)
    nb = l // (cpb * CHUNK)
    fw = pl.BlockSpec((1, cpb * CHUNK, GROUP_W), lambda i, c: (i, c, 0))
    bw = pl.BlockSpec((1, cpb * CHUNK, GROUP_W), lambda i, c: (i, nb - 1 - c, 0))
    st = pl.BlockSpec((1, 2, HEADS, HEAD_V, HEAD_V), lambda i, c: (i, 0, 0, 0, 0))
    return pl.pallas_call(
        functools.partial(_rw_scan_kernel, nb, cpb),
        grid=(b, nb),
        in_specs=[fw] * 6 + [bw] * 6 + [st],
        out_specs=[fw, bw, st],
        out_shape=[jax.ShapeDtypeStruct((b, l, GROUP_W), F32)] * 2
                  + [jax.ShapeDtypeStruct((b, 2, HEADS, HEAD_V, HEAD_V), F32)],
        scratch_shapes=[pltpu.VMEM((2, HEADS, HEAD_V, HEAD_V), F32)],
        compiler_params=_cparams("parallel", "arbitrary"),
        name="rwkv_scan",
    )(r, v, kk, lw0, kd0, as0, r, v, kk, lw1, kd1, as1, s0)


def _rope(x, cos, sin_signed):
    lane = lax.broadcasted_iota(jnp.int32, x.shape, 1)
    n = x.shape[1]
    swapped = jnp.where((lane & 16) == 0, pltpu.roll(x, n - 16, 1), pltpu.roll(x, 16, 1))
    return x * cos + swapped * sin_signed


def _ret_scan_kernel(nb, qf_ref, kf_ref, vf_ref, qb_ref, kb_ref, vb_ref,
                     dmat_ref, qdec_ref, kdec_ref, sdec_ref, s0_ref,
                     of_ref, ob_ref, sfin_ref, st_ref):
    c = pl.program_id(1)

    @pl.when(c == 0)
    def _():
        st_ref[...] = s0_ref[0]

    qs, ks, qds, khs, vs = [], [], [], [], []
    for d, (q_ref, k_ref, v_ref) in enumerate(((qf_ref, kf_ref, vf_ref), (qb_ref, kb_ref, vb_ref))):
        q, k, v = q_ref[0], k_ref[0], v_ref[0]
        qd = q * qdec_ref[d]
        kh = k * kdec_ref[d]
        for h in range(HEADS):
            sl = slice(h * HEAD_V, (h + 1) * HEAD_V)
            qs.append(q[:, sl])
            ks.append(k[:, sl])
            qds.append(qd[:, sl])
            khs.append(kh[:, sl])
            vs.append(v[:, sl])
    scores = [_mm_nt(q, k).astype(BF16) * dmat_ref[i // HEADS, i % HEADS]
              for i, (q, k) in enumerate(zip(qs, ks))]
    states = [st_ref[i // HEADS, i % HEADS] for i in range(2 * HEADS)]
    outs = [_mm(p, v) + _mm_nt(qd, st) for p, v, qd, st in zip(scores, vs, qds, states)]
    for i, (st, kh, v) in enumerate(zip(states, khs, vs)):
        d, h = i // HEADS, i % HEADS
        st_ref[d, h] = st * sdec_ref[d][:, h * HEAD_V:(h + 1) * HEAD_V] + _mm_tn(v, kh)
    of_ref[0] = jnp.concatenate(outs[:HEADS], axis=1)
    ob_ref[0] = jnp.concatenate(outs[HEADS:], axis=1)

    @pl.when(c == nb - 1)
    def _():
        sfin_ref[0] = st_ref[...]


def _ret_tables(decay_logit, n):
    lg = jax.nn.log_sigmoid(decay_logit.astype(F32))
    pos = jnp.arange(n, dtype=F32)
    lag = pos[:, None] - pos[None, :]
    lag = jnp.stack([lag, -lag])
    dmat = jnp.where(lag[:, None] >= 0, jnp.exp(lg[:, :, None, None] * lag[:, None]), 0.0)
    lanes = jnp.repeat(lg, RET_DK, axis=-1)[:, None, :]
    qpow = jnp.stack([pos + 1.0, n - pos])[:, :, None]
    kpow = jnp.stack([n - 1.0 - pos, pos])[:, :, None]
    return dmat.astype(BF16), jnp.exp(lanes * qpow), jnp.exp(lanes * kpow), jnp.exp(lanes * n)


def _gla_scan_kernel(nb, cpb, qf_ref, kf_ref, vf_ref, af_ref, qb_ref, kb_ref, vb_ref, ab_ref,
                     aup_ref, abias_ref, s0_ref, of_ref, ob_ref, sfin_ref, st_ref):
    c = pl.program_id(1)

    @pl.when(c == 0)
    def _():
        st_ref[...] = s0_ref[0]

    t = GLA_T
    n = cpb * t
    dk = GLA_DK
    scale = dk ** -0.5
    dirs = ((qf_ref, kf_ref, vf_ref, af_ref), (qb_ref, kb_ref, vb_ref, ab_ref))
    units = []
    for d, (q_ref, k_ref, v_ref, a_ref) in enumerate(dirs):
        incl = _tri_masks(t, d == 1)[0]
        q, k, v = q_ref[0], k_ref[0] * scale, v_ref[0]
        lw = -_softplus(-(_mm(a_ref[0], aup_ref[d]) + abias_ref[d])) * (1.0 / GLA_TAU)
        cin = _mm_left01(_chunk_tri01(n, t, d == 1), lw)
        qe = q * jnp.exp(cin)
        for j in range(cpb):
            rows = slice(j * t, (j + 1) * t)
            last = j * t + (0 if d == 1 else t - 1)
            mid = j * t + (t // 2 if d == 1 else t // 2 - 1)
            cj = cin[rows]
            clast = cin[last:last + 1]
            cmid = cin[mid:mid + 1]
            qt = q[rows] * jnp.exp(cj - cmid)
            kt = k[rows] * jnp.exp(cmid - cj)
            kh = k[rows] * jnp.exp(clast - cj)
            dec = jnp.exp(clast)
            for h in range(HEADS):
                sk = slice(h * dk, (h + 1) * dk)
                sv = slice(h * HEAD_V, (h + 1) * HEAD_V)
                units.append(dict(d=d, j=j, h=h, incl=incl, qt=qt[:, sk], kt=kt[:, sk], qe=qe[rows, sk],
                                  kh=kh[:, sk], v=v[rows, sv], dec=dec[:, sk]))

    amat = [jnp.where(u['incl'], _mm_nt(u['qt'], u['kt']), 0.0) for u in units]
    intra = [_mm(a, u['v']) for a, u in zip(amat, units)]
    kv = [_mm_tn(u['v'], u['kh']) for u in units]
    idx = {(u['d'], u['j'], u['h']): i for i, u in enumerate(units)}
    entering = {}
    for d in range(2):
        for h in range(HEADS):
            st = st_ref[d, h]
            for step in range(cpb):
                j = step if d == 0 else cpb - 1 - step
                i = idx[(d, j, h)]
                entering[i] = st
                st = units[i]['dec'] * st + kv[i]
            st_ref[d, h] = st
    outs = [x + _mm_nt(u['qe'], entering[i]) for i, (x, u) in enumerate(zip(intra, units))]
    for d, o_ref in enumerate((of_ref, ob_ref)):
        o_ref[0] = jnp.concatenate(
            [jnp.concatenate([outs[idx[(d, j, h)]] for h in range(HEADS)], axis=1) for j in range(cpb)], axis=0)

    @pl.when(c == nb - 1)
    def _():
        sfin_ref[0] = st_ref[...]


def _rope_tables(rows):
    nf = RET_DK // 4
    inv = ROPE_BASE ** (-jnp.arange(nf, dtype=F32) / nf)
    lane = jnp.arange(GROUP_W)
    freq = inv[lane % nf]
    sign = jnp.where((lane & 16) == 0, -1.0, 1.0)
    ar = jnp.arange(rows, dtype=F32)[:, None] * freq[None, :]
    ac = jnp.arange(GRID_W, dtype=F32)[:, None] * freq[None, :]
    return (jnp.cos(ar).reshape(rows, 1, GROUP_W), (jnp.sin(ar) * sign).reshape(rows, 1, GROUP_W),
            jnp.cos(ac), jnp.sin(ac) * sign)


def _ret_scan(z, tables, s0):
    b, l, _ = z.shape
    n = LA_CPB * CHUNK
    nb = l // n
    fw = lambda j: pl.BlockSpec((1, n, GROUP_W), lambda i, c: (i, c, j))
    bw = lambda j: pl.BlockSpec((1, n, GROUP_W), lambda i, c: (i, nb - 1 - c, j))
    full = lambda a: pl.BlockSpec(a.shape, lambda i, c: (0,) * a.ndim)
    st = pl.BlockSpec((1, 2, HEADS, HEAD_V, RET_DK), lambda i, c: (i, 0, 0, 0, 0))
    return pl.pallas_call(
        functools.partial(_ret_scan_kernel, nb),
        grid=(b, nb),
        in_specs=[fw(0), fw(1), fw(2), bw(0), bw(1), bw(2)] + [full(a) for a in tables] + [st],
        out_specs=[fw(0), bw(0), st],
        out_shape=[jax.ShapeDtypeStruct((b, l, GROUP_W), F32)] * 2
                  + [jax.ShapeDtypeStruct((b, 2, HEADS, HEAD_V, RET_DK), F32)],
        scratch_shapes=[pltpu.VMEM((2, HEADS, HEAD_V, RET_DK), F32)],
        compiler_params=_cparams("parallel", "arbitrary"),
        name="retention_scan",
    )(z, z, z, z, z, z, *tables, s0)


def _gla_scan(z, aup, abias, s0):
    b, l, _ = z.shape
    cpb = min(GLA_BLOCK, l) // CHUNK
    nb = l // (cpb * CHUNK)
    blk = lambda w, j, rev: pl.BlockSpec(
        (1, cpb * CHUNK, w), (lambda i, c: (i, nb - 1 - c, j)) if rev else (lambda i, c: (i, c, j)))
    st = pl.BlockSpec((1, 2, HEADS, HEAD_V, GLA_DK), lambda i, c: (i, 0, 0, 0, 0))
    ofw = pl.BlockSpec((1, cpb * CHUNK, GROUP_W), lambda i, c: (i, c, 0))
    obw = pl.BlockSpec((1, cpb * CHUNK, GROUP_W), lambda i, c: (i, nb - 1 - c, 0))
    return pl.pallas_call(
        functools.partial(_gla_scan_kernel, nb, cpb * CHUNK // GLA_T),
        grid=(b, nb),
        in_specs=[blk(GLA_QK, 0, False), blk(GLA_QK, 1, False), blk(GROUP_W, 1, False), blk(128, 6, False),
                  blk(GLA_QK, 0, True), blk(GLA_QK, 1, True), blk(GROUP_W, 1, True), blk(128, 6, True),
                  pl.BlockSpec(aup.shape, lambda i, c: (0, 0, 0)),
                  pl.BlockSpec(abias.shape, lambda i, c: (0, 0, 0)), st],
        out_specs=[ofw, obw, st],
        out_shape=[jax.ShapeDtypeStruct((b, l, GROUP_W), F32)] * 2
                  + [jax.ShapeDtypeStruct((b, 2, HEADS, HEAD_V, GLA_DK), F32)],
        scratch_shapes=[pltpu.VMEM((2, HEADS, HEAD_V, GLA_DK), F32)],
        compiler_params=_cparams("parallel", "arbitrary"),
        name="gla_scan",
    )(z, z, z, z, z, z, z, z, aup, abias, s0)


def _mix_kernel(x_ref, g1_ref, y5_ref, u5_ref, d5_ref, gw_ref, gb_ref,
                ryf_ref, ryb_ref, rbonus_ref, rg_ref, rlng_ref, rlnb_ref,
                tof_ref, tob_ref, tg_ref, tln_ref,
                gof_ref, gob_ref, gg_ref, gln_ref,
                bones_ref, wo_ref, o_ref, y5t_ref):
    bones = bones_ref[...]
    inv = 1.0 / HEAD_V

    def hmean(a):
        return _mm_right01(a, bones) * inv

    nch = y5_ref.shape[2]
    gph = LANES // S5_P
    y5g = [y5_ref[0, g] for g in range(S5_G)]
    for s in range(S5_T):
        for hf in range(GROUP_W // LANES):
            y5t_ref[hf, pl.ds(s, nch, stride=S5_T), :] = jnp.concatenate(
                [yg[:, s * S5_P:(s + 1) * S5_P] for yg in y5g[hf * gph:(hf + 1) * gph]], axis=1)
    y = jnp.concatenate([y5t_ref[0], y5t_ref[1]], axis=1) + d5_ref[...] * u5_ref[0]
    y = _gelu_tanh(y)
    ya = y * _sigmoid(_mm(y, gw_ref[...]) + gb_ref[...])
    yr = ryf_ref[0] + ryb_ref[0]
    dlt = yr - hmean(yr)
    yn = dlt * lax.rsqrt(hmean(dlt * dlt) + GN_EPS)
    yb = (yn * rlng_ref[...] + rlnb_ref[...] + rbonus_ref[0].astype(F32)) * rg_ref[0].astype(F32)
    ot = tof_ref[0] + tob_ref[0]
    yc = ot * lax.rsqrt(hmean(ot * ot) + EPS) * tln_ref[...] * _silu(tg_ref[0])
    og = gof_ref[0] + gob_ref[0]
    yd = og * lax.rsqrt(hmean(og * og) + EPS) * gln_ref[...] * _silu(gg_ref[0])
    mix = (_mm(ya, wo_ref[0]) + _mm(yb, wo_ref[1]) + _mm(yc, wo_ref[2]) + _mm(yd, wo_ref[3]))
    o_ref[0] = x_ref[0] + g1_ref[0] * mix


def _mix(x, g1, y5, u5, rw_pre, rw_y, zret, ret_o, zgla, gla_o, p, tm):
    b, l, d = x.shape
    tok = lambda w, j: pl.BlockSpec((1, tm, w), lambda i, t: (i, t, j))
    vec = lambda a: pl.BlockSpec(a.shape, lambda i, t: (0,) * a.ndim)
    t256 = tok(GROUP_W, 0)
    args = [x, g1, y5, u5, p['s5_d'], p['glu_w'], p['glu_b'],
            rw_y[0], rw_y[1], rw_pre[4], rw_pre[3], p['rw_ln_g'], p['rw_ln_b'],
            ret_o[0], ret_o[1], zret, p['ret_ln_g'],
            gla_o[0], gla_o[1], zgla, p['gla_ln_g'],
            p['bones'], p['w_out']]
    grouped = pl.BlockSpec((1, S5_G, tm // S5_T, S5_TP), lambda i, t: (i, 0, t, 0))
    specs = [tok(d, 0), pl.BlockSpec((1, 1, d), lambda i, t: (i, 0, 0)), grouped, t256,
             vec(p['s5_d']), vec(p['glu_w']), vec(p['glu_b']),
             t256, t256, t256, t256, vec(p['rw_ln_g']), vec(p['rw_ln_b']),
             t256, t256, tok(GROUP_W, 3), vec(p['ret_ln_g']),
             t256, t256, tok(GROUP_W, 2), vec(p['gla_ln_g']),
             vec(p['bones']), vec(p['w_out'])]
    return pl.pallas_call(
        _mix_kernel,
        grid=(b, l // tm),
        in_specs=specs,
        out_specs=tok(d, 0),
        out_shape=jax.ShapeDtypeStruct((b, l, d), F32),
        scratch_shapes=[pltpu.VMEM((GROUP_W // LANES, tm, LANES), F32)],
        compiler_params=_cparams("parallel", "parallel"),
        name="mix_outproj",
    )(*args)


def _mlp_kernel(final, nff, x_ref, g_ref, sc_ref, sh_ref, gate_ref, w1_ref, w2_ref, fg_ref, o_ref):
    x = x_ref[0]
    hb = _modnorm(x, g_ref[...], sc_ref[0], sh_ref[0]).astype(BF16)
    ff = w1_ref.shape[2] // nff
    acc = None
    for j in range(nff):
        a = jnp.maximum(jnp.dot(hb, w1_ref[0, :, j * ff:(j + 1) * ff], preferred_element_type=F32), 0.0)
        part = jnp.dot((a * a).astype(BF16), w2_ref[0, j * ff:(j + 1) * ff, :], preferred_element_type=F32)
        acc = part if acc is None else acc + part
    y = x + gate_ref[0] * acc
    if final:
        ms = jnp.mean(y * y, axis=-1, keepdims=True)
        y = y * lax.rsqrt(ms + EPS) * fg_ref[...]
    o_ref[0] = y


def _mlp(x, g, sc, sh, gate, w1, w2, layer, final_g, final, tm):
    b, l, d = x.shape
    tok = pl.BlockSpec((1, tm, d), lambda i, t: (i, t, 0))
    vec = pl.BlockSpec((1, 1, d), lambda i, t: (i, 0, 0))
    row = pl.BlockSpec((1, d), lambda i, t: (0, 0))
    once = lambda a: pl.BlockSpec((1,) + a.shape[1:], lambda i, t: (layer, 0, 0), pipeline_mode=pl.Buffered(1))
    return pl.pallas_call(
        functools.partial(_mlp_kernel, final, 4),
        grid=(b, l // tm),
        in_specs=[tok, row, vec, vec, vec, once(w1), once(w2), row],
        out_specs=tok,
        out_shape=jax.ShapeDtypeStruct((b, l, d), F32),
        compiler_params=_cparams("parallel", "parallel"),
        name="mlp",
    )(x, g, sc, sh, gate, w1, w2, final_g)


def kernel(x, c, ctx, c_ctx, ada_w, ada_b, norm1_g, norm2_g, w_in, w_out, s5_lam_re, s5_lam_im, s5_log_dt, s5_b_re, s5_b_im, s5_c_re, s5_c_im, s5_d, s5_glu_w, s5_glu_b, rw_mu, rw_w0, rw_w_up, rw_a0, rw_a_up, rw_g_up, rw_k_k, rw_k_a, rw_r_k, rw_ln_g, rw_ln_b, ret_decay_logit, ret_ln_g, gla_a_up, gla_a_b, gla_ln_g, mlp_w1, mlp_w2, final_g):
    b, l, d = x.shape
    lc = ctx.shape[1]
    depth = ada_w.shape[0]
    rows = l // GRID_W
    assert l % 512 == 0 and lc % CHUNK == 0 and lc % S5_T == 0 and d % 128 == 0

    cond = jnp.zeros((8, d), F32).at[:b].set(c).at[b].set(c_ctx)
    mod = _modulation(cond, ada_w, ada_b)

    lane = jnp.arange(GROUP_W)
    bones = (lane[:, None] // HEAD_V == lane[None, :] // HEAD_V).astype(BF16)
    rope_lat = _rope_tables(rows)
    ident = (jnp.ones((lc // CHUNK, 1, GROUP_W), F32), jnp.zeros((lc // CHUNK, 1, GROUP_W), F32),
             jnp.ones((GRID_W, GROUP_W), F32), jnp.zeros((GRID_W, GROUP_W), F32))
    nlev = max(1, (l // S5_T - 1).bit_length())
    c0, c1, c2 = GROUP_W, GROUP_W + RW_COLS, GROUP_W + RW_COLS + 4 * GROUP_W
    row2 = lambda a: a.reshape(1, -1).astype(F32)
    w_in_b, w_out_b, w1_b, w2_b = _to_bf16(w_in), _to_bf16(w_out), _to_bf16(mlp_w1), _to_bf16(mlp_w2)

    xc = ctx
    for i in range(depth):
        last = i == depth - 1
        m = mod[i].reshape(8, N_ADA, d)
        ml = m[:b, :, None, :]
        mc = jnp.broadcast_to(m[b][None, :, None, :], (b, N_ADA, 1, d))
        wi = w_in_b[i]
        w5, wr, wt = wi[:, :c0], wi[:, c0:c1], wi[:, c1:c2]
        wg = jnp.pad(wi[:, c2:], ((0, 0), (0, GLA_COLS_PAD - (wi.shape[1] - c2))))
        n1 = row2(norm1_g[i])
        n2 = row2(norm2_g[i])

        s5_tab = _s5_tables(s5_lam_re[i], s5_lam_im[i], s5_log_dt[i], s5_b_re[i], s5_b_im[i],
                            s5_c_re[i], s5_c_im[i], nlev)
        zpad = lambda a, lo, n: jnp.zeros((2, 128, n), F32).at[:, lo:lo + a.shape[1]].set(a).astype(BF16)
        rwp = dict(
            mu=row2(rw_mu[i]),
            w0=rw_w0[i].reshape(2, 1, GROUP_W), a0=rw_a0[i].reshape(2, 1, GROUP_W),
            wup=zpad(rw_w_up[i], 0, GROUP_W), aup=zpad(rw_a_up[i], RW_W_RANK, GROUP_W),
            gup=jnp.zeros((128, GROUP_W), F32).at[RW_W_RANK + RW_A_RANK:].set(rw_g_up[i]).astype(BF16),
            kk=row2(rw_k_k[i]), ka=row2(rw_k_a[i]), rk=row2(rw_r_k[i]), bones=bones)
        ret_tab = _ret_tables(ret_decay_logit[i], LA_CPB * CHUNK)
        gla_aup = zpad(gla_a_up[i], 0, GLA_QK)
        gla_ab = gla_a_b[i].reshape(2, 1, GLA_QK).astype(F32)
        mixp = dict(s5_d=row2(s5_d[i]), glu_w=s5_glu_w[i].astype(BF16), glu_b=row2(s5_glu_b[i]),
                    rw_ln_g=row2(rw_ln_g[i]), rw_ln_b=row2(rw_ln_b[i]), ret_ln_g=row2(ret_ln_g[i]),
                    gla_ln_g=row2(gla_ln_g[i]), bones=bones,
                    w_out=w_out_b[i].reshape(4, GROUP_W, d))

        def mixers(xx, mm, is_lat, states):
            tm = 512 if is_lat else lc
            z5, z5g, zt, zg, pre = _inproj(xx, n1, mm[:, 1], mm[:, 0], w5, wr, wt, wg,
                                           rope_lat if is_lat else ident, rwp, is_lat, tm)
            y5, h5 = _s5_scan(z5g, s5_tab, states[0])
            yrf, yrb, srw = _rw_scan(pre, states[1])
            otf, otb, sret = _ret_scan(zt, ret_tab, states[2])
            ogf, ogb, sgla = _gla_scan(zg, gla_aup, gla_ab, states[3])
            outs = (z5, y5, pre, (yrf, yrb), zt, (otf, otb), zg, (ogf, ogb))
            return outs, (h5, srw, sret, sgla)

        def block(xx, mm, outs, is_lat, fin):
            tm = 512 if is_lat else lc
            z5, y5, pre, yr, zt, ot, zg, og = outs
            x1 = _mix(xx, mm[:, 2], y5, z5, pre, yr, zt, ot, zg, og, mixp, tm)
            return _mlp(x1, n2, mm[:, 4], mm[:, 3], mm[:, 5], w1_b, w2_b, i, row2(final_g), fin, tm)

        zeros = (jnp.zeros((b, 2, S5_G, 1, 2 * S5_N), F32),
                 jnp.zeros((b, 2, HEADS, HEAD_V, HEAD_V), F32),
                 jnp.zeros((b, 2, HEADS, HEAD_V, RET_DK), F32),
                 jnp.zeros((b, 2, HEADS, HEAD_V, GLA_DK), F32))
        outs_c, st_c = mixers(xc, mc, False, zeros)
        outs_l, _ = mixers(x, ml, True, st_c)
        x = block(x, ml, outs_l, True, last)
        if not last:
            xc = block(xc, mc, outs_c, False, False)
    return x
```

```python
import functools
import math

import jax
import jax.numpy as jnp
from jax import lax
from jax.experimental import pallas as pl
from jax.experimental.pallas import tpu as pltpu

F32 = jnp.float32
BF16 = jnp.bfloat16

LANES = 128
GRID_W = 64
GROUP_W = 256
N_ADA = 6
EPS = 1e-6
GN_EPS = 64e-5
CHUNK = 64
RW_CPB = 4
RET_T = 256
RET_BLOCK = 512
GLA_T = 128
GLA_BLOCK = 512
HEADS = 4
HEAD_V = 64
S5_P = 16
S5_G = 16
S5_N = 64
S5_T = 32
S5_TP = S5_T * S5_P
RW_COLS = 896
RW_W_RANK = 32
RW_A_RANK = 32
RW_G_RANK = 64
RET_DK = 64
GLA_DK = 32
GLA_QK = 128
GLA_RANK = 16
GLA_TAU = 16.0
GLA_COLS_PAD = 896
ROPE_BASE = 10000.0
VMEM_LIMIT = 56 * 1024 * 1024


def _cparams(*sem):
    return pltpu.CompilerParams(dimension_semantics=sem, vmem_limit_bytes=VMEM_LIMIT)


def _mm(a, b):
    return jnp.dot(a.astype(BF16), b.astype(BF16), preferred_element_type=F32)


def _mm_nt(a, b):
    return lax.dot_general(a.astype(BF16), b.astype(BF16), (((1,), (1,)), ((), ())),
                           preferred_element_type=F32)


def _mm_tn(a, b):
    return lax.dot_general(a.astype(BF16), b.astype(BF16), (((0,), (0,)), ((), ())),
                           preferred_element_type=F32)


def _split2(x):
    hi = x.astype(BF16)
    return hi, (x - hi.astype(F32)).astype(BF16)


def _join2(p):
    return p[0].astype(F32) + p[1].astype(F32)


def _mm_x3p(a, b):
    return (jnp.dot(a[0], b[0], preferred_element_type=F32) + jnp.dot(a[0], b[1], preferred_element_type=F32)
            + jnp.dot(a[1], b[0], preferred_element_type=F32))


def _chunk_tri01(n, t, reverse):
    ri = lax.broadcasted_iota(jnp.int32, (n, n), 0)
    ci = lax.broadcasted_iota(jnp.int32, (n, n), 1)
    shift = t.bit_length() - 1
    tri = (ci >= ri) if reverse else (ci <= ri)
    return jnp.where((ri >> shift) == (ci >> shift), jnp.where(tri, 1.0, 0.0), 0.0).astype(BF16)


def _split3(x):
    hi = x.astype(BF16)
    r1 = x - hi.astype(F32)
    mid = r1.astype(BF16)
    lo = (r1 - mid.astype(F32)).astype(BF16)
    return hi, mid, lo


def _mm_left01(m01, x):
    hi, mid, lo = _split3(x)
    return (jnp.dot(m01, hi, preferred_element_type=F32)
            + jnp.dot(m01, mid, preferred_element_type=F32)
            + jnp.dot(m01, lo, preferred_element_type=F32))


def _mm_right01(x, m01):
    return jnp.dot(x.astype(BF16), m01, preferred_element_type=F32)


def _sigmoid(x):
    return 1.0 / (1.0 + jnp.exp(-x))


def _softplus(x):
    return jnp.maximum(x, 0.0) + jnp.log(1.0 + jnp.exp(-jnp.abs(x)))


def _silu(x):
    return x * _sigmoid(x)


def _gelu_tanh(x):
    return 0.5 * x * (1.0 + jnp.tanh(math.sqrt(2.0 / math.pi) * (x + 0.044715 * x * x * x)))


def _tri_masks(t, reverse):
    ri = lax.broadcasted_iota(jnp.int32, (t, t), 0)
    ci = lax.broadcasted_iota(jnp.int32, (t, t), 1)
    if reverse:
        return ci >= ri, ci > ri, ci == ri
    return ci <= ri, ci < ri, ci == ri


def _row_to_col(row, eye):
    n = row.shape[1]
    return jnp.sum(jnp.where(eye, jnp.broadcast_to(row, (n, n)), 0.0), axis=1, keepdims=True)


def _cast_kernel(x_ref, o_ref):
    o_ref[...] = x_ref[...].astype(o_ref.dtype)


def _to_bf16(w):
    depth, r, c = w.shape
    tr = 256
    spec = pl.BlockSpec((1, tr, c), lambda i, j: (i, j, 0))
    return pl.pallas_call(
        _cast_kernel,
        grid=(depth, r // tr),
        in_specs=[spec],
        out_specs=spec,
        out_shape=jax.ShapeDtypeStruct(w.shape, BF16),
        compiler_params=_cparams("parallel", "parallel"),
        name="cast_bf16",
    )(w)


def _mod_kernel(cond_ref, w_ref, b_ref, o_ref):
    c = cond_ref[...]
    o_ref[0] = _mm(_silu(c), w_ref[0]) + b_ref[0]


def _modulation(cond, ada_w, ada_b):
    depth, d, n = ada_w.shape
    tn = 1536
    return pl.pallas_call(
        _mod_kernel,
        grid=(depth, n // tn),
        in_specs=[pl.BlockSpec((8, d), lambda i, j: (0, 0)),
                  pl.BlockSpec((1, d, tn), lambda i, j: (i, 0, j)),
                  pl.BlockSpec((1, 1, tn), lambda i, j: (i, 0, j))],
        out_specs=pl.BlockSpec((1, 8, tn), lambda i, j: (i, 0, j)),
        out_shape=jax.ShapeDtypeStruct((depth, 8, n), F32),
        compiler_params=_cparams("parallel", "parallel"),
        name="adaln_mod",
    )(cond, ada_w, ada_b.reshape(depth, 1, n))


def _modnorm(x, g, sc, sh):
    ms = jnp.mean(x * x, axis=-1, keepdims=True)
    return x * lax.rsqrt(ms + EPS) * g * (1.0 + sc) + sh


def _shift_columns(z, segments):
    tm, width = z.shape
    lane = lax.broadcasted_iota(jnp.int32, (tm, LANES), 1)
    starts = [s for s, _ in segments] + [width]
    cols = []
    for c in range(width // LANES):
        lo, hi = c * LANES, (c + 1) * LANES
        inside = [(max(starts[i], lo), segments[i][1]) for i in range(len(segments))
                  if starts[i] < hi and starts[i + 1] > lo]
        col = inside[-1][1](c)
        for first, fn in reversed(inside[:-1]):
            nxt = [f for f, _ in inside if f > first][0]
            col = jnp.where(lane < nxt - lo, fn(c), col)
        cols.append(col)
    return jnp.concatenate(cols, axis=1)


def _rw_prep_math(grid_shift, z, zp, zn, mu, w0_ref, wup_ref, a0_ref, aup_ref, gup, kkp, ka, rk, bones):
    tm = z.shape[0]
    row = lax.broadcasted_iota(jnp.int32, (tm, LANES), 0)
    col = lambda x, c: x[:, c * LANES:(c + 1) * LANES]
    if grid_shift:
        pos = row & (GRID_W - 1)
        left = lambda c: jnp.where(pos == 0, 0.0, pltpu.roll(col(z, c), 1, 0))
        right = lambda c: jnp.where(pos == GRID_W - 1, 0.0, pltpu.roll(col(z, c), tm - 1, 0))
        up = lambda c: jnp.concatenate([col(zp, c), col(z, c)[:tm - GRID_W]], axis=0)
        down = lambda c: jnp.concatenate([col(z, c)[GRID_W:], col(zn, c)], axis=0)
        q = RW_COLS // 4
        shifted = _shift_columns(z, [(0, left), (q, right), (2 * q, up), (3 * q, down)])
    else:
        prev = lambda c: jnp.where(row == 0, 0.0, pltpu.roll(col(z, c), 1, 0))
        nxt = lambda c: jnp.where(row == tm - 1, 0.0, pltpu.roll(col(z, c), tm - 1, 0))
        shifted = _shift_columns(z, [(0, prev), (RW_COLS // 2, nxt)])
    zm = z + mu * (shifted - z)
    r = zm[:, 0:256]
    k = zm[:, 256:512]
    v = zm[:, 512:768]
    lo = zm[:, 768:896]
    g = _mm(_sigmoid(lo), gup)
    kk = k * kkp
    kk = kk * lax.rsqrt(_mm_right01(kk * kk, bones) + 1e-12)
    bonus = _mm_right01(r * k * rk, bones) * v
    outs = [r.astype(BF16), v.astype(BF16), kk.astype(BF16), g.astype(BF16), bonus.astype(BF16)]
    tlo = jnp.tanh(lo)
    for d in range(2):
        w_raw = -_softplus(-(w0_ref[d] + _mm(tlo, wup_ref[d]))) - 0.5
        a = _sigmoid(a0_ref[d] + _mm(lo, aup_ref[d]))
        outs += [-jnp.exp(w_raw),
                 (k * (1.0 + (a - 1.0) * ka)).astype(BF16), a.astype(BF16)]
    return outs


def _inproj_kernel(grid_shift, nt, x_ref, xp_ref, xn_ref, g_ref, sc_ref, sh_ref,
                   w5_ref, wr_ref, wt_ref, wg_ref, cr_ref, sr_ref, cc_ref, sn_ref,
                   mu_ref, w0_ref, wup_ref, a0_ref, aup_ref, gup_ref, kk_ref, ka_ref, rk_ref, bones_ref,
                   o5_ref, o5g_ref, ot_ref, og_ref, *rest):
    rw_refs, z5h_ref = rest[:-1], rest[-1]
    j = pl.program_id(1)
    norm = lambda x: _modnorm(x, g_ref[...], sc_ref[0], sh_ref[0]).astype(BF16)
    hb = norm(x_ref[0])
    z5 = jnp.dot(hb, w5_ref[...], preferred_element_type=F32)
    o5_ref[0] = z5
    og_ref[0] = jnp.dot(hb, wg_ref[...], preferred_element_type=F32)
    zr = jnp.dot(hb, wr_ref[...], preferred_element_type=F32)
    zp = zn = None
    if grid_shift:
        zp = jnp.where(j > 0, jnp.dot(norm(xp_ref[0]), wr_ref[...], preferred_element_type=F32), 0.0)
        zn = jnp.where(j < nt - 1, jnp.dot(norm(xn_ref[0]), wr_ref[...], preferred_element_type=F32), 0.0)
    rw = _rw_prep_math(grid_shift, zr, zp, zn, mu_ref[...], w0_ref, wup_ref, a0_ref, aup_ref, gup_ref[...],
                       kk_ref[...], ka_ref[...], rk_ref[...], bones_ref[...])
    for ref, val in zip(rw_refs, rw):
        ref[0] = val
    zt = jnp.dot(hb, wt_ref[...], preferred_element_type=F32)
    lane = lax.broadcasted_iota(jnp.int32, (GRID_W, GROUP_W), 1)
    by_row = (lane & 32) == 0
    nrow = cr_ref.shape[0]
    cos = jnp.concatenate([jnp.where(by_row, cr_ref[j], cc_ref[...]) for j in range(nrow)], axis=0)
    sin = jnp.concatenate([jnp.where(by_row, sr_ref[j], sn_ref[...]) for j in range(nrow)], axis=0)
    ot_ref[0, :, 0:GROUP_W] = _rope(zt[:, 0:GROUP_W], cos, sin)
    ot_ref[0, :, GROUP_W:2 * GROUP_W] = _rope(zt[:, GROUP_W:2 * GROUP_W] * RET_DK ** -0.5, cos, sin)
    ot_ref[0, :, 2 * GROUP_W:] = zt[:, 2 * GROUP_W:]
    nch = o5g_ref.shape[2]
    gph = LANES // S5_P
    for hf in range(GROUP_W // LANES):
        z5h_ref[hf] = z5[:, hf * LANES:(hf + 1) * LANES]
    for s in range(S5_T):
        for hf in range(GROUP_W // LANES):
            rows = z5h_ref[hf, pl.ds(s, nch, stride=S5_T), :]
            for g in range(gph):
                o5g_ref[0, hf * gph + g, :, s * S5_P:(s + 1) * S5_P] = rows[:, g * S5_P:(g + 1) * S5_P]


def _inproj(x, g, sc, sh, w5, wr, wt, wg, rope, rwp, grid_shift, tm):
    b, l, d = x.shape
    nt = l // tm
    hb = tm // GRID_W
    nhb = l // GRID_W
    tok = lambda n: pl.BlockSpec((1, tm, n), lambda i, j: (i, j, 0))
    vec = pl.BlockSpec((1, 1, d), lambda i, j: (i, 0, 0))
    full = lambda a: pl.BlockSpec(a.shape, lambda i, j: (0,) * a.ndim)
    grouped = pl.BlockSpec((1, S5_G, tm // S5_T, S5_TP), lambda i, j: (i, 0, j, 0))
    rowt = pl.BlockSpec((hb, 1, GROUP_W), lambda i, j: (j, 0, 0))
    above = pl.BlockSpec((1, GRID_W, d), lambda i, j: (i, jnp.maximum(j * hb - 1, 0), 0))
    below = pl.BlockSpec((1, GRID_W, d), lambda i, j: (i, jnp.minimum((j + 1) * hb, nhb - 1), 0))
    cr, sr, cc, sn = rope
    params = (rwp['mu'], rwp['w0'], rwp['wup'], rwp['a0'], rwp['aup'], rwp['gup'], rwp['kk'], rwp['ka'],
              rwp['rk'], rwp['bones'])
    rw_dtypes = (BF16,) * 5 + (F32, BF16, BF16) * 2
    outs = pl.pallas_call(
        functools.partial(_inproj_kernel, grid_shift, nt),
        grid=(b, nt),
        in_specs=[tok(d), above, below, pl.BlockSpec((1, d), lambda i, j: (0, 0)), vec, vec,
                  full(w5), full(wr), full(wt), full(wg), rowt, rowt, full(cc), full(sn)]
                 + [full(a) for a in params],
        out_specs=[tok(GROUP_W), grouped, tok(wt.shape[1]), tok(wg.shape[1])] + [tok(GROUP_W)] * 11,
        out_shape=[jax.ShapeDtypeStruct((b, l, GROUP_W), F32),
                   jax.ShapeDtypeStruct((b, S5_G, l // S5_T, S5_TP), F32),
                   jax.ShapeDtypeStruct((b, l, wt.shape[1]), F32),
                   jax.ShapeDtypeStruct((b, l, wg.shape[1]), F32)]
                  + [jax.ShapeDtypeStruct((b, l, GROUP_W), dt) for dt in rw_dtypes],
        scratch_shapes=[pltpu.VMEM((GROUP_W // LANES, tm, LANES), F32)],
        compiler_params=_cparams("parallel", "parallel"),
        name="norm_inproj",
    )(x, x, x, g, sc, sh, w5, wr, wt, wg, cr, sr, cc, sn, *params)
    return outs[0], outs[1], outs[2], outs[3], tuple(outs[4:])


def _s5_tables(lam_re, lam_im, log_dt, b_re, b_im, c_re, c_im, nlev):
    hp = lax.Precision.HIGHEST
    t = S5_T
    lam = lax.complex(jnp.minimum(lam_re.astype(F32), -1e-4), lam_im.astype(F32))
    ldt = lam * jnp.exp(log_dt.astype(F32))[..., None]
    a_bar = jnp.exp(ldt)
    bb = ((a_bar - 1.0) / lam)[..., None] * lax.complex(b_re.astype(F32), b_im.astype(F32))
    cm = lax.complex(c_re.astype(F32), c_im.astype(F32))
    tau = jnp.arange(t + 1, dtype=F32)
    apow = jnp.exp(ldt[:, :, None, :] * tau[None, None, :, None])
    taps = jnp.einsum('dgpn,dgtn,dgnq->dgtpq', cm, apow[:, :, :t], bb, precision=hp).real
    taprow = jnp.stack([taps[0], taps[1][:, ::-1]]).transpose(0, 1, 4, 2, 3).reshape(2, S5_G, S5_P, S5_TP)

    def pack(zc):
        return jnp.concatenate([zc.real, zc.imag], axis=-1)

    win_f = apow[0][:, t - 1 - jnp.arange(t), None, :] * bb[0].transpose(0, 2, 1)[:, None]
    win_b = apow[1][:, jnp.arange(t), None, :] * bb[1].transpose(0, 2, 1)[:, None]
    win = jnp.stack([pack(win_f), pack(win_b)]).reshape(2, S5_G, S5_TP, 2 * S5_N)
    ca_f = cm[0][:, None] * apow[0][:, 1 + jnp.arange(t), None, :]
    ca_b = cm[1][:, None] * apow[1][:, t - jnp.arange(t), None, :]

    def outpack(ca):
        w = jnp.concatenate([ca.real, -ca.imag], axis=-1)
        return w.reshape(S5_G, S5_TP, 2 * S5_N).transpose(0, 2, 1)

    wout = jnp.stack([outpack(ca_f), outpack(ca_b)])
    lev = (2.0 ** jnp.arange(nlev, dtype=F32)) * t
    pw = jnp.exp(ldt[:, :, None, :] * lev[None, None, :, None])
    p1 = jnp.concatenate([pw.real, pw.real], axis=-1)
    p2 = jnp.concatenate([-pw.imag, pw.imag], axis=-1)
    pw = jnp.stack([p1, p2], axis=3)
    return taprow, win.astype(BF16), wout.astype(BF16), pw


def _s5_kernel(nc, nlev, u_ref, tap_ref, win_ref, wout_ref, pw_ref, h0_ref, y_ref, hfin_ref, conv_ref):
    u = u_ref[0, 0].astype(BF16)
    row = lax.broadcasted_iota(jnp.int32, (nc, 2 * S5_N), 0)
    lane = lax.broadcasted_iota(jnp.int32, (S5_P, S5_TP), 1)
    for s in range(S5_T):
        lo = s * S5_P
        fwd = tap_ref[0, 0] if s == 0 else jnp.where(lane >= lo, pltpu.roll(tap_ref[0, 0], lo, 1), 0.0)
        hi = lo + S5_P
        bwd = tap_ref[1, 0] if hi == S5_TP else jnp.where(lane < hi, pltpu.roll(tap_ref[1, 0], hi, 1), 0.0)
        conv_ref[0, lo:hi, :] = fwd.astype(BF16)
        conv_ref[1, lo:hi, :] = bwd.astype(BF16)

    def cmul(x, d, j):
        return pw_ref[d, 0, j, 0:1] * x + pw_ref[d, 0, j, 1:2] * pltpu.roll(x, S5_N, 1)

    y = None
    for d in range(2):
        v = jnp.dot(u, win_ref[d, 0], preferred_element_type=F32)
        h0 = h0_ref[0, d, 0]
        if d == 0:
            x = jnp.where(row == 0, h0, pltpu.roll(v, 1, 0))
        else:
            x = jnp.where(row == nc - 1, h0, pltpu.roll(v, nc - 1, 0))
        for j in range(nlev):
            sh = 2 ** j
            if d == 0:
                xs = jnp.where(row >= sh, pltpu.roll(x, sh, 0), 0.0)
            else:
                xs = jnp.where(row < nc - sh, pltpu.roll(x, nc - sh, 0), 0.0)
            x = x + cmul(xs, d, j)
        last = nc - 1 if d == 0 else 0
        hfin_ref[0, d, 0] = cmul(x[last:last + 1], d, 0) + v[last:last + 1]
        yd = (jnp.dot(u, conv_ref[d], preferred_element_type=F32)
              + _mm(x, wout_ref[d, 0]))
        y = yd if y is None else y + yd
    y_ref[0, 0] = y


def _s5_scan(uf, tables, h0):
    conv, win, wout, pw = tables
    b, _, nc, _ = uf.shape
    nlev = max(1, (nc - 1).bit_length())
    pw = pw[:, :, :nlev]
    n2 = 2 * S5_N
    y, hfin = pl.pallas_call(
        functools.partial(_s5_kernel, nc, nlev),
        grid=(b, S5_G),
        in_specs=[pl.BlockSpec((1, 1, nc, S5_TP), lambda i, g: (i, g, 0, 0)),
                  pl.BlockSpec((2, 1, S5_P, S5_TP), lambda i, g: (0, g, 0, 0)),
                  pl.BlockSpec((2, 1, S5_TP, n2), lambda i, g: (0, g, 0, 0)),
                  pl.BlockSpec((2, 1, n2, S5_TP), lambda i, g: (0, g, 0, 0)),
                  pl.BlockSpec((2, 1, nlev, 2, n2), lambda i, g: (0, g, 0, 0, 0)),
                  pl.BlockSpec((1, 2, 1, 1, n2), lambda i, g: (i, 0, g, 0, 0))],
        out_specs=[pl.BlockSpec((1, 1, nc, S5_TP), lambda i, g: (i, g, 0, 0)),
                   pl.BlockSpec((1, 2, 1, 1, n2), lambda i, g: (i, 0, g, 0, 0))],
        out_shape=[jax.ShapeDtypeStruct((b, S5_G, nc, S5_TP), F32),
                   jax.ShapeDtypeStruct((b, 2, S5_G, 1, n2), F32)],
        scratch_shapes=[pltpu.VMEM((2, S5_TP, S5_TP), BF16)],
        compiler_params=_cparams("parallel", "parallel"),
        name="s5_scan",
    )(uf, conv, win, wout, pw, h0)
    return y, hfin


def _rw_scan_kernel(nb, cpb, rf_ref, vf_ref, kkf_ref, lwf_ref, kdf_ref, asf_ref,
                    rb_ref, vb_ref, kkb_ref, lwb_ref, kdb_ref, asb_ref, s0_ref,
                    yf_ref, yb_ref, sfin_ref, st_ref):
    c = pl.program_id(1)

    @pl.when(c == 0)
    def _():
        st_ref[...] = s0_ref[0]

    t = CHUNK
    ri = lax.broadcasted_iota(jnp.int32, (t, t), 0)
    ci = lax.broadcasted_iota(jnp.int32, (t, t), 1)
    eye = ri == ci
    eyef = jnp.where(eye, 1.0, 0.0)
    same = lambda s: jnp.where((ri >> s) == (ci >> s), 1.0, 0.0)
    m4, m8, m16, m32 = same(2), same(3), same(4), same(5)
    merge_masks = (m8 - m4, m16 - m8, m32 - m16, 1.0 - m32)
    dir_refs = ((rf_ref, vf_ref, kkf_ref, lwf_ref, kdf_ref, asf_ref),
                (rb_ref, vb_ref, kkb_ref, lwb_ref, kdb_ref, asb_ref))
    units = []
    for d, (r_ref, v_ref, kk_ref, lw_ref, kd_ref, as_ref) in enumerate(dir_refs):
        incl = (ci >= ri) if d == 1 else (ci <= ri)
        strict = (ci > ri) if d == 1 else (ci < ri)
        r, v, kk, kd = (x[0].astype(F32) for x in (r_ref, v_ref, kk_ref, kd_ref))
        lw = lw_ref[0]
        cin = _mm_left01(_chunk_tri01(cpb * t, t, d == 1), lw)
        e_in = jnp.exp(cin)
        e_neg = jnp.exp(-cin)
        rt = r * e_in
        at = -kk * jnp.exp(cin - lw)
        bvec = kk * as_ref[0].astype(F32)
        bt = bvec * e_neg
        kt = kd * e_neg
        for j in range(cpb):
            rows = slice(j * t, (j + 1) * t)
            last = j * t + (0 if d == 1 else t - 1)
            clast = cin[last:last + 1]
            dl = jnp.exp(clast - cin[rows])
            bh = bvec[rows] * dl
            kh = kd[rows] * dl
            dec = jnp.exp(clast)
            for h in range(HEADS):
                sl = slice(h * HEAD_V, (h + 1) * HEAD_V)
                units.append(dict(d=d, j=j, h=h, incl=incl, strict=strict,
                                  at=at[rows, sl], rt=rt[rows, sl], bt=bt[rows, sl], kt=kt[rows, sl],
                                  bh=bh[:, sl], kh=kh[:, sl], v=v[rows, sl], dec=dec[:, sl]))

    a_all = [_mm_nt(jnp.concatenate([u['at'], u['rt']], axis=0),
                    jnp.concatenate([u['bt'], u['kt']], axis=0)) for u in units]
    nmat = [jnp.where(u['strict'], a[0:t, 0:t], 0.0) for u, a in zip(units, a_all)]
    a_kk = [jnp.concatenate([jnp.where(u['strict'], a[0:t, t:2 * t], 0.0),
                             jnp.where(u['incl'], a[t:2 * t, t:2 * t], 0.0)], axis=0)
            for u, a in zip(units, a_all)]
    a_rb = [jnp.where(u['incl'], a[t:2 * t, 0:t], 0.0) for u, a in zip(units, a_all)]
    akv = [_mm(a, u['v']) for u, a in zip(units, a_kk)]
    kv = [_mm_tn(u['v'], u['kh']) for u in units]
    nd = [x * m4 for x in nmat]
    n2 = [_mm(x, x) for x in nd]
    tinv = [eyef + x + _mm(eyef + x, y) for x, y in zip(nd, n2)]
    for mk in merge_masks:
        w = [_mm(ti, x * mk) for ti, x in zip(tinv, nmat)]
        tinv = [ti + _mm(wi, ti) for ti, wi in zip(tinv, w)]
    zz = [_mm(ti, jnp.concatenate([u['at'], kvv[0:t]], axis=1)) for ti, u, kvv in zip(tinv, units, akv)]
    ght = [_mm_tn(z, u['bh']) for u, z in zip(units, zz)]
    qy = [_mm(a, z) for a, z in zip(a_rb, zz)]
    qmat = [u['rt'] + x[:, :HEAD_V] for u, x in zip(units, qy)]
    y0 = [x[:, HEAD_V:] + kvv[t:2 * t] for x, kvv in zip(qy, akv)]
    gmat = [x[:HEAD_V] for x in ght]
    hmat = [x[HEAD_V:] + k2 for x, k2 in zip(ght, kv)]
    idx = {(u['d'], u['j'], u['h']): n for n, u in enumerate(units)}
    state = {(d, h): st_ref[d, h] for d in range(2) for h in range(HEADS)}
    ys = {}
    for step in range(cpb):
        for d in range(2):
            j = step if d == 0 else cpb - 1 - step
            for h in range(HEADS):
                n = idx[(d, j, h)]
                st = state[(d, h)]
                ys[(d, j, h)] = _mm_nt(qmat[n], st) + y0[n]
                state[(d, h)] = units[n]['dec'] * st + _mm(st, gmat[n]) + hmat[n]
    for d, y_ref in enumerate((yf_ref, yb_ref)):
        y_ref[0] = jnp.concatenate(
            [jnp.concatenate([ys[(d, j, h)] for h in range(HEADS)], axis=1) for j in range(cpb)], axis=0)
        for h in range(HEADS):
            st_ref[d, h] = state[(d, h)]

    @pl.when(c == nb - 1)
    def _():
        sfin_ref[0] = st_ref[...]


def _rw_scan(pre, s0):
    r, v, kk, _, _, lw0, kd0, as0, lw1, kd1, as1 = pre
    b, l, _ = r.shape
    cpb = min(RW_CPB, l // CHUNK)
    nb = l // (cpb * CHUNK)
    fw = pl.BlockSpec((1, cpb * CHUNK, GROUP_W), lambda i, c: (i, c, 0))
    bw = pl.BlockSpec((1, cpb * CHUNK, GROUP_W), lambda i, c: (i, nb - 1 - c, 0))
    st = pl.BlockSpec((1, 2, HEADS, HEAD_V, HEAD_V), lambda i, c: (i, 0, 0, 0, 0))
    return pl.pallas_call(
        functools.partial(_rw_scan_kernel, nb, cpb),
        grid=(b, nb),
        in_specs=[fw] * 6 + [bw] * 6 + [st],
        out_specs=[fw, bw, st],
        out_shape=[jax.ShapeDtypeStruct((b, l, GROUP_W), F32)] * 2
                  + [jax.ShapeDtypeStruct((b, 2, HEADS, HEAD_V, HEAD_V), F32)],
        scratch_shapes=[pltpu.VMEM((2, HEADS, HEAD_V, HEAD_V), F32)],
        compiler_params=_cparams("parallel", "arbitrary"),
        name="rwkv_scan",
    )(r, v, kk, lw0, kd0, as0, r, v, kk, lw1, kd1, as1, s0)


def _rope(x, cos, sin_signed):
    lane = lax.broadcasted_iota(jnp.int32, x.shape, 1)
    n = x.shape[1]
    swapped = jnp.where((lane & 16) == 0, pltpu.roll(x, n - 16, 1), pltpu.roll(x, 16, 1))
    return x * cos + swapped * sin_signed


def _ret_scan_kernel(nb, cpb, qf_ref, kf_ref, vf_ref, qb_ref, kb_ref, vb_ref,
                     dmat_ref, qdec_ref, kdec_ref, sdec_ref, s0_ref,
                     of_ref, ob_ref, sfin_ref, st_ref):
    c = pl.program_id(1)

    @pl.when(c == 0)
    def _():
        st_ref[...] = s0_ref[0]

    t = dmat_ref.shape[-1]
    units = []
    for d, (q_ref, k_ref, v_ref) in enumerate(((qf_ref, kf_ref, vf_ref), (qb_ref, kb_ref, vb_ref))):
        for j in range(cpb):
            rows = slice(j * t, (j + 1) * t)
            q, k, v = q_ref[0, rows], k_ref[0, rows], v_ref[0, rows]
            qd = q * qdec_ref[d]
            kh = k * kdec_ref[d]
            for h in range(HEADS):
                sl = slice(h * HEAD_V, (h + 1) * HEAD_V)
                units.append(dict(d=d, j=j, h=h, q=q[:, sl], k=k[:, sl], qd=qd[:, sl], kh=kh[:, sl],
                                  v=v[:, sl], dec=sdec_ref[d][:, sl]))
    scores = [_mm_nt(u['q'], u['k']).astype(BF16) * dmat_ref[u['d'], u['h']] for u in units]
    intra = [_mm(p, u['v']) for p, u in zip(scores, units)]
    kv = [_mm_tn(u['v'], u['kh']) for u in units]
    idx = {(u['d'], u['j'], u['h']): i for i, u in enumerate(units)}
    entering = {}
    for d in range(2):
        for h in range(HEADS):
            st = st_ref[d, h]
            for step in range(cpb):
                j = step if d == 0 else cpb - 1 - step
                i = idx[(d, j, h)]
                entering[i] = st
                st = units[i]['dec'] * st + kv[i]
            st_ref[d, h] = st
    outs = [x + _mm_nt(u['qd'], entering[i]) for i, (x, u) in enumerate(zip(intra, units))]
    for d, o_ref in enumerate((of_ref, ob_ref)):
        o_ref[0] = jnp.concatenate(
            [jnp.concatenate([outs[idx[(d, j, h)]] for h in range(HEADS)], axis=1) for j in range(cpb)], axis=0)

    @pl.when(c == nb - 1)
    def _():
        sfin_ref[0] = st_ref[...]


def _ret_tables(decay_logit, n):
    lg = jax.nn.log_sigmoid(decay_logit.astype(F32))
    pos = jnp.arange(n, dtype=F32)
    lag = pos[:, None] - pos[None, :]
    lag = jnp.stack([lag, -lag])
    dmat = jnp.where(lag[:, None] >= 0, jnp.exp(lg[:, :, None, None] * lag[:, None]), 0.0)
    lanes = jnp.repeat(lg, RET_DK, axis=-1)[:, None, :]
    qpow = jnp.stack([pos + 1.0, n - pos])[:, :, None]
    kpow = jnp.stack([n - 1.0 - pos, pos])[:, :, None]
    return dmat.astype(BF16), jnp.exp(lanes * qpow), jnp.exp(lanes * kpow), jnp.exp(lanes * n)


def _gla_scan_kernel(nb, cpb, qf_ref, kf_ref, vf_ref, af_ref, qb_ref, kb_ref, vb_ref, ab_ref,
                     aup_ref, abias_ref, s0_ref, of_ref, ob_ref, sfin_ref, st_ref):
    c = pl.program_id(1)

    @pl.when(c == 0)
    def _():
        st_ref[...] = s0_ref[0]

    t = GLA_T
    n = cpb * t
    dk = GLA_DK
    scale = dk ** -0.5
    dirs = ((qf_ref, kf_ref, vf_ref, af_ref), (qb_ref, kb_ref, vb_ref, ab_ref))
    units = []
    for d, (q_ref, k_ref, v_ref, a_ref) in enumerate(dirs):
        incl = _tri_masks(t, d == 1)[0]
        q, k, v = q_ref[0], k_ref[0] * scale, v_ref[0]
        lw = -_softplus(-(_mm(a_ref[0], aup_ref[d]) + abias_ref[d])) * (1.0 / GLA_TAU)
        cin = _mm_left01(_chunk_tri01(n, t, d == 1), lw)
        qe = q * jnp.exp(cin)
        for j in range(cpb):
            rows = slice(j * t, (j + 1) * t)
            last = j * t + (0 if d == 1 else t - 1)
            mid = j * t + (t // 2 if d == 1 else t // 2 - 1)
            cj = cin[rows]
            clast = cin[last:last + 1]
            cmid = cin[mid:mid + 1]
            qt = q[rows] * jnp.exp(cj - cmid)
            kt = k[rows] * jnp.exp(cmid - cj)
            kh = k[rows] * jnp.exp(clast - cj)
            dec = jnp.exp(clast)
            for h in range(HEADS):
                sk = slice(h * dk, (h + 1) * dk)
                sv = slice(h * HEAD_V, (h + 1) * HEAD_V)
                units.append(dict(d=d, j=j, h=h, incl=incl, qt=qt[:, sk], kt=kt[:, sk], qe=qe[rows, sk],
                                  kh=kh[:, sk], v=v[rows, sv], dec=dec[:, sk]))

    amat = [jnp.where(u['incl'], _mm_nt(u['qt'], u['kt']), 0.0) for u in units]
    intra = [_mm(a, u['v']) for a, u in zip(amat, units)]
    kv = [_mm_tn(u['v'], u['kh']) for u in units]
    idx = {(u['d'], u['j'], u['h']): i for i, u in enumerate(units)}
    entering = {}
    for d in range(2):
        for h in range(HEADS):
            st = st_ref[d, h]
            for step in range(cpb):
                j = step if d == 0 else cpb - 1 - step
                i = idx[(d, j, h)]
                entering[i] = st
                st = units[i]['dec'] * st + kv[i]
            st_ref[d, h] = st
    outs = [x + _mm_nt(u['qe'], entering[i]) for i, (x, u) in enumerate(zip(intra, units))]
    for d, o_ref in enumerate((of_ref, ob_ref)):
        o_ref[0] = jnp.concatenate(
            [jnp.concatenate([outs[idx[(d, j, h)]] for h in range(HEADS)], axis=1) for j in range(cpb)], axis=0)

    @pl.when(c == nb - 1)
    def _():
        sfin_ref[0] = st_ref[...]


def _rope_tables(rows):
    nf = RET_DK // 4
    inv = ROPE_BASE ** (-jnp.arange(nf, dtype=F32) / nf)
    lane = jnp.arange(GROUP_W)
    freq = inv[lane % nf]
    sign = jnp.where((lane & 16) == 0, -1.0, 1.0)
    ar = jnp.arange(rows, dtype=F32)[:, None] * freq[None, :]
    ac = jnp.arange(GRID_W, dtype=F32)[:, None] * freq[None, :]
    return (jnp.cos(ar).reshape(rows, 1, GROUP_W), (jnp.sin(ar) * sign).reshape(rows, 1, GROUP_W),
            jnp.cos(ac), jnp.sin(ac) * sign)


def _ret_scan(z, tables, s0):
    b, l, _ = z.shape
    n = min(RET_BLOCK, l)
    nb = l // n
    fw = lambda j: pl.BlockSpec((1, n, GROUP_W), lambda i, c: (i, c, j))
    bw = lambda j: pl.BlockSpec((1, n, GROUP_W), lambda i, c: (i, nb - 1 - c, j))
    full = lambda a: pl.BlockSpec(a.shape, lambda i, c: (0,) * a.ndim)
    st = pl.BlockSpec((1, 2, HEADS, HEAD_V, RET_DK), lambda i, c: (i, 0, 0, 0, 0))
    return pl.pallas_call(
        functools.partial(_ret_scan_kernel, nb, n // RET_T),
        grid=(b, nb),
        in_specs=[fw(0), fw(1), fw(2), bw(0), bw(1), bw(2)] + [full(a) for a in tables] + [st],
        out_specs=[fw(0), bw(0), st],
        out_shape=[jax.ShapeDtypeStruct((b, l, GROUP_W), F32)] * 2
                  + [jax.ShapeDtypeStruct((b, 2, HEADS, HEAD_V, RET_DK), F32)],
        scratch_shapes=[pltpu.VMEM((2, HEADS, HEAD_V, RET_DK), F32)],
        compiler_params=_cparams("parallel", "arbitrary"),
        name="retention_scan",
    )(z, z, z, z, z, z, *tables, s0)


def _gla_scan(z, aup, abias, s0):
    b, l, _ = z.shape
    cpb = min(GLA_BLOCK, l) // CHUNK
    nb = l // (cpb * CHUNK)
    blk = lambda w, j, rev: pl.BlockSpec(
        (1, cpb * CHUNK, w), (lambda i, c: (i, nb - 1 - c, j)) if rev else (lambda i, c: (i, c, j)))
    st = pl.BlockSpec((1, 2, HEADS, HEAD_V, GLA_DK), lambda i, c: (i, 0, 0, 0, 0))
    ofw = pl.BlockSpec((1, cpb * CHUNK, GROUP_W), lambda i, c: (i, c, 0))
    obw = pl.BlockSpec((1, cpb * CHUNK, GROUP_W), lambda i, c: (i, nb - 1 - c, 0))
    return pl.pallas_call(
        functools.partial(_gla_scan_kernel, nb, cpb * CHUNK // GLA_T),
        grid=(b, nb),
        in_specs=[blk(GLA_QK, 0, False), blk(GLA_QK, 1, False), blk(GROUP_W, 1, False), blk(128, 6, False),
                  blk(GLA_QK, 0, True), blk(GLA_QK, 1, True), blk(GROUP_W, 1, True), blk(128, 6, True),
                  pl.BlockSpec(aup.shape, lambda i, c: (0, 0, 0)),
                  pl.BlockSpec(abias.shape, lambda i, c: (0, 0, 0)), st],
        out_specs=[ofw, obw, st],
        out_shape=[jax.ShapeDtypeStruct((b, l, GROUP_W), F32)] * 2
                  + [jax.ShapeDtypeStruct((b, 2, HEADS, HEAD_V, GLA_DK), F32)],
        scratch_shapes=[pltpu.VMEM((2, HEADS, HEAD_V, GLA_DK), F32)],
        compiler_params=_cparams("parallel", "arbitrary"),
        name="gla_scan",
    )(z, z, z, z, z, z, z, z, aup, abias, s0)


def _mix_kernel(x_ref, g1_ref, y5_ref, u5_ref, d5_ref, gw_ref, gb_ref,
                ryf_ref, ryb_ref, rbonus_ref, rg_ref, rlng_ref, rlnb_ref,
                tof_ref, tob_ref, tg_ref, tln_ref,
                gof_ref, gob_ref, gg_ref, gln_ref,
                bones_ref, wo_ref, o_ref, y5t_ref):
    bones = bones_ref[...]
    inv = 1.0 / HEAD_V

    def hmean(a):
        return _mm_right01(a, bones) * inv

    nch = y5_ref.shape[2]
    gph = LANES // S5_P
    y5g = [y5_ref[0, g] for g in range(S5_G)]
    for s in range(S5_T):
        for hf in range(GROUP_W // LANES):
            y5t_ref[hf, pl.ds(s, nch, stride=S5_T), :] = jnp.concatenate(
                [yg[:, s * S5_P:(s + 1) * S5_P] for yg in y5g[hf * gph:(hf + 1) * gph]], axis=1)
    y = jnp.concatenate([y5t_ref[0], y5t_ref[1]], axis=1) + d5_ref[...] * u5_ref[0]
    y = _gelu_tanh(y)
    ya = y * _sigmoid(_mm(y, gw_ref[...]) + gb_ref[...])
    yr = ryf_ref[0] + ryb_ref[0]
    dlt = yr - hmean(yr)
    yn = dlt * lax.rsqrt(hmean(dlt * dlt) + GN_EPS)
    yb = (yn * rlng_ref[...] + rlnb_ref[...] + rbonus_ref[0].astype(F32)) * rg_ref[0].astype(F32)
    ot = tof_ref[0] + tob_ref[0]
    yc = ot * lax.rsqrt(hmean(ot * ot) + EPS) * tln_ref[...] * _silu(tg_ref[0])
    og = gof_ref[0] + gob_ref[0]
    yd = og * lax.rsqrt(hmean(og * og) + EPS) * gln_ref[...] * _silu(gg_ref[0])
    mix = (_mm(ya, wo_ref[0]) + _mm(yb, wo_ref[1]) + _mm(yc, wo_ref[2]) + _mm(yd, wo_ref[3]))
    o_ref[0] = x_ref[0] + g1_ref[0] * mix


def _mix(x, g1, y5, u5, rw_pre, rw_y, zret, ret_o, zgla, gla_o, p, tm):
    b, l, d = x.shape
    tok = lambda w, j: pl.BlockSpec((1, tm, w), lambda i, t: (i, t, j))
    vec = lambda a: pl.BlockSpec(a.shape, lambda i, t: (0,) * a.ndim)
    t256 = tok(GROUP_W, 0)
    args = [x, g1, y5, u5, p['s5_d'], p['glu_w'], p['glu_b'],
            rw_y[0], rw_y[1], rw_pre[4], rw_pre[3], p['rw_ln_g'], p['rw_ln_b'],
            ret_o[0], ret_o[1], zret, p['ret_ln_g'],
            gla_o[0], gla_o[1], zgla, p['gla_ln_g'],
            p['bones'], p['w_out']]
    grouped = pl.BlockSpec((1, S5_G, tm // S5_T, S5_TP), lambda i, t: (i, 0, t, 0))
    specs = [tok(d, 0), pl.BlockSpec((1, 1, d), lambda i, t: (i, 0, 0)), grouped, t256,
             vec(p['s5_d']), vec(p['glu_w']), vec(p['glu_b']),
             t256, t256, t256, t256, vec(p['rw_ln_g']), vec(p['rw_ln_b']),
             t256, t256, tok(GROUP_W, 3), vec(p['ret_ln_g']),
             t256, t256, tok(GROUP_W, 2), vec(p['gla_ln_g']),
             vec(p['bones']), vec(p['w_out'])]
    return pl.pallas_call(
        _mix_kernel,
        grid=(b, l // tm),
        in_specs=specs,
        out_specs=tok(d, 0),
        out_shape=jax.ShapeDtypeStruct((b, l, d), F32),
        scratch_shapes=[pltpu.VMEM((GROUP_W // LANES, tm, LANES), F32)],
        compiler_params=_cparams("parallel", "parallel"),
        name="mix_outproj",
    )(*args)


def _mlp_kernel(final, nff, x_ref, g_ref, sc_ref, sh_ref, gate_ref, w1_ref, w2_ref, fg_ref, o_ref):
    x = x_ref[0]
    hb = _modnorm(x, g_ref[...], sc_ref[0], sh_ref[0]).astype(BF16)
    ff = w1_ref.shape[2] // nff
    acc = None
    for j in range(nff):
        a = jnp.maximum(jnp.dot(hb, w1_ref[0, :, j * ff:(j + 1) * ff], preferred_element_type=F32), 0.0)
        part = jnp.dot((a * a).astype(BF16), w2_ref[0, j * ff:(j + 1) * ff, :], preferred_element_type=F32)
        acc = part if acc is None else acc + part
    y = x + gate_ref[0] * acc
    if final:
        ms = jnp.mean(y * y, axis=-1, keepdims=True)
        y = y * lax.rsqrt(ms + EPS) * fg_ref[...]
    o_ref[0] = y


def _mlp(x, g, sc, sh, gate, w1, w2, layer, final_g, final, tm):
    b, l, d = x.shape
    tok = pl.BlockSpec((1, tm, d), lambda i, t: (i, t, 0))
    vec = pl.BlockSpec((1, 1, d), lambda i, t: (i, 0, 0))
    row = pl.BlockSpec((1, d), lambda i, t: (0, 0))
    once = lambda a: pl.BlockSpec((1,) + a.shape[1:], lambda i, t: (layer, 0, 0), pipeline_mode=pl.Buffered(1))
    return pl.pallas_call(
        functools.partial(_mlp_kernel, final, 4),
        grid=(b, l // tm),
        in_specs=[tok, row, vec, vec, vec, once(w1), once(w2), row],
        out_specs=tok,
        out_shape=jax.ShapeDtypeStruct((b, l, d), F32),
        compiler_params=_cparams("parallel", "parallel"),
        name="mlp",
    )(x, g, sc, sh, gate, w1, w2, final_g)


def kernel(x, c, ctx, c_ctx, ada_w, ada_b, norm1_g, norm2_g, w_in, w_out, s5_lam_re, s5_lam_im, s5_log_dt, s5_b_re, s5_b_im, s5_c_re, s5_c_im, s5_d, s5_glu_w, s5_glu_b, rw_mu, rw_w0, rw_w_up, rw_a0, rw_a_up, rw_g_up, rw_k_k, rw_k_a, rw_r_k, rw_ln_g, rw_ln_b, ret_decay_logit, ret_ln_g, gla_a_up, gla_a_b, gla_ln_g, mlp_w1, mlp_w2, final_g):
    b, l, d = x.shape
    lc = ctx.shape[1]
    depth = ada_w.shape[0]
    rows = l // GRID_W
    assert l % 512 == 0 and lc % CHUNK == 0 and lc % S5_T == 0 and d % 128 == 0

    cond = jnp.zeros((8, d), F32).at[:b].set(c).at[b].set(c_ctx)
    mod = _modulation(cond, ada_w, ada_b)

    lane = jnp.arange(GROUP_W)
    bones = (lane[:, None] // HEAD_V == lane[None, :] // HEAD_V).astype(BF16)
    rope_lat = _rope_tables(rows)
    ident = (jnp.ones((lc // CHUNK, 1, GROUP_W), F32), jnp.zeros((lc // CHUNK, 1, GROUP_W), F32),
             jnp.ones((GRID_W, GROUP_W), F32), jnp.zeros((GRID_W, GROUP_W), F32))
    nlev = max(1, (l // S5_T - 1).bit_length())
    c0, c1, c2 = GROUP_W, GROUP_W + RW_COLS, GROUP_W + RW_COLS + 4 * GROUP_W
    row2 = lambda a: a.reshape(1, -1).astype(F32)
    w_in_b, w_out_b, w1_b, w2_b = _to_bf16(w_in), _to_bf16(w_out), _to_bf16(mlp_w1), _to_bf16(mlp_w2)

    xc = ctx
    for i in range(depth):
        last = i == depth - 1
        m = mod[i].reshape(8, N_ADA, d)
        ml = m[:b, :, None, :]
        mc = jnp.broadcast_to(m[b][None, :, None, :], (b, N_ADA, 1, d))
        wi = w_in_b[i]
        w5, wr, wt = wi[:, :c0], wi[:, c0:c1], wi[:, c1:c2]
        wg = jnp.pad(wi[:, c2:], ((0, 0), (0, GLA_COLS_PAD - (wi.shape[1] - c2))))
        n1 = row2(norm1_g[i])
        n2 = row2(norm2_g[i])

        s5_tab = _s5_tables(s5_lam_re[i], s5_lam_im[i], s5_log_dt[i], s5_b_re[i], s5_b_im[i],
                            s5_c_re[i], s5_c_im[i], nlev)
        zpad = lambda a, lo, n: jnp.zeros((2, 128, n), F32).at[:, lo:lo + a.shape[1]].set(a).astype(BF16)
        rwp = dict(
            mu=row2(rw_mu[i]),
            w0=rw_w0[i].reshape(2, 1, GROUP_W), a0=rw_a0[i].reshape(2, 1, GROUP_W),
            wup=zpad(rw_w_up[i], 0, GROUP_W), aup=zpad(rw_a_up[i], RW_W_RANK, GROUP_W),
            gup=jnp.zeros((128, GROUP_W), F32).at[RW_W_RANK + RW_A_RANK:].set(rw_g_up[i]).astype(BF16),
            kk=row2(rw_k_k[i]), ka=row2(rw_k_a[i]), rk=row2(rw_r_k[i]), bones=bones)
        ret_tab = _ret_tables(ret_decay_logit[i], RET_T)
        gla_aup = zpad(gla_a_up[i], 0, GLA_QK)
        gla_ab = gla_a_b[i].reshape(2, 1, GLA_QK).astype(F32)
        mixp = dict(s5_d=row2(s5_d[i]), glu_w=s5_glu_w[i].astype(BF16), glu_b=row2(s5_glu_b[i]),
                    rw_ln_g=row2(rw_ln_g[i]), rw_ln_b=row2(rw_ln_b[i]), ret_ln_g=row2(ret_ln_g[i]),
                    gla_ln_g=row2(gla_ln_g[i]), bones=bones,
                    w_out=w_out_b[i].reshape(4, GROUP_W, d))

        def mixers(xx, mm, is_lat, states):
            tm = 512 if is_lat else lc
            z5, z5g, zt, zg, pre = _inproj(xx, n1, mm[:, 1], mm[:, 0], w5, wr, wt, wg,
                                           rope_lat if is_lat else ident, rwp, is_lat, tm)
            y5, h5 = _s5_scan(z5g, s5_tab, states[0])
            yrf, yrb, srw = _rw_scan(pre, states[1])
            otf, otb, sret = _ret_scan(zt, ret_tab, states[2])
            ogf, ogb, sgla = _gla_scan(zg, gla_aup, gla_ab, states[3])
            outs = (z5, y5, pre, (yrf, yrb), zt, (otf, otb), zg, (ogf, ogb))
            return outs, (h5, srw, sret, sgla)

        def block(xx, mm, outs, is_lat, fin):
            tm = 512 if is_lat else lc
            z5, y5, pre, yr, zt, ot, zg, og = outs
            x1 = _mix(xx, mm[:, 2], y5, z5, pre, yr, zt, ot, zg, og, mixp, tm)
            return _mlp(x1, n2, mm[:, 4], mm[:, 3], mm[:, 5], w1_b, w2_b, i, row2(final_g), fin, tm)

        zeros = (jnp.zeros((b, 2, S5_G, 1, 2 * S5_N), F32),
                 jnp.zeros((b, 2, HEADS, HEAD_V, HEAD_V), F32),
                 jnp.zeros((b, 2, HEADS, HEAD_V, RET_DK), F32),
                 jnp.zeros((b, 2, HEADS, HEAD_V, GLA_DK), F32))
        outs_c, st_c = mixers(xc, mc, False, zeros)
        outs_l, _ = mixers(x, ml, True, st_c)
        x = block(x, ml, outs_l, True, last)
        if not last:
            xc = block(xc, mc, outs_c, False, False)
    return x
```

```python
import functools
import math

import jax
import jax.numpy as jnp
from jax import lax
from jax.experimental import pallas as pl
from jax.experimental.pallas import tpu as pltpu

F32 = jnp.float32
BF16 = jnp.bfloat16

LANES = 128
GRID_W = 64
GROUP_W = 256
N_ADA = 6
EPS = 1e-6
GN_EPS = 64e-5
CHUNK = 64
RW_CPB = 4
RET_T = 256
RET_BLOCK = 512
GLA_T = 128
GLA_BLOCK = 512
HEADS = 4
HEAD_V = 64
S5_P = 16
S5_G = 16
S5_N = 64
S5_T = 32
S5_TP = S5_T * S5_P
RW_COLS = 896
RW_W_RANK = 32
RW_A_RANK = 32
RW_G_RANK = 64
RET_DK = 64
GLA_DK = 32
GLA_QK = 128
GLA_RANK = 16
GLA_TAU = 16.0
GLA_COLS_PAD = 896
ROPE_BASE = 10000.0
VMEM_LIMIT = 56 * 1024 * 1024


def _cparams(*sem):
    return pltpu.CompilerParams(dimension_semantics=sem, vmem_limit_bytes=VMEM_LIMIT)


def _mm(a, b):
    return jnp.dot(a.astype(BF16), b.astype(BF16), preferred_element_type=F32)


def _mm_nt(a, b):
    return lax.dot_general(a.astype(BF16), b.astype(BF16), (((1,), (1,)), ((), ())),
                           preferred_element_type=F32)


def _mm_tn(a, b):
    return lax.dot_general(a.astype(BF16), b.astype(BF16), (((0,), (0,)), ((), ())),
                           preferred_element_type=F32)


def _split2(x):
    hi = x.astype(BF16)
    return hi, (x - hi.astype(F32)).astype(BF16)


def _join2(p):
    return p[0].astype(F32) + p[1].astype(F32)


def _mm_x3p(a, b):
    return (jnp.dot(a[0], b[0], preferred_element_type=F32) + jnp.dot(a[0], b[1], preferred_element_type=F32)
            + jnp.dot(a[1], b[0], preferred_element_type=F32))


def _chunk_tri01(n, t, reverse):
    ri = lax.broadcasted_iota(jnp.int32, (n, n), 0)
    ci = lax.broadcasted_iota(jnp.int32, (n, n), 1)
    shift = t.bit_length() - 1
    tri = (ci >= ri) if reverse else (ci <= ri)
    return jnp.where((ri >> shift) == (ci >> shift), jnp.where(tri, 1.0, 0.0), 0.0).astype(BF16)


def _mm_left01(m01, x):
    hi, lo = _split2(x)
    return jnp.dot(m01, hi, preferred_element_type=F32) + jnp.dot(m01, lo, preferred_element_type=F32)


def _mm_right01(x, m01):
    return jnp.dot(x.astype(BF16), m01, preferred_element_type=F32)


def _sigmoid(x):
    return 1.0 / (1.0 + jnp.exp(-x))


def _softplus(x):
    return jnp.maximum(x, 0.0) + jnp.log(1.0 + jnp.exp(-jnp.abs(x)))


def _silu(x):
    return x * _sigmoid(x)


def _gelu_tanh(x):
    return 0.5 * x * (1.0 + jnp.tanh(math.sqrt(2.0 / math.pi) * (x + 0.044715 * x * x * x)))


def _tri_masks(t, reverse):
    ri = lax.broadcasted_iota(jnp.int32, (t, t), 0)
    ci = lax.broadcasted_iota(jnp.int32, (t, t), 1)
    if reverse:
        return ci >= ri, ci > ri, ci == ri
    return ci <= ri, ci < ri, ci == ri


def _row_to_col(row, eye):
    n = row.shape[1]
    return jnp.sum(jnp.where(eye, jnp.broadcast_to(row, (n, n)), 0.0), axis=1, keepdims=True)


def _cast_kernel(x_ref, o_ref):
    o_ref[...] = x_ref[...].astype(o_ref.dtype)


def _to_bf16(w):
    depth, r, c = w.shape
    tr = 256
    spec = pl.BlockSpec((1, tr, c), lambda i, j: (i, j, 0))
    return pl.pallas_call(
        _cast_kernel,
        grid=(depth, r // tr),
        in_specs=[spec],
        out_specs=spec,
        out_shape=jax.ShapeDtypeStruct(w.shape, BF16),
        compiler_params=_cparams("parallel", "parallel"),
        name="cast_bf16",
    )(w)


def _mod_kernel(cond_ref, w_ref, b_ref, o_ref):
    c = cond_ref[...]
    o_ref[0] = _mm(_silu(c), w_ref[0]) + b_ref[0]


def _modulation(cond, ada_w, ada_b):
    depth, d, n = ada_w.shape
    tn = 1536
    return pl.pallas_call(
        _mod_kernel,
        grid=(depth, n // tn),
        in_specs=[pl.BlockSpec((8, d), lambda i, j: (0, 0)),
                  pl.BlockSpec((1, d, tn), lambda i, j: (i, 0, j)),
                  pl.BlockSpec((1, 1, tn), lambda i, j: (i, 0, j))],
        out_specs=pl.BlockSpec((1, 8, tn), lambda i, j: (i, 0, j)),
        out_shape=jax.ShapeDtypeStruct((depth, 8, n), F32),
        compiler_params=_cparams("parallel", "parallel"),
        name="adaln_mod",
    )(cond, ada_w, ada_b.reshape(depth, 1, n))


def _modnorm(x, g, sc, sh):
    ms = jnp.mean(x * x, axis=-1, keepdims=True)
    return x * lax.rsqrt(ms + EPS) * g * (1.0 + sc) + sh


def _shift_columns(z, segments):
    tm, width = z.shape
    lane = lax.broadcasted_iota(jnp.int32, (tm, LANES), 1)
    starts = [s for s, _ in segments] + [width]
    cols = []
    for c in range(width // LANES):
        lo, hi = c * LANES, (c + 1) * LANES
        inside = [(max(starts[i], lo), segments[i][1]) for i in range(len(segments))
                  if starts[i] < hi and starts[i + 1] > lo]
        col = inside[-1][1](c)
        for first, fn in reversed(inside[:-1]):
            nxt = [f for f, _ in inside if f > first][0]
            col = jnp.where(lane < nxt - lo, fn(c), col)
        cols.append(col)
    return jnp.concatenate(cols, axis=1)


def _rw_prep_math(grid_shift, z, zp, zn, mu, w0_ref, wup_ref, a0_ref, aup_ref, gup, kkp, ka, rk, bones):
    tm = z.shape[0]
    row = lax.broadcasted_iota(jnp.int32, (tm, LANES), 0)
    col = lambda x, c: x[:, c * LANES:(c + 1) * LANES]
    if grid_shift:
        pos = row & (GRID_W - 1)
        left = lambda c: jnp.where(pos == 0, 0.0, pltpu.roll(col(z, c), 1, 0))
        right = lambda c: jnp.where(pos == GRID_W - 1, 0.0, pltpu.roll(col(z, c), tm - 1, 0))
        up = lambda c: jnp.concatenate([col(zp, c), col(z, c)[:tm - GRID_W]], axis=0)
        down = lambda c: jnp.concatenate([col(z, c)[GRID_W:], col(zn, c)], axis=0)
        q = RW_COLS // 4
        shifted = _shift_columns(z, [(0, left), (q, right), (2 * q, up), (3 * q, down)])
    else:
        prev = lambda c: jnp.where(row == 0, 0.0, pltpu.roll(col(z, c), 1, 0))
        nxt = lambda c: jnp.where(row == tm - 1, 0.0, pltpu.roll(col(z, c), tm - 1, 0))
        shifted = _shift_columns(z, [(0, prev), (RW_COLS // 2, nxt)])
    zm = z + mu * (shifted - z)
    r = zm[:, 0:256]
    k = zm[:, 256:512]
    v = zm[:, 512:768]
    lo = zm[:, 768:896]
    g = _mm(_sigmoid(lo), gup)
    kk = k * kkp
    kk = kk * lax.rsqrt(_mm_right01(kk * kk, bones) + 1e-12)
    bonus = _mm_right01(r * k * rk, bones) * v
    outs = [r.astype(BF16), v.astype(BF16), kk.astype(BF16), g.astype(BF16), bonus.astype(BF16)]
    tlo = jnp.tanh(lo)
    for d in range(2):
        w_raw = -_softplus(-(w0_ref[d] + _mm(tlo, wup_ref[d]))) - 0.5
        a = _sigmoid(a0_ref[d] + _mm(lo, aup_ref[d]))
        outs += [-jnp.exp(w_raw),
                 (k * (1.0 + (a - 1.0) * ka)).astype(BF16), a.astype(BF16)]
    return outs


def _inproj_kernel(grid_shift, nt, x_ref, xp_ref, xn_ref, g_ref, sc_ref, sh_ref,
                   w5_ref, wr_ref, wt_ref, wg_ref, cr_ref, sr_ref, cc_ref, sn_ref,
                   mu_ref, w0_ref, wup_ref, a0_ref, aup_ref, gup_ref, kk_ref, ka_ref, rk_ref, bones_ref,
                   o5_ref, o5g_ref, ot_ref, og_ref, *rest):
    rw_refs, z5h_ref = rest[:-1], rest[-1]
    j = pl.program_id(1)
    norm = lambda x: _modnorm(x, g_ref[...], sc_ref[0], sh_ref[0]).astype(BF16)
    hb = norm(x_ref[0])
    z5 = jnp.dot(hb, w5_ref[...], preferred_element_type=F32)
    o5_ref[0] = z5
    og_ref[0] = jnp.dot(hb, wg_ref[...], preferred_element_type=F32)
    zr = jnp.dot(hb, wr_ref[...], preferred_element_type=F32)
    zp = zn = None
    if grid_shift:
        zp = jnp.where(j > 0, jnp.dot(norm(xp_ref[0]), wr_ref[...], preferred_element_type=F32), 0.0)
        zn = jnp.where(j < nt - 1, jnp.dot(norm(xn_ref[0]), wr_ref[...], preferred_element_type=F32), 0.0)
    rw = _rw_prep_math(grid_shift, zr, zp, zn, mu_ref[...], w0_ref, wup_ref, a0_ref, aup_ref, gup_ref[...],
                       kk_ref[...], ka_ref[...], rk_ref[...], bones_ref[...])
    for ref, val in zip(rw_refs, rw):
        ref[0] = val
    zt = jnp.dot(hb, wt_ref[...], preferred_element_type=F32)
    lane = lax.broadcasted_iota(jnp.int32, (GRID_W, GROUP_W), 1)
    by_row = (lane & 32) == 0
    nrow = cr_ref.shape[0]
    cos = jnp.concatenate([jnp.where(by_row, cr_ref[j], cc_ref[...]) for j in range(nrow)], axis=0)
    sin = jnp.concatenate([jnp.where(by_row, sr_ref[j], sn_ref[...]) for j in range(nrow)], axis=0)
    ot_ref[0, :, 0:GROUP_W] = _rope(zt[:, 0:GROUP_W], cos, sin)
    ot_ref[0, :, GROUP_W:2 * GROUP_W] = _rope(zt[:, GROUP_W:2 * GROUP_W] * RET_DK ** -0.5, cos, sin)
    ot_ref[0, :, 2 * GROUP_W:] = zt[:, 2 * GROUP_W:]
    nch = o5g_ref.shape[2]
    gph = LANES // S5_P
    for hf in range(GROUP_W // LANES):
        z5h_ref[hf] = z5[:, hf * LANES:(hf + 1) * LANES]
    for s in range(S5_T):
        for hf in range(GROUP_W // LANES):
            rows = z5h_ref[hf, pl.ds(s, nch, stride=S5_T), :]
            for g in range(gph):
                o5g_ref[0, hf * gph + g, :, s * S5_P:(s + 1) * S5_P] = rows[:, g * S5_P:(g + 1) * S5_P]


def _inproj(x, g, sc, sh, w5, wr, wt, wg, rope, rwp, grid_shift, tm):
    b, l, d = x.shape
    nt = l // tm
    hb = tm // GRID_W
    nhb = l // GRID_W
    tok = lambda n: pl.BlockSpec((1, tm, n), lambda i, j: (i, j, 0))
    vec = pl.BlockSpec((1, 1, d), lambda i, j: (i, 0, 0))
    full = lambda a: pl.BlockSpec(a.shape, lambda i, j: (0,) * a.ndim)
    grouped = pl.BlockSpec((1, S5_G, tm // S5_T, S5_TP), lambda i, j: (i, 0, j, 0))
    rowt = pl.BlockSpec((hb, 1, GROUP_W), lambda i, j: (j, 0, 0))
    above = pl.BlockSpec((1, GRID_W, d), lambda i, j: (i, jnp.maximum(j * hb - 1, 0), 0))
    below = pl.BlockSpec((1, GRID_W, d), lambda i, j: (i, jnp.minimum((j + 1) * hb, nhb - 1), 0))
    cr, sr, cc, sn = rope
    params = (rwp['mu'], rwp['w0'], rwp['wup'], rwp['a0'], rwp['aup'], rwp['gup'], rwp['kk'], rwp['ka'],
              rwp['rk'], rwp['bones'])
    rw_dtypes = (BF16,) * 5 + (F32, BF16, BF16) * 2
    outs = pl.pallas_call(
        functools.partial(_inproj_kernel, grid_shift, nt),
        grid=(b, nt),
        in_specs=[tok(d), above, below, pl.BlockSpec((1, d), lambda i, j: (0, 0)), vec, vec,
                  full(w5), full(wr), full(wt), full(wg), rowt, rowt, full(cc), full(sn)]
                 + [full(a) for a in params],
        out_specs=[tok(GROUP_W), grouped, tok(wt.shape[1]), tok(wg.shape[1])] + [tok(GROUP_W)] * 11,
        out_shape=[jax.ShapeDtypeStruct((b, l, GROUP_W), F32),
                   jax.ShapeDtypeStruct((b, S5_G, l // S5_T, S5_TP), F32),
                   jax.ShapeDtypeStruct((b, l, wt.shape[1]), F32),
                   jax.ShapeDtypeStruct((b, l, wg.shape[1]), F32)]
                  + [jax.ShapeDtypeStruct((b, l, GROUP_W), dt) for dt in rw_dtypes],
        scratch_shapes=[pltpu.VMEM((GROUP_W // LANES, tm, LANES), F32)],
        compiler_params=_cparams("parallel", "parallel"),
        name="norm_inproj",
    )(x, x, x, g, sc, sh, w5, wr, wt, wg, cr, sr, cc, sn, *params)
    return outs[0], outs[1], outs[2], outs[3], tuple(outs[4:])


def _s5_tables(lam_re, lam_im, log_dt, b_re, b_im, c_re, c_im, nlev):
    hp = lax.Precision.HIGHEST
    t = S5_T
    lam = lax.complex(jnp.minimum(lam_re.astype(F32), -1e-4), lam_im.astype(F32))
    ldt = lam * jnp.exp(log_dt.astype(F32))[..., None]
    a_bar = jnp.exp(ldt)
    bb = ((a_bar - 1.0) / lam)[..., None] * lax.complex(b_re.astype(F32), b_im.astype(F32))
    cm = lax.complex(c_re.astype(F32), c_im.astype(F32))
    tau = jnp.arange(t + 1, dtype=F32)
    apow = jnp.exp(ldt[:, :, None, :] * tau[None, None, :, None])
    taps = jnp.einsum('dgpn,dgtn,dgnq->dgtpq', cm, apow[:, :, :t], bb, precision=hp).real
    taprow = jnp.stack([taps[0], taps[1][:, ::-1]]).transpose(0, 1, 4, 2, 3).reshape(2, S5_G, S5_P, S5_TP)

    def pack(zc):
        return jnp.concatenate([zc.real, zc.imag], axis=-1)

    win_f = apow[0][:, t - 1 - jnp.arange(t), None, :] * bb[0].transpose(0, 2, 1)[:, None]
    win_b = apow[1][:, jnp.arange(t), None, :] * bb[1].transpose(0, 2, 1)[:, None]
    win = jnp.stack([pack(win_f), pack(win_b)]).reshape(2, S5_G, S5_TP, 2 * S5_N)
    ca_f = cm[0][:, None] * apow[0][:, 1 + jnp.arange(t), None, :]
    ca_b = cm[1][:, None] * apow[1][:, t - jnp.arange(t), None, :]

    def outpack(ca):
        w = jnp.concatenate([ca.real, -ca.imag], axis=-1)
        return w.reshape(S5_G, S5_TP, 2 * S5_N).transpose(0, 2, 1)

    wout = jnp.stack([outpack(ca_f), outpack(ca_b)])
    lev = (2.0 ** jnp.arange(nlev, dtype=F32)) * t
    pw = jnp.exp(ldt[:, :, None, :] * lev[None, None, :, None])
    p1 = jnp.concatenate([pw.real, pw.real], axis=-1)
    p2 = jnp.concatenate([-pw.imag, pw.imag], axis=-1)
    pw = jnp.stack([p1, p2], axis=3)
    return taprow, win.astype(BF16), wout.astype(BF16), pw


def _s5_kernel(nc, nlev, u_ref, tap_ref, win_ref, wout_ref, pw_ref, h0_ref, y_ref, hfin_ref, conv_ref):
    u = u_ref[0, 0].astype(BF16)
    row = lax.broadcasted_iota(jnp.int32, (nc, 2 * S5_N), 0)
    lane = lax.broadcasted_iota(jnp.int32, (S5_P, S5_TP), 1)
    for s in range(S5_T):
        lo = s * S5_P
        fwd = tap_ref[0, 0] if s == 0 else jnp.where(lane >= lo, pltpu.roll(tap_ref[0, 0], lo, 1), 0.0)
        hi = lo + S5_P
        bwd = tap_ref[1, 0] if hi == S5_TP else jnp.where(lane < hi, pltpu.roll(tap_ref[1, 0], hi, 1), 0.0)
        conv_ref[0, lo:hi, :] = fwd.astype(BF16)
        conv_ref[1, lo:hi, :] = bwd.astype(BF16)

    def cmul(x, d, j):
        return pw_ref[d, 0, j, 0:1] * x + pw_ref[d, 0, j, 1:2] * pltpu.roll(x, S5_N, 1)

    v = [jnp.dot(u, win_ref[d, 0], preferred_element_type=F32) for d in range(2)]
    x = [jnp.where(row == 0, h0_ref[0, 0, 0], pltpu.roll(v[0], 1, 0)),
         jnp.where(row == nc - 1, h0_ref[0, 1, 0], pltpu.roll(v[1], nc - 1, 0))]
    for j in range(nlev):
        sh = 2 ** j
        xs = [jnp.where(row >= sh, pltpu.roll(x[0], sh, 0), 0.0),
              jnp.where(row < nc - sh, pltpu.roll(x[1], nc - sh, 0), 0.0)]
        x = [x[d] + cmul(xs[d], d, j) for d in range(2)]
    for d, last in enumerate((nc - 1, 0)):
        hfin_ref[0, d, 0] = cmul(x[d][last:last + 1], d, 0) + v[d][last:last + 1]
    y_ref[0, 0] = (jnp.dot(u, conv_ref[0], preferred_element_type=F32) + _mm(x[0], wout_ref[0, 0])
                   + jnp.dot(u, conv_ref[1], preferred_element_type=F32) + _mm(x[1], wout_ref[1, 0]))


def _s5_scan(uf, tables, h0):
    conv, win, wout, pw = tables
    b, _, nc, _ = uf.shape
    nlev = max(1, (nc - 1).bit_length())
    pw = pw[:, :, :nlev]
    n2 = 2 * S5_N
    y, hfin = pl.pallas_call(
        functools.partial(_s5_kernel, nc, nlev),
        grid=(b, S5_G),
        in_specs=[pl.BlockSpec((1, 1, nc, S5_TP), lambda i, g: (i, g, 0, 0)),
                  pl.BlockSpec((2, 1, S5_P, S5_TP), lambda i, g: (0, g, 0, 0)),
                  pl.BlockSpec((2, 1, S5_TP, n2), lambda i, g: (0, g, 0, 0)),
                  pl.BlockSpec((2, 1, n2, S5_TP), lambda i, g: (0, g, 0, 0)),
                  pl.BlockSpec((2, 1, nlev, 2, n2), lambda i, g: (0, g, 0, 0, 0)),
                  pl.BlockSpec((1, 2, 1, 1, n2), lambda i, g: (i, 0, g, 0, 0))],
        out_specs=[pl.BlockSpec((1, 1, nc, S5_TP), lambda i, g: (i, g, 0, 0)),
                   pl.BlockSpec((1, 2, 1, 1, n2), lambda i, g: (i, 0, g, 0, 0))],
        out_shape=[jax.ShapeDtypeStruct((b, S5_G, nc, S5_TP), F32),
                   jax.ShapeDtypeStruct((b, 2, S5_G, 1, n2), F32)],
        scratch_shapes=[pltpu.VMEM((2, S5_TP, S5_TP), BF16)],
        compiler_params=_cparams("parallel", "parallel"),
        name="s5_scan",
    )(uf, conv, win, wout, pw, h0)
    return y, hfin


def _rw_scan_kernel(nb, cpb, rf_ref, vf_ref, kkf_ref, lwf_ref, kdf_ref, asf_ref,
                    rb_ref, vb_ref, kkb_ref, lwb_ref, kdb_ref, asb_ref, s0_ref,
                    yf_ref, yb_ref, sfin_ref, st_ref):
    c = pl.program_id(1)

    @pl.when(c == 0)
    def _():
        st_ref[...] = s0_ref[0]

    t = CHUNK
    ri = lax.broadcasted_iota(jnp.int32, (t, LANES), 0)
    li = lax.broadcasted_iota(jnp.int32, (t, LANES), 1)
    ci = li & (t - 1)
    low = li < HEAD_V
    eyef = jnp.where(ri == ci, 1.0, 0.0)
    same = lambda s: jnp.where((ri >> s) == (ci >> s), 1.0, 0.0)
    m4, m8, m16, m32 = same(2), same(3), same(4), same(5)
    merge_masks = (m8 - m4, m16 - m8, m32 - m16, 1.0 - m32)

    def bd(x):
        return jnp.concatenate([jnp.where(low, x, 0.0), jnp.where(low, 0.0, x)], axis=0)

    def diag_blocks(full):
        return jnp.where(low, full[0:HEAD_V], full[HEAD_V:2 * HEAD_V])

    dir_refs = ((rf_ref, vf_ref, kkf_ref, lwf_ref, kdf_ref, asf_ref),
                (rb_ref, vb_ref, kkb_ref, lwb_ref, kdb_ref, asb_ref))
    units = []
    for d, (r_ref, v_ref, kk_ref, lw_ref, kd_ref, as_ref) in enumerate(dir_refs):
        incl = (ci >= ri) if d == 1 else (ci <= ri)
        strict = (ci > ri) if d == 1 else (ci < ri)
        r, v, kk, kd = (x[0].astype(F32) for x in (r_ref, v_ref, kk_ref, kd_ref))
        lw = lw_ref[0]
        cin = _mm_left01(_chunk_tri01(cpb * t, t, d == 1), lw)
        e_in = jnp.exp(cin)
        e_neg = jnp.exp(-cin)
        rt = r * e_in
        at = -kk * jnp.exp(cin - lw)
        bvec = kk * as_ref[0].astype(F32)
        bt = bvec * e_neg
        kt = kd * e_neg
        for j in range(cpb):
            rows = slice(j * t, (j + 1) * t)
            last = j * t + (0 if d == 1 else t - 1)
            clast = cin[last:last + 1]
            dl = jnp.exp(clast - cin[rows])
            bh = bvec[rows] * dl
            kh = kd[rows] * dl
            dec = jnp.exp(clast)
            for p in range(HEADS // 2):
                sl = slice(p * LANES, (p + 1) * LANES)
                units.append(dict(d=d, j=j, p=p, incl=incl, strict=strict,
                                  at=at[rows, sl], rt=rt[rows, sl], bt=bt[rows, sl], kt=kt[rows, sl],
                                  bh=bh[:, sl], kh=kh[:, sl], v=v[rows, sl], dec=dec[:, sl]))

    x1 = [jnp.concatenate([u['at'], u['rt']], axis=0) for u in units]
    ab_rb = [_mm_nt(x, bd(u['bt'])) for x, u in zip(x1, units)]
    ak_rk = [_mm_nt(x, bd(u['kt'])) for x, u in zip(x1, units)]
    nmat = [jnp.where(u['strict'], a[0:t], 0.0) for u, a in zip(units, ab_rb)]
    a_rb = [jnp.where(u['incl'], a[t:2 * t], 0.0) for u, a in zip(units, ab_rb)]
    a_kk = [jnp.concatenate([jnp.where(u['strict'], a[0:t], 0.0), jnp.where(u['incl'], a[t:2 * t], 0.0)], axis=0)
            for u, a in zip(units, ak_rk)]
    akv = [_mm(a, bd(u['v'])) for u, a in zip(units, a_kk)]
    kv = [diag_blocks(_mm_tn(u['v'], u['kh'])) for u in units]
    nd = [x * m4 for x in nmat]
    n2 = [_mm(x, bd(x)) for x in nd]
    tinv = [eyef + x + _mm(eyef + x, bd(y)) for x, y in zip(nd, n2)]
    for mk in merge_masks:
        w = [_mm(ti, bd(x * mk)) for ti, x in zip(tinv, nmat)]
        tinv = [ti + _mm(wi, bd(ti)) for ti, wi in zip(tinv, w)]
    zz = [_mm(ti, jnp.concatenate([bd(u['at']), bd(kvv[0:t])], axis=1))
          for ti, u, kvv in zip(tinv, units, akv)]
    ght = [_mm_tn(z, u['bh']) for u, z in zip(units, zz)]
    qy = [_mm(a, jnp.concatenate([bd(z[:, :LANES]), bd(z[:, LANES:])], axis=1))
          for a, z in zip(a_rb, zz)]
    qmat = [u['rt'] + x[:, :LANES] for u, x in zip(units, qy)]
    y0 = [x[:, LANES:] + kvv[t:2 * t] for x, kvv in zip(qy, akv)]
    gmat = [diag_blocks(x[:LANES]) for x in ght]
    hmat = [diag_blocks(x[LANES:]) + k2 for x, k2 in zip(ght, kv)]
    idx = {(u['d'], u['j'], u['p']): n for n, u in enumerate(units)}
    npair = HEADS // 2
    state = {(d, p): st_ref[d, p] for d in range(2) for p in range(npair)}
    ys = {}
    for step in range(cpb):
        for d in range(2):
            j = step if d == 0 else cpb - 1 - step
            for p in range(npair):
                n = idx[(d, j, p)]
                st = state[(d, p)]
                ys[(d, j, p)] = _mm_nt(qmat[n], bd(st)) + y0[n]
                state[(d, p)] = units[n]['dec'] * st + _mm(st, bd(gmat[n])) + hmat[n]
    for d, y_ref in enumerate((yf_ref, yb_ref)):
        y_ref[0] = jnp.concatenate(
            [jnp.concatenate([ys[(d, j, p)] for p in range(npair)], axis=1) for j in range(cpb)], axis=0)
        for p in range(npair):
            st_ref[d, p] = state[(d, p)]

    @pl.when(c == nb - 1)
    def _():
        sfin_ref[0] = st_ref[...]


def _rw_scan(pre, s0):
    r, v, kk, _, _, lw0, kd0, as0, lw1, kd1, as1 = pre
    b, l, _ = r.shape
    cpb = min(RW_CPB, l // CHUNK)
    nb = l // (cpb * CHUNK)
    fw = pl.BlockSpec((1, cpb * CHUNK, GROUP_W), lambda i, c: (i, c, 0))
    bw = pl.BlockSpec((1, cpb * CHUNK, GROUP_W), lambda i, c: (i, nb - 1 - c, 0))
    st = pl.BlockSpec((1, 2, HEADS // 2, HEAD_V, LANES), lambda i, c: (i, 0, 0, 0, 0))
    return pl.pallas_call(
        functools.partial(_rw_scan_kernel, nb, cpb),
        grid=(b, nb),
        in_specs=[fw] * 6 + [bw] * 6 + [st],
        out_specs=[fw, bw, st],
        out_shape=[jax.ShapeDtypeStruct((b, l, GROUP_W), F32)] * 2
                  + [jax.ShapeDtypeStruct((b, 2, HEADS // 2, HEAD_V, LANES), F32)],
        scratch_shapes=[pltpu.VMEM((2, HEADS // 2, HEAD_V, LANES), F32)],
        compiler_params=_cparams("parallel", "arbitrary"),
        name="rwkv_scan",
    )(r, v, kk, lw0, kd0, as0, r, v, kk, lw1, kd1, as1, s0)


def _rope(x, cos, sin_signed):
    lane = lax.broadcasted_iota(jnp.int32, x.shape, 1)
    n = x.shape[1]
    swapped = jnp.where((lane & 16) == 0, pltpu.roll(x, n - 16, 1), pltpu.roll(x, 16, 1))
    return x * cos + swapped * sin_signed


def _ret_scan_kernel(nb, cpb, qf_ref, kf_ref, vf_ref, qb_ref, kb_ref, vb_ref,
                     dmat_ref, qdec_ref, kdec_ref, sdec_ref, s0_ref,
                     of_ref, ob_ref, sfin_ref, st_ref):
    c = pl.program_id(1)

    @pl.when(c == 0)
    def _():
        st_ref[...] = s0_ref[0]

    t = dmat_ref.shape[-1]
    units = []
    for d, (q_ref, k_ref, v_ref) in enumerate(((qf_ref, kf_ref, vf_ref), (qb_ref, kb_ref, vb_ref))):
        for j in range(cpb):
            rows = slice(j * t, (j + 1) * t)
            q, k, v = q_ref[0, rows], k_ref[0, rows], v_ref[0, rows]
            qd = q * qdec_ref[d]
            kh = k * kdec_ref[d]
            for h in range(HEADS):
                sl = slice(h * HEAD_V, (h + 1) * HEAD_V)
                units.append(dict(d=d, j=j, h=h, q=q[:, sl], k=k[:, sl], qd=qd[:, sl], kh=kh[:, sl],
                                  v=v[:, sl], dec=sdec_ref[d][:, sl]))
    scores = [_mm_nt(u['q'], u['k']).astype(BF16) * dmat_ref[u['d'], u['h']] for u in units]
    intra = [_mm(p, u['v']) for p, u in zip(scores, units)]
    kv = [_mm_tn(u['v'], u['kh']) for u in units]
    idx = {(u['d'], u['j'], u['h']): i for i, u in enumerate(units)}
    entering = {}
    for d in range(2):
        for h in range(HEADS):
            st = st_ref[d, h]
            for step in range(cpb):
                j = step if d == 0 else cpb - 1 - step
                i = idx[(d, j, h)]
                entering[i] = st
                st = units[i]['dec'] * st + kv[i]
            st_ref[d, h] = st
    outs = [x + _mm_nt(u['qd'], entering[i]) for i, (x, u) in enumerate(zip(intra, units))]
    for d, o_ref in enumerate((of_ref, ob_ref)):
        o_ref[0] = jnp.concatenate(
            [jnp.concatenate([outs[idx[(d, j, h)]] for h in range(HEADS)], axis=1) for j in range(cpb)], axis=0)

    @pl.when(c == nb - 1)
    def _():
        sfin_ref[0] = st_ref[...]


def _ret_tables(decay_logit, n):
    lg = jax.nn.log_sigmoid(decay_logit.astype(F32))
    pos = jnp.arange(n, dtype=F32)
    lag = pos[:, None] - pos[None, :]
    lag = jnp.stack([lag, -lag])
    dmat = jnp.where(lag[:, None] >= 0, jnp.exp(lg[:, :, None, None] * lag[:, None]), 0.0)
    lanes = jnp.repeat(lg, RET_DK, axis=-1)[:, None, :]
    qpow = jnp.stack([pos + 1.0, n - pos])[:, :, None]
    kpow = jnp.stack([n - 1.0 - pos, pos])[:, :, None]
    return dmat.astype(BF16), jnp.exp(lanes * qpow), jnp.exp(lanes * kpow), jnp.exp(lanes * n)


def _gla_scan_kernel(nb, cpb, qf_ref, kf_ref, vf_ref, af_ref, qb_ref, kb_ref, vb_ref, ab_ref,
                     aup_ref, abias_ref, s0_ref, of_ref, ob_ref, sfin_ref, st_ref):
    c = pl.program_id(1)

    @pl.when(c == 0)
    def _():
        st_ref[...] = s0_ref[0]

    t = GLA_T
    n = cpb * t
    dk = GLA_DK
    scale = dk ** -0.5
    dirs = ((qf_ref, kf_ref, vf_ref, af_ref), (qb_ref, kb_ref, vb_ref, ab_ref))
    units = []
    for d, (q_ref, k_ref, v_ref, a_ref) in enumerate(dirs):
        incl = _tri_masks(t, d == 1)[0]
        q, k, v = q_ref[0], k_ref[0] * scale, v_ref[0]
        lw = -_softplus(-(_mm(a_ref[0], aup_ref[d]) + abias_ref[d])) * (1.0 / GLA_TAU)
        cin = _mm_left01(_chunk_tri01(n, t, d == 1), lw)
        qe = q * jnp.exp(cin)
        for j in range(cpb):
            rows = slice(j * t, (j + 1) * t)
            last = j * t + (0 if d == 1 else t - 1)
            mid = j * t + (t // 2 if d == 1 else t // 2 - 1)
            cj = cin[rows]
            clast = cin[last:last + 1]
            cmid = cin[mid:mid + 1]
            qt = q[rows] * jnp.exp(cj - cmid)
            kt = k[rows] * jnp.exp(cmid - cj)
            kh = k[rows] * jnp.exp(clast - cj)
            dec = jnp.exp(clast)
            for h in range(HEADS):
                sk = slice(h * dk, (h + 1) * dk)
                sv = slice(h * HEAD_V, (h + 1) * HEAD_V)
                units.append(dict(d=d, j=j, h=h, incl=incl, qt=qt[:, sk], kt=kt[:, sk], qe=qe[rows, sk],
                                  kh=kh[:, sk], v=v[rows, sv], dec=dec[:, sk]))

    amat = [jnp.where(u['incl'], _mm_nt(u['qt'], u['kt']), 0.0) for u in units]
    intra = [_mm(a, u['v']) for a, u in zip(amat, units)]
    kv = [_mm_tn(u['v'], u['kh']) for u in units]
    idx = {(u['d'], u['j'], u['h']): i for i, u in enumerate(units)}
    entering = {}
    for d in range(2):
        for h in range(HEADS):
            st = st_ref[d, h]
            for step in range(cpb):
                j = step if d == 0 else cpb - 1 - step
                i = idx[(d, j, h)]
                entering[i] = st
                st = units[i]['dec'] * st + kv[i]
            st_ref[d, h] = st
    outs = [x + _mm_nt(u['qe'], entering[i]) for i, (x, u) in enumerate(zip(intra, units))]
    for d, o_ref in enumerate((of_ref, ob_ref)):
        o_ref[0] = jnp.concatenate(
            [jnp.concatenate([outs[idx[(d, j, h)]] for h in range(HEADS)], axis=1) for j in range(cpb)], axis=0)

    @pl.when(c == nb - 1)
    def _():
        sfin_ref[0] = st_ref[...]


def _rope_tables(rows):
    nf = RET_DK // 4
    inv = ROPE_BASE ** (-jnp.arange(nf, dtype=F32) / nf)
    lane = jnp.arange(GROUP_W)
    freq = inv[lane % nf]
    sign = jnp.where((lane & 16) == 0, -1.0, 1.0)
    ar = jnp.arange(rows, dtype=F32)[:, None] * freq[None, :]
    ac = jnp.arange(GRID_W, dtype=F32)[:, None] * freq[None, :]
    return (jnp.cos(ar).reshape(rows, 1, GROUP_W), (jnp.sin(ar) * sign).reshape(rows, 1, GROUP_W),
            jnp.cos(ac), jnp.sin(ac) * sign)


def _ret_scan(z, tables, s0):
    b, l, _ = z.shape
    n = min(RET_BLOCK, l)
    nb = l // n
    fw = lambda j: pl.BlockSpec((1, n, GROUP_W), lambda i, c: (i, c, j))
    bw = lambda j: pl.BlockSpec((1, n, GROUP_W), lambda i, c: (i, nb - 1 - c, j))
    full = lambda a: pl.BlockSpec(a.shape, lambda i, c: (0,) * a.ndim)
    st = pl.BlockSpec((1, 2, HEADS, HEAD_V, RET_DK), lambda i, c: (i, 0, 0, 0, 0))
    return pl.pallas_call(
        functools.partial(_ret_scan_kernel, nb, n // RET_T),
        grid=(b, nb),
        in_specs=[fw(0), fw(1), fw(2), bw(0), bw(1), bw(2)] + [full(a) for a in tables] + [st],
        out_specs=[fw(0), bw(0), st],
        out_shape=[jax.ShapeDtypeStruct((b, l, GROUP_W), F32)] * 2
                  + [jax.ShapeDtypeStruct((b, 2, HEADS, HEAD_V, RET_DK), F32)],
        scratch_shapes=[pltpu.VMEM((2, HEADS, HEAD_V, RET_DK), F32)],
        compiler_params=_cparams("parallel", "arbitrary"),
        name="retention_scan",
    )(z, z, z, z, z, z, *tables, s0)


def _gla_scan(z, aup, abias, s0):
    b, l, _ = z.shape
    cpb = min(GLA_BLOCK, l) // CHUNK
    nb = l // (cpb * CHUNK)
    blk = lambda w, j, rev: pl.BlockSpec(
        (1, cpb * CHUNK, w), (lambda i, c: (i, nb - 1 - c, j)) if rev else (lambda i, c: (i, c, j)))
    st = pl.BlockSpec((1, 2, HEADS, HEAD_V, GLA_DK), lambda i, c: (i, 0, 0, 0, 0))
    ofw = pl.BlockSpec((1, cpb * CHUNK, GROUP_W), lambda i, c: (i, c, 0))
    obw = pl.BlockSpec((1, cpb * CHUNK, GROUP_W), lambda i, c: (i, nb - 1 - c, 0))
    return pl.pallas_call(
        functools.partial(_gla_scan_kernel, nb, cpb * CHUNK // GLA_T),
        grid=(b, nb),
        in_specs=[blk(GLA_QK, 0, False), blk(GLA_QK, 1, False), blk(GROUP_W, 1, False), blk(128, 6, False),
                  blk(GLA_QK, 0, True), blk(GLA_QK, 1, True), blk(GROUP_W, 1, True), blk(128, 6, True),
                  pl.BlockSpec(aup.shape, lambda i, c: (0, 0, 0)),
                  pl.BlockSpec(abias.shape, lambda i, c: (0, 0, 0)), st],
        out_specs=[ofw, obw, st],
        out_shape=[jax.ShapeDtypeStruct((b, l, GROUP_W), F32)] * 2
                  + [jax.ShapeDtypeStruct((b, 2, HEADS, HEAD_V, GLA_DK), F32)],
        scratch_shapes=[pltpu.VMEM((2, HEADS, HEAD_V, GLA_DK), F32)],
        compiler_params=_cparams("parallel", "arbitrary"),
        name="gla_scan",
    )(z, z, z, z, z, z, z, z, aup, abias, s0)


def _mix_kernel(x_ref, g1_ref, y5_ref, u5_ref, d5_ref, gw_ref, gb_ref,
                ryf_ref, ryb_ref, rbonus_ref, rg_ref, rlng_ref, rlnb_ref,
                tof_ref, tob_ref, tg_ref, tln_ref,
                gof_ref, gob_ref, gg_ref, gln_ref,
                bones_ref, wo_ref, o_ref, y5t_ref):
    bones = bones_ref[...]
    inv = 1.0 / HEAD_V

    def hmean(a):
        return _mm_right01(a, bones) * inv

    nch = y5_ref.shape[2]
    gph = LANES // S5_P
    y5g = [y5_ref[0, g] for g in range(S5_G)]
    for s in range(S5_T):
        for hf in range(GROUP_W // LANES):
            y5t_ref[hf, pl.ds(s, nch, stride=S5_T), :] = jnp.concatenate(
                [yg[:, s * S5_P:(s + 1) * S5_P] for yg in y5g[hf * gph:(hf + 1) * gph]], axis=1)
    y = jnp.concatenate([y5t_ref[0], y5t_ref[1]], axis=1) + d5_ref[...] * u5_ref[0]
    y = _gelu_tanh(y)
    ya = y * _sigmoid(_mm(y, gw_ref[...]) + gb_ref[...])
    yr = ryf_ref[0] + ryb_ref[0]
    dlt = yr - hmean(yr)
    yn = dlt * lax.rsqrt(hmean(dlt * dlt) + GN_EPS)
    yb = (yn * rlng_ref[...] + rlnb_ref[...] + rbonus_ref[0].astype(F32)) * rg_ref[0].astype(F32)
    ot = tof_ref[0] + tob_ref[0]
    yc = ot * lax.rsqrt(hmean(ot * ot) + EPS) * tln_ref[...] * _silu(tg_ref[0])
    og = gof_ref[0] + gob_ref[0]
    yd = og * lax.rsqrt(hmean(og * og) + EPS) * gln_ref[...] * _silu(gg_ref[0])
    mix = (_mm(ya, wo_ref[0]) + _mm(yb, wo_ref[1]) + _mm(yc, wo_ref[2]) + _mm(yd, wo_ref[3]))
    o_ref[0] = x_ref[0] + g1_ref[0] * mix


def _mix(x, g1, y5, u5, rw_pre, rw_y, zret, ret_o, zgla, gla_o, p, tm):
    b, l, d = x.shape
    tok = lambda w, j: pl.BlockSpec((1, tm, w), lambda i, t: (i, t, j))
    vec = lambda a: pl.BlockSpec(a.shape, lambda i, t: (0,) * a.ndim)
    t256 = tok(GROUP_W, 0)
    args = [x, g1, y5, u5, p['s5_d'], p['glu_w'], p['glu_b'],
            rw_y[0], rw_y[1], rw_pre[4], rw_pre[3], p['rw_ln_g'], p['rw_ln_b'],
            ret_o[0], ret_o[1], zret, p['ret_ln_g'],
            gla_o[0], gla_o[1], zgla, p['gla_ln_g'],
            p['bones'], p['w_out']]
    grouped = pl.BlockSpec((1, S5_G, tm // S5_T, S5_TP), lambda i, t: (i, 0, t, 0))
    specs = [tok(d, 0), pl.BlockSpec((1, 1, d), lambda i, t: (i, 0, 0)), grouped, t256,
             vec(p['s5_d']), vec(p['glu_w']), vec(p['glu_b']),
             t256, t256, t256, t256, vec(p['rw_ln_g']), vec(p['rw_ln_b']),
             t256, t256, tok(GROUP_W, 3), vec(p['ret_ln_g']),
             t256, t256, tok(GROUP_W, 2), vec(p['gla_ln_g']),
             vec(p['bones']), vec(p['w_out'])]
    return pl.pallas_call(
        _mix_kernel,
        grid=(b, l // tm),
        in_specs=specs,
        out_specs=tok(d, 0),
        out_shape=jax.ShapeDtypeStruct((b, l, d), F32),
        scratch_shapes=[pltpu.VMEM((GROUP_W // LANES, tm, LANES), F32)],
        compiler_params=_cparams("parallel", "parallel"),
        name="mix_outproj",
    )(*args)


def _mlp_kernel(final, nff, x_ref, g_ref, sc_ref, sh_ref, gate_ref, w1_ref, w2_ref, fg_ref, o_ref):
    x = x_ref[0]
    hb = _modnorm(x, g_ref[...], sc_ref[0], sh_ref[0]).astype(BF16)
    ff = w1_ref.shape[2] // nff
    acc = None
    for j in range(nff):
        a = jnp.maximum(jnp.dot(hb, w1_ref[0, :, j * ff:(j + 1) * ff], preferred_element_type=F32), 0.0)
        part = jnp.dot((a * a).astype(BF16), w2_ref[0, j * ff:(j + 1) * ff, :], preferred_element_type=F32)
        acc = part if acc is None else acc + part
    y = x + gate_ref[0] * acc
    if final:
        ms = jnp.mean(y * y, axis=-1, keepdims=True)
        y = y * lax.rsqrt(ms + EPS) * fg_ref[...]
    o_ref[0] = y


def _mlp(x, g, sc, sh, gate, w1, w2, layer, final_g, final, tm):
    b, l, d = x.shape
    tok = pl.BlockSpec((1, tm, d), lambda i, t: (i, t, 0))
    vec = pl.BlockSpec((1, 1, d), lambda i, t: (i, 0, 0))
    row = pl.BlockSpec((1, d), lambda i, t: (0, 0))
    once = lambda a: pl.BlockSpec((1,) + a.shape[1:], lambda i, t: (layer, 0, 0), pipeline_mode=pl.Buffered(1))
    return pl.pallas_call(
        functools.partial(_mlp_kernel, final, 4),
        grid=(b, l // tm),
        in_specs=[tok, row, vec, vec, vec, once(w1), once(w2), row],
        out_specs=tok,
        out_shape=jax.ShapeDtypeStruct((b, l, d), F32),
        compiler_params=_cparams("parallel", "parallel"),
        name="mlp",
    )(x, g, sc, sh, gate, w1, w2, final_g)


def kernel(x, c, ctx, c_ctx, ada_w, ada_b, norm1_g, norm2_g, w_in, w_out, s5_lam_re, s5_lam_im, s5_log_dt, s5_b_re, s5_b_im, s5_c_re, s5_c_im, s5_d, s5_glu_w, s5_glu_b, rw_mu, rw_w0, rw_w_up, rw_a0, rw_a_up, rw_g_up, rw_k_k, rw_k_a, rw_r_k, rw_ln_g, rw_ln_b, ret_decay_logit, ret_ln_g, gla_a_up, gla_a_b, gla_ln_g, mlp_w1, mlp_w2, final_g):
    b, l, d = x.shape
    lc = ctx.shape[1]
    depth = ada_w.shape[0]
    rows = l // GRID_W
    assert l % 512 == 0 and lc % CHUNK == 0 and lc % S5_T == 0 and d % 128 == 0

    cond = jnp.zeros((8, d), F32).at[:b].set(c).at[b].set(c_ctx)
    mod = _modulation(cond, ada_w, ada_b)

    lane = jnp.arange(GROUP_W)
    bones = (lane[:, None] // HEAD_V == lane[None, :] // HEAD_V).astype(BF16)
    rope_lat = _rope_tables(rows)
    ident = (jnp.ones((lc // CHUNK, 1, GROUP_W), F32), jnp.zeros((lc // CHUNK, 1, GROUP_W), F32),
             jnp.ones((GRID_W, GROUP_W), F32), jnp.zeros((GRID_W, GROUP_W), F32))
    nlev = max(1, (l // S5_T - 1).bit_length())
    c0, c1, c2 = GROUP_W, GROUP_W + RW_COLS, GROUP_W + RW_COLS + 4 * GROUP_W
    row2 = lambda a: a.reshape(1, -1).astype(F32)
    w_in_b, w_out_b, w1_b, w2_b = _to_bf16(w_in), _to_bf16(w_out), _to_bf16(mlp_w1), _to_bf16(mlp_w2)

    xc = ctx
    for i in range(depth):
        last = i == depth - 1
        m = mod[i].reshape(8, N_ADA, d)
        ml = m[:b, :, None, :]
        mc = jnp.broadcast_to(m[b][None, :, None, :], (b, N_ADA, 1, d))
        wi = w_in_b[i]
        w5, wr, wt = wi[:, :c0], wi[:, c0:c1], wi[:, c1:c2]
        wg = jnp.pad(wi[:, c2:], ((0, 0), (0, GLA_COLS_PAD - (wi.shape[1] - c2))))
        n1 = row2(norm1_g[i])
        n2 = row2(norm2_g[i])

        s5_tab = _s5_tables(s5_lam_re[i], s5_lam_im[i], s5_log_dt[i], s5_b_re[i], s5_b_im[i],
                            s5_c_re[i], s5_c_im[i], nlev)
        zpad = lambda a, lo, n: jnp.zeros((2, 128, n), F32).at[:, lo:lo + a.shape[1]].set(a).astype(BF16)
        rwp = dict(
            mu=row2(rw_mu[i]),
            w0=rw_w0[i].reshape(2, 1, GROUP_W), a0=rw_a0[i].reshape(2, 1, GROUP_W),
            wup=zpad(rw_w_up[i], 0, GROUP_W), aup=zpad(rw_a_up[i], RW_W_RANK, GROUP_W),
            gup=jnp.zeros((128, GROUP_W), F32).at[RW_W_RANK + RW_A_RANK:].set(rw_g_up[i]).astype(BF16),
            kk=row2(rw_k_k[i]), ka=row2(rw_k_a[i]), rk=row2(rw_r_k[i]), bones=bones)
        ret_tab = _ret_tables(ret_decay_logit[i], RET_T)
        gla_aup = zpad(gla_a_up[i], 0, GLA_QK)
        gla_ab = gla_a_b[i].reshape(2, 1, GLA_QK).astype(F32)
        mixp = dict(s5_d=row2(s5_d[i]), glu_w=s5_glu_w[i].astype(BF16), glu_b=row2(s5_glu_b[i]),
                    rw_ln_g=row2(rw_ln_g[i]), rw_ln_b=row2(rw_ln_b[i]), ret_ln_g=row2(ret_ln_g[i]),
                    gla_ln_g=row2(gla_ln_g[i]), bones=bones,
                    w_out=w_out_b[i].reshape(4, GROUP_W, d))

        def mixers(xx, mm, is_lat, states):
            tm = 512 if is_lat else lc
            z5, z5g, zt, zg, pre = _inproj(xx, n1, mm[:, 1], mm[:, 0], w5, wr, wt, wg,
                                           rope_lat if is_lat else ident, rwp, is_lat, tm)
            y5, h5 = _s5_scan(z5g, s5_tab, states[0])
            yrf, yrb, srw = _rw_scan(pre, states[1])
            otf, otb, sret = _ret_scan(zt, ret_tab, states[2])
            ogf, ogb, sgla = _gla_scan(zg, gla_aup, gla_ab, states[3])
            outs = (z5, y5, pre, (yrf, yrb), zt, (otf, otb), zg, (ogf, ogb))
            return outs, (h5, srw, sret, sgla)

        def block(xx, mm, outs, is_lat, fin):
            tm = 512 if is_lat else lc
            z5, y5, pre, yr, zt, ot, zg, og = outs
            x1 = _mix(xx, mm[:, 2], y5, z5, pre, yr, zt, ot, zg, og, mixp, tm)
            return _mlp(x1, n2, mm[:, 4], mm[:, 3], mm[:, 5], w1_b, w2_b, i, row2(final_g), fin, tm)

        zeros = (jnp.zeros((b, 2, S5_G, 1, 2 * S5_N), F32),
                 jnp.zeros((b, 2, HEADS // 2, HEAD_V, LANES), F32),
                 jnp.zeros((b, 2, HEADS, HEAD_V, RET_DK), F32),
                 jnp.zeros((b, 2, HEADS, HEAD_V, GLA_DK), F32))
        outs_c, st_c = mixers(xc, mc, False, zeros)
        outs_l, _ = mixers(x, ml, True, st_c)
        x = block(x, ml, outs_l, True, last)
        if not last:
            xc = block(xc, mc, outs_c, False, False)
    return x
```

```python
import functools
import math

import jax
import jax.numpy as jnp
from jax import lax
from jax.experimental import pallas as pl
from jax.experimental.pallas import tpu as pltpu

F32 = jnp.float32
BF16 = jnp.bfloat16

LANES = 128
GRID_W = 64
GROUP_W = 256
N_ADA = 6
EPS = 1e-6
GN_EPS = 64e-5
CHUNK = 64
RW_CPB = 4
RET_T = 256
RET_BLOCK = 512
GLA_T = 128
GLA_BLOCK = 512
HEADS = 4
HEAD_V = 64
S5_P = 16
S5_G = 16
S5_N = 64
S5_T = 32
S5_TP = S5_T * S5_P
RW_COLS = 896
RW_W_RANK = 32
RW_A_RANK = 32
RW_G_RANK = 64
RET_DK = 64
GLA_DK = 32
GLA_QK = 128
GLA_RANK = 16
GLA_TAU = 16.0
GLA_COLS_PAD = 896
ROPE_BASE = 10000.0
VMEM_LIMIT = 56 * 1024 * 1024


def _cparams(*sem):
    return pltpu.CompilerParams(dimension_semantics=sem, vmem_limit_bytes=VMEM_LIMIT)


def _mm(a, b):
    return jnp.dot(a.astype(BF16), b.astype(BF16), preferred_element_type=F32)


def _mm_nt(a, b):
    return lax.dot_general(a.astype(BF16), b.astype(BF16), (((1,), (1,)), ((), ())),
                           preferred_element_type=F32)


def _mm_tn(a, b):
    return lax.dot_general(a.astype(BF16), b.astype(BF16), (((0,), (0,)), ((), ())),
                           preferred_element_type=F32)


def _split2(x):
    hi = x.astype(BF16)
    return hi, (x - hi.astype(F32)).astype(BF16)


def _join2(p):
    return p[0].astype(F32) + p[1].astype(F32)


def _mm_x3p(a, b):
    return (jnp.dot(a[0], b[0], preferred_element_type=F32) + jnp.dot(a[0], b[1], preferred_element_type=F32)
            + jnp.dot(a[1], b[0], preferred_element_type=F32))


def _chunk_tri01(n, t, reverse):
    ri = lax.broadcasted_iota(jnp.int32, (n, n), 0)
    ci = lax.broadcasted_iota(jnp.int32, (n, n), 1)
    shift = t.bit_length() - 1
    tri = (ci >= ri) if reverse else (ci <= ri)
    return jnp.where((ri >> shift) == (ci >> shift), jnp.where(tri, 1.0, 0.0), 0.0).astype(BF16)


def _mm_left01(m01, x):
    hi, lo = _split2(x)
    return jnp.dot(m01, hi, preferred_element_type=F32) + jnp.dot(m01, lo, preferred_element_type=F32)


def _mm_right01(x, m01):
    return jnp.dot(x.astype(BF16), m01, preferred_element_type=F32)


def _sigmoid(x):
    return 1.0 / (1.0 + jnp.exp(-x))


def _softplus(x):
    return jnp.maximum(x, 0.0) + jnp.log(1.0 + jnp.exp(-jnp.abs(x)))


def _silu(x):
    return x * _sigmoid(x)


def _gelu_tanh(x):
    return 0.5 * x * (1.0 + jnp.tanh(math.sqrt(2.0 / math.pi) * (x + 0.044715 * x * x * x)))


def _tri_masks(t, reverse):
    ri = lax.broadcasted_iota(jnp.int32, (t, t), 0)
    ci = lax.broadcasted_iota(jnp.int32, (t, t), 1)
    if reverse:
        return ci >= ri, ci > ri, ci == ri
    return ci <= ri, ci < ri, ci == ri


def _row_to_col(row, eye):
    n = row.shape[1]
    return jnp.sum(jnp.where(eye, jnp.broadcast_to(row, (n, n)), 0.0), axis=1, keepdims=True)


def _cast_kernel(x_ref, o_ref):
    o_ref[...] = x_ref[...].astype(o_ref.dtype)


def _to_bf16(w):
    depth, r, c = w.shape
    tr = 256
    spec = pl.BlockSpec((1, tr, c), lambda i, j: (i, j, 0))
    return pl.pallas_call(
        _cast_kernel,
        grid=(depth, r // tr),
        in_specs=[spec],
        out_specs=spec,
        out_shape=jax.ShapeDtypeStruct(w.shape, BF16),
        compiler_params=_cparams("parallel", "parallel"),
        name="cast_bf16",
    )(w)


def _mod_kernel(cond_ref, w_ref, b_ref, o_ref):
    c = cond_ref[...]
    o_ref[0] = _mm(_silu(c), w_ref[0]) + b_ref[0]


def _modulation(cond, ada_w, ada_b):
    depth, d, n = ada_w.shape
    tn = 1536
    return pl.pallas_call(
        _mod_kernel,
        grid=(depth, n // tn),
        in_specs=[pl.BlockSpec((8, d), lambda i, j: (0, 0)),
                  pl.BlockSpec((1, d, tn), lambda i, j: (i, 0, j)),
                  pl.BlockSpec((1, 1, tn), lambda i, j: (i, 0, j))],
        out_specs=pl.BlockSpec((1, 8, tn), lambda i, j: (i, 0, j)),
        out_shape=jax.ShapeDtypeStruct((depth, 8, n), F32),
        compiler_params=_cparams("parallel", "parallel"),
        name="adaln_mod",
    )(cond, ada_w, ada_b.reshape(depth, 1, n))


def _modnorm(x, g, sc, sh):
    ms = jnp.mean(x * x, axis=-1, keepdims=True)
    return x * lax.rsqrt(ms + EPS) * g * (1.0 + sc) + sh


def _shift_columns(z, segments):
    tm, width = z.shape
    lane = lax.broadcasted_iota(jnp.int32, (tm, LANES), 1)
    starts = [s for s, _ in segments] + [width]
    cols = []
    for c in range(width // LANES):
        lo, hi = c * LANES, (c + 1) * LANES
        inside = [(max(starts[i], lo), segments[i][1]) for i in range(len(segments))
                  if starts[i] < hi and starts[i + 1] > lo]
        col = inside[-1][1](c)
        for first, fn in reversed(inside[:-1]):
            nxt = [f for f, _ in inside if f > first][0]
            col = jnp.where(lane < nxt - lo, fn(c), col)
        cols.append(col)
    return jnp.concatenate(cols, axis=1)


def _rw_prep_math(grid_shift, z, zp, zn, mu, w0_ref, wup_ref, a0_ref, aup_ref, gup, kkp, ka, rk, bones):
    tm = z.shape[0]
    row = lax.broadcasted_iota(jnp.int32, (tm, LANES), 0)
    col = lambda x, c: x[:, c * LANES:(c + 1) * LANES]
    if grid_shift:
        pos = row & (GRID_W - 1)
        left = lambda c: jnp.where(pos == 0, 0.0, pltpu.roll(col(z, c), 1, 0))
        right = lambda c: jnp.where(pos == GRID_W - 1, 0.0, pltpu.roll(col(z, c), tm - 1, 0))
        up = lambda c: jnp.concatenate([col(zp, c), col(z, c)[:tm - GRID_W]], axis=0)
        down = lambda c: jnp.concatenate([col(z, c)[GRID_W:], col(zn, c)], axis=0)
        q = RW_COLS // 4
        shifted = _shift_columns(z, [(0, left), (q, right), (2 * q, up), (3 * q, down)])
    else:
        prev = lambda c: jnp.where(row == 0, 0.0, pltpu.roll(col(z, c), 1, 0))
        nxt = lambda c: jnp.where(row == tm - 1, 0.0, pltpu.roll(col(z, c), tm - 1, 0))
        shifted = _shift_columns(z, [(0, prev), (RW_COLS // 2, nxt)])
    zm = z + mu * (shifted - z)
    r = zm[:, 0:256]
    k = zm[:, 256:512]
    v = zm[:, 512:768]
    lo = zm[:, 768:896]
    g = _mm(_sigmoid(lo), gup)
    kk = k * kkp
    kk = kk * lax.rsqrt(_mm_right01(kk * kk, bones) + 1e-12)
    bonus = _mm_right01(r * k * rk, bones) * v
    outs = [r.astype(BF16), v.astype(BF16), kk.astype(BF16), g.astype(BF16), bonus.astype(BF16)]
    tlo = jnp.tanh(lo)
    for d in range(2):
        w_raw = -_softplus(-(w0_ref[d] + _mm(tlo, wup_ref[d]))) - 0.5
        a = _sigmoid(a0_ref[d] + _mm(lo, aup_ref[d]))
        outs += [-jnp.exp(w_raw),
                 (k * (1.0 + (a - 1.0) * ka)).astype(BF16), a.astype(BF16)]
    return outs


def _inproj_kernel(grid_shift, nt, x_ref, xp_ref, xn_ref, g_ref, sc_ref, sh_ref,
                   w5_ref, wr_ref, wt_ref, wg_ref, cr_ref, sr_ref, cc_ref, sn_ref,
                   mu_ref, w0_ref, wup_ref, a0_ref, aup_ref, gup_ref, kk_ref, ka_ref, rk_ref, bones_ref,
                   o5_ref, o5g_ref, ot_ref, og_ref, *rest):
    rw_refs, z5h_ref = rest[:-1], rest[-1]
    j = pl.program_id(1)
    norm = lambda x: _modnorm(x, g_ref[...], sc_ref[0], sh_ref[0]).astype(BF16)
    hb = norm(x_ref[0])
    z5 = jnp.dot(hb, w5_ref[...], preferred_element_type=F32)
    o5_ref[0] = z5
    og_ref[0] = jnp.dot(hb, wg_ref[...], preferred_element_type=F32)
    zr = jnp.dot(hb, wr_ref[...], preferred_element_type=F32)
    zp = zn = None
    if grid_shift:
        zp = jnp.where(j > 0, jnp.dot(norm(xp_ref[0]), wr_ref[...], preferred_element_type=F32), 0.0)
        zn = jnp.where(j < nt - 1, jnp.dot(norm(xn_ref[0]), wr_ref[...], preferred_element_type=F32), 0.0)
    rw = _rw_prep_math(grid_shift, zr, zp, zn, mu_ref[...], w0_ref, wup_ref, a0_ref, aup_ref, gup_ref[...],
                       kk_ref[...], ka_ref[...], rk_ref[...], bones_ref[...])
    for ref, val in zip(rw_refs, rw):
        ref[0] = val
    zt = jnp.dot(hb, wt_ref[...], preferred_element_type=F32)
    lane = lax.broadcasted_iota(jnp.int32, (GRID_W, GROUP_W), 1)
    by_row = (lane & 32) == 0
    nrow = cr_ref.shape[0]
    cos = jnp.concatenate([jnp.where(by_row, cr_ref[j], cc_ref[...]) for j in range(nrow)], axis=0)
    sin = jnp.concatenate([jnp.where(by_row, sr_ref[j], sn_ref[...]) for j in range(nrow)], axis=0)
    ot_ref[0, :, 0:GROUP_W] = _rope(zt[:, 0:GROUP_W], cos, sin)
    ot_ref[0, :, GROUP_W:2 * GROUP_W] = _rope(zt[:, GROUP_W:2 * GROUP_W] * RET_DK ** -0.5, cos, sin)
    ot_ref[0, :, 2 * GROUP_W:] = zt[:, 2 * GROUP_W:]
    nch = o5g_ref.shape[2]
    gph = LANES // S5_P
    for hf in range(GROUP_W // LANES):
        z5h_ref[hf] = z5[:, hf * LANES:(hf + 1) * LANES]
    for s in range(S5_T):
        for hf in range(GROUP_W // LANES):
            rows = z5h_ref[hf, pl.ds(s, nch, stride=S5_T), :]
            for g in range(gph):
                o5g_ref[0, hf * gph + g, :, s * S5_P:(s + 1) * S5_P] = rows[:, g * S5_P:(g + 1) * S5_P]


def _inproj(x, g, sc, sh, w5, wr, wt, wg, rope, rwp, grid_shift, tm):
    b, l, d = x.shape
    nt = l // tm
    hb = tm // GRID_W
    nhb = l // GRID_W
    tok = lambda n: pl.BlockSpec((1, tm, n), lambda i, j: (i, j, 0))
    vec = pl.BlockSpec((1, 1, d), lambda i, j: (i, 0, 0))
    full = lambda a: pl.BlockSpec(a.shape, lambda i, j: (0,) * a.ndim)
    grouped = pl.BlockSpec((1, S5_G, tm // S5_T, S5_TP), lambda i, j: (i, 0, j, 0))
    rowt = pl.BlockSpec((hb, 1, GROUP_W), lambda i, j: (j, 0, 0))
    above = pl.BlockSpec((1, GRID_W, d), lambda i, j: (i, jnp.maximum(j * hb - 1, 0), 0))
    below = pl.BlockSpec((1, GRID_W, d), lambda i, j: (i, jnp.minimum((j + 1) * hb, nhb - 1), 0))
    cr, sr, cc, sn = rope
    params = (rwp['mu'], rwp['w0'], rwp['wup'], rwp['a0'], rwp['aup'], rwp['gup'], rwp['kk'], rwp['ka'],
              rwp['rk'], rwp['bones'])
    rw_dtypes = (BF16,) * 5 + (F32, BF16, BF16) * 2
    outs = pl.pallas_call(
        functools.partial(_inproj_kernel, grid_shift, nt),
        grid=(b, nt),
        in_specs=[tok(d), above, below, pl.BlockSpec((1, d), lambda i, j: (0, 0)), vec, vec,
                  full(w5), full(wr), full(wt), full(wg), rowt, rowt, full(cc), full(sn)]
                 + [full(a) for a in params],
        out_specs=[tok(GROUP_W), grouped, tok(wt.shape[1]), tok(wg.shape[1])] + [tok(GROUP_W)] * 11,
        out_shape=[jax.ShapeDtypeStruct((b, l, GROUP_W), F32),
                   jax.ShapeDtypeStruct((b, S5_G, l // S5_T, S5_TP), F32),
                   jax.ShapeDtypeStruct((b, l, wt.shape[1]), F32),
                   jax.ShapeDtypeStruct((b, l, wg.shape[1]), F32)]
                  + [jax.ShapeDtypeStruct((b, l, GROUP_W), dt) for dt in rw_dtypes],
        scratch_shapes=[pltpu.VMEM((GROUP_W // LANES, tm, LANES), F32)],
        compiler_params=_cparams("parallel", "parallel"),
        name="norm_inproj",
    )(x, x, x, g, sc, sh, w5, wr, wt, wg, cr, sr, cc, sn, *params)
    return outs[0], outs[1], outs[2], outs[3], tuple(outs[4:])


def _s5_tables(lam_re, lam_im, log_dt, b_re, b_im, c_re, c_im, nlev):
    hp = lax.Precision.HIGHEST
    t = S5_T
    lam = lax.complex(jnp.minimum(lam_re.astype(F32), -1e-4), lam_im.astype(F32))
    ldt = lam * jnp.exp(log_dt.astype(F32))[..., None]
    a_bar = jnp.exp(ldt)
    bb = ((a_bar - 1.0) / lam)[..., None] * lax.complex(b_re.astype(F32), b_im.astype(F32))
    cm = lax.complex(c_re.astype(F32), c_im.astype(F32))
    tau = jnp.arange(t + 1, dtype=F32)
    apow = jnp.exp(ldt[:, :, None, :] * tau[None, None, :, None])
    taps = jnp.einsum('dgpn,dgtn,dgnq->dgtpq', cm, apow[:, :, :t], bb, precision=hp).real
    taprow = jnp.stack([taps[0], taps[1][:, ::-1]]).transpose(0, 1, 4, 2, 3).reshape(2, S5_G, S5_P, S5_TP)

    def pack(zc):
        return jnp.concatenate([zc.real, zc.imag], axis=-1)

    win_f = apow[0][:, t - 1 - jnp.arange(t), None, :] * bb[0].transpose(0, 2, 1)[:, None]
    win_b = apow[1][:, jnp.arange(t), None, :] * bb[1].transpose(0, 2, 1)[:, None]
    win = jnp.stack([pack(win_f), pack(win_b)]).reshape(2, S5_G, S5_TP, 2 * S5_N)
    ca_f = cm[0][:, None] * apow[0][:, 1 + jnp.arange(t), None, :]
    ca_b = cm[1][:, None] * apow[1][:, t - jnp.arange(t), None, :]

    def outpack(ca):
        w = jnp.concatenate([ca.real, -ca.imag], axis=-1)
        return w.reshape(S5_G, S5_TP, 2 * S5_N).transpose(0, 2, 1)

    wout = jnp.stack([outpack(ca_f), outpack(ca_b)])
    lev = (2.0 ** jnp.arange(nlev, dtype=F32)) * t
    pw = jnp.exp(ldt[:, :, None, :] * lev[None, None, :, None])
    p1 = jnp.concatenate([pw.real, pw.real], axis=-1)
    p2 = jnp.concatenate([-pw.imag, pw.imag], axis=-1)
    pw = jnp.stack([p1, p2], axis=3)
    return taprow, win.astype(BF16), wout.astype(BF16), pw


def _s5_kernel(nc, nlev, u_ref, tap_ref, win_ref, wout_ref, pw_ref, h0_ref, y_ref, hfin_ref, conv_ref):
    u = u_ref[0, 0].astype(BF16)
    row = lax.broadcasted_iota(jnp.int32, (nc, 2 * S5_N), 0)
    lane = lax.broadcasted_iota(jnp.int32, (S5_P, S5_TP), 1)
    for s in range(S5_T):
        lo = s * S5_P
        fwd = tap_ref[0, 0] if s == 0 else jnp.where(lane >= lo, pltpu.roll(tap_ref[0, 0], lo, 1), 0.0)
        hi = lo + S5_P
        bwd = tap_ref[1, 0] if hi == S5_TP else jnp.where(lane < hi, pltpu.roll(tap_ref[1, 0], hi, 1), 0.0)
        conv_ref[0, lo:hi, :] = fwd.astype(BF16)
        conv_ref[1, lo:hi, :] = bwd.astype(BF16)

    def cmul(x, d, j):
        return pw_ref[d, 0, j, 0:1] * x + pw_ref[d, 0, j, 1:2] * pltpu.roll(x, S5_N, 1)

    v = [jnp.dot(u, win_ref[d, 0], preferred_element_type=F32) for d in range(2)]
    x = [jnp.where(row == 0, h0_ref[0, 0, 0], pltpu.roll(v[0], 1, 0)),
         jnp.where(row == nc - 1, h0_ref[0, 1, 0], pltpu.roll(v[1], nc - 1, 0))]
    for j in range(nlev):
        sh = 2 ** j
        xs = [jnp.where(row >= sh, pltpu.roll(x[0], sh, 0), 0.0),
              jnp.where(row < nc - sh, pltpu.roll(x[1], nc - sh, 0), 0.0)]
        x = [x[d] + cmul(xs[d], d, j) for d in range(2)]
    for d, last in enumerate((nc - 1, 0)):
        hfin_ref[0, d, 0] = cmul(x[d][last:last + 1], d, 0) + v[d][last:last + 1]
    y_ref[0, 0] = (jnp.dot(u, conv_ref[0], preferred_element_type=F32) + _mm(x[0], wout_ref[0, 0])
                   + jnp.dot(u, conv_ref[1], preferred_element_type=F32) + _mm(x[1], wout_ref[1, 0]))


def _s5_scan(uf, tables, h0):
    conv, win, wout, pw = tables
    b, _, nc, _ = uf.shape
    nlev = max(1, (nc - 1).bit_length())
    pw = pw[:, :, :nlev]
    n2 = 2 * S5_N
    y, hfin = pl.pallas_call(
        functools.partial(_s5_kernel, nc, nlev),
        grid=(b, S5_G),
        in_specs=[pl.BlockSpec((1, 1, nc, S5_TP), lambda i, g: (i, g, 0, 0)),
                  pl.BlockSpec((2, 1, S5_P, S5_TP), lambda i, g: (0, g, 0, 0)),
                  pl.BlockSpec((2, 1, S5_TP, n2), lambda i, g: (0, g, 0, 0)),
                  pl.BlockSpec((2, 1, n2, S5_TP), lambda i, g: (0, g, 0, 0)),
                  pl.BlockSpec((2, 1, nlev, 2, n2), lambda i, g: (0, g, 0, 0, 0)),
                  pl.BlockSpec((1, 2, 1, 1, n2), lambda i, g: (i, 0, g, 0, 0))],
        out_specs=[pl.BlockSpec((1, 1, nc, S5_TP), lambda i, g: (i, g, 0, 0)),
                   pl.BlockSpec((1, 2, 1, 1, n2), lambda i, g: (i, 0, g, 0, 0))],
        out_shape=[jax.ShapeDtypeStruct((b, S5_G, nc, S5_TP), F32),
                   jax.ShapeDtypeStruct((b, 2, S5_G, 1, n2), F32)],
        scratch_shapes=[pltpu.VMEM((2, S5_TP, S5_TP), BF16)],
        compiler_params=_cparams("parallel", "parallel"),
        name="s5_scan",
    )(uf, conv, win, wout, pw, h0)
    return y, hfin


def _rw_scan_kernel(nb, cpb, rf_ref, vf_ref, kkf_ref, lwf_ref, kdf_ref, asf_ref,
                    rb_ref, vb_ref, kkb_ref, lwb_ref, kdb_ref, asb_ref, s0_ref,
                    yf_ref, yb_ref, sfin_ref, st_ref):
    c = pl.program_id(1)

    @pl.when(c == 0)
    def _():
        st_ref[...] = s0_ref[0]

    t = CHUNK
    ri = lax.broadcasted_iota(jnp.int32, (t, LANES), 0)
    li = lax.broadcasted_iota(jnp.int32, (t, LANES), 1)
    ci = li & (t - 1)
    low = li < HEAD_V
    eyef = jnp.where(ri == ci, 1.0, 0.0)
    same = lambda s: jnp.where((ri >> s) == (ci >> s), 1.0, 0.0)
    m4, m8, m16, m32 = same(2), same(3), same(4), same(5)
    merge_masks = (m8 - m4, m16 - m8, m32 - m16, 1.0 - m32)

    def bd(x):
        return jnp.concatenate([jnp.where(low, x, 0.0), jnp.where(low, 0.0, x)], axis=0)

    def diag_blocks(full):
        return jnp.where(low, full[0:HEAD_V], full[HEAD_V:2 * HEAD_V])

    dir_refs = ((rf_ref, vf_ref, kkf_ref, lwf_ref, kdf_ref, asf_ref),
                (rb_ref, vb_ref, kkb_ref, lwb_ref, kdb_ref, asb_ref))
    units = []
    for d, (r_ref, v_ref, kk_ref, lw_ref, kd_ref, as_ref) in enumerate(dir_refs):
        incl = (ci >= ri) if d == 1 else (ci <= ri)
        strict = (ci > ri) if d == 1 else (ci < ri)
        r, v, kk, kd = (x[0].astype(F32) for x in (r_ref, v_ref, kk_ref, kd_ref))
        lw = lw_ref[0]
        cin = _mm_left01(_chunk_tri01(cpb * t, t, d == 1), lw)
        e_in = jnp.exp(cin)
        e_neg = jnp.exp(-cin)
        rt = r * e_in
        at = -kk * jnp.exp(cin - lw)
        bvec = kk * as_ref[0].astype(F32)
        bt = bvec * e_neg
        kt = kd * e_neg
        for j in range(cpb):
            rows = slice(j * t, (j + 1) * t)
            last = j * t + (0 if d == 1 else t - 1)
            clast = cin[last:last + 1]
            dl = jnp.exp(clast - cin[rows])
            bh = bvec[rows] * dl
            kh = kd[rows] * dl
            dec = jnp.exp(clast)
            for p in range(HEADS // 2):
                sl = slice(p * LANES, (p + 1) * LANES)
                units.append(dict(d=d, j=j, p=p, incl=incl, strict=strict,
                                  at=at[rows, sl], rt=rt[rows, sl], bt=bt[rows, sl], kt=kt[rows, sl],
                                  bh=bh[:, sl], kh=kh[:, sl], v=v[rows, sl], dec=dec[:, sl]))

    x1 = [jnp.concatenate([u['at'], u['rt']], axis=0) for u in units]
    ab_rb = [_mm_nt(x, bd(u['bt'])) for x, u in zip(x1, units)]
    ak_rk = [_mm_nt(x, bd(u['kt'])) for x, u in zip(x1, units)]
    nmat = [jnp.where(u['strict'], a[0:t], 0.0) for u, a in zip(units, ab_rb)]
    a_rb = [jnp.where(u['incl'], a[t:2 * t], 0.0) for u, a in zip(units, ab_rb)]
    a_kk = [jnp.concatenate([jnp.where(u['strict'], a[0:t], 0.0), jnp.where(u['incl'], a[t:2 * t], 0.0)], axis=0)
            for u, a in zip(units, ak_rk)]
    akv = [_mm(a, bd(u['v'])) for u, a in zip(units, a_kk)]
    kv = [diag_blocks(_mm_tn(u['v'], u['kh'])) for u in units]
    nd = [x * m4 for x in nmat]
    n2 = [_mm(x, bd(x)) for x in nd]
    tinv = [eyef + x + _mm(eyef + x, bd(y)) for x, y in zip(nd, n2)]
    for mk in merge_masks:
        w = [_mm(ti, bd(x * mk)) for ti, x in zip(tinv, nmat)]
        tinv = [ti + _mm(wi, bd(ti)) for ti, wi in zip(tinv, w)]
    zz = [_mm(ti, jnp.concatenate([bd(u['at']), bd(kvv[0:t])], axis=1))
          for ti, u, kvv in zip(tinv, units, akv)]
    ght = [_mm_tn(z, u['bh']) for u, z in zip(units, zz)]
    qy = [_mm(a, jnp.concatenate([bd(z[:, :LANES]), bd(z[:, LANES:])], axis=1))
          for a, z in zip(a_rb, zz)]
    qmat = [u['rt'] + x[:, :LANES] for u, x in zip(units, qy)]
    y0 = [x[:, LANES:] + kvv[t:2 * t] for x, kvv in zip(qy, akv)]
    gmat = [diag_blocks(x[:LANES]) for x in ght]
    hmat = [diag_blocks(x[LANES:]) + k2 for x, k2 in zip(ght, kv)]
    idx = {(u['d'], u['j'], u['p']): n for n, u in enumerate(units)}
    npair = HEADS // 2
    state = {(d, p): st_ref[d, p] for d in range(2) for p in range(npair)}
    ys = {}
    for step in range(cpb):
        for d in range(2):
            j = step if d == 0 else cpb - 1 - step
            for p in range(npair):
                n = idx[(d, j, p)]
                st = state[(d, p)]
                ys[(d, j, p)] = _mm_nt(qmat[n], bd(st)) + y0[n]
                state[(d, p)] = units[n]['dec'] * st + _mm(st, bd(gmat[n])) + hmat[n]
    for d, y_ref in enumerate((yf_ref, yb_ref)):
        y_ref[0] = jnp.concatenate(
            [jnp.concatenate([ys[(d, j, p)] for p in range(npair)], axis=1) for j in range(cpb)], axis=0)
        for p in range(npair):
            st_ref[d, p] = state[(d, p)]

    @pl.when(c == nb - 1)
    def _():
        sfin_ref[0] = st_ref[...]


def _rw_scan(pre, s0):
    r, v, kk, _, _, lw0, kd0, as0, lw1, kd1, as1 = pre
    b, l, _ = r.shape
    cpb = min(RW_CPB, l // CHUNK)
    nb = l // (cpb * CHUNK)
    fw = pl.BlockSpec((1, cpb * CHUNK, GROUP_W), lambda i, c: (i, c, 0))
    bw = pl.BlockSpec((1, cpb * CHUNK, GROUP_W), lambda i, c: (i, nb - 1 - c, 0))
    st = pl.BlockSpec((1, 2, HEADS // 2, HEAD_V, LANES), lambda i, c: (i, 0, 0, 0, 0))
    return pl.pallas_call(
        functools.partial(_rw_scan_kernel, nb, cpb),
        grid=(b, nb),
        in_specs=[fw] * 6 + [bw] * 6 + [st],
        out_specs=[fw, bw, st],
        out_shape=[jax.ShapeDtypeStruct((b, l, GROUP_W), F32)] * 2
                  + [jax.ShapeDtypeStruct((b, 2, HEADS // 2, HEAD_V, LANES), F32)],
        scratch_shapes=[pltpu.VMEM((2, HEADS // 2, HEAD_V, LANES), F32)],
        compiler_params=_cparams("parallel", "arbitrary"),
        name="rwkv_scan",
    )(r, v, kk, lw0, kd0, as0, r, v, kk, lw1, kd1, as1, s0)


def _rope(x, cos, sin_signed):
    lane = lax.broadcasted_iota(jnp.int32, x.shape, 1)
    n = x.shape[1]
    swapped = jnp.where((lane & 16) == 0, pltpu.roll(x, n - 16, 1), pltpu.roll(x, 16, 1))
    return x * cos + swapped * sin_signed


def _ret_scan_kernel(nb, cpb, qf_ref, kf_ref, vf_ref, qb_ref, kb_ref, vb_ref,
                     dmat_ref, qdec_ref, kdec_ref, sdec_ref, s0_ref,
                     of_ref, ob_ref, sfin_ref, st_ref):
    c = pl.program_id(1)

    @pl.when(c == 0)
    def _():
        st_ref[...] = s0_ref[0]

    t = dmat_ref.shape[-1]
    npair = HEADS // 2
    low_t = lax.broadcasted_iota(jnp.int32, (t, LANES), 1) < HEAD_V
    low_s = lax.broadcasted_iota(jnp.int32, (HEAD_V, LANES), 1) < HEAD_V

    def bd(x, low):
        return jnp.concatenate([jnp.where(low, x, 0.0), jnp.where(low, 0.0, x)], axis=0)

    units = []
    for d, (q_ref, k_ref, v_ref) in enumerate(((qf_ref, kf_ref, vf_ref), (qb_ref, kb_ref, vb_ref))):
        for j in range(cpb):
            rows = slice(j * t, (j + 1) * t)
            q, k, v = q_ref[0, rows], k_ref[0, rows], v_ref[0, rows]
            qd = q * qdec_ref[d]
            kh = k * kdec_ref[d]
            for p in range(npair):
                sl = slice(p * LANES, (p + 1) * LANES)
                units.append(dict(d=d, j=j, p=p, q=q[:, sl], k=k[:, sl], qd=qd[:, sl], kh=kh[:, sl],
                                  v=v[:, sl], dec=sdec_ref[d][:, sl]))
    scores = [[_mm_nt(jnp.where(low_t == (h == 0), u['q'], 0.0), u['k']).astype(BF16)
               * dmat_ref[u['d'], 2 * u['p'] + h] for h in range(2)] for u in units]
    intra = [_mm(jnp.concatenate(s2, axis=1), bd(u['v'], low_t)) for s2, u in zip(scores, units)]
    kv = []
    for u in units:
        full = _mm_tn(u['v'], u['kh'])
        kv.append(jnp.where(low_s, full[0:HEAD_V], full[HEAD_V:2 * HEAD_V]))
    idx = {(u['d'], u['j'], u['p']): i for i, u in enumerate(units)}
    entering = {}
    for d in range(2):
        for p in range(npair):
            st = st_ref[d, p]
            for step in range(cpb):
                j = step if d == 0 else cpb - 1 - step
                i = idx[(d, j, p)]
                entering[i] = st
                st = units[i]['dec'] * st + kv[i]
            st_ref[d, p] = st
    outs = [x + _mm_nt(u['qd'], bd(entering[i], low_s)) for i, (x, u) in enumerate(zip(intra, units))]
    for d, o_ref in enumerate((of_ref, ob_ref)):
        o_ref[0] = jnp.concatenate(
            [jnp.concatenate([outs[idx[(d, j, p)]] for p in range(npair)], axis=1) for j in range(cpb)], axis=0)

    @pl.when(c == nb - 1)
    def _():
        sfin_ref[0] = st_ref[...]


def _ret_tables(decay_logit, n):
    lg = jax.nn.log_sigmoid(decay_logit.astype(F32))
    pos = jnp.arange(n, dtype=F32)
    lag = pos[:, None] - pos[None, :]
    lag = jnp.stack([lag, -lag])
    dmat = jnp.where(lag[:, None] >= 0, jnp.exp(lg[:, :, None, None] * lag[:, None]), 0.0)
    lanes = jnp.repeat(lg, RET_DK, axis=-1)[:, None, :]
    qpow = jnp.stack([pos + 1.0, n - pos])[:, :, None]
    kpow = jnp.stack([n - 1.0 - pos, pos])[:, :, None]
    return dmat.astype(BF16), jnp.exp(lanes * qpow), jnp.exp(lanes * kpow), jnp.exp(lanes * n)


def _gla_scan_kernel(nb, cpb, qf_ref, kf_ref, vf_ref, af_ref, qb_ref, kb_ref, vb_ref, ab_ref,
                     aup_ref, abias_ref, s0_ref, of_ref, ob_ref, sfin_ref, st_ref):
    c = pl.program_id(1)

    @pl.when(c == 0)
    def _():
        st_ref[...] = s0_ref[0]

    t = GLA_T
    n = cpb * t
    dk = GLA_DK
    scale = dk ** -0.5
    dirs = ((qf_ref, kf_ref, vf_ref, af_ref), (qb_ref, kb_ref, vb_ref, ab_ref))
    units = []
    for d, (q_ref, k_ref, v_ref, a_ref) in enumerate(dirs):
        incl = _tri_masks(t, d == 1)[0]
        q, k, v = q_ref[0], k_ref[0] * scale, v_ref[0]
        lw = -_softplus(-(_mm(a_ref[0], aup_ref[d]) + abias_ref[d])) * (1.0 / GLA_TAU)
        cin = _mm_left01(_chunk_tri01(n, t, d == 1), lw)
        qe = q * jnp.exp(cin)
        for j in range(cpb):
            rows = slice(j * t, (j + 1) * t)
            last = j * t + (0 if d == 1 else t - 1)
            mid = j * t + (t // 2 if d == 1 else t // 2 - 1)
            cj = cin[rows]
            clast = cin[last:last + 1]
            cmid = cin[mid:mid + 1]
            qt = q[rows] * jnp.exp(cj - cmid)
            kt = k[rows] * jnp.exp(cmid - cj)
            kh = k[rows] * jnp.exp(clast - cj)
            units.append(dict(d=d, j=j, incl=incl, qt=qt, kt=kt, qe=qe[rows], kh=kh, v=v[rows],
                              dec=jnp.exp(clast)))

    klane = lax.broadcasted_iota(jnp.int32, (t, LANES), 1) // dk
    vlow = lax.broadcasted_iota(jnp.int32, (t, LANES), 1) < HEAD_V
    khead = lax.broadcasted_iota(jnp.int32, (HEAD_V, LANES), 1) // dk
    npair = HEADS // 2

    def v_blockdiag(x):
        return jnp.concatenate([jnp.where(vlow, x, 0.0), jnp.where(vlow, 0.0, x)], axis=0)

    def state_rows(st, p):
        return jnp.concatenate([jnp.where(khead == 2 * p, st, 0.0), jnp.where(khead == 2 * p + 1, st, 0.0)],
                               axis=0)

    amat = [[jnp.where(u['incl'], _mm_nt(jnp.where(klane == h, u['qt'], 0.0), u['kt']), 0.0)
             for h in range(HEADS)] for u in units]
    intra = [[_mm(jnp.concatenate(a[2 * p:2 * p + 2], axis=1), v_blockdiag(u['v'][:, p * LANES:(p + 1) * LANES]))
              for p in range(npair)] for a, u in zip(amat, units)]
    kv = []
    for u in units:
        full = _mm_tn(u['v'], u['kh'])
        blocks = [full[h * HEAD_V:(h + 1) * HEAD_V] for h in range(HEADS)]
        acc = blocks[HEADS - 1]
        for h in range(HEADS - 2, -1, -1):
            acc = jnp.where(khead == h, blocks[h], acc)
        kv.append(acc)
    idx = {(u['d'], u['j']): i for i, u in enumerate(units)}
    entering = {}
    for d in range(2):
        st = st_ref[d]
        for step in range(cpb):
            j = step if d == 0 else cpb - 1 - step
            i = idx[(d, j)]
            entering[i] = st
            st = units[i]['dec'] * st + kv[i]
        st_ref[d] = st
    outs = [[x[p] + _mm_nt(u['qe'], state_rows(entering[i], p)) for p in range(npair)]
            for i, (x, u) in enumerate(zip(intra, units))]
    for d, o_ref in enumerate((of_ref, ob_ref)):
        o_ref[0] = jnp.concatenate(
            [jnp.concatenate(outs[idx[(d, j)]], axis=1) for j in range(cpb)], axis=0)

    @pl.when(c == nb - 1)
    def _():
        sfin_ref[0] = st_ref[...]


def _rope_tables(rows):
    nf = RET_DK // 4
    inv = ROPE_BASE ** (-jnp.arange(nf, dtype=F32) / nf)
    lane = jnp.arange(GROUP_W)
    freq = inv[lane % nf]
    sign = jnp.where((lane & 16) == 0, -1.0, 1.0)
    ar = jnp.arange(rows, dtype=F32)[:, None] * freq[None, :]
    ac = jnp.arange(GRID_W, dtype=F32)[:, None] * freq[None, :]
    return (jnp.cos(ar).reshape(rows, 1, GROUP_W), (jnp.sin(ar) * sign).reshape(rows, 1, GROUP_W),
            jnp.cos(ac), jnp.sin(ac) * sign)


def _ret_scan(z, tables, s0):
    b, l, _ = z.shape
    n = min(RET_BLOCK, l)
    nb = l // n
    fw = lambda j: pl.BlockSpec((1, n, GROUP_W), lambda i, c: (i, c, j))
    bw = lambda j: pl.BlockSpec((1, n, GROUP_W), lambda i, c: (i, nb - 1 - c, j))
    full = lambda a: pl.BlockSpec(a.shape, lambda i, c: (0,) * a.ndim)
    st = pl.BlockSpec((1, 2, HEADS // 2, HEAD_V, LANES), lambda i, c: (i, 0, 0, 0, 0))
    return pl.pallas_call(
        functools.partial(_ret_scan_kernel, nb, n // RET_T),
        grid=(b, nb),
        in_specs=[fw(0), fw(1), fw(2), bw(0), bw(1), bw(2)] + [full(a) for a in tables] + [st],
        out_specs=[fw(0), bw(0), st],
        out_shape=[jax.ShapeDtypeStruct((b, l, GROUP_W), F32)] * 2
                  + [jax.ShapeDtypeStruct((b, 2, HEADS // 2, HEAD_V, LANES), F32)],
        scratch_shapes=[pltpu.VMEM((2, HEADS // 2, HEAD_V, LANES), F32)],
        compiler_params=_cparams("parallel", "arbitrary"),
        name="retention_scan",
    )(z, z, z, z, z, z, *tables, s0)


def _gla_scan(z, aup, abias, s0):
    b, l, _ = z.shape
    cpb = min(GLA_BLOCK, l) // CHUNK
    nb = l // (cpb * CHUNK)
    blk = lambda w, j, rev: pl.BlockSpec(
        (1, cpb * CHUNK, w), (lambda i, c: (i, nb - 1 - c, j)) if rev else (lambda i, c: (i, c, j)))
    st = pl.BlockSpec((1, 2, HEAD_V, GLA_QK), lambda i, c: (i, 0, 0, 0))
    ofw = pl.BlockSpec((1, cpb * CHUNK, GROUP_W), lambda i, c: (i, c, 0))
    obw = pl.BlockSpec((1, cpb * CHUNK, GROUP_W), lambda i, c: (i, nb - 1 - c, 0))
    return pl.pallas_call(
        functools.partial(_gla_scan_kernel, nb, cpb * CHUNK // GLA_T),
        grid=(b, nb),
        in_specs=[blk(GLA_QK, 0, False), blk(GLA_QK, 1, False), blk(GROUP_W, 1, False), blk(128, 6, False),
                  blk(GLA_QK, 0, True), blk(GLA_QK, 1, True), blk(GROUP_W, 1, True), blk(128, 6, True),
                  pl.BlockSpec(aup.shape, lambda i, c: (0, 0, 0)),
                  pl.BlockSpec(abias.shape, lambda i, c: (0, 0, 0)), st],
        out_specs=[ofw, obw, st],
        out_shape=[jax.ShapeDtypeStruct((b, l, GROUP_W), F32)] * 2
                  + [jax.ShapeDtypeStruct((b, 2, HEAD_V, GLA_QK), F32)],
        scratch_shapes=[pltpu.VMEM((2, HEAD_V, GLA_QK), F32)],
        compiler_params=_cparams("parallel", "arbitrary"),
        name="gla_scan",
    )(z, z, z, z, z, z, z, z, aup, abias, s0)


def _mix_kernel(x_ref, g1_ref, y5_ref, u5_ref, d5_ref, gw_ref, gb_ref,
                ryf_ref, ryb_ref, rbonus_ref, rg_ref, rlng_ref, rlnb_ref,
                tof_ref, tob_ref, tg_ref, tln_ref,
                gof_ref, gob_ref, gg_ref, gln_ref,
                bones_ref, wo_ref, o_ref, y5t_ref):
    bones = bones_ref[...]
    inv = 1.0 / HEAD_V

    def hmean(a):
        return _mm_right01(a, bones) * inv

    nch = y5_ref.shape[2]
    gph = LANES // S5_P
    y5g = [y5_ref[0, g] for g in range(S5_G)]
    for s in range(S5_T):
        for hf in range(GROUP_W // LANES):
            y5t_ref[hf, pl.ds(s, nch, stride=S5_T), :] = jnp.concatenate(
                [yg[:, s * S5_P:(s + 1) * S5_P] for yg in y5g[hf * gph:(hf + 1) * gph]], axis=1)
    y = jnp.concatenate([y5t_ref[0], y5t_ref[1]], axis=1) + d5_ref[...] * u5_ref[0]
    y = _gelu_tanh(y)
    ya = y * _sigmoid(_mm(y, gw_ref[...]) + gb_ref[...])
    yr = ryf_ref[0] + ryb_ref[0]
    dlt = yr - hmean(yr)
    yn = dlt * lax.rsqrt(hmean(dlt * dlt) + GN_EPS)
    yb = (yn * rlng_ref[...] + rlnb_ref[...] + rbonus_ref[0].astype(F32)) * rg_ref[0].astype(F32)
    ot = tof_ref[0] + tob_ref[0]
    yc = ot * lax.rsqrt(hmean(ot * ot) + EPS) * tln_ref[...] * _silu(tg_ref[0])
    og = gof_ref[0] + gob_ref[0]
    yd = og * lax.rsqrt(hmean(og * og) + EPS) * gln_ref[...] * _silu(gg_ref[0])
    mix = (_mm(ya, wo_ref[0]) + _mm(yb, wo_ref[1]) + _mm(yc, wo_ref[2]) + _mm(yd, wo_ref[3]))
    o_ref[0] = x_ref[0] + g1_ref[0] * mix


def _mix(x, g1, y5, u5, rw_pre, rw_y, zret, ret_o, zgla, gla_o, p, tm):
    b, l, d = x.shape
    tok = lambda w, j: pl.BlockSpec((1, tm, w), lambda i, t: (i, t, j))
    vec = lambda a: pl.BlockSpec(a.shape, lambda i, t: (0,) * a.ndim)
    t256 = tok(GROUP_W, 0)
    args = [x, g1, y5, u5, p['s5_d'], p['glu_w'], p['glu_b'],
            rw_y[0], rw_y[1], rw_pre[4], rw_pre[3], p['rw_ln_g'], p['rw_ln_b'],
            ret_o[0], ret_o[1], zret, p['ret_ln_g'],
            gla_o[0], gla_o[1], zgla, p['gla_ln_g'],
            p['bones'], p['w_out']]
    grouped = pl.BlockSpec((1, S5_G, tm // S5_T, S5_TP), lambda i, t: (i, 0, t, 0))
    specs = [tok(d, 0), pl.BlockSpec((1, 1, d), lambda i, t: (i, 0, 0)), grouped, t256,
             vec(p['s5_d']), vec(p['glu_w']), vec(p['glu_b']),
             t256, t256, t256, t256, vec(p['rw_ln_g']), vec(p['rw_ln_b']),
             t256, t256, tok(GROUP_W, 3), vec(p['ret_ln_g']),
             t256, t256, tok(GROUP_W, 2), vec(p['gla_ln_g']),
             vec(p['bones']), vec(p['w_out'])]
    return pl.pallas_call(
        _mix_kernel,
        grid=(b, l // tm),
        in_specs=specs,
        out_specs=tok(d, 0),
        out_shape=jax.ShapeDtypeStruct((b, l, d), F32),
        scratch_shapes=[pltpu.VMEM((GROUP_W // LANES, tm, LANES), F32)],
        compiler_params=_cparams("parallel", "parallel"),
        name="mix_outproj",
    )(*args)


def _mlp_kernel(final, nff, x_ref, g_ref, sc_ref, sh_ref, gate_ref, w1_ref, w2_ref, fg_ref, o_ref):
    x = x_ref[0]
    hb = _modnorm(x, g_ref[...], sc_ref[0], sh_ref[0]).astype(BF16)
    ff = w1_ref.shape[2] // nff
    acc = None
    for j in range(nff):
        a = jnp.maximum(jnp.dot(hb, w1_ref[0, :, j * ff:(j + 1) * ff], preferred_element_type=F32), 0.0)
        part = jnp.dot((a * a).astype(BF16), w2_ref[0, j * ff:(j + 1) * ff, :], preferred_element_type=F32)
        acc = part if acc is None else acc + part
    y = x + gate_ref[0] * acc
    if final:
        ms = jnp.mean(y * y, axis=-1, keepdims=True)
        y = y * lax.rsqrt(ms + EPS) * fg_ref[...]
    o_ref[0] = y


def _mlp(x, g, sc, sh, gate, w1, w2, layer, final_g, final, tm):
    b, l, d = x.shape
    tok = pl.BlockSpec((1, tm, d), lambda i, t: (i, t, 0))
    vec = pl.BlockSpec((1, 1, d), lambda i, t: (i, 0, 0))
    row = pl.BlockSpec((1, d), lambda i, t: (0, 0))
    once = lambda a: pl.BlockSpec((1,) + a.shape[1:], lambda i, t: (layer, 0, 0), pipeline_mode=pl.Buffered(1))
    return pl.pallas_call(
        functools.partial(_mlp_kernel, final, 4),
        grid=(b, l // tm),
        in_specs=[tok, row, vec, vec, vec, once(w1), once(w2), row],
        out_specs=tok,
        out_shape=jax.ShapeDtypeStruct((b, l, d), F32),
        compiler_params=_cparams("parallel", "parallel"),
        name="mlp",
    )(x, g, sc, sh, gate, w1, w2, final_g)


def kernel(x, c, ctx, c_ctx, ada_w, ada_b, norm1_g, norm2_g, w_in, w_out, s5_lam_re, s5_lam_im, s5_log_dt, s5_b_re, s5_b_im, s5_c_re, s5_c_im, s5_d, s5_glu_w, s5_glu_b, rw_mu, rw_w0, rw_w_up, rw_a0, rw_a_up, rw_g_up, rw_k_k, rw_k_a, rw_r_k, rw_ln_g, rw_ln_b, ret_decay_logit, ret_ln_g, gla_a_up, gla_a_b, gla_ln_g, mlp_w1, mlp_w2, final_g):
    b, l, d = x.shape
    lc = ctx.shape[1]
    depth = ada_w.shape[0]
    rows = l // GRID_W
    assert l % 512 == 0 and lc % CHUNK == 0 and lc % S5_T == 0 and d % 128 == 0

    cond = jnp.zeros((8, d), F32).at[:b].set(c).at[b].set(c_ctx)
    mod = _modulation(cond, ada_w, ada_b)

    lane = jnp.arange(GROUP_W)
    bones = (lane[:, None] // HEAD_V == lane[None, :] // HEAD_V).astype(BF16)
    rope_lat = _rope_tables(rows)
    ident = (jnp.ones((lc // CHUNK, 1, GROUP_W), F32), jnp.zeros((lc // CHUNK, 1, GROUP_W), F32),
             jnp.ones((GRID_W, GROUP_W), F32), jnp.zeros((GRID_W, GROUP_W), F32))
    nlev = max(1, (l // S5_T - 1).bit_length())
    c0, c1, c2 = GROUP_W, GROUP_W + RW_COLS, GROUP_W + RW_COLS + 4 * GROUP_W
    row2 = lambda a: a.reshape(1, -1).astype(F32)
    w_in_b, w_out_b, w1_b, w2_b = _to_bf16(w_in), _to_bf16(w_out), _to_bf16(mlp_w1), _to_bf16(mlp_w2)

    xc = ctx
    for i in range(depth):
        last = i == depth - 1
        m = mod[i].reshape(8, N_ADA, d)
        ml = m[:b, :, None, :]
        mc = jnp.broadcast_to(m[b][None, :, None, :], (b, N_ADA, 1, d))
        wi = w_in_b[i]
        w5, wr, wt = wi[:, :c0], wi[:, c0:c1], wi[:, c1:c2]
        wg = jnp.pad(wi[:, c2:], ((0, 0), (0, GLA_COLS_PAD - (wi.shape[1] - c2))))
        n1 = row2(norm1_g[i])
        n2 = row2(norm2_g[i])

        s5_tab = _s5_tables(s5_lam_re[i], s5_lam_im[i], s5_log_dt[i], s5_b_re[i], s5_b_im[i],
                            s5_c_re[i], s5_c_im[i], nlev)
        zpad = lambda a, lo, n: jnp.zeros((2, 128, n), F32).at[:, lo:lo + a.shape[1]].set(a).astype(BF16)
        rwp = dict(
            mu=row2(rw_mu[i]),
            w0=rw_w0[i].reshape(2, 1, GROUP_W), a0=rw_a0[i].reshape(2, 1, GROUP_W),
            wup=zpad(rw_w_up[i], 0, GROUP_W), aup=zpad(rw_a_up[i], RW_W_RANK, GROUP_W),
            gup=jnp.zeros((128, GROUP_W), F32).at[RW_W_RANK + RW_A_RANK:].set(rw_g_up[i]).astype(BF16),
            kk=row2(rw_k_k[i]), ka=row2(rw_k_a[i]), rk=row2(rw_r_k[i]), bones=bones)
        ret_tab = _ret_tables(ret_decay_logit[i], RET_T)
        gla_aup = zpad(gla_a_up[i], 0, GLA_QK)
        gla_ab = gla_a_b[i].reshape(2, 1, GLA_QK).astype(F32)
        mixp = dict(s5_d=row2(s5_d[i]), glu_w=s5_glu_w[i].astype(BF16), glu_b=row2(s5_glu_b[i]),
                    rw_ln_g=row2(rw_ln_g[i]), rw_ln_b=row2(rw_ln_b[i]), ret_ln_g=row2(ret_ln_g[i]),
                    gla_ln_g=row2(gla_ln_g[i]), bones=bones,
                    w_out=w_out_b[i].reshape(4, GROUP_W, d))

        def mixers(xx, mm, is_lat, states):
            tm = 512 if is_lat else lc
            z5, z5g, zt, zg, pre = _inproj(xx, n1, mm[:, 1], mm[:, 0], w5, wr, wt, wg,
                                           rope_lat if is_lat else ident, rwp, is_lat, tm)
            y5, h5 = _s5_scan(z5g, s5_tab, states[0])
            yrf, yrb, srw = _rw_scan(pre, states[1])
            otf, otb, sret = _ret_scan(zt, ret_tab, states[2])
            ogf, ogb, sgla = _gla_scan(zg, gla_aup, gla_ab, states[3])
            outs = (z5, y5, pre, (yrf, yrb), zt, (otf, otb), zg, (ogf, ogb))
            return outs, (h5, srw, sret, sgla)

        def block(xx, mm, outs, is_lat, fin):
            tm = 512 if is_lat else lc
            z5, y5, pre, yr, zt, ot, zg, og = outs
            x1 = _mix(xx, mm[:, 2], y5, z5, pre, yr, zt, ot, zg, og, mixp, tm)
            return _mlp(x1, n2, mm[:, 4], mm[:, 3], mm[:, 5], w1_b, w2_b, i, row2(final_g), fin, tm)

        zeros = (jnp.zeros((b, 2, S5_G, 1, 2 * S5_N), F32),
                 jnp.zeros((b, 2, HEADS // 2, HEAD_V, LANES), F32),
                 jnp.zeros((b, 2, HEADS // 2, HEAD_V, LANES), F32),
                 jnp.zeros((b, 2, HEAD_V, GLA_QK), F32))
        outs_c, st_c = mixers(xc, mc, False, zeros)
        outs_l, _ = mixers(x, ml, True, st_c)
        x = block(x, ml, outs_l, True, last)
        if not last:
            xc = block(xc, mc, outs_c, False, False)
    return x
```

```python
import functools
import math

import jax
import jax.numpy as jnp
from jax import lax
from jax.experimental import pallas as pl
from jax.experimental.pallas import tpu as pltpu

F32 = jnp.float32
BF16 = jnp.bfloat16

LANES = 128
GRID_W = 64
GROUP_W = 256
N_ADA = 6
EPS = 1e-6
GN_EPS = 64e-5
TOKEN_TILE = 512
CHUNK = 64
RW_CPB = 4
RET_T = 256
RET_BLOCK = 512
GLA_T = 128
GLA_BLOCK = 512
HEADS = 4
HEAD_V = 64
S5_P = 16
S5_G = 16
S5_N = 64
S5_T = 32
S5_TP = S5_T * S5_P
RW_COLS = 896
RW_W_RANK = 32
RW_A_RANK = 32
RW_G_RANK = 64
RET_DK = 64
GLA_DK = 32
GLA_QK = 128
GLA_RANK = 16
GLA_TAU = 16.0
GLA_COLS_PAD = 896
ROPE_BASE = 10000.0
VMEM_LIMIT = 56 * 1024 * 1024


def _cparams(*sem):
    return pltpu.CompilerParams(dimension_semantics=sem, vmem_limit_bytes=VMEM_LIMIT)


def _mm(a, b):
    return jnp.dot(a.astype(BF16), b.astype(BF16), preferred_element_type=F32)


def _mm_nt(a, b):
    return lax.dot_general(a.astype(BF16), b.astype(BF16), (((1,), (1,)), ((), ())),
                           preferred_element_type=F32)


def _mm_tn(a, b):
    return lax.dot_general(a.astype(BF16), b.astype(BF16), (((0,), (0,)), ((), ())),
                           preferred_element_type=F32)


def _split2(x):
    hi = x.astype(BF16)
    return hi, (x - hi.astype(F32)).astype(BF16)


def _chunk_tri01(n, t, reverse):
    ri = lax.broadcasted_iota(jnp.int32, (n, n), 0)
    ci = lax.broadcasted_iota(jnp.int32, (n, n), 1)
    shift = t.bit_length() - 1
    tri = (ci >= ri) if reverse else (ci <= ri)
    return jnp.where((ri >> shift) == (ci >> shift), jnp.where(tri, 1.0, 0.0), 0.0).astype(BF16)


def _mm_left01(m01, x):
    hi, lo = _split2(x)
    return jnp.dot(m01, hi, preferred_element_type=F32) + jnp.dot(m01, lo, preferred_element_type=F32)


def _mm_right01(x, m01):
    return jnp.dot(x.astype(BF16), m01, preferred_element_type=F32)


def _sigmoid(x):
    return 1.0 / (1.0 + jnp.exp(-x))


def _softplus(x):
    return jnp.maximum(x, 0.0) + jnp.log(1.0 + jnp.exp(-jnp.abs(x)))


def _silu(x):
    return x * _sigmoid(x)


def _gelu_tanh(x):
    return 0.5 * x * (1.0 + jnp.tanh(math.sqrt(2.0 / math.pi) * (x + 0.044715 * x * x * x)))


def _tri_incl(t, reverse):
    ri = lax.broadcasted_iota(jnp.int32, (t, t), 0)
    ci = lax.broadcasted_iota(jnp.int32, (t, t), 1)
    return (ci >= ri) if reverse else (ci <= ri)


def _cast_kernel(x_ref, o_ref):
    o_ref[...] = x_ref[...].astype(o_ref.dtype)


def _to_bf16(w):
    depth, r, c = w.shape
    tr = 256
    spec = pl.BlockSpec((1, tr, c), lambda i, j: (i, j, 0))
    return pl.pallas_call(
        _cast_kernel,
        grid=(depth, r // tr),
        in_specs=[spec],
        out_specs=spec,
        out_shape=jax.ShapeDtypeStruct(w.shape, BF16),
        compiler_params=_cparams("parallel", "parallel"),
        name="cast_bf16",
    )(w)


def _mod_kernel(cond_ref, w_ref, b_ref, o_ref):
    c = cond_ref[...]
    o_ref[0] = _mm(_silu(c), w_ref[0]) + b_ref[0]


def _modulation(cond, ada_w, ada_b):
    depth, d, n = ada_w.shape
    tn = 1536
    return pl.pallas_call(
        _mod_kernel,
        grid=(depth, n // tn),
        in_specs=[pl.BlockSpec((8, d), lambda i, j: (0, 0)),
                  pl.BlockSpec((1, d, tn), lambda i, j: (i, 0, j)),
                  pl.BlockSpec((1, 1, tn), lambda i, j: (i, 0, j))],
        out_specs=pl.BlockSpec((1, 8, tn), lambda i, j: (i, 0, j)),
        out_shape=jax.ShapeDtypeStruct((depth, 8, n), F32),
        compiler_params=_cparams("parallel", "parallel"),
        name="adaln_mod",
    )(cond, ada_w, ada_b.reshape(depth, 1, n))


def _modnorm(x, g, sc, sh):
    ms = jnp.mean(x * x, axis=-1, keepdims=True)
    return x * lax.rsqrt(ms + EPS) * g * (1.0 + sc) + sh


def _shift_columns(z, segments):
    tm, width = z.shape
    lane = lax.broadcasted_iota(jnp.int32, (tm, LANES), 1)
    starts = [s for s, _ in segments] + [width]
    cols = []
    for c in range(width // LANES):
        lo, hi = c * LANES, (c + 1) * LANES
        inside = [(max(starts[i], lo), segments[i][1]) for i in range(len(segments))
                  if starts[i] < hi and starts[i + 1] > lo]
        col = inside[-1][1](c)
        for first, fn in reversed(inside[:-1]):
            nxt = [f for f, _ in inside if f > first][0]
            col = jnp.where(lane < nxt - lo, fn(c), col)
        cols.append(col)
    return jnp.concatenate(cols, axis=1)


def _rw_prep_math(grid_shift, z, zp, zn, mu, w0_ref, wup_ref, a0_ref, aup_ref, gup, kkp, ka, rk, bones):
    tm = z.shape[0]
    row = lax.broadcasted_iota(jnp.int32, (tm, LANES), 0)
    col = lambda x, c: x[:, c * LANES:(c + 1) * LANES]
    if grid_shift:
        pos = row & (GRID_W - 1)
        left = lambda c: jnp.where(pos == 0, 0.0, pltpu.roll(col(z, c), 1, 0))
        right = lambda c: jnp.where(pos == GRID_W - 1, 0.0, pltpu.roll(col(z, c), tm - 1, 0))
        up = lambda c: jnp.concatenate([col(zp, c), col(z, c)[:tm - GRID_W]], axis=0)
        down = lambda c: jnp.concatenate([col(z, c)[GRID_W:], col(zn, c)], axis=0)
        q = RW_COLS // 4
        shifted = _shift_columns(z, [(0, left), (q, right), (2 * q, up), (3 * q, down)])
    else:
        prev = lambda c: jnp.where(row == 0, 0.0, pltpu.roll(col(z, c), 1, 0))
        nxt = lambda c: jnp.where(row == tm - 1, 0.0, pltpu.roll(col(z, c), tm - 1, 0))
        shifted = _shift_columns(z, [(0, prev), (RW_COLS // 2, nxt)])
    zm = z + mu * (shifted - z)
    r = zm[:, 0:GROUP_W]
    k = zm[:, GROUP_W:2 * GROUP_W]
    v = zm[:, 2 * GROUP_W:3 * GROUP_W]
    lo = zm[:, 3 * GROUP_W:RW_COLS]
    g = _mm(_sigmoid(lo), gup)
    kk = k * kkp
    kk = kk * lax.rsqrt(_mm_right01(kk * kk, bones) + 1e-12)
    bonus = _mm_right01(r * k * rk, bones) * v
    outs = [r.astype(BF16), v.astype(BF16), kk.astype(BF16), g.astype(BF16), bonus.astype(BF16)]
    tlo = jnp.tanh(lo)
    for d in range(2):
        w_raw = -_softplus(-(w0_ref[d] + _mm(tlo, wup_ref[d]))) - 0.5
        a = _sigmoid(a0_ref[d] + _mm(lo, aup_ref[d]))
        outs += [-jnp.exp(w_raw),
                 (k * (1.0 + (a - 1.0) * ka)).astype(BF16), a.astype(BF16)]
    return outs


def _inproj_kernel(grid_shift, nt, x_ref, xp_ref, xn_ref, g_ref, sc_ref, sh_ref,
                   w5_ref, wr_ref, wt_ref, wg_ref, cr_ref, sr_ref, cc_ref, sn_ref,
                   mu_ref, w0_ref, wup_ref, a0_ref, aup_ref, gup_ref, kk_ref, ka_ref, rk_ref, bones_ref,
                   o5_ref, o5g_ref, ot_ref, og_ref, *rest):
    rw_refs, z5h_ref = rest[:-1], rest[-1]
    j = pl.program_id(1)
    norm = lambda x: _modnorm(x, g_ref[...], sc_ref[0], sh_ref[0]).astype(BF16)
    hb = norm(x_ref[0])
    z5 = jnp.dot(hb, w5_ref[...], preferred_element_type=F32)
    o5_ref[0] = z5
    og_ref[0] = jnp.dot(hb, wg_ref[...], preferred_element_type=F32)
    zr = jnp.dot(hb, wr_ref[...], preferred_element_type=F32)
    zp = zn = None
    if grid_shift:
        zp = jnp.where(j > 0, jnp.dot(norm(xp_ref[0]), wr_ref[...], preferred_element_type=F32), 0.0)
        zn = jnp.where(j < nt - 1, jnp.dot(norm(xn_ref[0]), wr_ref[...], preferred_element_type=F32), 0.0)
    rw = _rw_prep_math(grid_shift, zr, zp, zn, mu_ref[...], w0_ref, wup_ref, a0_ref, aup_ref, gup_ref[...],
                       kk_ref[...], ka_ref[...], rk_ref[...], bones_ref[...])
    for ref, val in zip(rw_refs, rw):
        ref[0] = val
    zt = jnp.dot(hb, wt_ref[...], preferred_element_type=F32)
    lane = lax.broadcasted_iota(jnp.int32, (GRID_W, GROUP_W), 1)
    by_row = (lane & 32) == 0
    nrow = cr_ref.shape[0]
    cos = jnp.concatenate([jnp.where(by_row, cr_ref[j], cc_ref[...]) for j in range(nrow)], axis=0)
    sin = jnp.concatenate([jnp.where(by_row, sr_ref[j], sn_ref[...]) for j in range(nrow)], axis=0)
    ot_ref[0, :, 0:GROUP_W] = _rope(zt[:, 0:GROUP_W], cos, sin)
    ot_ref[0, :, GROUP_W:2 * GROUP_W] = _rope(zt[:, GROUP_W:2 * GROUP_W] * RET_DK ** -0.5, cos, sin)
    ot_ref[0, :, 2 * GROUP_W:] = zt[:, 2 * GROUP_W:]
    nch = o5g_ref.shape[2]
    gph = LANES // S5_P
    for hf in range(GROUP_W // LANES):
        z5h_ref[hf] = z5[:, hf * LANES:(hf + 1) * LANES]
    for s in range(S5_T):
        for hf in range(GROUP_W // LANES):
            rows = z5h_ref[hf, pl.ds(s, nch, stride=S5_T), :]
            for g in range(gph):
                o5g_ref[0, hf * gph + g, :, s * S5_P:(s + 1) * S5_P] = rows[:, g * S5_P:(g + 1) * S5_P]


def _inproj(x, g, sc, sh, w5, wr, wt, wg, rope, rwp, grid_shift, tm):
    b, l, d = x.shape
    nt = l // tm
    hb = tm // GRID_W
    nhb = l // GRID_W
    tok = lambda n: pl.BlockSpec((1, tm, n), lambda i, j: (i, j, 0))
    vec = pl.BlockSpec((1, 1, d), lambda i, j: (i, 0, 0))
    full = lambda a: pl.BlockSpec(a.shape, lambda i, j: (0,) * a.ndim)
    grouped = pl.BlockSpec((1, S5_G, tm // S5_T, S5_TP), lambda i, j: (i, 0, j, 0))
    rowt = pl.BlockSpec((hb, 1, GROUP_W), lambda i, j: (j, 0, 0))
    above = pl.BlockSpec((1, GRID_W, d), lambda i, j: (i, jnp.maximum(j * hb - 1, 0), 0))
    below = pl.BlockSpec((1, GRID_W, d), lambda i, j: (i, jnp.minimum((j + 1) * hb, nhb - 1), 0))
    cr, sr, cc, sn = rope
    params = (rwp['mu'], rwp['w0'], rwp['wup'], rwp['a0'], rwp['aup'], rwp['gup'], rwp['kk'], rwp['ka'],
              rwp['rk'], rwp['bones'])
    rw_dtypes = (BF16,) * 5 + (F32, BF16, BF16) * 2
    outs = pl.pallas_call(
        functools.partial(_inproj_kernel, grid_shift, nt),
        grid=(b, nt),
        in_specs=[tok(d), above, below, pl.BlockSpec((1, d), lambda i, j: (0, 0)), vec, vec,
                  full(w5), full(wr), full(wt), full(wg), rowt, rowt, full(cc), full(sn)]
                 + [full(a) for a in params],
        out_specs=[tok(GROUP_W), grouped, tok(wt.shape[1]), tok(wg.shape[1])] + [tok(GROUP_W)] * 11,
        out_shape=[jax.ShapeDtypeStruct((b, l, GROUP_W), F32),
                   jax.ShapeDtypeStruct((b, S5_G, l // S5_T, S5_TP), F32),
                   jax.ShapeDtypeStruct((b, l, wt.shape[1]), F32),
                   jax.ShapeDtypeStruct((b, l, wg.shape[1]), F32)]
                  + [jax.ShapeDtypeStruct((b, l, GROUP_W), dt) for dt in rw_dtypes],
        scratch_shapes=[pltpu.VMEM((GROUP_W // LANES, tm, LANES), F32)],
        compiler_params=_cparams("parallel", "parallel"),
        name="norm_inproj",
    )(x, x, x, g, sc, sh, w5, wr, wt, wg, cr, sr, cc, sn, *params)
    return outs[0], outs[1], outs[2], outs[3], tuple(outs[4:])


def _s5_tables(lam_re, lam_im, log_dt, b_re, b_im, c_re, c_im, nlev):
    hp = lax.Precision.HIGHEST
    t = S5_T
    lam = lax.complex(jnp.minimum(lam_re.astype(F32), -1e-4), lam_im.astype(F32))
    ldt = lam * jnp.exp(log_dt.astype(F32))[..., None]
    a_bar = jnp.exp(ldt)
    bb = ((a_bar - 1.0) / lam)[..., None] * lax.complex(b_re.astype(F32), b_im.astype(F32))
    cm = lax.complex(c_re.astype(F32), c_im.astype(F32))
    tau = jnp.arange(t + 1, dtype=F32)
    apow = jnp.exp(ldt[:, :, None, :] * tau[None, None, :, None])
    taps = jnp.einsum('dgpn,dgtn,dgnq->dgtpq', cm, apow[:, :, :t], bb, precision=hp).real
    taprow = jnp.stack([taps[0], taps[1][:, ::-1]]).transpose(0, 1, 4, 2, 3).reshape(2, S5_G, S5_P, S5_TP)

    def pack(zc):
        return jnp.concatenate([zc.real, zc.imag], axis=-1)

    win_f = apow[0][:, t - 1 - jnp.arange(t), None, :] * bb[0].transpose(0, 2, 1)[:, None]
    win_b = apow[1][:, jnp.arange(t), None, :] * bb[1].transpose(0, 2, 1)[:, None]
    win = jnp.stack([pack(win_f), pack(win_b)]).reshape(2, S5_G, S5_TP, 2 * S5_N)
    ca_f = cm[0][:, None] * apow[0][:, 1 + jnp.arange(t), None, :]
    ca_b = cm[1][:, None] * apow[1][:, t - jnp.arange(t), None, :]

    def outpack(ca):
        w = jnp.concatenate([ca.real, -ca.imag], axis=-1)
        return w.reshape(S5_G, S5_TP, 2 * S5_N).transpose(0, 2, 1)

    wout = jnp.stack([outpack(ca_f), outpack(ca_b)])
    lev = (2.0 ** jnp.arange(nlev, dtype=F32)) * t
    pw = jnp.exp(ldt[:, :, None, :] * lev[None, None, :, None])
    p1 = jnp.concatenate([pw.real, pw.real], axis=-1)
    p2 = jnp.concatenate([-pw.imag, pw.imag], axis=-1)
    pw = jnp.stack([p1, p2], axis=3)
    return taprow, win.astype(BF16), wout.astype(BF16), pw


def _s5_kernel(nc, nlev, u_ref, tap_ref, win_ref, wout_ref, pw_ref, h0_ref, y_ref, hfin_ref, conv_ref):
    u = u_ref[0, 0].astype(BF16)
    row = lax.broadcasted_iota(jnp.int32, (nc, 2 * S5_N), 0)
    lane = lax.broadcasted_iota(jnp.int32, (S5_P, S5_TP), 1)
    for s in range(S5_T):
        lo = s * S5_P
        fwd = tap_ref[0, 0] if s == 0 else jnp.where(lane >= lo, pltpu.roll(tap_ref[0, 0], lo, 1), 0.0)
        hi = lo + S5_P
        bwd = tap_ref[1, 0] if hi == S5_TP else jnp.where(lane < hi, pltpu.roll(tap_ref[1, 0], hi, 1), 0.0)
        conv_ref[0, lo:hi, :] = fwd.astype(BF16)
        conv_ref[1, lo:hi, :] = bwd.astype(BF16)

    def cmul(x, d, j):
        return pw_ref[d, 0, j, 0:1] * x + pw_ref[d, 0, j, 1:2] * pltpu.roll(x, S5_N, 1)

    v = [jnp.dot(u, win_ref[d, 0], preferred_element_type=F32) for d in range(2)]
    x = [jnp.where(row == 0, h0_ref[0, 0, 0], pltpu.roll(v[0], 1, 0)),
         jnp.where(row == nc - 1, h0_ref[0, 1, 0], pltpu.roll(v[1], nc - 1, 0))]
    for j in range(nlev):
        sh = 2 ** j
        xs = [jnp.where(row >= sh, pltpu.roll(x[0], sh, 0), 0.0),
              jnp.where(row < nc - sh, pltpu.roll(x[1], nc - sh, 0), 0.0)]
        x = [x[d] + cmul(xs[d], d, j) for d in range(2)]
    for d, last in enumerate((nc - 1, 0)):
        hfin_ref[0, d, 0] = cmul(x[d][last:last + 1], d, 0) + v[d][last:last + 1]
    y_ref[0, 0] = (jnp.dot(u, conv_ref[0], preferred_element_type=F32) + _mm(x[0], wout_ref[0, 0])
                   + jnp.dot(u, conv_ref[1], preferred_element_type=F32) + _mm(x[1], wout_ref[1, 0]))


def _s5_scan(uf, tables, h0):
    conv, win, wout, pw = tables
    b, _, nc, _ = uf.shape
    nlev = max(1, (nc - 1).bit_length())
    pw = pw[:, :, :nlev]
    n2 = 2 * S5_N
    y, hfin = pl.pallas_call(
        functools.partial(_s5_kernel, nc, nlev),
        grid=(b, S5_G),
        in_specs=[pl.BlockSpec((1, 1, nc, S5_TP), lambda i, g: (i, g, 0, 0)),
                  pl.BlockSpec((2, 1, S5_P, S5_TP), lambda i, g: (0, g, 0, 0)),
                  pl.BlockSpec((2, 1, S5_TP, n2), lambda i, g: (0, g, 0, 0)),
                  pl.BlockSpec((2, 1, n2, S5_TP), lambda i, g: (0, g, 0, 0)),
                  pl.BlockSpec((2, 1, nlev, 2, n2), lambda i, g: (0, g, 0, 0, 0)),
                  pl.BlockSpec((1, 2, 1, 1, n2), lambda i, g: (i, 0, g, 0, 0))],
        out_specs=[pl.BlockSpec((1, 1, nc, S5_TP), lambda i, g: (i, g, 0, 0)),
                   pl.BlockSpec((1, 2, 1, 1, n2), lambda i, g: (i, 0, g, 0, 0))],
        out_shape=[jax.ShapeDtypeStruct((b, S5_G, nc, S5_TP), F32),
                   jax.ShapeDtypeStruct((b, 2, S5_G, 1, n2), F32)],
        scratch_shapes=[pltpu.VMEM((2, S5_TP, S5_TP), BF16)],
        compiler_params=_cparams("parallel", "parallel"),
        name="s5_scan",
    )(uf, conv, win, wout, pw, h0)
    return y, hfin


def _rw_scan_kernel(nb, cpb, rf_ref, vf_ref, kkf_ref, lwf_ref, kdf_ref, asf_ref,
                    rb_ref, vb_ref, kkb_ref, lwb_ref, kdb_ref, asb_ref, s0_ref,
                    yf_ref, yb_ref, sfin_ref, st_ref):
    c = pl.program_id(1)

    @pl.when(c == 0)
    def _():
        st_ref[...] = s0_ref[0]

    t = CHUNK
    ri = lax.broadcasted_iota(jnp.int32, (t, LANES), 0)
    li = lax.broadcasted_iota(jnp.int32, (t, LANES), 1)
    ci = li & (t - 1)
    low = li < HEAD_V
    eyef = jnp.where(ri == ci, 1.0, 0.0)
    same = lambda s: jnp.where((ri >> s) == (ci >> s), 1.0, 0.0)
    m4, m8, m16, m32 = same(2), same(3), same(4), same(5)
    merge_masks = (m8 - m4, m16 - m8, m32 - m16, 1.0 - m32)

    def bd(x):
        return jnp.concatenate([jnp.where(low, x, 0.0), jnp.where(low, 0.0, x)], axis=0)

    def diag_blocks(full):
        return jnp.where(low, full[0:HEAD_V], full[HEAD_V:2 * HEAD_V])

    dir_refs = ((rf_ref, vf_ref, kkf_ref, lwf_ref, kdf_ref, asf_ref),
                (rb_ref, vb_ref, kkb_ref, lwb_ref, kdb_ref, asb_ref))
    units = []
    for d, (r_ref, v_ref, kk_ref, lw_ref, kd_ref, as_ref) in enumerate(dir_refs):
        incl = (ci >= ri) if d == 1 else (ci <= ri)
        strict = (ci > ri) if d == 1 else (ci < ri)
        r, v, kk, kd = (x[0].astype(F32) for x in (r_ref, v_ref, kk_ref, kd_ref))
        lw = lw_ref[0]
        cin = _mm_left01(_chunk_tri01(cpb * t, t, d == 1), lw)
        e_in = jnp.exp(cin)
        e_neg = jnp.exp(-cin)
        rt = r * e_in
        at = -kk * jnp.exp(cin - lw)
        bvec = kk * as_ref[0].astype(F32)
        bt = bvec * e_neg
        kt = kd * e_neg
        for j in range(cpb):
            rows = slice(j * t, (j + 1) * t)
            last = j * t + (0 if d == 1 else t - 1)
            clast = cin[last:last + 1]
            dl = jnp.exp(clast - cin[rows])
            bh = bvec[rows] * dl
            kh = kd[rows] * dl
            dec = jnp.exp(clast)
            for p in range(HEADS // 2):
                sl = slice(p * LANES, (p + 1) * LANES)
                units.append(dict(d=d, j=j, p=p, incl=incl, strict=strict,
                                  at=at[rows, sl], rt=rt[rows, sl], bt=bt[rows, sl], kt=kt[rows, sl],
                                  bh=bh[:, sl], kh=kh[:, sl], v=v[rows, sl], dec=dec[:, sl]))

    x1 = [jnp.concatenate([u['at'], u['rt']], axis=0) for u in units]
    ab_rb = [_mm_nt(x, bd(u['bt'])) for x, u in zip(x1, units)]
    ak_rk = [_mm_nt(x, bd(u['kt'])) for x, u in zip(x1, units)]
    nmat = [jnp.where(u['strict'], a[0:t], 0.0) for u, a in zip(units, ab_rb)]
    a_rb = [jnp.where(u['incl'], a[t:2 * t], 0.0) for u, a in zip(units, ab_rb)]
    a_kk = [jnp.concatenate([jnp.where(u['strict'], a[0:t], 0.0), jnp.where(u['incl'], a[t:2 * t], 0.0)], axis=0)
            for u, a in zip(units, ak_rk)]
    akv = [_mm(a, bd(u['v'])) for u, a in zip(units, a_kk)]
    kv = [diag_blocks(_mm_tn(u['v'], u['kh'])) for u in units]
    nd = [x * m4 for x in nmat]
    n2 = [_mm(x, bd(x)) for x in nd]
    tinv = [eyef + x + _mm(eyef + x, bd(y)) for x, y in zip(nd, n2)]
    for mk in merge_masks:
        w = [_mm(ti, bd(x * mk)) for ti, x in zip(tinv, nmat)]
        tinv = [ti + _mm(wi, bd(ti)) for ti, wi in zip(tinv, w)]
    zz = [_mm(ti, jnp.concatenate([bd(u['at']), bd(kvv[0:t])], axis=1))
          for ti, u, kvv in zip(tinv, units, akv)]
    ght = [_mm_tn(z, u['bh']) for u, z in zip(units, zz)]
    qy = [_mm(a, jnp.concatenate([bd(z[:, :LANES]), bd(z[:, LANES:])], axis=1))
          for a, z in zip(a_rb, zz)]
    qmat = [u['rt'] + x[:, :LANES] for u, x in zip(units, qy)]
    y0 = [x[:, LANES:] + kvv[t:2 * t] for x, kvv in zip(qy, akv)]
    gmat = [diag_blocks(x[:LANES]) for x in ght]
    hmat = [diag_blocks(x[LANES:]) + k2 for x, k2 in zip(ght, kv)]
    idx = {(u['d'], u['j'], u['p']): n for n, u in enumerate(units)}
    npair = HEADS // 2
    state = {(d, p): st_ref[d, p] for d in range(2) for p in range(npair)}
    ys = {}
    for step in range(cpb):
        for d in range(2):
            j = step if d == 0 else cpb - 1 - step
            for p in range(npair):
                n = idx[(d, j, p)]
                st = state[(d, p)]
                ys[(d, j, p)] = _mm_nt(qmat[n], bd(st)) + y0[n]
                state[(d, p)] = units[n]['dec'] * st + _mm(st, bd(gmat[n])) + hmat[n]
    for d, y_ref in enumerate((yf_ref, yb_ref)):
        y_ref[0] = jnp.concatenate(
            [jnp.concatenate([ys[(d, j, p)] for p in range(npair)], axis=1) for j in range(cpb)], axis=0)
        for p in range(npair):
            st_ref[d, p] = state[(d, p)]

    @pl.when(c == nb - 1)
    def _():
        sfin_ref[0] = st_ref[...]


def _rw_scan(pre, s0):
    r, v, kk, _, _, lw0, kd0, as0, lw1, kd1, as1 = pre
    b, l, _ = r.shape
    cpb = min(RW_CPB, l // CHUNK)
    nb = l // (cpb * CHUNK)
    fw = pl.BlockSpec((1, cpb * CHUNK, GROUP_W), lambda i, c: (i, c, 0))
    bw = pl.BlockSpec((1, cpb * CHUNK, GROUP_W), lambda i, c: (i, nb - 1 - c, 0))
    st = pl.BlockSpec((1, 2, HEADS // 2, HEAD_V, LANES), lambda i, c: (i, 0, 0, 0, 0))
    return pl.pallas_call(
        functools.partial(_rw_scan_kernel, nb, cpb),
        grid=(b, nb),
        in_specs=[fw] * 6 + [bw] * 6 + [st],
        out_specs=[fw, bw, st],
        out_shape=[jax.ShapeDtypeStruct((b, l, GROUP_W), F32)] * 2
                  + [jax.ShapeDtypeStruct((b, 2, HEADS // 2, HEAD_V, LANES), F32)],
        scratch_shapes=[pltpu.VMEM((2, HEADS // 2, HEAD_V, LANES), F32)],
        compiler_params=_cparams("parallel", "arbitrary"),
        name="rwkv_scan",
    )(r, v, kk, lw0, kd0, as0, r, v, kk, lw1, kd1, as1, s0)


def _rope(x, cos, sin_signed):
    lane = lax.broadcasted_iota(jnp.int32, x.shape, 1)
    n = x.shape[1]
    swapped = jnp.where((lane & 16) == 0, pltpu.roll(x, n - 16, 1), pltpu.roll(x, 16, 1))
    return x * cos + swapped * sin_signed


def _ret_scan_kernel(nb, cpb, qf_ref, kf_ref, vf_ref, qb_ref, kb_ref, vb_ref,
                     dmat_ref, qdec_ref, kdec_ref, sdec_ref, s0_ref,
                     of_ref, ob_ref, sfin_ref, st_ref):
    c = pl.program_id(1)

    @pl.when(c == 0)
    def _():
        st_ref[...] = s0_ref[0]

    t = dmat_ref.shape[-1]
    npair = HEADS // 2
    low_t = lax.broadcasted_iota(jnp.int32, (t, LANES), 1) < HEAD_V
    low_s = lax.broadcasted_iota(jnp.int32, (HEAD_V, LANES), 1) < HEAD_V

    def bd(x, low):
        return jnp.concatenate([jnp.where(low, x, 0.0), jnp.where(low, 0.0, x)], axis=0)

    units = []
    for d, (q_ref, k_ref, v_ref) in enumerate(((qf_ref, kf_ref, vf_ref), (qb_ref, kb_ref, vb_ref))):
        for j in range(cpb):
            rows = slice(j * t, (j + 1) * t)
            q, k, v = q_ref[0, rows], k_ref[0, rows], v_ref[0, rows]
            qd = q * qdec_ref[d]
            kh = k * kdec_ref[d]
            for p in range(npair):
                sl = slice(p * LANES, (p + 1) * LANES)
                units.append(dict(d=d, j=j, p=p, q=q[:, sl], k=k[:, sl], qd=qd[:, sl], kh=kh[:, sl],
                                  v=v[:, sl], dec=sdec_ref[d][:, sl]))
    scores = [[_mm_nt(jnp.where(low_t == (h == 0), u['q'], 0.0), u['k']).astype(BF16)
               * dmat_ref[u['d'], 2 * u['p'] + h] for h in range(2)] for u in units]
    intra = [_mm(jnp.concatenate(s2, axis=1), bd(u['v'], low_t)) for s2, u in zip(scores, units)]
    kv = []
    for u in units:
        full = _mm_tn(u['v'], u['kh'])
        kv.append(jnp.where(low_s, full[0:HEAD_V], full[HEAD_V:2 * HEAD_V]))
    idx = {(u['d'], u['j'], u['p']): i for i, u in enumerate(units)}
    entering = {}
    for d in range(2):
        for p in range(npair):
            st = st_ref[d, p]
            for step in range(cpb):
                j = step if d == 0 else cpb - 1 - step
                i = idx[(d, j, p)]
                entering[i] = st
                st = units[i]['dec'] * st + kv[i]
            st_ref[d, p] = st
    outs = [x + _mm_nt(u['qd'], bd(entering[i], low_s)) for i, (x, u) in enumerate(zip(intra, units))]
    for d, o_ref in enumerate((of_ref, ob_ref)):
        o_ref[0] = jnp.concatenate(
            [jnp.concatenate([outs[idx[(d, j, p)]] for p in range(npair)], axis=1) for j in range(cpb)], axis=0)

    @pl.when(c == nb - 1)
    def _():
        sfin_ref[0] = st_ref[...]


def _ret_tables(decay_logit, n):
    lg = jax.nn.log_sigmoid(decay_logit.astype(F32))
    pos = jnp.arange(n, dtype=F32)
    lag = pos[:, None] - pos[None, :]
    lag = jnp.stack([lag, -lag])
    dmat = jnp.where(lag[:, None] >= 0, jnp.exp(lg[:, :, None, None] * lag[:, None]), 0.0)
    lanes = jnp.repeat(lg, RET_DK, axis=-1)[:, None, :]
    qpow = jnp.stack([pos + 1.0, n - pos])[:, :, None]
    kpow = jnp.stack([n - 1.0 - pos, pos])[:, :, None]
    return dmat.astype(BF16), jnp.exp(lanes * qpow), jnp.exp(lanes * kpow), jnp.exp(lanes * n)


def _gla_scan_kernel(nb, cpb, qf_ref, kf_ref, vf_ref, af_ref, qb_ref, kb_ref, vb_ref, ab_ref,
                     aup_ref, abias_ref, s0_ref, of_ref, ob_ref, sfin_ref, st_ref):
    c = pl.program_id(1)

    @pl.when(c == 0)
    def _():
        st_ref[...] = s0_ref[0]

    t = GLA_T
    n = cpb * t
    dk = GLA_DK
    scale = dk ** -0.5
    dirs = ((qf_ref, kf_ref, vf_ref, af_ref), (qb_ref, kb_ref, vb_ref, ab_ref))
    units = []
    for d, (q_ref, k_ref, v_ref, a_ref) in enumerate(dirs):
        incl = _tri_incl(t, d == 1)
        q, k, v = q_ref[0], k_ref[0] * scale, v_ref[0]
        lw = -_softplus(-(_mm(a_ref[0], aup_ref[d]) + abias_ref[d])) * (1.0 / GLA_TAU)
        cin = _mm_left01(_chunk_tri01(n, t, d == 1), lw)
        qe = q * jnp.exp(cin)
        for j in range(cpb):
            rows = slice(j * t, (j + 1) * t)
            last = j * t + (0 if d == 1 else t - 1)
            mid = j * t + (t // 2 if d == 1 else t // 2 - 1)
            cj = cin[rows]
            clast = cin[last:last + 1]
            cmid = cin[mid:mid + 1]
            qt = q[rows] * jnp.exp(cj - cmid)
            kt = k[rows] * jnp.exp(cmid - cj)
            kh = k[rows] * jnp.exp(clast - cj)
            units.append(dict(d=d, j=j, incl=incl, qt=qt, kt=kt, qe=qe[rows], kh=kh, v=v[rows],
                              dec=jnp.exp(clast)))

    klane = lax.broadcasted_iota(jnp.int32, (t, LANES), 1) // dk
    vlow = lax.broadcasted_iota(jnp.int32, (t, LANES), 1) < HEAD_V
    khead = lax.broadcasted_iota(jnp.int32, (HEAD_V, LANES), 1) // dk
    npair = HEADS // 2

    def v_blockdiag(x):
        return jnp.concatenate([jnp.where(vlow, x, 0.0), jnp.where(vlow, 0.0, x)], axis=0)

    def state_rows(st, p):
        return jnp.concatenate([jnp.where(khead == 2 * p, st, 0.0), jnp.where(khead == 2 * p + 1, st, 0.0)],
                               axis=0)

    amat = [[jnp.where(u['incl'], _mm_nt(jnp.where(klane == h, u['qt'], 0.0), u['kt']), 0.0)
             for h in range(HEADS)] for u in units]
    intra = [[_mm(jnp.concatenate(a[2 * p:2 * p + 2], axis=1), v_blockdiag(u['v'][:, p * LANES:(p + 1) * LANES]))
              for p in range(npair)] for a, u in zip(amat, units)]
    kv = []
    for u in units:
        full = _mm_tn(u['v'], u['kh'])
        blocks = [full[h * HEAD_V:(h + 1) * HEAD_V] for h in range(HEADS)]
        acc = blocks[HEADS - 1]
        for h in range(HEADS - 2, -1, -1):
            acc = jnp.where(khead == h, blocks[h], acc)
        kv.append(acc)
    idx = {(u['d'], u['j']): i for i, u in enumerate(units)}
    entering = {}
    for d in range(2):
        st = st_ref[d]
        for step in range(cpb):
            j = step if d == 0 else cpb - 1 - step
            i = idx[(d, j)]
            entering[i] = st
            st = units[i]['dec'] * st + kv[i]
        st_ref[d] = st
    outs = [[x[p] + _mm_nt(u['qe'], state_rows(entering[i], p)) for p in range(npair)]
            for i, (x, u) in enumerate(zip(intra, units))]
    for d, o_ref in enumerate((of_ref, ob_ref)):
        o_ref[0] = jnp.concatenate(
            [jnp.concatenate(outs[idx[(d, j)]], axis=1) for j in range(cpb)], axis=0)

    @pl.when(c == nb - 1)
    def _():
        sfin_ref[0] = st_ref[...]


def _rope_tables(rows):
    nf = RET_DK // 4
    inv = ROPE_BASE ** (-jnp.arange(nf, dtype=F32) / nf)
    lane = jnp.arange(GROUP_W)
    freq = inv[lane % nf]
    sign = jnp.where((lane & 16) == 0, -1.0, 1.0)
    ar = jnp.arange(rows, dtype=F32)[:, None] * freq[None, :]
    ac = jnp.arange(GRID_W, dtype=F32)[:, None] * freq[None, :]
    return (jnp.cos(ar).reshape(rows, 1, GROUP_W), (jnp.sin(ar) * sign).reshape(rows, 1, GROUP_W),
            jnp.cos(ac), jnp.sin(ac) * sign)


def _ret_scan(z, tables, s0):
    b, l, _ = z.shape
    n = min(RET_BLOCK, l)
    nb = l // n
    fw = lambda j: pl.BlockSpec((1, n, GROUP_W), lambda i, c: (i, c, j))
    bw = lambda j: pl.BlockSpec((1, n, GROUP_W), lambda i, c: (i, nb - 1 - c, j))
    full = lambda a: pl.BlockSpec(a.shape, lambda i, c: (0,) * a.ndim)
    st = pl.BlockSpec((1, 2, HEADS // 2, HEAD_V, LANES), lambda i, c: (i, 0, 0, 0, 0))
    return pl.pallas_call(
        functools.partial(_ret_scan_kernel, nb, n // RET_T),
        grid=(b, nb),
        in_specs=[fw(0), fw(1), fw(2), bw(0), bw(1), bw(2)] + [full(a) for a in tables] + [st],
        out_specs=[fw(0), bw(0), st],
        out_shape=[jax.ShapeDtypeStruct((b, l, GROUP_W), F32)] * 2
                  + [jax.ShapeDtypeStruct((b, 2, HEADS // 2, HEAD_V, LANES), F32)],
        scratch_shapes=[pltpu.VMEM((2, HEADS // 2, HEAD_V, LANES), F32)],
        compiler_params=_cparams("parallel", "arbitrary"),
        name="retention_scan",
    )(z, z, z, z, z, z, *tables, s0)


def _gla_scan(z, aup, abias, s0):
    b, l, _ = z.shape
    cpb = min(GLA_BLOCK, l) // CHUNK
    nb = l // (cpb * CHUNK)
    blk = lambda w, j, rev: pl.BlockSpec(
        (1, cpb * CHUNK, w), (lambda i, c: (i, nb - 1 - c, j)) if rev else (lambda i, c: (i, c, j)))
    st = pl.BlockSpec((1, 2, HEAD_V, GLA_QK), lambda i, c: (i, 0, 0, 0))
    ofw = pl.BlockSpec((1, cpb * CHUNK, GROUP_W), lambda i, c: (i, c, 0))
    obw = pl.BlockSpec((1, cpb * CHUNK, GROUP_W), lambda i, c: (i, nb - 1 - c, 0))
    return pl.pallas_call(
        functools.partial(_gla_scan_kernel, nb, cpb * CHUNK // GLA_T),
        grid=(b, nb),
        in_specs=[blk(GLA_QK, 0, False), blk(GLA_QK, 1, False), blk(GROUP_W, 1, False), blk(128, 6, False),
                  blk(GLA_QK, 0, True), blk(GLA_QK, 1, True), blk(GROUP_W, 1, True), blk(128, 6, True),
                  pl.BlockSpec(aup.shape, lambda i, c: (0, 0, 0)),
                  pl.BlockSpec(abias.shape, lambda i, c: (0, 0, 0)), st],
        out_specs=[ofw, obw, st],
        out_shape=[jax.ShapeDtypeStruct((b, l, GROUP_W), F32)] * 2
                  + [jax.ShapeDtypeStruct((b, 2, HEAD_V, GLA_QK), F32)],
        scratch_shapes=[pltpu.VMEM((2, HEAD_V, GLA_QK), F32)],
        compiler_params=_cparams("parallel", "arbitrary"),
        name="gla_scan",
    )(z, z, z, z, z, z, z, z, aup, abias, s0)


def _mix_kernel(x_ref, g1_ref, y5_ref, u5_ref, d5_ref, gw_ref, gb_ref,
                ryf_ref, ryb_ref, rbonus_ref, rg_ref, rlng_ref, rlnb_ref,
                tof_ref, tob_ref, tg_ref, tln_ref,
                gof_ref, gob_ref, gg_ref, gln_ref,
                bones_ref, wo_ref, o_ref, y5t_ref):
    bones = bones_ref[...]
    inv = 1.0 / HEAD_V

    def hmean(a):
        return _mm_right01(a, bones) * inv

    nch = y5_ref.shape[2]
    gph = LANES // S5_P
    y5g = [y5_ref[0, g] for g in range(S5_G)]
    for s in range(S5_T):
        for hf in range(GROUP_W // LANES):
            y5t_ref[hf, pl.ds(s, nch, stride=S5_T), :] = jnp.concatenate(
                [yg[:, s * S5_P:(s + 1) * S5_P] for yg in y5g[hf * gph:(hf + 1) * gph]], axis=1)
    y = jnp.concatenate([y5t_ref[0], y5t_ref[1]], axis=1) + d5_ref[...] * u5_ref[0]
    y = _gelu_tanh(y)
    ya = y * _sigmoid(_mm(y, gw_ref[...]) + gb_ref[...])
    yr = ryf_ref[0] + ryb_ref[0]
    dlt = yr - hmean(yr)
    yn = dlt * lax.rsqrt(hmean(dlt * dlt) + GN_EPS)
    yb = (yn * rlng_ref[...] + rlnb_ref[...] + rbonus_ref[0].astype(F32)) * rg_ref[0].astype(F32)
    ot = tof_ref[0] + tob_ref[0]
    yc = ot * lax.rsqrt(hmean(ot * ot) + EPS) * tln_ref[...] * _silu(tg_ref[0])
    og = gof_ref[0] + gob_ref[0]
    yd = og * lax.rsqrt(hmean(og * og) + EPS) * gln_ref[...] * _silu(gg_ref[0])
    mix = (_mm(ya, wo_ref[0]) + _mm(yb, wo_ref[1]) + _mm(yc, wo_ref[2]) + _mm(yd, wo_ref[3]))
    o_ref[0] = x_ref[0] + g1_ref[0] * mix


def _mix(x, g1, y5, u5, rw_pre, rw_y, zret, ret_o, zgla, gla_o, p, tm):
    b, l, d = x.shape
    tok = lambda w, j: pl.BlockSpec((1, tm, w), lambda i, t: (i, t, j))
    vec = lambda a: pl.BlockSpec(a.shape, lambda i, t: (0,) * a.ndim)
    t256 = tok(GROUP_W, 0)
    args = [x, g1, y5, u5, p['s5_d'], p['glu_w'], p['glu_b'],
            rw_y[0], rw_y[1], rw_pre[4], rw_pre[3], p['rw_ln_g'], p['rw_ln_b'],
            ret_o[0], ret_o[1], zret, p['ret_ln_g'],
            gla_o[0], gla_o[1], zgla, p['gla_ln_g'],
            p['bones'], p['w_out']]
    grouped = pl.BlockSpec((1, S5_G, tm // S5_T, S5_TP), lambda i, t: (i, 0, t, 0))
    specs = [tok(d, 0), pl.BlockSpec((1, 1, d), lambda i, t: (i, 0, 0)), grouped, t256,
             vec(p['s5_d']), vec(p['glu_w']), vec(p['glu_b']),
             t256, t256, t256, t256, vec(p['rw_ln_g']), vec(p['rw_ln_b']),
             t256, t256, tok(GROUP_W, 3), vec(p['ret_ln_g']),
             t256, t256, tok(GROUP_W, 2), vec(p['gla_ln_g']),
             vec(p['bones']), vec(p['w_out'])]
    return pl.pallas_call(
        _mix_kernel,
        grid=(b, l // tm),
        in_specs=specs,
        out_specs=tok(d, 0),
        out_shape=jax.ShapeDtypeStruct((b, l, d), F32),
        scratch_shapes=[pltpu.VMEM((GROUP_W // LANES, tm, LANES), F32)],
        compiler_params=_cparams("parallel", "parallel"),
        name="mix_outproj",
    )(*args)


def _mlp_kernel(final, nff, x_ref, g_ref, sc_ref, sh_ref, gate_ref, w1_ref, w2_ref, fg_ref, o_ref):
    x = x_ref[0]
    hb = _modnorm(x, g_ref[...], sc_ref[0], sh_ref[0]).astype(BF16)
    ff = w1_ref.shape[2] // nff
    acc = None
    for j in range(nff):
        a = jnp.maximum(jnp.dot(hb, w1_ref[0, :, j * ff:(j + 1) * ff], preferred_element_type=F32), 0.0)
        part = jnp.dot((a * a).astype(BF16), w2_ref[0, j * ff:(j + 1) * ff, :], preferred_element_type=F32)
        acc = part if acc is None else acc + part
    y = x + gate_ref[0] * acc
    if final:
        ms = jnp.mean(y * y, axis=-1, keepdims=True)
        y = y * lax.rsqrt(ms + EPS) * fg_ref[...]
    o_ref[0] = y


def _mlp(x, g, sc, sh, gate, w1, w2, layer, final_g, final, tm):
    b, l, d = x.shape
    tok = pl.BlockSpec((1, tm, d), lambda i, t: (i, t, 0))
    vec = pl.BlockSpec((1, 1, d), lambda i, t: (i, 0, 0))
    row = pl.BlockSpec((1, d), lambda i, t: (0, 0))
    once = lambda a: pl.BlockSpec((1,) + a.shape[1:], lambda i, t: (layer, 0, 0), pipeline_mode=pl.Buffered(1))
    return pl.pallas_call(
        functools.partial(_mlp_kernel, final, 4),
        grid=(b, l // tm),
        in_specs=[tok, row, vec, vec, vec, once(w1), once(w2), row],
        out_specs=tok,
        out_shape=jax.ShapeDtypeStruct((b, l, d), F32),
        compiler_params=_cparams("parallel", "parallel"),
        name="mlp",
    )(x, g, sc, sh, gate, w1, w2, final_g)


def kernel(x, c, ctx, c_ctx, ada_w, ada_b, norm1_g, norm2_g, w_in, w_out, s5_lam_re, s5_lam_im, s5_log_dt, s5_b_re, s5_b_im, s5_c_re, s5_c_im, s5_d, s5_glu_w, s5_glu_b, rw_mu, rw_w0, rw_w_up, rw_a0, rw_a_up, rw_g_up, rw_k_k, rw_k_a, rw_r_k, rw_ln_g, rw_ln_b, ret_decay_logit, ret_ln_g, gla_a_up, gla_a_b, gla_ln_g, mlp_w1, mlp_w2, final_g):
    b, l, d = x.shape
    lc = ctx.shape[1]
    depth = ada_w.shape[0]
    rows = l // GRID_W
    assert l % TOKEN_TILE == 0 and lc % RET_T == 0 and lc <= TOKEN_TILE and d % LANES == 0

    cond = jnp.zeros((8, d), F32).at[:b].set(c).at[b].set(c_ctx)
    mod = _modulation(cond, ada_w, ada_b)

    lane = jnp.arange(GROUP_W)
    bones = (lane[:, None] // HEAD_V == lane[None, :] // HEAD_V).astype(BF16)
    rope_lat = _rope_tables(rows)
    ident = (jnp.ones((lc // CHUNK, 1, GROUP_W), F32), jnp.zeros((lc // CHUNK, 1, GROUP_W), F32),
             jnp.ones((GRID_W, GROUP_W), F32), jnp.zeros((GRID_W, GROUP_W), F32))
    nlev = max(1, (l // S5_T - 1).bit_length())
    c0, c1, c2 = GROUP_W, GROUP_W + RW_COLS, GROUP_W + RW_COLS + 4 * GROUP_W
    row2 = lambda a: a.reshape(1, -1).astype(F32)
    w_in_b, w_out_b, w1_b, w2_b = _to_bf16(w_in), _to_bf16(w_out), _to_bf16(mlp_w1), _to_bf16(mlp_w2)

    s5_tab_all = jax.vmap(lambda *a: _s5_tables(*a, nlev))(
        s5_lam_re, s5_lam_im, s5_log_dt, s5_b_re, s5_b_im, s5_c_re, s5_c_im)
    ret_tab_all = jax.vmap(lambda a: _ret_tables(a, RET_T))(ret_decay_logit)
    zpad = lambda a, lo, n: jnp.zeros((depth, 2, LANES, n), F32).at[:, :, lo:lo + a.shape[2]].set(a).astype(BF16)
    rw_wup_all = zpad(rw_w_up, 0, GROUP_W)
    rw_aup_all = zpad(rw_a_up, RW_W_RANK, GROUP_W)
    rw_gup_all = jnp.zeros((depth, LANES, GROUP_W), F32).at[:, RW_W_RANK + RW_A_RANK:].set(rw_g_up).astype(BF16)
    gla_aup_all = zpad(gla_a_up, 0, GLA_QK)

    xc = ctx
    for i in range(depth):
        last = i == depth - 1
        m = mod[i].reshape(8, N_ADA, d)
        ml = m[:b, :, None, :]
        mc = jnp.broadcast_to(m[b][None, :, None, :], (b, N_ADA, 1, d))
        wi = w_in_b[i]
        w5, wr, wt = wi[:, :c0], wi[:, c0:c1], wi[:, c1:c2]
        wg = jnp.pad(wi[:, c2:], ((0, 0), (0, GLA_COLS_PAD - (wi.shape[1] - c2))))
        n1 = row2(norm1_g[i])
        n2 = row2(norm2_g[i])

        s5_tab = tuple(a[i] for a in s5_tab_all)
        rwp = dict(
            mu=row2(rw_mu[i]),
            w0=rw_w0[i].reshape(2, 1, GROUP_W), a0=rw_a0[i].reshape(2, 1, GROUP_W),
            wup=rw_wup_all[i], aup=rw_aup_all[i], gup=rw_gup_all[i],
            kk=row2(rw_k_k[i]), ka=row2(rw_k_a[i]), rk=row2(rw_r_k[i]), bones=bones)
        ret_tab = tuple(a[i] for a in ret_tab_all)
        gla_aup = gla_aup_all[i]
        gla_ab = gla_a_b[i].reshape(2, 1, GLA_QK).astype(F32)
        mixp = dict(s5_d=row2(s5_d[i]), glu_w=s5_glu_w[i].astype(BF16), glu_b=row2(s5_glu_b[i]),
                    rw_ln_g=row2(rw_ln_g[i]), rw_ln_b=row2(rw_ln_b[i]), ret_ln_g=row2(ret_ln_g[i]),
                    gla_ln_g=row2(gla_ln_g[i]), bones=bones,
                    w_out=w_out_b[i].reshape(4, GROUP_W, d))

        def mixers(xx, mm, is_lat, states):
            tm = TOKEN_TILE if is_lat else lc
            z5, z5g, zt, zg, pre = _inproj(xx, n1, mm[:, 1], mm[:, 0], w5, wr, wt, wg,
                                           rope_lat if is_lat else ident, rwp, is_lat, tm)
            y5, h5 = _s5_scan(z5g, s5_tab, states[0])
            yrf, yrb, srw = _rw_scan(pre, states[1])
            otf, otb, sret = _ret_scan(zt, ret_tab, states[2])
            ogf, ogb, sgla = _gla_scan(zg, gla_aup, gla_ab, states[3])
            outs = (z5, y5, pre, (yrf, yrb), zt, (otf, otb), zg, (ogf, ogb))
            return outs, (h5, srw, sret, sgla)

        def block(xx, mm, outs, is_lat, fin):
            tm = TOKEN_TILE if is_lat else lc
            z5, y5, pre, yr, zt, ot, zg, og = outs
            x1 = _mix(xx, mm[:, 2], y5, z5, pre, yr, zt, ot, zg, og, mixp, tm)
            return _mlp(x1, n2, mm[:, 4], mm[:, 3], mm[:, 5], w1_b, w2_b, i, row2(final_g), fin, tm)

        zeros = (jnp.zeros((b, 2, S5_G, 1, 2 * S5_N), F32),
                 jnp.zeros((b, 2, HEADS // 2, HEAD_V, LANES), F32),
                 jnp.zeros((b, 2, HEADS // 2, HEAD_V, LANES), F32),
                 jnp.zeros((b, 2, HEAD_V, GLA_QK), F32))
        outs_c, st_c = mixers(xc, mc, False, zeros)
        outs_l, _ = mixers(x, ml, True, st_c)
        x = block(x, ml, outs_l, True, last)
        if not last:
            xc = block(xc, mc, outs_c, False, False)
    return x
```

```python
import functools
import math

import jax
import jax.numpy as jnp
from jax import lax
from jax.experimental import pallas as pl
from jax.experimental.pallas import tpu as pltpu

F32 = jnp.float32
BF16 = jnp.bfloat16

LANES = 128
GRID_W = 64
GROUP_W = 256
N_ADA = 6
EPS = 1e-6
GN_EPS = 64e-5
TOKEN_TILE = 512
CHUNK = 64
RW_CPB = 4
RET_T = 256
RET_BLOCK = 512
GLA_T = 128
GLA_BLOCK = 512
HEADS = 4
HEAD_V = 64
S5_P = 16
S5_G = 16
S5_N = 64
S5_T = 32
S5_TP = S5_T * S5_P
RW_COLS = 896
RW_W_RANK = 32
RW_A_RANK = 32
RW_G_RANK = 64
RET_DK = 64
GLA_DK = 32
GLA_QK = 128
GLA_RANK = 16
GLA_TAU = 16.0
GLA_COLS_PAD = 896
ROPE_BASE = 10000.0
ROPE_HALF = RET_DK // 2
ROPE_NF = RET_DK // 4
COND_ROWS = 8
CAST_ROWS = 256
ADA_TILE = 1536
VMEM_LIMIT = 56 * 1024 * 1024


def _cparams(*sem):
    return pltpu.CompilerParams(dimension_semantics=sem, vmem_limit_bytes=VMEM_LIMIT)


def _mm(a, b):
    return jnp.dot(a.astype(BF16), b.astype(BF16), preferred_element_type=F32)


def _mm_nt(a, b):
    return lax.dot_general(a.astype(BF16), b.astype(BF16), (((1,), (1,)), ((), ())),
                           preferred_element_type=F32)


def _mm_tn(a, b):
    return lax.dot_general(a.astype(BF16), b.astype(BF16), (((0,), (0,)), ((), ())),
                           preferred_element_type=F32)


def _split2(x):
    hi = x.astype(BF16)
    return hi, (x - hi.astype(F32)).astype(BF16)


def _chunk_tri01(n, t, reverse):
    ri = lax.broadcasted_iota(jnp.int32, (n, n), 0)
    ci = lax.broadcasted_iota(jnp.int32, (n, n), 1)
    shift = t.bit_length() - 1
    tri = (ci >= ri) if reverse else (ci <= ri)
    return jnp.where((ri >> shift) == (ci >> shift), jnp.where(tri, 1.0, 0.0), 0.0).astype(BF16)


def _mm_left01(m01, x):
    hi, lo = _split2(x)
    return jnp.dot(m01, hi, preferred_element_type=F32) + jnp.dot(m01, lo, preferred_element_type=F32)


def _mm_right01(x, m01):
    return jnp.dot(x.astype(BF16), m01, preferred_element_type=F32)


def _sigmoid(x):
    return 1.0 / (1.0 + jnp.exp(-x))


def _softplus(x):
    return jnp.maximum(x, 0.0) + jnp.log(1.0 + jnp.exp(-jnp.abs(x)))


def _silu(x):
    return x * _sigmoid(x)


def _gelu_tanh(x):
    return 0.5 * x * (1.0 + jnp.tanh(math.sqrt(2.0 / math.pi) * (x + 0.044715 * x * x * x)))


def _tri_incl(t, reverse):
    ri = lax.broadcasted_iota(jnp.int32, (t, t), 0)
    ci = lax.broadcasted_iota(jnp.int32, (t, t), 1)
    return (ci >= ri) if reverse else (ci <= ri)


def _cast_kernel(x_ref, o_ref):
    o_ref[...] = x_ref[...].astype(o_ref.dtype)


def _to_bf16(w):
    depth, r, c = w.shape
    tr = CAST_ROWS
    spec = pl.BlockSpec((1, tr, c), lambda i, j: (i, j, 0))
    return pl.pallas_call(
        _cast_kernel,
        grid=(depth, r // tr),
        in_specs=[spec],
        out_specs=spec,
        out_shape=jax.ShapeDtypeStruct(w.shape, BF16),
        compiler_params=_cparams("parallel", "parallel"),
        name="cast_bf16",
    )(w)


def _mod_kernel(cond_ref, w_ref, b_ref, o_ref):
    c = cond_ref[...]
    o_ref[0] = _mm(_silu(c), w_ref[0]) + b_ref[0]


def _modulation(cond, ada_w, ada_b):
    depth, d, n = ada_w.shape
    tn = ADA_TILE
    return pl.pallas_call(
        _mod_kernel,
        grid=(depth, n // tn),
        in_specs=[pl.BlockSpec((COND_ROWS, d), lambda i, j: (0, 0)),
                  pl.BlockSpec((1, d, tn), lambda i, j: (i, 0, j)),
                  pl.BlockSpec((1, 1, tn), lambda i, j: (i, 0, j))],
        out_specs=pl.BlockSpec((1, COND_ROWS, tn), lambda i, j: (i, 0, j)),
        out_shape=jax.ShapeDtypeStruct((depth, COND_ROWS, n), F32),
        compiler_params=_cparams("parallel", "parallel"),
        name="adaln_mod",
    )(cond, ada_w, ada_b.reshape(depth, 1, n))


def _modnorm(x, g, sc, sh):
    ms = jnp.mean(x * x, axis=-1, keepdims=True)
    return x * lax.rsqrt(ms + EPS) * g * (1.0 + sc) + sh


def _shift_columns(z, segments):
    tm, width = z.shape
    lane = lax.broadcasted_iota(jnp.int32, (tm, LANES), 1)
    starts = [s for s, _ in segments] + [width]
    cols = []
    for c in range(width // LANES):
        lo, hi = c * LANES, (c + 1) * LANES
        inside = [(max(starts[i], lo), segments[i][1]) for i in range(len(segments))
                  if starts[i] < hi and starts[i + 1] > lo]
        col = inside[-1][1](c)
        for first, fn in reversed(inside[:-1]):
            nxt = [f for f, _ in inside if f > first][0]
            col = jnp.where(lane < nxt - lo, fn(c), col)
        cols.append(col)
    return jnp.concatenate(cols, axis=1)


def _rw_prep_math(grid_shift, z, zp, zn, mu, w0_ref, wup_ref, a0_ref, aup_ref, gup, kkp, ka, rk, bones):
    tm = z.shape[0]
    row = lax.broadcasted_iota(jnp.int32, (tm, LANES), 0)
    col = lambda x, c: x[:, c * LANES:(c + 1) * LANES]
    if grid_shift:
        pos = row & (GRID_W - 1)
        left = lambda c: jnp.where(pos == 0, 0.0, pltpu.roll(col(z, c), 1, 0))
        right = lambda c: jnp.where(pos == GRID_W - 1, 0.0, pltpu.roll(col(z, c), tm - 1, 0))
        up = lambda c: jnp.concatenate([col(zp, c), col(z, c)[:tm - GRID_W]], axis=0)
        down = lambda c: jnp.concatenate([col(z, c)[GRID_W:], col(zn, c)], axis=0)
        q = RW_COLS // 4
        shifted = _shift_columns(z, [(0, left), (q, right), (2 * q, up), (3 * q, down)])
    else:
        prev = lambda c: jnp.where(row == 0, 0.0, pltpu.roll(col(z, c), 1, 0))
        nxt = lambda c: jnp.where(row == tm - 1, 0.0, pltpu.roll(col(z, c), tm - 1, 0))
        shifted = _shift_columns(z, [(0, prev), (RW_COLS // 2, nxt)])
    zm = z + mu * (shifted - z)
    r = zm[:, 0:GROUP_W]
    k = zm[:, GROUP_W:2 * GROUP_W]
    v = zm[:, 2 * GROUP_W:3 * GROUP_W]
    lo = zm[:, 3 * GROUP_W:RW_COLS]
    g = _mm(_sigmoid(lo), gup)
    kk = k * kkp
    kk = kk * lax.rsqrt(_mm_right01(kk * kk, bones) + 1e-12)
    bonus = _mm_right01(r * k * rk, bones) * v
    outs = [r.astype(BF16), v.astype(BF16), kk.astype(BF16), g.astype(BF16), bonus.astype(BF16)]
    tlo = jnp.tanh(lo)
    for d in range(2):
        w_raw = -_softplus(-(w0_ref[d] + _mm(tlo, wup_ref[d]))) - 0.5
        a = _sigmoid(a0_ref[d] + _mm(lo, aup_ref[d]))
        outs += [-jnp.exp(w_raw),
                 (k * (1.0 + (a - 1.0) * ka)).astype(BF16), a.astype(BF16)]
    return outs


def _inproj_kernel(grid_shift, nt, x_ref, xp_ref, xn_ref, g_ref, sc_ref, sh_ref,
                   w5_ref, wr_ref, wt_ref, wg_ref, cr_ref, sr_ref, cc_ref, sn_ref,
                   mu_ref, w0_ref, wup_ref, a0_ref, aup_ref, gup_ref, kk_ref, ka_ref, rk_ref, bones_ref,
                   o5_ref, o5g_ref, ot_ref, og_ref, *rest):
    rw_refs, z5h_ref = rest[:-1], rest[-1]
    j = pl.program_id(1)
    norm = lambda x: _modnorm(x, g_ref[...], sc_ref[0], sh_ref[0]).astype(BF16)
    hb = norm(x_ref[0])
    z5 = jnp.dot(hb, w5_ref[...], preferred_element_type=F32)
    o5_ref[0] = z5
    og_ref[0] = jnp.dot(hb, wg_ref[...], preferred_element_type=F32)
    zr = jnp.dot(hb, wr_ref[...], preferred_element_type=F32)
    zp = zn = None
    if grid_shift:
        zp = jnp.where(j > 0, jnp.dot(norm(xp_ref[0]), wr_ref[...], preferred_element_type=F32), 0.0)
        zn = jnp.where(j < nt - 1, jnp.dot(norm(xn_ref[0]), wr_ref[...], preferred_element_type=F32), 0.0)
    rw = _rw_prep_math(grid_shift, zr, zp, zn, mu_ref[...], w0_ref, wup_ref, a0_ref, aup_ref, gup_ref[...],
                       kk_ref[...], ka_ref[...], rk_ref[...], bones_ref[...])
    for ref, val in zip(rw_refs, rw):
        ref[0] = val
    zt = jnp.dot(hb, wt_ref[...], preferred_element_type=F32)
    lane = lax.broadcasted_iota(jnp.int32, (GRID_W, GROUP_W), 1)
    by_row = (lane & ROPE_HALF) == 0
    nrow = cr_ref.shape[0]
    cos = jnp.concatenate([jnp.where(by_row, cr_ref[j], cc_ref[...]) for j in range(nrow)], axis=0)
    sin = jnp.concatenate([jnp.where(by_row, sr_ref[j], sn_ref[...]) for j in range(nrow)], axis=0)
    ot_ref[0, :, 0:GROUP_W] = _rope(zt[:, 0:GROUP_W], cos, sin)
    ot_ref[0, :, GROUP_W:2 * GROUP_W] = _rope(zt[:, GROUP_W:2 * GROUP_W] * RET_DK ** -0.5, cos, sin)
    ot_ref[0, :, 2 * GROUP_W:] = zt[:, 2 * GROUP_W:]
    nch = o5g_ref.shape[2]
    gph = LANES // S5_P
    for hf in range(GROUP_W // LANES):
        z5h_ref[hf] = z5[:, hf * LANES:(hf + 1) * LANES]
    for s in range(S5_T):
        for hf in range(GROUP_W // LANES):
            rows = z5h_ref[hf, pl.ds(s, nch, stride=S5_T), :]
            for g in range(gph):
                o5g_ref[0, hf * gph + g, :, s * S5_P:(s + 1) * S5_P] = rows[:, g * S5_P:(g + 1) * S5_P]


def _inproj(x, g, sc, sh, w5, wr, wt, wg, rope, rwp, grid_shift, tm):
    b, l, d = x.shape
    nt = l // tm
    hb = tm // GRID_W
    nhb = l // GRID_W
    tok = lambda n: pl.BlockSpec((1, tm, n), lambda i, j: (i, j, 0))
    vec = pl.BlockSpec((1, 1, d), lambda i, j: (i, 0, 0))
    full = lambda a: pl.BlockSpec(a.shape, lambda i, j: (0,) * a.ndim)
    grouped = pl.BlockSpec((1, S5_G, tm // S5_T, S5_TP), lambda i, j: (i, 0, j, 0))
    rowt = pl.BlockSpec((hb, 1, GROUP_W), lambda i, j: (j, 0, 0))
    above = pl.BlockSpec((1, GRID_W, d), lambda i, j: (i, jnp.maximum(j * hb - 1, 0), 0))
    below = pl.BlockSpec((1, GRID_W, d), lambda i, j: (i, jnp.minimum((j + 1) * hb, nhb - 1), 0))
    cr, sr, cc, sn = rope
    params = (rwp['mu'], rwp['w0'], rwp['wup'], rwp['a0'], rwp['aup'], rwp['gup'], rwp['kk'], rwp['ka'],
              rwp['rk'], rwp['bones'])
    rw_dtypes = (BF16,) * 5 + (F32, BF16, BF16) * 2
    outs = pl.pallas_call(
        functools.partial(_inproj_kernel, grid_shift, nt),
        grid=(b, nt),
        in_specs=[tok(d), above, below, pl.BlockSpec((1, d), lambda i, j: (0, 0)), vec, vec,
                  full(w5), full(wr), full(wt), full(wg), rowt, rowt, full(cc), full(sn)]
                 + [full(a) for a in params],
        out_specs=[tok(GROUP_W), grouped, tok(wt.shape[1]), tok(wg.shape[1])] + [tok(GROUP_W)] * 11,
        out_shape=[jax.ShapeDtypeStruct((b, l, GROUP_W), F32),
                   jax.ShapeDtypeStruct((b, S5_G, l // S5_T, S5_TP), F32),
                   jax.ShapeDtypeStruct((b, l, wt.shape[1]), F32),
                   jax.ShapeDtypeStruct((b, l, wg.shape[1]), F32)]
                  + [jax.ShapeDtypeStruct((b, l, GROUP_W), dt) for dt in rw_dtypes],
        scratch_shapes=[pltpu.VMEM((GROUP_W // LANES, tm, LANES), F32)],
        compiler_params=_cparams("parallel", "parallel"),
        name="norm_inproj",
    )(x, x, x, g, sc, sh, w5, wr, wt, wg, cr, sr, cc, sn, *params)
    return outs[0], outs[1], outs[2], outs[3], tuple(outs[4:])


def _s5_tables(lam_re, lam_im, log_dt, b_re, b_im, c_re, c_im, nlev):
    hp = lax.Precision.HIGHEST
    t = S5_T
    lam = lax.complex(jnp.minimum(lam_re.astype(F32), -1e-4), lam_im.astype(F32))
    ldt = lam * jnp.exp(log_dt.astype(F32))[..., None]
    a_bar = jnp.exp(ldt)
    bb = ((a_bar - 1.0) / lam)[..., None] * lax.complex(b_re.astype(F32), b_im.astype(F32))
    cm = lax.complex(c_re.astype(F32), c_im.astype(F32))
    tau = jnp.arange(t + 1, dtype=F32)
    apow = jnp.exp(ldt[:, :, None, :] * tau[None, None, :, None])
    taps = jnp.einsum('dgpn,dgtn,dgnq->dgtpq', cm, apow[:, :, :t], bb, precision=hp).real
    taprow = jnp.stack([taps[0], taps[1][:, ::-1]]).transpose(0, 1, 4, 2, 3).reshape(2, S5_G, S5_P, S5_TP)

    def pack(zc):
        return jnp.concatenate([zc.real, zc.imag], axis=-1)

    win_f = apow[0][:, t - 1 - jnp.arange(t), None, :] * bb[0].transpose(0, 2, 1)[:, None]
    win_b = apow[1][:, jnp.arange(t), None, :] * bb[1].transpose(0, 2, 1)[:, None]
    win = jnp.stack([pack(win_f), pack(win_b)]).reshape(2, S5_G, S5_TP, 2 * S5_N)
    ca_f = cm[0][:, None] * apow[0][:, 1 + jnp.arange(t), None, :]
    ca_b = cm[1][:, None] * apow[1][:, t - jnp.arange(t), None, :]

    def outpack(ca):
        w = jnp.concatenate([ca.real, -ca.imag], axis=-1)
        return w.reshape(S5_G, S5_TP, 2 * S5_N).transpose(0, 2, 1)

    wout = jnp.stack([outpack(ca_f), outpack(ca_b)])
    lev = (2.0 ** jnp.arange(nlev, dtype=F32)) * t
    pw = jnp.exp(ldt[:, :, None, :] * lev[None, None, :, None])
    p1 = jnp.concatenate([pw.real, pw.real], axis=-1)
    p2 = jnp.concatenate([-pw.imag, pw.imag], axis=-1)
    pw = jnp.stack([p1, p2], axis=3)
    return taprow, win.astype(BF16), wout.astype(BF16), pw


def _s5_kernel(nc, nlev, u_ref, tap_ref, win_ref, wout_ref, pw_ref, h0_ref, y_ref, hfin_ref, conv_ref):
    u = u_ref[0, 0].astype(BF16)
    row = lax.broadcasted_iota(jnp.int32, (nc, 2 * S5_N), 0)
    lane = lax.broadcasted_iota(jnp.int32, (S5_P, S5_TP), 1)
    for s in range(S5_T):
        lo = s * S5_P
        fwd = tap_ref[0, 0] if s == 0 else jnp.where(lane >= lo, pltpu.roll(tap_ref[0, 0], lo, 1), 0.0)
        hi = lo + S5_P
        bwd = tap_ref[1, 0] if hi == S5_TP else jnp.where(lane < hi, pltpu.roll(tap_ref[1, 0], hi, 1), 0.0)
        conv_ref[0, lo:hi, :] = fwd.astype(BF16)
        conv_ref[1, lo:hi, :] = bwd.astype(BF16)

    def cmul(x, d, j):
        return pw_ref[d, 0, j, 0:1] * x + pw_ref[d, 0, j, 1:2] * pltpu.roll(x, S5_N, 1)

    v = [jnp.dot(u, win_ref[d, 0], preferred_element_type=F32) for d in range(2)]
    x = [jnp.where(row == 0, h0_ref[0, 0, 0], pltpu.roll(v[0], 1, 0)),
         jnp.where(row == nc - 1, h0_ref[0, 1, 0], pltpu.roll(v[1], nc - 1, 0))]
    for j in range(nlev):
        sh = 2 ** j
        xs = [jnp.where(row >= sh, pltpu.roll(x[0], sh, 0), 0.0),
              jnp.where(row < nc - sh, pltpu.roll(x[1], nc - sh, 0), 0.0)]
        x = [x[d] + cmul(xs[d], d, j) for d in range(2)]
    for d, last in enumerate((nc - 1, 0)):
        hfin_ref[0, d, 0] = cmul(x[d][last:last + 1], d, 0) + v[d][last:last + 1]
    y_ref[0, 0] = (jnp.dot(u, conv_ref[0], preferred_element_type=F32) + _mm(x[0], wout_ref[0, 0])
                   + jnp.dot(u, conv_ref[1], preferred_element_type=F32) + _mm(x[1], wout_ref[1, 0]))


def _s5_scan(uf, tables, h0):
    conv, win, wout, pw = tables
    b, _, nc, _ = uf.shape
    nlev = max(1, (nc - 1).bit_length())
    pw = pw[:, :, :nlev]
    n2 = 2 * S5_N
    y, hfin = pl.pallas_call(
        functools.partial(_s5_kernel, nc, nlev),
        grid=(b, S5_G),
        in_specs=[pl.BlockSpec((1, 1, nc, S5_TP), lambda i, g: (i, g, 0, 0)),
                  pl.BlockSpec((2, 1, S5_P, S5_TP), lambda i, g: (0, g, 0, 0)),
                  pl.BlockSpec((2, 1, S5_TP, n2), lambda i, g: (0, g, 0, 0)),
                  pl.BlockSpec((2, 1, n2, S5_TP), lambda i, g: (0, g, 0, 0)),
                  pl.BlockSpec((2, 1, nlev, 2, n2), lambda i, g: (0, g, 0, 0, 0)),
                  pl.BlockSpec((1, 2, 1, 1, n2), lambda i, g: (i, 0, g, 0, 0))],
        out_specs=[pl.BlockSpec((1, 1, nc, S5_TP), lambda i, g: (i, g, 0, 0)),
                   pl.BlockSpec((1, 2, 1, 1, n2), lambda i, g: (i, 0, g, 0, 0))],
        out_shape=[jax.ShapeDtypeStruct((b, S5_G, nc, S5_TP), F32),
                   jax.ShapeDtypeStruct((b, 2, S5_G, 1, n2), F32)],
        scratch_shapes=[pltpu.VMEM((2, S5_TP, S5_TP), BF16)],
        compiler_params=_cparams("parallel", "parallel"),
        name="s5_scan",
    )(uf, conv, win, wout, pw, h0)
    return y, hfin


def _rw_scan_kernel(nb, cpb, rf_ref, vf_ref, kkf_ref, lwf_ref, kdf_ref, asf_ref,
                    rb_ref, vb_ref, kkb_ref, lwb_ref, kdb_ref, asb_ref, s0_ref,
                    yf_ref, yb_ref, sfin_ref, st_ref):
    c = pl.program_id(1)

    @pl.when(c == 0)
    def _():
        st_ref[...] = s0_ref[0]

    t = CHUNK
    ri = lax.broadcasted_iota(jnp.int32, (t, LANES), 0)
    li = lax.broadcasted_iota(jnp.int32, (t, LANES), 1)
    ci = li & (t - 1)
    low = li < HEAD_V
    eyef = jnp.where(ri == ci, 1.0, 0.0)
    same = lambda s: jnp.where((ri >> s) == (ci >> s), 1.0, 0.0)
    m4, m8, m16, m32 = same(2), same(3), same(4), same(5)
    merge_masks = (m8 - m4, m16 - m8, m32 - m16, 1.0 - m32)

    def bd(x):
        return jnp.concatenate([jnp.where(low, x, 0.0), jnp.where(low, 0.0, x)], axis=0)

    def diag_blocks(full):
        return jnp.where(low, full[0:HEAD_V], full[HEAD_V:2 * HEAD_V])

    dir_refs = ((rf_ref, vf_ref, kkf_ref, lwf_ref, kdf_ref, asf_ref),
                (rb_ref, vb_ref, kkb_ref, lwb_ref, kdb_ref, asb_ref))
    units = []
    for d, (r_ref, v_ref, kk_ref, lw_ref, kd_ref, as_ref) in enumerate(dir_refs):
        incl = (ci >= ri) if d == 1 else (ci <= ri)
        strict = (ci > ri) if d == 1 else (ci < ri)
        r, v, kk, kd = (x[0].astype(F32) for x in (r_ref, v_ref, kk_ref, kd_ref))
        lw = lw_ref[0]
        cin = _mm_left01(_chunk_tri01(cpb * t, t, d == 1), lw)
        e_in = jnp.exp(cin)
        e_neg = jnp.exp(-cin)
        rt = r * e_in
        at = -kk * jnp.exp(cin - lw)
        bvec = kk * as_ref[0].astype(F32)
        bt = bvec * e_neg
        kt = kd * e_neg
        for j in range(cpb):
            rows = slice(j * t, (j + 1) * t)
            last = j * t + (0 if d == 1 else t - 1)
            clast = cin[last:last + 1]
            dl = jnp.exp(clast - cin[rows])
            bh = bvec[rows] * dl
            kh = kd[rows] * dl
            dec = jnp.exp(clast)
            for p in range(HEADS // 2):
                sl = slice(p * LANES, (p + 1) * LANES)
                units.append(dict(d=d, j=j, p=p, incl=incl, strict=strict,
                                  at=at[rows, sl], rt=rt[rows, sl], bt=bt[rows, sl], kt=kt[rows, sl],
                                  bh=bh[:, sl], kh=kh[:, sl], v=v[rows, sl], dec=dec[:, sl]))

    x1 = [jnp.concatenate([u['at'], u['rt']], axis=0) for u in units]
    scores = [_mm_nt(x, jnp.concatenate([bd(u['bt']), bd(u['kt'])], axis=0))
              for x, u in zip(x1, units)]
    nmat = [jnp.where(u['strict'], a[0:t, :LANES], 0.0) for u, a in zip(units, scores)]
    a_rb = [jnp.where(u['incl'], a[t:2 * t, :LANES], 0.0) for u, a in zip(units, scores)]
    a_kk = [jnp.concatenate([jnp.where(u['strict'], a[0:t, LANES:], 0.0),
                             jnp.where(u['incl'], a[t:2 * t, LANES:], 0.0)], axis=0)
            for u, a in zip(units, scores)]
    akv = [_mm(a, bd(u['v'])) for u, a in zip(units, a_kk)]
    kv = [diag_blocks(_mm_tn(u['v'], u['kh'])) for u in units]
    nd = [x * m4 for x in nmat]
    n2 = [_mm(x, bd(x)) for x in nd]
    tinv = [eyef + x + _mm(eyef + x, bd(y)) for x, y in zip(nd, n2)]
    for mk in merge_masks:
        w = [_mm(ti, bd(x * mk)) for ti, x in zip(tinv, nmat)]
        tinv = [ti + _mm(wi, bd(ti)) for ti, wi in zip(tinv, w)]
    zz = [_mm(ti, jnp.concatenate([bd(u['at']), bd(kvv[0:t])], axis=1))
          for ti, u, kvv in zip(tinv, units, akv)]
    ght = [_mm_tn(z, u['bh']) for u, z in zip(units, zz)]
    qy = [_mm(a, jnp.concatenate([bd(z[:, :LANES]), bd(z[:, LANES:])], axis=1))
          for a, z in zip(a_rb, zz)]
    qmat = [u['rt'] + x[:, :LANES] for u, x in zip(units, qy)]
    y0 = [x[:, LANES:] + kvv[t:2 * t] for x, kvv in zip(qy, akv)]
    gmat = [diag_blocks(x[:LANES]) for x in ght]
    hmat = [diag_blocks(x[LANES:]) + k2 for x, k2 in zip(ght, kv)]
    idx = {(u['d'], u['j'], u['p']): n for n, u in enumerate(units)}
    npair = HEADS // 2
    state = {(d, p): st_ref[d, p] for d in range(2) for p in range(npair)}
    ys = {}
    for step in range(cpb):
        for d in range(2):
            j = step if d == 0 else cpb - 1 - step
            for p in range(npair):
                n = idx[(d, j, p)]
                st = state[(d, p)]
                ys[(d, j, p)] = _mm_nt(qmat[n], bd(st)) + y0[n]
                state[(d, p)] = units[n]['dec'] * st + _mm(st, bd(gmat[n])) + hmat[n]
    for d, y_ref in enumerate((yf_ref, yb_ref)):
        y_ref[0] = jnp.concatenate(
            [jnp.concatenate([ys[(d, j, p)] for p in range(npair)], axis=1) for j in range(cpb)], axis=0)
        for p in range(npair):
            st_ref[d, p] = state[(d, p)]

    @pl.when(c == nb - 1)
    def _():
        sfin_ref[0] = st_ref[...]


def _rw_scan(pre, s0):
    r, v, kk, _, _, lw0, kd0, as0, lw1, kd1, as1 = pre
    b, l, _ = r.shape
    cpb = min(RW_CPB, l // CHUNK)
    nb = l // (cpb * CHUNK)
    fw = pl.BlockSpec((1, cpb * CHUNK, GROUP_W), lambda i, c: (i, c, 0))
    bw = pl.BlockSpec((1, cpb * CHUNK, GROUP_W), lambda i, c: (i, nb - 1 - c, 0))
    st = pl.BlockSpec((1, 2, HEADS // 2, HEAD_V, LANES), lambda i, c: (i, 0, 0, 0, 0))
    return pl.pallas_call(
        functools.partial(_rw_scan_kernel, nb, cpb),
        grid=(b, nb),
        in_specs=[fw] * 6 + [bw] * 6 + [st],
        out_specs=[fw, bw, st],
        out_shape=[jax.ShapeDtypeStruct((b, l, GROUP_W), F32)] * 2
                  + [jax.ShapeDtypeStruct((b, 2, HEADS // 2, HEAD_V, LANES), F32)],
        scratch_shapes=[pltpu.VMEM((2, HEADS // 2, HEAD_V, LANES), F32)],
        compiler_params=_cparams("parallel", "arbitrary"),
        name="rwkv_scan",
    )(r, v, kk, lw0, kd0, as0, r, v, kk, lw1, kd1, as1, s0)


def _rope(x, cos, sin_signed):
    lane = lax.broadcasted_iota(jnp.int32, x.shape, 1)
    n = x.shape[1]
    swapped = jnp.where((lane & ROPE_NF) == 0, pltpu.roll(x, n - ROPE_NF, 1), pltpu.roll(x, ROPE_NF, 1))
    return x * cos + swapped * sin_signed


def _ret_scan_kernel(nb, cpb, qf_ref, kf_ref, vf_ref, qb_ref, kb_ref, vb_ref,
                     dmat_ref, qdec_ref, kdec_ref, sdec_ref, s0_ref,
                     of_ref, ob_ref, sfin_ref, st_ref):
    c = pl.program_id(1)

    @pl.when(c == 0)
    def _():
        st_ref[...] = s0_ref[0]

    t = dmat_ref.shape[-1]
    npair = HEADS // 2
    low_t = lax.broadcasted_iota(jnp.int32, (t, LANES), 1) < HEAD_V
    low_s = lax.broadcasted_iota(jnp.int32, (HEAD_V, LANES), 1) < HEAD_V

    def bd(x, low):
        return jnp.concatenate([jnp.where(low, x, 0.0), jnp.where(low, 0.0, x)], axis=0)

    units = []
    for d, (q_ref, k_ref, v_ref) in enumerate(((qf_ref, kf_ref, vf_ref), (qb_ref, kb_ref, vb_ref))):
        for j in range(cpb):
            rows = slice(j * t, (j + 1) * t)
            q, k, v = q_ref[0, rows], k_ref[0, rows], v_ref[0, rows]
            qd = q * qdec_ref[d]
            kh = k * kdec_ref[d]
            for p in range(npair):
                sl = slice(p * LANES, (p + 1) * LANES)
                units.append(dict(d=d, j=j, p=p, q=q[:, sl], k=k[:, sl], qd=qd[:, sl], kh=kh[:, sl],
                                  v=v[:, sl], dec=sdec_ref[d][:, sl]))
    scores = [[_mm_nt(jnp.where(low_t == (h == 0), u['q'], 0.0), u['k']).astype(BF16)
               * dmat_ref[u['d'], 2 * u['p'] + h] for h in range(2)] for u in units]
    intra = [_mm(jnp.concatenate(s2, axis=1), bd(u['v'], low_t)) for s2, u in zip(scores, units)]
    kv = []
    for u in units:
        full = _mm_tn(u['v'], u['kh'])
        kv.append(jnp.where(low_s, full[0:HEAD_V], full[HEAD_V:2 * HEAD_V]))
    idx = {(u['d'], u['j'], u['p']): i for i, u in enumerate(units)}
    entering = {}
    for d in range(2):
        for p in range(npair):
            st = st_ref[d, p]
            for step in range(cpb):
                j = step if d == 0 else cpb - 1 - step
                i = idx[(d, j, p)]
                entering[i] = st
                st = units[i]['dec'] * st + kv[i]
            st_ref[d, p] = st
    outs = [x + _mm_nt(u['qd'], bd(entering[i], low_s)) for i, (x, u) in enumerate(zip(intra, units))]
    for d, o_ref in enumerate((of_ref, ob_ref)):
        o_ref[0] = jnp.concatenate(
            [jnp.concatenate([outs[idx[(d, j, p)]] for p in range(npair)], axis=1) for j in range(cpb)], axis=0)

    @pl.when(c == nb - 1)
    def _():
        sfin_ref[0] = st_ref[...]


def _ret_tables(decay_logit, n):
    lg = jax.nn.log_sigmoid(decay_logit.astype(F32))
    pos = jnp.arange(n, dtype=F32)
    lag = pos[:, None] - pos[None, :]
    lag = jnp.stack([lag, -lag])
    dmat = jnp.where(lag[:, None] >= 0, jnp.exp(lg[:, :, None, None] * lag[:, None]), 0.0)
    lanes = jnp.repeat(lg, RET_DK, axis=-1)[:, None, :]
    qpow = jnp.stack([pos + 1.0, n - pos])[:, :, None]
    kpow = jnp.stack([n - 1.0 - pos, pos])[:, :, None]
    return dmat.astype(BF16), jnp.exp(lanes * qpow), jnp.exp(lanes * kpow), jnp.exp(lanes * n)


def _gla_scan_kernel(nb, cpb, qf_ref, kf_ref, vf_ref, af_ref, qb_ref, kb_ref, vb_ref, ab_ref,
                     aup_ref, abias_ref, s0_ref, of_ref, ob_ref, sfin_ref, st_ref):
    c = pl.program_id(1)

    @pl.when(c == 0)
    def _():
        st_ref[...] = s0_ref[0]

    t = GLA_T
    n = cpb * t
    dk = GLA_DK
    scale = dk ** -0.5
    dirs = ((qf_ref, kf_ref, vf_ref, af_ref), (qb_ref, kb_ref, vb_ref, ab_ref))
    units = []
    for d, (q_ref, k_ref, v_ref, a_ref) in enumerate(dirs):
        incl = _tri_incl(t, d == 1)
        q, k, v = q_ref[0], k_ref[0] * scale, v_ref[0]
        lw = -_softplus(-(_mm(a_ref[0], aup_ref[d]) + abias_ref[d])) * (1.0 / GLA_TAU)
        cin = _mm_left01(_chunk_tri01(n, t, d == 1), lw)
        qe = q * jnp.exp(cin)
        for j in range(cpb):
            rows = slice(j * t, (j + 1) * t)
            last = j * t + (0 if d == 1 else t - 1)
            mid = j * t + (t // 2 if d == 1 else t // 2 - 1)
            cj = cin[rows]
            clast = cin[last:last + 1]
            cmid = cin[mid:mid + 1]
            qt = q[rows] * jnp.exp(cj - cmid)
            kt = k[rows] * jnp.exp(cmid - cj)
            kh = k[rows] * jnp.exp(clast - cj)
            units.append(dict(d=d, j=j, incl=incl, qt=qt, kt=kt, qe=qe[rows], kh=kh, v=v[rows],
                              dec=jnp.exp(clast)))

    klane = lax.broadcasted_iota(jnp.int32, (t, LANES), 1) // dk
    vlow = lax.broadcasted_iota(jnp.int32, (t, LANES), 1) < HEAD_V
    khead = lax.broadcasted_iota(jnp.int32, (HEAD_V, LANES), 1) // dk
    npair = HEADS // 2

    def v_blockdiag(x):
        return jnp.concatenate([jnp.where(vlow, x, 0.0), jnp.where(vlow, 0.0, x)], axis=0)

    def state_rows(st, p):
        return jnp.concatenate([jnp.where(khead == 2 * p, st, 0.0), jnp.where(khead == 2 * p + 1, st, 0.0)],
                               axis=0)

    amat = [[jnp.where(u['incl'], _mm_nt(jnp.where(klane == h, u['qt'], 0.0), u['kt']), 0.0)
             for h in range(HEADS)] for u in units]
    intra = [[_mm(jnp.concatenate(a[2 * p:2 * p + 2], axis=1), v_blockdiag(u['v'][:, p * LANES:(p + 1) * LANES]))
              for p in range(npair)] for a, u in zip(amat, units)]
    kv = []
    for u in units:
        full = _mm_tn(u['v'], u['kh'])
        blocks = [full[h * HEAD_V:(h + 1) * HEAD_V] for h in range(HEADS)]
        acc = blocks[HEADS - 1]
        for h in range(HEADS - 2, -1, -1):
            acc = jnp.where(khead == h, blocks[h], acc)
        kv.append(acc)
    idx = {(u['d'], u['j']): i for i, u in enumerate(units)}
    entering = {}
    for d in range(2):
        st = st_ref[d]
        for step in range(cpb):
            j = step if d == 0 else cpb - 1 - step
            i = idx[(d, j)]
            entering[i] = st
            st = units[i]['dec'] * st + kv[i]
        st_ref[d] = st
    outs = [[x[p] + _mm_nt(u['qe'], state_rows(entering[i], p)) for p in range(npair)]
            for i, (x, u) in enumerate(zip(intra, units))]
    for d, o_ref in enumerate((of_ref, ob_ref)):
        o_ref[0] = jnp.concatenate(
            [jnp.concatenate(outs[idx[(d, j)]], axis=1) for j in range(cpb)], axis=0)

    @pl.when(c == nb - 1)
    def _():
        sfin_ref[0] = st_ref[...]


def _rope_tables(rows):
    nf = ROPE_NF
    inv = ROPE_BASE ** (-jnp.arange(nf, dtype=F32) / nf)
    lane = jnp.arange(GROUP_W)
    freq = inv[lane % nf]
    sign = jnp.where((lane & ROPE_NF) == 0, -1.0, 1.0)
    ar = jnp.arange(rows, dtype=F32)[:, None] * freq[None, :]
    ac = jnp.arange(GRID_W, dtype=F32)[:, None] * freq[None, :]
    return (jnp.cos(ar).reshape(rows, 1, GROUP_W), (jnp.sin(ar) * sign).reshape(rows, 1, GROUP_W),
            jnp.cos(ac), jnp.sin(ac) * sign)


def _ret_scan(z, tables, s0):
    b, l, _ = z.shape
    n = min(RET_BLOCK, l)
    nb = l // n
    fw = lambda j: pl.BlockSpec((1, n, GROUP_W), lambda i, c: (i, c, j))
    bw = lambda j: pl.BlockSpec((1, n, GROUP_W), lambda i, c: (i, nb - 1 - c, j))
    full = lambda a: pl.BlockSpec(a.shape, lambda i, c: (0,) * a.ndim)
    st = pl.BlockSpec((1, 2, HEADS // 2, HEAD_V, LANES), lambda i, c: (i, 0, 0, 0, 0))
    return pl.pallas_call(
        functools.partial(_ret_scan_kernel, nb, n // RET_T),
        grid=(b, nb),
        in_specs=[fw(0), fw(1), fw(2), bw(0), bw(1), bw(2)] + [full(a) for a in tables] + [st],
        out_specs=[fw(0), bw(0), st],
        out_shape=[jax.ShapeDtypeStruct((b, l, GROUP_W), F32)] * 2
                  + [jax.ShapeDtypeStruct((b, 2, HEADS // 2, HEAD_V, LANES), F32)],
        scratch_shapes=[pltpu.VMEM((2, HEADS // 2, HEAD_V, LANES), F32)],
        compiler_params=_cparams("parallel", "arbitrary"),
        name="retention_scan",
    )(z, z, z, z, z, z, *tables, s0)


def _gla_scan(z, aup, abias, s0):
    b, l, _ = z.shape
    cpb = min(GLA_BLOCK, l) // CHUNK
    nb = l // (cpb * CHUNK)
    blk = lambda w, j, rev: pl.BlockSpec(
        (1, cpb * CHUNK, w), (lambda i, c: (i, nb - 1 - c, j)) if rev else (lambda i, c: (i, c, j)))
    st = pl.BlockSpec((1, 2, HEAD_V, GLA_QK), lambda i, c: (i, 0, 0, 0))
    ofw = pl.BlockSpec((1, cpb * CHUNK, GROUP_W), lambda i, c: (i, c, 0))
    obw = pl.BlockSpec((1, cpb * CHUNK, GROUP_W), lambda i, c: (i, nb - 1 - c, 0))
    return pl.pallas_call(
        functools.partial(_gla_scan_kernel, nb, cpb * CHUNK // GLA_T),
        grid=(b, nb),
        in_specs=[blk(GLA_QK, 0, False), blk(GLA_QK, 1, False), blk(GROUP_W, 1, False), blk(128, 6, False),
                  blk(GLA_QK, 0, True), blk(GLA_QK, 1, True), blk(GROUP_W, 1, True), blk(128, 6, True),
                  pl.BlockSpec(aup.shape, lambda i, c: (0, 0, 0)),
                  pl.BlockSpec(abias.shape, lambda i, c: (0, 0, 0)), st],
        out_specs=[ofw, obw, st],
        out_shape=[jax.ShapeDtypeStruct((b, l, GROUP_W), F32)] * 2
                  + [jax.ShapeDtypeStruct((b, 2, HEAD_V, GLA_QK), F32)],
        scratch_shapes=[pltpu.VMEM((2, HEAD_V, GLA_QK), F32)],
        compiler_params=_cparams("parallel", "arbitrary"),
        name="gla_scan",
    )(z, z, z, z, z, z, z, z, aup, abias, s0)


def _mix_kernel(x_ref, g1_ref, y5_ref, u5_ref, d5_ref, gw_ref, gb_ref,
                ryf_ref, ryb_ref, rbonus_ref, rg_ref, rlng_ref, rlnb_ref,
                tof_ref, tob_ref, tg_ref, tln_ref,
                gof_ref, gob_ref, gg_ref, gln_ref,
                bones_ref, wo_ref, o_ref, y5t_ref):
    bones = bones_ref[...]
    inv = 1.0 / HEAD_V

    def hmean(a):
        return _mm_right01(a, bones) * inv

    nch = y5_ref.shape[2]
    gph = LANES // S5_P
    y5g = [y5_ref[0, g] for g in range(S5_G)]
    for s in range(S5_T):
        for hf in range(GROUP_W // LANES):
            y5t_ref[hf, pl.ds(s, nch, stride=S5_T), :] = jnp.concatenate(
                [yg[:, s * S5_P:(s + 1) * S5_P] for yg in y5g[hf * gph:(hf + 1) * gph]], axis=1)
    y = jnp.concatenate([y5t_ref[0], y5t_ref[1]], axis=1) + d5_ref[...] * u5_ref[0]
    y = _gelu_tanh(y)
    ya = y * _sigmoid(_mm(y, gw_ref[...]) + gb_ref[...])
    yr = ryf_ref[0] + ryb_ref[0]
    dlt = yr - hmean(yr)
    yn = dlt * lax.rsqrt(hmean(dlt * dlt) + GN_EPS)
    yb = (yn * rlng_ref[...] + rlnb_ref[...] + rbonus_ref[0].astype(F32)) * rg_ref[0].astype(F32)
    ot = tof_ref[0] + tob_ref[0]
    yc = ot * lax.rsqrt(hmean(ot * ot) + EPS) * tln_ref[...] * _silu(tg_ref[0])
    og = gof_ref[0] + gob_ref[0]
    yd = og * lax.rsqrt(hmean(og * og) + EPS) * gln_ref[...] * _silu(gg_ref[0])
    mix = (_mm(ya, wo_ref[0]) + _mm(yb, wo_ref[1]) + _mm(yc, wo_ref[2]) + _mm(yd, wo_ref[3]))
    o_ref[0] = x_ref[0] + g1_ref[0] * mix


def _mix(x, g1, y5, u5, rw_pre, rw_y, zret, ret_o, zgla, gla_o, p, tm):
    b, l, d = x.shape
    tok = lambda w, j: pl.BlockSpec((1, tm, w), lambda i, t: (i, t, j))
    vec = lambda a: pl.BlockSpec(a.shape, lambda i, t: (0,) * a.ndim)
    t256 = tok(GROUP_W, 0)
    args = [x, g1, y5, u5, p['s5_d'], p['glu_w'], p['glu_b'],
            rw_y[0], rw_y[1], rw_pre[4], rw_pre[3], p['rw_ln_g'], p['rw_ln_b'],
            ret_o[0], ret_o[1], zret, p['ret_ln_g'],
            gla_o[0], gla_o[1], zgla, p['gla_ln_g'],
            p['bones'], p['w_out']]
    grouped = pl.BlockSpec((1, S5_G, tm // S5_T, S5_TP), lambda i, t: (i, 0, t, 0))
    specs = [tok(d, 0), pl.BlockSpec((1, 1, d), lambda i, t: (i, 0, 0)), grouped, t256,
             vec(p['s5_d']), vec(p['glu_w']), vec(p['glu_b']),
             t256, t256, t256, t256, vec(p['rw_ln_g']), vec(p['rw_ln_b']),
             t256, t256, tok(GROUP_W, 3), vec(p['ret_ln_g']),
             t256, t256, tok(GROUP_W, 2), vec(p['gla_ln_g']),
             vec(p['bones']), vec(p['w_out'])]
    return pl.pallas_call(
        _mix_kernel,
        grid=(b, l // tm),
        in_specs=specs,
        out_specs=tok(d, 0),
        out_shape=jax.ShapeDtypeStruct((b, l, d), F32),
        scratch_shapes=[pltpu.VMEM((GROUP_W // LANES, tm, LANES), F32)],
        compiler_params=_cparams("parallel", "parallel"),
        name="mix_outproj",
    )(*args)


def _mlp_kernel(final, nff, x_ref, g_ref, sc_ref, sh_ref, gate_ref, w1_ref, w2_ref, fg_ref, o_ref):
    x = x_ref[0]
    hb = _modnorm(x, g_ref[...], sc_ref[0], sh_ref[0]).astype(BF16)
    ff = w1_ref.shape[2] // nff
    acc = None
    for j in range(nff):
        a = jnp.maximum(jnp.dot(hb, w1_ref[0, :, j * ff:(j + 1) * ff], preferred_element_type=F32), 0.0)
        part = jnp.dot((a * a).astype(BF16), w2_ref[0, j * ff:(j + 1) * ff, :], preferred_element_type=F32)
        acc = part if acc is None else acc + part
    y = x + gate_ref[0] * acc
    if final:
        ms = jnp.mean(y * y, axis=-1, keepdims=True)
        y = y * lax.rsqrt(ms + EPS) * fg_ref[...]
    o_ref[0] = y


def _mlp(x, g, sc, sh, gate, w1, w2, layer, final_g, final, tm):
    b, l, d = x.shape
    tok = pl.BlockSpec((1, tm, d), lambda i, t: (i, t, 0))
    vec = pl.BlockSpec((1, 1, d), lambda i, t: (i, 0, 0))
    row = pl.BlockSpec((1, d), lambda i, t: (0, 0))
    once = lambda a: pl.BlockSpec((1,) + a.shape[1:], lambda i, t: (layer, 0, 0), pipeline_mode=pl.Buffered(1))
    return pl.pallas_call(
        functools.partial(_mlp_kernel, final, 4),
        grid=(b, l // tm),
        in_specs=[tok, row, vec, vec, vec, once(w1), once(w2), row],
        out_specs=tok,
        out_shape=jax.ShapeDtypeStruct((b, l, d), F32),
        compiler_params=_cparams("parallel", "parallel"),
        name="mlp",
    )(x, g, sc, sh, gate, w1, w2, final_g)


def kernel(x, c, ctx, c_ctx, ada_w, ada_b, norm1_g, norm2_g, w_in, w_out, s5_lam_re, s5_lam_im, s5_log_dt, s5_b_re, s5_b_im, s5_c_re, s5_c_im, s5_d, s5_glu_w, s5_glu_b, rw_mu, rw_w0, rw_w_up, rw_a0, rw_a_up, rw_g_up, rw_k_k, rw_k_a, rw_r_k, rw_ln_g, rw_ln_b, ret_decay_logit, ret_ln_g, gla_a_up, gla_a_b, gla_ln_g, mlp_w1, mlp_w2, final_g):
    b, l, d = x.shape
    lc = ctx.shape[1]
    depth = ada_w.shape[0]
    rows = l // GRID_W
    assert l % TOKEN_TILE == 0 and lc % RET_T == 0 and lc <= TOKEN_TILE and d % LANES == 0
    assert b < COND_ROWS and l % GRID_W == 0

    cond = jnp.zeros((COND_ROWS, d), F32).at[:b].set(c).at[b].set(c_ctx)
    mod = _modulation(cond, ada_w, ada_b)

    lane = jnp.arange(GROUP_W)
    bones = (lane[:, None] // HEAD_V == lane[None, :] // HEAD_V).astype(BF16)
    rope_lat = _rope_tables(rows)
    ident = (jnp.ones((lc // CHUNK, 1, GROUP_W), F32), jnp.zeros((lc // CHUNK, 1, GROUP_W), F32),
             jnp.ones((GRID_W, GROUP_W), F32), jnp.zeros((GRID_W, GROUP_W), F32))
    nlev = max(1, (l // S5_T - 1).bit_length())
    c0, c1, c2 = GROUP_W, GROUP_W + RW_COLS, GROUP_W + RW_COLS + 4 * GROUP_W
    row2 = lambda a: a.reshape(1, -1).astype(F32)
    w_in_b, w_out_b, w1_b, w2_b = _to_bf16(w_in), _to_bf16(w_out), _to_bf16(mlp_w1), _to_bf16(mlp_w2)

    s5_tab_all = jax.vmap(lambda *a: _s5_tables(*a, nlev))(
        s5_lam_re, s5_lam_im, s5_log_dt, s5_b_re, s5_b_im, s5_c_re, s5_c_im)
    ret_tab_all = jax.vmap(lambda a: _ret_tables(a, RET_T))(ret_decay_logit)
    zpad = lambda a, lo, n: jnp.zeros((depth, 2, LANES, n), F32).at[:, :, lo:lo + a.shape[2]].set(a).astype(BF16)
    rw_wup_all = zpad(rw_w_up, 0, GROUP_W)
    rw_aup_all = zpad(rw_a_up, RW_W_RANK, GROUP_W)
    rw_gup_all = jnp.zeros((depth, LANES, GROUP_W), F32).at[:, RW_W_RANK + RW_A_RANK:].set(rw_g_up).astype(BF16)
    gla_aup_all = zpad(gla_a_up, 0, GLA_QK)

    xc = ctx
    for i in range(depth):
        last = i == depth - 1
        m = mod[i].reshape(COND_ROWS, N_ADA, d)
        ml = m[:b, :, None, :]
        mc = jnp.broadcast_to(m[b][None, :, None, :], (b, N_ADA, 1, d))
        wi = w_in_b[i]
        w5, wr, wt = wi[:, :c0], wi[:, c0:c1], wi[:, c1:c2]
        wg = jnp.pad(wi[:, c2:], ((0, 0), (0, GLA_COLS_PAD - (wi.shape[1] - c2))))
        n1 = row2(norm1_g[i])
        n2 = row2(norm2_g[i])

        s5_tab = tuple(a[i] for a in s5_tab_all)
        rwp = dict(
            mu=row2(rw_mu[i]),
            w0=rw_w0[i].reshape(2, 1, GROUP_W), a0=rw_a0[i].reshape(2, 1, GROUP_W),
            wup=rw_wup_all[i], aup=rw_aup_all[i], gup=rw_gup_all[i],
            kk=row2(rw_k_k[i]), ka=row2(rw_k_a[i]), rk=row2(rw_r_k[i]), bones=bones)
        ret_tab = tuple(a[i] for a in ret_tab_all)
        gla_aup = gla_aup_all[i]
        gla_ab = gla_a_b[i].reshape(2, 1, GLA_QK).astype(F32)
        mixp = dict(s5_d=row2(s5_d[i]), glu_w=s5_glu_w[i].astype(BF16), glu_b=row2(s5_glu_b[i]),
                    rw_ln_g=row2(rw_ln_g[i]), rw_ln_b=row2(rw_ln_b[i]), ret_ln_g=row2(ret_ln_g[i]),
                    gla_ln_g=row2(gla_ln_g[i]), bones=bones,
                    w_out=w_out_b[i].reshape(4, GROUP_W, d))

        def mixers(xx, mm, is_lat, states):
            tm = TOKEN_TILE if is_lat else lc
            z5, z5g, zt, zg, pre = _inproj(xx, n1, mm[:, 1], mm[:, 0], w5, wr, wt, wg,
                                           rope_lat if is_lat else ident, rwp, is_lat, tm)
            y5, h5 = _s5_scan(z5g, s5_tab, states[0])
            yrf, yrb, srw = _rw_scan(pre, states[1])
            otf, otb, sret = _ret_scan(zt, ret_tab, states[2])
            ogf, ogb, sgla = _gla_scan(zg, gla_aup, gla_ab, states[3])
            outs = (z5, y5, pre, (yrf, yrb), zt, (otf, otb), zg, (ogf, ogb))
            return outs, (h5, srw, sret, sgla)

        def block(xx, mm, outs, is_lat, fin):
            tm = TOKEN_TILE if is_lat else lc
            z5, y5, pre, yr, zt, ot, zg, og = outs
            x1 = _mix(xx, mm[:, 2], y5, z5, pre, yr, zt, ot, zg, og, mixp, tm)
            return _mlp(x1, n2, mm[:, 4], mm[:, 3], mm[:, 5], w1_b, w2_b, i, row2(final_g), fin, tm)

        zeros = (jnp.zeros((b, 2, S5_G, 1, 2 * S5_N), F32),
                 jnp.zeros((b, 2, HEADS // 2, HEAD_V, LANES), F32),
                 jnp.zeros((b, 2, HEADS // 2, HEAD_V, LANES), F32),
                 jnp.zeros((b, 2, HEAD_V, GLA_QK), F32))
        outs_c, st_c = mixers(xc, mc, False, zeros)
        outs_l, _ = mixers(x, ml, True, st_c)
        x = block(x, ml, outs_l, True, last)
        if not last:
            xc = block(xc, mc, outs_c, False, False)
    return x
```

```python
import functools
import math

import jax
import jax.numpy as jnp
from jax import lax
from jax.experimental import pallas as pl
from jax.experimental.pallas import tpu as pltpu

F32 = jnp.float32
BF16 = jnp.bfloat16

LANES = 128
GRID_W = 64
GROUP_W = 256
N_ADA = 6
EPS = 1e-6
GN_EPS = 64e-5
TOKEN_TILE = 512
CHUNK = 64
RW_CPB = 4
RET_T = 256
RET_BLOCK = 512
GLA_T = 128
GLA_BLOCK = 512
HEADS = 4
HEAD_V = 64
S5_P = 16
S5_G = 16
S5_N = 64
S5_T = 32
S5_TP = S5_T * S5_P
RW_COLS = 896
RW_W_RANK = 32
RW_A_RANK = 32
RW_G_RANK = 64
RET_DK = 64
GLA_DK = 32
GLA_QK = 128
GLA_RANK = 16
GLA_TAU = 16.0
GLA_COLS_PAD = 896
ROPE_BASE = 10000.0
ROPE_HALF = RET_DK // 2
ROPE_NF = RET_DK // 4
COND_ROWS = 8
CAST_ROWS = 256
ADA_TILE = 1536
VMEM_LIMIT = 56 * 1024 * 1024


def _cparams(*sem):
    return pltpu.CompilerParams(dimension_semantics=sem, vmem_limit_bytes=VMEM_LIMIT)


def _mm(a, b):
    return jnp.dot(a.astype(BF16), b.astype(BF16), preferred_element_type=F32)


def _mm_nt(a, b):
    return lax.dot_general(a.astype(BF16), b.astype(BF16), (((1,), (1,)), ((), ())),
                           preferred_element_type=F32)


def _mm_tn(a, b):
    return lax.dot_general(a.astype(BF16), b.astype(BF16), (((0,), (0,)), ((), ())),
                           preferred_element_type=F32)


def _split2(x):
    hi = x.astype(BF16)
    return hi, (x - hi.astype(F32)).astype(BF16)


def _chunk_tri01(n, t, reverse):
    ri = lax.broadcasted_iota(jnp.int32, (n, n), 0)
    ci = lax.broadcasted_iota(jnp.int32, (n, n), 1)
    shift = t.bit_length() - 1
    tri = (ci >= ri) if reverse else (ci <= ri)
    return jnp.where((ri >> shift) == (ci >> shift), jnp.where(tri, 1.0, 0.0), 0.0).astype(BF16)


def _mm_left01(m01, x):
    hi, lo = _split2(x)
    return jnp.dot(m01, hi, preferred_element_type=F32) + jnp.dot(m01, lo, preferred_element_type=F32)


def _mm_right01(x, m01):
    return jnp.dot(x.astype(BF16), m01, preferred_element_type=F32)


def _sigmoid(x):
    return 1.0 / (1.0 + jnp.exp(-x))


def _softplus(x):
    return jnp.maximum(x, 0.0) + jnp.log(1.0 + jnp.exp(-jnp.abs(x)))


def _silu(x):
    return x * _sigmoid(x)


def _gelu_tanh(x):
    return 0.5 * x * (1.0 + jnp.tanh(math.sqrt(2.0 / math.pi) * (x + 0.044715 * x * x * x)))


def _tri_incl(t, reverse):
    ri = lax.broadcasted_iota(jnp.int32, (t, t), 0)
    ci = lax.broadcasted_iota(jnp.int32, (t, t), 1)
    return (ci >= ri) if reverse else (ci <= ri)


def _cast_kernel(x_ref, o_ref):
    o_ref[...] = x_ref[...].astype(o_ref.dtype)


def _to_bf16(w):
    depth, r, c = w.shape
    tr = CAST_ROWS
    spec = pl.BlockSpec((1, tr, c), lambda i, j: (i, j, 0))
    return pl.pallas_call(
        _cast_kernel,
        grid=(depth, r // tr),
        in_specs=[spec],
        out_specs=spec,
        out_shape=jax.ShapeDtypeStruct(w.shape, BF16),
        compiler_params=_cparams("parallel", "parallel"),
        name="cast_bf16",
    )(w)


def _mod_kernel(cond_ref, w_ref, b_ref, o_ref):
    c = cond_ref[...]
    o_ref[0] = _mm(_silu(c), w_ref[0]) + b_ref[0]


def _modulation(cond, ada_w, ada_b):
    depth, d, n = ada_w.shape
    tn = ADA_TILE
    return pl.pallas_call(
        _mod_kernel,
        grid=(depth, n // tn),
        in_specs=[pl.BlockSpec((COND_ROWS, d), lambda i, j: (0, 0)),
                  pl.BlockSpec((1, d, tn), lambda i, j: (i, 0, j)),
                  pl.BlockSpec((1, 1, tn), lambda i, j: (i, 0, j))],
        out_specs=pl.BlockSpec((1, COND_ROWS, tn), lambda i, j: (i, 0, j)),
        out_shape=jax.ShapeDtypeStruct((depth, COND_ROWS, n), F32),
        compiler_params=_cparams("parallel", "parallel"),
        name="adaln_mod",
    )(cond, ada_w, ada_b.reshape(depth, 1, n))


def _modnorm(x, g, sc, sh):
    ms = jnp.mean(x * x, axis=-1, keepdims=True)
    return x * lax.rsqrt(ms + EPS) * g * (1.0 + sc) + sh


def _shift_columns(z, segments):
    tm, width = z.shape
    lane = lax.broadcasted_iota(jnp.int32, (tm, LANES), 1)
    starts = [s for s, _ in segments] + [width]
    cols = []
    for c in range(width // LANES):
        lo, hi = c * LANES, (c + 1) * LANES
        inside = [(max(starts[i], lo), segments[i][1]) for i in range(len(segments))
                  if starts[i] < hi and starts[i + 1] > lo]
        col = inside[-1][1](c)
        for first, fn in reversed(inside[:-1]):
            nxt = [f for f, _ in inside if f > first][0]
            col = jnp.where(lane < nxt - lo, fn(c), col)
        cols.append(col)
    return jnp.concatenate(cols, axis=1)


def _rw_prep_math(grid_shift, z, zp, zn, mu, w0_ref, wup_ref, a0_ref, aup_ref, gup, kkp, ka, rk, bones):
    tm = z.shape[0]
    row = lax.broadcasted_iota(jnp.int32, (tm, LANES), 0)
    col = lambda x, c: x[:, c * LANES:(c + 1) * LANES]
    if grid_shift:
        pos = row & (GRID_W - 1)
        left = lambda c: jnp.where(pos == 0, 0.0, pltpu.roll(col(z, c), 1, 0))
        right = lambda c: jnp.where(pos == GRID_W - 1, 0.0, pltpu.roll(col(z, c), tm - 1, 0))
        up = lambda c: jnp.concatenate([col(zp, c), col(z, c)[:tm - GRID_W]], axis=0)
        down = lambda c: jnp.concatenate([col(z, c)[GRID_W:], col(zn, c)], axis=0)
        q = RW_COLS // 4
        shifted = _shift_columns(z, [(0, left), (q, right), (2 * q, up), (3 * q, down)])
    else:
        prev = lambda c: jnp.where(row == 0, 0.0, pltpu.roll(col(z, c), 1, 0))
        nxt = lambda c: jnp.where(row == tm - 1, 0.0, pltpu.roll(col(z, c), tm - 1, 0))
        shifted = _shift_columns(z, [(0, prev), (RW_COLS // 2, nxt)])
    zm = z + mu * (shifted - z)
    r = zm[:, 0:GROUP_W]
    k = zm[:, GROUP_W:2 * GROUP_W]
    v = zm[:, 2 * GROUP_W:3 * GROUP_W]
    lo = zm[:, 3 * GROUP_W:RW_COLS]
    g = _mm(_sigmoid(lo), gup)
    kk = k * kkp
    kk = kk * lax.rsqrt(_mm_right01(kk * kk, bones) + 1e-12)
    bonus = _mm_right01(r * k * rk, bones) * v
    outs = [r.astype(BF16), v.astype(BF16), kk.astype(BF16), g.astype(BF16), bonus.astype(BF16)]
    tlo = jnp.tanh(lo)
    for d in range(2):
        log_decay = -math.exp(-0.5) * _sigmoid(w0_ref[d] + _mm(tlo, wup_ref[d]))
        a = _sigmoid(a0_ref[d] + _mm(lo, aup_ref[d]))
        outs += [log_decay,
                 (k * (1.0 + (a - 1.0) * ka)).astype(BF16), a.astype(BF16)]
    return outs


def _inproj_kernel(grid_shift, nt, x_ref, xp_ref, xn_ref, g_ref, sc_ref, sh_ref,
                   w5_ref, wr_ref, wt_ref, wg_ref, cr_ref, sr_ref, cc_ref, sn_ref,
                   mu_ref, w0_ref, wup_ref, a0_ref, aup_ref, gup_ref, kk_ref, ka_ref, rk_ref, bones_ref,
                   o5_ref, o5g_ref, ot_ref, og_ref, *rest):
    rw_refs, z5h_ref = rest[:-1], rest[-1]
    j = pl.program_id(1)
    norm = lambda x: _modnorm(x, g_ref[...], sc_ref[0], sh_ref[0]).astype(BF16)
    hb = norm(x_ref[0])
    z5 = jnp.dot(hb, w5_ref[...], preferred_element_type=F32)
    o5_ref[0] = z5
    og_ref[0] = jnp.dot(hb, wg_ref[...], preferred_element_type=F32)
    zr = jnp.dot(hb, wr_ref[...], preferred_element_type=F32)
    zp = zn = None
    if grid_shift:
        zp = jnp.where(j > 0, jnp.dot(norm(xp_ref[0]), wr_ref[...], preferred_element_type=F32), 0.0)
        zn = jnp.where(j < nt - 1, jnp.dot(norm(xn_ref[0]), wr_ref[...], preferred_element_type=F32), 0.0)
    rw = _rw_prep_math(grid_shift, zr, zp, zn, mu_ref[...], w0_ref, wup_ref, a0_ref, aup_ref, gup_ref[...],
                       kk_ref[...], ka_ref[...], rk_ref[...], bones_ref[...])
    for ref, val in zip(rw_refs, rw):
        ref[0] = val
    zt = jnp.dot(hb, wt_ref[...], preferred_element_type=F32)
    lane = lax.broadcasted_iota(jnp.int32, (GRID_W, GROUP_W), 1)
    by_row = (lane & ROPE_HALF) == 0
    nrow = cr_ref.shape[0]
    cos = jnp.concatenate([jnp.where(by_row, cr_ref[j], cc_ref[...]) for j in range(nrow)], axis=0)
    sin = jnp.concatenate([jnp.where(by_row, sr_ref[j], sn_ref[...]) for j in range(nrow)], axis=0)
    ot_ref[0, :, 0:GROUP_W] = _rope(zt[:, 0:GROUP_W], cos, sin)
    ot_ref[0, :, GROUP_W:2 * GROUP_W] = _rope(zt[:, GROUP_W:2 * GROUP_W] * RET_DK ** -0.5, cos, sin)
    ot_ref[0, :, 2 * GROUP_W:] = zt[:, 2 * GROUP_W:]
    nch = o5g_ref.shape[2]
    gph = LANES // S5_P
    for hf in range(GROUP_W // LANES):
        z5h_ref[hf] = z5[:, hf * LANES:(hf + 1) * LANES]
    for s in range(S5_T):
        for hf in range(GROUP_W // LANES):
            rows = z5h_ref[hf, pl.ds(s, nch, stride=S5_T), :]
            for g in range(gph):
                o5g_ref[0, hf * gph + g, :, s * S5_P:(s + 1) * S5_P] = rows[:, g * S5_P:(g + 1) * S5_P]


def _inproj(x, g, sc, sh, w5, wr, wt, wg, rope, rwp, grid_shift, tm):
    b, l, d = x.shape
    nt = l // tm
    hb = tm // GRID_W
    nhb = l // GRID_W
    tok = lambda n: pl.BlockSpec((1, tm, n), lambda i, j: (i, j, 0))
    vec = pl.BlockSpec((1, 1, d), lambda i, j: (i, 0, 0))
    full = lambda a: pl.BlockSpec(a.shape, lambda i, j: (0,) * a.ndim)
    grouped = pl.BlockSpec((1, S5_G, tm // S5_T, S5_TP), lambda i, j: (i, 0, j, 0))
    rowt = pl.BlockSpec((hb, 1, GROUP_W), lambda i, j: (j, 0, 0))
    above = pl.BlockSpec((1, GRID_W, d), lambda i, j: (i, jnp.maximum(j * hb - 1, 0), 0))
    below = pl.BlockSpec((1, GRID_W, d), lambda i, j: (i, jnp.minimum((j + 1) * hb, nhb - 1), 0))
    cr, sr, cc, sn = rope
    params = (rwp['mu'], rwp['w0'], rwp['wup'], rwp['a0'], rwp['aup'], rwp['gup'], rwp['kk'], rwp['ka'],
              rwp['rk'], rwp['bones'])
    rw_dtypes = (BF16,) * 5 + (F32, BF16, BF16) * 2
    outs = pl.pallas_call(
        functools.partial(_inproj_kernel, grid_shift, nt),
        grid=(b, nt),
        in_specs=[tok(d), above, below, pl.BlockSpec((1, d), lambda i, j: (0, 0)), vec, vec,
                  full(w5), full(wr), full(wt), full(wg), rowt, rowt, full(cc), full(sn)]
                 + [full(a) for a in params],
        out_specs=[tok(GROUP_W), grouped, tok(wt.shape[1]), tok(wg.shape[1])] + [tok(GROUP_W)] * 11,
        out_shape=[jax.ShapeDtypeStruct((b, l, GROUP_W), F32),
                   jax.ShapeDtypeStruct((b, S5_G, l // S5_T, S5_TP), F32),
                   jax.ShapeDtypeStruct((b, l, wt.shape[1]), F32),
                   jax.ShapeDtypeStruct((b, l, wg.shape[1]), F32)]
                  + [jax.ShapeDtypeStruct((b, l, GROUP_W), dt) for dt in rw_dtypes],
        scratch_shapes=[pltpu.VMEM((GROUP_W // LANES, tm, LANES), F32)],
        compiler_params=_cparams("parallel", "parallel"),
        name="norm_inproj",
    )(x, x, x, g, sc, sh, w5, wr, wt, wg, cr, sr, cc, sn, *params)
    return outs[0], outs[1], outs[2], outs[3], tuple(outs[4:])


def _s5_tables(lam_re, lam_im, log_dt, b_re, b_im, c_re, c_im, nlev):
    hp = lax.Precision.HIGHEST
    t = S5_T
    lam = lax.complex(jnp.minimum(lam_re.astype(F32), -1e-4), lam_im.astype(F32))
    ldt = lam * jnp.exp(log_dt.astype(F32))[..., None]
    a_bar = jnp.exp(ldt)
    bb = ((a_bar - 1.0) / lam)[..., None] * lax.complex(b_re.astype(F32), b_im.astype(F32))
    cm = lax.complex(c_re.astype(F32), c_im.astype(F32))
    tau = jnp.arange(t + 1, dtype=F32)
    apow = jnp.exp(ldt[:, :, None, :] * tau[None, None, :, None])
    taps = jnp.einsum('dgpn,dgtn,dgnq->dgtpq', cm, apow[:, :, :t], bb, precision=hp).real
    taprow = jnp.stack([taps[0], taps[1][:, ::-1]]).transpose(0, 1, 4, 2, 3).reshape(2, S5_G, S5_P, S5_TP)

    def pack(zc):
        return jnp.concatenate([zc.real, zc.imag], axis=-1)

    win_f = apow[0][:, t - 1 - jnp.arange(t), None, :] * bb[0].transpose(0, 2, 1)[:, None]
    win_b = apow[1][:, jnp.arange(t), None, :] * bb[1].transpose(0, 2, 1)[:, None]
    win = jnp.stack([pack(win_f), pack(win_b)]).reshape(2, S5_G, S5_TP, 2 * S5_N)
    ca_f = cm[0][:, None] * apow[0][:, 1 + jnp.arange(t), None, :]
    ca_b = cm[1][:, None] * apow[1][:, t - jnp.arange(t), None, :]

    def outpack(ca):
        w = jnp.concatenate([ca.real, -ca.imag], axis=-1)
        return w.reshape(S5_G, S5_TP, 2 * S5_N).transpose(0, 2, 1)

    wout = jnp.stack([outpack(ca_f), outpack(ca_b)])
    lev = (2.0 ** jnp.arange(nlev, dtype=F32)) * t
    pw = jnp.exp(ldt[:, :, None, :] * lev[None, None, :, None])
    p1 = jnp.concatenate([pw.real, pw.real], axis=-1)
    p2 = jnp.concatenate([-pw.imag, pw.imag], axis=-1)
    pw = jnp.stack([p1, p2], axis=3)
    return taprow, win.astype(BF16), wout.astype(BF16), pw


def _s5_kernel(nc, nlev, u_ref, tap_ref, win_ref, wout_ref, pw_ref, h0_ref, y_ref, hfin_ref, conv_ref):
    u = u_ref[0, 0].astype(BF16)
    row = lax.broadcasted_iota(jnp.int32, (nc, 2 * S5_N), 0)
    lane = lax.broadcasted_iota(jnp.int32, (S5_P, S5_TP), 1)
    for s in range(S5_T):
        lo = s * S5_P
        fwd = tap_ref[0, 0] if s == 0 else jnp.where(lane >= lo, pltpu.roll(tap_ref[0, 0], lo, 1), 0.0)
        hi = lo + S5_P
        bwd = tap_ref[1, 0] if hi == S5_TP else jnp.where(lane < hi, pltpu.roll(tap_ref[1, 0], hi, 1), 0.0)
        conv_ref[0, lo:hi, :] = fwd.astype(BF16)
        conv_ref[1, lo:hi, :] = bwd.astype(BF16)

    def cmul(x, d, j):
        return pw_ref[d, 0, j, 0:1] * x + pw_ref[d, 0, j, 1:2] * pltpu.roll(x, S5_N, 1)

    v = [jnp.dot(u, win_ref[d, 0], preferred_element_type=F32) for d in range(2)]
    x = [jnp.where(row == 0, h0_ref[0, 0, 0], pltpu.roll(v[0], 1, 0)),
         jnp.where(row == nc - 1, h0_ref[0, 1, 0], pltpu.roll(v[1], nc - 1, 0))]
    for j in range(nlev):
        sh = 2 ** j
        xs = [jnp.where(row >= sh, pltpu.roll(x[0], sh, 0), 0.0),
              jnp.where(row < nc - sh, pltpu.roll(x[1], nc - sh, 0), 0.0)]
        x = [x[d] + cmul(xs[d], d, j) for d in range(2)]
    for d, last in enumerate((nc - 1, 0)):
        hfin_ref[0, d, 0] = cmul(x[d][last:last + 1], d, 0) + v[d][last:last + 1]
    y_ref[0, 0] = (jnp.dot(u, conv_ref[0], preferred_element_type=F32) + _mm(x[0], wout_ref[0, 0])
                   + jnp.dot(u, conv_ref[1], preferred_element_type=F32) + _mm(x[1], wout_ref[1, 0]))


def _s5_scan(uf, tables, h0):
    conv, win, wout, pw = tables
    b, _, nc, _ = uf.shape
    nlev = max(1, (nc - 1).bit_length())
    pw = pw[:, :, :nlev]
    n2 = 2 * S5_N
    y, hfin = pl.pallas_call(
        functools.partial(_s5_kernel, nc, nlev),
        grid=(b, S5_G),
        in_specs=[pl.BlockSpec((1, 1, nc, S5_TP), lambda i, g: (i, g, 0, 0)),
                  pl.BlockSpec((2, 1, S5_P, S5_TP), lambda i, g: (0, g, 0, 0)),
                  pl.BlockSpec((2, 1, S5_TP, n2), lambda i, g: (0, g, 0, 0)),
                  pl.BlockSpec((2, 1, n2, S5_TP), lambda i, g: (0, g, 0, 0)),
                  pl.BlockSpec((2, 1, nlev, 2, n2), lambda i, g: (0, g, 0, 0, 0)),
                  pl.BlockSpec((1, 2, 1, 1, n2), lambda i, g: (i, 0, g, 0, 0))],
        out_specs=[pl.BlockSpec((1, 1, nc, S5_TP), lambda i, g: (i, g, 0, 0)),
                   pl.BlockSpec((1, 2, 1, 1, n2), lambda i, g: (i, 0, g, 0, 0))],
        out_shape=[jax.ShapeDtypeStruct((b, S5_G, nc, S5_TP), F32),
                   jax.ShapeDtypeStruct((b, 2, S5_G, 1, n2), F32)],
        scratch_shapes=[pltpu.VMEM((2, S5_TP, S5_TP), BF16)],
        compiler_params=_cparams("parallel", "parallel"),
        name="s5_scan",
    )(uf, conv, win, wout, pw, h0)
    return y, hfin


def _rw_scan_kernel(nb, cpb, rf_ref, vf_ref, kkf_ref, lwf_ref, kdf_ref, asf_ref,
                    rb_ref, vb_ref, kkb_ref, lwb_ref, kdb_ref, asb_ref, s0_ref,
                    yf_ref, yb_ref, sfin_ref, st_ref):
    c = pl.program_id(1)

    @pl.when(c == 0)
    def _():
        st_ref[...] = s0_ref[0]

    t = CHUNK
    ri = lax.broadcasted_iota(jnp.int32, (t, LANES), 0)
    li = lax.broadcasted_iota(jnp.int32, (t, LANES), 1)
    ci = li & (t - 1)
    low = li < HEAD_V
    eyef = jnp.where(ri == ci, 1.0, 0.0)
    same = lambda s: jnp.where((ri >> s) == (ci >> s), 1.0, 0.0)
    m4, m8, m16, m32 = same(2), same(3), same(4), same(5)
    merge_masks = (m8 - m4, m16 - m8, m32 - m16, 1.0 - m32)

    def bd(x):
        return jnp.concatenate([jnp.where(low, x, 0.0), jnp.where(low, 0.0, x)], axis=0)

    def diag_blocks(full):
        return jnp.where(low, full[0:HEAD_V], full[HEAD_V:2 * HEAD_V])

    dir_refs = ((rf_ref, vf_ref, kkf_ref, lwf_ref, kdf_ref, asf_ref),
                (rb_ref, vb_ref, kkb_ref, lwb_ref, kdb_ref, asb_ref))
    units = []
    for d, (r_ref, v_ref, kk_ref, lw_ref, kd_ref, as_ref) in enumerate(dir_refs):
        incl = (ci >= ri) if d == 1 else (ci <= ri)
        strict = (ci > ri) if d == 1 else (ci < ri)
        r, v, kk, kd = (x[0].astype(F32) for x in (r_ref, v_ref, kk_ref, kd_ref))
        lw = lw_ref[0]
        cin = _mm_left01(_chunk_tri01(cpb * t, t, d == 1), lw)
        e_in = jnp.exp(cin)
        e_neg = jnp.exp(-cin)
        rt = r * e_in
        at = -kk * jnp.exp(cin - lw)
        bvec = kk * as_ref[0].astype(F32)
        bt = bvec * e_neg
        kt = kd * e_neg
        for j in range(cpb):
            rows = slice(j * t, (j + 1) * t)
            last = j * t + (0 if d == 1 else t - 1)
            clast = cin[last:last + 1]
            dl = jnp.exp(clast - cin[rows])
            bh = bvec[rows] * dl
            kh = kd[rows] * dl
            dec = jnp.exp(clast)
            for p in range(HEADS // 2):
                sl = slice(p * LANES, (p + 1) * LANES)
                units.append(dict(d=d, j=j, p=p, incl=incl, strict=strict,
                                  at=at[rows, sl], rt=rt[rows, sl], bt=bt[rows, sl], kt=kt[rows, sl],
                                  bh=bh[:, sl], kh=kh[:, sl], v=v[rows, sl], dec=dec[:, sl]))

    x1 = [jnp.concatenate([u['at'], u['rt']], axis=0) for u in units]
    scores = [_mm_nt(x, jnp.concatenate([bd(u['bt']), bd(u['kt'])], axis=0))
              for x, u in zip(x1, units)]
    nmat = [jnp.where(u['strict'], a[0:t, :LANES], 0.0) for u, a in zip(units, scores)]
    a_rb = [jnp.where(u['incl'], a[t:2 * t, :LANES], 0.0) for u, a in zip(units, scores)]
    a_kk = [jnp.concatenate([jnp.where(u['strict'], a[0:t, LANES:], 0.0),
                             jnp.where(u['incl'], a[t:2 * t, LANES:], 0.0)], axis=0)
            for u, a in zip(units, scores)]
    akv = [_mm(a, bd(u['v'])) for u, a in zip(units, a_kk)]
    kv = [diag_blocks(_mm_tn(u['v'], u['kh'])) for u in units]
    nd = [x * m4 for x in nmat]
    n2 = [_mm(x, bd(x)) for x in nd]
    tinv = [eyef + x + _mm(eyef + x, bd(y)) for x, y in zip(nd, n2)]
    for mk in merge_masks:
        w = [_mm(ti, bd(x * mk)) for ti, x in zip(tinv, nmat)]
        tinv = [ti + _mm(wi, bd(ti)) for ti, wi in zip(tinv, w)]
    zz = [_mm(ti, jnp.concatenate([bd(u['at']), bd(kvv[0:t])], axis=1))
          for ti, u, kvv in zip(tinv, units, akv)]
    ght = [_mm_tn(z, u['bh']) for u, z in zip(units, zz)]
    qy = [_mm(a, jnp.concatenate([bd(z[:, :LANES]), bd(z[:, LANES:])], axis=1))
          for a, z in zip(a_rb, zz)]
    qmat = [u['rt'] + x[:, :LANES] for u, x in zip(units, qy)]
    y0 = [x[:, LANES:] + kvv[t:2 * t] for x, kvv in zip(qy, akv)]
    gmat = [diag_blocks(x[:LANES]) for x in ght]
    hmat = [diag_blocks(x[LANES:]) + k2 for x, k2 in zip(ght, kv)]
    idx = {(u['d'], u['j'], u['p']): n for n, u in enumerate(units)}
    npair = HEADS // 2
    state = {(d, p): st_ref[d, p] for d in range(2) for p in range(npair)}
    ys = {}
    for step in range(cpb):
        for d in range(2):
            j = step if d == 0 else cpb - 1 - step
            for p in range(npair):
                n = idx[(d, j, p)]
                st = state[(d, p)]
                ys[(d, j, p)] = _mm_nt(qmat[n], bd(st)) + y0[n]
                state[(d, p)] = units[n]['dec'] * st + _mm(st, bd(gmat[n])) + hmat[n]
    for d, y_ref in enumerate((yf_ref, yb_ref)):
        y_ref[0] = jnp.concatenate(
            [jnp.concatenate([ys[(d, j, p)] for p in range(npair)], axis=1) for j in range(cpb)], axis=0)
        for p in range(npair):
            st_ref[d, p] = state[(d, p)]

    @pl.when(c == nb - 1)
    def _():
        sfin_ref[0] = st_ref[...]


def _rw_scan(pre, s0):
    r, v, kk, _, _, lw0, kd0, as0, lw1, kd1, as1 = pre
    b, l, _ = r.shape
    cpb = min(RW_CPB, l // CHUNK)
    nb = l // (cpb * CHUNK)
    fw = pl.BlockSpec((1, cpb * CHUNK, GROUP_W), lambda i, c: (i, c, 0))
    bw = pl.BlockSpec((1, cpb * CHUNK, GROUP_W), lambda i, c: (i, nb - 1 - c, 0))
    st = pl.BlockSpec((1, 2, HEADS // 2, HEAD_V, LANES), lambda i, c: (i, 0, 0, 0, 0))
    return pl.pallas_call(
        functools.partial(_rw_scan_kernel, nb, cpb),
        grid=(b, nb),
        in_specs=[fw] * 6 + [bw] * 6 + [st],
        out_specs=[fw, bw, st],
        out_shape=[jax.ShapeDtypeStruct((b, l, GROUP_W), F32)] * 2
                  + [jax.ShapeDtypeStruct((b, 2, HEADS // 2, HEAD_V, LANES), F32)],
        scratch_shapes=[pltpu.VMEM((2, HEADS // 2, HEAD_V, LANES), F32)],
        compiler_params=_cparams("parallel", "arbitrary"),
        name="rwkv_scan",
    )(r, v, kk, lw0, kd0, as0, r, v, kk, lw1, kd1, as1, s0)


def _rope(x, cos, sin_signed):
    lane = lax.broadcasted_iota(jnp.int32, x.shape, 1)
    n = x.shape[1]
    swapped = jnp.where((lane & ROPE_NF) == 0, pltpu.roll(x, n - ROPE_NF, 1), pltpu.roll(x, ROPE_NF, 1))
    return x * cos + swapped * sin_signed


def _ret_scan_kernel(nb, cpb, qf_ref, kf_ref, vf_ref, qb_ref, kb_ref, vb_ref,
                     dmat_ref, qdec_ref, kdec_ref, sdec_ref, s0_ref,
                     of_ref, ob_ref, sfin_ref, st_ref):
    c = pl.program_id(1)

    @pl.when(c == 0)
    def _():
        st_ref[...] = s0_ref[0]

    t = dmat_ref.shape[-1]
    npair = HEADS // 2
    low_t = lax.broadcasted_iota(jnp.int32, (t, LANES), 1) < HEAD_V
    low_s = lax.broadcasted_iota(jnp.int32, (HEAD_V, LANES), 1) < HEAD_V

    def bd(x, low):
        return jnp.concatenate([jnp.where(low, x, 0.0), jnp.where(low, 0.0, x)], axis=0)

    units = []
    for d, (q_ref, k_ref, v_ref) in enumerate(((qf_ref, kf_ref, vf_ref), (qb_ref, kb_ref, vb_ref))):
        for j in range(cpb):
            rows = slice(j * t, (j + 1) * t)
            q, k, v = q_ref[0, rows], k_ref[0, rows], v_ref[0, rows]
            qd = q * qdec_ref[d]
            kh = k * kdec_ref[d]
            for p in range(npair):
                sl = slice(p * LANES, (p + 1) * LANES)
                units.append(dict(d=d, j=j, p=p, q=q[:, sl], k=k[:, sl], qd=qd[:, sl], kh=kh[:, sl],
                                  v=v[:, sl], dec=sdec_ref[d][:, sl]))
    stacked = [_mm_nt(jnp.concatenate([jnp.where(low_t, u['q'], 0.0), jnp.where(low_t, 0.0, u['q'])], axis=0),
                      u['k']) for u in units]
    scores = [[s2[h * t:(h + 1) * t].astype(BF16) * dmat_ref[u['d'], 2 * u['p'] + h] for h in range(2)]
              for s2, u in zip(stacked, units)]
    intra = [_mm(jnp.concatenate(s2, axis=1), bd(u['v'], low_t)) for s2, u in zip(scores, units)]
    kv = []
    for u in units:
        full = _mm_tn(u['v'], u['kh'])
        kv.append(jnp.where(low_s, full[0:HEAD_V], full[HEAD_V:2 * HEAD_V]))
    idx = {(u['d'], u['j'], u['p']): i for i, u in enumerate(units)}
    entering = {}
    for d in range(2):
        for p in range(npair):
            st = st_ref[d, p]
            for step in range(cpb):
                j = step if d == 0 else cpb - 1 - step
                i = idx[(d, j, p)]
                entering[i] = st
                st = units[i]['dec'] * st + kv[i]
            st_ref[d, p] = st
    outs = [x + _mm_nt(u['qd'], bd(entering[i], low_s)) for i, (x, u) in enumerate(zip(intra, units))]
    for d, o_ref in enumerate((of_ref, ob_ref)):
        o_ref[0] = jnp.concatenate(
            [jnp.concatenate([outs[idx[(d, j, p)]] for p in range(npair)], axis=1) for j in range(cpb)], axis=0)

    @pl.when(c == nb - 1)
    def _():
        sfin_ref[0] = st_ref[...]


def _ret_tables(decay_logit, n):
    lg = jax.nn.log_sigmoid(decay_logit.astype(F32))
    pos = jnp.arange(n, dtype=F32)
    lag = pos[:, None] - pos[None, :]
    lag = jnp.stack([lag, -lag])
    dmat = jnp.where(lag[:, None] >= 0, jnp.exp(lg[:, :, None, None] * lag[:, None]), 0.0)
    lanes = jnp.repeat(lg, RET_DK, axis=-1)[:, None, :]
    qpow = jnp.stack([pos + 1.0, n - pos])[:, :, None]
    kpow = jnp.stack([n - 1.0 - pos, pos])[:, :, None]
    return dmat.astype(BF16), jnp.exp(lanes * qpow), jnp.exp(lanes * kpow), jnp.exp(lanes * n)


def _gla_scan_kernel(nb, cpb, qf_ref, kf_ref, vf_ref, af_ref, qb_ref, kb_ref, vb_ref, ab_ref,
                     aup_ref, abias_ref, s0_ref, of_ref, ob_ref, sfin_ref, st_ref):
    c = pl.program_id(1)

    @pl.when(c == 0)
    def _():
        st_ref[...] = s0_ref[0]

    t = GLA_T
    n = cpb * t
    dk = GLA_DK
    scale = dk ** -0.5
    dirs = ((qf_ref, kf_ref, vf_ref, af_ref), (qb_ref, kb_ref, vb_ref, ab_ref))
    units = []
    for d, (q_ref, k_ref, v_ref, a_ref) in enumerate(dirs):
        incl = _tri_incl(t, d == 1)
        q, k, v = q_ref[0], k_ref[0] * scale, v_ref[0]
        lw = -_softplus(-(_mm(a_ref[0], aup_ref[d]) + abias_ref[d])) * (1.0 / GLA_TAU)
        cin = _mm_left01(_chunk_tri01(n, t, d == 1), lw)
        qe = q * jnp.exp(cin)
        for j in range(cpb):
            rows = slice(j * t, (j + 1) * t)
            last = j * t + (0 if d == 1 else t - 1)
            mid = j * t + (t // 2 if d == 1 else t // 2 - 1)
            cj = cin[rows]
            clast = cin[last:last + 1]
            cmid = cin[mid:mid + 1]
            qt = q[rows] * jnp.exp(cj - cmid)
            kt = k[rows] * jnp.exp(cmid - cj)
            kh = k[rows] * jnp.exp(clast - cj)
            units.append(dict(d=d, j=j, incl=incl, qt=qt, kt=kt, qe=qe[rows], kh=kh, v=v[rows],
                              dec=jnp.exp(clast)))

    klane = lax.broadcasted_iota(jnp.int32, (t, LANES), 1) // dk
    vlow = lax.broadcasted_iota(jnp.int32, (t, LANES), 1) < HEAD_V
    khead = lax.broadcasted_iota(jnp.int32, (HEAD_V, LANES), 1) // dk
    npair = HEADS // 2

    def v_blockdiag(x):
        return jnp.concatenate([jnp.where(vlow, x, 0.0), jnp.where(vlow, 0.0, x)], axis=0)

    def state_rows(st, p):
        return jnp.concatenate([jnp.where(khead == 2 * p, st, 0.0), jnp.where(khead == 2 * p + 1, st, 0.0)],
                               axis=0)

    stacked = [_mm_nt(jnp.concatenate([jnp.where(klane == h, u['qt'], 0.0) for h in range(HEADS)], axis=0),
                      u['kt']) for u in units]
    amat = [[jnp.where(u['incl'], s4[h * t:(h + 1) * t], 0.0) for h in range(HEADS)]
            for s4, u in zip(stacked, units)]
    intra = [[_mm(jnp.concatenate(a[2 * p:2 * p + 2], axis=1), v_blockdiag(u['v'][:, p * LANES:(p + 1) * LANES]))
              for p in range(npair)] for a, u in zip(amat, units)]
    kv = []
    for u in units:
        full = _mm_tn(u['v'], u['kh'])
        blocks = [full[h * HEAD_V:(h + 1) * HEAD_V] for h in range(HEADS)]
        acc = blocks[HEADS - 1]
        for h in range(HEADS - 2, -1, -1):
            acc = jnp.where(khead == h, blocks[h], acc)
        kv.append(acc)
    idx = {(u['d'], u['j']): i for i, u in enumerate(units)}
    entering = {}
    for d in range(2):
        st = st_ref[d]
        for step in range(cpb):
            j = step if d == 0 else cpb - 1 - step
            i = idx[(d, j)]
            entering[i] = st
            st = units[i]['dec'] * st + kv[i]
        st_ref[d] = st
    inter = [_mm_nt(u['qe'], jnp.concatenate([state_rows(entering[i], p) for p in range(npair)], axis=0))
             for i, u in enumerate(units)]
    outs = [[x[p] + y[:, p * LANES:(p + 1) * LANES] for p in range(npair)] for x, y in zip(intra, inter)]
    for d, o_ref in enumerate((of_ref, ob_ref)):
        o_ref[0] = jnp.concatenate(
            [jnp.concatenate(outs[idx[(d, j)]], axis=1) for j in range(cpb)], axis=0)

    @pl.when(c == nb - 1)
    def _():
        sfin_ref[0] = st_ref[...]


def _rope_tables(rows):
    nf = ROPE_NF
    inv = ROPE_BASE ** (-jnp.arange(nf, dtype=F32) / nf)
    lane = jnp.arange(GROUP_W)
    freq = inv[lane % nf]
    sign = jnp.where((lane & ROPE_NF) == 0, -1.0, 1.0)
    ar = jnp.arange(rows, dtype=F32)[:, None] * freq[None, :]
    ac = jnp.arange(GRID_W, dtype=F32)[:, None] * freq[None, :]
    return (jnp.cos(ar).reshape(rows, 1, GROUP_W), (jnp.sin(ar) * sign).reshape(rows, 1, GROUP_W),
            jnp.cos(ac), jnp.sin(ac) * sign)


def _ret_scan(z, tables, s0):
    b, l, _ = z.shape
    n = min(RET_BLOCK, l)
    nb = l // n
    fw = lambda j: pl.BlockSpec((1, n, GROUP_W), lambda i, c: (i, c, j))
    bw = lambda j: pl.BlockSpec((1, n, GROUP_W), lambda i, c: (i, nb - 1 - c, j))
    full = lambda a: pl.BlockSpec(a.shape, lambda i, c: (0,) * a.ndim)
    st = pl.BlockSpec((1, 2, HEADS // 2, HEAD_V, LANES), lambda i, c: (i, 0, 0, 0, 0))
    return pl.pallas_call(
        functools.partial(_ret_scan_kernel, nb, n // RET_T),
        grid=(b, nb),
        in_specs=[fw(0), fw(1), fw(2), bw(0), bw(1), bw(2)] + [full(a) for a in tables] + [st],
        out_specs=[fw(0), bw(0), st],
        out_shape=[jax.ShapeDtypeStruct((b, l, GROUP_W), F32)] * 2
                  + [jax.ShapeDtypeStruct((b, 2, HEADS // 2, HEAD_V, LANES), F32)],
        scratch_shapes=[pltpu.VMEM((2, HEADS // 2, HEAD_V, LANES), F32)],
        compiler_params=_cparams("parallel", "arbitrary"),
        name="retention_scan",
    )(z, z, z, z, z, z, *tables, s0)


def _gla_scan(z, aup, abias, s0):
    b, l, _ = z.shape
    cpb = min(GLA_BLOCK, l) // CHUNK
    nb = l // (cpb * CHUNK)
    blk = lambda w, j, rev: pl.BlockSpec(
        (1, cpb * CHUNK, w), (lambda i, c: (i, nb - 1 - c, j)) if rev else (lambda i, c: (i, c, j)))
    st = pl.BlockSpec((1, 2, HEAD_V, GLA_QK), lambda i, c: (i, 0, 0, 0))
    ofw = pl.BlockSpec((1, cpb * CHUNK, GROUP_W), lambda i, c: (i, c, 0))
    obw = pl.BlockSpec((1, cpb * CHUNK, GROUP_W), lambda i, c: (i, nb - 1 - c, 0))
    return pl.pallas_call(
        functools.partial(_gla_scan_kernel, nb, cpb * CHUNK // GLA_T),
        grid=(b, nb),
        in_specs=[blk(GLA_QK, 0, False), blk(GLA_QK, 1, False), blk(GROUP_W, 1, False), blk(128, 6, False),
                  blk(GLA_QK, 0, True), blk(GLA_QK, 1, True), blk(GROUP_W, 1, True), blk(128, 6, True),
                  pl.BlockSpec(aup.shape, lambda i, c: (0, 0, 0)),
                  pl.BlockSpec(abias.shape, lambda i, c: (0, 0, 0)), st],
        out_specs=[ofw, obw, st],
        out_shape=[jax.ShapeDtypeStruct((b, l, GROUP_W), F32)] * 2
                  + [jax.ShapeDtypeStruct((b, 2, HEAD_V, GLA_QK), F32)],
        scratch_shapes=[pltpu.VMEM((2, HEAD_V, GLA_QK), F32)],
        compiler_params=_cparams("parallel", "arbitrary"),
        name="gla_scan",
    )(z, z, z, z, z, z, z, z, aup, abias, s0)


def _mix_kernel(x_ref, g1_ref, y5_ref, u5_ref, d5_ref, gw_ref, gb_ref,
                ryf_ref, ryb_ref, rbonus_ref, rg_ref, rlng_ref, rlnb_ref,
                tof_ref, tob_ref, tg_ref, tln_ref,
                gof_ref, gob_ref, gg_ref, gln_ref,
                bones_ref, wo_ref, o_ref, y5t_ref):
    bones = bones_ref[...]
    inv = 1.0 / HEAD_V

    def hmean(a):
        return _mm_right01(a, bones) * inv

    nch = y5_ref.shape[2]
    gph = LANES // S5_P
    y5g = [y5_ref[0, g] for g in range(S5_G)]
    for s in range(S5_T):
        for hf in range(GROUP_W // LANES):
            y5t_ref[hf, pl.ds(s, nch, stride=S5_T), :] = jnp.concatenate(
                [yg[:, s * S5_P:(s + 1) * S5_P] for yg in y5g[hf * gph:(hf + 1) * gph]], axis=1)
    y = jnp.concatenate([y5t_ref[0], y5t_ref[1]], axis=1) + d5_ref[...] * u5_ref[0]
    y = _gelu_tanh(y)
    ya = y * _sigmoid(_mm(y, gw_ref[...]) + gb_ref[...])
    yr = ryf_ref[0] + ryb_ref[0]
    dlt = yr - hmean(yr)
    yn = dlt * lax.rsqrt(hmean(dlt * dlt) + GN_EPS)
    yb = (yn * rlng_ref[...] + rlnb_ref[...] + rbonus_ref[0].astype(F32)) * rg_ref[0].astype(F32)
    ot = tof_ref[0] + tob_ref[0]
    yc = ot * lax.rsqrt(hmean(ot * ot) + EPS) * tln_ref[...] * _silu(tg_ref[0])
    og = gof_ref[0] + gob_ref[0]
    yd = og * lax.rsqrt(hmean(og * og) + EPS) * gln_ref[...] * _silu(gg_ref[0])
    mix = (_mm(ya, wo_ref[0]) + _mm(yb, wo_ref[1]) + _mm(yc, wo_ref[2]) + _mm(yd, wo_ref[3]))
    o_ref[0] = x_ref[0] + g1_ref[0] * mix


def _mix(x, g1, y5, u5, rw_pre, rw_y, zret, ret_o, zgla, gla_o, p, tm):
    b, l, d = x.shape
    tok = lambda w, j: pl.BlockSpec((1, tm, w), lambda i, t: (i, t, j))
    vec = lambda a: pl.BlockSpec(a.shape, lambda i, t: (0,) * a.ndim)
    t256 = tok(GROUP_W, 0)
    args = [x, g1, y5, u5, p['s5_d'], p['glu_w'], p['glu_b'],
            rw_y[0], rw_y[1], rw_pre[4], rw_pre[3], p['rw_ln_g'], p['rw_ln_b'],
            ret_o[0], ret_o[1], zret, p['ret_ln_g'],
            gla_o[0], gla_o[1], zgla, p['gla_ln_g'],
            p['bones'], p['w_out']]
    grouped = pl.BlockSpec((1, S5_G, tm // S5_T, S5_TP), lambda i, t: (i, 0, t, 0))
    specs = [tok(d, 0), pl.BlockSpec((1, 1, d), lambda i, t: (i, 0, 0)), grouped, t256,
             vec(p['s5_d']), vec(p['glu_w']), vec(p['glu_b']),
             t256, t256, t256, t256, vec(p['rw_ln_g']), vec(p['rw_ln_b']),
             t256, t256, tok(GROUP_W, 3), vec(p['ret_ln_g']),
             t256, t256, tok(GROUP_W, 2), vec(p['gla_ln_g']),
             vec(p['bones']), vec(p['w_out'])]
    return pl.pallas_call(
        _mix_kernel,
        grid=(b, l // tm),
        in_specs=specs,
        out_specs=tok(d, 0),
        out_shape=jax.ShapeDtypeStruct((b, l, d), F32),
        scratch_shapes=[pltpu.VMEM((GROUP_W // LANES, tm, LANES), F32)],
        compiler_params=_cparams("parallel", "parallel"),
        name="mix_outproj",
    )(*args)


def _mlp_kernel(final, nff, x_ref, g_ref, sc_ref, sh_ref, gate_ref, w1_ref, w2_ref, fg_ref, o_ref):
    x = x_ref[0]
    hb = _modnorm(x, g_ref[...], sc_ref[0], sh_ref[0]).astype(BF16)
    ff = w1_ref.shape[2] // nff
    acc = None
    for j in range(nff):
        a = jnp.maximum(jnp.dot(hb, w1_ref[0, :, j * ff:(j + 1) * ff], preferred_element_type=F32), 0.0)
        part = jnp.dot((a * a).astype(BF16), w2_ref[0, j * ff:(j + 1) * ff, :], preferred_element_type=F32)
        acc = part if acc is None else acc + part
    y = x + gate_ref[0] * acc
    if final:
        ms = jnp.mean(y * y, axis=-1, keepdims=True)
        y = y * lax.rsqrt(ms + EPS) * fg_ref[...]
    o_ref[0] = y


def _mlp(x, g, sc, sh, gate, w1, w2, layer, final_g, final, tm):
    b, l, d = x.shape
    tok = pl.BlockSpec((1, tm, d), lambda i, t: (i, t, 0))
    vec = pl.BlockSpec((1, 1, d), lambda i, t: (i, 0, 0))
    row = pl.BlockSpec((1, d), lambda i, t: (0, 0))
    once = lambda a: pl.BlockSpec((1,) + a.shape[1:], lambda i, t: (layer, 0, 0), pipeline_mode=pl.Buffered(1))
    return pl.pallas_call(
        functools.partial(_mlp_kernel, final, 4),
        grid=(b, l // tm),
        in_specs=[tok, row, vec, vec, vec, once(w1), once(w2), row],
        out_specs=tok,
        out_shape=jax.ShapeDtypeStruct((b, l, d), F32),
        compiler_params=_cparams("parallel", "parallel"),
        name="mlp",
    )(x, g, sc, sh, gate, w1, w2, final_g)


def kernel(x, c, ctx, c_ctx, ada_w, ada_b, norm1_g, norm2_g, w_in, w_out, s5_lam_re, s5_lam_im, s5_log_dt, s5_b_re, s5_b_im, s5_c_re, s5_c_im, s5_d, s5_glu_w, s5_glu_b, rw_mu, rw_w0, rw_w_up, rw_a0, rw_a_up, rw_g_up, rw_k_k, rw_k_a, rw_r_k, rw_ln_g, rw_ln_b, ret_decay_logit, ret_ln_g, gla_a_up, gla_a_b, gla_ln_g, mlp_w1, mlp_w2, final_g):
    b, l, d = x.shape
    lc = ctx.shape[1]
    depth = ada_w.shape[0]
    rows = l // GRID_W
    assert l % TOKEN_TILE == 0 and lc % RET_T == 0 and lc <= TOKEN_TILE and d % LANES == 0
    assert b < COND_ROWS and l % GRID_W == 0

    cond = jnp.zeros((COND_ROWS, d), F32).at[:b].set(c).at[b].set(c_ctx)
    mod = _modulation(cond, ada_w, ada_b)

    lane = jnp.arange(GROUP_W)
    bones = (lane[:, None] // HEAD_V == lane[None, :] // HEAD_V).astype(BF16)
    rope_lat = _rope_tables(rows)
    ident = (jnp.ones((lc // CHUNK, 1, GROUP_W), F32), jnp.zeros((lc // CHUNK, 1, GROUP_W), F32),
             jnp.ones((GRID_W, GROUP_W), F32), jnp.zeros((GRID_W, GROUP_W), F32))
    nlev = max(1, (l // S5_T - 1).bit_length())
    c0, c1, c2 = GROUP_W, GROUP_W + RW_COLS, GROUP_W + RW_COLS + 4 * GROUP_W
    row2 = lambda a: a.reshape(1, -1).astype(F32)
    w_in_b, w_out_b, w1_b, w2_b = _to_bf16(w_in), _to_bf16(w_out), _to_bf16(mlp_w1), _to_bf16(mlp_w2)

    s5_tab_all = jax.vmap(lambda *a: _s5_tables(*a, nlev))(
        s5_lam_re, s5_lam_im, s5_log_dt, s5_b_re, s5_b_im, s5_c_re, s5_c_im)
    ret_tab_all = jax.vmap(lambda a: _ret_tables(a, RET_T))(ret_decay_logit)
    zpad = lambda a, lo, n: jnp.zeros((depth, 2, LANES, n), F32).at[:, :, lo:lo + a.shape[2]].set(a).astype(BF16)
    rw_wup_all = zpad(rw_w_up, 0, GROUP_W)
    rw_aup_all = zpad(rw_a_up, RW_W_RANK, GROUP_W)
    rw_gup_all = jnp.zeros((depth, LANES, GROUP_W), F32).at[:, RW_W_RANK + RW_A_RANK:].set(rw_g_up).astype(BF16)
    gla_aup_all = zpad(gla_a_up, 0, GLA_QK)

    xc = ctx
    for i in range(depth):
        last = i == depth - 1
        m = mod[i].reshape(COND_ROWS, N_ADA, d)
        ml = m[:b, :, None, :]
        mc = jnp.broadcast_to(m[b][None, :, None, :], (b, N_ADA, 1, d))
        wi = w_in_b[i]
        w5, wr, wt = wi[:, :c0], wi[:, c0:c1], wi[:, c1:c2]
        wg = jnp.pad(wi[:, c2:], ((0, 0), (0, GLA_COLS_PAD - (wi.shape[1] - c2))))
        n1 = row2(norm1_g[i])
        n2 = row2(norm2_g[i])

        s5_tab = tuple(a[i] for a in s5_tab_all)
        rwp = dict(
            mu=row2(rw_mu[i]),
            w0=rw_w0[i].reshape(2, 1, GROUP_W), a0=rw_a0[i].reshape(2, 1, GROUP_W),
            wup=rw_wup_all[i], aup=rw_aup_all[i], gup=rw_gup_all[i],
            kk=row2(rw_k_k[i]), ka=row2(rw_k_a[i]), rk=row2(rw_r_k[i]), bones=bones)
        ret_tab = tuple(a[i] for a in ret_tab_all)
        gla_aup = gla_aup_all[i]
        gla_ab = gla_a_b[i].reshape(2, 1, GLA_QK).astype(F32)
        mixp = dict(s5_d=row2(s5_d[i]), glu_w=s5_glu_w[i].astype(BF16), glu_b=row2(s5_glu_b[i]),
                    rw_ln_g=row2(rw_ln_g[i]), rw_ln_b=row2(rw_ln_b[i]), ret_ln_g=row2(ret_ln_g[i]),
                    gla_ln_g=row2(gla_ln_g[i]), bones=bones,
                    w_out=w_out_b[i].reshape(4, GROUP_W, d))

        def mixers(xx, mm, is_lat, states):
            tm = TOKEN_TILE if is_lat else lc
            z5, z5g, zt, zg, pre = _inproj(xx, n1, mm[:, 1], mm[:, 0], w5, wr, wt, wg,
                                           rope_lat if is_lat else ident, rwp, is_lat, tm)
            y5, h5 = _s5_scan(z5g, s5_tab, states[0])
            yrf, yrb, srw = _rw_scan(pre, states[1])
            otf, otb, sret = _ret_scan(zt, ret_tab, states[2])
            ogf, ogb, sgla = _gla_scan(zg, gla_aup, gla_ab, states[3])
            outs = (z5, y5, pre, (yrf, yrb), zt, (otf, otb), zg, (ogf, ogb))
            return outs, (h5, srw, sret, sgla)

        def block(xx, mm, outs, is_lat, fin):
            tm = TOKEN_TILE if is_lat else lc
            z5, y5, pre, yr, zt, ot, zg, og = outs
            x1 = _mix(xx, mm[:, 2], y5, z5, pre, yr, zt, ot, zg, og, mixp, tm)
            return _mlp(x1, n2, mm[:, 4], mm[:, 3], mm[:, 5], w1_b, w2_b, i, row2(final_g), fin, tm)

        zeros = (jnp.zeros((b, 2, S5_G, 1, 2 * S5_N), F32),
                 jnp.zeros((b, 2, HEADS // 2, HEAD_V, LANES), F32),
                 jnp.zeros((b, 2, HEADS // 2, HEAD_V, LANES), F32),
                 jnp.zeros((b, 2, HEAD_V, GLA_QK), F32))
        outs_c, st_c = mixers(xc, mc, False, zeros)
        outs_l, _ = mixers(x, ml, True, st_c)
        x = block(x, ml, outs_l, True, last)
        if not last:
            xc = block(xc, mc, outs_c, False, False)
    return x
```

```python
import functools
import math

import jax
import jax.numpy as jnp
from jax import lax
from jax.experimental import pallas as pl
from jax.experimental.pallas import tpu as pltpu

F32 = jnp.float32
BF16 = jnp.bfloat16

LANES = 128
GRID_W = 64
GROUP_W = 256
N_ADA = 6
EPS = 1e-6
GN_EPS = 64e-5
TOKEN_TILE = 512
CHUNK = 64
RW_CPB = 4
RET_T = 256
RET_BLOCK = 512
GLA_T = 128
GLA_BLOCK = 512
HEADS = 4
HEAD_V = 64
S5_P = 16
S5_G = 16
S5_N = 64
S5_T = 32
S5_TP = S5_T * S5_P
RW_COLS = 896
RW_W_RANK = 32
RW_A_RANK = 32
RW_G_RANK = 64
RET_DK = 64
GLA_DK = 32
GLA_QK = 128
GLA_RANK = 16
GLA_TAU = 16.0
GLA_COLS_PAD = 896
ROPE_BASE = 10000.0
ROPE_HALF = RET_DK // 2
ROPE_NF = RET_DK // 4
COND_ROWS = 8
CAST_ROWS = 256
ADA_TILE = 1536
VMEM_LIMIT = 56 * 1024 * 1024


def _cparams(*sem):
    return pltpu.CompilerParams(dimension_semantics=sem, vmem_limit_bytes=VMEM_LIMIT)


def _mm(a, b):
    return jnp.dot(a.astype(BF16), b.astype(BF16), preferred_element_type=F32)


def _mm_nt(a, b):
    return lax.dot_general(a.astype(BF16), b.astype(BF16), (((1,), (1,)), ((), ())),
                           preferred_element_type=F32)


def _mm_tn(a, b):
    return lax.dot_general(a.astype(BF16), b.astype(BF16), (((0,), (0,)), ((), ())),
                           preferred_element_type=F32)


def _split2(x):
    hi = x.astype(BF16)
    return hi, (x - hi.astype(F32)).astype(BF16)


def _chunk_tri01(n, t, reverse):
    ri = lax.broadcasted_iota(jnp.int32, (n, n), 0)
    ci = lax.broadcasted_iota(jnp.int32, (n, n), 1)
    shift = t.bit_length() - 1
    tri = (ci >= ri) if reverse else (ci <= ri)
    return jnp.where((ri >> shift) == (ci >> shift), jnp.where(tri, 1.0, 0.0), 0.0).astype(BF16)


def _mm_left01(m01, x):
    hi, lo = _split2(x)
    return jnp.dot(m01, hi, preferred_element_type=F32) + jnp.dot(m01, lo, preferred_element_type=F32)


def _mm_right01(x, m01):
    return jnp.dot(x.astype(BF16), m01, preferred_element_type=F32)


def _sigmoid(x):
    return 1.0 / (1.0 + jnp.exp(-x))


def _softplus(x):
    return jnp.maximum(x, 0.0) + jnp.log(1.0 + jnp.exp(-jnp.abs(x)))


def _silu(x):
    return x * _sigmoid(x)


def _gelu_tanh(x):
    return 0.5 * x * (1.0 + jnp.tanh(math.sqrt(2.0 / math.pi) * (x + 0.044715 * x * x * x)))


def _tri_incl(t, reverse):
    ri = lax.broadcasted_iota(jnp.int32, (t, t), 0)
    ci = lax.broadcasted_iota(jnp.int32, (t, t), 1)
    return (ci >= ri) if reverse else (ci <= ri)


def _cast_kernel(x_ref, o_ref):
    o_ref[...] = x_ref[...].astype(o_ref.dtype)


def _to_bf16(w):
    depth, r, c = w.shape
    tr = CAST_ROWS
    spec = pl.BlockSpec((1, tr, c), lambda i, j: (i, j, 0))
    return pl.pallas_call(
        _cast_kernel,
        grid=(depth, r // tr),
        in_specs=[spec],
        out_specs=spec,
        out_shape=jax.ShapeDtypeStruct(w.shape, BF16),
        compiler_params=_cparams("parallel", "parallel"),
        name="cast_bf16",
    )(w)


def _mod_kernel(cond_ref, w_ref, b_ref, o_ref):
    c = cond_ref[...]
    o_ref[0] = _mm(_silu(c), w_ref[0]) + b_ref[0]


def _modulation(cond, ada_w, ada_b):
    depth, d, n = ada_w.shape
    tn = ADA_TILE
    return pl.pallas_call(
        _mod_kernel,
        grid=(depth, n // tn),
        in_specs=[pl.BlockSpec((COND_ROWS, d), lambda i, j: (0, 0)),
                  pl.BlockSpec((1, d, tn), lambda i, j: (i, 0, j)),
                  pl.BlockSpec((1, 1, tn), lambda i, j: (i, 0, j))],
        out_specs=pl.BlockSpec((1, COND_ROWS, tn), lambda i, j: (i, 0, j)),
        out_shape=jax.ShapeDtypeStruct((depth, COND_ROWS, n), F32),
        compiler_params=_cparams("parallel", "parallel"),
        name="adaln_mod",
    )(cond, ada_w, ada_b.reshape(depth, 1, n))


def _modnorm(x, g, sc, sh):
    ms = jnp.mean(x * x, axis=-1, keepdims=True)
    return x * lax.rsqrt(ms + EPS) * g * (1.0 + sc) + sh


def _shift_columns(z, segments):
    tm, width = z.shape
    lane = lax.broadcasted_iota(jnp.int32, (tm, LANES), 1)
    starts = [s for s, _ in segments] + [width]
    cols = []
    for c in range(width // LANES):
        lo, hi = c * LANES, (c + 1) * LANES
        inside = [(max(starts[i], lo), segments[i][1]) for i in range(len(segments))
                  if starts[i] < hi and starts[i + 1] > lo]
        col = inside[-1][1](c)
        for first, fn in reversed(inside[:-1]):
            nxt = [f for f, _ in inside if f > first][0]
            col = jnp.where(lane < nxt - lo, fn(c), col)
        cols.append(col)
    return jnp.concatenate(cols, axis=1)


def _rw_prep_math(grid_shift, z, zp, zn, mu, w0_ref, wup_ref, a0_ref, aup_ref, gup, kkp, ka, rk, bones):
    tm = z.shape[0]
    row = lax.broadcasted_iota(jnp.int32, (tm, LANES), 0)
    col = lambda x, c: x[:, c * LANES:(c + 1) * LANES]
    if grid_shift:
        pos = row & (GRID_W - 1)
        left = lambda c: jnp.where(pos == 0, 0.0, pltpu.roll(col(z, c), 1, 0))
        right = lambda c: jnp.where(pos == GRID_W - 1, 0.0, pltpu.roll(col(z, c), tm - 1, 0))
        up = lambda c: jnp.concatenate([col(zp, c), col(z, c)[:tm - GRID_W]], axis=0)
        down = lambda c: jnp.concatenate([col(z, c)[GRID_W:], col(zn, c)], axis=0)
        q = RW_COLS // 4
        shifted = _shift_columns(z, [(0, left), (q, right), (2 * q, up), (3 * q, down)])
    else:
        prev = lambda c: jnp.where(row == 0, 0.0, pltpu.roll(col(z, c), 1, 0))
        nxt = lambda c: jnp.where(row == tm - 1, 0.0, pltpu.roll(col(z, c), tm - 1, 0))
        shifted = _shift_columns(z, [(0, prev), (RW_COLS // 2, nxt)])
    zm = z + mu * (shifted - z)
    r = zm[:, 0:GROUP_W]
    k = zm[:, GROUP_W:2 * GROUP_W]
    v = zm[:, 2 * GROUP_W:3 * GROUP_W]
    lo = zm[:, 3 * GROUP_W:RW_COLS]
    g = _mm(_sigmoid(lo), gup)
    kk = k * kkp
    kk = kk * lax.rsqrt(_mm_right01(kk * kk, bones) + 1e-12)
    bonus = _mm_right01(r * k * rk, bones) * v
    outs = [r.astype(BF16), v.astype(BF16), kk.astype(BF16), g.astype(BF16), bonus.astype(BF16)]
    tlo = jnp.tanh(lo)
    for d in range(2):
        log_decay = -math.exp(-0.5) * _sigmoid(w0_ref[d] + _mm(tlo, wup_ref[d]))
        a = _sigmoid(a0_ref[d] + _mm(lo, aup_ref[d]))
        outs += [log_decay,
                 (k * (1.0 + (a - 1.0) * ka)).astype(BF16), a.astype(BF16)]
    return outs


def _inproj_kernel(grid_shift, nt, x_ref, xp_ref, xn_ref, g_ref, sc_ref, sh_ref,
                   w5_ref, wr_ref, wt_ref, wg_ref, cr_ref, sr_ref, cc_ref, sn_ref,
                   mu_ref, w0_ref, wup_ref, a0_ref, aup_ref, gup_ref, kk_ref, ka_ref, rk_ref, bones_ref,
                   o5_ref, o5g_ref, ot_ref, og_ref, *rest):
    rw_refs, z5h_ref = rest[:-1], rest[-1]
    j = pl.program_id(1)
    norm = lambda x: _modnorm(x, g_ref[...], sc_ref[0], sh_ref[0]).astype(BF16)
    hb = norm(x_ref[0])
    z5 = jnp.dot(hb, w5_ref[...], preferred_element_type=F32)
    o5_ref[0] = z5
    og_ref[0] = jnp.dot(hb, wg_ref[...], preferred_element_type=F32)
    zr = jnp.dot(hb, wr_ref[...], preferred_element_type=F32)
    zp = zn = None
    if grid_shift:
        zp = jnp.where(j > 0, jnp.dot(norm(xp_ref[0]), wr_ref[...], preferred_element_type=F32), 0.0)
        zn = jnp.where(j < nt - 1, jnp.dot(norm(xn_ref[0]), wr_ref[...], preferred_element_type=F32), 0.0)
    rw = _rw_prep_math(grid_shift, zr, zp, zn, mu_ref[...], w0_ref, wup_ref, a0_ref, aup_ref, gup_ref[...],
                       kk_ref[...], ka_ref[...], rk_ref[...], bones_ref[...])
    for ref, val in zip(rw_refs, rw):
        ref[0] = val
    zt = jnp.dot(hb, wt_ref[...], preferred_element_type=F32)
    lane = lax.broadcasted_iota(jnp.int32, (GRID_W, GROUP_W), 1)
    by_row = (lane & ROPE_HALF) == 0
    nrow = cr_ref.shape[0]
    cos = jnp.concatenate([jnp.where(by_row, cr_ref[j], cc_ref[...]) for j in range(nrow)], axis=0)
    sin = jnp.concatenate([jnp.where(by_row, sr_ref[j], sn_ref[...]) for j in range(nrow)], axis=0)
    ot_ref[0, :, 0:GROUP_W] = _rope(zt[:, 0:GROUP_W], cos, sin)
    ot_ref[0, :, GROUP_W:2 * GROUP_W] = _rope(zt[:, GROUP_W:2 * GROUP_W] * RET_DK ** -0.5, cos, sin)
    ot_ref[0, :, 2 * GROUP_W:] = zt[:, 2 * GROUP_W:]
    nch = o5g_ref.shape[2]
    gph = LANES // S5_P
    pitch = S5_T + 1
    for hf in range(GROUP_W // LANES):
        for ch in range(nch):
            z5h_ref[hf, ch * pitch:ch * pitch + S5_T] = z5[ch * S5_T:(ch + 1) * S5_T, hf * LANES:(hf + 1) * LANES]
    for s in range(S5_T):
        for hf in range(GROUP_W // LANES):
            rows = z5h_ref[hf, pl.ds(s, nch, stride=pitch), :]
            for g in range(gph):
                o5g_ref[0, hf * gph + g, :, s * S5_P:(s + 1) * S5_P] = rows[:, g * S5_P:(g + 1) * S5_P]


def _inproj(x, g, sc, sh, w5, wr, wt, wg, rope, rwp, grid_shift, tm):
    b, l, d = x.shape
    nt = l // tm
    hb = tm // GRID_W
    nhb = l // GRID_W
    tok = lambda n: pl.BlockSpec((1, tm, n), lambda i, j: (i, j, 0))
    vec = pl.BlockSpec((1, 1, d), lambda i, j: (i, 0, 0))
    full = lambda a: pl.BlockSpec(a.shape, lambda i, j: (0,) * a.ndim)
    grouped = pl.BlockSpec((1, S5_G, tm // S5_T, S5_TP), lambda i, j: (i, 0, j, 0))
    rowt = pl.BlockSpec((hb, 1, GROUP_W), lambda i, j: (j, 0, 0))
    above = pl.BlockSpec((1, GRID_W, d), lambda i, j: (i, jnp.maximum(j * hb - 1, 0), 0))
    below = pl.BlockSpec((1, GRID_W, d), lambda i, j: (i, jnp.minimum((j + 1) * hb, nhb - 1), 0))
    cr, sr, cc, sn = rope
    params = (rwp['mu'], rwp['w0'], rwp['wup'], rwp['a0'], rwp['aup'], rwp['gup'], rwp['kk'], rwp['ka'],
              rwp['rk'], rwp['bones'])
    rw_dtypes = (BF16,) * 5 + (F32, BF16, BF16) * 2
    outs = pl.pallas_call(
        functools.partial(_inproj_kernel, grid_shift, nt),
        grid=(b, nt),
        in_specs=[tok(d), above, below, pl.BlockSpec((1, d), lambda i, j: (0, 0)), vec, vec,
                  full(w5), full(wr), full(wt), full(wg), rowt, rowt, full(cc), full(sn)]
                 + [full(a) for a in params],
        out_specs=[tok(GROUP_W), grouped, tok(wt.shape[1]), tok(wg.shape[1])] + [tok(GROUP_W)] * 11,
        out_shape=[jax.ShapeDtypeStruct((b, l, GROUP_W), F32),
                   jax.ShapeDtypeStruct((b, S5_G, l // S5_T, S5_TP), F32),
                   jax.ShapeDtypeStruct((b, l, wt.shape[1]), F32),
                   jax.ShapeDtypeStruct((b, l, wg.shape[1]), F32)]
                  + [jax.ShapeDtypeStruct((b, l, GROUP_W), dt) for dt in rw_dtypes],
        scratch_shapes=[pltpu.VMEM((GROUP_W // LANES, tm // S5_T * (S5_T + 1), LANES), F32)],
        compiler_params=_cparams("parallel", "parallel"),
        name="norm_inproj",
    )(x, x, x, g, sc, sh, w5, wr, wt, wg, cr, sr, cc, sn, *params)
    return outs[0], outs[1], outs[2], outs[3], tuple(outs[4:])


def _s5_tables(lam_re, lam_im, log_dt, b_re, b_im, c_re, c_im, nlev):
    hp = lax.Precision.HIGHEST
    t = S5_T
    lam = lax.complex(jnp.minimum(lam_re.astype(F32), -1e-4), lam_im.astype(F32))
    ldt = lam * jnp.exp(log_dt.astype(F32))[..., None]
    a_bar = jnp.exp(ldt)
    bb = ((a_bar - 1.0) / lam)[..., None] * lax.complex(b_re.astype(F32), b_im.astype(F32))
    cm = lax.complex(c_re.astype(F32), c_im.astype(F32))
    tau = jnp.arange(t + 1, dtype=F32)
    apow = jnp.exp(ldt[:, :, None, :] * tau[None, None, :, None])
    taps = jnp.einsum('dgpn,dgtn,dgnq->dgtpq', cm, apow[:, :, :t], bb, precision=hp).real
    taprow = jnp.stack([taps[0], taps[1][:, ::-1]]).transpose(0, 1, 4, 2, 3).reshape(2, S5_G, S5_P, S5_TP)

    def pack(zc):
        return jnp.concatenate([zc.real, zc.imag], axis=-1)

    win_f = apow[0][:, t - 1 - jnp.arange(t), None, :] * bb[0].transpose(0, 2, 1)[:, None]
    win_b = apow[1][:, jnp.arange(t), None, :] * bb[1].transpose(0, 2, 1)[:, None]
    win = jnp.stack([pack(win_f), pack(win_b)]).reshape(2, S5_G, S5_TP, 2 * S5_N)
    ca_f = cm[0][:, None] * apow[0][:, 1 + jnp.arange(t), None, :]
    ca_b = cm[1][:, None] * apow[1][:, t - jnp.arange(t), None, :]

    def outpack(ca):
        w = jnp.concatenate([ca.real, -ca.imag], axis=-1)
        return w.reshape(S5_G, S5_TP, 2 * S5_N).transpose(0, 2, 1)

    wout = jnp.stack([outpack(ca_f), outpack(ca_b)])
    lev = (2.0 ** jnp.arange(nlev, dtype=F32)) * t
    pw = jnp.exp(ldt[:, :, None, :] * lev[None, None, :, None])
    p1 = jnp.concatenate([pw.real, pw.real], axis=-1)
    p2 = jnp.concatenate([-pw.imag, pw.imag], axis=-1)
    pw = jnp.stack([p1, p2], axis=3)
    return taprow, win.astype(BF16), wout.astype(BF16), pw


def _s5_kernel(nc, nlev, u_ref, tap_ref, win_ref, wout_ref, pw_ref, h0_ref, y_ref, hfin_ref, conv_ref):
    u = u_ref[0, 0].astype(BF16)
    row = lax.broadcasted_iota(jnp.int32, (nc, 2 * S5_N), 0)
    lane = lax.broadcasted_iota(jnp.int32, (S5_P, S5_TP), 1)
    for s in range(S5_T):
        lo = s * S5_P
        fwd = tap_ref[0, 0] if s == 0 else jnp.where(lane >= lo, pltpu.roll(tap_ref[0, 0], lo, 1), 0.0)
        hi = lo + S5_P
        bwd = tap_ref[1, 0] if hi == S5_TP else jnp.where(lane < hi, pltpu.roll(tap_ref[1, 0], hi, 1), 0.0)
        conv_ref[0, lo:hi, :] = fwd.astype(BF16)
        conv_ref[1, lo:hi, :] = bwd.astype(BF16)

    def cmul(x, d, j):
        return pw_ref[d, 0, j, 0:1] * x + pw_ref[d, 0, j, 1:2] * pltpu.roll(x, S5_N, 1)

    v = [jnp.dot(u, win_ref[d, 0], preferred_element_type=F32) for d in range(2)]
    x = [jnp.where(row == 0, h0_ref[0, 0, 0], pltpu.roll(v[0], 1, 0)),
         jnp.where(row == nc - 1, h0_ref[0, 1, 0], pltpu.roll(v[1], nc - 1, 0))]
    for j in range(nlev):
        sh = 2 ** j
        xs = [jnp.where(row >= sh, pltpu.roll(x[0], sh, 0), 0.0),
              jnp.where(row < nc - sh, pltpu.roll(x[1], nc - sh, 0), 0.0)]
        x = [x[d] + cmul(xs[d], d, j) for d in range(2)]
    for d, last in enumerate((nc - 1, 0)):
        hfin_ref[0, d, 0] = cmul(x[d][last:last + 1], d, 0) + v[d][last:last + 1]
    y_ref[0, 0] = (jnp.dot(u, conv_ref[0], preferred_element_type=F32) + _mm(x[0], wout_ref[0, 0])
                   + jnp.dot(u, conv_ref[1], preferred_element_type=F32) + _mm(x[1], wout_ref[1, 0]))


def _s5_scan(uf, tables, h0):
    conv, win, wout, pw = tables
    b, _, nc, _ = uf.shape
    nlev = max(1, (nc - 1).bit_length())
    pw = pw[:, :, :nlev]
    n2 = 2 * S5_N
    y, hfin = pl.pallas_call(
        functools.partial(_s5_kernel, nc, nlev),
        grid=(b, S5_G),
        in_specs=[pl.BlockSpec((1, 1, nc, S5_TP), lambda i, g: (i, g, 0, 0)),
                  pl.BlockSpec((2, 1, S5_P, S5_TP), lambda i, g: (0, g, 0, 0)),
                  pl.BlockSpec((2, 1, S5_TP, n2), lambda i, g: (0, g, 0, 0)),
                  pl.BlockSpec((2, 1, n2, S5_TP), lambda i, g: (0, g, 0, 0)),
                  pl.BlockSpec((2, 1, nlev, 2, n2), lambda i, g: (0, g, 0, 0, 0)),
                  pl.BlockSpec((1, 2, 1, 1, n2), lambda i, g: (i, 0, g, 0, 0))],
        out_specs=[pl.BlockSpec((1, 1, nc, S5_TP), lambda i, g: (i, g, 0, 0)),
                   pl.BlockSpec((1, 2, 1, 1, n2), lambda i, g: (i, 0, g, 0, 0))],
        out_shape=[jax.ShapeDtypeStruct((b, S5_G, nc, S5_TP), F32),
                   jax.ShapeDtypeStruct((b, 2, S5_G, 1, n2), F32)],
        scratch_shapes=[pltpu.VMEM((2, S5_TP, S5_TP), BF16)],
        compiler_params=_cparams("parallel", "parallel"),
        name="s5_scan",
    )(uf, conv, win, wout, pw, h0)
    return y, hfin


def _rw_scan_kernel(nb, cpb, rf_ref, vf_ref, kkf_ref, lwf_ref, kdf_ref, asf_ref,
                    rb_ref, vb_ref, kkb_ref, lwb_ref, kdb_ref, asb_ref, s0_ref,
                    yf_ref, yb_ref, sfin_ref, st_ref):
    c = pl.program_id(1)

    @pl.when(c == 0)
    def _():
        st_ref[...] = s0_ref[0]

    t = CHUNK
    ri = lax.broadcasted_iota(jnp.int32, (t, LANES), 0)
    li = lax.broadcasted_iota(jnp.int32, (t, LANES), 1)
    ci = li & (t - 1)
    low = li < HEAD_V
    eyef = jnp.where(ri == ci, 1.0, 0.0)
    same = lambda s: jnp.where((ri >> s) == (ci >> s), 1.0, 0.0)
    m4, m8, m16, m32 = same(2), same(3), same(4), same(5)
    merge_masks = (m8 - m4, m16 - m8, m32 - m16, 1.0 - m32)

    def bd(x):
        return jnp.concatenate([jnp.where(low, x, 0.0), jnp.where(low, 0.0, x)], axis=0)

    def diag_blocks(full):
        return jnp.where(low, full[0:HEAD_V], full[HEAD_V:2 * HEAD_V])

    dir_refs = ((rf_ref, vf_ref, kkf_ref, lwf_ref, kdf_ref, asf_ref),
                (rb_ref, vb_ref, kkb_ref, lwb_ref, kdb_ref, asb_ref))
    units = []
    for d, (r_ref, v_ref, kk_ref, lw_ref, kd_ref, as_ref) in enumerate(dir_refs):
        incl = (ci >= ri) if d == 1 else (ci <= ri)
        strict = (ci > ri) if d == 1 else (ci < ri)
        r, v, kk, kd = (x[0].astype(F32) for x in (r_ref, v_ref, kk_ref, kd_ref))
        lw = lw_ref[0]
        cin = _mm_left01(_chunk_tri01(cpb * t, t, d == 1), lw)
        e_in = jnp.exp(cin)
        e_neg = jnp.exp(-cin)
        rt = r * e_in
        at = -kk * jnp.exp(cin - lw)
        bvec = kk * as_ref[0].astype(F32)
        bt = bvec * e_neg
        kt = kd * e_neg
        for j in range(cpb):
            rows = slice(j * t, (j + 1) * t)
            last = j * t + (0 if d == 1 else t - 1)
            clast = cin[last:last + 1]
            dl = jnp.exp(clast - cin[rows])
            bh = bvec[rows] * dl
            kh = kd[rows] * dl
            dec = jnp.exp(clast)
            for p in range(HEADS // 2):
                sl = slice(p * LANES, (p + 1) * LANES)
                units.append(dict(d=d, j=j, p=p, incl=incl, strict=strict,
                                  at=at[rows, sl], rt=rt[rows, sl], bt=bt[rows, sl], kt=kt[rows, sl],
                                  bh=bh[:, sl], kh=kh[:, sl], v=v[rows, sl], dec=dec[:, sl]))

    x1 = [jnp.concatenate([u['at'], u['rt']], axis=0) for u in units]
    scores = [_mm_nt(x, jnp.concatenate([bd(u['bt']), bd(u['kt'])], axis=0))
              for x, u in zip(x1, units)]
    nmat = [jnp.where(u['strict'], a[0:t, :LANES], 0.0) for u, a in zip(units, scores)]
    a_rb = [jnp.where(u['incl'], a[t:2 * t, :LANES], 0.0) for u, a in zip(units, scores)]
    a_kk = [jnp.concatenate([jnp.where(u['strict'], a[0:t, LANES:], 0.0),
                             jnp.where(u['incl'], a[t:2 * t, LANES:], 0.0)], axis=0)
            for u, a in zip(units, scores)]
    akv = [_mm(a, bd(u['v'])) for u, a in zip(units, a_kk)]
    kv = [diag_blocks(_mm_tn(u['v'], u['kh'])) for u in units]
    nd = [x * m4 for x in nmat]
    n2 = [_mm(x, bd(x)) for x in nd]
    tinv = [eyef + x + _mm(eyef + x, bd(y)) for x, y in zip(nd, n2)]
    for mk in merge_masks:
        w = [_mm(ti, bd(x * mk)) for ti, x in zip(tinv, nmat)]
        tinv = [ti + _mm(wi, bd(ti)) for ti, wi in zip(tinv, w)]
    zz = [_mm(ti, jnp.concatenate([bd(u['at']), bd(kvv[0:t])], axis=1))
          for ti, u, kvv in zip(tinv, units, akv)]
    ght = [_mm_tn(z, u['bh']) for u, z in zip(units, zz)]
    qy = [_mm(a, jnp.concatenate([bd(z[:, :LANES]), bd(z[:, LANES:])], axis=1))
          for a, z in zip(a_rb, zz)]
    qmat = [u['rt'] + x[:, :LANES] for u, x in zip(units, qy)]
    y0 = [x[:, LANES:] + kvv[t:2 * t] for x, kvv in zip(qy, akv)]
    gmat = [diag_blocks(x[:LANES]) for x in ght]
    hmat = [diag_blocks(x[LANES:]) + k2 for x, k2 in zip(ght, kv)]
    idx = {(u['d'], u['j'], u['p']): n for n, u in enumerate(units)}
    npair = HEADS // 2
    state = {(d, p): st_ref[d, p] for d in range(2) for p in range(npair)}
    ys = {}
    for step in range(cpb):
        for d in range(2):
            j = step if d == 0 else cpb - 1 - step
            for p in range(npair):
                n = idx[(d, j, p)]
                st = state[(d, p)]
                ys[(d, j, p)] = _mm_nt(qmat[n], bd(st)) + y0[n]
                state[(d, p)] = units[n]['dec'] * st + _mm(st, bd(gmat[n])) + hmat[n]
    for d, y_ref in enumerate((yf_ref, yb_ref)):
        y_ref[0] = jnp.concatenate(
            [jnp.concatenate([ys[(d, j, p)] for p in range(npair)], axis=1) for j in range(cpb)], axis=0)
        for p in range(npair):
            st_ref[d, p] = state[(d, p)]

    @pl.when(c == nb - 1)
    def _():
        sfin_ref[0] = st_ref[...]


def _rw_scan(pre, s0):
    r, v, kk, _, _, lw0, kd0, as0, lw1, kd1, as1 = pre
    b, l, _ = r.shape
    cpb = min(RW_CPB, l // CHUNK)
    nb = l // (cpb * CHUNK)
    fw = pl.BlockSpec((1, cpb * CHUNK, GROUP_W), lambda i, c: (i, c, 0))
    bw = pl.BlockSpec((1, cpb * CHUNK, GROUP_W), lambda i, c: (i, nb - 1 - c, 0))
    st = pl.BlockSpec((1, 2, HEADS // 2, HEAD_V, LANES), lambda i, c: (i, 0, 0, 0, 0))
    return pl.pallas_call(
        functools.partial(_rw_scan_kernel, nb, cpb),
        grid=(b, nb),
        in_specs=[fw] * 6 + [bw] * 6 + [st],
        out_specs=[fw, bw, st],
        out_shape=[jax.ShapeDtypeStruct((b, l, GROUP_W), F32)] * 2
                  + [jax.ShapeDtypeStruct((b, 2, HEADS // 2, HEAD_V, LANES), F32)],
        scratch_shapes=[pltpu.VMEM((2, HEADS // 2, HEAD_V, LANES), F32)],
        compiler_params=_cparams("parallel", "arbitrary"),
        name="rwkv_scan",
    )(r, v, kk, lw0, kd0, as0, r, v, kk, lw1, kd1, as1, s0)


def _rope(x, cos, sin_signed):
    lane = lax.broadcasted_iota(jnp.int32, x.shape, 1)
    n = x.shape[1]
    swapped = jnp.where((lane & ROPE_NF) == 0, pltpu.roll(x, n - ROPE_NF, 1), pltpu.roll(x, ROPE_NF, 1))
    return x * cos + swapped * sin_signed


def _ret_scan_kernel(nb, cpb, qf_ref, kf_ref, vf_ref, qb_ref, kb_ref, vb_ref,
                     dmat_ref, qdec_ref, kdec_ref, sdec_ref, s0_ref,
                     of_ref, ob_ref, sfin_ref, st_ref):
    c = pl.program_id(1)

    @pl.when(c == 0)
    def _():
        st_ref[...] = s0_ref[0]

    t = dmat_ref.shape[-1]
    npair = HEADS // 2
    low_t = lax.broadcasted_iota(jnp.int32, (t, LANES), 1) < HEAD_V
    low_s = lax.broadcasted_iota(jnp.int32, (HEAD_V, LANES), 1) < HEAD_V

    def bd(x, low):
        return jnp.concatenate([jnp.where(low, x, 0.0), jnp.where(low, 0.0, x)], axis=0)

    units = []
    for d, (q_ref, k_ref, v_ref) in enumerate(((qf_ref, kf_ref, vf_ref), (qb_ref, kb_ref, vb_ref))):
        for j in range(cpb):
            rows = slice(j * t, (j + 1) * t)
            q, k, v = q_ref[0, rows], k_ref[0, rows], v_ref[0, rows]
            qd = q * qdec_ref[d]
            kh = k * kdec_ref[d]
            for p in range(npair):
                sl = slice(p * LANES, (p + 1) * LANES)
                units.append(dict(d=d, j=j, p=p, q=q[:, sl], k=k[:, sl], qd=qd[:, sl], kh=kh[:, sl],
                                  v=v[:, sl], dec=sdec_ref[d][:, sl]))
    stacked = [_mm_nt(jnp.concatenate([jnp.where(low_t, u['q'], 0.0), jnp.where(low_t, 0.0, u['q'])], axis=0),
                      u['k']) for u in units]
    scores = [[s2[h * t:(h + 1) * t].astype(BF16) * dmat_ref[u['d'], 2 * u['p'] + h] for h in range(2)]
              for s2, u in zip(stacked, units)]
    intra = [_mm(jnp.concatenate(s2, axis=1), bd(u['v'], low_t)) for s2, u in zip(scores, units)]
    kv = []
    for u in units:
        full = _mm_tn(u['v'], u['kh'])
        kv.append(jnp.where(low_s, full[0:HEAD_V], full[HEAD_V:2 * HEAD_V]))
    idx = {(u['d'], u['j'], u['p']): i for i, u in enumerate(units)}
    entering = {}
    for d in range(2):
        for p in range(npair):
            st = st_ref[d, p]
            for step in range(cpb):
                j = step if d == 0 else cpb - 1 - step
                i = idx[(d, j, p)]
                entering[i] = st
                st = units[i]['dec'] * st + kv[i]
            st_ref[d, p] = st
    outs = [x + _mm_nt(u['qd'], bd(entering[i], low_s)) for i, (x, u) in enumerate(zip(intra, units))]
    for d, o_ref in enumerate((of_ref, ob_ref)):
        o_ref[0] = jnp.concatenate(
            [jnp.concatenate([outs[idx[(d, j, p)]] for p in range(npair)], axis=1) for j in range(cpb)], axis=0)

    @pl.when(c == nb - 1)
    def _():
        sfin_ref[0] = st_ref[...]


def _ret_tables(decay_logit, n):
    lg = jax.nn.log_sigmoid(decay_logit.astype(F32))
    pos = jnp.arange(n, dtype=F32)
    lag = pos[:, None] - pos[None, :]
    lag = jnp.stack([lag, -lag])
    dmat = jnp.where(lag[:, None] >= 0, jnp.exp(lg[:, :, None, None] * lag[:, None]), 0.0)
    lanes = jnp.repeat(lg, RET_DK, axis=-1)[:, None, :]
    qpow = jnp.stack([pos + 1.0, n - pos])[:, :, None]
    kpow = jnp.stack([n - 1.0 - pos, pos])[:, :, None]
    return dmat.astype(BF16), jnp.exp(lanes * qpow), jnp.exp(lanes * kpow), jnp.exp(lanes * n)


def _gla_scan_kernel(nb, cpb, qf_ref, kf_ref, vf_ref, af_ref, qb_ref, kb_ref, vb_ref, ab_ref,
                     aup_ref, abias_ref, s0_ref, of_ref, ob_ref, sfin_ref, st_ref):
    c = pl.program_id(1)

    @pl.when(c == 0)
    def _():
        st_ref[...] = s0_ref[0]

    t = GLA_T
    n = cpb * t
    dk = GLA_DK
    scale = dk ** -0.5
    dirs = ((qf_ref, kf_ref, vf_ref, af_ref), (qb_ref, kb_ref, vb_ref, ab_ref))
    units = []
    for d, (q_ref, k_ref, v_ref, a_ref) in enumerate(dirs):
        incl = _tri_incl(t, d == 1)
        q, k, v = q_ref[0], k_ref[0] * scale, v_ref[0]
        lw = -_softplus(-(_mm(a_ref[0], aup_ref[d]) + abias_ref[d])) * (1.0 / GLA_TAU)
        cin = _mm_left01(_chunk_tri01(n, t, d == 1), lw)
        qe = q * jnp.exp(cin)
        for j in range(cpb):
            rows = slice(j * t, (j + 1) * t)
            last = j * t + (0 if d == 1 else t - 1)
            mid = j * t + (t // 2 if d == 1 else t // 2 - 1)
            cj = cin[rows]
            clast = cin[last:last + 1]
            cmid = cin[mid:mid + 1]
            qt = q[rows] * jnp.exp(cj - cmid)
            kt = k[rows] * jnp.exp(cmid - cj)
            kh = k[rows] * jnp.exp(clast - cj)
            units.append(dict(d=d, j=j, incl=incl, qt=qt, kt=kt, qe=qe[rows], kh=kh, v=v[rows],
                              dec=jnp.exp(clast)))

    klane = lax.broadcasted_iota(jnp.int32, (t, LANES), 1) // dk
    vlow = lax.broadcasted_iota(jnp.int32, (t, LANES), 1) < HEAD_V
    khead = lax.broadcasted_iota(jnp.int32, (HEAD_V, LANES), 1) // dk
    npair = HEADS // 2

    def v_blockdiag(x):
        return jnp.concatenate([jnp.where(vlow, x, 0.0), jnp.where(vlow, 0.0, x)], axis=0)

    def state_rows(st, p):
        return jnp.concatenate([jnp.where(khead == 2 * p, st, 0.0), jnp.where(khead == 2 * p + 1, st, 0.0)],
                               axis=0)

    stacked = [_mm_nt(jnp.concatenate([jnp.where(klane == h, u['qt'], 0.0) for h in range(HEADS)], axis=0),
                      u['kt']) for u in units]
    amat = [[jnp.where(u['incl'], s4[h * t:(h + 1) * t], 0.0) for h in range(HEADS)]
            for s4, u in zip(stacked, units)]
    intra = [[_mm(jnp.concatenate(a[2 * p:2 * p + 2], axis=1), v_blockdiag(u['v'][:, p * LANES:(p + 1) * LANES]))
              for p in range(npair)] for a, u in zip(amat, units)]
    kv = []
    for u in units:
        full = _mm_tn(u['v'], u['kh'])
        blocks = [full[h * HEAD_V:(h + 1) * HEAD_V] for h in range(HEADS)]
        acc = blocks[HEADS - 1]
        for h in range(HEADS - 2, -1, -1):
            acc = jnp.where(khead == h, blocks[h], acc)
        kv.append(acc)
    idx = {(u['d'], u['j']): i for i, u in enumerate(units)}
    entering = {}
    for d in range(2):
        st = st_ref[d]
        for step in range(cpb):
            j = step if d == 0 else cpb - 1 - step
            i = idx[(d, j)]
            entering[i] = st
            st = units[i]['dec'] * st + kv[i]
        st_ref[d] = st
    inter = [_mm_nt(u['qe'], jnp.concatenate([state_rows(entering[i], p) for p in range(npair)], axis=0))
             for i, u in enumerate(units)]
    outs = [[x[p] + y[:, p * LANES:(p + 1) * LANES] for p in range(npair)] for x, y in zip(intra, inter)]
    for d, o_ref in enumerate((of_ref, ob_ref)):
        o_ref[0] = jnp.concatenate(
            [jnp.concatenate(outs[idx[(d, j)]], axis=1) for j in range(cpb)], axis=0)

    @pl.when(c == nb - 1)
    def _():
        sfin_ref[0] = st_ref[...]


def _rope_tables(rows):
    nf = ROPE_NF
    inv = ROPE_BASE ** (-jnp.arange(nf, dtype=F32) / nf)
    lane = jnp.arange(GROUP_W)
    freq = inv[lane % nf]
    sign = jnp.where((lane & ROPE_NF) == 0, -1.0, 1.0)
    ar = jnp.arange(rows, dtype=F32)[:, None] * freq[None, :]
    ac = jnp.arange(GRID_W, dtype=F32)[:, None] * freq[None, :]
    return (jnp.cos(ar).reshape(rows, 1, GROUP_W), (jnp.sin(ar) * sign).reshape(rows, 1, GROUP_W),
            jnp.cos(ac), jnp.sin(ac) * sign)


def _ret_scan(z, tables, s0):
    b, l, _ = z.shape
    n = min(RET_BLOCK, l)
    nb = l // n
    fw = lambda j: pl.BlockSpec((1, n, GROUP_W), lambda i, c: (i, c, j))
    bw = lambda j: pl.BlockSpec((1, n, GROUP_W), lambda i, c: (i, nb - 1 - c, j))
    full = lambda a: pl.BlockSpec(a.shape, lambda i, c: (0,) * a.ndim)
    st = pl.BlockSpec((1, 2, HEADS // 2, HEAD_V, LANES), lambda i, c: (i, 0, 0, 0, 0))
    return pl.pallas_call(
        functools.partial(_ret_scan_kernel, nb, n // RET_T),
        grid=(b, nb),
        in_specs=[fw(0), fw(1), fw(2), bw(0), bw(1), bw(2)] + [full(a) for a in tables] + [st],
        out_specs=[fw(0), bw(0), st],
        out_shape=[jax.ShapeDtypeStruct((b, l, GROUP_W), F32)] * 2
                  + [jax.ShapeDtypeStruct((b, 2, HEADS // 2, HEAD_V, LANES), F32)],
        scratch_shapes=[pltpu.VMEM((2, HEADS // 2, HEAD_V, LANES), F32)],
        compiler_params=_cparams("parallel", "arbitrary"),
        name="retention_scan",
    )(z, z, z, z, z, z, *tables, s0)


def _gla_scan(z, aup, abias, s0):
    b, l, _ = z.shape
    cpb = min(GLA_BLOCK, l) // CHUNK
    nb = l // (cpb * CHUNK)
    blk = lambda w, j, rev: pl.BlockSpec(
        (1, cpb * CHUNK, w), (lambda i, c: (i, nb - 1 - c, j)) if rev else (lambda i, c: (i, c, j)))
    st = pl.BlockSpec((1, 2, HEAD_V, GLA_QK), lambda i, c: (i, 0, 0, 0))
    ofw = pl.BlockSpec((1, cpb * CHUNK, GROUP_W), lambda i, c: (i, c, 0))
    obw = pl.BlockSpec((1, cpb * CHUNK, GROUP_W), lambda i, c: (i, nb - 1 - c, 0))
    return pl.pallas_call(
        functools.partial(_gla_scan_kernel, nb, cpb * CHUNK // GLA_T),
        grid=(b, nb),
        in_specs=[blk(GLA_QK, 0, False), blk(GLA_QK, 1, False), blk(GROUP_W, 1, False), blk(128, 6, False),
                  blk(GLA_QK, 0, True), blk(GLA_QK, 1, True), blk(GROUP_W, 1, True), blk(128, 6, True),
                  pl.BlockSpec(aup.shape, lambda i, c: (0, 0, 0)),
                  pl.BlockSpec(abias.shape, lambda i, c: (0, 0, 0)), st],
        out_specs=[ofw, obw, st],
        out_shape=[jax.ShapeDtypeStruct((b, l, GROUP_W), F32)] * 2
                  + [jax.ShapeDtypeStruct((b, 2, HEAD_V, GLA_QK), F32)],
        scratch_shapes=[pltpu.VMEM((2, HEAD_V, GLA_QK), F32)],
        compiler_params=_cparams("parallel", "arbitrary"),
        name="gla_scan",
    )(z, z, z, z, z, z, z, z, aup, abias, s0)


def _mix_kernel(x_ref, g1_ref, y5_ref, u5_ref, d5_ref, gw_ref, gb_ref,
                ryf_ref, ryb_ref, rbonus_ref, rg_ref, rlng_ref, rlnb_ref,
                tof_ref, tob_ref, tg_ref, tln_ref,
                gof_ref, gob_ref, gg_ref, gln_ref,
                bones_ref, wo_ref, o_ref, y5t_ref):
    bones = bones_ref[...]
    inv = 1.0 / HEAD_V

    def hmean(a):
        return _mm_right01(a, bones) * inv

    nch = y5_ref.shape[2]
    gph = LANES // S5_P
    y5g = [y5_ref[0, g] for g in range(S5_G)]
    pitch = S5_T + 1
    for s in range(S5_T):
        for hf in range(GROUP_W // LANES):
            y5t_ref[hf, pl.ds(s, nch, stride=pitch), :] = jnp.concatenate(
                [yg[:, s * S5_P:(s + 1) * S5_P] for yg in y5g[hf * gph:(hf + 1) * gph]], axis=1)
    y5t = jnp.concatenate(
        [jnp.concatenate([y5t_ref[hf, ch * pitch:ch * pitch + S5_T] for ch in range(nch)], axis=0)
         for hf in range(GROUP_W // LANES)], axis=1)
    y = y5t + d5_ref[...] * u5_ref[0]
    y = _gelu_tanh(y)
    ya = y * _sigmoid(_mm(y, gw_ref[...]) + gb_ref[...])
    yr = ryf_ref[0] + ryb_ref[0]
    dlt = yr - hmean(yr)
    yn = dlt * lax.rsqrt(hmean(dlt * dlt) + GN_EPS)
    yb = (yn * rlng_ref[...] + rlnb_ref[...] + rbonus_ref[0].astype(F32)) * rg_ref[0].astype(F32)
    ot = tof_ref[0] + tob_ref[0]
    yc = ot * lax.rsqrt(hmean(ot * ot) + EPS) * tln_ref[...] * _silu(tg_ref[0])
    og = gof_ref[0] + gob_ref[0]
    yd = og * lax.rsqrt(hmean(og * og) + EPS) * gln_ref[...] * _silu(gg_ref[0])
    mix = (_mm(ya, wo_ref[0]) + _mm(yb, wo_ref[1]) + _mm(yc, wo_ref[2]) + _mm(yd, wo_ref[3]))
    o_ref[0] = x_ref[0] + g1_ref[0] * mix


def _mix(x, g1, y5, u5, rw_pre, rw_y, zret, ret_o, zgla, gla_o, p, tm):
    b, l, d = x.shape
    tok = lambda w, j: pl.BlockSpec((1, tm, w), lambda i, t: (i, t, j))
    vec = lambda a: pl.BlockSpec(a.shape, lambda i, t: (0,) * a.ndim)
    t256 = tok(GROUP_W, 0)
    args = [x, g1, y5, u5, p['s5_d'], p['glu_w'], p['glu_b'],
            rw_y[0], rw_y[1], rw_pre[4], rw_pre[3], p['rw_ln_g'], p['rw_ln_b'],
            ret_o[0], ret_o[1], zret, p['ret_ln_g'],
            gla_o[0], gla_o[1], zgla, p['gla_ln_g'],
            p['bones'], p['w_out']]
    grouped = pl.BlockSpec((1, S5_G, tm // S5_T, S5_TP), lambda i, t: (i, 0, t, 0))
    specs = [tok(d, 0), pl.BlockSpec((1, 1, d), lambda i, t: (i, 0, 0)), grouped, t256,
             vec(p['s5_d']), vec(p['glu_w']), vec(p['glu_b']),
             t256, t256, t256, t256, vec(p['rw_ln_g']), vec(p['rw_ln_b']),
             t256, t256, tok(GROUP_W, 3), vec(p['ret_ln_g']),
             t256, t256, tok(GROUP_W, 2), vec(p['gla_ln_g']),
             vec(p['bones']), vec(p['w_out'])]
    return pl.pallas_call(
        _mix_kernel,
        grid=(b, l // tm),
        in_specs=specs,
        out_specs=tok(d, 0),
        out_shape=jax.ShapeDtypeStruct((b, l, d), F32),
        scratch_shapes=[pltpu.VMEM((GROUP_W // LANES, tm // S5_T * (S5_T + 1), LANES), F32)],
        compiler_params=_cparams("parallel", "parallel"),
        name="mix_outproj",
    )(*args)


def _mlp_kernel(final, nff, x_ref, g_ref, sc_ref, sh_ref, gate_ref, w1_ref, w2_ref, fg_ref, o_ref):
    x = x_ref[0]
    hb = _modnorm(x, g_ref[...], sc_ref[0], sh_ref[0]).astype(BF16)
    ff = w1_ref.shape[2] // nff
    acc = None
    for j in range(nff):
        a = jnp.maximum(jnp.dot(hb, w1_ref[0, :, j * ff:(j + 1) * ff], preferred_element_type=F32), 0.0)
        part = jnp.dot((a * a).astype(BF16), w2_ref[0, j * ff:(j + 1) * ff, :], preferred_element_type=F32)
        acc = part if acc is None else acc + part
    y = x + gate_ref[0] * acc
    if final:
        ms = jnp.mean(y * y, axis=-1, keepdims=True)
        y = y * lax.rsqrt(ms + EPS) * fg_ref[...]
    o_ref[0] = y


def _mlp(x, g, sc, sh, gate, w1, w2, layer, final_g, final, tm):
    b, l, d = x.shape
    tok = pl.BlockSpec((1, tm, d), lambda i, t: (i, t, 0))
    vec = pl.BlockSpec((1, 1, d), lambda i, t: (i, 0, 0))
    row = pl.BlockSpec((1, d), lambda i, t: (0, 0))
    once = lambda a: pl.BlockSpec((1,) + a.shape[1:], lambda i, t: (layer, 0, 0), pipeline_mode=pl.Buffered(1))
    return pl.pallas_call(
        functools.partial(_mlp_kernel, final, 4),
        grid=(b, l // tm),
        in_specs=[tok, row, vec, vec, vec, once(w1), once(w2), row],
        out_specs=tok,
        out_shape=jax.ShapeDtypeStruct((b, l, d), F32),
        compiler_params=_cparams("parallel", "parallel"),
        name="mlp",
    )(x, g, sc, sh, gate, w1, w2, final_g)


def kernel(x, c, ctx, c_ctx, ada_w, ada_b, norm1_g, norm2_g, w_in, w_out, s5_lam_re, s5_lam_im, s5_log_dt, s5_b_re, s5_b_im, s5_c_re, s5_c_im, s5_d, s5_glu_w, s5_glu_b, rw_mu, rw_w0, rw_w_up, rw_a0, rw_a_up, rw_g_up, rw_k_k, rw_k_a, rw_r_k, rw_ln_g, rw_ln_b, ret_decay_logit, ret_ln_g, gla_a_up, gla_a_b, gla_ln_g, mlp_w1, mlp_w2, final_g):
    b, l, d = x.shape
    lc = ctx.shape[1]
    depth = ada_w.shape[0]
    rows = l // GRID_W
    assert l % TOKEN_TILE == 0 and lc % RET_T == 0 and lc <= TOKEN_TILE and d % LANES == 0
    assert b < COND_ROWS and l % GRID_W == 0

    cond = jnp.zeros((COND_ROWS, d), F32).at[:b].set(c).at[b].set(c_ctx)
    mod = _modulation(cond, ada_w, ada_b)

    lane = jnp.arange(GROUP_W)
    bones = (lane[:, None] // HEAD_V == lane[None, :] // HEAD_V).astype(BF16)
    rope_lat = _rope_tables(rows)
    ident = (jnp.ones((lc // CHUNK, 1, GROUP_W), F32), jnp.zeros((lc // CHUNK, 1, GROUP_W), F32),
             jnp.ones((GRID_W, GROUP_W), F32), jnp.zeros((GRID_W, GROUP_W), F32))
    nlev = max(1, (l // S5_T - 1).bit_length())
    c0, c1, c2 = GROUP_W, GROUP_W + RW_COLS, GROUP_W + RW_COLS + 4 * GROUP_W
    row2 = lambda a: a.reshape(1, -1).astype(F32)
    w_in_b, w_out_b, w1_b, w2_b = _to_bf16(w_in), _to_bf16(w_out), _to_bf16(mlp_w1), _to_bf16(mlp_w2)

    s5_tab_all = jax.vmap(lambda *a: _s5_tables(*a, nlev))(
        s5_lam_re, s5_lam_im, s5_log_dt, s5_b_re, s5_b_im, s5_c_re, s5_c_im)
    ret_tab_all = jax.vmap(lambda a: _ret_tables(a, RET_T))(ret_decay_logit)
    zpad = lambda a, lo, n: jnp.zeros((depth, 2, LANES, n), F32).at[:, :, lo:lo + a.shape[2]].set(a).astype(BF16)
    rw_wup_all = zpad(rw_w_up, 0, GROUP_W)
    rw_aup_all = zpad(rw_a_up, RW_W_RANK, GROUP_W)
    rw_gup_all = jnp.zeros((depth, LANES, GROUP_W), F32).at[:, RW_W_RANK + RW_A_RANK:].set(rw_g_up).astype(BF16)
    gla_aup_all = zpad(gla_a_up, 0, GLA_QK)

    xc = ctx
    for i in range(depth):
        last = i == depth - 1
        m = mod[i].reshape(COND_ROWS, N_ADA, d)
        ml = m[:b, :, None, :]
        mc = jnp.broadcast_to(m[b][None, :, None, :], (b, N_ADA, 1, d))
        wi = w_in_b[i]
        w5, wr, wt = wi[:, :c0], wi[:, c0:c1], wi[:, c1:c2]
        wg = jnp.pad(wi[:, c2:], ((0, 0), (0, GLA_COLS_PAD - (wi.shape[1] - c2))))
        n1 = row2(norm1_g[i])
        n2 = row2(norm2_g[i])

        s5_tab = tuple(a[i] for a in s5_tab_all)
        rwp = dict(
            mu=row2(rw_mu[i]),
            w0=rw_w0[i].reshape(2, 1, GROUP_W), a0=rw_a0[i].reshape(2, 1, GROUP_W),
            wup=rw_wup_all[i], aup=rw_aup_all[i], gup=rw_gup_all[i],
            kk=row2(rw_k_k[i]), ka=row2(rw_k_a[i]), rk=row2(rw_r_k[i]), bones=bones)
        ret_tab = tuple(a[i] for a in ret_tab_all)
        gla_aup = gla_aup_all[i]
        gla_ab = gla_a_b[i].reshape(2, 1, GLA_QK).astype(F32)
        mixp = dict(s5_d=row2(s5_d[i]), glu_w=s5_glu_w[i].astype(BF16), glu_b=row2(s5_glu_b[i]),
                    rw_ln_g=row2(rw_ln_g[i]), rw_ln_b=row2(rw_ln_b[i]), ret_ln_g=row2(ret_ln_g[i]),
                    gla_ln_g=row2(gla_ln_g[i]), bones=bones,
                    w_out=w_out_b[i].reshape(4, GROUP_W, d))

        def mixers(xx, mm, is_lat, states):
            tm = TOKEN_TILE if is_lat else lc
            z5, z5g, zt, zg, pre = _inproj(xx, n1, mm[:, 1], mm[:, 0], w5, wr, wt, wg,
                                           rope_lat if is_lat else ident, rwp, is_lat, tm)
            y5, h5 = _s5_scan(z5g, s5_tab, states[0])
            yrf, yrb, srw = _rw_scan(pre, states[1])
            otf, otb, sret = _ret_scan(zt, ret_tab, states[2])
            ogf, ogb, sgla = _gla_scan(zg, gla_aup, gla_ab, states[3])
            outs = (z5, y5, pre, (yrf, yrb), zt, (otf, otb), zg, (ogf, ogb))
            return outs, (h5, srw, sret, sgla)

        def block(xx, mm, outs, is_lat, fin):
            tm = TOKEN_TILE if is_lat else lc
            z5, y5, pre, yr, zt, ot, zg, og = outs
            x1 = _mix(xx, mm[:, 2], y5, z5, pre, yr, zt, ot, zg, og, mixp, tm)
            return _mlp(x1, n2, mm[:, 4], mm[:, 3], mm[:, 5], w1_b, w2_b, i, row2(final_g), fin, tm)

        zeros = (jnp.zeros((b, 2, S5_G, 1, 2 * S5_N), F32),
                 jnp.zeros((b, 2, HEADS // 2, HEAD_V, LANES), F32),
                 jnp.zeros((b, 2, HEADS // 2, HEAD_V, LANES), F32),
                 jnp.zeros((b, 2, HEAD_V, GLA_QK), F32))
        outs_c, st_c = mixers(xc, mc, False, zeros)
        outs_l, _ = mixers(x, ml, True, st_c)
        x = block(x, ml, outs_l, True, last)
        if not last:
            xc = block(xc, mc, outs_c, False, False)
    return x
```

```python
import functools
import math

import jax
import jax.numpy as jnp
from jax import lax
from jax.experimental import pallas as pl
from jax.experimental.pallas import tpu as pltpu

F32 = jnp.float32
BF16 = jnp.bfloat16

LANES = 128
GRID_W = 64
GROUP_W = 256
N_ADA = 6
EPS = 1e-6
GN_EPS = 64e-5
TOKEN_TILE = 512
CHUNK = 64
RW_CPB = 4
RET_T = 256
RET_BLOCK = 512
GLA_T = 128
GLA_BLOCK = 512
HEADS = 4
HEAD_V = 64
S5_P = 16
S5_G = 16
S5_N = 64
S5_T = 32
S5_TP = S5_T * S5_P
RW_COLS = 896
RW_W_RANK = 32
RW_A_RANK = 32
RW_G_RANK = 64
RET_DK = 64
GLA_DK = 32
GLA_QK = 128
GLA_RANK = 16
GLA_TAU = 16.0
GLA_COLS_PAD = 896
ROPE_BASE = 10000.0
ROPE_HALF = RET_DK // 2
ROPE_NF = RET_DK // 4
COND_ROWS = 8
CAST_ROWS = 256
ADA_TILE = 1536
VMEM_LIMIT = 56 * 1024 * 1024


def _cparams(*sem):
    return pltpu.CompilerParams(dimension_semantics=sem, vmem_limit_bytes=VMEM_LIMIT)


def _mm(a, b):
    return jnp.dot(a.astype(BF16), b.astype(BF16), preferred_element_type=F32)


def _mm_nt(a, b):
    return lax.dot_general(a.astype(BF16), b.astype(BF16), (((1,), (1,)), ((), ())),
                           preferred_element_type=F32)


def _mm_tn(a, b):
    return lax.dot_general(a.astype(BF16), b.astype(BF16), (((0,), (0,)), ((), ())),
                           preferred_element_type=F32)


def _split2(x):
    hi = x.astype(BF16)
    return hi, (x - hi.astype(F32)).astype(BF16)


def _chunk_tri01(n, t, reverse):
    ri = lax.broadcasted_iota(jnp.int32, (n, n), 0)
    ci = lax.broadcasted_iota(jnp.int32, (n, n), 1)
    shift = t.bit_length() - 1
    tri = (ci >= ri) if reverse else (ci <= ri)
    return jnp.where((ri >> shift) == (ci >> shift), jnp.where(tri, 1.0, 0.0), 0.0).astype(BF16)


def _mm_left01(m01, x):
    hi, lo = _split2(x)
    return jnp.dot(m01, hi, preferred_element_type=F32) + jnp.dot(m01, lo, preferred_element_type=F32)


def _mm_right01(x, m01):
    return jnp.dot(x.astype(BF16), m01, preferred_element_type=F32)


def _sigmoid(x):
    return 1.0 / (1.0 + jnp.exp(-x))


def _softplus(x):
    return jnp.maximum(x, 0.0) + jnp.log(1.0 + jnp.exp(-jnp.abs(x)))


def _silu(x):
    return x * _sigmoid(x)


def _gelu_tanh(x):
    return 0.5 * x * (1.0 + jnp.tanh(math.sqrt(2.0 / math.pi) * (x + 0.044715 * x * x * x)))


def _tri_incl(t, reverse):
    ri = lax.broadcasted_iota(jnp.int32, (t, t), 0)
    ci = lax.broadcasted_iota(jnp.int32, (t, t), 1)
    return (ci >= ri) if reverse else (ci <= ri)


def _cast_kernel(x_ref, o_ref):
    o_ref[...] = x_ref[...].astype(o_ref.dtype)


def _to_bf16(w):
    depth, r, c = w.shape
    tr = CAST_ROWS
    spec = pl.BlockSpec((1, tr, c), lambda i, j: (i, j, 0))
    return pl.pallas_call(
        _cast_kernel,
        grid=(depth, r // tr),
        in_specs=[spec],
        out_specs=spec,
        out_shape=jax.ShapeDtypeStruct(w.shape, BF16),
        compiler_params=_cparams("parallel", "parallel"),
        name="cast_bf16",
    )(w)


def _mod_kernel(cond_ref, w_ref, b_ref, o_ref):
    c = cond_ref[...]
    o_ref[0] = _mm(_silu(c), w_ref[0]) + b_ref[0]


def _modulation(cond, ada_w, ada_b):
    depth, d, n = ada_w.shape
    tn = ADA_TILE
    return pl.pallas_call(
        _mod_kernel,
        grid=(depth, n // tn),
        in_specs=[pl.BlockSpec((COND_ROWS, d), lambda i, j: (0, 0)),
                  pl.BlockSpec((1, d, tn), lambda i, j: (i, 0, j)),
                  pl.BlockSpec((1, 1, tn), lambda i, j: (i, 0, j))],
        out_specs=pl.BlockSpec((1, COND_ROWS, tn), lambda i, j: (i, 0, j)),
        out_shape=jax.ShapeDtypeStruct((depth, COND_ROWS, n), F32),
        compiler_params=_cparams("parallel", "parallel"),
        name="adaln_mod",
    )(cond, ada_w, ada_b.reshape(depth, 1, n))


def _modnorm(x, g, sc, sh):
    ms = jnp.mean(x * x, axis=-1, keepdims=True)
    return x * lax.rsqrt(ms + EPS) * g * (1.0 + sc) + sh


def _shift_columns(z, segments):
    tm, width = z.shape
    lane = lax.broadcasted_iota(jnp.int32, (tm, LANES), 1)
    starts = [s for s, _ in segments] + [width]
    cols = []
    for c in range(width // LANES):
        lo, hi = c * LANES, (c + 1) * LANES
        inside = [(max(starts[i], lo), segments[i][1]) for i in range(len(segments))
                  if starts[i] < hi and starts[i + 1] > lo]
        col = inside[-1][1](c)
        for first, fn in reversed(inside[:-1]):
            nxt = [f for f, _ in inside if f > first][0]
            col = jnp.where(lane < nxt - lo, fn(c), col)
        cols.append(col)
    return jnp.concatenate(cols, axis=1)


def _rw_prep_math(grid_shift, z, zp, zn, mu, w0_ref, wup_ref, a0_ref, aup_ref, gup, kkp, ka, rk, bones):
    tm = z.shape[0]
    row = lax.broadcasted_iota(jnp.int32, (tm, LANES), 0)
    col = lambda x, c: x[:, c * LANES:(c + 1) * LANES]
    if grid_shift:
        pos = row & (GRID_W - 1)
        left = lambda c: jnp.where(pos == 0, 0.0, pltpu.roll(col(z, c), 1, 0))
        right = lambda c: jnp.where(pos == GRID_W - 1, 0.0, pltpu.roll(col(z, c), tm - 1, 0))
        up = lambda c: jnp.concatenate([col(zp, c), col(z, c)[:tm - GRID_W]], axis=0)
        down = lambda c: jnp.concatenate([col(z, c)[GRID_W:], col(zn, c)], axis=0)
        q = RW_COLS // 4
        shifted = _shift_columns(z, [(0, left), (q, right), (2 * q, up), (3 * q, down)])
    else:
        prev = lambda c: jnp.where(row == 0, 0.0, pltpu.roll(col(z, c), 1, 0))
        nxt = lambda c: jnp.where(row == tm - 1, 0.0, pltpu.roll(col(z, c), tm - 1, 0))
        shifted = _shift_columns(z, [(0, prev), (RW_COLS // 2, nxt)])
    zm = z + mu * (shifted - z)
    r = zm[:, 0:GROUP_W]
    k = zm[:, GROUP_W:2 * GROUP_W]
    v = zm[:, 2 * GROUP_W:3 * GROUP_W]
    lo = zm[:, 3 * GROUP_W:RW_COLS]
    g = _mm(_sigmoid(lo), gup)
    kk = k * kkp
    kk = kk * lax.rsqrt(_mm_right01(kk * kk, bones) + 1e-12)
    bonus = _mm_right01(r * k * rk, bones) * v
    outs = [r.astype(BF16), v.astype(BF16), kk.astype(BF16), g.astype(BF16), bonus.astype(BF16)]
    tlo = jnp.tanh(lo)
    for d in range(2):
        log_decay = -math.exp(-0.5) * _sigmoid(w0_ref[d] + _mm(tlo, wup_ref[d]))
        a = _sigmoid(a0_ref[d] + _mm(lo, aup_ref[d]))
        outs += [log_decay,
                 (k * (1.0 + (a - 1.0) * ka)).astype(BF16), a.astype(BF16)]
    return outs


def _inproj_kernel(grid_shift, nt, x_ref, xp_ref, xn_ref, g_ref, sc_ref, sh_ref,
                   w5_ref, wr_ref, wt_ref, wg_ref, cr_ref, sr_ref, cc_ref, sn_ref,
                   mu_ref, w0_ref, wup_ref, a0_ref, aup_ref, gup_ref, kk_ref, ka_ref, rk_ref, bones_ref,
                   o5_ref, o5g_ref, ot_ref, og_ref, *rest):
    rw_refs, z5h_ref = rest[:-1], rest[-1]
    j = pl.program_id(1)
    norm = lambda x: _modnorm(x, g_ref[...], sc_ref[0], sh_ref[0]).astype(BF16)
    hb = norm(x_ref[0])
    z5 = jnp.dot(hb, w5_ref[...], preferred_element_type=F32)
    o5_ref[0] = z5
    og_ref[0] = jnp.dot(hb, wg_ref[...], preferred_element_type=F32)
    zr = jnp.dot(hb, wr_ref[...], preferred_element_type=F32)
    zp = zn = None
    if grid_shift:
        zp = jnp.where(j > 0, jnp.dot(norm(xp_ref[0]), wr_ref[...], preferred_element_type=F32), 0.0)
        zn = jnp.where(j < nt - 1, jnp.dot(norm(xn_ref[0]), wr_ref[...], preferred_element_type=F32), 0.0)
    rw = _rw_prep_math(grid_shift, zr, zp, zn, mu_ref[...], w0_ref, wup_ref, a0_ref, aup_ref, gup_ref[...],
                       kk_ref[...], ka_ref[...], rk_ref[...], bones_ref[...])
    for ref, val in zip(rw_refs, rw):
        ref[0] = val
    zt = jnp.dot(hb, wt_ref[...], preferred_element_type=F32)
    lane = lax.broadcasted_iota(jnp.int32, (GRID_W, GROUP_W), 1)
    by_row = (lane & ROPE_HALF) == 0
    nrow = cr_ref.shape[0]
    cos = jnp.concatenate([jnp.where(by_row, cr_ref[j], cc_ref[...]) for j in range(nrow)], axis=0)
    sin = jnp.concatenate([jnp.where(by_row, sr_ref[j], sn_ref[...]) for j in range(nrow)], axis=0)
    ot_ref[0, :, 0:GROUP_W] = _rope(zt[:, 0:GROUP_W], cos, sin)
    ot_ref[0, :, GROUP_W:2 * GROUP_W] = _rope(zt[:, GROUP_W:2 * GROUP_W] * RET_DK ** -0.5, cos, sin)
    ot_ref[0, :, 2 * GROUP_W:] = zt[:, 2 * GROUP_W:]
    nch = o5g_ref.shape[2]
    gph = LANES // S5_P
    for hf in range(GROUP_W // LANES):
        z5h_ref[hf] = z5[:, hf * LANES:(hf + 1) * LANES]
    for s in range(S5_T):
        for hf in range(GROUP_W // LANES):
            rows = z5h_ref[hf, pl.ds(s, nch, stride=S5_T), :]
            for g in range(gph):
                o5g_ref[0, hf * gph + g, :, s * S5_P:(s + 1) * S5_P] = rows[:, g * S5_P:(g + 1) * S5_P]


def _inproj(x, g, sc, sh, w5, wr, wt, wg, rope, rwp, grid_shift, tm):
    b, l, d = x.shape
    nt = l // tm
    hb = tm // GRID_W
    nhb = l // GRID_W
    tok = lambda n: pl.BlockSpec((1, tm, n), lambda i, j: (i, j, 0))
    vec = pl.BlockSpec((1, 1, d), lambda i, j: (i, 0, 0))
    full = lambda a: pl.BlockSpec(a.shape, lambda i, j: (0,) * a.ndim)
    grouped = pl.BlockSpec((1, S5_G, tm // S5_T, S5_TP), lambda i, j: (i, 0, j, 0))
    rowt = pl.BlockSpec((hb, 1, GROUP_W), lambda i, j: (j, 0, 0))
    above = pl.BlockSpec((1, GRID_W, d), lambda i, j: (i, jnp.maximum(j * hb - 1, 0), 0))
    below = pl.BlockSpec((1, GRID_W, d), lambda i, j: (i, jnp.minimum((j + 1) * hb, nhb - 1), 0))
    cr, sr, cc, sn = rope
    params = (rwp['mu'], rwp['w0'], rwp['wup'], rwp['a0'], rwp['aup'], rwp['gup'], rwp['kk'], rwp['ka'],
              rwp['rk'], rwp['bones'])
    rw_dtypes = (BF16,) * 5 + (F32, BF16, BF16) * 2
    outs = pl.pallas_call(
        functools.partial(_inproj_kernel, grid_shift, nt),
        grid=(b, nt),
        in_specs=[tok(d), above, below, pl.BlockSpec((1, d), lambda i, j: (0, 0)), vec, vec,
                  full(w5), full(wr), full(wt), full(wg), rowt, rowt, full(cc), full(sn)]
                 + [full(a) for a in params],
        out_specs=[tok(GROUP_W), grouped, tok(wt.shape[1]), tok(wg.shape[1])] + [tok(GROUP_W)] * 11,
        out_shape=[jax.ShapeDtypeStruct((b, l, GROUP_W), F32),
                   jax.ShapeDtypeStruct((b, S5_G, l // S5_T, S5_TP), F32),
                   jax.ShapeDtypeStruct((b, l, wt.shape[1]), F32),
                   jax.ShapeDtypeStruct((b, l, wg.shape[1]), F32)]
                  + [jax.ShapeDtypeStruct((b, l, GROUP_W), dt) for dt in rw_dtypes],
        scratch_shapes=[pltpu.VMEM((GROUP_W // LANES, tm, LANES), F32)],
        compiler_params=_cparams("parallel", "parallel"),
        name="norm_inproj",
    )(x, x, x, g, sc, sh, w5, wr, wt, wg, cr, sr, cc, sn, *params)
    return outs[0], outs[1], outs[2], outs[3], tuple(outs[4:])


def _s5_tables(lam_re, lam_im, log_dt, b_re, b_im, c_re, c_im, nlev):
    hp = lax.Precision.HIGHEST
    t = S5_T
    lam = lax.complex(jnp.minimum(lam_re.astype(F32), -1e-4), lam_im.astype(F32))
    ldt = lam * jnp.exp(log_dt.astype(F32))[..., None]
    a_bar = jnp.exp(ldt)
    bb = ((a_bar - 1.0) / lam)[..., None] * lax.complex(b_re.astype(F32), b_im.astype(F32))
    cm = lax.complex(c_re.astype(F32), c_im.astype(F32))
    tau = jnp.arange(t + 1, dtype=F32)
    apow = jnp.exp(ldt[:, :, None, :] * tau[None, None, :, None])
    taps = jnp.einsum('dgpn,dgtn,dgnq->dgtpq', cm, apow[:, :, :t], bb, precision=hp).real
    taprow = jnp.stack([taps[0], taps[1][:, ::-1]]).transpose(0, 1, 4, 2, 3).reshape(2, S5_G, S5_P, S5_TP)

    def pack(zc):
        return jnp.concatenate([zc.real, zc.imag], axis=-1)

    win_f = apow[0][:, t - 1 - jnp.arange(t), None, :] * bb[0].transpose(0, 2, 1)[:, None]
    win_b = apow[1][:, jnp.arange(t), None, :] * bb[1].transpose(0, 2, 1)[:, None]
    win = jnp.stack([pack(win_f), pack(win_b)]).reshape(2, S5_G, S5_TP, 2 * S5_N)
    ca_f = cm[0][:, None] * apow[0][:, 1 + jnp.arange(t), None, :]
    ca_b = cm[1][:, None] * apow[1][:, t - jnp.arange(t), None, :]

    def outpack(ca):
        w = jnp.concatenate([ca.real, -ca.imag], axis=-1)
        return w.reshape(S5_G, S5_TP, 2 * S5_N).transpose(0, 2, 1)

    wout = jnp.stack([outpack(ca_f), outpack(ca_b)])
    lev = (2.0 ** jnp.arange(nlev, dtype=F32)) * t
    pw = jnp.exp(ldt[:, :, None, :] * lev[None, None, :, None])
    p1 = jnp.concatenate([pw.real, pw.real], axis=-1)
    p2 = jnp.concatenate([-pw.imag, pw.imag], axis=-1)
    pw = jnp.stack([p1, p2], axis=3)
    return taprow, win.astype(BF16), wout.astype(BF16), pw


def _s5_kernel(nc, nlev, u_ref, tap_ref, win_ref, wout_ref, pw_ref, h0_ref, y_ref, hfin_ref, conv_ref):
    u = u_ref[0, 0].astype(BF16)
    row = lax.broadcasted_iota(jnp.int32, (nc, 2 * S5_N), 0)
    lane = lax.broadcasted_iota(jnp.int32, (S5_P, S5_TP), 1)
    for s in range(S5_T):
        lo = s * S5_P
        fwd = tap_ref[0, 0] if s == 0 else jnp.where(lane >= lo, pltpu.roll(tap_ref[0, 0], lo, 1), 0.0)
        hi = lo + S5_P
        bwd = tap_ref[1, 0] if hi == S5_TP else jnp.where(lane < hi, pltpu.roll(tap_ref[1, 0], hi, 1), 0.0)
        conv_ref[0, lo:hi, :] = fwd.astype(BF16)
        conv_ref[1, lo:hi, :] = bwd.astype(BF16)

    def cmul(x, d, j):
        return pw_ref[d, 0, j, 0:1] * x + pw_ref[d, 0, j, 1:2] * pltpu.roll(x, S5_N, 1)

    v = [jnp.dot(u, win_ref[d, 0], preferred_element_type=F32) for d in range(2)]
    x = [jnp.where(row == 0, h0_ref[0, 0, 0], pltpu.roll(v[0], 1, 0)),
         jnp.where(row == nc - 1, h0_ref[0, 1, 0], pltpu.roll(v[1], nc - 1, 0))]
    for j in range(nlev):
        sh = 2 ** j
        xs = [jnp.where(row >= sh, pltpu.roll(x[0], sh, 0), 0.0),
              jnp.where(row < nc - sh, pltpu.roll(x[1], nc - sh, 0), 0.0)]
        x = [x[d] + cmul(xs[d], d, j) for d in range(2)]
    for d, last in enumerate((nc - 1, 0)):
        hfin_ref[0, d, 0] = cmul(x[d][last:last + 1], d, 0) + v[d][last:last + 1]
    y_ref[0, 0] = (jnp.dot(u, conv_ref[0], preferred_element_type=F32) + _mm(x[0], wout_ref[0, 0])
                   + jnp.dot(u, conv_ref[1], preferred_element_type=F32) + _mm(x[1], wout_ref[1, 0]))


def _s5_scan(uf, tables, h0):
    conv, win, wout, pw = tables
    b, _, nc, _ = uf.shape
    nlev = max(1, (nc - 1).bit_length())
    pw = pw[:, :, :nlev]
    n2 = 2 * S5_N
    y, hfin = pl.pallas_call(
        functools.partial(_s5_kernel, nc, nlev),
        grid=(b, S5_G),
        in_specs=[pl.BlockSpec((1, 1, nc, S5_TP), lambda i, g: (i, g, 0, 0)),
                  pl.BlockSpec((2, 1, S5_P, S5_TP), lambda i, g: (0, g, 0, 0)),
                  pl.BlockSpec((2, 1, S5_TP, n2), lambda i, g: (0, g, 0, 0)),
                  pl.BlockSpec((2, 1, n2, S5_TP), lambda i, g: (0, g, 0, 0)),
                  pl.BlockSpec((2, 1, nlev, 2, n2), lambda i, g: (0, g, 0, 0, 0)),
                  pl.BlockSpec((1, 2, 1, 1, n2), lambda i, g: (i, 0, g, 0, 0))],
        out_specs=[pl.BlockSpec((1, 1, nc, S5_TP), lambda i, g: (i, g, 0, 0)),
                   pl.BlockSpec((1, 2, 1, 1, n2), lambda i, g: (i, 0, g, 0, 0))],
        out_shape=[jax.ShapeDtypeStruct((b, S5_G, nc, S5_TP), F32),
                   jax.ShapeDtypeStruct((b, 2, S5_G, 1, n2), F32)],
        scratch_shapes=[pltpu.VMEM((2, S5_TP, S5_TP), BF16)],
        compiler_params=_cparams("parallel", "parallel"),
        name="s5_scan",
    )(uf, conv, win, wout, pw, h0)
    return y, hfin


def _rw_scan_kernel(nb, cpb, rf_ref, vf_ref, kkf_ref, lwf_ref, kdf_ref, asf_ref,
                    rb_ref, vb_ref, kkb_ref, lwb_ref, kdb_ref, asb_ref, s0_ref,
                    yf_ref, yb_ref, sfin_ref, st_ref):
    c = pl.program_id(1)

    @pl.when(c == 0)
    def _():
        st_ref[...] = s0_ref[0]

    t = CHUNK
    ri = lax.broadcasted_iota(jnp.int32, (t, LANES), 0)
    li = lax.broadcasted_iota(jnp.int32, (t, LANES), 1)
    ci = li & (t - 1)
    low = li < HEAD_V
    eyef = jnp.where(ri == ci, 1.0, 0.0)
    same = lambda s: jnp.where((ri >> s) == (ci >> s), 1.0, 0.0)
    m4, m8, m16, m32 = same(2), same(3), same(4), same(5)
    merge_masks = (m8 - m4, m16 - m8, m32 - m16, 1.0 - m32)

    def bd(x):
        return jnp.concatenate([jnp.where(low, x, 0.0), jnp.where(low, 0.0, x)], axis=0)

    def diag_blocks(full):
        return jnp.where(low, full[0:HEAD_V], full[HEAD_V:2 * HEAD_V])

    dir_refs = ((rf_ref, vf_ref, kkf_ref, lwf_ref, kdf_ref, asf_ref),
                (rb_ref, vb_ref, kkb_ref, lwb_ref, kdb_ref, asb_ref))
    units = []
    for d, (r_ref, v_ref, kk_ref, lw_ref, kd_ref, as_ref) in enumerate(dir_refs):
        incl = (ci >= ri) if d == 1 else (ci <= ri)
        strict = (ci > ri) if d == 1 else (ci < ri)
        r, v, kk, kd = (x[0].astype(F32) for x in (r_ref, v_ref, kk_ref, kd_ref))
        lw = lw_ref[0]
        cin = _mm_left01(_chunk_tri01(cpb * t, t, d == 1), lw)
        e_in = jnp.exp(cin)
        e_neg = jnp.exp(-cin)
        rt = r * e_in
        at = -kk * jnp.exp(cin - lw)
        bvec = kk * as_ref[0].astype(F32)
        bt = bvec * e_neg
        kt = kd * e_neg
        for j in range(cpb):
            rows = slice(j * t, (j + 1) * t)
            last = j * t + (0 if d == 1 else t - 1)
            clast = cin[last:last + 1]
            dl = jnp.exp(clast - cin[rows])
            bh = bvec[rows] * dl
            kh = kd[rows] * dl
            dec = jnp.exp(clast)
            for p in range(HEADS // 2):
                sl = slice(p * LANES, (p + 1) * LANES)
                units.append(dict(d=d, j=j, p=p, incl=incl, strict=strict,
                                  at=at[rows, sl], rt=rt[rows, sl], bt=bt[rows, sl], kt=kt[rows, sl],
                                  bh=bh[:, sl], kh=kh[:, sl], v=v[rows, sl], dec=dec[:, sl]))

    x1 = [jnp.concatenate([u['at'], u['rt']], axis=0) for u in units]
    scores = [_mm_nt(x, jnp.concatenate([bd(u['bt']), bd(u['kt'])], axis=0))
              for x, u in zip(x1, units)]
    nmat = [jnp.where(u['strict'], a[0:t, :LANES], 0.0) for u, a in zip(units, scores)]
    a_rb = [jnp.where(u['incl'], a[t:2 * t, :LANES], 0.0) for u, a in zip(units, scores)]
    a_kk = [jnp.concatenate([jnp.where(u['strict'], a[0:t, LANES:], 0.0),
                             jnp.where(u['incl'], a[t:2 * t, LANES:], 0.0)], axis=0)
            for u, a in zip(units, scores)]
    akv = [_mm(a, bd(u['v'])) for u, a in zip(units, a_kk)]
    kv = [diag_blocks(_mm_tn(u['v'], u['kh'])) for u in units]
    nd = [x * m4 for x in nmat]
    n2 = [_mm(x, bd(x)) for x in nd]
    tinv = [eyef + x + _mm(eyef + x, bd(y)) for x, y in zip(nd, n2)]
    for mk in merge_masks:
        w = [_mm(ti, bd(x * mk)) for ti, x in zip(tinv, nmat)]
        tinv = [ti + _mm(wi, bd(ti)) for ti, wi in zip(tinv, w)]
    zz = [_mm(ti, jnp.concatenate([bd(u['at']), bd(kvv[0:t])], axis=1))
          for ti, u, kvv in zip(tinv, units, akv)]
    ght = [_mm_tn(z, u['bh']) for u, z in zip(units, zz)]
    qy = [_mm(a, jnp.concatenate([bd(z[:, :LANES]), bd(z[:, LANES:])], axis=1))
          for a, z in zip(a_rb, zz)]
    qmat = [u['rt'] + x[:, :LANES] for u, x in zip(units, qy)]
    y0 = [x[:, LANES:] + kvv[t:2 * t] for x, kvv in zip(qy, akv)]
    gmat = [diag_blocks(x[:LANES]) for x in ght]
    hmat = [diag_blocks(x[LANES:]) + k2 for x, k2 in zip(ght, kv)]
    idx = {(u['d'], u['j'], u['p']): n for n, u in enumerate(units)}
    npair = HEADS // 2
    state = {(d, p): st_ref[d, p] for d in range(2) for p in range(npair)}
    ys = {}
    for step in range(cpb):
        for d in range(2):
            j = step if d == 0 else cpb - 1 - step
            for p in range(npair):
                n = idx[(d, j, p)]
                st = state[(d, p)]
                ys[(d, j, p)] = _mm_nt(qmat[n], bd(st)) + y0[n]
                state[(d, p)] = units[n]['dec'] * st + _mm(st, bd(gmat[n])) + hmat[n]
    for d, y_ref in enumerate((yf_ref, yb_ref)):
        y_ref[0] = jnp.concatenate(
            [jnp.concatenate([ys[(d, j, p)] for p in range(npair)], axis=1) for j in range(cpb)], axis=0)
        for p in range(npair):
            st_ref[d, p] = state[(d, p)]

    @pl.when(c == nb - 1)
    def _():
        sfin_ref[0] = st_ref[...]


def _rw_scan(pre, s0):
    r, v, kk, _, _, lw0, kd0, as0, lw1, kd1, as1 = pre
    b, l, _ = r.shape
    cpb = min(RW_CPB, l // CHUNK)
    nb = l // (cpb * CHUNK)
    fw = pl.BlockSpec((1, cpb * CHUNK, GROUP_W), lambda i, c: (i, c, 0))
    bw = pl.BlockSpec((1, cpb * CHUNK, GROUP_W), lambda i, c: (i, nb - 1 - c, 0))
    st = pl.BlockSpec((1, 2, HEADS // 2, HEAD_V, LANES), lambda i, c: (i, 0, 0, 0, 0))
    return pl.pallas_call(
        functools.partial(_rw_scan_kernel, nb, cpb),
        grid=(b, nb),
        in_specs=[fw] * 6 + [bw] * 6 + [st],
        out_specs=[fw, bw, st],
        out_shape=[jax.ShapeDtypeStruct((b, l, GROUP_W), F32)] * 2
                  + [jax.ShapeDtypeStruct((b, 2, HEADS // 2, HEAD_V, LANES), F32)],
        scratch_shapes=[pltpu.VMEM((2, HEADS // 2, HEAD_V, LANES), F32)],
        compiler_params=_cparams("parallel", "arbitrary"),
        name="rwkv_scan",
    )(r, v, kk, lw0, kd0, as0, r, v, kk, lw1, kd1, as1, s0)


def _rope(x, cos, sin_signed):
    lane = lax.broadcasted_iota(jnp.int32, x.shape, 1)
    n = x.shape[1]
    swapped = jnp.where((lane & ROPE_NF) == 0, pltpu.roll(x, n - ROPE_NF, 1), pltpu.roll(x, ROPE_NF, 1))
    return x * cos + swapped * sin_signed


def _ret_scan_kernel(nb, cpb, qf_ref, kf_ref, vf_ref, qb_ref, kb_ref, vb_ref,
                     dmat_ref, qdec_ref, kdec_ref, sdec_ref, s0_ref,
                     of_ref, ob_ref, sfin_ref, st_ref):
    c = pl.program_id(1)

    @pl.when(c == 0)
    def _():
        st_ref[...] = s0_ref[0]

    t = dmat_ref.shape[-1]
    npair = HEADS // 2
    low_t = lax.broadcasted_iota(jnp.int32, (t, LANES), 1) < HEAD_V
    low_s = lax.broadcasted_iota(jnp.int32, (HEAD_V, LANES), 1) < HEAD_V

    def bd(x, low):
        return jnp.concatenate([jnp.where(low, x, 0.0), jnp.where(low, 0.0, x)], axis=0)

    units = []
    for d, (q_ref, k_ref, v_ref) in enumerate(((qf_ref, kf_ref, vf_ref), (qb_ref, kb_ref, vb_ref))):
        for j in range(cpb):
            rows = slice(j * t, (j + 1) * t)
            q, k, v = q_ref[0, rows], k_ref[0, rows], v_ref[0, rows]
            qd = q * qdec_ref[d]
            kh = k * kdec_ref[d]
            for p in range(npair):
                sl = slice(p * LANES, (p + 1) * LANES)
                units.append(dict(d=d, j=j, p=p, q=q[:, sl], k=k[:, sl], qd=qd[:, sl], kh=kh[:, sl],
                                  v=v[:, sl], dec=sdec_ref[d][:, sl]))
    stacked = [_mm_nt(jnp.concatenate([jnp.where(low_t, u['q'], 0.0), jnp.where(low_t, 0.0, u['q'])], axis=0),
                      u['k']) for u in units]
    scores = [[s2[h * t:(h + 1) * t].astype(BF16) * dmat_ref[u['d'], 2 * u['p'] + h] for h in range(2)]
              for s2, u in zip(stacked, units)]
    intra = [_mm(jnp.concatenate(s2, axis=1), bd(u['v'], low_t)) for s2, u in zip(scores, units)]
    kv = []
    for u in units:
        full = _mm_tn(u['v'], u['kh'])
        kv.append(jnp.where(low_s, full[0:HEAD_V], full[HEAD_V:2 * HEAD_V]))
    idx = {(u['d'], u['j'], u['p']): i for i, u in enumerate(units)}
    entering = {}
    for d in range(2):
        for p in range(npair):
            st = st_ref[d, p]
            for step in range(cpb):
                j = step if d == 0 else cpb - 1 - step
                i = idx[(d, j, p)]
                entering[i] = st
                st = units[i]['dec'] * st + kv[i]
            st_ref[d, p] = st
    outs = [x + _mm_nt(u['qd'], bd(entering[i], low_s)) for i, (x, u) in enumerate(zip(intra, units))]
    for d, o_ref in enumerate((of_ref, ob_ref)):
        o_ref[0] = jnp.concatenate(
            [jnp.concatenate([outs[idx[(d, j, p)]] for p in range(npair)], axis=1) for j in range(cpb)], axis=0)

    @pl.when(c == nb - 1)
    def _():
        sfin_ref[0] = st_ref[...]


def _ret_tables(decay_logit, n):
    lg = jax.nn.log_sigmoid(decay_logit.astype(F32))
    pos = jnp.arange(n, dtype=F32)
    lag = pos[:, None] - pos[None, :]
    lag = jnp.stack([lag, -lag])
    dmat = jnp.where(lag[:, None] >= 0, jnp.exp(lg[:, :, None, None] * lag[:, None]), 0.0)
    lanes = jnp.repeat(lg, RET_DK, axis=-1)[:, None, :]
    qpow = jnp.stack([pos + 1.0, n - pos])[:, :, None]
    kpow = jnp.stack([n - 1.0 - pos, pos])[:, :, None]
    return dmat.astype(BF16), jnp.exp(lanes * qpow), jnp.exp(lanes * kpow), jnp.exp(lanes * n)


def _gla_scan_kernel(nb, cpb, qf_ref, kf_ref, vf_ref, af_ref, qb_ref, kb_ref, vb_ref, ab_ref,
                     aup_ref, abias_ref, s0_ref, of_ref, ob_ref, sfin_ref, st_ref):
    c = pl.program_id(1)

    @pl.when(c == 0)
    def _():
        st_ref[...] = s0_ref[0]

    t = GLA_T
    n = cpb * t
    dk = GLA_DK
    scale = dk ** -0.5
    dirs = ((qf_ref, kf_ref, vf_ref, af_ref), (qb_ref, kb_ref, vb_ref, ab_ref))
    units = []
    for d, (q_ref, k_ref, v_ref, a_ref) in enumerate(dirs):
        incl = _tri_incl(t, d == 1)
        q, k, v = q_ref[0], k_ref[0] * scale, v_ref[0]
        lw = -_softplus(-(_mm(a_ref[0], aup_ref[d]) + abias_ref[d])) * (1.0 / GLA_TAU)
        cin = _mm_left01(_chunk_tri01(n, t, d == 1), lw)
        qe = q * jnp.exp(cin)
        for j in range(cpb):
            rows = slice(j * t, (j + 1) * t)
            last = j * t + (0 if d == 1 else t - 1)
            mid = j * t + (t // 2 if d == 1 else t // 2 - 1)
            cj = cin[rows]
            clast = cin[last:last + 1]
            cmid = cin[mid:mid + 1]
            qt = q[rows] * jnp.exp(cj - cmid)
            kt = k[rows] * jnp.exp(cmid - cj)
            kh = k[rows] * jnp.exp(clast - cj)
            units.append(dict(d=d, j=j, incl=incl, qt=qt, kt=kt, qe=qe[rows], kh=kh, v=v[rows],
                              dec=jnp.exp(clast)))

    klane = lax.broadcasted_iota(jnp.int32, (t, LANES), 1) // dk
    vlow = lax.broadcasted_iota(jnp.int32, (t, LANES), 1) < HEAD_V
    khead = lax.broadcasted_iota(jnp.int32, (HEAD_V, LANES), 1) // dk
    npair = HEADS // 2

    def v_blockdiag(x):
        return jnp.concatenate([jnp.where(vlow, x, 0.0), jnp.where(vlow, 0.0, x)], axis=0)

    def state_rows(st, p):
        return jnp.concatenate([jnp.where(khead == 2 * p, st, 0.0), jnp.where(khead == 2 * p + 1, st, 0.0)],
                               axis=0)

    stacked = [_mm_nt(jnp.concatenate([jnp.where(klane == h, u['qt'], 0.0) for h in range(HEADS)], axis=0),
                      u['kt']) for u in units]
    amat = [[jnp.where(u['incl'], s4[h * t:(h + 1) * t], 0.0) for h in range(HEADS)]
            for s4, u in zip(stacked, units)]
    intra = [[_mm(jnp.concatenate(a[2 * p:2 * p + 2], axis=1), v_blockdiag(u['v'][:, p * LANES:(p + 1) * LANES]))
              for p in range(npair)] for a, u in zip(amat, units)]
    kv = []
    for u in units:
        full = _mm_tn(u['v'], u['kh'])
        blocks = [full[h * HEAD_V:(h + 1) * HEAD_V] for h in range(HEADS)]
        acc = blocks[HEADS - 1]
        for h in range(HEADS - 2, -1, -1):
            acc = jnp.where(khead == h, blocks[h], acc)
        kv.append(acc)
    idx = {(u['d'], u['j']): i for i, u in enumerate(units)}
    entering = {}
    for d in range(2):
        st = st_ref[d]
        for step in range(cpb):
            j = step if d == 0 else cpb - 1 - step
            i = idx[(d, j)]
            entering[i] = st
            st = units[i]['dec'] * st + kv[i]
        st_ref[d] = st
    inter = [_mm_nt(u['qe'], jnp.concatenate([state_rows(entering[i], p) for p in range(npair)], axis=0))
             for i, u in enumerate(units)]
    outs = [[x[p] + y[:, p * LANES:(p + 1) * LANES] for p in range(npair)] for x, y in zip(intra, inter)]
    for d, o_ref in enumerate((of_ref, ob_ref)):
        o_ref[0] = jnp.concatenate(
            [jnp.concatenate(outs[idx[(d, j)]], axis=1) for j in range(cpb)], axis=0)

    @pl.when(c == nb - 1)
    def _():
        sfin_ref[0] = st_ref[...]


def _rope_tables(rows):
    nf = ROPE_NF
    inv = ROPE_BASE ** (-jnp.arange(nf, dtype=F32) / nf)
    lane = jnp.arange(GROUP_W)
    freq = inv[lane % nf]
    sign = jnp.where((lane & ROPE_NF) == 0, -1.0, 1.0)
    ar = jnp.arange(rows, dtype=F32)[:, None] * freq[None, :]
    ac = jnp.arange(GRID_W, dtype=F32)[:, None] * freq[None, :]
    return (jnp.cos(ar).reshape(rows, 1, GROUP_W), (jnp.sin(ar) * sign).reshape(rows, 1, GROUP_W),
            jnp.cos(ac), jnp.sin(ac) * sign)


def _ret_scan(z, tables, s0):
    b, l, _ = z.shape
    n = min(RET_BLOCK, l)
    nb = l // n
    fw = lambda j: pl.BlockSpec((1, n, GROUP_W), lambda i, c: (i, c, j))
    bw = lambda j: pl.BlockSpec((1, n, GROUP_W), lambda i, c: (i, nb - 1 - c, j))
    full = lambda a: pl.BlockSpec(a.shape, lambda i, c: (0,) * a.ndim)
    st = pl.BlockSpec((1, 2, HEADS // 2, HEAD_V, LANES), lambda i, c: (i, 0, 0, 0, 0))
    return pl.pallas_call(
        functools.partial(_ret_scan_kernel, nb, n // RET_T),
        grid=(b, nb),
        in_specs=[fw(0), fw(1), fw(2), bw(0), bw(1), bw(2)] + [full(a) for a in tables] + [st],
        out_specs=[fw(0), bw(0), st],
        out_shape=[jax.ShapeDtypeStruct((b, l, GROUP_W), F32)] * 2
                  + [jax.ShapeDtypeStruct((b, 2, HEADS // 2, HEAD_V, LANES), F32)],
        scratch_shapes=[pltpu.VMEM((2, HEADS // 2, HEAD_V, LANES), F32)],
        compiler_params=_cparams("parallel", "arbitrary"),
        name="retention_scan",
    )(z, z, z, z, z, z, *tables, s0)


def _gla_scan(z, aup, abias, s0):
    b, l, _ = z.shape
    cpb = min(GLA_BLOCK, l) // CHUNK
    nb = l // (cpb * CHUNK)
    blk = lambda w, j, rev: pl.BlockSpec(
        (1, cpb * CHUNK, w), (lambda i, c: (i, nb - 1 - c, j)) if rev else (lambda i, c: (i, c, j)))
    st = pl.BlockSpec((1, 2, HEAD_V, GLA_QK), lambda i, c: (i, 0, 0, 0))
    ofw = pl.BlockSpec((1, cpb * CHUNK, GROUP_W), lambda i, c: (i, c, 0))
    obw = pl.BlockSpec((1, cpb * CHUNK, GROUP_W), lambda i, c: (i, nb - 1 - c, 0))
    return pl.pallas_call(
        functools.partial(_gla_scan_kernel, nb, cpb * CHUNK // GLA_T),
        grid=(b, nb),
        in_specs=[blk(GLA_QK, 0, False), blk(GLA_QK, 1, False), blk(GROUP_W, 1, False), blk(128, 6, False),
                  blk(GLA_QK, 0, True), blk(GLA_QK, 1, True), blk(GROUP_W, 1, True), blk(128, 6, True),
                  pl.BlockSpec(aup.shape, lambda i, c: (0, 0, 0)),
                  pl.BlockSpec(abias.shape, lambda i, c: (0, 0, 0)), st],
        out_specs=[ofw, obw, st],
        out_shape=[jax.ShapeDtypeStruct((b, l, GROUP_W), F32)] * 2
                  + [jax.ShapeDtypeStruct((b, 2, HEAD_V, GLA_QK), F32)],
        scratch_shapes=[pltpu.VMEM((2, HEAD_V, GLA_QK), F32)],
        compiler_params=_cparams("parallel", "arbitrary"),
        name="gla_scan",
    )(z, z, z, z, z, z, z, z, aup, abias, s0)


def _mix_kernel(x_ref, g1_ref, y5_ref, u5_ref, d5_ref, gw_ref, gb_ref,
                ryf_ref, ryb_ref, rbonus_ref, rg_ref, rlng_ref, rlnb_ref,
                tof_ref, tob_ref, tg_ref, tln_ref,
                gof_ref, gob_ref, gg_ref, gln_ref,
                bones_ref, wo_ref, o_ref, y5t_ref):
    bones = bones_ref[...]
    inv = 1.0 / HEAD_V

    def hmean(a):
        return _mm_right01(a, bones) * inv

    nch = y5_ref.shape[2]
    gph = LANES // S5_P
    y5g = [y5_ref[0, g] for g in range(S5_G)]
    pitch = S5_T + 1
    for s in range(S5_T):
        for hf in range(GROUP_W // LANES):
            y5t_ref[hf, pl.ds(s, nch, stride=pitch), :] = jnp.concatenate(
                [yg[:, s * S5_P:(s + 1) * S5_P] for yg in y5g[hf * gph:(hf + 1) * gph]], axis=1)
    y5t = jnp.concatenate(
        [jnp.concatenate([y5t_ref[hf, ch * pitch:ch * pitch + S5_T] for ch in range(nch)], axis=0)
         for hf in range(GROUP_W // LANES)], axis=1)
    y = y5t + d5_ref[...] * u5_ref[0]
    y = _gelu_tanh(y)
    ya = y * _sigmoid(_mm(y, gw_ref[...]) + gb_ref[...])
    yr = ryf_ref[0] + ryb_ref[0]
    dlt = yr - hmean(yr)
    yn = dlt * lax.rsqrt(hmean(dlt * dlt) + GN_EPS)
    yb = (yn * rlng_ref[...] + rlnb_ref[...] + rbonus_ref[0].astype(F32)) * rg_ref[0].astype(F32)
    ot = tof_ref[0] + tob_ref[0]
    yc = ot * lax.rsqrt(hmean(ot * ot) + EPS) * tln_ref[...] * _silu(tg_ref[0])
    og = gof_ref[0] + gob_ref[0]
    yd = og * lax.rsqrt(hmean(og * og) + EPS) * gln_ref[...] * _silu(gg_ref[0])
    mix = (_mm(ya, wo_ref[0]) + _mm(yb, wo_ref[1]) + _mm(yc, wo_ref[2]) + _mm(yd, wo_ref[3]))
    o_ref[0] = x_ref[0] + g1_ref[0] * mix


def _mix(x, g1, y5, u5, rw_pre, rw_y, zret, ret_o, zgla, gla_o, p, tm):
    b, l, d = x.shape
    tok = lambda w, j: pl.BlockSpec((1, tm, w), lambda i, t: (i, t, j))
    vec = lambda a: pl.BlockSpec(a.shape, lambda i, t: (0,) * a.ndim)
    t256 = tok(GROUP_W, 0)
    args = [x, g1, y5, u5, p['s5_d'], p['glu_w'], p['glu_b'],
            rw_y[0], rw_y[1], rw_pre[4], rw_pre[3], p['rw_ln_g'], p['rw_ln_b'],
            ret_o[0], ret_o[1], zret, p['ret_ln_g'],
            gla_o[0], gla_o[1], zgla, p['gla_ln_g'],
            p['bones'], p['w_out']]
    grouped = pl.BlockSpec((1, S5_G, tm // S5_T, S5_TP), lambda i, t: (i, 0, t, 0))
    specs = [tok(d, 0), pl.BlockSpec((1, 1, d), lambda i, t: (i, 0, 0)), grouped, t256,
             vec(p['s5_d']), vec(p['glu_w']), vec(p['glu_b']),
             t256, t256, t256, t256, vec(p['rw_ln_g']), vec(p['rw_ln_b']),
             t256, t256, tok(GROUP_W, 3), vec(p['ret_ln_g']),
             t256, t256, tok(GROUP_W, 2), vec(p['gla_ln_g']),
             vec(p['bones']), vec(p['w_out'])]
    return pl.pallas_call(
        _mix_kernel,
        grid=(b, l // tm),
        in_specs=specs,
        out_specs=tok(d, 0),
        out_shape=jax.ShapeDtypeStruct((b, l, d), F32),
        scratch_shapes=[pltpu.VMEM((GROUP_W // LANES, tm // S5_T * (S5_T + 1), LANES), F32)],
        compiler_params=_cparams("parallel", "parallel"),
        name="mix_outproj",
    )(*args)


def _mlp_kernel(final, nff, x_ref, g_ref, sc_ref, sh_ref, gate_ref, w1_ref, w2_ref, fg_ref, o_ref):
    x = x_ref[0]
    hb = _modnorm(x, g_ref[...], sc_ref[0], sh_ref[0]).astype(BF16)
    ff = w1_ref.shape[2] // nff
    acc = None
    for j in range(nff):
        a = jnp.maximum(jnp.dot(hb, w1_ref[0, :, j * ff:(j + 1) * ff], preferred_element_type=F32), 0.0)
        part = jnp.dot((a * a).astype(BF16), w2_ref[0, j * ff:(j + 1) * ff, :], preferred_element_type=F32)
        acc = part if acc is None else acc + part
    y = x + gate_ref[0] * acc
    if final:
        ms = jnp.mean(y * y, axis=-1, keepdims=True)
        y = y * lax.rsqrt(ms + EPS) * fg_ref[...]
    o_ref[0] = y


def _mlp(x, g, sc, sh, gate, w1, w2, layer, final_g, final, tm):
    b, l, d = x.shape
    tok = pl.BlockSpec((1, tm, d), lambda i, t: (i, t, 0))
    vec = pl.BlockSpec((1, 1, d), lambda i, t: (i, 0, 0))
    row = pl.BlockSpec((1, d), lambda i, t: (0, 0))
    once = lambda a: pl.BlockSpec((1,) + a.shape[1:], lambda i, t: (layer, 0, 0), pipeline_mode=pl.Buffered(1))
    return pl.pallas_call(
        functools.partial(_mlp_kernel, final, 4),
        grid=(b, l // tm),
        in_specs=[tok, row, vec, vec, vec, once(w1), once(w2), row],
        out_specs=tok,
        out_shape=jax.ShapeDtypeStruct((b, l, d), F32),
        compiler_params=_cparams("parallel", "parallel"),
        name="mlp",
    )(x, g, sc, sh, gate, w1, w2, final_g)


def kernel(x, c, ctx, c_ctx, ada_w, ada_b, norm1_g, norm2_g, w_in, w_out, s5_lam_re, s5_lam_im, s5_log_dt, s5_b_re, s5_b_im, s5_c_re, s5_c_im, s5_d, s5_glu_w, s5_glu_b, rw_mu, rw_w0, rw_w_up, rw_a0, rw_a_up, rw_g_up, rw_k_k, rw_k_a, rw_r_k, rw_ln_g, rw_ln_b, ret_decay_logit, ret_ln_g, gla_a_up, gla_a_b, gla_ln_g, mlp_w1, mlp_w2, final_g):
    b, l, d = x.shape
    lc = ctx.shape[1]
    depth = ada_w.shape[0]
    rows = l // GRID_W
    assert l % TOKEN_TILE == 0 and lc % RET_T == 0 and lc <= TOKEN_TILE and d % LANES == 0
    assert b < COND_ROWS and l % GRID_W == 0

    cond = jnp.zeros((COND_ROWS, d), F32).at[:b].set(c).at[b].set(c_ctx)
    mod = _modulation(cond, ada_w, ada_b)

    lane = jnp.arange(GROUP_W)
    bones = (lane[:, None] // HEAD_V == lane[None, :] // HEAD_V).astype(BF16)
    rope_lat = _rope_tables(rows)
    ident = (jnp.ones((lc // CHUNK, 1, GROUP_W), F32), jnp.zeros((lc // CHUNK, 1, GROUP_W), F32),
             jnp.ones((GRID_W, GROUP_W), F32), jnp.zeros((GRID_W, GROUP_W), F32))
    nlev = max(1, (l // S5_T - 1).bit_length())
    c0, c1, c2 = GROUP_W, GROUP_W + RW_COLS, GROUP_W + RW_COLS + 4 * GROUP_W
    row2 = lambda a: a.reshape(1, -1).astype(F32)
    w_in_b, w_out_b, w1_b, w2_b = _to_bf16(w_in), _to_bf16(w_out), _to_bf16(mlp_w1), _to_bf16(mlp_w2)

    s5_tab_all = jax.vmap(lambda *a: _s5_tables(*a, nlev))(
        s5_lam_re, s5_lam_im, s5_log_dt, s5_b_re, s5_b_im, s5_c_re, s5_c_im)
    ret_tab_all = jax.vmap(lambda a: _ret_tables(a, RET_T))(ret_decay_logit)
    zpad = lambda a, lo, n: jnp.zeros((depth, 2, LANES, n), F32).at[:, :, lo:lo + a.shape[2]].set(a).astype(BF16)
    rw_wup_all = zpad(rw_w_up, 0, GROUP_W)
    rw_aup_all = zpad(rw_a_up, RW_W_RANK, GROUP_W)
    rw_gup_all = jnp.zeros((depth, LANES, GROUP_W), F32).at[:, RW_W_RANK + RW_A_RANK:].set(rw_g_up).astype(BF16)
    gla_aup_all = zpad(gla_a_up, 0, GLA_QK)

    xc = ctx
    for i in range(depth):
        last = i == depth - 1
        m = mod[i].reshape(COND_ROWS, N_ADA, d)
        ml = m[:b, :, None, :]
        mc = jnp.broadcast_to(m[b][None, :, None, :], (b, N_ADA, 1, d))
        wi = w_in_b[i]
        w5, wr, wt = wi[:, :c0], wi[:, c0:c1], wi[:, c1:c2]
        wg = jnp.pad(wi[:, c2:], ((0, 0), (0, GLA_COLS_PAD - (wi.shape[1] - c2))))
        n1 = row2(norm1_g[i])
        n2 = row2(norm2_g[i])

        s5_tab = tuple(a[i] for a in s5_tab_all)
        rwp = dict(
            mu=row2(rw_mu[i]),
            w0=rw_w0[i].reshape(2, 1, GROUP_W), a0=rw_a0[i].reshape(2, 1, GROUP_W),
            wup=rw_wup_all[i], aup=rw_aup_all[i], gup=rw_gup_all[i],
            kk=row2(rw_k_k[i]), ka=row2(rw_k_a[i]), rk=row2(rw_r_k[i]), bones=bones)
        ret_tab = tuple(a[i] for a in ret_tab_all)
        gla_aup = gla_aup_all[i]
        gla_ab = gla_a_b[i].reshape(2, 1, GLA_QK).astype(F32)
        mixp = dict(s5_d=row2(s5_d[i]), glu_w=s5_glu_w[i].astype(BF16), glu_b=row2(s5_glu_b[i]),
                    rw_ln_g=row2(rw_ln_g[i]), rw_ln_b=row2(rw_ln_b[i]), ret_ln_g=row2(ret_ln_g[i]),
                    gla_ln_g=row2(gla_ln_g[i]), bones=bones,
                    w_out=w_out_b[i].reshape(4, GROUP_W, d))

        def mixers(xx, mm, is_lat, states):
            tm = TOKEN_TILE if is_lat else lc
            z5, z5g, zt, zg, pre = _inproj(xx, n1, mm[:, 1], mm[:, 0], w5, wr, wt, wg,
                                           rope_lat if is_lat else ident, rwp, is_lat, tm)
            y5, h5 = _s5_scan(z5g, s5_tab, states[0])
            yrf, yrb, srw = _rw_scan(pre, states[1])
            otf, otb, sret = _ret_scan(zt, ret_tab, states[2])
            ogf, ogb, sgla = _gla_scan(zg, gla_aup, gla_ab, states[3])
            outs = (z5, y5, pre, (yrf, yrb), zt, (otf, otb), zg, (ogf, ogb))
            return outs, (h5, srw, sret, sgla)

        def block(xx, mm, outs, is_lat, fin):
            tm = TOKEN_TILE if is_lat else lc
            z5, y5, pre, yr, zt, ot, zg, og = outs
            x1 = _mix(xx, mm[:, 2], y5, z5, pre, yr, zt, ot, zg, og, mixp, tm)
            return _mlp(x1, n2, mm[:, 4], mm[:, 3], mm[:, 5], w1_b, w2_b, i, row2(final_g), fin, tm)

        zeros = (jnp.zeros((b, 2, S5_G, 1, 2 * S5_N), F32),
                 jnp.zeros((b, 2, HEADS // 2, HEAD_V, LANES), F32),
                 jnp.zeros((b, 2, HEADS // 2, HEAD_V, LANES), F32),
                 jnp.zeros((b, 2, HEAD_V, GLA_QK), F32))
        outs_c, st_c = mixers(xc, mc, False, zeros)
        outs_l, _ = mixers(x, ml, True, st_c)
        x = block(x, ml, outs_l, True, last)
        if not last:
            xc = block(xc, mc, outs_c, False, False)
    return x
```

```python
import functools
import math

import jax
import jax.numpy as jnp
from jax import lax
from jax.experimental import pallas as pl
from jax.experimental.pallas import tpu as pltpu

F32 = jnp.float32
BF16 = jnp.bfloat16

LANES = 128
GRID_W = 64
GROUP_W = 256
N_ADA = 6
EPS = 1e-6
GN_EPS = 64e-5
TOKEN_TILE = 512
MLP_TILE = 1024
CHUNK = 64
RW_CPB = 4
RET_T = 256
RET_BLOCK = 512
GLA_T = 128
GLA_BLOCK = 512
HEADS = 4
HEAD_V = 64
S5_P = 16
S5_G = 16
S5_N = 64
S5_T = 32
S5_TP = S5_T * S5_P
RW_COLS = 896
RW_W_RANK = 32
RW_A_RANK = 32
RW_G_RANK = 64
RET_DK = 64
GLA_DK = 32
GLA_QK = 128
GLA_RANK = 16
GLA_TAU = 16.0
GLA_COLS_PAD = 896
ROPE_BASE = 10000.0
ROPE_HALF = RET_DK // 2
ROPE_NF = RET_DK // 4
COND_ROWS = 8
CAST_ROWS = 256
ADA_TILE = 1536
VMEM_LIMIT = 56 * 1024 * 1024


def _cparams(*sem):
    return pltpu.CompilerParams(dimension_semantics=sem, vmem_limit_bytes=VMEM_LIMIT)


def _mm(a, b):
    return jnp.dot(a.astype(BF16), b.astype(BF16), preferred_element_type=F32)


def _mm_nt(a, b):
    return lax.dot_general(a.astype(BF16), b.astype(BF16), (((1,), (1,)), ((), ())),
                           preferred_element_type=F32)


def _mm_tn(a, b):
    return lax.dot_general(a.astype(BF16), b.astype(BF16), (((0,), (0,)), ((), ())),
                           preferred_element_type=F32)


def _split2(x):
    hi = x.astype(BF16)
    return hi, (x - hi.astype(F32)).astype(BF16)


def _chunk_tri01(n, t, reverse):
    ri = lax.broadcasted_iota(jnp.int32, (n, n), 0)
    ci = lax.broadcasted_iota(jnp.int32, (n, n), 1)
    shift = t.bit_length() - 1
    tri = (ci >= ri) if reverse else (ci <= ri)
    return jnp.where((ri >> shift) == (ci >> shift), jnp.where(tri, 1.0, 0.0), 0.0).astype(BF16)


def _mm_left01(m01, x):
    hi, lo = _split2(x)
    return jnp.dot(m01, hi, preferred_element_type=F32) + jnp.dot(m01, lo, preferred_element_type=F32)


def _mm_right01(x, m01):
    return jnp.dot(x.astype(BF16), m01, preferred_element_type=F32)


def _sigmoid(x):
    return 1.0 / (1.0 + jnp.exp(-x))


def _softplus(x):
    return jnp.maximum(x, 0.0) + jnp.log(1.0 + jnp.exp(-jnp.abs(x)))


def _silu(x):
    return x * _sigmoid(x)


def _gelu_tanh(x):
    return 0.5 * x * (1.0 + jnp.tanh(math.sqrt(2.0 / math.pi) * (x + 0.044715 * x * x * x)))


def _tri_incl(t, reverse):
    ri = lax.broadcasted_iota(jnp.int32, (t, t), 0)
    ci = lax.broadcasted_iota(jnp.int32, (t, t), 1)
    return (ci >= ri) if reverse else (ci <= ri)


def _cast_kernel(x_ref, o_ref):
    o_ref[...] = x_ref[...].astype(o_ref.dtype)


def _to_bf16(w):
    depth, r, c = w.shape
    tr = CAST_ROWS
    spec = pl.BlockSpec((1, tr, c), lambda i, j: (i, j, 0))
    return pl.pallas_call(
        _cast_kernel,
        grid=(depth, r // tr),
        in_specs=[spec],
        out_specs=spec,
        out_shape=jax.ShapeDtypeStruct(w.shape, BF16),
        compiler_params=_cparams("parallel", "parallel"),
        name="cast_bf16",
    )(w)


def _mod_kernel(cond_ref, w_ref, b_ref, o_ref):
    c = cond_ref[...]
    o_ref[0] = _mm(_silu(c), w_ref[0]) + b_ref[0]


def _modulation(cond, ada_w, ada_b):
    depth, d, n = ada_w.shape
    tn = ADA_TILE
    return pl.pallas_call(
        _mod_kernel,
        grid=(depth, n // tn),
        in_specs=[pl.BlockSpec((COND_ROWS, d), lambda i, j: (0, 0)),
                  pl.BlockSpec((1, d, tn), lambda i, j: (i, 0, j)),
                  pl.BlockSpec((1, 1, tn), lambda i, j: (i, 0, j))],
        out_specs=pl.BlockSpec((1, COND_ROWS, tn), lambda i, j: (i, 0, j)),
        out_shape=jax.ShapeDtypeStruct((depth, COND_ROWS, n), F32),
        compiler_params=_cparams("parallel", "parallel"),
        name="adaln_mod",
    )(cond, ada_w, ada_b.reshape(depth, 1, n))


def _modnorm(x, g, sc, sh):
    ms = jnp.mean(x * x, axis=-1, keepdims=True)
    return x * lax.rsqrt(ms + EPS) * g * (1.0 + sc) + sh


def _shift_columns(z, segments):
    tm, width = z.shape
    lane = lax.broadcasted_iota(jnp.int32, (tm, LANES), 1)
    starts = [s for s, _ in segments] + [width]
    cols = []
    for c in range(width // LANES):
        lo, hi = c * LANES, (c + 1) * LANES
        inside = [(max(starts[i], lo), segments[i][1]) for i in range(len(segments))
                  if starts[i] < hi and starts[i + 1] > lo]
        col = inside[-1][1](c)
        for first, fn in reversed(inside[:-1]):
            nxt = [f for f, _ in inside if f > first][0]
            col = jnp.where(lane < nxt - lo, fn(c), col)
        cols.append(col)
    return jnp.concatenate(cols, axis=1)


def _rw_prep_math(grid_shift, z, zp, zn, mu, w0_ref, wup_ref, a0_ref, aup_ref, gup, kkp, ka, rk, bones):
    tm = z.shape[0]
    row = lax.broadcasted_iota(jnp.int32, (tm, LANES), 0)
    col = lambda x, c: x[:, c * LANES:(c + 1) * LANES]
    if grid_shift:
        pos = row & (GRID_W - 1)
        left = lambda c: jnp.where(pos == 0, 0.0, pltpu.roll(col(z, c), 1, 0))
        right = lambda c: jnp.where(pos == GRID_W - 1, 0.0, pltpu.roll(col(z, c), tm - 1, 0))
        up = lambda c: jnp.concatenate([col(zp, c), col(z, c)[:tm - GRID_W]], axis=0)
        down = lambda c: jnp.concatenate([col(z, c)[GRID_W:], col(zn, c)], axis=0)
        q = RW_COLS // 4
        shifted = _shift_columns(z, [(0, left), (q, right), (2 * q, up), (3 * q, down)])
    else:
        prev = lambda c: jnp.where(row == 0, 0.0, pltpu.roll(col(z, c), 1, 0))
        nxt = lambda c: jnp.where(row == tm - 1, 0.0, pltpu.roll(col(z, c), tm - 1, 0))
        shifted = _shift_columns(z, [(0, prev), (RW_COLS // 2, nxt)])
    zm = z + mu * (shifted - z)
    r = zm[:, 0:GROUP_W]
    k = zm[:, GROUP_W:2 * GROUP_W]
    v = zm[:, 2 * GROUP_W:3 * GROUP_W]
    lo = zm[:, 3 * GROUP_W:RW_COLS]
    g = _mm(_sigmoid(lo), gup)
    kk = k * kkp
    kk = kk * lax.rsqrt(_mm_right01(kk * kk, bones) + 1e-12)
    bonus = _mm_right01(r * k * rk, bones) * v
    outs = [r.astype(BF16), v.astype(BF16), kk.astype(BF16), g.astype(BF16), bonus.astype(BF16)]
    tlo = jnp.tanh(lo)
    for d in range(2):
        log_decay = -math.exp(-0.5) * _sigmoid(w0_ref[d] + _mm(tlo, wup_ref[d]))
        a = _sigmoid(a0_ref[d] + _mm(lo, aup_ref[d]))
        outs += [log_decay,
                 (k * (1.0 + (a - 1.0) * ka)).astype(BF16), a.astype(BF16)]
    return outs


def _inproj_kernel(grid_shift, nt, x_ref, xp_ref, xn_ref, g_ref, sc_ref, sh_ref,
                   w5_ref, wr_ref, wt_ref, wg_ref, cr_ref, sr_ref, cc_ref, sn_ref,
                   mu_ref, w0_ref, wup_ref, a0_ref, aup_ref, gup_ref, kk_ref, ka_ref, rk_ref, bones_ref,
                   o5_ref, o5g_ref, ot_ref, og_ref, *rest):
    rw_refs, z5h_ref = rest[:-1], rest[-1]
    j = pl.program_id(1)
    norm = lambda x: _modnorm(x, g_ref[...], sc_ref[0], sh_ref[0]).astype(BF16)
    hb = norm(x_ref[0])
    z5 = jnp.dot(hb, w5_ref[...], preferred_element_type=F32)
    o5_ref[0] = z5
    og_ref[0] = jnp.dot(hb, wg_ref[...], preferred_element_type=F32)
    zr = jnp.dot(hb, wr_ref[...], preferred_element_type=F32)
    zp = zn = None
    if grid_shift:
        zp = jnp.where(j > 0, jnp.dot(norm(xp_ref[0]), wr_ref[...], preferred_element_type=F32), 0.0)
        zn = jnp.where(j < nt - 1, jnp.dot(norm(xn_ref[0]), wr_ref[...], preferred_element_type=F32), 0.0)
    rw = _rw_prep_math(grid_shift, zr, zp, zn, mu_ref[...], w0_ref, wup_ref, a0_ref, aup_ref, gup_ref[...],
                       kk_ref[...], ka_ref[...], rk_ref[...], bones_ref[...])
    for ref, val in zip(rw_refs, rw):
        ref[0] = val
    zt = jnp.dot(hb, wt_ref[...], preferred_element_type=F32)
    lane = lax.broadcasted_iota(jnp.int32, (GRID_W, GROUP_W), 1)
    by_row = (lane & ROPE_HALF) == 0
    nrow = cr_ref.shape[0]
    cos = jnp.concatenate([jnp.where(by_row, cr_ref[j], cc_ref[...]) for j in range(nrow)], axis=0)
    sin = jnp.concatenate([jnp.where(by_row, sr_ref[j], sn_ref[...]) for j in range(nrow)], axis=0)
    ot_ref[0, :, 0:GROUP_W] = _rope(zt[:, 0:GROUP_W], cos, sin)
    ot_ref[0, :, GROUP_W:2 * GROUP_W] = _rope(zt[:, GROUP_W:2 * GROUP_W] * RET_DK ** -0.5, cos, sin)
    ot_ref[0, :, 2 * GROUP_W:] = zt[:, 2 * GROUP_W:]
    nch = o5g_ref.shape[2]
    gph = LANES // S5_P
    for hf in range(GROUP_W // LANES):
        z5h_ref[hf] = z5[:, hf * LANES:(hf + 1) * LANES]
    for s in range(S5_T):
        for hf in range(GROUP_W // LANES):
            rows = z5h_ref[hf, pl.ds(s, nch, stride=S5_T), :]
            for g in range(gph):
                o5g_ref[0, hf * gph + g, :, s * S5_P:(s + 1) * S5_P] = rows[:, g * S5_P:(g + 1) * S5_P]


def _inproj(x, g, sc, sh, w5, wr, wt, wg, rope, rwp, grid_shift, tm):
    b, l, d = x.shape
    nt = l // tm
    hb = tm // GRID_W
    nhb = l // GRID_W
    tok = lambda n: pl.BlockSpec((1, tm, n), lambda i, j: (i, j, 0))
    vec = pl.BlockSpec((1, 1, d), lambda i, j: (i, 0, 0))
    full = lambda a: pl.BlockSpec(a.shape, lambda i, j: (0,) * a.ndim)
    grouped = pl.BlockSpec((1, S5_G, tm // S5_T, S5_TP), lambda i, j: (i, 0, j, 0))
    rowt = pl.BlockSpec((hb, 1, GROUP_W), lambda i, j: (j, 0, 0))
    above = pl.BlockSpec((1, GRID_W, d), lambda i, j: (i, jnp.maximum(j * hb - 1, 0), 0))
    below = pl.BlockSpec((1, GRID_W, d), lambda i, j: (i, jnp.minimum((j + 1) * hb, nhb - 1), 0))
    cr, sr, cc, sn = rope
    params = (rwp['mu'], rwp['w0'], rwp['wup'], rwp['a0'], rwp['aup'], rwp['gup'], rwp['kk'], rwp['ka'],
              rwp['rk'], rwp['bones'])
    rw_dtypes = (BF16,) * 5 + (F32, BF16, BF16) * 2
    outs = pl.pallas_call(
        functools.partial(_inproj_kernel, grid_shift, nt),
        grid=(b, nt),
        in_specs=[tok(d), above, below, pl.BlockSpec((1, d), lambda i, j: (0, 0)), vec, vec,
                  full(w5), full(wr), full(wt), full(wg), rowt, rowt, full(cc), full(sn)]
                 + [full(a) for a in params],
        out_specs=[tok(GROUP_W), grouped, tok(wt.shape[1]), tok(wg.shape[1])] + [tok(GROUP_W)] * 11,
        out_shape=[jax.ShapeDtypeStruct((b, l, GROUP_W), F32),
                   jax.ShapeDtypeStruct((b, S5_G, l // S5_T, S5_TP), F32),
                   jax.ShapeDtypeStruct((b, l, wt.shape[1]), F32),
                   jax.ShapeDtypeStruct((b, l, wg.shape[1]), F32)]
                  + [jax.ShapeDtypeStruct((b, l, GROUP_W), dt) for dt in rw_dtypes],
        scratch_shapes=[pltpu.VMEM((GROUP_W // LANES, tm, LANES), F32)],
        compiler_params=_cparams("parallel", "parallel"),
        name="norm_inproj",
    )(x, x, x, g, sc, sh, w5, wr, wt, wg, cr, sr, cc, sn, *params)
    return outs[0], outs[1], outs[2], outs[3], tuple(outs[4:])


def _s5_tables(lam_re, lam_im, log_dt, b_re, b_im, c_re, c_im, nlev):
    hp = lax.Precision.HIGHEST
    t = S5_T
    lam = lax.complex(jnp.minimum(lam_re.astype(F32), -1e-4), lam_im.astype(F32))
    ldt = lam * jnp.exp(log_dt.astype(F32))[..., None]
    a_bar = jnp.exp(ldt)
    bb = ((a_bar - 1.0) / lam)[..., None] * lax.complex(b_re.astype(F32), b_im.astype(F32))
    cm = lax.complex(c_re.astype(F32), c_im.astype(F32))
    tau = jnp.arange(t + 1, dtype=F32)
    apow = jnp.exp(ldt[:, :, None, :] * tau[None, None, :, None])
    taps = jnp.einsum('dgpn,dgtn,dgnq->dgtpq', cm, apow[:, :, :t], bb, precision=hp).real
    taprow = jnp.stack([taps[0], taps[1][:, ::-1]]).transpose(0, 1, 4, 2, 3).reshape(2, S5_G, S5_P, S5_TP)

    def pack(zc):
        return jnp.concatenate([zc.real, zc.imag], axis=-1)

    win_f = apow[0][:, t - 1 - jnp.arange(t), None, :] * bb[0].transpose(0, 2, 1)[:, None]
    win_b = apow[1][:, jnp.arange(t), None, :] * bb[1].transpose(0, 2, 1)[:, None]
    win = jnp.stack([pack(win_f), pack(win_b)]).reshape(2, S5_G, S5_TP, 2 * S5_N)
    ca_f = cm[0][:, None] * apow[0][:, 1 + jnp.arange(t), None, :]
    ca_b = cm[1][:, None] * apow[1][:, t - jnp.arange(t), None, :]

    def outpack(ca):
        w = jnp.concatenate([ca.real, -ca.imag], axis=-1)
        return w.reshape(S5_G, S5_TP, 2 * S5_N).transpose(0, 2, 1)

    wout = jnp.stack([outpack(ca_f), outpack(ca_b)])
    lev = (2.0 ** jnp.arange(nlev, dtype=F32)) * t
    pw = jnp.exp(ldt[:, :, None, :] * lev[None, None, :, None])
    p1 = jnp.concatenate([pw.real, pw.real], axis=-1)
    p2 = jnp.concatenate([-pw.imag, pw.imag], axis=-1)
    pw = jnp.stack([p1, p2], axis=3)
    return taprow, win.astype(BF16), wout.astype(BF16), pw


def _s5_kernel(nc, nlev, u_ref, tap_ref, win_ref, wout_ref, pw_ref, h0_ref, y_ref, hfin_ref, conv_ref):
    u = u_ref[0, 0].astype(BF16)
    row = lax.broadcasted_iota(jnp.int32, (nc, 2 * S5_N), 0)
    lane = lax.broadcasted_iota(jnp.int32, (S5_P, S5_TP), 1)
    for s in range(S5_T):
        lo = s * S5_P
        fwd = tap_ref[0, 0] if s == 0 else jnp.where(lane >= lo, pltpu.roll(tap_ref[0, 0], lo, 1), 0.0)
        hi = lo + S5_P
        bwd = tap_ref[1, 0] if hi == S5_TP else jnp.where(lane < hi, pltpu.roll(tap_ref[1, 0], hi, 1), 0.0)
        conv_ref[0, lo:hi, :] = fwd.astype(BF16)
        conv_ref[1, lo:hi, :] = bwd.astype(BF16)

    def cmul(x, d, j):
        return pw_ref[d, 0, j, 0:1] * x + pw_ref[d, 0, j, 1:2] * pltpu.roll(x, S5_N, 1)

    v = [jnp.dot(u, win_ref[d, 0], preferred_element_type=F32) for d in range(2)]
    x = [jnp.where(row == 0, h0_ref[0, 0, 0], pltpu.roll(v[0], 1, 0)),
         jnp.where(row == nc - 1, h0_ref[0, 1, 0], pltpu.roll(v[1], nc - 1, 0))]
    for j in range(nlev):
        sh = 2 ** j
        xs = [jnp.where(row >= sh, pltpu.roll(x[0], sh, 0), 0.0),
              jnp.where(row < nc - sh, pltpu.roll(x[1], nc - sh, 0), 0.0)]
        x = [x[d] + cmul(xs[d], d, j) for d in range(2)]
    for d, last in enumerate((nc - 1, 0)):
        hfin_ref[0, d, 0] = cmul(x[d][last:last + 1], d, 0) + v[d][last:last + 1]
    y_ref[0, 0] = (jnp.dot(u, conv_ref[0], preferred_element_type=F32) + _mm(x[0], wout_ref[0, 0])
                   + jnp.dot(u, conv_ref[1], preferred_element_type=F32) + _mm(x[1], wout_ref[1, 0]))


def _s5_scan(uf, tables, h0):
    conv, win, wout, pw = tables
    b, _, nc, _ = uf.shape
    nlev = max(1, (nc - 1).bit_length())
    pw = pw[:, :, :nlev]
    n2 = 2 * S5_N
    y, hfin = pl.pallas_call(
        functools.partial(_s5_kernel, nc, nlev),
        grid=(b, S5_G),
        in_specs=[pl.BlockSpec((1, 1, nc, S5_TP), lambda i, g: (i, g, 0, 0)),
                  pl.BlockSpec((2, 1, S5_P, S5_TP), lambda i, g: (0, g, 0, 0)),
                  pl.BlockSpec((2, 1, S5_TP, n2), lambda i, g: (0, g, 0, 0)),
                  pl.BlockSpec((2, 1, n2, S5_TP), lambda i, g: (0, g, 0, 0)),
                  pl.BlockSpec((2, 1, nlev, 2, n2), lambda i, g: (0, g, 0, 0, 0)),
                  pl.BlockSpec((1, 2, 1, 1, n2), lambda i, g: (i, 0, g, 0, 0))],
        out_specs=[pl.BlockSpec((1, 1, nc, S5_TP), lambda i, g: (i, g, 0, 0)),
                   pl.BlockSpec((1, 2, 1, 1, n2), lambda i, g: (i, 0, g, 0, 0))],
        out_shape=[jax.ShapeDtypeStruct((b, S5_G, nc, S5_TP), F32),
                   jax.ShapeDtypeStruct((b, 2, S5_G, 1, n2), F32)],
        scratch_shapes=[pltpu.VMEM((2, S5_TP, S5_TP), BF16)],
        compiler_params=_cparams("parallel", "parallel"),
        name="s5_scan",
    )(uf, conv, win, wout, pw, h0)
    return y, hfin


def _rw_scan_kernel(nb, cpb, rf_ref, vf_ref, kkf_ref, lwf_ref, kdf_ref, asf_ref,
                    rb_ref, vb_ref, kkb_ref, lwb_ref, kdb_ref, asb_ref, s0_ref,
                    yf_ref, yb_ref, sfin_ref, st_ref):
    c = pl.program_id(1)

    @pl.when(c == 0)
    def _():
        st_ref[...] = s0_ref[0]

    t = CHUNK
    ri = lax.broadcasted_iota(jnp.int32, (t, LANES), 0)
    li = lax.broadcasted_iota(jnp.int32, (t, LANES), 1)
    ci = li & (t - 1)
    low = li < HEAD_V
    eyef = jnp.where(ri == ci, 1.0, 0.0)
    same = lambda s: jnp.where((ri >> s) == (ci >> s), 1.0, 0.0)
    m4, m8, m16, m32 = same(2), same(3), same(4), same(5)
    merge_masks = (m8 - m4, m16 - m8, m32 - m16, 1.0 - m32)

    def bd(x):
        return jnp.concatenate([jnp.where(low, x, 0.0), jnp.where(low, 0.0, x)], axis=0)

    def diag_blocks(full):
        return jnp.where(low, full[0:HEAD_V], full[HEAD_V:2 * HEAD_V])

    dir_refs = ((rf_ref, vf_ref, kkf_ref, lwf_ref, kdf_ref, asf_ref),
                (rb_ref, vb_ref, kkb_ref, lwb_ref, kdb_ref, asb_ref))
    units = []
    for d, (r_ref, v_ref, kk_ref, lw_ref, kd_ref, as_ref) in enumerate(dir_refs):
        incl = (ci >= ri) if d == 1 else (ci <= ri)
        strict = (ci > ri) if d == 1 else (ci < ri)
        r, v, kk, kd = (x[0].astype(F32) for x in (r_ref, v_ref, kk_ref, kd_ref))
        lw = lw_ref[0]
        cin = _mm_left01(_chunk_tri01(cpb * t, t, d == 1), lw)
        e_in = jnp.exp(cin)
        e_neg = jnp.exp(-cin)
        rt = r * e_in
        at = -kk * jnp.exp(cin - lw)
        bvec = kk * as_ref[0].astype(F32)
        bt = bvec * e_neg
        kt = kd * e_neg
        for j in range(cpb):
            rows = slice(j * t, (j + 1) * t)
            last = j * t + (0 if d == 1 else t - 1)
            clast = cin[last:last + 1]
            dl = jnp.exp(clast - cin[rows])
            bh = bvec[rows] * dl
            kh = kd[rows] * dl
            dec = jnp.exp(clast)
            for p in range(HEADS // 2):
                sl = slice(p * LANES, (p + 1) * LANES)
                units.append(dict(d=d, j=j, p=p, incl=incl, strict=strict,
                                  at=at[rows, sl], rt=rt[rows, sl], bt=bt[rows, sl], kt=kt[rows, sl],
                                  bh=bh[:, sl], kh=kh[:, sl], v=v[rows, sl], dec=dec[:, sl]))

    x1 = [jnp.concatenate([u['at'], u['rt']], axis=0) for u in units]
    scores = [_mm_nt(x, jnp.concatenate([bd(u['bt']), bd(u['kt'])], axis=0))
              for x, u in zip(x1, units)]
    nmat = [jnp.where(u['strict'], a[0:t, :LANES], 0.0) for u, a in zip(units, scores)]
    a_rb = [jnp.where(u['incl'], a[t:2 * t, :LANES], 0.0) for u, a in zip(units, scores)]
    a_kk = [jnp.concatenate([jnp.where(u['strict'], a[0:t, LANES:], 0.0),
                             jnp.where(u['incl'], a[t:2 * t, LANES:], 0.0)], axis=0)
            for u, a in zip(units, scores)]
    akv = [_mm(a, bd(u['v'])) for u, a in zip(units, a_kk)]
    kv = [diag_blocks(_mm_tn(u['v'], u['kh'])) for u in units]
    nd = [x * m4 for x in nmat]
    n2 = [_mm(x, bd(x)) for x in nd]
    tinv = [eyef + x + _mm(eyef + x, bd(y)) for x, y in zip(nd, n2)]
    for mk in merge_masks:
        w = [_mm(ti, bd(x * mk)) for ti, x in zip(tinv, nmat)]
        tinv = [ti + _mm(wi, bd(ti)) for ti, wi in zip(tinv, w)]
    zz = [_mm(ti, jnp.concatenate([bd(u['at']), bd(kvv[0:t])], axis=1))
          for ti, u, kvv in zip(tinv, units, akv)]
    ght = [_mm_tn(z, u['bh']) for u, z in zip(units, zz)]
    qy = [_mm(a, jnp.concatenate([bd(z[:, :LANES]), bd(z[:, LANES:])], axis=1))
          for a, z in zip(a_rb, zz)]
    qmat = [u['rt'] + x[:, :LANES] for u, x in zip(units, qy)]
    y0 = [x[:, LANES:] + kvv[t:2 * t] for x, kvv in zip(qy, akv)]
    gmat = [diag_blocks(x[:LANES]) for x in ght]
    hmat = [diag_blocks(x[LANES:]) + k2 for x, k2 in zip(ght, kv)]
    idx = {(u['d'], u['j'], u['p']): n for n, u in enumerate(units)}
    npair = HEADS // 2
    state = {(d, p): st_ref[d, p] for d in range(2) for p in range(npair)}
    ys = {}
    for step in range(cpb):
        for d in range(2):
            j = step if d == 0 else cpb - 1 - step
            for p in range(npair):
                n = idx[(d, j, p)]
                st = state[(d, p)]
                ys[(d, j, p)] = _mm_nt(qmat[n], bd(st)) + y0[n]
                state[(d, p)] = units[n]['dec'] * st + _mm(st, bd(gmat[n])) + hmat[n]
    for d, y_ref in enumerate((yf_ref, yb_ref)):
        y_ref[0] = jnp.concatenate(
            [jnp.concatenate([ys[(d, j, p)] for p in range(npair)], axis=1) for j in range(cpb)], axis=0)
        for p in range(npair):
            st_ref[d, p] = state[(d, p)]

    @pl.when(c == nb - 1)
    def _():
        sfin_ref[0] = st_ref[...]


def _rw_scan(pre, s0):
    r, v, kk, _, _, lw0, kd0, as0, lw1, kd1, as1 = pre
    b, l, _ = r.shape
    cpb = min(RW_CPB, l // CHUNK)
    nb = l // (cpb * CHUNK)
    fw = pl.BlockSpec((1, cpb * CHUNK, GROUP_W), lambda i, c: (i, c, 0))
    bw = pl.BlockSpec((1, cpb * CHUNK, GROUP_W), lambda i, c: (i, nb - 1 - c, 0))
    st = pl.BlockSpec((1, 2, HEADS // 2, HEAD_V, LANES), lambda i, c: (i, 0, 0, 0, 0))
    return pl.pallas_call(
        functools.partial(_rw_scan_kernel, nb, cpb),
        grid=(b, nb),
        in_specs=[fw] * 6 + [bw] * 6 + [st],
        out_specs=[fw, bw, st],
        out_shape=[jax.ShapeDtypeStruct((b, l, GROUP_W), F32)] * 2
                  + [jax.ShapeDtypeStruct((b, 2, HEADS // 2, HEAD_V, LANES), F32)],
        scratch_shapes=[pltpu.VMEM((2, HEADS // 2, HEAD_V, LANES), F32)],
        compiler_params=_cparams("parallel", "arbitrary"),
        name="rwkv_scan",
    )(r, v, kk, lw0, kd0, as0, r, v, kk, lw1, kd1, as1, s0)


def _rope(x, cos, sin_signed):
    lane = lax.broadcasted_iota(jnp.int32, x.shape, 1)
    n = x.shape[1]
    swapped = jnp.where((lane & ROPE_NF) == 0, pltpu.roll(x, n - ROPE_NF, 1), pltpu.roll(x, ROPE_NF, 1))
    return x * cos + swapped * sin_signed


def _ret_scan_kernel(nb, cpb, qf_ref, kf_ref, vf_ref, qb_ref, kb_ref, vb_ref,
                     dmat_ref, qdec_ref, kdec_ref, sdec_ref, s0_ref,
                     of_ref, ob_ref, sfin_ref, st_ref):
    c = pl.program_id(1)

    @pl.when(c == 0)
    def _():
        st_ref[...] = s0_ref[0]

    t = dmat_ref.shape[-1]
    npair = HEADS // 2
    low_t = lax.broadcasted_iota(jnp.int32, (t, LANES), 1) < HEAD_V
    low_s = lax.broadcasted_iota(jnp.int32, (HEAD_V, LANES), 1) < HEAD_V

    def bd(x, low):
        return jnp.concatenate([jnp.where(low, x, 0.0), jnp.where(low, 0.0, x)], axis=0)

    units = []
    for d, (q_ref, k_ref, v_ref) in enumerate(((qf_ref, kf_ref, vf_ref), (qb_ref, kb_ref, vb_ref))):
        for j in range(cpb):
            rows = slice(j * t, (j + 1) * t)
            q, k, v = q_ref[0, rows], k_ref[0, rows], v_ref[0, rows]
            qd = q * qdec_ref[d]
            kh = k * kdec_ref[d]
            for p in range(npair):
                sl = slice(p * LANES, (p + 1) * LANES)
                units.append(dict(d=d, j=j, p=p, q=q[:, sl], k=k[:, sl], qd=qd[:, sl], kh=kh[:, sl],
                                  v=v[:, sl], dec=sdec_ref[d][:, sl]))
    stacked = [_mm_nt(jnp.concatenate([jnp.where(low_t, u['q'], 0.0), jnp.where(low_t, 0.0, u['q'])], axis=0),
                      u['k']) for u in units]
    scores = [[s2[h * t:(h + 1) * t].astype(BF16) * dmat_ref[u['d'], 2 * u['p'] + h] for h in range(2)]
              for s2, u in zip(stacked, units)]
    intra = [_mm(jnp.concatenate(s2, axis=1), bd(u['v'], low_t)) for s2, u in zip(scores, units)]
    kv = []
    for u in units:
        full = _mm_tn(u['v'], u['kh'])
        kv.append(jnp.where(low_s, full[0:HEAD_V], full[HEAD_V:2 * HEAD_V]))
    idx = {(u['d'], u['j'], u['p']): i for i, u in enumerate(units)}
    entering = {}
    for d in range(2):
        for p in range(npair):
            st = st_ref[d, p]
            for step in range(cpb):
                j = step if d == 0 else cpb - 1 - step
                i = idx[(d, j, p)]
                entering[i] = st
                st = units[i]['dec'] * st + kv[i]
            st_ref[d, p] = st
    outs = [x + _mm_nt(u['qd'], bd(entering[i], low_s)) for i, (x, u) in enumerate(zip(intra, units))]
    for d, o_ref in enumerate((of_ref, ob_ref)):
        o_ref[0] = jnp.concatenate(
            [jnp.concatenate([outs[idx[(d, j, p)]] for p in range(npair)], axis=1) for j in range(cpb)], axis=0)

    @pl.when(c == nb - 1)
    def _():
        sfin_ref[0] = st_ref[...]


def _ret_tables(decay_logit, n):
    lg = jax.nn.log_sigmoid(decay_logit.astype(F32))
    pos = jnp.arange(n, dtype=F32)
    lag = pos[:, None] - pos[None, :]
    lag = jnp.stack([lag, -lag])
    dmat = jnp.where(lag[:, None] >= 0, jnp.exp(lg[:, :, None, None] * lag[:, None]), 0.0)
    lanes = jnp.repeat(lg, RET_DK, axis=-1)[:, None, :]
    qpow = jnp.stack([pos + 1.0, n - pos])[:, :, None]
    kpow = jnp.stack([n - 1.0 - pos, pos])[:, :, None]
    return dmat.astype(BF16), jnp.exp(lanes * qpow), jnp.exp(lanes * kpow), jnp.exp(lanes * n)


def _gla_scan_kernel(nb, cpb, qf_ref, kf_ref, vf_ref, af_ref, qb_ref, kb_ref, vb_ref, ab_ref,
                     aup_ref, abias_ref, s0_ref, of_ref, ob_ref, sfin_ref, st_ref):
    c = pl.program_id(1)

    @pl.when(c == 0)
    def _():
        st_ref[...] = s0_ref[0]

    t = GLA_T
    n = cpb * t
    dk = GLA_DK
    scale = dk ** -0.5
    dirs = ((qf_ref, kf_ref, vf_ref, af_ref), (qb_ref, kb_ref, vb_ref, ab_ref))
    units = []
    for d, (q_ref, k_ref, v_ref, a_ref) in enumerate(dirs):
        incl = _tri_incl(t, d == 1)
        q, k, v = q_ref[0], k_ref[0] * scale, v_ref[0]
        lw = -_softplus(-(_mm(a_ref[0], aup_ref[d]) + abias_ref[d])) * (1.0 / GLA_TAU)
        cin = _mm_left01(_chunk_tri01(n, t, d == 1), lw)
        qe = q * jnp.exp(cin)
        for j in range(cpb):
            rows = slice(j * t, (j + 1) * t)
            last = j * t + (0 if d == 1 else t - 1)
            mid = j * t + (t // 2 if d == 1 else t // 2 - 1)
            cj = cin[rows]
            clast = cin[last:last + 1]
            cmid = cin[mid:mid + 1]
            qt = q[rows] * jnp.exp(cj - cmid)
            kt = k[rows] * jnp.exp(cmid - cj)
            kh = k[rows] * jnp.exp(clast - cj)
            units.append(dict(d=d, j=j, incl=incl, qt=qt, kt=kt, qe=qe[rows], kh=kh, v=v[rows],
                              dec=jnp.exp(clast)))

    klane = lax.broadcasted_iota(jnp.int32, (t, LANES), 1) // dk
    vlow = lax.broadcasted_iota(jnp.int32, (t, LANES), 1) < HEAD_V
    khead = lax.broadcasted_iota(jnp.int32, (HEAD_V, LANES), 1) // dk
    npair = HEADS // 2

    def v_blockdiag(x):
        return jnp.concatenate([jnp.where(vlow, x, 0.0), jnp.where(vlow, 0.0, x)], axis=0)

    def state_rows(st, p):
        return jnp.concatenate([jnp.where(khead == 2 * p, st, 0.0), jnp.where(khead == 2 * p + 1, st, 0.0)],
                               axis=0)

    stacked = [_mm_nt(jnp.concatenate([jnp.where(klane == h, u['qt'], 0.0) for h in range(HEADS)], axis=0),
                      u['kt']) for u in units]
    amat = [[jnp.where(u['incl'], s4[h * t:(h + 1) * t], 0.0) for h in range(HEADS)]
            for s4, u in zip(stacked, units)]
    intra = [[_mm(jnp.concatenate(a[2 * p:2 * p + 2], axis=1), v_blockdiag(u['v'][:, p * LANES:(p + 1) * LANES]))
              for p in range(npair)] for a, u in zip(amat, units)]
    kv = []
    for u in units:
        full = _mm_tn(u['v'], u['kh'])
        blocks = [full[h * HEAD_V:(h + 1) * HEAD_V] for h in range(HEADS)]
        acc = blocks[HEADS - 1]
        for h in range(HEADS - 2, -1, -1):
            acc = jnp.where(khead == h, blocks[h], acc)
        kv.append(acc)
    idx = {(u['d'], u['j']): i for i, u in enumerate(units)}
    entering = {}
    for d in range(2):
        st = st_ref[d]
        for step in range(cpb):
            j = step if d == 0 else cpb - 1 - step
            i = idx[(d, j)]
            entering[i] = st
            st = units[i]['dec'] * st + kv[i]
        st_ref[d] = st
    inter = [_mm_nt(u['qe'], jnp.concatenate([state_rows(entering[i], p) for p in range(npair)], axis=0))
             for i, u in enumerate(units)]
    outs = [[x[p] + y[:, p * LANES:(p + 1) * LANES] for p in range(npair)] for x, y in zip(intra, inter)]
    for d, o_ref in enumerate((of_ref, ob_ref)):
        o_ref[0] = jnp.concatenate(
            [jnp.concatenate(outs[idx[(d, j)]], axis=1) for j in range(cpb)], axis=0)

    @pl.when(c == nb - 1)
    def _():
        sfin_ref[0] = st_ref[...]


def _rope_tables(rows):
    nf = ROPE_NF
    inv = ROPE_BASE ** (-jnp.arange(nf, dtype=F32) / nf)
    lane = jnp.arange(GROUP_W)
    freq = inv[lane % nf]
    sign = jnp.where((lane & ROPE_NF) == 0, -1.0, 1.0)
    ar = jnp.arange(rows, dtype=F32)[:, None] * freq[None, :]
    ac = jnp.arange(GRID_W, dtype=F32)[:, None] * freq[None, :]
    return (jnp.cos(ar).reshape(rows, 1, GROUP_W), (jnp.sin(ar) * sign).reshape(rows, 1, GROUP_W),
            jnp.cos(ac), jnp.sin(ac) * sign)


def _ret_scan(z, tables, s0):
    b, l, _ = z.shape
    n = min(RET_BLOCK, l)
    nb = l // n
    fw = lambda j: pl.BlockSpec((1, n, GROUP_W), lambda i, c: (i, c, j))
    bw = lambda j: pl.BlockSpec((1, n, GROUP_W), lambda i, c: (i, nb - 1 - c, j))
    full = lambda a: pl.BlockSpec(a.shape, lambda i, c: (0,) * a.ndim)
    st = pl.BlockSpec((1, 2, HEADS // 2, HEAD_V, LANES), lambda i, c: (i, 0, 0, 0, 0))
    return pl.pallas_call(
        functools.partial(_ret_scan_kernel, nb, n // RET_T),
        grid=(b, nb),
        in_specs=[fw(0), fw(1), fw(2), bw(0), bw(1), bw(2)] + [full(a) for a in tables] + [st],
        out_specs=[fw(0), bw(0), st],
        out_shape=[jax.ShapeDtypeStruct((b, l, GROUP_W), F32)] * 2
                  + [jax.ShapeDtypeStruct((b, 2, HEADS // 2, HEAD_V, LANES), F32)],
        scratch_shapes=[pltpu.VMEM((2, HEADS // 2, HEAD_V, LANES), F32)],
        compiler_params=_cparams("parallel", "arbitrary"),
        name="retention_scan",
    )(z, z, z, z, z, z, *tables, s0)


def _gla_scan(z, aup, abias, s0):
    b, l, _ = z.shape
    cpb = min(GLA_BLOCK, l) // CHUNK
    nb = l // (cpb * CHUNK)
    blk = lambda w, j, rev: pl.BlockSpec(
        (1, cpb * CHUNK, w), (lambda i, c: (i, nb - 1 - c, j)) if rev else (lambda i, c: (i, c, j)))
    st = pl.BlockSpec((1, 2, HEAD_V, GLA_QK), lambda i, c: (i, 0, 0, 0))
    ofw = pl.BlockSpec((1, cpb * CHUNK, GROUP_W), lambda i, c: (i, c, 0))
    obw = pl.BlockSpec((1, cpb * CHUNK, GROUP_W), lambda i, c: (i, nb - 1 - c, 0))
    return pl.pallas_call(
        functools.partial(_gla_scan_kernel, nb, cpb * CHUNK // GLA_T),
        grid=(b, nb),
        in_specs=[blk(GLA_QK, 0, False), blk(GLA_QK, 1, False), blk(GROUP_W, 1, False), blk(128, 6, False),
                  blk(GLA_QK, 0, True), blk(GLA_QK, 1, True), blk(GROUP_W, 1, True), blk(128, 6, True),
                  pl.BlockSpec(aup.shape, lambda i, c: (0, 0, 0)),
                  pl.BlockSpec(abias.shape, lambda i, c: (0, 0, 0)), st],
        out_specs=[ofw, obw, st],
        out_shape=[jax.ShapeDtypeStruct((b, l, GROUP_W), F32)] * 2
                  + [jax.ShapeDtypeStruct((b, 2, HEAD_V, GLA_QK), F32)],
        scratch_shapes=[pltpu.VMEM((2, HEAD_V, GLA_QK), F32)],
        compiler_params=_cparams("parallel", "arbitrary"),
        name="gla_scan",
    )(z, z, z, z, z, z, z, z, aup, abias, s0)


def _mix_kernel(x_ref, g1_ref, y5_ref, u5_ref, d5_ref, gw_ref, gb_ref,
                ryf_ref, ryb_ref, rbonus_ref, rg_ref, rlng_ref, rlnb_ref,
                tof_ref, tob_ref, tg_ref, tln_ref,
                gof_ref, gob_ref, gg_ref, gln_ref,
                bones_ref, wo_ref, o_ref, y5t_ref):
    bones = bones_ref[...]
    inv = 1.0 / HEAD_V

    def hmean(a):
        return _mm_right01(a, bones) * inv

    nch = y5_ref.shape[2]
    gph = LANES // S5_P
    y5g = [y5_ref[0, g] for g in range(S5_G)]
    pitch = S5_T + 1
    for s in range(S5_T):
        for hf in range(GROUP_W // LANES):
            y5t_ref[hf, pl.ds(s, nch, stride=pitch), :] = jnp.concatenate(
                [yg[:, s * S5_P:(s + 1) * S5_P] for yg in y5g[hf * gph:(hf + 1) * gph]], axis=1)
    y5t = jnp.concatenate(
        [jnp.concatenate([y5t_ref[hf, ch * pitch:ch * pitch + S5_T] for ch in range(nch)], axis=0)
         for hf in range(GROUP_W // LANES)], axis=1)
    y = y5t + d5_ref[...] * u5_ref[0]
    y = _gelu_tanh(y)
    ya = y * _sigmoid(_mm(y, gw_ref[...]) + gb_ref[...])
    yr = ryf_ref[0] + ryb_ref[0]
    dlt = yr - hmean(yr)
    yn = dlt * lax.rsqrt(hmean(dlt * dlt) + GN_EPS)
    yb = (yn * rlng_ref[...] + rlnb_ref[...] + rbonus_ref[0].astype(F32)) * rg_ref[0].astype(F32)
    ot = tof_ref[0] + tob_ref[0]
    yc = ot * lax.rsqrt(hmean(ot * ot) + EPS) * tln_ref[...] * _silu(tg_ref[0])
    og = gof_ref[0] + gob_ref[0]
    yd = og * lax.rsqrt(hmean(og * og) + EPS) * gln_ref[...] * _silu(gg_ref[0])
    mix = (_mm(ya, wo_ref[0]) + _mm(yb, wo_ref[1]) + _mm(yc, wo_ref[2]) + _mm(yd, wo_ref[3]))
    o_ref[0] = x_ref[0] + g1_ref[0] * mix


def _mix(x, g1, y5, u5, rw_pre, rw_y, zret, ret_o, zgla, gla_o, p, tm):
    b, l, d = x.shape
    tok = lambda w, j: pl.BlockSpec((1, tm, w), lambda i, t: (i, t, j))
    vec = lambda a: pl.BlockSpec(a.shape, lambda i, t: (0,) * a.ndim)
    t256 = tok(GROUP_W, 0)
    args = [x, g1, y5, u5, p['s5_d'], p['glu_w'], p['glu_b'],
            rw_y[0], rw_y[1], rw_pre[4], rw_pre[3], p['rw_ln_g'], p['rw_ln_b'],
            ret_o[0], ret_o[1], zret, p['ret_ln_g'],
            gla_o[0], gla_o[1], zgla, p['gla_ln_g'],
            p['bones'], p['w_out']]
    grouped = pl.BlockSpec((1, S5_G, tm // S5_T, S5_TP), lambda i, t: (i, 0, t, 0))
    specs = [tok(d, 0), pl.BlockSpec((1, 1, d), lambda i, t: (i, 0, 0)), grouped, t256,
             vec(p['s5_d']), vec(p['glu_w']), vec(p['glu_b']),
             t256, t256, t256, t256, vec(p['rw_ln_g']), vec(p['rw_ln_b']),
             t256, t256, tok(GROUP_W, 3), vec(p['ret_ln_g']),
             t256, t256, tok(GROUP_W, 2), vec(p['gla_ln_g']),
             vec(p['bones']), vec(p['w_out'])]
    return pl.pallas_call(
        _mix_kernel,
        grid=(b, l // tm),
        in_specs=specs,
        out_specs=tok(d, 0),
        out_shape=jax.ShapeDtypeStruct((b, l, d), F32),
        scratch_shapes=[pltpu.VMEM((GROUP_W // LANES, tm // S5_T * (S5_T + 1), LANES), F32)],
        compiler_params=_cparams("parallel", "parallel"),
        name="mix_outproj",
    )(*args)


def _mlp_kernel(final, nff, x_ref, g_ref, sc_ref, sh_ref, gate_ref, w1_ref, w2_ref, fg_ref, o_ref):
    x = x_ref[0]
    hb = _modnorm(x, g_ref[...], sc_ref[0], sh_ref[0]).astype(BF16)
    ff = w1_ref.shape[2] // nff
    acc = None
    for j in range(nff):
        a = jnp.maximum(jnp.dot(hb, w1_ref[0, :, j * ff:(j + 1) * ff], preferred_element_type=F32), 0.0)
        part = jnp.dot((a * a).astype(BF16), w2_ref[0, j * ff:(j + 1) * ff, :], preferred_element_type=F32)
        acc = part if acc is None else acc + part
    y = x + gate_ref[0] * acc
    if final:
        ms = jnp.mean(y * y, axis=-1, keepdims=True)
        y = y * lax.rsqrt(ms + EPS) * fg_ref[...]
    o_ref[0] = y


def _mlp(x, g, sc, sh, gate, w1, w2, layer, final_g, final, tm):
    b, l, d = x.shape
    tok = pl.BlockSpec((1, tm, d), lambda i, t: (i, t, 0))
    vec = pl.BlockSpec((1, 1, d), lambda i, t: (i, 0, 0))
    row = pl.BlockSpec((1, d), lambda i, t: (0, 0))
    once = lambda a: pl.BlockSpec((1,) + a.shape[1:], lambda i, t: (layer, 0, 0), pipeline_mode=pl.Buffered(1))
    return pl.pallas_call(
        functools.partial(_mlp_kernel, final, 4),
        grid=(b, l // tm),
        in_specs=[tok, row, vec, vec, vec, once(w1), once(w2), row],
        out_specs=tok,
        out_shape=jax.ShapeDtypeStruct((b, l, d), F32),
        compiler_params=_cparams("parallel", "parallel"),
        name="mlp",
    )(x, g, sc, sh, gate, w1, w2, final_g)


def kernel(x, c, ctx, c_ctx, ada_w, ada_b, norm1_g, norm2_g, w_in, w_out, s5_lam_re, s5_lam_im, s5_log_dt, s5_b_re, s5_b_im, s5_c_re, s5_c_im, s5_d, s5_glu_w, s5_glu_b, rw_mu, rw_w0, rw_w_up, rw_a0, rw_a_up, rw_g_up, rw_k_k, rw_k_a, rw_r_k, rw_ln_g, rw_ln_b, ret_decay_logit, ret_ln_g, gla_a_up, gla_a_b, gla_ln_g, mlp_w1, mlp_w2, final_g):
    b, l, d = x.shape
    lc = ctx.shape[1]
    depth = ada_w.shape[0]
    rows = l // GRID_W
    assert l % TOKEN_TILE == 0 and lc % RET_T == 0 and lc <= TOKEN_TILE and d % LANES == 0
    assert b < COND_ROWS and l % GRID_W == 0

    cond = jnp.zeros((COND_ROWS, d), F32).at[:b].set(c).at[b].set(c_ctx)
    mod = _modulation(cond, ada_w, ada_b)

    lane = jnp.arange(GROUP_W)
    bones = (lane[:, None] // HEAD_V == lane[None, :] // HEAD_V).astype(BF16)
    rope_lat = _rope_tables(rows)
    ident = (jnp.ones((lc // CHUNK, 1, GROUP_W), F32), jnp.zeros((lc // CHUNK, 1, GROUP_W), F32),
             jnp.ones((GRID_W, GROUP_W), F32), jnp.zeros((GRID_W, GROUP_W), F32))
    nlev = max(1, (l // S5_T - 1).bit_length())
    c0, c1, c2 = GROUP_W, GROUP_W + RW_COLS, GROUP_W + RW_COLS + 4 * GROUP_W
    row2 = lambda a: a.reshape(1, -1).astype(F32)
    w_in_b, w_out_b, w1_b, w2_b = _to_bf16(w_in), _to_bf16(w_out), _to_bf16(mlp_w1), _to_bf16(mlp_w2)

    s5_tab_all = jax.vmap(lambda *a: _s5_tables(*a, nlev))(
        s5_lam_re, s5_lam_im, s5_log_dt, s5_b_re, s5_b_im, s5_c_re, s5_c_im)
    ret_tab_all = jax.vmap(lambda a: _ret_tables(a, RET_T))(ret_decay_logit)
    zpad = lambda a, lo, n: jnp.zeros((depth, 2, LANES, n), F32).at[:, :, lo:lo + a.shape[2]].set(a).astype(BF16)
    rw_wup_all = zpad(rw_w_up, 0, GROUP_W)
    rw_aup_all = zpad(rw_a_up, RW_W_RANK, GROUP_W)
    rw_gup_all = jnp.zeros((depth, LANES, GROUP_W), F32).at[:, RW_W_RANK + RW_A_RANK:].set(rw_g_up).astype(BF16)
    gla_aup_all = zpad(gla_a_up, 0, GLA_QK)

    xc = ctx
    for i in range(depth):
        last = i == depth - 1
        m = mod[i].reshape(COND_ROWS, N_ADA, d)
        ml = m[:b, :, None, :]
        mc = jnp.broadcast_to(m[b][None, :, None, :], (b, N_ADA, 1, d))
        wi = w_in_b[i]
        w5, wr, wt = wi[:, :c0], wi[:, c0:c1], wi[:, c1:c2]
        wg = jnp.pad(wi[:, c2:], ((0, 0), (0, GLA_COLS_PAD - (wi.shape[1] - c2))))
        n1 = row2(norm1_g[i])
        n2 = row2(norm2_g[i])

        s5_tab = tuple(a[i] for a in s5_tab_all)
        rwp = dict(
            mu=row2(rw_mu[i]),
            w0=rw_w0[i].reshape(2, 1, GROUP_W), a0=rw_a0[i].reshape(2, 1, GROUP_W),
            wup=rw_wup_all[i], aup=rw_aup_all[i], gup=rw_gup_all[i],
            kk=row2(rw_k_k[i]), ka=row2(rw_k_a[i]), rk=row2(rw_r_k[i]), bones=bones)
        ret_tab = tuple(a[i] for a in ret_tab_all)
        gla_aup = gla_aup_all[i]
        gla_ab = gla_a_b[i].reshape(2, 1, GLA_QK).astype(F32)
        mixp = dict(s5_d=row2(s5_d[i]), glu_w=s5_glu_w[i].astype(BF16), glu_b=row2(s5_glu_b[i]),
                    rw_ln_g=row2(rw_ln_g[i]), rw_ln_b=row2(rw_ln_b[i]), ret_ln_g=row2(ret_ln_g[i]),
                    gla_ln_g=row2(gla_ln_g[i]), bones=bones,
                    w_out=w_out_b[i].reshape(4, GROUP_W, d))

        def mixers(xx, mm, is_lat, states):
            tm = TOKEN_TILE if is_lat else lc
            z5, z5g, zt, zg, pre = _inproj(xx, n1, mm[:, 1], mm[:, 0], w5, wr, wt, wg,
                                           rope_lat if is_lat else ident, rwp, is_lat, tm)
            y5, h5 = _s5_scan(z5g, s5_tab, states[0])
            yrf, yrb, srw = _rw_scan(pre, states[1])
            otf, otb, sret = _ret_scan(zt, ret_tab, states[2])
            ogf, ogb, sgla = _gla_scan(zg, gla_aup, gla_ab, states[3])
            outs = (z5, y5, pre, (yrf, yrb), zt, (otf, otb), zg, (ogf, ogb))
            return outs, (h5, srw, sret, sgla)

        def block(xx, mm, outs, is_lat, fin):
            tm = TOKEN_TILE if is_lat else lc
            z5, y5, pre, yr, zt, ot, zg, og = outs
            x1 = _mix(xx, mm[:, 2], y5, z5, pre, yr, zt, ot, zg, og, mixp, tm)
            return _mlp(x1, n2, mm[:, 4], mm[:, 3], mm[:, 5], w1_b, w2_b, i, row2(final_g), fin,
                        MLP_TILE if is_lat else lc)

        zeros = (jnp.zeros((b, 2, S5_G, 1, 2 * S5_N), F32),
                 jnp.zeros((b, 2, HEADS // 2, HEAD_V, LANES), F32),
                 jnp.zeros((b, 2, HEADS // 2, HEAD_V, LANES), F32),
                 jnp.zeros((b, 2, HEAD_V, GLA_QK), F32))
        outs_c, st_c = mixers(xc, mc, False, zeros)
        outs_l, _ = mixers(x, ml, True, st_c)
        x = block(x, ml, outs_l, True, last)
        if not last:
            xc = block(xc, mc, outs_c, False, False)
    return x
```

```python
import functools
import math

import jax
import jax.numpy as jnp
from jax import lax
from jax.experimental import pallas as pl
from jax.experimental.pallas import tpu as pltpu

F32 = jnp.float32
BF16 = jnp.bfloat16

LANES = 128
GRID_W = 64
GROUP_W = 256
N_ADA = 6
EPS = 1e-6
GN_EPS = 64e-5
TOKEN_TILE = 512
MLP_TILE = 1024
CHUNK = 64
RW_CPB = 4
RET_T = 256
RET_BLOCK = 1024
GLA_T = 128
GLA_BLOCK = 512
HEADS = 4
HEAD_V = 64
S5_P = 16
S5_G = 16
S5_N = 64
S5_T = 32
S5_TP = S5_T * S5_P
RW_COLS = 896
RW_W_RANK = 32
RW_A_RANK = 32
RW_G_RANK = 64
RET_DK = 64
GLA_DK = 32
GLA_QK = 128
GLA_RANK = 16
GLA_TAU = 16.0
GLA_COLS_PAD = 896
ROPE_BASE = 10000.0
ROPE_HALF = RET_DK // 2
ROPE_NF = RET_DK // 4
COND_ROWS = 8
CAST_ROWS = 256
ADA_TILE = 1536
VMEM_LIMIT = 56 * 1024 * 1024


def _cparams(*sem):
    return pltpu.CompilerParams(dimension_semantics=sem, vmem_limit_bytes=VMEM_LIMIT)


def _mm(a, b):
    return jnp.dot(a.astype(BF16), b.astype(BF16), preferred_element_type=F32)


def _mm_nt(a, b):
    return lax.dot_general(a.astype(BF16), b.astype(BF16), (((1,), (1,)), ((), ())),
                           preferred_element_type=F32)


def _mm_tn(a, b):
    return lax.dot_general(a.astype(BF16), b.astype(BF16), (((0,), (0,)), ((), ())),
                           preferred_element_type=F32)


def _split2(x):
    hi = x.astype(BF16)
    return hi, (x - hi.astype(F32)).astype(BF16)


def _chunk_tri01(n, t, reverse):
    ri = lax.broadcasted_iota(jnp.int32, (n, n), 0)
    ci = lax.broadcasted_iota(jnp.int32, (n, n), 1)
    shift = t.bit_length() - 1
    tri = (ci >= ri) if reverse else (ci <= ri)
    return jnp.where((ri >> shift) == (ci >> shift), jnp.where(tri, 1.0, 0.0), 0.0).astype(BF16)


def _mm_left01(m01, x):
    hi, lo = _split2(x)
    return jnp.dot(m01, hi, preferred_element_type=F32) + jnp.dot(m01, lo, preferred_element_type=F32)


def _mm_right01(x, m01):
    return jnp.dot(x.astype(BF16), m01, preferred_element_type=F32)


def _sigmoid(x):
    return 1.0 / (1.0 + jnp.exp(-x))


def _softplus(x):
    return jnp.maximum(x, 0.0) + jnp.log(1.0 + jnp.exp(-jnp.abs(x)))


def _silu(x):
    return x * _sigmoid(x)


def _gelu_tanh(x):
    return 0.5 * x * (1.0 + jnp.tanh(math.sqrt(2.0 / math.pi) * (x + 0.044715 * x * x * x)))


def _tri_incl(t, reverse):
    ri = lax.broadcasted_iota(jnp.int32, (t, t), 0)
    ci = lax.broadcasted_iota(jnp.int32, (t, t), 1)
    return (ci >= ri) if reverse else (ci <= ri)


def _cast_kernel(x_ref, o_ref):
    o_ref[...] = x_ref[...].astype(o_ref.dtype)


def _to_bf16(w):
    depth, r, c = w.shape
    tr = CAST_ROWS
    spec = pl.BlockSpec((1, tr, c), lambda i, j: (i, j, 0))
    return pl.pallas_call(
        _cast_kernel,
        grid=(depth, r // tr),
        in_specs=[spec],
        out_specs=spec,
        out_shape=jax.ShapeDtypeStruct(w.shape, BF16),
        compiler_params=_cparams("parallel", "parallel"),
        name="cast_bf16",
    )(w)


def _mod_kernel(cond_ref, w_ref, b_ref, o_ref):
    c = cond_ref[...]
    o_ref[0] = _mm(_silu(c), w_ref[0]) + b_ref[0]


def _modulation(cond, ada_w, ada_b):
    depth, d, n = ada_w.shape
    tn = ADA_TILE
    return pl.pallas_call(
        _mod_kernel,
        grid=(depth, n // tn),
        in_specs=[pl.BlockSpec((COND_ROWS, d), lambda i, j: (0, 0)),
                  pl.BlockSpec((1, d, tn), lambda i, j: (i, 0, j)),
                  pl.BlockSpec((1, 1, tn), lambda i, j: (i, 0, j))],
        out_specs=pl.BlockSpec((1, COND_ROWS, tn), lambda i, j: (i, 0, j)),
        out_shape=jax.ShapeDtypeStruct((depth, COND_ROWS, n), F32),
        compiler_params=_cparams("parallel", "parallel"),
        name="adaln_mod",
    )(cond, ada_w, ada_b.reshape(depth, 1, n))


def _modnorm(x, g, sc, sh):
    ms = jnp.mean(x * x, axis=-1, keepdims=True)
    return x * lax.rsqrt(ms + EPS) * g * (1.0 + sc) + sh


def _shift_columns(z, segments):
    tm, width = z.shape
    lane = lax.broadcasted_iota(jnp.int32, (tm, LANES), 1)
    starts = [s for s, _ in segments] + [width]
    cols = []
    for c in range(width // LANES):
        lo, hi = c * LANES, (c + 1) * LANES
        inside = [(max(starts[i], lo), segments[i][1]) for i in range(len(segments))
                  if starts[i] < hi and starts[i + 1] > lo]
        col = inside[-1][1](c)
        for first, fn in reversed(inside[:-1]):
            nxt = [f for f, _ in inside if f > first][0]
            col = jnp.where(lane < nxt - lo, fn(c), col)
        cols.append(col)
    return jnp.concatenate(cols, axis=1)


def _rw_prep_math(grid_shift, z, zp, zn, mu, w0_ref, wup_ref, a0_ref, aup_ref, gup, kkp, ka, rk, bones):
    tm = z.shape[0]
    row = lax.broadcasted_iota(jnp.int32, (tm, LANES), 0)
    col = lambda x, c: x[:, c * LANES:(c + 1) * LANES]
    if grid_shift:
        pos = row & (GRID_W - 1)
        left = lambda c: jnp.where(pos == 0, 0.0, pltpu.roll(col(z, c), 1, 0))
        right = lambda c: jnp.where(pos == GRID_W - 1, 0.0, pltpu.roll(col(z, c), tm - 1, 0))
        up = lambda c: jnp.concatenate([col(zp, c), col(z, c)[:tm - GRID_W]], axis=0)
        down = lambda c: jnp.concatenate([col(z, c)[GRID_W:], col(zn, c)], axis=0)
        q = RW_COLS // 4
        shifted = _shift_columns(z, [(0, left), (q, right), (2 * q, up), (3 * q, down)])
    else:
        prev = lambda c: jnp.where(row == 0, 0.0, pltpu.roll(col(z, c), 1, 0))
        nxt = lambda c: jnp.where(row == tm - 1, 0.0, pltpu.roll(col(z, c), tm - 1, 0))
        shifted = _shift_columns(z, [(0, prev), (RW_COLS // 2, nxt)])
    zm = z + mu * (shifted - z)
    r = zm[:, 0:GROUP_W]
    k = zm[:, GROUP_W:2 * GROUP_W]
    v = zm[:, 2 * GROUP_W:3 * GROUP_W]
    lo = zm[:, 3 * GROUP_W:RW_COLS]
    g = _mm(_sigmoid(lo), gup)
    kk = k * kkp
    kk = kk * lax.rsqrt(_mm_right01(kk * kk, bones) + 1e-12)
    bonus = _mm_right01(r * k * rk, bones) * v
    outs = [r.astype(BF16), v.astype(BF16), kk.astype(BF16), g.astype(BF16), bonus.astype(BF16)]
    tlo = jnp.tanh(lo)
    for d in range(2):
        log_decay = -math.exp(-0.5) * _sigmoid(w0_ref[d] + _mm(tlo, wup_ref[d]))
        a = _sigmoid(a0_ref[d] + _mm(lo, aup_ref[d]))
        outs += [log_decay,
                 (k * (1.0 + (a - 1.0) * ka)).astype(BF16), a.astype(BF16)]
    return outs


def _inproj_kernel(grid_shift, nt, x_ref, xp_ref, xn_ref, g_ref, sc_ref, sh_ref,
                   w5_ref, wr_ref, wt_ref, wg_ref, cr_ref, sr_ref, cc_ref, sn_ref,
                   mu_ref, w0_ref, wup_ref, a0_ref, aup_ref, gup_ref, kk_ref, ka_ref, rk_ref, bones_ref,
                   o5_ref, o5g_ref, ot_ref, og_ref, *rest):
    rw_refs, z5h_ref = rest[:-1], rest[-1]
    j = pl.program_id(1)
    norm = lambda x: _modnorm(x, g_ref[...], sc_ref[0], sh_ref[0]).astype(BF16)
    hb = norm(x_ref[0])
    z5 = jnp.dot(hb, w5_ref[...], preferred_element_type=F32)
    o5_ref[0] = z5
    og_ref[0] = jnp.dot(hb, wg_ref[...], preferred_element_type=F32)
    zr = jnp.dot(hb, wr_ref[...], preferred_element_type=F32)
    zp = zn = None
    if grid_shift:
        zp = jnp.where(j > 0, jnp.dot(norm(xp_ref[0]), wr_ref[...], preferred_element_type=F32), 0.0)
        zn = jnp.where(j < nt - 1, jnp.dot(norm(xn_ref[0]), wr_ref[...], preferred_element_type=F32), 0.0)
    rw = _rw_prep_math(grid_shift, zr, zp, zn, mu_ref[...], w0_ref, wup_ref, a0_ref, aup_ref, gup_ref[...],
                       kk_ref[...], ka_ref[...], rk_ref[...], bones_ref[...])
    for ref, val in zip(rw_refs, rw):
        ref[0] = val
    zt = jnp.dot(hb, wt_ref[...], preferred_element_type=F32)
    lane = lax.broadcasted_iota(jnp.int32, (GRID_W, GROUP_W), 1)
    by_row = (lane & ROPE_HALF) == 0
    nrow = cr_ref.shape[0]
    cos = jnp.concatenate([jnp.where(by_row, cr_ref[j], cc_ref[...]) for j in range(nrow)], axis=0)
    sin = jnp.concatenate([jnp.where(by_row, sr_ref[j], sn_ref[...]) for j in range(nrow)], axis=0)
    ot_ref[0, :, 0:GROUP_W] = _rope(zt[:, 0:GROUP_W], cos, sin)
    ot_ref[0, :, GROUP_W:2 * GROUP_W] = _rope(zt[:, GROUP_W:2 * GROUP_W] * RET_DK ** -0.5, cos, sin)
    ot_ref[0, :, 2 * GROUP_W:] = zt[:, 2 * GROUP_W:]
    nch = o5g_ref.shape[2]
    gph = LANES // S5_P
    for hf in range(GROUP_W // LANES):
        z5h_ref[hf] = z5[:, hf * LANES:(hf + 1) * LANES]
    for s in range(S5_T):
        for hf in range(GROUP_W // LANES):
            rows = z5h_ref[hf, pl.ds(s, nch, stride=S5_T), :]
            for g in range(gph):
                o5g_ref[0, hf * gph + g, :, s * S5_P:(s + 1) * S5_P] = rows[:, g * S5_P:(g + 1) * S5_P]


def _inproj(x, g, sc, sh, w5, wr, wt, wg, rope, rwp, grid_shift, tm):
    b, l, d = x.shape
    nt = l // tm
    hb = tm // GRID_W
    nhb = l // GRID_W
    tok = lambda n: pl.BlockSpec((1, tm, n), lambda i, j: (i, j, 0))
    vec = pl.BlockSpec((1, 1, d), lambda i, j: (i, 0, 0))
    full = lambda a: pl.BlockSpec(a.shape, lambda i, j: (0,) * a.ndim)
    grouped = pl.BlockSpec((1, S5_G, tm // S5_T, S5_TP), lambda i, j: (i, 0, j, 0))
    rowt = pl.BlockSpec((hb, 1, GROUP_W), lambda i, j: (j, 0, 0))
    above = pl.BlockSpec((1, GRID_W, d), lambda i, j: (i, jnp.maximum(j * hb - 1, 0), 0))
    below = pl.BlockSpec((1, GRID_W, d), lambda i, j: (i, jnp.minimum((j + 1) * hb, nhb - 1), 0))
    cr, sr, cc, sn = rope
    params = (rwp['mu'], rwp['w0'], rwp['wup'], rwp['a0'], rwp['aup'], rwp['gup'], rwp['kk'], rwp['ka'],
              rwp['rk'], rwp['bones'])
    rw_dtypes = (BF16,) * 5 + (F32, BF16, BF16) * 2
    outs = pl.pallas_call(
        functools.partial(_inproj_kernel, grid_shift, nt),
        grid=(b, nt),
        in_specs=[tok(d), above, below, pl.BlockSpec((1, d), lambda i, j: (0, 0)), vec, vec,
                  full(w5), full(wr), full(wt), full(wg), rowt, rowt, full(cc), full(sn)]
                 + [full(a) for a in params],
        out_specs=[tok(GROUP_W), grouped, tok(wt.shape[1]), tok(wg.shape[1])] + [tok(GROUP_W)] * 11,
        out_shape=[jax.ShapeDtypeStruct((b, l, GROUP_W), F32),
                   jax.ShapeDtypeStruct((b, S5_G, l // S5_T, S5_TP), F32),
                   jax.ShapeDtypeStruct((b, l, wt.shape[1]), F32),
                   jax.ShapeDtypeStruct((b, l, wg.shape[1]), F32)]
                  + [jax.ShapeDtypeStruct((b, l, GROUP_W), dt) for dt in rw_dtypes],
        scratch_shapes=[pltpu.VMEM((GROUP_W // LANES, tm, LANES), F32)],
        compiler_params=_cparams("parallel", "parallel"),
        name="norm_inproj",
    )(x, x, x, g, sc, sh, w5, wr, wt, wg, cr, sr, cc, sn, *params)
    return outs[0], outs[1], outs[2], outs[3], tuple(outs[4:])


def _s5_tables(lam_re, lam_im, log_dt, b_re, b_im, c_re, c_im, nlev):
    hp = lax.Precision.HIGHEST
    t = S5_T
    lam = lax.complex(jnp.minimum(lam_re.astype(F32), -1e-4), lam_im.astype(F32))
    ldt = lam * jnp.exp(log_dt.astype(F32))[..., None]
    a_bar = jnp.exp(ldt)
    bb = ((a_bar - 1.0) / lam)[..., None] * lax.complex(b_re.astype(F32), b_im.astype(F32))
    cm = lax.complex(c_re.astype(F32), c_im.astype(F32))
    tau = jnp.arange(t + 1, dtype=F32)
    apow = jnp.exp(ldt[:, :, None, :] * tau[None, None, :, None])
    taps = jnp.einsum('dgpn,dgtn,dgnq->dgtpq', cm, apow[:, :, :t], bb, precision=hp).real
    taprow = jnp.stack([taps[0], taps[1][:, ::-1]]).transpose(0, 1, 4, 2, 3).reshape(2, S5_G, S5_P, S5_TP)

    def pack(zc):
        return jnp.concatenate([zc.real, zc.imag], axis=-1)

    win_f = apow[0][:, t - 1 - jnp.arange(t), None, :] * bb[0].transpose(0, 2, 1)[:, None]
    win_b = apow[1][:, jnp.arange(t), None, :] * bb[1].transpose(0, 2, 1)[:, None]
    win = jnp.stack([pack(win_f), pack(win_b)]).reshape(2, S5_G, S5_TP, 2 * S5_N)
    ca_f = cm[0][:, None] * apow[0][:, 1 + jnp.arange(t), None, :]
    ca_b = cm[1][:, None] * apow[1][:, t - jnp.arange(t), None, :]

    def outpack(ca):
        w = jnp.concatenate([ca.real, -ca.imag], axis=-1)
        return w.reshape(S5_G, S5_TP, 2 * S5_N).transpose(0, 2, 1)

    wout = jnp.stack([outpack(ca_f), outpack(ca_b)])
    lev = (2.0 ** jnp.arange(nlev, dtype=F32)) * t
    pw = jnp.exp(ldt[:, :, None, :] * lev[None, None, :, None])
    p1 = jnp.concatenate([pw.real, pw.real], axis=-1)
    p2 = jnp.concatenate([-pw.imag, pw.imag], axis=-1)
    pw = jnp.stack([p1, p2], axis=3)
    return taprow, win.astype(BF16), wout.astype(BF16), pw


def _s5_kernel(nc, nlev, u_ref, tap_ref, win_ref, wout_ref, pw_ref, h0_ref, y_ref, hfin_ref, conv_ref):
    u = u_ref[0, 0].astype(BF16)
    row = lax.broadcasted_iota(jnp.int32, (nc, 2 * S5_N), 0)
    lane = lax.broadcasted_iota(jnp.int32, (S5_P, S5_TP), 1)
    for s in range(S5_T):
        lo = s * S5_P
        fwd = tap_ref[0, 0] if s == 0 else jnp.where(lane >= lo, pltpu.roll(tap_ref[0, 0], lo, 1), 0.0)
        hi = lo + S5_P
        bwd = tap_ref[1, 0] if hi == S5_TP else jnp.where(lane < hi, pltpu.roll(tap_ref[1, 0], hi, 1), 0.0)
        conv_ref[0, lo:hi, :] = fwd.astype(BF16)
        conv_ref[1, lo:hi, :] = bwd.astype(BF16)

    def cmul(x, d, j):
        return pw_ref[d, 0, j, 0:1] * x + pw_ref[d, 0, j, 1:2] * pltpu.roll(x, S5_N, 1)

    v = [jnp.dot(u, win_ref[d, 0], preferred_element_type=F32) for d in range(2)]
    x = [jnp.where(row == 0, h0_ref[0, 0, 0], pltpu.roll(v[0], 1, 0)),
         jnp.where(row == nc - 1, h0_ref[0, 1, 0], pltpu.roll(v[1], nc - 1, 0))]
    for j in range(nlev):
        sh = 2 ** j
        xs = [jnp.where(row >= sh, pltpu.roll(x[0], sh, 0), 0.0),
              jnp.where(row < nc - sh, pltpu.roll(x[1], nc - sh, 0), 0.0)]
        x = [x[d] + cmul(xs[d], d, j) for d in range(2)]
    for d, last in enumerate((nc - 1, 0)):
        hfin_ref[0, d, 0] = cmul(x[d][last:last + 1], d, 0) + v[d][last:last + 1]
    y_ref[0, 0] = (jnp.dot(u, conv_ref[0], preferred_element_type=F32) + _mm(x[0], wout_ref[0, 0])
                   + jnp.dot(u, conv_ref[1], preferred_element_type=F32) + _mm(x[1], wout_ref[1, 0]))


def _s5_scan(uf, tables, h0):
    conv, win, wout, pw = tables
    b, _, nc, _ = uf.shape
    nlev = max(1, (nc - 1).bit_length())
    pw = pw[:, :, :nlev]
    n2 = 2 * S5_N
    y, hfin = pl.pallas_call(
        functools.partial(_s5_kernel, nc, nlev),
        grid=(b, S5_G),
        in_specs=[pl.BlockSpec((1, 1, nc, S5_TP), lambda i, g: (i, g, 0, 0)),
                  pl.BlockSpec((2, 1, S5_P, S5_TP), lambda i, g: (0, g, 0, 0)),
                  pl.BlockSpec((2, 1, S5_TP, n2), lambda i, g: (0, g, 0, 0)),
                  pl.BlockSpec((2, 1, n2, S5_TP), lambda i, g: (0, g, 0, 0)),
                  pl.BlockSpec((2, 1, nlev, 2, n2), lambda i, g: (0, g, 0, 0, 0)),
                  pl.BlockSpec((1, 2, 1, 1, n2), lambda i, g: (i, 0, g, 0, 0))],
        out_specs=[pl.BlockSpec((1, 1, nc, S5_TP), lambda i, g: (i, g, 0, 0)),
                   pl.BlockSpec((1, 2, 1, 1, n2), lambda i, g: (i, 0, g, 0, 0))],
        out_shape=[jax.ShapeDtypeStruct((b, S5_G, nc, S5_TP), F32),
                   jax.ShapeDtypeStruct((b, 2, S5_G, 1, n2), F32)],
        scratch_shapes=[pltpu.VMEM((2, S5_TP, S5_TP), BF16)],
        compiler_params=_cparams("parallel", "parallel"),
        name="s5_scan",
    )(uf, conv, win, wout, pw, h0)
    return y, hfin


def _rw_scan_kernel(nb, cpb, rf_ref, vf_ref, kkf_ref, lwf_ref, kdf_ref, asf_ref,
                    rb_ref, vb_ref, kkb_ref, lwb_ref, kdb_ref, asb_ref, s0_ref,
                    yf_ref, yb_ref, sfin_ref, st_ref):
    c = pl.program_id(1)

    @pl.when(c == 0)
    def _():
        st_ref[...] = s0_ref[0]

    t = CHUNK
    ri = lax.broadcasted_iota(jnp.int32, (t, LANES), 0)
    li = lax.broadcasted_iota(jnp.int32, (t, LANES), 1)
    ci = li & (t - 1)
    low = li < HEAD_V
    eyef = jnp.where(ri == ci, 1.0, 0.0)
    same = lambda s: jnp.where((ri >> s) == (ci >> s), 1.0, 0.0)
    m4, m8, m16, m32 = same(2), same(3), same(4), same(5)
    merge_masks = (m8 - m4, m16 - m8, m32 - m16, 1.0 - m32)

    def bd(x):
        return jnp.concatenate([jnp.where(low, x, 0.0), jnp.where(low, 0.0, x)], axis=0)

    def diag_blocks(full):
        return jnp.where(low, full[0:HEAD_V], full[HEAD_V:2 * HEAD_V])

    dir_refs = ((rf_ref, vf_ref, kkf_ref, lwf_ref, kdf_ref, asf_ref),
                (rb_ref, vb_ref, kkb_ref, lwb_ref, kdb_ref, asb_ref))
    units = []
    for d, (r_ref, v_ref, kk_ref, lw_ref, kd_ref, as_ref) in enumerate(dir_refs):
        incl = (ci >= ri) if d == 1 else (ci <= ri)
        strict = (ci > ri) if d == 1 else (ci < ri)
        r, v, kk, kd = (x[0].astype(F32) for x in (r_ref, v_ref, kk_ref, kd_ref))
        lw = lw_ref[0]
        cin = _mm_left01(_chunk_tri01(cpb * t, t, d == 1), lw)
        e_in = jnp.exp(cin)
        e_neg = jnp.exp(-cin)
        rt = r * e_in
        at = -kk * jnp.exp(cin - lw)
        bvec = kk * as_ref[0].astype(F32)
        bt = bvec * e_neg
        kt = kd * e_neg
        for j in range(cpb):
            rows = slice(j * t, (j + 1) * t)
            last = j * t + (0 if d == 1 else t - 1)
            clast = cin[last:last + 1]
            dl = jnp.exp(clast - cin[rows])
            bh = bvec[rows] * dl
            kh = kd[rows] * dl
            dec = jnp.exp(clast)
            for p in range(HEADS // 2):
                sl = slice(p * LANES, (p + 1) * LANES)
                units.append(dict(d=d, j=j, p=p, incl=incl, strict=strict,
                                  at=at[rows, sl], rt=rt[rows, sl], bt=bt[rows, sl], kt=kt[rows, sl],
                                  bh=bh[:, sl], kh=kh[:, sl], v=v[rows, sl], dec=dec[:, sl]))

    x1 = [jnp.concatenate([u['at'], u['rt']], axis=0) for u in units]
    scores = [_mm_nt(x, jnp.concatenate([bd(u['bt']), bd(u['kt'])], axis=0))
              for x, u in zip(x1, units)]
    nmat = [jnp.where(u['strict'], a[0:t, :LANES], 0.0) for u, a in zip(units, scores)]
    a_rb = [jnp.where(u['incl'], a[t:2 * t, :LANES], 0.0) for u, a in zip(units, scores)]
    a_kk = [jnp.concatenate([jnp.where(u['strict'], a[0:t, LANES:], 0.0),
                             jnp.where(u['incl'], a[t:2 * t, LANES:], 0.0)], axis=0)
            for u, a in zip(units, scores)]
    akv = [_mm(a, bd(u['v'])) for u, a in zip(units, a_kk)]
    kv = [diag_blocks(_mm_tn(u['v'], u['kh'])) for u in units]
    nd = [x * m4 for x in nmat]
    n2 = [_mm(x, bd(x)) for x in nd]
    tinv = [eyef + x + _mm(eyef + x, bd(y)) for x, y in zip(nd, n2)]
    for mk in merge_masks:
        w = [_mm(ti, bd(x * mk)) for ti, x in zip(tinv, nmat)]
        tinv = [ti + _mm(wi, bd(ti)) for ti, wi in zip(tinv, w)]
    zz = [_mm(ti, jnp.concatenate([bd(u['at']), bd(kvv[0:t])], axis=1))
          for ti, u, kvv in zip(tinv, units, akv)]
    ght = [_mm_tn(z, u['bh']) for u, z in zip(units, zz)]
    qy = [_mm(a, jnp.concatenate([bd(z[:, :LANES]), bd(z[:, LANES:])], axis=1))
          for a, z in zip(a_rb, zz)]
    qmat = [u['rt'] + x[:, :LANES] for u, x in zip(units, qy)]
    y0 = [x[:, LANES:] + kvv[t:2 * t] for x, kvv in zip(qy, akv)]
    gmat = [diag_blocks(x[:LANES]) for x in ght]
    hmat = [diag_blocks(x[LANES:]) + k2 for x, k2 in zip(ght, kv)]
    idx = {(u['d'], u['j'], u['p']): n for n, u in enumerate(units)}
    npair = HEADS // 2
    state = {(d, p): st_ref[d, p] for d in range(2) for p in range(npair)}
    ys = {}
    for step in range(cpb):
        for d in range(2):
            j = step if d == 0 else cpb - 1 - step
            for p in range(npair):
                n = idx[(d, j, p)]
                st = state[(d, p)]
                ys[(d, j, p)] = _mm_nt(qmat[n], bd(st)) + y0[n]
                state[(d, p)] = units[n]['dec'] * st + _mm(st, bd(gmat[n])) + hmat[n]
    for d, y_ref in enumerate((yf_ref, yb_ref)):
        y_ref[0] = jnp.concatenate(
            [jnp.concatenate([ys[(d, j, p)] for p in range(npair)], axis=1) for j in range(cpb)], axis=0)
        for p in range(npair):
            st_ref[d, p] = state[(d, p)]

    @pl.when(c == nb - 1)
    def _():
        sfin_ref[0] = st_ref[...]


def _rw_scan(pre, s0):
    r, v, kk, _, _, lw0, kd0, as0, lw1, kd1, as1 = pre
    b, l, _ = r.shape
    cpb = min(RW_CPB, l // CHUNK)
    nb = l // (cpb * CHUNK)
    fw = pl.BlockSpec((1, cpb * CHUNK, GROUP_W), lambda i, c: (i, c, 0))
    bw = pl.BlockSpec((1, cpb * CHUNK, GROUP_W), lambda i, c: (i, nb - 1 - c, 0))
    st = pl.BlockSpec((1, 2, HEADS // 2, HEAD_V, LANES), lambda i, c: (i, 0, 0, 0, 0))
    return pl.pallas_call(
        functools.partial(_rw_scan_kernel, nb, cpb),
        grid=(b, nb),
        in_specs=[fw] * 6 + [bw] * 6 + [st],
        out_specs=[fw, bw, st],
        out_shape=[jax.ShapeDtypeStruct((b, l, GROUP_W), F32)] * 2
                  + [jax.ShapeDtypeStruct((b, 2, HEADS // 2, HEAD_V, LANES), F32)],
        scratch_shapes=[pltpu.VMEM((2, HEADS // 2, HEAD_V, LANES), F32)],
        compiler_params=_cparams("parallel", "arbitrary"),
        name="rwkv_scan",
    )(r, v, kk, lw0, kd0, as0, r, v, kk, lw1, kd1, as1, s0)


def _rope(x, cos, sin_signed):
    lane = lax.broadcasted_iota(jnp.int32, x.shape, 1)
    n = x.shape[1]
    swapped = jnp.where((lane & ROPE_NF) == 0, pltpu.roll(x, n - ROPE_NF, 1), pltpu.roll(x, ROPE_NF, 1))
    return x * cos + swapped * sin_signed


def _ret_scan_kernel(nb, cpb, qf_ref, kf_ref, vf_ref, qb_ref, kb_ref, vb_ref,
                     dmat_ref, qdec_ref, kdec_ref, sdec_ref, s0_ref,
                     of_ref, ob_ref, sfin_ref, st_ref):
    c = pl.program_id(1)

    @pl.when(c == 0)
    def _():
        st_ref[...] = s0_ref[0]

    t = dmat_ref.shape[-1]
    npair = HEADS // 2
    low_t = lax.broadcasted_iota(jnp.int32, (t, LANES), 1) < HEAD_V
    low_s = lax.broadcasted_iota(jnp.int32, (HEAD_V, LANES), 1) < HEAD_V

    def bd(x, low):
        return jnp.concatenate([jnp.where(low, x, 0.0), jnp.where(low, 0.0, x)], axis=0)

    units = []
    for d, (q_ref, k_ref, v_ref) in enumerate(((qf_ref, kf_ref, vf_ref), (qb_ref, kb_ref, vb_ref))):
        for j in range(cpb):
            rows = slice(j * t, (j + 1) * t)
            q, k, v = q_ref[0, rows], k_ref[0, rows], v_ref[0, rows]
            qd = q * qdec_ref[d]
            kh = k * kdec_ref[d]
            for p in range(npair):
                sl = slice(p * LANES, (p + 1) * LANES)
                units.append(dict(d=d, j=j, p=p, q=q[:, sl], k=k[:, sl], qd=qd[:, sl], kh=kh[:, sl],
                                  v=v[:, sl], dec=sdec_ref[d][:, sl]))
    stacked = [_mm_nt(jnp.concatenate([jnp.where(low_t, u['q'], 0.0), jnp.where(low_t, 0.0, u['q'])], axis=0),
                      u['k']) for u in units]
    scores = [[s2[h * t:(h + 1) * t].astype(BF16) * dmat_ref[u['d'], 2 * u['p'] + h] for h in range(2)]
              for s2, u in zip(stacked, units)]
    intra = [_mm(jnp.concatenate(s2, axis=1), bd(u['v'], low_t)) for s2, u in zip(scores, units)]
    kv = []
    for u in units:
        full = _mm_tn(u['v'], u['kh'])
        kv.append(jnp.where(low_s, full[0:HEAD_V], full[HEAD_V:2 * HEAD_V]))
    idx = {(u['d'], u['j'], u['p']): i for i, u in enumerate(units)}
    entering = {}
    for d in range(2):
        for p in range(npair):
            st = st_ref[d, p]
            for step in range(cpb):
                j = step if d == 0 else cpb - 1 - step
                i = idx[(d, j, p)]
                entering[i] = st
                st = units[i]['dec'] * st + kv[i]
            st_ref[d, p] = st
    outs = [x + _mm_nt(u['qd'], bd(entering[i], low_s)) for i, (x, u) in enumerate(zip(intra, units))]
    for d, o_ref in enumerate((of_ref, ob_ref)):
        o_ref[0] = jnp.concatenate(
            [jnp.concatenate([outs[idx[(d, j, p)]] for p in range(npair)], axis=1) for j in range(cpb)], axis=0)

    @pl.when(c == nb - 1)
    def _():
        sfin_ref[0] = st_ref[...]


def _ret_tables(decay_logit, n):
    lg = jax.nn.log_sigmoid(decay_logit.astype(F32))
    pos = jnp.arange(n, dtype=F32)
    lag = pos[:, None] - pos[None, :]
    lag = jnp.stack([lag, -lag])
    dmat = jnp.where(lag[:, None] >= 0, jnp.exp(lg[:, :, None, None] * lag[:, None]), 0.0)
    lanes = jnp.repeat(lg, RET_DK, axis=-1)[:, None, :]
    qpow = jnp.stack([pos + 1.0, n - pos])[:, :, None]
    kpow = jnp.stack([n - 1.0 - pos, pos])[:, :, None]
    return dmat.astype(BF16), jnp.exp(lanes * qpow), jnp.exp(lanes * kpow), jnp.exp(lanes * n)


def _gla_scan_kernel(nb, cpb, qf_ref, kf_ref, vf_ref, af_ref, qb_ref, kb_ref, vb_ref, ab_ref,
                     aup_ref, abias_ref, s0_ref, of_ref, ob_ref, sfin_ref, st_ref):
    c = pl.program_id(1)

    @pl.when(c == 0)
    def _():
        st_ref[...] = s0_ref[0]

    t = GLA_T
    n = cpb * t
    dk = GLA_DK
    scale = dk ** -0.5
    dirs = ((qf_ref, kf_ref, vf_ref, af_ref), (qb_ref, kb_ref, vb_ref, ab_ref))
    units = []
    for d, (q_ref, k_ref, v_ref, a_ref) in enumerate(dirs):
        incl = _tri_incl(t, d == 1)
        q, k, v = q_ref[0], k_ref[0] * scale, v_ref[0]
        lw = -_softplus(-(_mm(a_ref[0], aup_ref[d]) + abias_ref[d])) * (1.0 / GLA_TAU)
        cin = _mm_left01(_chunk_tri01(n, t, d == 1), lw)
        qe = q * jnp.exp(cin)
        for j in range(cpb):
            rows = slice(j * t, (j + 1) * t)
            last = j * t + (0 if d == 1 else t - 1)
            mid = j * t + (t // 2 if d == 1 else t // 2 - 1)
            cj = cin[rows]
            clast = cin[last:last + 1]
            cmid = cin[mid:mid + 1]
            qt = q[rows] * jnp.exp(cj - cmid)
            kt = k[rows] * jnp.exp(cmid - cj)
            kh = k[rows] * jnp.exp(clast - cj)
            units.append(dict(d=d, j=j, incl=incl, qt=qt, kt=kt, qe=qe[rows], kh=kh, v=v[rows],
                              dec=jnp.exp(clast)))

    klane = lax.broadcasted_iota(jnp.int32, (t, LANES), 1) // dk
    vlow = lax.broadcasted_iota(jnp.int32, (t, LANES), 1) < HEAD_V
    khead = lax.broadcasted_iota(jnp.int32, (HEAD_V, LANES), 1) // dk
    npair = HEADS // 2

    def v_blockdiag(x):
        return jnp.concatenate([jnp.where(vlow, x, 0.0), jnp.where(vlow, 0.0, x)], axis=0)

    def state_rows(st, p):
        return jnp.concatenate([jnp.where(khead == 2 * p, st, 0.0), jnp.where(khead == 2 * p + 1, st, 0.0)],
                               axis=0)

    stacked = [_mm_nt(jnp.concatenate([jnp.where(klane == h, u['qt'], 0.0) for h in range(HEADS)], axis=0),
                      u['kt']) for u in units]
    amat = [[jnp.where(u['incl'], s4[h * t:(h + 1) * t], 0.0) for h in range(HEADS)]
            for s4, u in zip(stacked, units)]
    intra = [[_mm(jnp.concatenate(a[2 * p:2 * p + 2], axis=1), v_blockdiag(u['v'][:, p * LANES:(p + 1) * LANES]))
              for p in range(npair)] for a, u in zip(amat, units)]
    kv = []
    for u in units:
        full = _mm_tn(u['v'], u['kh'])
        blocks = [full[h * HEAD_V:(h + 1) * HEAD_V] for h in range(HEADS)]
        acc = blocks[HEADS - 1]
        for h in range(HEADS - 2, -1, -1):
            acc = jnp.where(khead == h, blocks[h], acc)
        kv.append(acc)
    idx = {(u['d'], u['j']): i for i, u in enumerate(units)}
    entering = {}
    for d in range(2):
        st = st_ref[d]
        for step in range(cpb):
            j = step if d == 0 else cpb - 1 - step
            i = idx[(d, j)]
            entering[i] = st
            st = units[i]['dec'] * st + kv[i]
        st_ref[d] = st
    inter = [_mm_nt(u['qe'], jnp.concatenate([state_rows(entering[i], p) for p in range(npair)], axis=0))
             for i, u in enumerate(units)]
    outs = [[x[p] + y[:, p * LANES:(p + 1) * LANES] for p in range(npair)] for x, y in zip(intra, inter)]
    for d, o_ref in enumerate((of_ref, ob_ref)):
        o_ref[0] = jnp.concatenate(
            [jnp.concatenate(outs[idx[(d, j)]], axis=1) for j in range(cpb)], axis=0)

    @pl.when(c == nb - 1)
    def _():
        sfin_ref[0] = st_ref[...]


def _rope_tables(rows):
    nf = ROPE_NF
    inv = ROPE_BASE ** (-jnp.arange(nf, dtype=F32) / nf)
    lane = jnp.arange(GROUP_W)
    freq = inv[lane % nf]
    sign = jnp.where((lane & ROPE_NF) == 0, -1.0, 1.0)
    ar = jnp.arange(rows, dtype=F32)[:, None] * freq[None, :]
    ac = jnp.arange(GRID_W, dtype=F32)[:, None] * freq[None, :]
    return (jnp.cos(ar).reshape(rows, 1, GROUP_W), (jnp.sin(ar) * sign).reshape(rows, 1, GROUP_W),
            jnp.cos(ac), jnp.sin(ac) * sign)


def _ret_scan(z, tables, s0):
    b, l, _ = z.shape
    n = min(RET_BLOCK, l)
    nb = l // n
    fw = lambda j: pl.BlockSpec((1, n, GROUP_W), lambda i, c: (i, c, j))
    bw = lambda j: pl.BlockSpec((1, n, GROUP_W), lambda i, c: (i, nb - 1 - c, j))
    full = lambda a: pl.BlockSpec(a.shape, lambda i, c: (0,) * a.ndim)
    st = pl.BlockSpec((1, 2, HEADS // 2, HEAD_V, LANES), lambda i, c: (i, 0, 0, 0, 0))
    return pl.pallas_call(
        functools.partial(_ret_scan_kernel, nb, n // RET_T),
        grid=(b, nb),
        in_specs=[fw(0), fw(1), fw(2), bw(0), bw(1), bw(2)] + [full(a) for a in tables] + [st],
        out_specs=[fw(0), bw(0), st],
        out_shape=[jax.ShapeDtypeStruct((b, l, GROUP_W), F32)] * 2
                  + [jax.ShapeDtypeStruct((b, 2, HEADS // 2, HEAD_V, LANES), F32)],
        scratch_shapes=[pltpu.VMEM((2, HEADS // 2, HEAD_V, LANES), F32)],
        compiler_params=_cparams("parallel", "arbitrary"),
        name="retention_scan",
    )(z, z, z, z, z, z, *tables, s0)


def _gla_scan(z, aup, abias, s0):
    b, l, _ = z.shape
    cpb = min(GLA_BLOCK, l) // CHUNK
    nb = l // (cpb * CHUNK)
    blk = lambda w, j, rev: pl.BlockSpec(
        (1, cpb * CHUNK, w), (lambda i, c: (i, nb - 1 - c, j)) if rev else (lambda i, c: (i, c, j)))
    st = pl.BlockSpec((1, 2, HEAD_V, GLA_QK), lambda i, c: (i, 0, 0, 0))
    ofw = pl.BlockSpec((1, cpb * CHUNK, GROUP_W), lambda i, c: (i, c, 0))
    obw = pl.BlockSpec((1, cpb * CHUNK, GROUP_W), lambda i, c: (i, nb - 1 - c, 0))
    return pl.pallas_call(
        functools.partial(_gla_scan_kernel, nb, cpb * CHUNK // GLA_T),
        grid=(b, nb),
        in_specs=[blk(GLA_QK, 0, False), blk(GLA_QK, 1, False), blk(GROUP_W, 1, False), blk(128, 6, False),
                  blk(GLA_QK, 0, True), blk(GLA_QK, 1, True), blk(GROUP_W, 1, True), blk(128, 6, True),
                  pl.BlockSpec(aup.shape, lambda i, c: (0, 0, 0)),
                  pl.BlockSpec(abias.shape, lambda i, c: (0, 0, 0)), st],
        out_specs=[ofw, obw, st],
        out_shape=[jax.ShapeDtypeStruct((b, l, GROUP_W), F32)] * 2
                  + [jax.ShapeDtypeStruct((b, 2, HEAD_V, GLA_QK), F32)],
        scratch_shapes=[pltpu.VMEM((2, HEAD_V, GLA_QK), F32)],
        compiler_params=_cparams("parallel", "arbitrary"),
        name="gla_scan",
    )(z, z, z, z, z, z, z, z, aup, abias, s0)


def _mix_kernel(x_ref, g1_ref, y5_ref, u5_ref, d5_ref, gw_ref, gb_ref,
                ryf_ref, ryb_ref, rbonus_ref, rg_ref, rlng_ref, rlnb_ref,
                tof_ref, tob_ref, tg_ref, tln_ref,
                gof_ref, gob_ref, gg_ref, gln_ref,
                bones_ref, wo_ref, o_ref, y5t_ref):
    bones = bones_ref[...]
    inv = 1.0 / HEAD_V

    def hmean(a):
        return _mm_right01(a, bones) * inv

    nch = y5_ref.shape[2]
    gph = LANES // S5_P
    y5g = [y5_ref[0, g] for g in range(S5_G)]
    pitch = S5_T + 1
    for s in range(S5_T):
        for hf in range(GROUP_W // LANES):
            y5t_ref[hf, pl.ds(s, nch, stride=pitch), :] = jnp.concatenate(
                [yg[:, s * S5_P:(s + 1) * S5_P] for yg in y5g[hf * gph:(hf + 1) * gph]], axis=1)
    y5t = jnp.concatenate(
        [jnp.concatenate([y5t_ref[hf, ch * pitch:ch * pitch + S5_T] for ch in range(nch)], axis=0)
         for hf in range(GROUP_W // LANES)], axis=1)
    y = y5t + d5_ref[...] * u5_ref[0]
    y = _gelu_tanh(y)
    ya = y * _sigmoid(_mm(y, gw_ref[...]) + gb_ref[...])
    yr = ryf_ref[0] + ryb_ref[0]
    dlt = yr - hmean(yr)
    yn = dlt * lax.rsqrt(hmean(dlt * dlt) + GN_EPS)
    yb = (yn * rlng_ref[...] + rlnb_ref[...] + rbonus_ref[0].astype(F32)) * rg_ref[0].astype(F32)
    ot = tof_ref[0] + tob_ref[0]
    yc = ot * lax.rsqrt(hmean(ot * ot) + EPS) * tln_ref[...] * _silu(tg_ref[0])
    og = gof_ref[0] + gob_ref[0]
    yd = og * lax.rsqrt(hmean(og * og) + EPS) * gln_ref[...] * _silu(gg_ref[0])
    mix = (_mm(ya, wo_ref[0]) + _mm(yb, wo_ref[1]) + _mm(yc, wo_ref[2]) + _mm(yd, wo_ref[3]))
    o_ref[0] = x_ref[0] + g1_ref[0] * mix


def _mix(x, g1, y5, u5, rw_pre, rw_y, zret, ret_o, zgla, gla_o, p, tm):
    b, l, d = x.shape
    tok = lambda w, j: pl.BlockSpec((1, tm, w), lambda i, t: (i, t, j))
    vec = lambda a: pl.BlockSpec(a.shape, lambda i, t: (0,) * a.ndim)
    t256 = tok(GROUP_W, 0)
    args = [x, g1, y5, u5, p['s5_d'], p['glu_w'], p['glu_b'],
            rw_y[0], rw_y[1], rw_pre[4], rw_pre[3], p['rw_ln_g'], p['rw_ln_b'],
            ret_o[0], ret_o[1], zret, p['ret_ln_g'],
            gla_o[0], gla_o[1], zgla, p['gla_ln_g'],
            p['bones'], p['w_out']]
    grouped = pl.BlockSpec((1, S5_G, tm // S5_T, S5_TP), lambda i, t: (i, 0, t, 0))
    specs = [tok(d, 0), pl.BlockSpec((1, 1, d), lambda i, t: (i, 0, 0)), grouped, t256,
             vec(p['s5_d']), vec(p['glu_w']), vec(p['glu_b']),
             t256, t256, t256, t256, vec(p['rw_ln_g']), vec(p['rw_ln_b']),
             t256, t256, tok(GROUP_W, 3), vec(p['ret_ln_g']),
             t256, t256, tok(GROUP_W, 2), vec(p['gla_ln_g']),
             vec(p['bones']), vec(p['w_out'])]
    return pl.pallas_call(
        _mix_kernel,
        grid=(b, l // tm),
        in_specs=specs,
        out_specs=tok(d, 0),
        out_shape=jax.ShapeDtypeStruct((b, l, d), F32),
        scratch_shapes=[pltpu.VMEM((GROUP_W // LANES, tm // S5_T * (S5_T + 1), LANES), F32)],
        compiler_params=_cparams("parallel", "parallel"),
        name="mix_outproj",
    )(*args)


def _mlp_kernel(final, nff, x_ref, g_ref, sc_ref, sh_ref, gate_ref, w1_ref, w2_ref, fg_ref, o_ref):
    x = x_ref[0]
    hb = _modnorm(x, g_ref[...], sc_ref[0], sh_ref[0]).astype(BF16)
    ff = w1_ref.shape[2] // nff
    acc = None
    for j in range(nff):
        a = jnp.maximum(jnp.dot(hb, w1_ref[0, :, j * ff:(j + 1) * ff], preferred_element_type=F32), 0.0)
        part = jnp.dot((a * a).astype(BF16), w2_ref[0, j * ff:(j + 1) * ff, :], preferred_element_type=F32)
        acc = part if acc is None else acc + part
    y = x + gate_ref[0] * acc
    if final:
        ms = jnp.mean(y * y, axis=-1, keepdims=True)
        y = y * lax.rsqrt(ms + EPS) * fg_ref[...]
    o_ref[0] = y


def _mlp(x, g, sc, sh, gate, w1, w2, layer, final_g, final, tm):
    b, l, d = x.shape
    tok = pl.BlockSpec((1, tm, d), lambda i, t: (i, t, 0))
    vec = pl.BlockSpec((1, 1, d), lambda i, t: (i, 0, 0))
    row = pl.BlockSpec((1, d), lambda i, t: (0, 0))
    once = lambda a: pl.BlockSpec((1,) + a.shape[1:], lambda i, t: (layer, 0, 0), pipeline_mode=pl.Buffered(1))
    return pl.pallas_call(
        functools.partial(_mlp_kernel, final, 4),
        grid=(b, l // tm),
        in_specs=[tok, row, vec, vec, vec, once(w1), once(w2), row],
        out_specs=tok,
        out_shape=jax.ShapeDtypeStruct((b, l, d), F32),
        compiler_params=_cparams("parallel", "parallel"),
        name="mlp",
    )(x, g, sc, sh, gate, w1, w2, final_g)


def kernel(x, c, ctx, c_ctx, ada_w, ada_b, norm1_g, norm2_g, w_in, w_out, s5_lam_re, s5_lam_im, s5_log_dt, s5_b_re, s5_b_im, s5_c_re, s5_c_im, s5_d, s5_glu_w, s5_glu_b, rw_mu, rw_w0, rw_w_up, rw_a0, rw_a_up, rw_g_up, rw_k_k, rw_k_a, rw_r_k, rw_ln_g, rw_ln_b, ret_decay_logit, ret_ln_g, gla_a_up, gla_a_b, gla_ln_g, mlp_w1, mlp_w2, final_g):
    b, l, d = x.shape
    lc = ctx.shape[1]
    depth = ada_w.shape[0]
    rows = l // GRID_W
    assert l % TOKEN_TILE == 0 and lc % RET_T == 0 and lc <= TOKEN_TILE and d % LANES == 0
    assert b < COND_ROWS and l % GRID_W == 0

    cond = jnp.zeros((COND_ROWS, d), F32).at[:b].set(c).at[b].set(c_ctx)
    mod = _modulation(cond, ada_w, ada_b)

    lane = jnp.arange(GROUP_W)
    bones = (lane[:, None] // HEAD_V == lane[None, :] // HEAD_V).astype(BF16)
    rope_lat = _rope_tables(rows)
    ident = (jnp.ones((lc // CHUNK, 1, GROUP_W), F32), jnp.zeros((lc // CHUNK, 1, GROUP_W), F32),
             jnp.ones((GRID_W, GROUP_W), F32), jnp.zeros((GRID_W, GROUP_W), F32))
    nlev = max(1, (l // S5_T - 1).bit_length())
    c0, c1, c2 = GROUP_W, GROUP_W + RW_COLS, GROUP_W + RW_COLS + 4 * GROUP_W
    row2 = lambda a: a.reshape(1, -1).astype(F32)
    w_in_b, w_out_b, w1_b, w2_b = _to_bf16(w_in), _to_bf16(w_out), _to_bf16(mlp_w1), _to_bf16(mlp_w2)

    s5_tab_all = jax.vmap(lambda *a: _s5_tables(*a, nlev))(
        s5_lam_re, s5_lam_im, s5_log_dt, s5_b_re, s5_b_im, s5_c_re, s5_c_im)
    ret_tab_all = jax.vmap(lambda a: _ret_tables(a, RET_T))(ret_decay_logit)
    zpad = lambda a, lo, n: jnp.zeros((depth, 2, LANES, n), F32).at[:, :, lo:lo + a.shape[2]].set(a).astype(BF16)
    rw_wup_all = zpad(rw_w_up, 0, GROUP_W)
    rw_aup_all = zpad(rw_a_up, RW_W_RANK, GROUP_W)
    rw_gup_all = jnp.zeros((depth, LANES, GROUP_W), F32).at[:, RW_W_RANK + RW_A_RANK:].set(rw_g_up).astype(BF16)
    gla_aup_all = zpad(gla_a_up, 0, GLA_QK)

    xc = ctx
    for i in range(depth):
        last = i == depth - 1
        m = mod[i].reshape(COND_ROWS, N_ADA, d)
        ml = m[:b, :, None, :]
        mc = jnp.broadcast_to(m[b][None, :, None, :], (b, N_ADA, 1, d))
        wi = w_in_b[i]
        w5, wr, wt = wi[:, :c0], wi[:, c0:c1], wi[:, c1:c2]
        wg = jnp.pad(wi[:, c2:], ((0, 0), (0, GLA_COLS_PAD - (wi.shape[1] - c2))))
        n1 = row2(norm1_g[i])
        n2 = row2(norm2_g[i])

        s5_tab = tuple(a[i] for a in s5_tab_all)
        rwp = dict(
            mu=row2(rw_mu[i]),
            w0=rw_w0[i].reshape(2, 1, GROUP_W), a0=rw_a0[i].reshape(2, 1, GROUP_W),
            wup=rw_wup_all[i], aup=rw_aup_all[i], gup=rw_gup_all[i],
            kk=row2(rw_k_k[i]), ka=row2(rw_k_a[i]), rk=row2(rw_r_k[i]), bones=bones)
        ret_tab = tuple(a[i] for a in ret_tab_all)
        gla_aup = gla_aup_all[i]
        gla_ab = gla_a_b[i].reshape(2, 1, GLA_QK).astype(F32)
        mixp = dict(s5_d=row2(s5_d[i]), glu_w=s5_glu_w[i].astype(BF16), glu_b=row2(s5_glu_b[i]),
                    rw_ln_g=row2(rw_ln_g[i]), rw_ln_b=row2(rw_ln_b[i]), ret_ln_g=row2(ret_ln_g[i]),
                    gla_ln_g=row2(gla_ln_g[i]), bones=bones,
                    w_out=w_out_b[i].reshape(4, GROUP_W, d))

        def mixers(xx, mm, is_lat, states):
            tm = TOKEN_TILE if is_lat else lc
            z5, z5g, zt, zg, pre = _inproj(xx, n1, mm[:, 1], mm[:, 0], w5, wr, wt, wg,
                                           rope_lat if is_lat else ident, rwp, is_lat, tm)
            y5, h5 = _s5_scan(z5g, s5_tab, states[0])
            yrf, yrb, srw = _rw_scan(pre, states[1])
            otf, otb, sret = _ret_scan(zt, ret_tab, states[2])
            ogf, ogb, sgla = _gla_scan(zg, gla_aup, gla_ab, states[3])
            outs = (z5, y5, pre, (yrf, yrb), zt, (otf, otb), zg, (ogf, ogb))
            return outs, (h5, srw, sret, sgla)

        def block(xx, mm, outs, is_lat, fin):
            tm = TOKEN_TILE if is_lat else lc
            z5, y5, pre, yr, zt, ot, zg, og = outs
            x1 = _mix(xx, mm[:, 2], y5, z5, pre, yr, zt, ot, zg, og, mixp, tm)
            return _mlp(x1, n2, mm[:, 4], mm[:, 3], mm[:, 5], w1_b, w2_b, i, row2(final_g), fin,
                        MLP_TILE if is_lat else lc)

        zeros = (jnp.zeros((b, 2, S5_G, 1, 2 * S5_N), F32),
                 jnp.zeros((b, 2, HEADS // 2, HEAD_V, LANES), F32),
                 jnp.zeros((b, 2, HEADS // 2, HEAD_V, LANES), F32),
                 jnp.zeros((b, 2, HEAD_V, GLA_QK), F32))
        outs_c, st_c = mixers(xc, mc, False, zeros)
        outs_l, _ = mixers(x, ml, True, st_c)
        x = block(x, ml, outs_l, True, last)
        if not last:
            xc = block(xc, mc, outs_c, False, False)
    return x
```

```python
import functools
import math

import jax
import jax.numpy as jnp
from jax import lax
from jax.experimental import pallas as pl
from jax.experimental.pallas import tpu as pltpu

F32 = jnp.float32
BF16 = jnp.bfloat16

LANES = 128
GRID_W = 64
GROUP_W = 256
N_ADA = 6
EPS = 1e-6
GN_EPS = 64e-5
TOKEN_TILE = 512
MLP_TILE = 1024
CHUNK = 64
RW_CPB = 8
RET_T = 256
RET_BLOCK = 1024
GLA_T = 128
GLA_BLOCK = 512
HEADS = 4
HEAD_V = 64
S5_P = 16
S5_G = 16
S5_N = 64
S5_T = 32
S5_TP = S5_T * S5_P
RW_COLS = 896
RW_W_RANK = 32
RW_A_RANK = 32
RW_G_RANK = 64
RET_DK = 64
GLA_DK = 32
GLA_QK = 128
GLA_RANK = 16
GLA_TAU = 16.0
GLA_COLS_PAD = 896
ROPE_BASE = 10000.0
ROPE_HALF = RET_DK // 2
ROPE_NF = RET_DK // 4
COND_ROWS = 8
CAST_ROWS = 256
ADA_TILE = 1536
VMEM_LIMIT = 56 * 1024 * 1024


def _cparams(*sem):
    return pltpu.CompilerParams(dimension_semantics=sem, vmem_limit_bytes=VMEM_LIMIT)


def _mm(a, b):
    return jnp.dot(a.astype(BF16), b.astype(BF16), preferred_element_type=F32)


def _mm_nt(a, b):
    return lax.dot_general(a.astype(BF16), b.astype(BF16), (((1,), (1,)), ((), ())),
                           preferred_element_type=F32)


def _mm_tn(a, b):
    return lax.dot_general(a.astype(BF16), b.astype(BF16), (((0,), (0,)), ((), ())),
                           preferred_element_type=F32)


def _split2(x):
    hi = x.astype(BF16)
    return hi, (x - hi.astype(F32)).astype(BF16)


def _chunk_tri01(n, t, reverse):
    ri = lax.broadcasted_iota(jnp.int32, (n, n), 0)
    ci = lax.broadcasted_iota(jnp.int32, (n, n), 1)
    shift = t.bit_length() - 1
    tri = (ci >= ri) if reverse else (ci <= ri)
    return jnp.where((ri >> shift) == (ci >> shift), jnp.where(tri, 1.0, 0.0), 0.0).astype(BF16)


def _mm_left01(m01, x):
    hi, lo = _split2(x)
    return jnp.dot(m01, hi, preferred_element_type=F32) + jnp.dot(m01, lo, preferred_element_type=F32)


def _mm_right01(x, m01):
    return jnp.dot(x.astype(BF16), m01, preferred_element_type=F32)


def _sigmoid(x):
    return 1.0 / (1.0 + jnp.exp(-x))


def _softplus(x):
    return jnp.maximum(x, 0.0) + jnp.log(1.0 + jnp.exp(-jnp.abs(x)))


def _silu(x):
    return x * _sigmoid(x)


def _gelu_tanh(x):
    return 0.5 * x * (1.0 + jnp.tanh(math.sqrt(2.0 / math.pi) * (x + 0.044715 * x * x * x)))


def _tri_incl(t, reverse):
    ri = lax.broadcasted_iota(jnp.int32, (t, t), 0)
    ci = lax.broadcasted_iota(jnp.int32, (t, t), 1)
    return (ci >= ri) if reverse else (ci <= ri)


def _cast_kernel(x_ref, o_ref):
    o_ref[...] = x_ref[...].astype(o_ref.dtype)


def _to_bf16(w):
    depth, r, c = w.shape
    tr = CAST_ROWS
    spec = pl.BlockSpec((1, tr, c), lambda i, j: (i, j, 0))
    return pl.pallas_call(
        _cast_kernel,
        grid=(depth, r // tr),
        in_specs=[spec],
        out_specs=spec,
        out_shape=jax.ShapeDtypeStruct(w.shape, BF16),
        compiler_params=_cparams("parallel", "parallel"),
        name="cast_bf16",
    )(w)


def _mod_kernel(cond_ref, w_ref, b_ref, o_ref):
    c = cond_ref[...]
    o_ref[0] = _mm(_silu(c), w_ref[0]) + b_ref[0]


def _modulation(cond, ada_w, ada_b):
    depth, d, n = ada_w.shape
    tn = ADA_TILE
    return pl.pallas_call(
        _mod_kernel,
        grid=(depth, n // tn),
        in_specs=[pl.BlockSpec((COND_ROWS, d), lambda i, j: (0, 0)),
                  pl.BlockSpec((1, d, tn), lambda i, j: (i, 0, j)),
                  pl.BlockSpec((1, 1, tn), lambda i, j: (i, 0, j))],
        out_specs=pl.BlockSpec((1, COND_ROWS, tn), lambda i, j: (i, 0, j)),
        out_shape=jax.ShapeDtypeStruct((depth, COND_ROWS, n), F32),
        compiler_params=_cparams("parallel", "parallel"),
        name="adaln_mod",
    )(cond, ada_w, ada_b.reshape(depth, 1, n))


def _modnorm(x, g, sc, sh):
    ms = jnp.mean(x * x, axis=-1, keepdims=True)
    return x * lax.rsqrt(ms + EPS) * g * (1.0 + sc) + sh


def _shift_columns(z, segments):
    tm, width = z.shape
    lane = lax.broadcasted_iota(jnp.int32, (tm, LANES), 1)
    starts = [s for s, _ in segments] + [width]
    cols = []
    for c in range(width // LANES):
        lo, hi = c * LANES, (c + 1) * LANES
        inside = [(max(starts[i], lo), segments[i][1]) for i in range(len(segments))
                  if starts[i] < hi and starts[i + 1] > lo]
        col = inside[-1][1](c)
        for first, fn in reversed(inside[:-1]):
            nxt = [f for f, _ in inside if f > first][0]
            col = jnp.where(lane < nxt - lo, fn(c), col)
        cols.append(col)
    return jnp.concatenate(cols, axis=1)


def _rw_prep_math(grid_shift, z, zp, zn, mu, w0_ref, wup_ref, a0_ref, aup_ref, gup, kkp, ka, rk, bones):
    tm = z.shape[0]
    row = lax.broadcasted_iota(jnp.int32, (tm, LANES), 0)
    col = lambda x, c: x[:, c * LANES:(c + 1) * LANES]
    if grid_shift:
        pos = row & (GRID_W - 1)
        left = lambda c: jnp.where(pos == 0, 0.0, pltpu.roll(col(z, c), 1, 0))
        right = lambda c: jnp.where(pos == GRID_W - 1, 0.0, pltpu.roll(col(z, c), tm - 1, 0))
        up = lambda c: jnp.concatenate([col(zp, c), col(z, c)[:tm - GRID_W]], axis=0)
        down = lambda c: jnp.concatenate([col(z, c)[GRID_W:], col(zn, c)], axis=0)
        q = RW_COLS // 4
        shifted = _shift_columns(z, [(0, left), (q, right), (2 * q, up), (3 * q, down)])
    else:
        prev = lambda c: jnp.where(row == 0, 0.0, pltpu.roll(col(z, c), 1, 0))
        nxt = lambda c: jnp.where(row == tm - 1, 0.0, pltpu.roll(col(z, c), tm - 1, 0))
        shifted = _shift_columns(z, [(0, prev), (RW_COLS // 2, nxt)])
    zm = z + mu * (shifted - z)
    r = zm[:, 0:GROUP_W]
    k = zm[:, GROUP_W:2 * GROUP_W]
    v = zm[:, 2 * GROUP_W:3 * GROUP_W]
    lo = zm[:, 3 * GROUP_W:RW_COLS]
    g = _mm(_sigmoid(lo), gup)
    kk = k * kkp
    kk = kk * lax.rsqrt(_mm_right01(kk * kk, bones) + 1e-12)
    bonus = _mm_right01(r * k * rk, bones) * v
    outs = [r.astype(BF16), v.astype(BF16), kk.astype(BF16), g.astype(BF16), bonus.astype(BF16)]
    tlo = jnp.tanh(lo)
    for d in range(2):
        log_decay = -math.exp(-0.5) * _sigmoid(w0_ref[d] + _mm(tlo, wup_ref[d]))
        a = _sigmoid(a0_ref[d] + _mm(lo, aup_ref[d]))
        outs += [log_decay,
                 (k * (1.0 + (a - 1.0) * ka)).astype(BF16), a.astype(BF16)]
    return outs


def _inproj_kernel(grid_shift, nt, x_ref, xp_ref, xn_ref, g_ref, sc_ref, sh_ref,
                   w5_ref, wr_ref, wt_ref, wg_ref, cr_ref, sr_ref, cc_ref, sn_ref,
                   mu_ref, w0_ref, wup_ref, a0_ref, aup_ref, gup_ref, kk_ref, ka_ref, rk_ref, bones_ref,
                   o5_ref, o5g_ref, ot_ref, og_ref, *rest):
    rw_refs, z5h_ref = rest[:-1], rest[-1]
    j = pl.program_id(1)
    norm = lambda x: _modnorm(x, g_ref[...], sc_ref[0], sh_ref[0]).astype(BF16)
    hb = norm(x_ref[0])
    z5 = jnp.dot(hb, w5_ref[...], preferred_element_type=F32)
    o5_ref[0] = z5
    og_ref[0] = jnp.dot(hb, wg_ref[...], preferred_element_type=F32)
    zr = jnp.dot(hb, wr_ref[...], preferred_element_type=F32)
    zp = zn = None
    if grid_shift:
        zp = jnp.where(j > 0, jnp.dot(norm(xp_ref[0]), wr_ref[...], preferred_element_type=F32), 0.0)
        zn = jnp.where(j < nt - 1, jnp.dot(norm(xn_ref[0]), wr_ref[...], preferred_element_type=F32), 0.0)
    rw = _rw_prep_math(grid_shift, zr, zp, zn, mu_ref[...], w0_ref, wup_ref, a0_ref, aup_ref, gup_ref[...],
                       kk_ref[...], ka_ref[...], rk_ref[...], bones_ref[...])
    for ref, val in zip(rw_refs, rw):
        ref[0] = val
    zt = jnp.dot(hb, wt_ref[...], preferred_element_type=F32)
    lane = lax.broadcasted_iota(jnp.int32, (GRID_W, GROUP_W), 1)
    by_row = (lane & ROPE_HALF) == 0
    nrow = cr_ref.shape[0]
    cos = jnp.concatenate([jnp.where(by_row, cr_ref[j], cc_ref[...]) for j in range(nrow)], axis=0)
    sin = jnp.concatenate([jnp.where(by_row, sr_ref[j], sn_ref[...]) for j in range(nrow)], axis=0)
    ot_ref[0, :, 0:GROUP_W] = _rope(zt[:, 0:GROUP_W], cos, sin)
    ot_ref[0, :, GROUP_W:2 * GROUP_W] = _rope(zt[:, GROUP_W:2 * GROUP_W] * RET_DK ** -0.5, cos, sin)
    ot_ref[0, :, 2 * GROUP_W:] = zt[:, 2 * GROUP_W:]
    nch = o5g_ref.shape[2]
    gph = LANES // S5_P
    for hf in range(GROUP_W // LANES):
        z5h_ref[hf] = z5[:, hf * LANES:(hf + 1) * LANES]
    for s in range(S5_T):
        for hf in range(GROUP_W // LANES):
            rows = z5h_ref[hf, pl.ds(s, nch, stride=S5_T), :]
            for g in range(gph):
                o5g_ref[0, hf * gph + g, :, s * S5_P:(s + 1) * S5_P] = rows[:, g * S5_P:(g + 1) * S5_P]


def _inproj(x, g, sc, sh, w5, wr, wt, wg, rope, rwp, grid_shift, tm):
    b, l, d = x.shape
    nt = l // tm
    hb = tm // GRID_W
    nhb = l // GRID_W
    tok = lambda n: pl.BlockSpec((1, tm, n), lambda i, j: (i, j, 0))
    vec = pl.BlockSpec((1, 1, d), lambda i, j: (i, 0, 0))
    full = lambda a: pl.BlockSpec(a.shape, lambda i, j: (0,) * a.ndim)
    grouped = pl.BlockSpec((1, S5_G, tm // S5_T, S5_TP), lambda i, j: (i, 0, j, 0))
    rowt = pl.BlockSpec((hb, 1, GROUP_W), lambda i, j: (j, 0, 0))
    above = pl.BlockSpec((1, GRID_W, d), lambda i, j: (i, jnp.maximum(j * hb - 1, 0), 0))
    below = pl.BlockSpec((1, GRID_W, d), lambda i, j: (i, jnp.minimum((j + 1) * hb, nhb - 1), 0))
    cr, sr, cc, sn = rope
    params = (rwp['mu'], rwp['w0'], rwp['wup'], rwp['a0'], rwp['aup'], rwp['gup'], rwp['kk'], rwp['ka'],
              rwp['rk'], rwp['bones'])
    rw_dtypes = (BF16,) * 5 + (F32, BF16, BF16) * 2
    outs = pl.pallas_call(
        functools.partial(_inproj_kernel, grid_shift, nt),
        grid=(b, nt),
        in_specs=[tok(d), above, below, pl.BlockSpec((1, d), lambda i, j: (0, 0)), vec, vec,
                  full(w5), full(wr), full(wt), full(wg), rowt, rowt, full(cc), full(sn)]
                 + [full(a) for a in params],
        out_specs=[tok(GROUP_W), grouped, tok(wt.shape[1]), tok(wg.shape[1])] + [tok(GROUP_W)] * 11,
        out_shape=[jax.ShapeDtypeStruct((b, l, GROUP_W), F32),
                   jax.ShapeDtypeStruct((b, S5_G, l // S5_T, S5_TP), F32),
                   jax.ShapeDtypeStruct((b, l, wt.shape[1]), F32),
                   jax.ShapeDtypeStruct((b, l, wg.shape[1]), F32)]
                  + [jax.ShapeDtypeStruct((b, l, GROUP_W), dt) for dt in rw_dtypes],
        scratch_shapes=[pltpu.VMEM((GROUP_W // LANES, tm, LANES), F32)],
        compiler_params=_cparams("parallel", "parallel"),
        name="norm_inproj",
    )(x, x, x, g, sc, sh, w5, wr, wt, wg, cr, sr, cc, sn, *params)
    return outs[0], outs[1], outs[2], outs[3], tuple(outs[4:])


def _s5_tables(lam_re, lam_im, log_dt, b_re, b_im, c_re, c_im, nlev):
    hp = lax.Precision.HIGHEST
    t = S5_T
    lam = lax.complex(jnp.minimum(lam_re.astype(F32), -1e-4), lam_im.astype(F32))
    ldt = lam * jnp.exp(log_dt.astype(F32))[..., None]
    a_bar = jnp.exp(ldt)
    bb = ((a_bar - 1.0) / lam)[..., None] * lax.complex(b_re.astype(F32), b_im.astype(F32))
    cm = lax.complex(c_re.astype(F32), c_im.astype(F32))
    tau = jnp.arange(t + 1, dtype=F32)
    apow = jnp.exp(ldt[:, :, None, :] * tau[None, None, :, None])
    taps = jnp.einsum('dgpn,dgtn,dgnq->dgtpq', cm, apow[:, :, :t], bb, precision=hp).real
    taprow = jnp.stack([taps[0], taps[1][:, ::-1]]).transpose(0, 1, 4, 2, 3).reshape(2, S5_G, S5_P, S5_TP)

    def pack(zc):
        return jnp.concatenate([zc.real, zc.imag], axis=-1)

    win_f = apow[0][:, t - 1 - jnp.arange(t), None, :] * bb[0].transpose(0, 2, 1)[:, None]
    win_b = apow[1][:, jnp.arange(t), None, :] * bb[1].transpose(0, 2, 1)[:, None]
    win = jnp.stack([pack(win_f), pack(win_b)]).reshape(2, S5_G, S5_TP, 2 * S5_N)
    ca_f = cm[0][:, None] * apow[0][:, 1 + jnp.arange(t), None, :]
    ca_b = cm[1][:, None] * apow[1][:, t - jnp.arange(t), None, :]

    def outpack(ca):
        w = jnp.concatenate([ca.real, -ca.imag], axis=-1)
        return w.reshape(S5_G, S5_TP, 2 * S5_N).transpose(0, 2, 1)

    wout = jnp.stack([outpack(ca_f), outpack(ca_b)])
    lev = (2.0 ** jnp.arange(nlev, dtype=F32)) * t
    pw = jnp.exp(ldt[:, :, None, :] * lev[None, None, :, None])
    p1 = jnp.concatenate([pw.real, pw.real], axis=-1)
    p2 = jnp.concatenate([-pw.imag, pw.imag], axis=-1)
    pw = jnp.stack([p1, p2], axis=3)
    return taprow, win.astype(BF16), wout.astype(BF16), pw


def _s5_kernel(nc, nlev, u_ref, tap_ref, win_ref, wout_ref, pw_ref, h0_ref, y_ref, hfin_ref, conv_ref):
    u = u_ref[0, 0].astype(BF16)
    row = lax.broadcasted_iota(jnp.int32, (nc, 2 * S5_N), 0)
    lane = lax.broadcasted_iota(jnp.int32, (S5_P, S5_TP), 1)
    for s in range(S5_T):
        lo = s * S5_P
        fwd = tap_ref[0, 0] if s == 0 else jnp.where(lane >= lo, pltpu.roll(tap_ref[0, 0], lo, 1), 0.0)
        hi = lo + S5_P
        bwd = tap_ref[1, 0] if hi == S5_TP else jnp.where(lane < hi, pltpu.roll(tap_ref[1, 0], hi, 1), 0.0)
        conv_ref[0, lo:hi, :] = fwd.astype(BF16)
        conv_ref[1, lo:hi, :] = bwd.astype(BF16)

    def cmul(x, d, j):
        return pw_ref[d, 0, j, 0:1] * x + pw_ref[d, 0, j, 1:2] * pltpu.roll(x, S5_N, 1)

    v = [jnp.dot(u, win_ref[d, 0], preferred_element_type=F32) for d in range(2)]
    x = [jnp.where(row == 0, h0_ref[0, 0, 0], pltpu.roll(v[0], 1, 0)),
         jnp.where(row == nc - 1, h0_ref[0, 1, 0], pltpu.roll(v[1], nc - 1, 0))]
    for j in range(nlev):
        sh = 2 ** j
        xs = [jnp.where(row >= sh, pltpu.roll(x[0], sh, 0), 0.0),
              jnp.where(row < nc - sh, pltpu.roll(x[1], nc - sh, 0), 0.0)]
        x = [x[d] + cmul(xs[d], d, j) for d in range(2)]
    for d, last in enumerate((nc - 1, 0)):
        hfin_ref[0, d, 0] = cmul(x[d][last:last + 1], d, 0) + v[d][last:last + 1]
    y_ref[0, 0] = (jnp.dot(u, conv_ref[0], preferred_element_type=F32) + _mm(x[0], wout_ref[0, 0])
                   + jnp.dot(u, conv_ref[1], preferred_element_type=F32) + _mm(x[1], wout_ref[1, 0]))


def _s5_scan(uf, tables, h0):
    conv, win, wout, pw = tables
    b, _, nc, _ = uf.shape
    nlev = max(1, (nc - 1).bit_length())
    pw = pw[:, :, :nlev]
    n2 = 2 * S5_N
    y, hfin = pl.pallas_call(
        functools.partial(_s5_kernel, nc, nlev),
        grid=(b, S5_G),
        in_specs=[pl.BlockSpec((1, 1, nc, S5_TP), lambda i, g: (i, g, 0, 0)),
                  pl.BlockSpec((2, 1, S5_P, S5_TP), lambda i, g: (0, g, 0, 0)),
                  pl.BlockSpec((2, 1, S5_TP, n2), lambda i, g: (0, g, 0, 0)),
                  pl.BlockSpec((2, 1, n2, S5_TP), lambda i, g: (0, g, 0, 0)),
                  pl.BlockSpec((2, 1, nlev, 2, n2), lambda i, g: (0, g, 0, 0, 0)),
                  pl.BlockSpec((1, 2, 1, 1, n2), lambda i, g: (i, 0, g, 0, 0))],
        out_specs=[pl.BlockSpec((1, 1, nc, S5_TP), lambda i, g: (i, g, 0, 0)),
                   pl.BlockSpec((1, 2, 1, 1, n2), lambda i, g: (i, 0, g, 0, 0))],
        out_shape=[jax.ShapeDtypeStruct((b, S5_G, nc, S5_TP), F32),
                   jax.ShapeDtypeStruct((b, 2, S5_G, 1, n2), F32)],
        scratch_shapes=[pltpu.VMEM((2, S5_TP, S5_TP), BF16)],
        compiler_params=_cparams("parallel", "parallel"),
        name="s5_scan",
    )(uf, conv, win, wout, pw, h0)
    return y, hfin


def _rw_scan_kernel(nb, cpb, rf_ref, vf_ref, kkf_ref, lwf_ref, kdf_ref, asf_ref,
                    rb_ref, vb_ref, kkb_ref, lwb_ref, kdb_ref, asb_ref, s0_ref,
                    yf_ref, yb_ref, sfin_ref, st_ref):
    c = pl.program_id(1)

    @pl.when(c == 0)
    def _():
        st_ref[...] = s0_ref[0]

    t = CHUNK
    ri = lax.broadcasted_iota(jnp.int32, (t, LANES), 0)
    li = lax.broadcasted_iota(jnp.int32, (t, LANES), 1)
    ci = li & (t - 1)
    low = li < HEAD_V
    eyef = jnp.where(ri == ci, 1.0, 0.0)
    same = lambda s: jnp.where((ri >> s) == (ci >> s), 1.0, 0.0)
    m4, m8, m16, m32 = same(2), same(3), same(4), same(5)
    merge_masks = (m8 - m4, m16 - m8, m32 - m16, 1.0 - m32)

    def bd(x):
        return jnp.concatenate([jnp.where(low, x, 0.0), jnp.where(low, 0.0, x)], axis=0)

    def diag_blocks(full):
        return jnp.where(low, full[0:HEAD_V], full[HEAD_V:2 * HEAD_V])

    dir_refs = ((rf_ref, vf_ref, kkf_ref, lwf_ref, kdf_ref, asf_ref),
                (rb_ref, vb_ref, kkb_ref, lwb_ref, kdb_ref, asb_ref))
    units = []
    for d, (r_ref, v_ref, kk_ref, lw_ref, kd_ref, as_ref) in enumerate(dir_refs):
        incl = (ci >= ri) if d == 1 else (ci <= ri)
        strict = (ci > ri) if d == 1 else (ci < ri)
        r, v, kk, kd = (x[0].astype(F32) for x in (r_ref, v_ref, kk_ref, kd_ref))
        lw = lw_ref[0]
        cin = _mm_left01(_chunk_tri01(cpb * t, t, d == 1), lw)
        e_in = jnp.exp(cin)
        e_neg = jnp.exp(-cin)
        rt = r * e_in
        at = -kk * jnp.exp(cin - lw)
        bvec = kk * as_ref[0].astype(F32)
        bt = bvec * e_neg
        kt = kd * e_neg
        for j in range(cpb):
            rows = slice(j * t, (j + 1) * t)
            last = j * t + (0 if d == 1 else t - 1)
            clast = cin[last:last + 1]
            dl = jnp.exp(clast - cin[rows])
            bh = bvec[rows] * dl
            kh = kd[rows] * dl
            dec = jnp.exp(clast)
            for p in range(HEADS // 2):
                sl = slice(p * LANES, (p + 1) * LANES)
                units.append(dict(d=d, j=j, p=p, incl=incl, strict=strict,
                                  at=at[rows, sl], rt=rt[rows, sl], bt=bt[rows, sl], kt=kt[rows, sl],
                                  bh=bh[:, sl], kh=kh[:, sl], v=v[rows, sl], dec=dec[:, sl]))

    x1 = [jnp.concatenate([u['at'], u['rt']], axis=0) for u in units]
    scores = [_mm_nt(x, jnp.concatenate([bd(u['bt']), bd(u['kt'])], axis=0))
              for x, u in zip(x1, units)]
    nmat = [jnp.where(u['strict'], a[0:t, :LANES], 0.0) for u, a in zip(units, scores)]
    a_rb = [jnp.where(u['incl'], a[t:2 * t, :LANES], 0.0) for u, a in zip(units, scores)]
    a_kk = [jnp.concatenate([jnp.where(u['strict'], a[0:t, LANES:], 0.0),
                             jnp.where(u['incl'], a[t:2 * t, LANES:], 0.0)], axis=0)
            for u, a in zip(units, scores)]
    akv = [_mm(a, bd(u['v'])) for u, a in zip(units, a_kk)]
    kv = [diag_blocks(_mm_tn(u['v'], u['kh'])) for u in units]
    nd = [x * m4 for x in nmat]
    n2 = [_mm(x, bd(x)) for x in nd]
    tinv = [eyef + x + _mm(eyef + x, bd(y)) for x, y in zip(nd, n2)]
    for mk in merge_masks:
        w = [_mm(ti, bd(x * mk)) for ti, x in zip(tinv, nmat)]
        tinv = [ti + _mm(wi, bd(ti)) for ti, wi in zip(tinv, w)]
    zz = [_mm(ti, jnp.concatenate([bd(u['at']), bd(kvv[0:t])], axis=1))
          for ti, u, kvv in zip(tinv, units, akv)]
    ght = [_mm_tn(z, u['bh']) for u, z in zip(units, zz)]
    qy = [_mm(a, jnp.concatenate([bd(z[:, :LANES]), bd(z[:, LANES:])], axis=1))
          for a, z in zip(a_rb, zz)]
    qmat = [u['rt'] + x[:, :LANES] for u, x in zip(units, qy)]
    y0 = [x[:, LANES:] + kvv[t:2 * t] for x, kvv in zip(qy, akv)]
    gmat = [diag_blocks(x[:LANES]) for x in ght]
    hmat = [diag_blocks(x[LANES:]) + k2 for x, k2 in zip(ght, kv)]
    idx = {(u['d'], u['j'], u['p']): n for n, u in enumerate(units)}
    npair = HEADS // 2
    state = {(d, p): st_ref[d, p] for d in range(2) for p in range(npair)}
    ys = {}
    for step in range(cpb):
        for d in range(2):
            j = step if d == 0 else cpb - 1 - step
            for p in range(npair):
                n = idx[(d, j, p)]
                st = state[(d, p)]
                ys[(d, j, p)] = _mm_nt(qmat[n], bd(st)) + y0[n]
                state[(d, p)] = units[n]['dec'] * st + _mm(st, bd(gmat[n])) + hmat[n]
    for d, y_ref in enumerate((yf_ref, yb_ref)):
        y_ref[0] = jnp.concatenate(
            [jnp.concatenate([ys[(d, j, p)] for p in range(npair)], axis=1) for j in range(cpb)], axis=0)
        for p in range(npair):
            st_ref[d, p] = state[(d, p)]

    @pl.when(c == nb - 1)
    def _():
        sfin_ref[0] = st_ref[...]


def _rw_scan(pre, s0):
    r, v, kk, _, _, lw0, kd0, as0, lw1, kd1, as1 = pre
    b, l, _ = r.shape
    cpb = min(RW_CPB, l // CHUNK)
    nb = l // (cpb * CHUNK)
    fw = pl.BlockSpec((1, cpb * CHUNK, GROUP_W), lambda i, c: (i, c, 0))
    bw = pl.BlockSpec((1, cpb * CHUNK, GROUP_W), lambda i, c: (i, nb - 1 - c, 0))
    st = pl.BlockSpec((1, 2, HEADS // 2, HEAD_V, LANES), lambda i, c: (i, 0, 0, 0, 0))
    return pl.pallas_call(
        functools.partial(_rw_scan_kernel, nb, cpb),
        grid=(b, nb),
        in_specs=[fw] * 6 + [bw] * 6 + [st],
        out_specs=[fw, bw, st],
        out_shape=[jax.ShapeDtypeStruct((b, l, GROUP_W), F32)] * 2
                  + [jax.ShapeDtypeStruct((b, 2, HEADS // 2, HEAD_V, LANES), F32)],
        scratch_shapes=[pltpu.VMEM((2, HEADS // 2, HEAD_V, LANES), F32)],
        compiler_params=_cparams("parallel", "arbitrary"),
        name="rwkv_scan",
    )(r, v, kk, lw0, kd0, as0, r, v, kk, lw1, kd1, as1, s0)


def _rope(x, cos, sin_signed):
    lane = lax.broadcasted_iota(jnp.int32, x.shape, 1)
    n = x.shape[1]
    swapped = jnp.where((lane & ROPE_NF) == 0, pltpu.roll(x, n - ROPE_NF, 1), pltpu.roll(x, ROPE_NF, 1))
    return x * cos + swapped * sin_signed


def _ret_scan_kernel(nb, cpb, qf_ref, kf_ref, vf_ref, qb_ref, kb_ref, vb_ref,
                     dmat_ref, qdec_ref, kdec_ref, sdec_ref, s0_ref,
                     of_ref, ob_ref, sfin_ref, st_ref):
    c = pl.program_id(1)

    @pl.when(c == 0)
    def _():
        st_ref[...] = s0_ref[0]

    t = dmat_ref.shape[-1]
    npair = HEADS // 2
    low_t = lax.broadcasted_iota(jnp.int32, (t, LANES), 1) < HEAD_V
    low_s = lax.broadcasted_iota(jnp.int32, (HEAD_V, LANES), 1) < HEAD_V

    def bd(x, low):
        return jnp.concatenate([jnp.where(low, x, 0.0), jnp.where(low, 0.0, x)], axis=0)

    units = []
    for d, (q_ref, k_ref, v_ref) in enumerate(((qf_ref, kf_ref, vf_ref), (qb_ref, kb_ref, vb_ref))):
        for j in range(cpb):
            rows = slice(j * t, (j + 1) * t)
            q, k, v = q_ref[0, rows], k_ref[0, rows], v_ref[0, rows]
            qd = q * qdec_ref[d]
            kh = k * kdec_ref[d]
            for p in range(npair):
                sl = slice(p * LANES, (p + 1) * LANES)
                units.append(dict(d=d, j=j, p=p, q=q[:, sl], k=k[:, sl], qd=qd[:, sl], kh=kh[:, sl],
                                  v=v[:, sl], dec=sdec_ref[d][:, sl]))
    stacked = [_mm_nt(jnp.concatenate([jnp.where(low_t, u['q'], 0.0), jnp.where(low_t, 0.0, u['q'])], axis=0),
                      u['k']) for u in units]
    scores = [[s2[h * t:(h + 1) * t].astype(BF16) * dmat_ref[u['d'], 2 * u['p'] + h] for h in range(2)]
              for s2, u in zip(stacked, units)]
    intra = [_mm(jnp.concatenate(s2, axis=1), bd(u['v'], low_t)) for s2, u in zip(scores, units)]
    kv = []
    for u in units:
        full = _mm_tn(u['v'], u['kh'])
        kv.append(jnp.where(low_s, full[0:HEAD_V], full[HEAD_V:2 * HEAD_V]))
    idx = {(u['d'], u['j'], u['p']): i for i, u in enumerate(units)}
    entering = {}
    for d in range(2):
        for p in range(npair):
            st = st_ref[d, p]
            for step in range(cpb):
                j = step if d == 0 else cpb - 1 - step
                i = idx[(d, j, p)]
                entering[i] = st
                st = units[i]['dec'] * st + kv[i]
            st_ref[d, p] = st
    outs = [x + _mm_nt(u['qd'], bd(entering[i], low_s)) for i, (x, u) in enumerate(zip(intra, units))]
    for d, o_ref in enumerate((of_ref, ob_ref)):
        o_ref[0] = jnp.concatenate(
            [jnp.concatenate([outs[idx[(d, j, p)]] for p in range(npair)], axis=1) for j in range(cpb)], axis=0)

    @pl.when(c == nb - 1)
    def _():
        sfin_ref[0] = st_ref[...]


def _ret_tables(decay_logit, n):
    lg = jax.nn.log_sigmoid(decay_logit.astype(F32))
    pos = jnp.arange(n, dtype=F32)
    lag = pos[:, None] - pos[None, :]
    lag = jnp.stack([lag, -lag])
    dmat = jnp.where(lag[:, None] >= 0, jnp.exp(lg[:, :, None, None] * lag[:, None]), 0.0)
    lanes = jnp.repeat(lg, RET_DK, axis=-1)[:, None, :]
    qpow = jnp.stack([pos + 1.0, n - pos])[:, :, None]
    kpow = jnp.stack([n - 1.0 - pos, pos])[:, :, None]
    return dmat.astype(BF16), jnp.exp(lanes * qpow), jnp.exp(lanes * kpow), jnp.exp(lanes * n)


def _gla_scan_kernel(nb, cpb, qf_ref, kf_ref, vf_ref, af_ref, qb_ref, kb_ref, vb_ref, ab_ref,
                     aup_ref, abias_ref, s0_ref, of_ref, ob_ref, sfin_ref, st_ref):
    c = pl.program_id(1)

    @pl.when(c == 0)
    def _():
        st_ref[...] = s0_ref[0]

    t = GLA_T
    n = cpb * t
    dk = GLA_DK
    scale = dk ** -0.5
    dirs = ((qf_ref, kf_ref, vf_ref, af_ref), (qb_ref, kb_ref, vb_ref, ab_ref))
    units = []
    for d, (q_ref, k_ref, v_ref, a_ref) in enumerate(dirs):
        incl = _tri_incl(t, d == 1)
        q, k, v = q_ref[0], k_ref[0] * scale, v_ref[0]
        lw = -_softplus(-(_mm(a_ref[0], aup_ref[d]) + abias_ref[d])) * (1.0 / GLA_TAU)
        cin = _mm_left01(_chunk_tri01(n, t, d == 1), lw)
        qe = q * jnp.exp(cin)
        for j in range(cpb):
            rows = slice(j * t, (j + 1) * t)
            last = j * t + (0 if d == 1 else t - 1)
            mid = j * t + (t // 2 if d == 1 else t // 2 - 1)
            cj = cin[rows]
            clast = cin[last:last + 1]
            cmid = cin[mid:mid + 1]
            qt = q[rows] * jnp.exp(cj - cmid)
            kt = k[rows] * jnp.exp(cmid - cj)
            kh = k[rows] * jnp.exp(clast - cj)
            units.append(dict(d=d, j=j, incl=incl, qt=qt, kt=kt, qe=qe[rows], kh=kh, v=v[rows],
                              dec=jnp.exp(clast)))

    klane = lax.broadcasted_iota(jnp.int32, (t, LANES), 1) // dk
    vlow = lax.broadcasted_iota(jnp.int32, (t, LANES), 1) < HEAD_V
    khead = lax.broadcasted_iota(jnp.int32, (HEAD_V, LANES), 1) // dk
    npair = HEADS // 2

    def v_blockdiag(x):
        return jnp.concatenate([jnp.where(vlow, x, 0.0), jnp.where(vlow, 0.0, x)], axis=0)

    def state_rows(st, p):
        return jnp.concatenate([jnp.where(khead == 2 * p, st, 0.0), jnp.where(khead == 2 * p + 1, st, 0.0)],
                               axis=0)

    stacked = [_mm_nt(jnp.concatenate([jnp.where(klane == h, u['qt'], 0.0) for h in range(HEADS)], axis=0),
                      u['kt']) for u in units]
    amat = [[jnp.where(u['incl'], s4[h * t:(h + 1) * t], 0.0) for h in range(HEADS)]
            for s4, u in zip(stacked, units)]
    intra = [[_mm(jnp.concatenate(a[2 * p:2 * p + 2], axis=1), v_blockdiag(u['v'][:, p * LANES:(p + 1) * LANES]))
              for p in range(npair)] for a, u in zip(amat, units)]
    kv = []
    for u in units:
        full = _mm_tn(u['v'], u['kh'])
        blocks = [full[h * HEAD_V:(h + 1) * HEAD_V] for h in range(HEADS)]
        acc = blocks[HEADS - 1]
        for h in range(HEADS - 2, -1, -1):
            acc = jnp.where(khead == h, blocks[h], acc)
        kv.append(acc)
    idx = {(u['d'], u['j']): i for i, u in enumerate(units)}
    entering = {}
    for d in range(2):
        st = st_ref[d]
        for step in range(cpb):
            j = step if d == 0 else cpb - 1 - step
            i = idx[(d, j)]
            entering[i] = st
            st = units[i]['dec'] * st + kv[i]
        st_ref[d] = st
    inter = [_mm_nt(u['qe'], jnp.concatenate([state_rows(entering[i], p) for p in range(npair)], axis=0))
             for i, u in enumerate(units)]
    outs = [[x[p] + y[:, p * LANES:(p + 1) * LANES] for p in range(npair)] for x, y in zip(intra, inter)]
    for d, o_ref in enumerate((of_ref, ob_ref)):
        o_ref[0] = jnp.concatenate(
            [jnp.concatenate(outs[idx[(d, j)]], axis=1) for j in range(cpb)], axis=0)

    @pl.when(c == nb - 1)
    def _():
        sfin_ref[0] = st_ref[...]


def _rope_tables(rows):
    nf = ROPE_NF
    inv = ROPE_BASE ** (-jnp.arange(nf, dtype=F32) / nf)
    lane = jnp.arange(GROUP_W)
    freq = inv[lane % nf]
    sign = jnp.where((lane & ROPE_NF) == 0, -1.0, 1.0)
    ar = jnp.arange(rows, dtype=F32)[:, None] * freq[None, :]
    ac = jnp.arange(GRID_W, dtype=F32)[:, None] * freq[None, :]
    return (jnp.cos(ar).reshape(rows, 1, GROUP_W), (jnp.sin(ar) * sign).reshape(rows, 1, GROUP_W),
            jnp.cos(ac), jnp.sin(ac) * sign)


def _ret_scan(z, tables, s0):
    b, l, _ = z.shape
    n = min(RET_BLOCK, l)
    nb = l // n
    fw = lambda j: pl.BlockSpec((1, n, GROUP_W), lambda i, c: (i, c, j))
    bw = lambda j: pl.BlockSpec((1, n, GROUP_W), lambda i, c: (i, nb - 1 - c, j))
    full = lambda a: pl.BlockSpec(a.shape, lambda i, c: (0,) * a.ndim)
    st = pl.BlockSpec((1, 2, HEADS // 2, HEAD_V, LANES), lambda i, c: (i, 0, 0, 0, 0))
    return pl.pallas_call(
        functools.partial(_ret_scan_kernel, nb, n // RET_T),
        grid=(b, nb),
        in_specs=[fw(0), fw(1), fw(2), bw(0), bw(1), bw(2)] + [full(a) for a in tables] + [st],
        out_specs=[fw(0), bw(0), st],
        out_shape=[jax.ShapeDtypeStruct((b, l, GROUP_W), F32)] * 2
                  + [jax.ShapeDtypeStruct((b, 2, HEADS // 2, HEAD_V, LANES), F32)],
        scratch_shapes=[pltpu.VMEM((2, HEADS // 2, HEAD_V, LANES), F32)],
        compiler_params=_cparams("parallel", "arbitrary"),
        name="retention_scan",
    )(z, z, z, z, z, z, *tables, s0)


def _gla_scan(z, aup, abias, s0):
    b, l, _ = z.shape
    cpb = min(GLA_BLOCK, l) // CHUNK
    nb = l // (cpb * CHUNK)
    blk = lambda w, j, rev: pl.BlockSpec(
        (1, cpb * CHUNK, w), (lambda i, c: (i, nb - 1 - c, j)) if rev else (lambda i, c: (i, c, j)))
    st = pl.BlockSpec((1, 2, HEAD_V, GLA_QK), lambda i, c: (i, 0, 0, 0))
    ofw = pl.BlockSpec((1, cpb * CHUNK, GROUP_W), lambda i, c: (i, c, 0))
    obw = pl.BlockSpec((1, cpb * CHUNK, GROUP_W), lambda i, c: (i, nb - 1 - c, 0))
    return pl.pallas_call(
        functools.partial(_gla_scan_kernel, nb, cpb * CHUNK // GLA_T),
        grid=(b, nb),
        in_specs=[blk(GLA_QK, 0, False), blk(GLA_QK, 1, False), blk(GROUP_W, 1, False), blk(128, 6, False),
                  blk(GLA_QK, 0, True), blk(GLA_QK, 1, True), blk(GROUP_W, 1, True), blk(128, 6, True),
                  pl.BlockSpec(aup.shape, lambda i, c: (0, 0, 0)),
                  pl.BlockSpec(abias.shape, lambda i, c: (0, 0, 0)), st],
        out_specs=[ofw, obw, st],
        out_shape=[jax.ShapeDtypeStruct((b, l, GROUP_W), F32)] * 2
                  + [jax.ShapeDtypeStruct((b, 2, HEAD_V, GLA_QK), F32)],
        scratch_shapes=[pltpu.VMEM((2, HEAD_V, GLA_QK), F32)],
        compiler_params=_cparams("parallel", "arbitrary"),
        name="gla_scan",
    )(z, z, z, z, z, z, z, z, aup, abias, s0)


def _mix_kernel(x_ref, g1_ref, y5_ref, u5_ref, d5_ref, gw_ref, gb_ref,
                ryf_ref, ryb_ref, rbonus_ref, rg_ref, rlng_ref, rlnb_ref,
                tof_ref, tob_ref, tg_ref, tln_ref,
                gof_ref, gob_ref, gg_ref, gln_ref,
                bones_ref, wo_ref, o_ref, y5t_ref):
    bones = bones_ref[...]
    inv = 1.0 / HEAD_V

    def hmean(a):
        return _mm_right01(a, bones) * inv

    nch = y5_ref.shape[2]
    gph = LANES // S5_P
    y5g = [y5_ref[0, g] for g in range(S5_G)]
    pitch = S5_T + 1
    for s in range(S5_T):
        for hf in range(GROUP_W // LANES):
            y5t_ref[hf, pl.ds(s, nch, stride=pitch), :] = jnp.concatenate(
                [yg[:, s * S5_P:(s + 1) * S5_P] for yg in y5g[hf * gph:(hf + 1) * gph]], axis=1)
    y5t = jnp.concatenate(
        [jnp.concatenate([y5t_ref[hf, ch * pitch:ch * pitch + S5_T] for ch in range(nch)], axis=0)
         for hf in range(GROUP_W // LANES)], axis=1)
    y = y5t + d5_ref[...] * u5_ref[0]
    y = _gelu_tanh(y)
    ya = y * _sigmoid(_mm(y, gw_ref[...]) + gb_ref[...])
    yr = ryf_ref[0] + ryb_ref[0]
    dlt = yr - hmean(yr)
    yn = dlt * lax.rsqrt(hmean(dlt * dlt) + GN_EPS)
    yb = (yn * rlng_ref[...] + rlnb_ref[...] + rbonus_ref[0].astype(F32)) * rg_ref[0].astype(F32)
    ot = tof_ref[0] + tob_ref[0]
    yc = ot * lax.rsqrt(hmean(ot * ot) + EPS) * tln_ref[...] * _silu(tg_ref[0])
    og = gof_ref[0] + gob_ref[0]
    yd = og * lax.rsqrt(hmean(og * og) + EPS) * gln_ref[...] * _silu(gg_ref[0])
    mix = (_mm(ya, wo_ref[0]) + _mm(yb, wo_ref[1]) + _mm(yc, wo_ref[2]) + _mm(yd, wo_ref[3]))
    o_ref[0] = x_ref[0] + g1_ref[0] * mix


def _mix(x, g1, y5, u5, rw_pre, rw_y, zret, ret_o, zgla, gla_o, p, tm):
    b, l, d = x.shape
    tok = lambda w, j: pl.BlockSpec((1, tm, w), lambda i, t: (i, t, j))
    vec = lambda a: pl.BlockSpec(a.shape, lambda i, t: (0,) * a.ndim)
    t256 = tok(GROUP_W, 0)
    args = [x, g1, y5, u5, p['s5_d'], p['glu_w'], p['glu_b'],
            rw_y[0], rw_y[1], rw_pre[4], rw_pre[3], p['rw_ln_g'], p['rw_ln_b'],
            ret_o[0], ret_o[1], zret, p['ret_ln_g'],
            gla_o[0], gla_o[1], zgla, p['gla_ln_g'],
            p['bones'], p['w_out']]
    grouped = pl.BlockSpec((1, S5_G, tm // S5_T, S5_TP), lambda i, t: (i, 0, t, 0))
    specs = [tok(d, 0), pl.BlockSpec((1, 1, d), lambda i, t: (i, 0, 0)), grouped, t256,
             vec(p['s5_d']), vec(p['glu_w']), vec(p['glu_b']),
             t256, t256, t256, t256, vec(p['rw_ln_g']), vec(p['rw_ln_b']),
             t256, t256, tok(GROUP_W, 3), vec(p['ret_ln_g']),
             t256, t256, tok(GROUP_W, 2), vec(p['gla_ln_g']),
             vec(p['bones']), vec(p['w_out'])]
    return pl.pallas_call(
        _mix_kernel,
        grid=(b, l // tm),
        in_specs=specs,
        out_specs=tok(d, 0),
        out_shape=jax.ShapeDtypeStruct((b, l, d), F32),
        scratch_shapes=[pltpu.VMEM((GROUP_W // LANES, tm // S5_T * (S5_T + 1), LANES), F32)],
        compiler_params=_cparams("parallel", "parallel"),
        name="mix_outproj",
    )(*args)


def _mlp_kernel(final, nff, x_ref, g_ref, sc_ref, sh_ref, gate_ref, w1_ref, w2_ref, fg_ref, o_ref):
    x = x_ref[0]
    hb = _modnorm(x, g_ref[...], sc_ref[0], sh_ref[0]).astype(BF16)
    ff = w1_ref.shape[2] // nff
    acc = None
    for j in range(nff):
        a = jnp.maximum(jnp.dot(hb, w1_ref[0, :, j * ff:(j + 1) * ff], preferred_element_type=F32), 0.0)
        part = jnp.dot((a * a).astype(BF16), w2_ref[0, j * ff:(j + 1) * ff, :], preferred_element_type=F32)
        acc = part if acc is None else acc + part
    y = x + gate_ref[0] * acc
    if final:
        ms = jnp.mean(y * y, axis=-1, keepdims=True)
        y = y * lax.rsqrt(ms + EPS) * fg_ref[...]
    o_ref[0] = y


def _mlp(x, g, sc, sh, gate, w1, w2, layer, final_g, final, tm):
    b, l, d = x.shape
    tok = pl.BlockSpec((1, tm, d), lambda i, t: (i, t, 0))
    vec = pl.BlockSpec((1, 1, d), lambda i, t: (i, 0, 0))
    row = pl.BlockSpec((1, d), lambda i, t: (0, 0))
    once = lambda a: pl.BlockSpec((1,) + a.shape[1:], lambda i, t: (layer, 0, 0), pipeline_mode=pl.Buffered(1))
    return pl.pallas_call(
        functools.partial(_mlp_kernel, final, 4),
        grid=(b, l // tm),
        in_specs=[tok, row, vec, vec, vec, once(w1), once(w2), row],
        out_specs=tok,
        out_shape=jax.ShapeDtypeStruct((b, l, d), F32),
        compiler_params=_cparams("parallel", "parallel"),
        name="mlp",
    )(x, g, sc, sh, gate, w1, w2, final_g)


def kernel(x, c, ctx, c_ctx, ada_w, ada_b, norm1_g, norm2_g, w_in, w_out, s5_lam_re, s5_lam_im, s5_log_dt, s5_b_re, s5_b_im, s5_c_re, s5_c_im, s5_d, s5_glu_w, s5_glu_b, rw_mu, rw_w0, rw_w_up, rw_a0, rw_a_up, rw_g_up, rw_k_k, rw_k_a, rw_r_k, rw_ln_g, rw_ln_b, ret_decay_logit, ret_ln_g, gla_a_up, gla_a_b, gla_ln_g, mlp_w1, mlp_w2, final_g):
    b, l, d = x.shape
    lc = ctx.shape[1]
    depth = ada_w.shape[0]
    rows = l // GRID_W
    assert l % TOKEN_TILE == 0 and lc % RET_T == 0 and lc <= TOKEN_TILE and d % LANES == 0
    assert b < COND_ROWS and l % GRID_W == 0

    cond = jnp.zeros((COND_ROWS, d), F32).at[:b].set(c).at[b].set(c_ctx)
    mod = _modulation(cond, ada_w, ada_b)

    lane = jnp.arange(GROUP_W)
    bones = (lane[:, None] // HEAD_V == lane[None, :] // HEAD_V).astype(BF16)
    rope_lat = _rope_tables(rows)
    ident = (jnp.ones((lc // CHUNK, 1, GROUP_W), F32), jnp.zeros((lc // CHUNK, 1, GROUP_W), F32),
             jnp.ones((GRID_W, GROUP_W), F32), jnp.zeros((GRID_W, GROUP_W), F32))
    nlev = max(1, (l // S5_T - 1).bit_length())
    c0, c1, c2 = GROUP_W, GROUP_W + RW_COLS, GROUP_W + RW_COLS + 4 * GROUP_W
    row2 = lambda a: a.reshape(1, -1).astype(F32)
    w_in_b, w_out_b, w1_b, w2_b = _to_bf16(w_in), _to_bf16(w_out), _to_bf16(mlp_w1), _to_bf16(mlp_w2)

    s5_tab_all = jax.vmap(lambda *a: _s5_tables(*a, nlev))(
        s5_lam_re, s5_lam_im, s5_log_dt, s5_b_re, s5_b_im, s5_c_re, s5_c_im)
    ret_tab_all = jax.vmap(lambda a: _ret_tables(a, RET_T))(ret_decay_logit)
    zpad = lambda a, lo, n: jnp.zeros((depth, 2, LANES, n), F32).at[:, :, lo:lo + a.shape[2]].set(a).astype(BF16)
    rw_wup_all = zpad(rw_w_up, 0, GROUP_W)
    rw_aup_all = zpad(rw_a_up, RW_W_RANK, GROUP_W)
    rw_gup_all = jnp.zeros((depth, LANES, GROUP_W), F32).at[:, RW_W_RANK + RW_A_RANK:].set(rw_g_up).astype(BF16)
    gla_aup_all = zpad(gla_a_up, 0, GLA_QK)

    xc = ctx
    for i in range(depth):
        last = i == depth - 1
        m = mod[i].reshape(COND_ROWS, N_ADA, d)
        ml = m[:b, :, None, :]
        mc = jnp.broadcast_to(m[b][None, :, None, :], (b, N_ADA, 1, d))
        wi = w_in_b[i]
        w5, wr, wt = wi[:, :c0], wi[:, c0:c1], wi[:, c1:c2]
        wg = jnp.pad(wi[:, c2:], ((0, 0), (0, GLA_COLS_PAD - (wi.shape[1] - c2))))
        n1 = row2(norm1_g[i])
        n2 = row2(norm2_g[i])

        s5_tab = tuple(a[i] for a in s5_tab_all)
        rwp = dict(
            mu=row2(rw_mu[i]),
            w0=rw_w0[i].reshape(2, 1, GROUP_W), a0=rw_a0[i].reshape(2, 1, GROUP_W),
            wup=rw_wup_all[i], aup=rw_aup_all[i], gup=rw_gup_all[i],
            kk=row2(rw_k_k[i]), ka=row2(rw_k_a[i]), rk=row2(rw_r_k[i]), bones=bones)
        ret_tab = tuple(a[i] for a in ret_tab_all)
        gla_aup = gla_aup_all[i]
        gla_ab = gla_a_b[i].reshape(2, 1, GLA_QK).astype(F32)
        mixp = dict(s5_d=row2(s5_d[i]), glu_w=s5_glu_w[i].astype(BF16), glu_b=row2(s5_glu_b[i]),
                    rw_ln_g=row2(rw_ln_g[i]), rw_ln_b=row2(rw_ln_b[i]), ret_ln_g=row2(ret_ln_g[i]),
                    gla_ln_g=row2(gla_ln_g[i]), bones=bones,
                    w_out=w_out_b[i].reshape(4, GROUP_W, d))

        def mixers(xx, mm, is_lat, states):
            tm = TOKEN_TILE if is_lat else lc
            z5, z5g, zt, zg, pre = _inproj(xx, n1, mm[:, 1], mm[:, 0], w5, wr, wt, wg,
                                           rope_lat if is_lat else ident, rwp, is_lat, tm)
            y5, h5 = _s5_scan(z5g, s5_tab, states[0])
            yrf, yrb, srw = _rw_scan(pre, states[1])
            otf, otb, sret = _ret_scan(zt, ret_tab, states[2])
            ogf, ogb, sgla = _gla_scan(zg, gla_aup, gla_ab, states[3])
            outs = (z5, y5, pre, (yrf, yrb), zt, (otf, otb), zg, (ogf, ogb))
            return outs, (h5, srw, sret, sgla)

        def block(xx, mm, outs, is_lat, fin):
            tm = TOKEN_TILE if is_lat else lc
            z5, y5, pre, yr, zt, ot, zg, og = outs
            x1 = _mix(xx, mm[:, 2], y5, z5, pre, yr, zt, ot, zg, og, mixp, tm)
            return _mlp(x1, n2, mm[:, 4], mm[:, 3], mm[:, 5], w1_b, w2_b, i, row2(final_g), fin,
                        MLP_TILE if is_lat else lc)

        zeros = (jnp.zeros((b, 2, S5_G, 1, 2 * S5_N), F32),
                 jnp.zeros((b, 2, HEADS // 2, HEAD_V, LANES), F32),
                 jnp.zeros((b, 2, HEADS // 2, HEAD_V, LANES), F32),
                 jnp.zeros((b, 2, HEAD_V, GLA_QK), F32))
        outs_c, st_c = mixers(xc, mc, False, zeros)
        outs_l, _ = mixers(x, ml, True, st_c)
        x = block(x, ml, outs_l, True, last)
        if not last:
            xc = block(xc, mc, outs_c, False, False)
    return x
```

```python
import functools
import math

import jax
import jax.numpy as jnp
from jax import lax
from jax.experimental import pallas as pl
from jax.experimental.pallas import tpu as pltpu

F32 = jnp.float32
BF16 = jnp.bfloat16

LANES = 128
GRID_W = 64
GROUP_W = 256
N_ADA = 6
EPS = 1e-6
GN_EPS = 64e-5
TOKEN_TILE = 512
MLP_TILE = 1024
CHUNK = 64
RW_CPB = 8
RET_T = 256
RET_BLOCK = 2048
GLA_T = 128
GLA_BLOCK = 512
HEADS = 4
HEAD_V = 64
S5_P = 16
S5_G = 16
S5_N = 64
S5_T = 32
S5_TP = S5_T * S5_P
RW_COLS = 896
RW_W_RANK = 32
RW_A_RANK = 32
RW_G_RANK = 64
RET_DK = 64
GLA_DK = 32
GLA_QK = 128
GLA_RANK = 16
GLA_TAU = 16.0
GLA_COLS_PAD = 896
ROPE_BASE = 10000.0
ROPE_HALF = RET_DK // 2
ROPE_NF = RET_DK // 4
COND_ROWS = 8
CAST_ROWS = 512
ADA_TILE = 1536
VMEM_LIMIT = 56 * 1024 * 1024


def _cparams(*sem):
    return pltpu.CompilerParams(dimension_semantics=sem, vmem_limit_bytes=VMEM_LIMIT)


def _mm(a, b):
    return jnp.dot(a.astype(BF16), b.astype(BF16), preferred_element_type=F32)


def _mm_nt(a, b):
    return lax.dot_general(a.astype(BF16), b.astype(BF16), (((1,), (1,)), ((), ())),
                           preferred_element_type=F32)


def _mm_tn(a, b):
    return lax.dot_general(a.astype(BF16), b.astype(BF16), (((0,), (0,)), ((), ())),
                           preferred_element_type=F32)


def _split2(x):
    hi = x.astype(BF16)
    return hi, (x - hi.astype(F32)).astype(BF16)


def _chunk_tri01(n, t, reverse):
    ri = lax.broadcasted_iota(jnp.int32, (n, n), 0)
    ci = lax.broadcasted_iota(jnp.int32, (n, n), 1)
    shift = t.bit_length() - 1
    tri = (ci >= ri) if reverse else (ci <= ri)
    return jnp.where((ri >> shift) == (ci >> shift), jnp.where(tri, 1.0, 0.0), 0.0).astype(BF16)


def _mm_left01(m01, x):
    hi, lo = _split2(x)
    return jnp.dot(m01, hi, preferred_element_type=F32) + jnp.dot(m01, lo, preferred_element_type=F32)


def _mm_right01(x, m01):
    return jnp.dot(x.astype(BF16), m01, preferred_element_type=F32)


def _sigmoid(x):
    return 1.0 / (1.0 + jnp.exp(-x))


def _softplus(x):
    return jnp.maximum(x, 0.0) + jnp.log(1.0 + jnp.exp(-jnp.abs(x)))


def _silu(x):
    return x * _sigmoid(x)


def _gelu_tanh(x):
    return 0.5 * x * (1.0 + jnp.tanh(math.sqrt(2.0 / math.pi) * (x + 0.044715 * x * x * x)))


def _tri_incl(t, reverse):
    ri = lax.broadcasted_iota(jnp.int32, (t, t), 0)
    ci = lax.broadcasted_iota(jnp.int32, (t, t), 1)
    return (ci >= ri) if reverse else (ci <= ri)


def _cast_kernel(x_ref, o_ref):
    o_ref[...] = x_ref[...].astype(o_ref.dtype)


def _to_bf16(w):
    depth, r, c = w.shape
    tr = CAST_ROWS
    spec = pl.BlockSpec((1, tr, c), lambda i, j: (i, j, 0))
    return pl.pallas_call(
        _cast_kernel,
        grid=(depth, r // tr),
        in_specs=[spec],
        out_specs=spec,
        out_shape=jax.ShapeDtypeStruct(w.shape, BF16),
        compiler_params=_cparams("parallel", "parallel"),
        name="cast_bf16",
    )(w)


def _mod_kernel(cond_ref, w_ref, b_ref, o_ref):
    c = cond_ref[...]
    o_ref[0] = _mm(_silu(c), w_ref[0]) + b_ref[0]


def _modulation(cond, ada_w, ada_b):
    depth, d, n = ada_w.shape
    tn = ADA_TILE
    return pl.pallas_call(
        _mod_kernel,
        grid=(depth, n // tn),
        in_specs=[pl.BlockSpec((COND_ROWS, d), lambda i, j: (0, 0)),
                  pl.BlockSpec((1, d, tn), lambda i, j: (i, 0, j)),
                  pl.BlockSpec((1, 1, tn), lambda i, j: (i, 0, j))],
        out_specs=pl.BlockSpec((1, COND_ROWS, tn), lambda i, j: (i, 0, j)),
        out_shape=jax.ShapeDtypeStruct((depth, COND_ROWS, n), F32),
        compiler_params=_cparams("parallel", "parallel"),
        name="adaln_mod",
    )(cond, ada_w, ada_b.reshape(depth, 1, n))


def _modnorm(x, g, sc, sh):
    ms = jnp.mean(x * x, axis=-1, keepdims=True)
    return x * lax.rsqrt(ms + EPS) * g * (1.0 + sc) + sh


def _shift_columns(z, segments):
    tm, width = z.shape
    lane = lax.broadcasted_iota(jnp.int32, (tm, LANES), 1)
    starts = [s for s, _ in segments] + [width]
    cols = []
    for c in range(width // LANES):
        lo, hi = c * LANES, (c + 1) * LANES
        inside = [(max(starts[i], lo), segments[i][1]) for i in range(len(segments))
                  if starts[i] < hi and starts[i + 1] > lo]
        col = inside[-1][1](c)
        for first, fn in reversed(inside[:-1]):
            nxt = [f for f, _ in inside if f > first][0]
            col = jnp.where(lane < nxt - lo, fn(c), col)
        cols.append(col)
    return jnp.concatenate(cols, axis=1)


def _rw_prep_math(grid_shift, z, zp, zn, mu, w0_ref, wup_ref, a0_ref, aup_ref, gup, kkp, ka, rk, bones):
    tm = z.shape[0]
    row = lax.broadcasted_iota(jnp.int32, (tm, LANES), 0)
    col = lambda x, c: x[:, c * LANES:(c + 1) * LANES]
    if grid_shift:
        pos = row & (GRID_W - 1)
        left = lambda c: jnp.where(pos == 0, 0.0, pltpu.roll(col(z, c), 1, 0))
        right = lambda c: jnp.where(pos == GRID_W - 1, 0.0, pltpu.roll(col(z, c), tm - 1, 0))
        up = lambda c: jnp.concatenate([col(zp, c), col(z, c)[:tm - GRID_W]], axis=0)
        down = lambda c: jnp.concatenate([col(z, c)[GRID_W:], col(zn, c)], axis=0)
        q = RW_COLS // 4
        shifted = _shift_columns(z, [(0, left), (q, right), (2 * q, up), (3 * q, down)])
    else:
        prev = lambda c: jnp.where(row == 0, 0.0, pltpu.roll(col(z, c), 1, 0))
        nxt = lambda c: jnp.where(row == tm - 1, 0.0, pltpu.roll(col(z, c), tm - 1, 0))
        shifted = _shift_columns(z, [(0, prev), (RW_COLS // 2, nxt)])
    zm = z + mu * (shifted - z)
    r = zm[:, 0:GROUP_W]
    k = zm[:, GROUP_W:2 * GROUP_W]
    v = zm[:, 2 * GROUP_W:3 * GROUP_W]
    lo = zm[:, 3 * GROUP_W:RW_COLS]
    g = _mm(_sigmoid(lo), gup)
    kk = k * kkp
    kk = kk * lax.rsqrt(_mm_right01(kk * kk, bones) + 1e-12)
    bonus = _mm_right01(r * k * rk, bones) * v
    outs = [r.astype(BF16), v.astype(BF16), kk.astype(BF16), g.astype(BF16), bonus.astype(BF16)]
    tlo = jnp.tanh(lo)
    for d in range(2):
        log_decay = -math.exp(-0.5) * _sigmoid(w0_ref[d] + _mm(tlo, wup_ref[d]))
        a = _sigmoid(a0_ref[d] + _mm(lo, aup_ref[d]))
        outs += [log_decay,
                 (k * (1.0 + (a - 1.0) * ka)).astype(BF16), a.astype(BF16)]
    return outs


def _inproj_kernel(grid_shift, nt, x_ref, xp_ref, xn_ref, g_ref, sc_ref, sh_ref,
                   w5_ref, wr_ref, wt_ref, wg_ref, cr_ref, sr_ref, cc_ref, sn_ref,
                   mu_ref, w0_ref, wup_ref, a0_ref, aup_ref, gup_ref, kk_ref, ka_ref, rk_ref, bones_ref,
                   o5_ref, o5g_ref, ot_ref, og_ref, *rest):
    rw_refs, z5h_ref = rest[:-1], rest[-1]
    j = pl.program_id(1)
    norm = lambda x: _modnorm(x, g_ref[...], sc_ref[0], sh_ref[0]).astype(BF16)
    hb = norm(x_ref[0])
    z5 = jnp.dot(hb, w5_ref[...], preferred_element_type=F32)
    o5_ref[0] = z5
    og_ref[0] = jnp.dot(hb, wg_ref[...], preferred_element_type=F32)
    zr = jnp.dot(hb, wr_ref[...], preferred_element_type=F32)
    zp = zn = None
    if grid_shift:
        zp = jnp.where(j > 0, jnp.dot(norm(xp_ref[0]), wr_ref[...], preferred_element_type=F32), 0.0)
        zn = jnp.where(j < nt - 1, jnp.dot(norm(xn_ref[0]), wr_ref[...], preferred_element_type=F32), 0.0)
    rw = _rw_prep_math(grid_shift, zr, zp, zn, mu_ref[...], w0_ref, wup_ref, a0_ref, aup_ref, gup_ref[...],
                       kk_ref[...], ka_ref[...], rk_ref[...], bones_ref[...])
    for ref, val in zip(rw_refs, rw):
        ref[0] = val
    zt = jnp.dot(hb, wt_ref[...], preferred_element_type=F32)
    lane = lax.broadcasted_iota(jnp.int32, (GRID_W, GROUP_W), 1)
    by_row = (lane & ROPE_HALF) == 0
    nrow = cr_ref.shape[0]
    cos = jnp.concatenate([jnp.where(by_row, cr_ref[j], cc_ref[...]) for j in range(nrow)], axis=0)
    sin = jnp.concatenate([jnp.where(by_row, sr_ref[j], sn_ref[...]) for j in range(nrow)], axis=0)
    ot_ref[0, :, 0:GROUP_W] = _rope(zt[:, 0:GROUP_W], cos, sin)
    ot_ref[0, :, GROUP_W:2 * GROUP_W] = _rope(zt[:, GROUP_W:2 * GROUP_W] * RET_DK ** -0.5, cos, sin)
    ot_ref[0, :, 2 * GROUP_W:] = zt[:, 2 * GROUP_W:]
    nch = o5g_ref.shape[2]
    gph = LANES // S5_P
    for hf in range(GROUP_W // LANES):
        z5h_ref[hf] = z5[:, hf * LANES:(hf + 1) * LANES]
    for s in range(S5_T):
        for hf in range(GROUP_W // LANES):
            rows = z5h_ref[hf, pl.ds(s, nch, stride=S5_T), :]
            for g in range(gph):
                o5g_ref[0, hf * gph + g, :, s * S5_P:(s + 1) * S5_P] = rows[:, g * S5_P:(g + 1) * S5_P]


def _inproj(x, g, sc, sh, w5, wr, wt, wg, rope, rwp, grid_shift, tm):
    b, l, d = x.shape
    nt = l // tm
    hb = tm // GRID_W
    nhb = l // GRID_W
    tok = lambda n: pl.BlockSpec((1, tm, n), lambda i, j: (i, j, 0))
    vec = pl.BlockSpec((1, 1, d), lambda i, j: (i, 0, 0))
    full = lambda a: pl.BlockSpec(a.shape, lambda i, j: (0,) * a.ndim)
    grouped = pl.BlockSpec((1, S5_G, tm // S5_T, S5_TP), lambda i, j: (i, 0, j, 0))
    rowt = pl.BlockSpec((hb, 1, GROUP_W), lambda i, j: (j, 0, 0))
    above = pl.BlockSpec((1, GRID_W, d), lambda i, j: (i, jnp.maximum(j * hb - 1, 0), 0))
    below = pl.BlockSpec((1, GRID_W, d), lambda i, j: (i, jnp.minimum((j + 1) * hb, nhb - 1), 0))
    cr, sr, cc, sn = rope
    params = (rwp['mu'], rwp['w0'], rwp['wup'], rwp['a0'], rwp['aup'], rwp['gup'], rwp['kk'], rwp['ka'],
              rwp['rk'], rwp['bones'])
    rw_dtypes = (BF16,) * 5 + (F32, BF16, BF16) * 2
    outs = pl.pallas_call(
        functools.partial(_inproj_kernel, grid_shift, nt),
        grid=(b, nt),
        in_specs=[tok(d), above, below, pl.BlockSpec((1, d), lambda i, j: (0, 0)), vec, vec,
                  full(w5), full(wr), full(wt), full(wg), rowt, rowt, full(cc), full(sn)]
                 + [full(a) for a in params],
        out_specs=[tok(GROUP_W), grouped, tok(wt.shape[1]), tok(wg.shape[1])] + [tok(GROUP_W)] * 11,
        out_shape=[jax.ShapeDtypeStruct((b, l, GROUP_W), F32),
                   jax.ShapeDtypeStruct((b, S5_G, l // S5_T, S5_TP), F32),
                   jax.ShapeDtypeStruct((b, l, wt.shape[1]), F32),
                   jax.ShapeDtypeStruct((b, l, wg.shape[1]), F32)]
                  + [jax.ShapeDtypeStruct((b, l, GROUP_W), dt) for dt in rw_dtypes],
        scratch_shapes=[pltpu.VMEM((GROUP_W // LANES, tm, LANES), F32)],
        compiler_params=_cparams("parallel", "parallel"),
        name="norm_inproj",
    )(x, x, x, g, sc, sh, w5, wr, wt, wg, cr, sr, cc, sn, *params)
    return outs[0], outs[1], outs[2], outs[3], tuple(outs[4:])


def _s5_tables(lam_re, lam_im, log_dt, b_re, b_im, c_re, c_im, nlev):
    hp = lax.Precision.HIGHEST
    t = S5_T
    lam = lax.complex(jnp.minimum(lam_re.astype(F32), -1e-4), lam_im.astype(F32))
    ldt = lam * jnp.exp(log_dt.astype(F32))[..., None]
    a_bar = jnp.exp(ldt)
    bb = ((a_bar - 1.0) / lam)[..., None] * lax.complex(b_re.astype(F32), b_im.astype(F32))
    cm = lax.complex(c_re.astype(F32), c_im.astype(F32))
    tau = jnp.arange(t + 1, dtype=F32)
    apow = jnp.exp(ldt[:, :, None, :] * tau[None, None, :, None])
    taps = jnp.einsum('dgpn,dgtn,dgnq->dgtpq', cm, apow[:, :, :t], bb, precision=hp).real
    taprow = jnp.stack([taps[0], taps[1][:, ::-1]]).transpose(0, 1, 4, 2, 3).reshape(2, S5_G, S5_P, S5_TP)

    def pack(zc):
        return jnp.concatenate([zc.real, zc.imag], axis=-1)

    win_f = apow[0][:, t - 1 - jnp.arange(t), None, :] * bb[0].transpose(0, 2, 1)[:, None]
    win_b = apow[1][:, jnp.arange(t), None, :] * bb[1].transpose(0, 2, 1)[:, None]
    win = jnp.stack([pack(win_f), pack(win_b)]).reshape(2, S5_G, S5_TP, 2 * S5_N)
    ca_f = cm[0][:, None] * apow[0][:, 1 + jnp.arange(t), None, :]
    ca_b = cm[1][:, None] * apow[1][:, t - jnp.arange(t), None, :]

    def outpack(ca):
        w = jnp.concatenate([ca.real, -ca.imag], axis=-1)
        return w.reshape(S5_G, S5_TP, 2 * S5_N).transpose(0, 2, 1)

    wout = jnp.stack([outpack(ca_f), outpack(ca_b)])
    lev = (2.0 ** jnp.arange(nlev, dtype=F32)) * t
    pw = jnp.exp(ldt[:, :, None, :] * lev[None, None, :, None])
    p1 = jnp.concatenate([pw.real, pw.real], axis=-1)
    p2 = jnp.concatenate([-pw.imag, pw.imag], axis=-1)
    pw = jnp.stack([p1, p2], axis=3)
    return taprow, win.astype(BF16), wout.astype(BF16), pw


def _s5_kernel(nc, nlev, u_ref, tap_ref, win_ref, wout_ref, pw_ref, h0_ref, y_ref, hfin_ref, conv_ref):
    u = u_ref[0, 0].astype(BF16)
    row = lax.broadcasted_iota(jnp.int32, (nc, 2 * S5_N), 0)
    lane = lax.broadcasted_iota(jnp.int32, (S5_P, S5_TP), 1)
    for s in range(S5_T):
        lo = s * S5_P
        fwd = tap_ref[0, 0] if s == 0 else jnp.where(lane >= lo, pltpu.roll(tap_ref[0, 0], lo, 1), 0.0)
        hi = lo + S5_P
        bwd = tap_ref[1, 0] if hi == S5_TP else jnp.where(lane < hi, pltpu.roll(tap_ref[1, 0], hi, 1), 0.0)
        conv_ref[0, lo:hi, :] = fwd.astype(BF16)
        conv_ref[1, lo:hi, :] = bwd.astype(BF16)

    def cmul(x, d, j):
        return pw_ref[d, 0, j, 0:1] * x + pw_ref[d, 0, j, 1:2] * pltpu.roll(x, S5_N, 1)

    v = [jnp.dot(u, win_ref[d, 0], preferred_element_type=F32) for d in range(2)]
    x = [jnp.where(row == 0, h0_ref[0, 0, 0], pltpu.roll(v[0], 1, 0)),
         jnp.where(row == nc - 1, h0_ref[0, 1, 0], pltpu.roll(v[1], nc - 1, 0))]
    for j in range(nlev):
        sh = 2 ** j
        xs = [jnp.where(row >= sh, pltpu.roll(x[0], sh, 0), 0.0),
              jnp.where(row < nc - sh, pltpu.roll(x[1], nc - sh, 0), 0.0)]
        x = [x[d] + cmul(xs[d], d, j) for d in range(2)]
    for d, last in enumerate((nc - 1, 0)):
        hfin_ref[0, d, 0] = cmul(x[d][last:last + 1], d, 0) + v[d][last:last + 1]
    y_ref[0, 0] = (jnp.dot(u, conv_ref[0], preferred_element_type=F32) + _mm(x[0], wout_ref[0, 0])
                   + jnp.dot(u, conv_ref[1], preferred_element_type=F32) + _mm(x[1], wout_ref[1, 0]))


def _s5_scan(uf, tables, h0):
    conv, win, wout, pw = tables
    b, _, nc, _ = uf.shape
    nlev = max(1, (nc - 1).bit_length())
    pw = pw[:, :, :nlev]
    n2 = 2 * S5_N
    y, hfin = pl.pallas_call(
        functools.partial(_s5_kernel, nc, nlev),
        grid=(b, S5_G),
        in_specs=[pl.BlockSpec((1, 1, nc, S5_TP), lambda i, g: (i, g, 0, 0)),
                  pl.BlockSpec((2, 1, S5_P, S5_TP), lambda i, g: (0, g, 0, 0)),
                  pl.BlockSpec((2, 1, S5_TP, n2), lambda i, g: (0, g, 0, 0)),
                  pl.BlockSpec((2, 1, n2, S5_TP), lambda i, g: (0, g, 0, 0)),
                  pl.BlockSpec((2, 1, nlev, 2, n2), lambda i, g: (0, g, 0, 0, 0)),
                  pl.BlockSpec((1, 2, 1, 1, n2), lambda i, g: (i, 0, g, 0, 0))],
        out_specs=[pl.BlockSpec((1, 1, nc, S5_TP), lambda i, g: (i, g, 0, 0)),
                   pl.BlockSpec((1, 2, 1, 1, n2), lambda i, g: (i, 0, g, 0, 0))],
        out_shape=[jax.ShapeDtypeStruct((b, S5_G, nc, S5_TP), F32),
                   jax.ShapeDtypeStruct((b, 2, S5_G, 1, n2), F32)],
        scratch_shapes=[pltpu.VMEM((2, S5_TP, S5_TP), BF16)],
        compiler_params=_cparams("parallel", "parallel"),
        name="s5_scan",
    )(uf, conv, win, wout, pw, h0)
    return y, hfin


def _rw_scan_kernel(nb, cpb, rf_ref, vf_ref, kkf_ref, lwf_ref, kdf_ref, asf_ref,
                    rb_ref, vb_ref, kkb_ref, lwb_ref, kdb_ref, asb_ref, s0_ref,
                    yf_ref, yb_ref, sfin_ref, st_ref):
    c = pl.program_id(1)

    @pl.when(c == 0)
    def _():
        st_ref[...] = s0_ref[0]

    t = CHUNK
    ri = lax.broadcasted_iota(jnp.int32, (t, LANES), 0)
    li = lax.broadcasted_iota(jnp.int32, (t, LANES), 1)
    ci = li & (t - 1)
    low = li < HEAD_V
    eyef = jnp.where(ri == ci, 1.0, 0.0)
    same = lambda s: jnp.where((ri >> s) == (ci >> s), 1.0, 0.0)
    m4, m8, m16, m32 = same(2), same(3), same(4), same(5)
    merge_masks = (m8 - m4, m16 - m8, m32 - m16, 1.0 - m32)

    def bd(x):
        return jnp.concatenate([jnp.where(low, x, 0.0), jnp.where(low, 0.0, x)], axis=0)

    def diag_blocks(full):
        return jnp.where(low, full[0:HEAD_V], full[HEAD_V:2 * HEAD_V])

    dir_refs = ((rf_ref, vf_ref, kkf_ref, lwf_ref, kdf_ref, asf_ref),
                (rb_ref, vb_ref, kkb_ref, lwb_ref, kdb_ref, asb_ref))
    units = []
    for d, (r_ref, v_ref, kk_ref, lw_ref, kd_ref, as_ref) in enumerate(dir_refs):
        incl = (ci >= ri) if d == 1 else (ci <= ri)
        strict = (ci > ri) if d == 1 else (ci < ri)
        r, v, kk, kd = (x[0].astype(F32) for x in (r_ref, v_ref, kk_ref, kd_ref))
        lw = lw_ref[0]
        cin = _mm_left01(_chunk_tri01(cpb * t, t, d == 1), lw)
        e_in = jnp.exp(cin)
        e_neg = jnp.exp(-cin)
        rt = r * e_in
        at = -kk * jnp.exp(cin - lw)
        bvec = kk * as_ref[0].astype(F32)
        bt = bvec * e_neg
        kt = kd * e_neg
        for j in range(cpb):
            rows = slice(j * t, (j + 1) * t)
            last = j * t + (0 if d == 1 else t - 1)
            clast = cin[last:last + 1]
            dl = jnp.exp(clast - cin[rows])
            bh = bvec[rows] * dl
            kh = kd[rows] * dl
            dec = jnp.exp(clast)
            for p in range(HEADS // 2):
                sl = slice(p * LANES, (p + 1) * LANES)
                units.append(dict(d=d, j=j, p=p, incl=incl, strict=strict,
                                  at=at[rows, sl], rt=rt[rows, sl], bt=bt[rows, sl], kt=kt[rows, sl],
                                  bh=bh[:, sl], kh=kh[:, sl], v=v[rows, sl], dec=dec[:, sl]))

    x1 = [jnp.concatenate([u['at'], u['rt']], axis=0) for u in units]
    scores = [_mm_nt(x, jnp.concatenate([bd(u['bt']), bd(u['kt'])], axis=0))
              for x, u in zip(x1, units)]
    nmat = [jnp.where(u['strict'], a[0:t, :LANES], 0.0) for u, a in zip(units, scores)]
    a_rb = [jnp.where(u['incl'], a[t:2 * t, :LANES], 0.0) for u, a in zip(units, scores)]
    a_kk = [jnp.concatenate([jnp.where(u['strict'], a[0:t, LANES:], 0.0),
                             jnp.where(u['incl'], a[t:2 * t, LANES:], 0.0)], axis=0)
            for u, a in zip(units, scores)]
    akv = [_mm(a, bd(u['v'])) for u, a in zip(units, a_kk)]
    kv = [diag_blocks(_mm_tn(u['v'], u['kh'])) for u in units]
    nd = [x * m4 for x in nmat]
    n2 = [_mm(x, bd(x)) for x in nd]
    tinv = [eyef + x + _mm(eyef + x, bd(y)) for x, y in zip(nd, n2)]
    for mk in merge_masks:
        w = [_mm(ti, bd(x * mk)) for ti, x in zip(tinv, nmat)]
        tinv = [ti + _mm(wi, bd(ti)) for ti, wi in zip(tinv, w)]
    zz = [_mm(ti, jnp.concatenate([bd(u['at']), bd(kvv[0:t])], axis=1))
          for ti, u, kvv in zip(tinv, units, akv)]
    ght = [_mm_tn(z, u['bh']) for u, z in zip(units, zz)]
    qy = [_mm(a, jnp.concatenate([bd(z[:, :LANES]), bd(z[:, LANES:])], axis=1))
          for a, z in zip(a_rb, zz)]
    qmat = [u['rt'] + x[:, :LANES] for u, x in zip(units, qy)]
    y0 = [x[:, LANES:] + kvv[t:2 * t] for x, kvv in zip(qy, akv)]
    gmat = [diag_blocks(x[:LANES]) for x in ght]
    hmat = [diag_blocks(x[LANES:]) + k2 for x, k2 in zip(ght, kv)]
    idx = {(u['d'], u['j'], u['p']): n for n, u in enumerate(units)}
    npair = HEADS // 2
    state = {(d, p): st_ref[d, p] for d in range(2) for p in range(npair)}
    ys = {}
    for step in range(cpb):
        for d in range(2):
            j = step if d == 0 else cpb - 1 - step
            for p in range(npair):
                n = idx[(d, j, p)]
                st = state[(d, p)]
                ys[(d, j, p)] = _mm_nt(qmat[n], bd(st)) + y0[n]
                state[(d, p)] = units[n]['dec'] * st + _mm(st, bd(gmat[n])) + hmat[n]
    for d, y_ref in enumerate((yf_ref, yb_ref)):
        y_ref[0] = jnp.concatenate(
            [jnp.concatenate([ys[(d, j, p)] for p in range(npair)], axis=1) for j in range(cpb)], axis=0)
        for p in range(npair):
            st_ref[d, p] = state[(d, p)]

    @pl.when(c == nb - 1)
    def _():
        sfin_ref[0] = st_ref[...]


def _rw_scan(pre, s0):
    r, v, kk, _, _, lw0, kd0, as0, lw1, kd1, as1 = pre
    b, l, _ = r.shape
    cpb = min(RW_CPB, l // CHUNK)
    nb = l // (cpb * CHUNK)
    fw = pl.BlockSpec((1, cpb * CHUNK, GROUP_W), lambda i, c: (i, c, 0))
    bw = pl.BlockSpec((1, cpb * CHUNK, GROUP_W), lambda i, c: (i, nb - 1 - c, 0))
    st = pl.BlockSpec((1, 2, HEADS // 2, HEAD_V, LANES), lambda i, c: (i, 0, 0, 0, 0))
    return pl.pallas_call(
        functools.partial(_rw_scan_kernel, nb, cpb),
        grid=(b, nb),
        in_specs=[fw] * 6 + [bw] * 6 + [st],
        out_specs=[fw, bw, st],
        out_shape=[jax.ShapeDtypeStruct((b, l, GROUP_W), F32)] * 2
                  + [jax.ShapeDtypeStruct((b, 2, HEADS // 2, HEAD_V, LANES), F32)],
        scratch_shapes=[pltpu.VMEM((2, HEADS // 2, HEAD_V, LANES), F32)],
        compiler_params=_cparams("parallel", "arbitrary"),
        name="rwkv_scan",
    )(r, v, kk, lw0, kd0, as0, r, v, kk, lw1, kd1, as1, s0)


def _rope(x, cos, sin_signed):
    lane = lax.broadcasted_iota(jnp.int32, x.shape, 1)
    n = x.shape[1]
    swapped = jnp.where((lane & ROPE_NF) == 0, pltpu.roll(x, n - ROPE_NF, 1), pltpu.roll(x, ROPE_NF, 1))
    return x * cos + swapped * sin_signed


def _ret_scan_kernel(nb, cpb, qf_ref, kf_ref, vf_ref, qb_ref, kb_ref, vb_ref,
                     dmat_ref, qdec_ref, kdec_ref, sdec_ref, s0_ref,
                     of_ref, ob_ref, sfin_ref, st_ref):
    c = pl.program_id(1)

    @pl.when(c == 0)
    def _():
        st_ref[...] = s0_ref[0]

    t = dmat_ref.shape[-1]
    npair = HEADS // 2
    low_t = lax.broadcasted_iota(jnp.int32, (t, LANES), 1) < HEAD_V
    low_s = lax.broadcasted_iota(jnp.int32, (HEAD_V, LANES), 1) < HEAD_V

    def bd(x, low):
        return jnp.concatenate([jnp.where(low, x, 0.0), jnp.where(low, 0.0, x)], axis=0)

    units = []
    for d, (q_ref, k_ref, v_ref) in enumerate(((qf_ref, kf_ref, vf_ref), (qb_ref, kb_ref, vb_ref))):
        for j in range(cpb):
            rows = slice(j * t, (j + 1) * t)
            q, k, v = q_ref[0, rows], k_ref[0, rows], v_ref[0, rows]
            qd = q * qdec_ref[d]
            kh = k * kdec_ref[d]
            for p in range(npair):
                sl = slice(p * LANES, (p + 1) * LANES)
                units.append(dict(d=d, j=j, p=p, q=q[:, sl], k=k[:, sl], qd=qd[:, sl], kh=kh[:, sl],
                                  v=v[:, sl], dec=sdec_ref[d][:, sl]))
    stacked = [_mm_nt(jnp.concatenate([jnp.where(low_t, u['q'], 0.0), jnp.where(low_t, 0.0, u['q'])], axis=0),
                      u['k']) for u in units]
    scores = [[s2[h * t:(h + 1) * t].astype(BF16) * dmat_ref[u['d'], 2 * u['p'] + h] for h in range(2)]
              for s2, u in zip(stacked, units)]
    intra = [_mm(jnp.concatenate(s2, axis=1), bd(u['v'], low_t)) for s2, u in zip(scores, units)]
    kv = []
    for u in units:
        full = _mm_tn(u['v'], u['kh'])
        kv.append(jnp.where(low_s, full[0:HEAD_V], full[HEAD_V:2 * HEAD_V]))
    idx = {(u['d'], u['j'], u['p']): i for i, u in enumerate(units)}
    entering = {}
    for d in range(2):
        for p in range(npair):
            st = st_ref[d, p]
            for step in range(cpb):
                j = step if d == 0 else cpb - 1 - step
                i = idx[(d, j, p)]
                entering[i] = st
                st = units[i]['dec'] * st + kv[i]
            st_ref[d, p] = st
    outs = [x + _mm_nt(u['qd'], bd(entering[i], low_s)) for i, (x, u) in enumerate(zip(intra, units))]
    for d, o_ref in enumerate((of_ref, ob_ref)):
        o_ref[0] = jnp.concatenate(
            [jnp.concatenate([outs[idx[(d, j, p)]] for p in range(npair)], axis=1) for j in range(cpb)], axis=0)

    @pl.when(c == nb - 1)
    def _():
        sfin_ref[0] = st_ref[...]


def _ret_tables(decay_logit, n):
    lg = jax.nn.log_sigmoid(decay_logit.astype(F32))
    pos = jnp.arange(n, dtype=F32)
    lag = pos[:, None] - pos[None, :]
    lag = jnp.stack([lag, -lag])
    dmat = jnp.where(lag[:, None] >= 0, jnp.exp(lg[:, :, None, None] * lag[:, None]), 0.0)
    lanes = jnp.repeat(lg, RET_DK, axis=-1)[:, None, :]
    qpow = jnp.stack([pos + 1.0, n - pos])[:, :, None]
    kpow = jnp.stack([n - 1.0 - pos, pos])[:, :, None]
    return dmat.astype(BF16), jnp.exp(lanes * qpow), jnp.exp(lanes * kpow), jnp.exp(lanes * n)


def _gla_scan_kernel(nb, cpb, qf_ref, kf_ref, vf_ref, af_ref, qb_ref, kb_ref, vb_ref, ab_ref,
                     aup_ref, abias_ref, s0_ref, of_ref, ob_ref, sfin_ref, st_ref):
    c = pl.program_id(1)

    @pl.when(c == 0)
    def _():
        st_ref[...] = s0_ref[0]

    t = GLA_T
    n = cpb * t
    dk = GLA_DK
    scale = dk ** -0.5
    dirs = ((qf_ref, kf_ref, vf_ref, af_ref), (qb_ref, kb_ref, vb_ref, ab_ref))
    units = []
    for d, (q_ref, k_ref, v_ref, a_ref) in enumerate(dirs):
        incl = _tri_incl(t, d == 1)
        q, k, v = q_ref[0], k_ref[0] * scale, v_ref[0]
        lw = -_softplus(-(_mm(a_ref[0], aup_ref[d]) + abias_ref[d])) * (1.0 / GLA_TAU)
        cin = _mm_left01(_chunk_tri01(n, t, d == 1), lw)
        qe = q * jnp.exp(cin)
        for j in range(cpb):
            rows = slice(j * t, (j + 1) * t)
            last = j * t + (0 if d == 1 else t - 1)
            mid = j * t + (t // 2 if d == 1 else t // 2 - 1)
            cj = cin[rows]
            clast = cin[last:last + 1]
            cmid = cin[mid:mid + 1]
            qt = q[rows] * jnp.exp(cj - cmid)
            kt = k[rows] * jnp.exp(cmid - cj)
            kh = k[rows] * jnp.exp(clast - cj)
            units.append(dict(d=d, j=j, incl=incl, qt=qt, kt=kt, qe=qe[rows], kh=kh, v=v[rows],
                              dec=jnp.exp(clast)))

    klane = lax.broadcasted_iota(jnp.int32, (t, LANES), 1) // dk
    vlow = lax.broadcasted_iota(jnp.int32, (t, LANES), 1) < HEAD_V
    khead = lax.broadcasted_iota(jnp.int32, (HEAD_V, LANES), 1) // dk
    npair = HEADS // 2

    def v_blockdiag(x):
        return jnp.concatenate([jnp.where(vlow, x, 0.0), jnp.where(vlow, 0.0, x)], axis=0)

    def state_rows(st, p):
        return jnp.concatenate([jnp.where(khead == 2 * p, st, 0.0), jnp.where(khead == 2 * p + 1, st, 0.0)],
                               axis=0)

    stacked = [_mm_nt(jnp.concatenate([jnp.where(klane == h, u['qt'], 0.0) for h in range(HEADS)], axis=0),
                      u['kt']) for u in units]
    amat = [[jnp.where(u['incl'], s4[h * t:(h + 1) * t], 0.0) for h in range(HEADS)]
            for s4, u in zip(stacked, units)]
    intra = [[_mm(jnp.concatenate(a[2 * p:2 * p + 2], axis=1), v_blockdiag(u['v'][:, p * LANES:(p + 1) * LANES]))
              for p in range(npair)] for a, u in zip(amat, units)]
    kv = []
    for u in units:
        full = _mm_tn(u['v'], u['kh'])
        blocks = [full[h * HEAD_V:(h + 1) * HEAD_V] for h in range(HEADS)]
        acc = blocks[HEADS - 1]
        for h in range(HEADS - 2, -1, -1):
            acc = jnp.where(khead == h, blocks[h], acc)
        kv.append(acc)
    idx = {(u['d'], u['j']): i for i, u in enumerate(units)}
    entering = {}
    for d in range(2):
        st = st_ref[d]
        for step in range(cpb):
            j = step if d == 0 else cpb - 1 - step
            i = idx[(d, j)]
            entering[i] = st
            st = units[i]['dec'] * st + kv[i]
        st_ref[d] = st
    inter = [_mm_nt(u['qe'], jnp.concatenate([state_rows(entering[i], p) for p in range(npair)], axis=0))
             for i, u in enumerate(units)]
    outs = [[x[p] + y[:, p * LANES:(p + 1) * LANES] for p in range(npair)] for x, y in zip(intra, inter)]
    for d, o_ref in enumerate((of_ref, ob_ref)):
        o_ref[0] = jnp.concatenate(
            [jnp.concatenate(outs[idx[(d, j)]], axis=1) for j in range(cpb)], axis=0)

    @pl.when(c == nb - 1)
    def _():
        sfin_ref[0] = st_ref[...]


def _rope_tables(rows):
    nf = ROPE_NF
    inv = ROPE_BASE ** (-jnp.arange(nf, dtype=F32) / nf)
    lane = jnp.arange(GROUP_W)
    freq = inv[lane % nf]
    sign = jnp.where((lane & ROPE_NF) == 0, -1.0, 1.0)
    ar = jnp.arange(rows, dtype=F32)[:, None] * freq[None, :]
    ac = jnp.arange(GRID_W, dtype=F32)[:, None] * freq[None, :]
    return (jnp.cos(ar).reshape(rows, 1, GROUP_W), (jnp.sin(ar) * sign).reshape(rows, 1, GROUP_W),
            jnp.cos(ac), jnp.sin(ac) * sign)


def _ret_scan(z, tables, s0):
    b, l, _ = z.shape
    n = min(RET_BLOCK, l)
    nb = l // n
    fw = lambda j: pl.BlockSpec((1, n, GROUP_W), lambda i, c: (i, c, j))
    bw = lambda j: pl.BlockSpec((1, n, GROUP_W), lambda i, c: (i, nb - 1 - c, j))
    full = lambda a: pl.BlockSpec(a.shape, lambda i, c: (0,) * a.ndim)
    st = pl.BlockSpec((1, 2, HEADS // 2, HEAD_V, LANES), lambda i, c: (i, 0, 0, 0, 0))
    return pl.pallas_call(
        functools.partial(_ret_scan_kernel, nb, n // RET_T),
        grid=(b, nb),
        in_specs=[fw(0), fw(1), fw(2), bw(0), bw(1), bw(2)] + [full(a) for a in tables] + [st],
        out_specs=[fw(0), bw(0), st],
        out_shape=[jax.ShapeDtypeStruct((b, l, GROUP_W), F32)] * 2
                  + [jax.ShapeDtypeStruct((b, 2, HEADS // 2, HEAD_V, LANES), F32)],
        scratch_shapes=[pltpu.VMEM((2, HEADS // 2, HEAD_V, LANES), F32)],
        compiler_params=_cparams("parallel", "arbitrary"),
        name="retention_scan",
    )(z, z, z, z, z, z, *tables, s0)


def _gla_scan(z, aup, abias, s0):
    b, l, _ = z.shape
    cpb = min(GLA_BLOCK, l) // CHUNK
    nb = l // (cpb * CHUNK)
    blk = lambda w, j, rev: pl.BlockSpec(
        (1, cpb * CHUNK, w), (lambda i, c: (i, nb - 1 - c, j)) if rev else (lambda i, c: (i, c, j)))
    st = pl.BlockSpec((1, 2, HEAD_V, GLA_QK), lambda i, c: (i, 0, 0, 0))
    ofw = pl.BlockSpec((1, cpb * CHUNK, GROUP_W), lambda i, c: (i, c, 0))
    obw = pl.BlockSpec((1, cpb * CHUNK, GROUP_W), lambda i, c: (i, nb - 1 - c, 0))
    return pl.pallas_call(
        functools.partial(_gla_scan_kernel, nb, cpb * CHUNK // GLA_T),
        grid=(b, nb),
        in_specs=[blk(GLA_QK, 0, False), blk(GLA_QK, 1, False), blk(GROUP_W, 1, False), blk(128, 6, False),
                  blk(GLA_QK, 0, True), blk(GLA_QK, 1, True), blk(GROUP_W, 1, True), blk(128, 6, True),
                  pl.BlockSpec(aup.shape, lambda i, c: (0, 0, 0)),
                  pl.BlockSpec(abias.shape, lambda i, c: (0, 0, 0)), st],
        out_specs=[ofw, obw, st],
        out_shape=[jax.ShapeDtypeStruct((b, l, GROUP_W), F32)] * 2
                  + [jax.ShapeDtypeStruct((b, 2, HEAD_V, GLA_QK), F32)],
        scratch_shapes=[pltpu.VMEM((2, HEAD_V, GLA_QK), F32)],
        compiler_params=_cparams("parallel", "arbitrary"),
        name="gla_scan",
    )(z, z, z, z, z, z, z, z, aup, abias, s0)


def _mix_kernel(x_ref, g1_ref, y5_ref, u5_ref, d5_ref, gw_ref, gb_ref,
                ryf_ref, ryb_ref, rbonus_ref, rg_ref, rlng_ref, rlnb_ref,
                tof_ref, tob_ref, tg_ref, tln_ref,
                gof_ref, gob_ref, gg_ref, gln_ref,
                bones_ref, wo_ref, o_ref, y5t_ref):
    bones = bones_ref[...]
    inv = 1.0 / HEAD_V

    def hmean(a):
        return _mm_right01(a, bones) * inv

    nch = y5_ref.shape[2]
    gph = LANES // S5_P
    y5g = [y5_ref[0, g] for g in range(S5_G)]
    pitch = S5_T + 1
    for s in range(S5_T):
        for hf in range(GROUP_W // LANES):
            y5t_ref[hf, pl.ds(s, nch, stride=pitch), :] = jnp.concatenate(
                [yg[:, s * S5_P:(s + 1) * S5_P] for yg in y5g[hf * gph:(hf + 1) * gph]], axis=1)
    y5t = jnp.concatenate(
        [jnp.concatenate([y5t_ref[hf, ch * pitch:ch * pitch + S5_T] for ch in range(nch)], axis=0)
         for hf in range(GROUP_W // LANES)], axis=1)
    y = y5t + d5_ref[...] * u5_ref[0]
    y = _gelu_tanh(y)
    ya = y * _sigmoid(_mm(y, gw_ref[...]) + gb_ref[...])
    yr = ryf_ref[0] + ryb_ref[0]
    dlt = yr - hmean(yr)
    yn = dlt * lax.rsqrt(hmean(dlt * dlt) + GN_EPS)
    yb = (yn * rlng_ref[...] + rlnb_ref[...] + rbonus_ref[0].astype(F32)) * rg_ref[0].astype(F32)
    ot = tof_ref[0] + tob_ref[0]
    yc = ot * lax.rsqrt(hmean(ot * ot) + EPS) * tln_ref[...] * _silu(tg_ref[0])
    og = gof_ref[0] + gob_ref[0]
    yd = og * lax.rsqrt(hmean(og * og) + EPS) * gln_ref[...] * _silu(gg_ref[0])
    mix = (_mm(ya, wo_ref[0]) + _mm(yb, wo_ref[1]) + _mm(yc, wo_ref[2]) + _mm(yd, wo_ref[3]))
    o_ref[0] = x_ref[0] + g1_ref[0] * mix


def _mix(x, g1, y5, u5, rw_pre, rw_y, zret, ret_o, zgla, gla_o, p, tm):
    b, l, d = x.shape
    tok = lambda w, j: pl.BlockSpec((1, tm, w), lambda i, t: (i, t, j))
    vec = lambda a: pl.BlockSpec(a.shape, lambda i, t: (0,) * a.ndim)
    t256 = tok(GROUP_W, 0)
    args = [x, g1, y5, u5, p['s5_d'], p['glu_w'], p['glu_b'],
            rw_y[0], rw_y[1], rw_pre[4], rw_pre[3], p['rw_ln_g'], p['rw_ln_b'],
            ret_o[0], ret_o[1], zret, p['ret_ln_g'],
            gla_o[0], gla_o[1], zgla, p['gla_ln_g'],
            p['bones'], p['w_out']]
    grouped = pl.BlockSpec((1, S5_G, tm // S5_T, S5_TP), lambda i, t: (i, 0, t, 0))
    specs = [tok(d, 0), pl.BlockSpec((1, 1, d), lambda i, t: (i, 0, 0)), grouped, t256,
             vec(p['s5_d']), vec(p['glu_w']), vec(p['glu_b']),
             t256, t256, t256, t256, vec(p['rw_ln_g']), vec(p['rw_ln_b']),
             t256, t256, tok(GROUP_W, 3), vec(p['ret_ln_g']),
             t256, t256, tok(GROUP_W, 2), vec(p['gla_ln_g']),
             vec(p['bones']), vec(p['w_out'])]
    return pl.pallas_call(
        _mix_kernel,
        grid=(b, l // tm),
        in_specs=specs,
        out_specs=tok(d, 0),
        out_shape=jax.ShapeDtypeStruct((b, l, d), F32),
        scratch_shapes=[pltpu.VMEM((GROUP_W // LANES, tm // S5_T * (S5_T + 1), LANES), F32)],
        compiler_params=_cparams("parallel", "parallel"),
        name="mix_outproj",
    )(*args)


def _mlp_kernel(final, nff, x_ref, g_ref, sc_ref, sh_ref, gate_ref, w1_ref, w2_ref, fg_ref, o_ref):
    x = x_ref[0]
    hb = _modnorm(x, g_ref[...], sc_ref[0], sh_ref[0]).astype(BF16)
    ff = w1_ref.shape[2] // nff
    acc = None
    for j in range(nff):
        a = jnp.maximum(jnp.dot(hb, w1_ref[0, :, j * ff:(j + 1) * ff], preferred_element_type=F32), 0.0)
        part = jnp.dot((a * a).astype(BF16), w2_ref[0, j * ff:(j + 1) * ff, :], preferred_element_type=F32)
        acc = part if acc is None else acc + part
    y = x + gate_ref[0] * acc
    if final:
        ms = jnp.mean(y * y, axis=-1, keepdims=True)
        y = y * lax.rsqrt(ms + EPS) * fg_ref[...]
    o_ref[0] = y


def _mlp(x, g, sc, sh, gate, w1, w2, layer, final_g, final, tm):
    b, l, d = x.shape
    tok = pl.BlockSpec((1, tm, d), lambda i, t: (i, t, 0))
    vec = pl.BlockSpec((1, 1, d), lambda i, t: (i, 0, 0))
    row = pl.BlockSpec((1, d), lambda i, t: (0, 0))
    once = lambda a: pl.BlockSpec((1,) + a.shape[1:], lambda i, t: (layer, 0, 0), pipeline_mode=pl.Buffered(1))
    return pl.pallas_call(
        functools.partial(_mlp_kernel, final, 4),
        grid=(b, l // tm),
        in_specs=[tok, row, vec, vec, vec, once(w1), once(w2), row],
        out_specs=tok,
        out_shape=jax.ShapeDtypeStruct((b, l, d), F32),
        compiler_params=_cparams("parallel", "parallel"),
        name="mlp",
    )(x, g, sc, sh, gate, w1, w2, final_g)


def kernel(x, c, ctx, c_ctx, ada_w, ada_b, norm1_g, norm2_g, w_in, w_out, s5_lam_re, s5_lam_im, s5_log_dt, s5_b_re, s5_b_im, s5_c_re, s5_c_im, s5_d, s5_glu_w, s5_glu_b, rw_mu, rw_w0, rw_w_up, rw_a0, rw_a_up, rw_g_up, rw_k_k, rw_k_a, rw_r_k, rw_ln_g, rw_ln_b, ret_decay_logit, ret_ln_g, gla_a_up, gla_a_b, gla_ln_g, mlp_w1, mlp_w2, final_g):
    b, l, d = x.shape
    lc = ctx.shape[1]
    depth = ada_w.shape[0]
    rows = l // GRID_W
    assert l % TOKEN_TILE == 0 and lc % RET_T == 0 and lc <= TOKEN_TILE and d % LANES == 0
    assert b < COND_ROWS and l % GRID_W == 0

    cond = jnp.zeros((COND_ROWS, d), F32).at[:b].set(c).at[b].set(c_ctx)
    mod = _modulation(cond, ada_w, ada_b)

    lane = jnp.arange(GROUP_W)
    bones = (lane[:, None] // HEAD_V == lane[None, :] // HEAD_V).astype(BF16)
    rope_lat = _rope_tables(rows)
    ident = (jnp.ones((lc // CHUNK, 1, GROUP_W), F32), jnp.zeros((lc // CHUNK, 1, GROUP_W), F32),
             jnp.ones((GRID_W, GROUP_W), F32), jnp.zeros((GRID_W, GROUP_W), F32))
    nlev = max(1, (l // S5_T - 1).bit_length())
    c0, c1, c2 = GROUP_W, GROUP_W + RW_COLS, GROUP_W + RW_COLS + 4 * GROUP_W
    row2 = lambda a: a.reshape(1, -1).astype(F32)
    w_in_b, w_out_b, w1_b, w2_b = _to_bf16(w_in), _to_bf16(w_out), _to_bf16(mlp_w1), _to_bf16(mlp_w2)

    s5_tab_all = jax.vmap(lambda *a: _s5_tables(*a, nlev))(
        s5_lam_re, s5_lam_im, s5_log_dt, s5_b_re, s5_b_im, s5_c_re, s5_c_im)
    ret_tab_all = jax.vmap(lambda a: _ret_tables(a, RET_T))(ret_decay_logit)
    zpad = lambda a, lo, n: jnp.zeros((depth, 2, LANES, n), F32).at[:, :, lo:lo + a.shape[2]].set(a).astype(BF16)
    rw_wup_all = zpad(rw_w_up, 0, GROUP_W)
    rw_aup_all = zpad(rw_a_up, RW_W_RANK, GROUP_W)
    rw_gup_all = jnp.zeros((depth, LANES, GROUP_W), F32).at[:, RW_W_RANK + RW_A_RANK:].set(rw_g_up).astype(BF16)
    gla_aup_all = zpad(gla_a_up, 0, GLA_QK)

    xc = ctx
    for i in range(depth):
        last = i == depth - 1
        m = mod[i].reshape(COND_ROWS, N_ADA, d)
        ml = m[:b, :, None, :]
        mc = jnp.broadcast_to(m[b][None, :, None, :], (b, N_ADA, 1, d))
        wi = w_in_b[i]
        w5, wr, wt = wi[:, :c0], wi[:, c0:c1], wi[:, c1:c2]
        wg = jnp.pad(wi[:, c2:], ((0, 0), (0, GLA_COLS_PAD - (wi.shape[1] - c2))))
        n1 = row2(norm1_g[i])
        n2 = row2(norm2_g[i])

        s5_tab = tuple(a[i] for a in s5_tab_all)
        rwp = dict(
            mu=row2(rw_mu[i]),
            w0=rw_w0[i].reshape(2, 1, GROUP_W), a0=rw_a0[i].reshape(2, 1, GROUP_W),
            wup=rw_wup_all[i], aup=rw_aup_all[i], gup=rw_gup_all[i],
            kk=row2(rw_k_k[i]), ka=row2(rw_k_a[i]), rk=row2(rw_r_k[i]), bones=bones)
        ret_tab = tuple(a[i] for a in ret_tab_all)
        gla_aup = gla_aup_all[i]
        gla_ab = gla_a_b[i].reshape(2, 1, GLA_QK).astype(F32)
        mixp = dict(s5_d=row2(s5_d[i]), glu_w=s5_glu_w[i].astype(BF16), glu_b=row2(s5_glu_b[i]),
                    rw_ln_g=row2(rw_ln_g[i]), rw_ln_b=row2(rw_ln_b[i]), ret_ln_g=row2(ret_ln_g[i]),
                    gla_ln_g=row2(gla_ln_g[i]), bones=bones,
                    w_out=w_out_b[i].reshape(4, GROUP_W, d))

        def mixers(xx, mm, is_lat, states):
            tm = TOKEN_TILE if is_lat else lc
            z5, z5g, zt, zg, pre = _inproj(xx, n1, mm[:, 1], mm[:, 0], w5, wr, wt, wg,
                                           rope_lat if is_lat else ident, rwp, is_lat, tm)
            y5, h5 = _s5_scan(z5g, s5_tab, states[0])
            yrf, yrb, srw = _rw_scan(pre, states[1])
            otf, otb, sret = _ret_scan(zt, ret_tab, states[2])
            ogf, ogb, sgla = _gla_scan(zg, gla_aup, gla_ab, states[3])
            outs = (z5, y5, pre, (yrf, yrb), zt, (otf, otb), zg, (ogf, ogb))
            return outs, (h5, srw, sret, sgla)

        def block(xx, mm, outs, is_lat, fin):
            tm = TOKEN_TILE if is_lat else lc
            z5, y5, pre, yr, zt, ot, zg, og = outs
            x1 = _mix(xx, mm[:, 2], y5, z5, pre, yr, zt, ot, zg, og, mixp, tm)
            return _mlp(x1, n2, mm[:, 4], mm[:, 3], mm[:, 5], w1_b, w2_b, i, row2(final_g), fin,
                        MLP_TILE if is_lat else lc)

        zeros = (jnp.zeros((b, 2, S5_G, 1, 2 * S5_N), F32),
                 jnp.zeros((b, 2, HEADS // 2, HEAD_V, LANES), F32),
                 jnp.zeros((b, 2, HEADS // 2, HEAD_V, LANES), F32),
                 jnp.zeros((b, 2, HEAD_V, GLA_QK), F32))
        outs_c, st_c = mixers(xc, mc, False, zeros)
        outs_l, _ = mixers(x, ml, True, st_c)
        x = block(x, ml, outs_l, True, last)
        if not last:
            xc = block(xc, mc, outs_c, False, False)
    return x
```
